```python
import jax, jax.numpy as jnp
from jax import lax
import numpy as np

D_MODEL = 1024
BATCH = 8
SEQ = 16384
DEPTH = 1

HEAD_DIM = 64
SB_HEADS = 8
DSA_GROUPS = ((128, 1), (512, 4), (2048, 16))
DSA_HEADS_PER_GROUP = 4
DSA_HEADS = DSA_HEADS_PER_GROUP * len(DSA_GROUPS)
MEM_HEADS = 4
MEM_LEN = 256
D_FF = 2816
ROPE_THETA = 10000.0
NORM_EPS = 1e-6
Q_BLOCK = 128
N_BRANCH = 3
SB_W = SB_HEADS * HEAD_DIM
DSA_W = DSA_HEADS * HEAD_DIM
DSA_OUT_W = DSA_HEADS_PER_GROUP * HEAD_DIM
MEM_W = MEM_HEADS * HEAD_DIM
IN_COLS = 3 * SB_W + 3 * DSA_W + MEM_W
MAX_DIL = max(r for _, r in DSA_GROUPS)

kernel_name = "hybrid_stickbreak_dilated_memory_block"

F32 = jnp.float32


def rms_norm(x, g):
    xf = x.astype(F32)
    y = xf * lax.rsqrt(jnp.mean(xf * xf, axis=-1, keepdims=True) + NORM_EPS)
    return (y * g.astype(F32)).astype(x.dtype)


def swiglu(x, w1, w3, w2):
    return (jax.nn.silu(x @ w1) * (x @ w3)) @ w2


def split_heads(t, n):
    b, s, _ = t.shape
    return t.reshape(b, s, n, HEAD_DIM).transpose(0, 2, 1, 3)


def merge_heads(t):
    b, n, s, hd = t.shape
    return t.transpose(0, 2, 1, 3).reshape(b, s, n * hd)


def rope(x, positions):
    half = HEAD_DIM // 2
    inv_freq = jnp.power(ROPE_THETA, -jnp.arange(half, dtype=F32) / half)
    ang = positions.astype(F32)[:, None] * inv_freq[None, :]
    cos, sin = jnp.cos(ang), jnp.sin(ang)
    xf = x.astype(F32)
    x1, x2 = xf[..., :half], xf[..., half:]
    return jnp.concatenate([x1 * cos - x2 * sin, x2 * cos + x1 * sin], axis=-1).astype(x.dtype)


def stick_breaking_attention(q, k, v):
    b, h, s, hd = q.shape
    nb = s // Q_BLOCK
    scale = hd ** -0.5
    qb = q.reshape(b, h, nb, Q_BLOCK, hd).transpose(2, 0, 1, 3, 4)
    key_pos = jnp.arange(s)
    vf = v.astype(F32)

    def block(args):
        qi, bi = args
        z = jnp.einsum('bhqd,bhkd->bhqk', qi, k).astype(F32) * scale
        q_pos = bi * Q_BLOCK + jnp.arange(Q_BLOCK)
        before = key_pos[None, :] < q_pos[:, None]
        log_fail = jnp.where(before, jax.nn.log_sigmoid(-z), 0.0)
        later = lax.cumsum(log_fail, axis=3, reverse=True) - log_fail
        w = jnp.where(before, jnp.exp(jax.nn.log_sigmoid(z) + later), 0.0)
        return jnp.einsum('bhqk,bhkd->bhqd', w, vf)

    out = lax.map(block, (qb, jnp.arange(nb)))
    return out.transpose(1, 2, 0, 3, 4).reshape(b, h, s, hd).astype(q.dtype)


def banded_window_attention(q, k, v, n_back):
    *lead, n, hd = q.shape
    nb = n // Q_BLOCK
    scale = hd ** -0.5
    qb = q.reshape(*lead, nb, Q_BLOCK, hd)

    def with_prev(t):
        tp = jnp.concatenate([jnp.zeros_like(t[..., :Q_BLOCK, :]), t], axis=-2)
        tp = tp.reshape(*lead, nb + 1, Q_BLOCK, hd)
        return jnp.concatenate([tp[..., :-1, :, :], tp[..., 1:, :, :]], axis=-2)

    kb, vb = with_prev(k), with_prev(v)
    sc = jnp.einsum('...qd,...kd->...qk', qb, kb).astype(F32) * scale
    qi = jnp.arange(Q_BLOCK)[:, None]
    kj = jnp.arange(2 * Q_BLOCK)[None, :]
    dist = Q_BLOCK + qi - kj
    blk = jnp.arange(nb)[:, None, None]
    valid = (dist >= 0) & (dist <= n_back) & ((blk > 0) | (kj >= Q_BLOCK))
    sc = jnp.where(valid, sc, -jnp.inf)
    m = jnp.max(sc, axis=-1, keepdims=True)
    p = jnp.exp(sc - m)
    den = jnp.sum(p, axis=-1, keepdims=True)
    out = jnp.einsum('...qk,...kd->...qd', p, vb.astype(F32)) / den
    lse = (m + jnp.log(den))[..., 0]
    return out.reshape(*lead, n, hd), lse.reshape(*lead, n)


def dilated_mixture_attention(q, k, v):
    b, _, s, hd = q.shape
    unit = Q_BLOCK * MAX_DIL
    sp = ((s + unit - 1) // unit) * unit
    pad = ((0, 0), (0, 0), (0, sp - s), (0, 0))
    q, k, v = jnp.pad(q, pad), jnp.pad(k, pad), jnp.pad(v, pad)
    outs, lses = [], []
    for g, (window, dil) in enumerate(DSA_GROUPS):
        sl = slice(g * DSA_HEADS_PER_GROUP, (g + 1) * DSA_HEADS_PER_GROUP)

        def stride_gather(t):
            return t[:, sl].reshape(b, DSA_HEADS_PER_GROUP, sp // dil, dil, hd).swapaxes(2, 3)

        o, l = banded_window_attention(stride_gather(q), stride_gather(k), stride_gather(v), window // dil)
        outs.append(o.swapaxes(2, 3).reshape(b, DSA_HEADS_PER_GROUP, sp, hd))
        lses.append(l.swapaxes(2, 3).reshape(b, DSA_HEADS_PER_GROUP, sp))
    alpha = jax.nn.softmax(jnp.stack(lses, axis=0), axis=0)
    o = jnp.sum(alpha[..., None] * jnp.stack(outs, axis=0), axis=0)
    return o[:, :, :s].astype(q.dtype)


def memory_cross_attention(q, mem_h, w_mem_kv, qn, kn):
    kv = mem_h @ w_mem_kv
    km, vm = jnp.split(kv, 2, axis=-1)
    km = rms_norm(split_heads(km, MEM_HEADS), kn)
    vm = split_heads(vm, MEM_HEADS)
    q = rms_norm(q, qn)
    sc = jnp.einsum('bhqd,bhkd->bhqk', q, km).astype(F32) * (HEAD_DIM ** -0.5)
    p = jax.nn.softmax(sc, axis=-1)
    return jnp.einsum('bhqk,bhkd->bhqd', p, vm.astype(F32)).astype(q.dtype)


def _fwd_setup_inputs(seed: int = 0) -> dict:
    key = jax.random.key(seed)
    ks = iter(jax.random.split(key, 32))

    def w(shape, fan_in):
        return jax.random.normal(next(ks), (DEPTH,) + shape, F32) * (fan_in ** -0.5)

    def gain(shape):
        return 1.0 + 0.02 * jax.random.normal(next(ks), (DEPTH,) + shape, F32)

    return {
        "x": jax.random.normal(next(ks), (BATCH, SEQ, D_MODEL), F32),
        "mem": jax.random.normal(next(ks), (BATCH, MEM_LEN, D_MODEL), F32),
        "ffn1_norm": gain((D_MODEL,)),
        "ffn1_w1": w((D_MODEL, D_FF), D_MODEL),
        "ffn1_w3": w((D_MODEL, D_FF), D_MODEL),
        "ffn1_w2": w((D_FF, D_MODEL), D_FF),
        "mix_norm": gain((D_MODEL,)),
        "mem_norm": gain((D_MODEL,)),
        "w_in": w((D_MODEL, IN_COLS), D_MODEL),
        "w_mem_kv": w((D_MODEL, 2 * MEM_W), D_MODEL),
        "qn_dsa": gain((HEAD_DIM,)),
        "kn_dsa": gain((HEAD_DIM,)),
        "qn_mem": gain((HEAD_DIM,)),
        "kn_mem": gain((HEAD_DIM,)),
        "w_branch_sb": w((SB_W, D_MODEL), SB_W),
        "w_branch_dsa": w((DSA_OUT_W, D_MODEL), DSA_OUT_W),
        "w_branch_mem": w((MEM_W, D_MODEL), MEM_W),
        "w_gate": w((D_MODEL, N_BRANCH * D_MODEL), D_MODEL),
        "b_gate": 0.01 * jax.random.normal(next(ks), (DEPTH, N_BRANCH * D_MODEL), F32),
        "w_out": w((D_MODEL, D_MODEL), D_MODEL),
        "ffn2_norm": gain((D_MODEL,)),
        "ffn2_w1": w((D_MODEL, D_FF), D_MODEL),
        "ffn2_w3": w((D_MODEL, D_FF), D_MODEL),
        "ffn2_w2": w((D_FF, D_MODEL), D_FF),
    }


def _fwd_reference(x, mem, ffn1_norm, ffn1_w1, ffn1_w3, ffn1_w2, mix_norm, mem_norm, w_in, w_mem_kv,
              qn_dsa, kn_dsa, qn_mem, kn_mem, w_branch_sb, w_branch_dsa, w_branch_mem,
              w_gate, b_gate, w_out, ffn2_norm, ffn2_w1, ffn2_w3, ffn2_w2):
    b, s, d = x.shape
    positions = jnp.arange(s)
    cuts = np.cumsum([SB_W, SB_W, SB_W, DSA_W, DSA_W, DSA_W])
    for l in range(DEPTH):
        x = x + 0.5 * swiglu(rms_norm(x, ffn1_norm[l]), ffn1_w1[l], ffn1_w3[l], ffn1_w2[l])

        h = rms_norm(x, mix_norm[l])
        qa, ka, va, qb, kb, vb, qc = jnp.split(h @ w_in[l], cuts, axis=-1)

        ya = stick_breaking_attention(split_heads(qa, SB_HEADS), split_heads(ka, SB_HEADS),
                                      split_heads(va, SB_HEADS))
        ya = merge_heads(ya) @ w_branch_sb[l]

        qb_h = rope(rms_norm(split_heads(qb, DSA_HEADS), qn_dsa[l]), positions)
        kb_h = rope(rms_norm(split_heads(kb, DSA_HEADS), kn_dsa[l]), positions)
        yb = dilated_mixture_attention(qb_h, kb_h, split_heads(vb, DSA_HEADS))
        yb = merge_heads(yb) @ w_branch_dsa[l]

        yc = memory_cross_attention(split_heads(qc, MEM_HEADS), rms_norm(mem, mem_norm[l]),
                                    w_mem_kv[l], qn_mem[l], kn_mem[l])
        yc = merge_heads(yc) @ w_branch_mem[l]

        gates = jax.nn.sigmoid(h @ w_gate[l] + b_gate[l]).reshape(b, s, N_BRANCH, d)
        merged = gates[:, :, 0] * ya + gates[:, :, 1] * yb + gates[:, :, 2] * yc
        x = x + merged @ w_out[l]

        x = x + 0.5 * swiglu(rms_norm(x, ffn2_norm[l]), ffn2_w1[l], ffn2_w3[l], ffn2_w2[l])
    return x


import jax as _jax
import jax.numpy as _jnp

TWIN_FORMAT = 'train_step'
FWD_PARAMS = ['x', 'mem', 'ffn1_norm', 'ffn1_w1', 'ffn1_w3', 'ffn1_w2', 'mix_norm', 'mem_norm', 'w_in', 'w_mem_kv', 'qn_dsa', 'kn_dsa', 'qn_mem', 'kn_mem', 'w_branch_sb', 'w_branch_dsa', 'w_branch_mem', 'w_gate', 'b_gate', 'w_out', 'ffn2_norm', 'ffn2_w1', 'ffn2_w3', 'ffn2_w2']
TWIN_WEIGHTS = ['ffn1_norm', 'ffn1_w1', 'ffn1_w3', 'ffn1_w2', 'mix_norm', 'mem_norm', 'w_in', 'w_mem_kv', 'qn_dsa', 'kn_dsa', 'qn_mem', 'kn_mem', 'w_branch_sb', 'w_branch_dsa', 'w_branch_mem', 'w_gate', 'b_gate', 'w_out', 'ffn2_norm', 'ffn2_w1', 'ffn2_w3', 'ffn2_w2']
TWIN_DIFF_INPUT = 'x'
TWIN_INPUTS = ['x', 'mem', 'ffn1_norm', 'ffn1_w1', 'ffn1_w3', 'ffn1_w2', 'mix_norm', 'mem_norm', 'w_in', 'w_mem_kv', 'qn_dsa', 'kn_dsa', 'qn_mem', 'kn_mem', 'w_branch_sb', 'w_branch_dsa', 'w_branch_mem', 'w_gate', 'b_gate', 'w_out', 'ffn2_norm', 'ffn2_w1', 'ffn2_w3', 'ffn2_w2', 'loss_target', 'm_ffn1_norm', 'm_ffn1_w1', 'm_ffn1_w3', 'm_ffn1_w2', 'm_mix_norm', 'm_mem_norm', 'm_w_in', 'm_w_mem_kv', 'm_qn_dsa', 'm_kn_dsa', 'm_qn_mem', 'm_kn_mem', 'm_w_branch_sb', 'm_w_branch_dsa', 'm_w_branch_mem', 'm_w_gate', 'm_b_gate', 'm_w_out', 'm_ffn2_norm', 'm_ffn2_w1', 'm_ffn2_w3', 'm_ffn2_w2', 'v_ffn1_norm', 'v_ffn1_w1', 'v_ffn1_w3', 'v_ffn1_w2', 'v_mix_norm', 'v_mem_norm', 'v_w_in', 'v_w_mem_kv', 'v_qn_dsa', 'v_kn_dsa', 'v_qn_mem', 'v_kn_mem', 'v_w_branch_sb', 'v_w_branch_dsa', 'v_w_branch_mem', 'v_w_gate', 'v_b_gate', 'v_w_out', 'v_ffn2_norm', 'v_ffn2_w1', 'v_ffn2_w3', 'v_ffn2_w2']
TWIN_OUTPUTS = ['loss', 'grad_x', 'grad_ffn1_norm', 'grad_ffn1_w1', 'grad_ffn1_w3', 'grad_ffn1_w2', 'grad_mix_norm', 'grad_mem_norm', 'grad_w_in', 'grad_w_mem_kv', 'grad_qn_dsa', 'grad_kn_dsa', 'grad_qn_mem', 'grad_kn_mem', 'grad_w_branch_sb', 'grad_w_branch_dsa', 'grad_w_branch_mem', 'grad_w_gate', 'grad_b_gate', 'grad_w_out', 'grad_ffn2_norm', 'grad_ffn2_w1', 'grad_ffn2_w3', 'grad_ffn2_w2', 'delta_ffn1_norm', 'delta_ffn1_w1', 'delta_ffn1_w3', 'delta_ffn1_w2', 'delta_mix_norm', 'delta_mem_norm', 'delta_w_in', 'delta_w_mem_kv', 'delta_qn_dsa', 'delta_kn_dsa', 'delta_qn_mem', 'delta_kn_mem', 'delta_w_branch_sb', 'delta_w_branch_dsa', 'delta_w_branch_mem', 'delta_w_gate', 'delta_b_gate', 'delta_w_out', 'delta_ffn2_norm', 'delta_ffn2_w1', 'delta_ffn2_w3', 'delta_ffn2_w2', 'new_m_ffn1_norm', 'new_m_ffn1_w1', 'new_m_ffn1_w3', 'new_m_ffn1_w2', 'new_m_mix_norm', 'new_m_mem_norm', 'new_m_w_in', 'new_m_w_mem_kv', 'new_m_qn_dsa', 'new_m_kn_dsa', 'new_m_qn_mem', 'new_m_kn_mem', 'new_m_w_branch_sb', 'new_m_w_branch_dsa', 'new_m_w_branch_mem', 'new_m_w_gate', 'new_m_b_gate', 'new_m_w_out', 'new_m_ffn2_norm', 'new_m_ffn2_w1', 'new_m_ffn2_w3', 'new_m_ffn2_w2', 'new_v_ffn1_norm', 'new_v_ffn1_w1', 'new_v_ffn1_w3', 'new_v_ffn1_w2', 'new_v_mix_norm', 'new_v_mem_norm', 'new_v_w_in', 'new_v_w_mem_kv', 'new_v_qn_dsa', 'new_v_kn_dsa', 'new_v_qn_mem', 'new_v_kn_mem', 'new_v_w_branch_sb', 'new_v_w_branch_dsa', 'new_v_w_branch_mem', 'new_v_w_gate', 'new_v_b_gate', 'new_v_w_out', 'new_v_ffn2_norm', 'new_v_ffn2_w1', 'new_v_ffn2_w3', 'new_v_ffn2_w2']
TWIN_LEAF_KINDS = {'loss': 'loss', 'grad_x': 'grad_x', 'grad_ffn1_norm': 'grad_w', 'grad_ffn1_w1': 'grad_w', 'grad_ffn1_w3': 'grad_w', 'grad_ffn1_w2': 'grad_w', 'grad_mix_norm': 'grad_w', 'grad_mem_norm': 'grad_w', 'grad_w_in': 'grad_w', 'grad_w_mem_kv': 'grad_w', 'grad_qn_dsa': 'grad_w', 'grad_kn_dsa': 'grad_w', 'grad_qn_mem': 'grad_w', 'grad_kn_mem': 'grad_w', 'grad_w_branch_sb': 'grad_w', 'grad_w_branch_dsa': 'grad_w', 'grad_w_branch_mem': 'grad_w', 'grad_w_gate': 'grad_w', 'grad_b_gate': 'grad_w', 'grad_w_out': 'grad_w', 'grad_ffn2_norm': 'grad_w', 'grad_ffn2_w1': 'grad_w', 'grad_ffn2_w3': 'grad_w', 'grad_ffn2_w2': 'grad_w', 'delta_ffn1_norm': 'delta_w', 'delta_ffn1_w1': 'delta_w', 'delta_ffn1_w3': 'delta_w', 'delta_ffn1_w2': 'delta_w', 'delta_mix_norm': 'delta_w', 'delta_mem_norm': 'delta_w', 'delta_w_in': 'delta_w', 'delta_w_mem_kv': 'delta_w', 'delta_qn_dsa': 'delta_w', 'delta_kn_dsa': 'delta_w', 'delta_qn_mem': 'delta_w', 'delta_kn_mem': 'delta_w', 'delta_w_branch_sb': 'delta_w', 'delta_w_branch_dsa': 'delta_w', 'delta_w_branch_mem': 'delta_w', 'delta_w_gate': 'delta_w', 'delta_b_gate': 'delta_w', 'delta_w_out': 'delta_w', 'delta_ffn2_norm': 'delta_w', 'delta_ffn2_w1': 'delta_w', 'delta_ffn2_w3': 'delta_w', 'delta_ffn2_w2': 'delta_w', 'new_m_ffn1_norm': 'new_m', 'new_m_ffn1_w1': 'new_m', 'new_m_ffn1_w3': 'new_m', 'new_m_ffn1_w2': 'new_m', 'new_m_mix_norm': 'new_m', 'new_m_mem_norm': 'new_m', 'new_m_w_in': 'new_m', 'new_m_w_mem_kv': 'new_m', 'new_m_qn_dsa': 'new_m', 'new_m_kn_dsa': 'new_m', 'new_m_qn_mem': 'new_m', 'new_m_kn_mem': 'new_m', 'new_m_w_branch_sb': 'new_m', 'new_m_w_branch_dsa': 'new_m', 'new_m_w_branch_mem': 'new_m', 'new_m_w_gate': 'new_m', 'new_m_b_gate': 'new_m', 'new_m_w_out': 'new_m', 'new_m_ffn2_norm': 'new_m', 'new_m_ffn2_w1': 'new_m', 'new_m_ffn2_w3': 'new_m', 'new_m_ffn2_w2': 'new_m', 'new_v_ffn1_norm': 'new_v', 'new_v_ffn1_w1': 'new_v', 'new_v_ffn1_w3': 'new_v', 'new_v_ffn1_w2': 'new_v', 'new_v_mix_norm': 'new_v', 'new_v_mem_norm': 'new_v', 'new_v_w_in': 'new_v', 'new_v_w_mem_kv': 'new_v', 'new_v_qn_dsa': 'new_v', 'new_v_kn_dsa': 'new_v', 'new_v_qn_mem': 'new_v', 'new_v_kn_mem': 'new_v', 'new_v_w_branch_sb': 'new_v', 'new_v_w_branch_dsa': 'new_v', 'new_v_w_branch_mem': 'new_v', 'new_v_w_gate': 'new_v', 'new_v_b_gate': 'new_v', 'new_v_w_out': 'new_v', 'new_v_ffn2_norm': 'new_v', 'new_v_ffn2_w1': 'new_v', 'new_v_ffn2_w3': 'new_v', 'new_v_ffn2_w2': 'new_v'}


def _forward(args):
    return _fwd_reference(*[args[k] for k in FWD_PARAMS])


def _output_shape():
    def fwd():
        inp = _fwd_setup_inputs(0)
        return _fwd_reference(*[inp[k] for k in FWD_PARAMS])
    out = _jax.eval_shape(fwd)
    return out.shape, out.dtype

N_MICROBATCH = 1
ADAM_LR = 0.001
ADAM_B1 = 0.9
ADAM_B2 = 0.999
ADAM_EPS = 1e-08
ADAM_WD = 0.01
ADAM_STEP = 10
PER_EXAMPLE_BATCH_AXIS = {'x': 0, 'mem': 0, 'loss_target': 0}
SHARED_INPUTS = []
_WEIGHT_DTYPES = {'ffn1_norm': _jnp.float32, 'ffn1_w1': _jnp.float32, 'ffn1_w3': _jnp.float32, 'ffn1_w2': _jnp.float32, 'mix_norm': _jnp.float32, 'mem_norm': _jnp.float32, 'w_in': _jnp.float32, 'w_mem_kv': _jnp.float32, 'qn_dsa': _jnp.float32, 'kn_dsa': _jnp.float32, 'qn_mem': _jnp.float32, 'kn_mem': _jnp.float32, 'w_branch_sb': _jnp.float32, 'w_branch_dsa': _jnp.float32, 'w_branch_mem': _jnp.float32, 'w_gate': _jnp.float32, 'b_gate': _jnp.float32, 'w_out': _jnp.float32, 'ffn2_norm': _jnp.float32, 'ffn2_w1': _jnp.float32, 'ffn2_w3': _jnp.float32, 'ffn2_w2': _jnp.float32}
MOMENT_SCALE = {'ffn1_norm': 2.416389e+01, 'ffn1_w1': 1.399564e-01, 'ffn1_w3': 1.749849e-01, 'ffn1_w2': 2.870262e-01, 'mix_norm': 2.130608e+01, 'mem_norm': 3.271624e-01, 'w_in': 2.461417e-01, 'w_mem_kv': 1.411130e-01, 'qn_dsa': 1.908911e+00, 'kn_dsa': 1.906515e+00, 'qn_mem': 5.183471e+00, 'kn_mem': 5.186107e+00, 'w_branch_sb': 6.205468e-01, 'w_branch_dsa': 6.573110e-02, 'w_branch_mem': 6.600359e-02, 'w_gate': 5.146501e-02, 'b_gate': 3.175718e+00, 'w_out': 4.770192e-01, 'ffn2_norm': 2.470990e+01, 'ffn2_w1': 1.251535e-01, 'ffn2_w3': 1.784950e-01, 'ffn2_w2': 2.930143e-01}


def _to_microbatches(a, axis):
    t = _jnp.moveaxis(a, axis, 0)
    t = t.reshape((N_MICROBATCH, t.shape[0] // N_MICROBATCH) + t.shape[1:])
    return _jnp.moveaxis(t, 1, axis + 1)


def setup_inputs(seed: int = 0) -> dict:
    inp = _fwd_setup_inputs(seed)
    key = _jax.random.fold_in(_jax.random.key(seed), 7919)
    shape, _ = _output_shape()
    out = dict(inp)
    out["loss_target"] = _jax.random.normal(_jax.random.fold_in(key, 0), shape, _jnp.float32)
    for i, name in enumerate(TWIN_WEIGHTS):
        w = inp[name].astype(_jnp.float32)
        if MOMENT_SCALE is None:
            s = _jnp.sqrt(_jnp.mean(_jnp.square(w)) + 1e-30)
        else:
            s = MOMENT_SCALE[name]
        km, kv = _jax.random.split(_jax.random.fold_in(key, i + 1))
        out[name] = w
        out["m_" + name] = s * _jax.random.normal(km, w.shape, _jnp.float32)
        out["v_" + name] = (s * s) * _jax.random.uniform(kv, w.shape, _jnp.float32, 0.5, 1.5)
    if N_MICROBATCH > 1:
        for name, axis in PER_EXAMPLE_BATCH_AXIS.items():
            out[name] = _to_microbatches(out[name], axis)
    return {'x': out['x'], 'mem': out['mem'], 'ffn1_norm': out['ffn1_norm'], 'ffn1_w1': out['ffn1_w1'], 'ffn1_w3': out['ffn1_w3'], 'ffn1_w2': out['ffn1_w2'], 'mix_norm': out['mix_norm'], 'mem_norm': out['mem_norm'], 'w_in': out['w_in'], 'w_mem_kv': out['w_mem_kv'], 'qn_dsa': out['qn_dsa'], 'kn_dsa': out['kn_dsa'], 'qn_mem': out['qn_mem'], 'kn_mem': out['kn_mem'], 'w_branch_sb': out['w_branch_sb'], 'w_branch_dsa': out['w_branch_dsa'], 'w_branch_mem': out['w_branch_mem'], 'w_gate': out['w_gate'], 'b_gate': out['b_gate'], 'w_out': out['w_out'], 'ffn2_norm': out['ffn2_norm'], 'ffn2_w1': out['ffn2_w1'], 'ffn2_w3': out['ffn2_w3'], 'ffn2_w2': out['ffn2_w2'], 'loss_target': out['loss_target'], 'm_ffn1_norm': out['m_ffn1_norm'], 'm_ffn1_w1': out['m_ffn1_w1'], 'm_ffn1_w3': out['m_ffn1_w3'], 'm_ffn1_w2': out['m_ffn1_w2'], 'm_mix_norm': out['m_mix_norm'], 'm_mem_norm': out['m_mem_norm'], 'm_w_in': out['m_w_in'], 'm_w_mem_kv': out['m_w_mem_kv'], 'm_qn_dsa': out['m_qn_dsa'], 'm_kn_dsa': out['m_kn_dsa'], 'm_qn_mem': out['m_qn_mem'], 'm_kn_mem': out['m_kn_mem'], 'm_w_branch_sb': out['m_w_branch_sb'], 'm_w_branch_dsa': out['m_w_branch_dsa'], 'm_w_branch_mem': out['m_w_branch_mem'], 'm_w_gate': out['m_w_gate'], 'm_b_gate': out['m_b_gate'], 'm_w_out': out['m_w_out'], 'm_ffn2_norm': out['m_ffn2_norm'], 'm_ffn2_w1': out['m_ffn2_w1'], 'm_ffn2_w3': out['m_ffn2_w3'], 'm_ffn2_w2': out['m_ffn2_w2'], 'v_ffn1_norm': out['v_ffn1_norm'], 'v_ffn1_w1': out['v_ffn1_w1'], 'v_ffn1_w3': out['v_ffn1_w3'], 'v_ffn1_w2': out['v_ffn1_w2'], 'v_mix_norm': out['v_mix_norm'], 'v_mem_norm': out['v_mem_norm'], 'v_w_in': out['v_w_in'], 'v_w_mem_kv': out['v_w_mem_kv'], 'v_qn_dsa': out['v_qn_dsa'], 'v_kn_dsa': out['v_kn_dsa'], 'v_qn_mem': out['v_qn_mem'], 'v_kn_mem': out['v_kn_mem'], 'v_w_branch_sb': out['v_w_branch_sb'], 'v_w_branch_dsa': out['v_w_branch_dsa'], 'v_w_branch_mem': out['v_w_branch_mem'], 'v_w_gate': out['v_w_gate'], 'v_b_gate': out['v_b_gate'], 'v_w_out': out['v_w_out'], 'v_ffn2_norm': out['v_ffn2_norm'], 'v_ffn2_w1': out['v_ffn2_w1'], 'v_ffn2_w3': out['v_ffn2_w3'], 'v_ffn2_w2': out['v_ffn2_w2']}


def _loss(weights, diff, rest, loss_target):
    with _jax.named_scope("forward"):
        args = {**rest, TWIN_DIFF_INPUT: diff, **{k: w.astype(_WEIGHT_DTYPES[k]) for k, w in weights.items()}}
        y = _forward(args)
    with _jax.named_scope("loss_head"):
        err = _jnp.square(y.astype(_jnp.float32) - loss_target)
        return 0.5 * _jnp.sum(_jnp.mean(err, axis=-1)) if err.ndim else 0.5 * err


def _adamw(w, g, m, v):
    m = ADAM_B1 * m + (1.0 - ADAM_B1) * g
    v = ADAM_B2 * v + (1.0 - ADAM_B2) * _jnp.square(g)
    m_hat = m / (1.0 - ADAM_B1 ** ADAM_STEP)
    v_hat = v / (1.0 - ADAM_B2 ** ADAM_STEP)
    delta = -ADAM_LR * (m_hat / (_jnp.sqrt(v_hat) + ADAM_EPS) + ADAM_WD * w)
    return delta, m, v


def reference(x, mem, ffn1_norm, ffn1_w1, ffn1_w3, ffn1_w2, mix_norm, mem_norm, w_in, w_mem_kv, qn_dsa, kn_dsa, qn_mem, kn_mem, w_branch_sb, w_branch_dsa, w_branch_mem, w_gate, b_gate, w_out, ffn2_norm, ffn2_w1, ffn2_w3, ffn2_w2, loss_target, m_ffn1_norm, m_ffn1_w1, m_ffn1_w3, m_ffn1_w2, m_mix_norm, m_mem_norm, m_w_in, m_w_mem_kv, m_qn_dsa, m_kn_dsa, m_qn_mem, m_kn_mem, m_w_branch_sb, m_w_branch_dsa, m_w_branch_mem, m_w_gate, m_b_gate, m_w_out, m_ffn2_norm, m_ffn2_w1, m_ffn2_w3, m_ffn2_w2, v_ffn1_norm, v_ffn1_w1, v_ffn1_w3, v_ffn1_w2, v_mix_norm, v_mem_norm, v_w_in, v_w_mem_kv, v_qn_dsa, v_kn_dsa, v_qn_mem, v_kn_mem, v_w_branch_sb, v_w_branch_dsa, v_w_branch_mem, v_w_gate, v_b_gate, v_w_out, v_ffn2_norm, v_ffn2_w1, v_ffn2_w3, v_ffn2_w2):
    given = dict(x=x, mem=mem, ffn1_norm=ffn1_norm, ffn1_w1=ffn1_w1, ffn1_w3=ffn1_w3, ffn1_w2=ffn1_w2, mix_norm=mix_norm, mem_norm=mem_norm, w_in=w_in, w_mem_kv=w_mem_kv, qn_dsa=qn_dsa, kn_dsa=kn_dsa, qn_mem=qn_mem, kn_mem=kn_mem, w_branch_sb=w_branch_sb, w_branch_dsa=w_branch_dsa, w_branch_mem=w_branch_mem, w_gate=w_gate, b_gate=b_gate, w_out=w_out, ffn2_norm=ffn2_norm, ffn2_w1=ffn2_w1, ffn2_w3=ffn2_w3, ffn2_w2=ffn2_w2, loss_target=loss_target, m_ffn1_norm=m_ffn1_norm, m_ffn1_w1=m_ffn1_w1, m_ffn1_w3=m_ffn1_w3, m_ffn1_w2=m_ffn1_w2, m_mix_norm=m_mix_norm, m_mem_norm=m_mem_norm, m_w_in=m_w_in, m_w_mem_kv=m_w_mem_kv, m_qn_dsa=m_qn_dsa, m_kn_dsa=m_kn_dsa, m_qn_mem=m_qn_mem, m_kn_mem=m_kn_mem, m_w_branch_sb=m_w_branch_sb, m_w_branch_dsa=m_w_branch_dsa, m_w_branch_mem=m_w_branch_mem, m_w_gate=m_w_gate, m_b_gate=m_b_gate, m_w_out=m_w_out, m_ffn2_norm=m_ffn2_norm, m_ffn2_w1=m_ffn2_w1, m_ffn2_w3=m_ffn2_w3, m_ffn2_w2=m_ffn2_w2, v_ffn1_norm=v_ffn1_norm, v_ffn1_w1=v_ffn1_w1, v_ffn1_w3=v_ffn1_w3, v_ffn1_w2=v_ffn1_w2, v_mix_norm=v_mix_norm, v_mem_norm=v_mem_norm, v_w_in=v_w_in, v_w_mem_kv=v_w_mem_kv, v_qn_dsa=v_qn_dsa, v_kn_dsa=v_kn_dsa, v_qn_mem=v_qn_mem, v_kn_mem=v_kn_mem, v_w_branch_sb=v_w_branch_sb, v_w_branch_dsa=v_w_branch_dsa, v_w_branch_mem=v_w_branch_mem, v_w_gate=v_w_gate, v_b_gate=v_b_gate, v_w_out=v_w_out, v_ffn2_norm=v_ffn2_norm, v_ffn2_w1=v_ffn2_w1, v_ffn2_w3=v_ffn2_w3, v_ffn2_w2=v_ffn2_w2)
    weights = {n: given[n] for n in TWIN_WEIGHTS}
    shared = {n: given[n] for n in SHARED_INPUTS}
    per_example = {n: given[n] for n in ['x', 'mem']}
    grad_fn = _jax.value_and_grad(_loss, argnums=(0, 1))

    def one_microbatch(ex, loss_target):
        ex = dict(ex)
        diff = ex.pop(TWIN_DIFF_INPUT)
        return grad_fn(weights, diff, {**shared, **ex}, loss_target)

    if N_MICROBATCH == 1:
        loss, (grad_w, grad_x) = one_microbatch(per_example, given["loss_target"])
    else:
        def body(carry, xs):
            loss_sum, grad_sum = carry
            l_k, (gw_k, gx_k) = one_microbatch(xs[0], xs[1])
            with _jax.named_scope("update"):
                return (loss_sum + l_k, _jax.tree.map(_jnp.add, grad_sum, gw_k)), gx_k

        init = (_jnp.zeros((), _jnp.float32), _jax.tree.map(_jnp.zeros_like, weights))
        (loss, grad_w), grad_x = _jax.lax.scan(body, init, (per_example, given["loss_target"]))
    with _jax.named_scope("update"):
        delta_w, new_m, new_v = {}, {}, {}
        for n in TWIN_WEIGHTS:
            delta_w[n], new_m[n], new_v[n] = _adamw(weights[n], grad_w[n], given["m_" + n], given["v_" + n])
    return (loss, grad_x, *[grad_w[n] for n in TWIN_WEIGHTS], *[delta_w[n] for n in TWIN_WEIGHTS],
            *[new_m[n] for n in TWIN_WEIGHTS], *[new_v[n] for n in TWIN_WEIGHTS])
```

```python
import functools
import math

import jax
import jax.numpy as jnp
from jax import lax
from jax.experimental import pallas as pl
from jax.experimental.pallas import tpu as pltpu

F32 = jnp.float32
BF16 = jnp.bfloat16
MXU_DT = jnp.bfloat16

N_DEV = 8
HEAD_DIM = 64
SB_HEADS = 8
DSA_GROUPS = ((128, 1), (512, 4), (2048, 16))
DSA_HPG = 4
MEM_HEADS = 4
SB_W = SB_HEADS * HEAD_DIM
DSA_W = DSA_HPG * len(DSA_GROUPS) * HEAD_DIM
DSA_OUT_W = DSA_HPG * HEAD_DIM
MEM_W = MEM_HEADS * HEAD_DIM
ROPE_THETA = 10000.0
NORM_EPS = 1e-6
QB = 128
SCALE = HEAD_DIM ** -0.5
ADAM_LR, ADAM_B1, ADAM_B2, ADAM_EPS, ADAM_WD, ADAM_STEP = 0.001, 0.9, 0.999, 1e-08, 0.01, 10

LANES = 128
VMEM_LIMIT = 48 * 1024 * 1024
SB_DEAD = -110.0

WEIGHTS = ['ffn1_norm', 'ffn1_w1', 'ffn1_w3', 'ffn1_w2', 'mix_norm', 'mem_norm', 'w_in', 'w_mem_kv', 'qn_dsa', 'kn_dsa',
           'qn_mem', 'kn_mem', 'w_branch_sb', 'w_branch_dsa', 'w_branch_mem', 'w_gate', 'b_gate', 'w_out', 'ffn2_norm',
           'ffn2_w1', 'ffn2_w3', 'ffn2_w2']
SHARD_AXIS = {'ffn1_norm': None, 'ffn1_w1': 1, 'ffn1_w3': 1, 'ffn1_w2': 0, 'mix_norm': None, 'mem_norm': None, 'w_in': 1,
              'w_mem_kv': 0, 'qn_dsa': None, 'kn_dsa': None, 'qn_mem': None, 'kn_mem': None, 'w_branch_sb': 1,
              'w_branch_dsa': 1, 'w_branch_mem': 1, 'w_gate': 1, 'b_gate': None, 'w_out': 0, 'ffn2_norm': None,
              'ffn2_w1': 1, 'ffn2_w3': 1, 'ffn2_w2': 0}
BIG = [n for n in WEIGHTS if SHARD_AXIS[n] is not None]
SMALL = [n for n in WEIGHTS if SHARD_AXIS[n] is None]


def _pcall(kern, **kw):
    return pl.pallas_call(kern, **kw)


def _params(*sem):
    return pltpu.CompilerParams(dimension_semantics=sem, vmem_limit_bytes=VMEM_LIMIT)


def _dot(a, b, dims):
    return lax.dot_general(a.astype(MXU_DT), b.astype(MXU_DT), (dims, ((), ())), preferred_element_type=F32)


def _nn(a, b):
    return _dot(a, b, ((1,), (0,)))


def _nt(a, b):
    return _dot(a, b, ((1,), (1,)))


def _tn(a, b):
    return _dot(a, b, ((0,), (0,)))


def _pick(n, prefs):
    for p in prefs:
        if n % p == 0:
            return p
    return n


def _matmul(a, b, *, name, ta=False, tb=False, out_dtype=F32, res=None, alpha=1.0, tm=1024, tn=512, tk=1024):
    if ta:
        kdim, m = a.shape
    else:
        m, kdim = a.shape
    n = b.shape[0] if tb else b.shape[1]
    tm = _pick(m, (tm, 512, 256, 128))
    tn = _pick(n, (tn, 512, 384, 256, 128))
    tk = _pick(kdim, (tk, 1024, 512, 256, 128))
    nk = kdim // tk
    a_spec = pl.BlockSpec((tk, tm), lambda i, j, k: (k, i)) if ta else pl.BlockSpec((tm, tk), lambda i, j, k: (i, k))
    b_spec = pl.BlockSpec((tn, tk), lambda i, j, k: (j, k)) if tb else pl.BlockSpec((tk, tn), lambda i, j, k: (k, j))
    o_spec = pl.BlockSpec((tm, tn), lambda i, j, k: (i, j))
    dims = ((0 if ta else 1,), (1 if tb else 0,))

    def kern(*refs):
        if res is None:
            a_ref, b_ref, o_ref, acc_ref = refs
            r_ref = None
        else:
            a_ref, b_ref, r_ref, o_ref, acc_ref = refs
        k = pl.program_id(2)

        @pl.when(k == 0)
        def _():
            acc_ref[...] = jnp.zeros_like(acc_ref)

        acc_ref[...] += _dot(a_ref[...], b_ref[...], dims)

        @pl.when(k == nk - 1)
        def _():
            r = acc_ref[...]
            if alpha != 1.0:
                r = r * alpha
            if r_ref is not None:
                r = r_ref[...] + r
            o_ref[...] = r.astype(out_dtype)

    ins = [a, b] + ([] if res is None else [res])
    specs = [a_spec, b_spec] + ([] if res is None else [o_spec])
    return _pcall(kern, name=name, grid=(m // tm, n // tn, nk), in_specs=specs, out_specs=o_spec,
                  out_shape=jax.ShapeDtypeStruct((m, n), out_dtype), scratch_shapes=[pltpu.VMEM((tm, tn), F32)],
                  compiler_params=_params("parallel", "parallel", "arbitrary"))(*ins)


def _rms_fwd(x, g, *, name):
    s, d = x.shape
    ts = _pick(s, (512, 256))

    def kern(x_ref, g_ref, h_ref):
        xf = x_ref[...]
        r = lax.rsqrt(jnp.mean(xf * xf, axis=-1, keepdims=True) + NORM_EPS)
        h_ref[...] = (xf * r * g_ref[...]).astype(h_ref.dtype)

    return _pcall(kern, name=name, grid=(s // ts,),
                  in_specs=[pl.BlockSpec((ts, d), lambda i: (i, 0)), pl.BlockSpec((1, d), lambda i: (0, 0))],
                  out_specs=pl.BlockSpec((ts, d), lambda i: (i, 0)), out_shape=jax.ShapeDtypeStruct((s, d), BF16),
                  compiler_params=_params("parallel"))(x, g)


def _rms_bwd(x, g, dh, res, *, name):
    s, d = x.shape
    ts = _pick(s, (512, 256))

    def kern(*refs):
        if res is None:
            x_ref, g_ref, dh_ref, dx_ref, dg_ref = refs
            r_ref = None
        else:
            x_ref, g_ref, dh_ref, r_ref, dx_ref, dg_ref = refs
        xf = x_ref[...]
        r = lax.rsqrt(jnp.mean(xf * xf, axis=-1, keepdims=True) + NORM_EPS)
        xh = xf * r
        dhf = dh_ref[...].astype(F32)
        dy = dhf * g_ref[...]
        dx = r * (dy - xh * jnp.mean(dy * xh, axis=-1, keepdims=True))
        if r_ref is not None:
            dx = r_ref[...] + dx
        dx_ref[...] = dx

        @pl.when(pl.program_id(0) == 0)
        def _():
            dg_ref[...] = jnp.zeros_like(dg_ref)

        dg_ref[...] += jnp.sum(dhf * xh, axis=0, keepdims=True)

    row = pl.BlockSpec((ts, d), lambda i: (i, 0))
    vec = pl.BlockSpec((1, d), lambda i: (0, 0))
    ins = [x, g, dh] + ([] if res is None else [res])
    return _pcall(kern, name=name, grid=(s // ts,), in_specs=[row, vec, row] + ([] if res is None else [row]),
                  out_specs=[row, vec], out_shape=[jax.ShapeDtypeStruct((s, d), F32), jax.ShapeDtypeStruct((1, d), F32)],
                  compiler_params=_params("arbitrary"))(*ins)


def _sigmoid(x):
    return 1.0 / (1.0 + jnp.exp(-x))


def _swiglu_fwd(ab, *, name):
    s, f2 = ab.shape
    f = f2 // 2
    ts = _pick(s, (512, 256))
    tf = _pick(f, (1408, 1024, 512, 256, 128))
    nf = f // tf

    def kern(a_ref, b_ref, f_ref):
        a = a_ref[...].astype(F32)
        f_ref[...] = (a * _sigmoid(a) * b_ref[...].astype(F32)).astype(f_ref.dtype)

    return _pcall(kern, name=name, grid=(s // ts, nf),
                  in_specs=[pl.BlockSpec((ts, tf), lambda i, j: (i, j)), pl.BlockSpec((ts, tf), lambda i, j: (i, j + nf))],
                  out_specs=pl.BlockSpec((ts, tf), lambda i, j: (i, j)), out_shape=jax.ShapeDtypeStruct((s, f), BF16),
                  compiler_params=_params("parallel", "parallel"))(ab, ab)


def _swiglu_bwd(ab, df, *, name):
    s, f2 = ab.shape
    f = f2 // 2
    ts = _pick(s, (512, 256))
    tf = _pick(f, (1408, 1024, 512, 256, 128))
    nf = f // tf

    def kern(a_ref, b_ref, df_ref, o_ref):
        j = pl.program_id(1)
        a = a_ref[...].astype(F32)
        d = df_ref[...].astype(F32)
        sg = _sigmoid(a)

        @pl.when(j < nf)
        def _():
            o_ref[...] = (d * b_ref[...].astype(F32) * (sg + a * sg * (1.0 - sg))).astype(o_ref.dtype)

        @pl.when(j >= nf)
        def _():
            o_ref[...] = (d * (a * sg)).astype(o_ref.dtype)

    lo = pl.BlockSpec((ts, tf), lambda i, j: (i, j % nf))
    hi = pl.BlockSpec((ts, tf), lambda i, j: (i, j % nf + nf))
    return _pcall(kern, name=name, grid=(s // ts, 2 * nf), in_specs=[lo, hi, lo],
                  out_specs=pl.BlockSpec((ts, tf), lambda i, j: (i, j)), out_shape=jax.ShapeDtypeStruct((s, f2), BF16),
                  compiler_params=_params("parallel", "parallel"))(ab, ab, df)


def _loss_head(y, t, *, name):
    s, d = y.shape
    ts = _pick(s, (512, 256))
    n = s // ts

    def kern(y_ref, t_ref, dy_ref, l_ref, acc_ref):
        i = pl.program_id(0)

        @pl.when(i == 0)
        def _():
            acc_ref[...] = jnp.zeros_like(acc_ref)

        e = y_ref[...] - t_ref[...]
        dy_ref[...] = e / d
        acc_ref[...] += jnp.sum(e * e, axis=0, keepdims=True)

        @pl.when(i == n - 1)
        def _():
            l_ref[...] = jnp.sum(acc_ref[...], axis=1, keepdims=True) * (0.5 / d)

    row = pl.BlockSpec((ts, d), lambda i: (i, 0))
    return _pcall(kern, name=name, grid=(n,), in_specs=[row, row], out_specs=[row, pl.BlockSpec((1, 1), lambda i: (0, 0))],
                  out_shape=[jax.ShapeDtypeStruct((s, d), F32), jax.ShapeDtypeStruct((1, 1), F32)],
                  scratch_shapes=[pltpu.VMEM((1, d), F32)], compiler_params=_params("arbitrary"))(y, t)


def _head_mean(v, bd):
    hi = v.astype(BF16)
    lo = (v - hi.astype(F32)).astype(BF16)
    return (lax.dot_general(hi, bd, (((1,), (0,)), ((), ())), preferred_element_type=F32)
            + lax.dot_general(lo, bd, (((1,), (0,)), ((), ())), preferred_element_type=F32))


def _partner(v):
    w = v.shape[1]
    lane = lax.broadcasted_iota(jnp.int32, v.shape, 1)
    return jnp.where(lane % HEAD_DIM < HEAD_DIM // 2, pltpu.roll(v, w - HEAD_DIM // 2, 1), pltpu.roll(v, HEAD_DIM // 2, 1))


def _block_diag(w):
    r = lax.broadcasted_iota(jnp.int32, (w, w), 0) // HEAD_DIM
    c = lax.broadcasted_iota(jnp.int32, (w, w), 1) // HEAD_DIM
    return jnp.where(r == c, 1.0 / HEAD_DIM, 0.0).astype(BF16)


def _rope_tables(s):
    half = HEAD_DIM // 2
    inv_freq = jnp.power(ROPE_THETA, -jnp.arange(half, dtype=F32) / half)
    ang = jnp.arange(s).astype(F32)[:, None] * inv_freq[None, :]
    cos, sin = jnp.cos(ang), jnp.sin(ang)
    cos2 = jnp.concatenate([cos, cos, cos, cos], axis=1)
    sin2 = jnp.concatenate([-sin, sin, -sin, sin], axis=1)
    return cos2, sin2


def _qknorm_fwd(src, col0, width, gain, rope, *, name):
    s = src.shape[0]
    ts = _pick(s, (512, 256))
    cb = col0 // width
    assert col0 % width == 0
    reps = width // LANES
    g = jnp.tile(gain, (1, width // HEAD_DIM))

    def kern(*refs):
        if rope is None:
            x_ref, g_ref, o_ref = refs
        else:
            x_ref, g_ref, c_ref, s_ref, o_ref = refs
        x = x_ref[...].astype(F32)
        bd = _block_diag(width)
        r = lax.rsqrt(_head_mean(x * x, bd) + NORM_EPS)
        y = x * r * g_ref[...]
        if rope is not None:
            y = y * jnp.tile(c_ref[...], (1, reps)) + _partner(y) * jnp.tile(s_ref[...], (1, reps))
        o_ref[...] = y.astype(o_ref.dtype)

    xs = pl.BlockSpec((ts, width), lambda i: (i, cb))
    tab = pl.BlockSpec((ts, LANES), lambda i: (i, 0))
    ins = [src, g] + ([] if rope is None else list(rope))
    specs = [xs, pl.BlockSpec((1, width), lambda i: (0, 0))] + ([] if rope is None else [tab, tab])
    return _pcall(kern, name=name, grid=(s // ts,), in_specs=specs, out_specs=pl.BlockSpec((ts, width), lambda i: (i, 0)),
                  out_shape=jax.ShapeDtypeStruct((s, width), BF16), compiler_params=_params("parallel"))(*ins)


def _qknorm_bwd(src, col0, width, gain, rope, dout, *, name):
    s = src.shape[0]
    ts = _pick(s, (512, 256))
    cb = col0 // width
    reps = width // LANES
    g = jnp.tile(gain, (1, width // HEAD_DIM))

    def kern(*refs):
        if rope is None:
            x_ref, g_ref, do_ref, dx_ref, dg_ref = refs
        else:
            x_ref, g_ref, c_ref, s_ref, do_ref, dx_ref, dg_ref = refs
        x = x_ref[...].astype(F32)
        bd = _block_diag(width)
        r = lax.rsqrt(_head_mean(x * x, bd) + NORM_EPS)
        xh = x * r
        dy = do_ref[...].astype(F32)
        if rope is not None:
            dy = dy * jnp.tile(c_ref[...], (1, reps)) + _partner(dy * jnp.tile(s_ref[...], (1, reps)))
        dxh = dy * g_ref[...]
        dx_ref[...] = (r * (dxh - xh * _head_mean(dxh * xh, bd))).astype(dx_ref.dtype)

        @pl.when(pl.program_id(0) == 0)
        def _():
            dg_ref[...] = jnp.zeros_like(dg_ref)

        dg_ref[...] += jnp.sum(dy * xh, axis=0, keepdims=True)

    xs = pl.BlockSpec((ts, width), lambda i: (i, cb))
    row = pl.BlockSpec((ts, width), lambda i: (i, 0))
    vec = pl.BlockSpec((1, width), lambda i: (0, 0))
    tab = pl.BlockSpec((ts, LANES), lambda i: (i, 0))
    ins = [src, g] + ([] if rope is None else list(rope)) + [dout]
    specs = [xs, vec] + ([] if rope is None else [tab, tab]) + [row]
    dx, dg = _pcall(kern, name=name, grid=(s // ts,), in_specs=specs, out_specs=[row, vec],
                    out_shape=[jax.ShapeDtypeStruct((s, width), BF16), jax.ShapeDtypeStruct((1, width), F32)],
                    compiler_params=_params("arbitrary"))(*ins)
    return dx, jnp.sum(dg.reshape(width // HEAD_DIM, HEAD_DIM), axis=0, keepdims=True)


def _tri(strict):
    r = lax.broadcasted_iota(jnp.int32, (2 * QB, QB), 0) % QB
    c = lax.broadcasted_iota(jnp.int32, (2 * QB, QB), 1)
    return jnp.where((r > c) if strict else (r >= c), 1.0, 0.0).astype(BF16)


def _split_dot(v, t2):
    hi = v.astype(BF16)
    lo = (v - hi.astype(F32)).astype(BF16)
    return lax.dot_general(jnp.concatenate([hi, lo], axis=1), t2, (((1,), (0,)), ((), ())), preferred_element_type=F32)


def _log_sigmoids(z):
    t = jnp.log(1.0 + jnp.exp(-jnp.abs(z)))
    return -(jnp.maximum(-z, 0.0) + t), -(jnp.maximum(z, 0.0) + t)


def _key_blocks(t):
    s = t.shape[0]
    n = t.shape[1] // HEAD_DIM
    return t.reshape(s // QB, QB, n, HEAD_DIM).transpose(2, 0, 3, 1)


def _from_key_blocks(t):
    n, nb, hd, qb = t.shape
    return t.transpose(1, 3, 0, 2).reshape(nb * qb, n * hd)


def _sb_fwd(q, kt, vt, *, name):
    h, s, hd = q.shape
    nq = s // QB

    def kern(q_ref, k_ref, v_ref, o_ref):
        i = pl.program_id(1)
        qb = q_ref[...]
        t2 = _tri(True)
        row = lax.broadcasted_iota(jnp.int32, (QB, QB), 0)
        col = lax.broadcasted_iota(jnp.int32, (QB, QB), 1)

        def tile(j, carry, acc, diag):
            z = _nn(qb, k_ref[j]) * SCALE
            ls, lf = _log_sigmoids(z)
            if diag:
                lf = jnp.where(col < row, lf, 0.0)
            w = jnp.exp(ls + _split_dot(lf, t2) + carry)
            if diag:
                w = jnp.where(col < row, w, 0.0)
            return carry + jnp.sum(lf, axis=1, keepdims=True), acc + _nt(w, v_ref[j])

        carry, acc = tile(i, jnp.zeros((QB, 1), F32), jnp.zeros((QB, hd), F32), True)

        def cond(st):
            return jnp.logical_and(st[0] >= 0, st[1] > 0)

        def body(st):
            j, _, carry, acc = st
            carry, acc = tile(j, carry, acc, False)
            return j - 1, (jnp.max(carry) > SB_DEAD).astype(jnp.int32), carry, acc

        _, _, _, acc = lax.while_loop(cond, body, (i - 1, jnp.int32(1), carry, acc))
        o_ref[...] = acc

    blk = pl.BlockSpec((None, QB, hd), lambda a, i: (a, i, 0))
    full = pl.BlockSpec((None, nq, hd, QB), lambda a, i: (a, 0, 0, 0))
    return _pcall(kern, name=name, grid=(h, nq), in_specs=[blk, full, full], out_specs=blk,
                  out_shape=jax.ShapeDtypeStruct((h, s, hd), F32), compiler_params=_params("parallel", "arbitrary"))(q, kt, vt)


def _sb_bwd(q, kt, vt, o, do, *, name):
    h, s, hd = q.shape
    nq = s // QB

    def kern(q_ref, k_ref, v_ref, o_ref, do_ref, dq_ref, dk_ref, dv_ref):
        i = pl.program_id(1)

        @pl.when(i == 0)
        def _():
            dk_ref[...] = jnp.zeros_like(dk_ref)
            dv_ref[...] = jnp.zeros_like(dv_ref)

        qb = q_ref[...]
        dob = do_ref[...]
        dsum = jnp.sum(dob.astype(F32) * o_ref[...], axis=1, keepdims=True)
        t_strict = _tri(True)
        t_incl = _tri(False)
        row = lax.broadcasted_iota(jnp.int32, (QB, QB), 0)
        col = lax.broadcasted_iota(jnp.int32, (QB, QB), 1)

        def tile(j, carry, gcarry, dq, diag):
            kb = k_ref[j]
            z = _nn(qb, kb) * SCALE
            ls, lf = _log_sigmoids(z)
            if diag:
                lf = jnp.where(col < row, lf, 0.0)
            w = jnp.exp(ls + _split_dot(lf, t_strict) + carry)
            if diag:
                w = jnp.where(col < row, w, 0.0)
            wr = w.astype(MXU_DT)
            g = _nn(dob, v_ref[j]) * wr.astype(F32)
            big_g = dsum - (_split_dot(g, t_incl) + gcarry)
            sig = jnp.exp(ls)
            dz = g * (1.0 - sig) - sig * big_g
            if diag:
                dz = jnp.where(col < row, dz, 0.0)
            dz = dz * SCALE
            dk_ref[j] += _tn(qb, dz)
            dv_ref[j] += _tn(dob, wr)
            return (carry + jnp.sum(lf, axis=1, keepdims=True), gcarry + jnp.sum(g, axis=1, keepdims=True),
                    dq + _nt(dz, kb))

        zero = jnp.zeros((QB, 1), F32)
        carry, gcarry, dq = tile(i, zero, zero, jnp.zeros((QB, hd), F32), True)

        def cond(st):
            return jnp.logical_and(st[0] >= 0, st[1] > 0)

        def body(st):
            j, _, carry, gcarry, dq = st
            carry, gcarry, dq = tile(j, carry, gcarry, dq, False)
            return j - 1, (jnp.max(carry) > SB_DEAD).astype(jnp.int32), carry, gcarry, dq

        st = lax.while_loop(cond, body, (i - 1, jnp.int32(1), carry, gcarry, dq))
        dq_ref[...] = st[4]

    blk = pl.BlockSpec((None, QB, hd), lambda a, i: (a, i, 0))
    full = pl.BlockSpec((None, nq, hd, QB), lambda a, i: (a, 0, 0, 0))
    kshape = jax.ShapeDtypeStruct((h, nq, hd, QB), F32)
    return _pcall(kern, name=name, grid=(h, nq), in_specs=[blk, full, full, blk, blk], out_specs=[blk, full, full],
                  out_shape=[jax.ShapeDtypeStruct((h, s, hd), F32), kshape, kshape],
                  compiler_params=_params("parallel", "arbitrary"))(q, kt, vt, o, do)


DSA_SUB = 4


def _dsa_seq_blocks(t, s):
    steps_per_group = 4 * s // (QB * DSA_SUB)
    g = t // steps_per_group
    b0, b1, b2 = (s // (QB * r) for _, r in DSA_GROUPS)
    return jnp.where(g == 0, b0, jnp.where(g == 1, b1, b2))


def _dsa_rel():
    qi = lax.broadcasted_iota(jnp.int32, (QB, QB), 0)
    kj = lax.broadcasted_iota(jnp.int32, (QB, QB), 1)
    return kj - qi


def _prev_mask(rel, has_prev):
    return rel >= jnp.where(has_prev, 0, QB)


def _dsa_fwd(q, k, vp, *, name):
    rows = q.shape[0]
    s = rows // 12
    big = QB * DSA_SUB
    nsteps = rows // big

    def kern(q_ref, k_ref, kp_ref, v_ref, vpv_ref, o_ref):
        t = pl.program_id(0)
        bps = _dsa_seq_blocks(t, s)
        rel = _dsa_rel()
        lane = lax.broadcasted_iota(jnp.int32, (QB, LANES), 1)
        for a in range(DSA_SUB):
            qa = q_ref[pl.ds(a * QB, QB), :]
            kc = k_ref[pl.ds(a * QB, QB), :]
            vc = v_ref[pl.ds(a * QB, QB), :]
            if a == 0:
                kpv, vpv = kp_ref[...], vpv_ref[...]
            else:
                kpv, vpv = k_ref[pl.ds((a - 1) * QB, QB), :], v_ref[pl.ds((a - 1) * QB, QB), :]
            has_prev = (t * DSA_SUB + a) % bps != 0
            sc = jnp.where(rel <= 0, _nt(qa, kc) * SCALE, -jnp.inf)
            sp = jnp.where(_prev_mask(rel, has_prev), _nt(qa, kpv) * SCALE, -jnp.inf)
            m = jnp.maximum(jnp.max(sc, axis=1, keepdims=True), jnp.max(sp, axis=1, keepdims=True))
            pc = jnp.exp(sc - m)
            pp = jnp.exp(sp - m)
            den = jnp.sum(pc, axis=1, keepdims=True) + jnp.sum(pp, axis=1, keepdims=True)
            o = (_nn(pc, vc) + _nn(pp, vpv)) / den
            o_ref[pl.ds(a * QB, QB), :] = jnp.where(lane < HEAD_DIM, o, m + jnp.log(den))

    cur64 = pl.BlockSpec((big, HEAD_DIM), lambda t: (t, 0))
    prev64 = pl.BlockSpec((QB, HEAD_DIM), lambda t: (jnp.maximum(t * DSA_SUB - 1, 0), 0))
    cur128 = pl.BlockSpec((big, LANES), lambda t: (t, 0))
    prev128 = pl.BlockSpec((QB, LANES), lambda t: (jnp.maximum(t * DSA_SUB - 1, 0), 0))
    return _pcall(kern, name=name, grid=(nsteps,), in_specs=[cur64, cur64, prev64, cur128, prev128], out_specs=cur128,
                  out_shape=jax.ShapeDtypeStruct((rows, LANES), F32), compiler_params=_params("parallel"))(q, k, k, vp, vp)


def _dsa_combine(p0, p1, p2, *, name):
    hh, s, _ = p0.shape
    ts = _pick(s, (512, 256))

    def kern(a_ref, b_ref, c_ref, o_ref):
        lane = lax.broadcasted_iota(jnp.int32, (ts, LANES), 1)
        xs = [a_ref[...], b_ref[...], c_ref[...]]
        ls = [jnp.where(lane < HEAD_DIM, pltpu.roll(x, HEAD_DIM, 1), x) for x in xs]
        m = jnp.maximum(jnp.maximum(ls[0], ls[1]), ls[2])
        es = [jnp.exp(l - m) for l in ls]
        den = es[0] + es[1] + es[2]
        o = (es[0] * xs[0] + es[1] * xs[1] + es[2] * xs[2]) / den
        o_ref[...] = jnp.where(lane < HEAD_DIM, o, m + jnp.log(den))

    blk = pl.BlockSpec((None, ts, LANES), lambda a, i: (a, i, 0))
    return _pcall(kern, name=name, grid=(hh, s // ts), in_specs=[blk, blk, blk], out_specs=blk,
                  out_shape=jax.ShapeDtypeStruct((hh, s, LANES), F32), compiler_params=_params("parallel", "parallel"))(p0, p1, p2)


def _dsa_bwd_prep(comb, dop, *, name):
    hh, s, _ = comb.shape
    ts = _pick(s, (512, 256))

    def kern(c_ref, d_ref, o_ref):
        lane = lax.broadcasted_iota(jnp.int32, (ts, LANES), 1)
        c = c_ref[...]
        d = d_ref[...]
        dsum = jnp.sum(jnp.where(lane < HEAD_DIM, c * d, 0.0), axis=1, keepdims=True)
        o_ref[...] = jnp.where(lane < HEAD_DIM, d, jnp.where(lane < HEAD_DIM + 32, c, dsum))

    blk = pl.BlockSpec((None, ts, LANES), lambda a, i: (a, i, 0))
    return _pcall(kern, name=name, grid=(hh, s // ts), in_specs=[blk, blk], out_specs=blk,
                  out_shape=jax.ShapeDtypeStruct((hh, s, LANES), F32), compiler_params=_params("parallel", "parallel"))(comb, dop)


def _dsa_bwd(q, k, vp, pk, *, name):
    rows = q.shape[0]
    s = rows // 12
    big = QB * DSA_SUB
    nsteps = rows // big
    nblk = rows // QB

    def kern(q_ref, qn_ref, k_ref, kp_ref, v_ref, vpv_ref, p_ref, pn_ref, dq_ref, dk_ref, dv_ref):
        t = pl.program_id(0)
        bps = _dsa_seq_blocks(t, s)
        rel = _dsa_rel()
        lane = lax.broadcasted_iota(jnp.int32, (QB, LANES), 1)

        def stats(pa):
            lse = jnp.max(jnp.where(jnp.logical_and(lane >= HEAD_DIM, lane < HEAD_DIM + 32), pa, -jnp.inf), axis=1, keepdims=True)
            dsum = jnp.max(jnp.where(lane >= HEAD_DIM + 32, pa, -jnp.inf), axis=1, keepdims=True)
            return lse, dsum

        def pair(qa, pa, st, kb, vb, mask):
            p = jnp.where(mask, jnp.exp(_nt(qa, kb) * SCALE - st[0]), 0.0)
            ds = p * (_nt(pa, vb) - st[1]) * SCALE
            return _nn(ds, kb), _tn(ds, qa), _tn(p, pa)

        for a in range(DSA_SUB):
            qa = q_ref[pl.ds(a * QB, QB), :]
            pa = p_ref[pl.ds(a * QB, QB), :]
            st = stats(pa)
            kc = k_ref[pl.ds(a * QB, QB), :]
            vc = v_ref[pl.ds(a * QB, QB), :]
            if a == 0:
                kpv, vpv = kp_ref[...], vpv_ref[...]
            else:
                kpv, vpv = k_ref[pl.ds((a - 1) * QB, QB), :], v_ref[pl.ds((a - 1) * QB, QB), :]
            has_prev = (t * DSA_SUB + a) % bps != 0
            dq_c, dk_c, dv_c = pair(qa, pa, st, kc, vc, rel <= 0)
            dq_p, dk_p, dv_p = pair(qa, pa, st, kpv, vpv, _prev_mask(rel, has_prev))
            dq_ref[pl.ds(a * QB, QB), :] = dq_c + dq_p
            if a == 0:
                dk_ref[pl.ds(0, QB), :] = dk_c
                dv_ref[pl.ds(0, QB), :] = dv_c
            else:
                dk_ref[pl.ds(a * QB, QB), :] = dk_c
                dv_ref[pl.ds(a * QB, QB), :] = dv_c
                dk_ref[pl.ds((a - 1) * QB, QB), :] += dk_p
                dv_ref[pl.ds((a - 1) * QB, QB), :] += dv_p
        nxt = t * DSA_SUB + DSA_SUB
        has_next = jnp.logical_and(nxt < nblk, nxt % bps != 0)
        last = (DSA_SUB - 1) * QB
        pn = pn_ref[...]
        _, dk_n, dv_n = pair(qn_ref[...], pn, stats(pn), k_ref[pl.ds(last, QB), :], v_ref[pl.ds(last, QB), :],
                             _prev_mask(rel, has_next))
        dk_ref[pl.ds(last, QB), :] += dk_n
        dv_ref[pl.ds(last, QB), :] += dv_n

    def prev_map(t):
        return (jnp.maximum(t * DSA_SUB - 1, 0), 0)

    def next_map(t):
        return (jnp.minimum(t * DSA_SUB + DSA_SUB, nblk - 1), 0)

    cur64 = pl.BlockSpec((big, HEAD_DIM), lambda t: (t, 0))
    cur128 = pl.BlockSpec((big, LANES), lambda t: (t, 0))
    specs = [cur64, pl.BlockSpec((QB, HEAD_DIM), next_map), cur64, pl.BlockSpec((QB, HEAD_DIM), prev_map),
             cur128, pl.BlockSpec((QB, LANES), prev_map), cur128, pl.BlockSpec((QB, LANES), next_map)]
    return _pcall(kern, name=name, grid=(nsteps,), in_specs=specs, out_specs=[cur64, cur64, cur128],
                  out_shape=[jax.ShapeDtypeStruct((rows, HEAD_DIM), F32), jax.ShapeDtypeStruct((rows, HEAD_DIM), F32),
                             jax.ShapeDtypeStruct((rows, LANES), F32)],
                  compiler_params=_params("parallel"))(q, q, k, k, vp, vp, pk, pk)


def _mem_fwd(q, km, vm, *, name):
    hh, s, hd = q.shape
    ml = km.shape[1]
    tq = _pick(s, (512, 256))

    def kern(q_ref, k_ref, v_ref, o_ref):
        sc = _nt(q_ref[...], k_ref[...]) * SCALE
        e = jnp.exp(sc - jnp.max(sc, axis=1, keepdims=True))
        p = e / jnp.sum(e, axis=1, keepdims=True)
        o_ref[...] = _nn(p, v_ref[...])

    blk = pl.BlockSpec((None, tq, hd), lambda a, i: (a, i, 0))
    kv = pl.BlockSpec((None, ml, hd), lambda a, i: (a, 0, 0))
    return _pcall(kern, name=name, grid=(hh, s // tq), in_specs=[blk, kv, kv], out_specs=blk,
                  out_shape=jax.ShapeDtypeStruct((hh, s, hd), F32), compiler_params=_params("parallel", "parallel"))(q, km, vm)


def _mem_bwd(q, km, vm, do, *, name):
    hh, s, hd = q.shape
    ml = km.shape[1]
    tq = _pick(s, (512, 256))

    def kern(q_ref, k_ref, v_ref, do_ref, dq_ref, dk_ref, dv_ref):
        @pl.when(pl.program_id(1) == 0)
        def _():
            dk_ref[...] = jnp.zeros_like(dk_ref)
            dv_ref[...] = jnp.zeros_like(dv_ref)

        qb = q_ref[...]
        dob = do_ref[...]
        sc = _nt(qb, k_ref[...]) * SCALE
        e = jnp.exp(sc - jnp.max(sc, axis=1, keepdims=True))
        p = e / jnp.sum(e, axis=1, keepdims=True)
        dp = _nt(dob, v_ref[...])
        ds = p * (dp - jnp.sum(p * dp, axis=1, keepdims=True)) * SCALE
        dq_ref[...] = _nn(ds, k_ref[...])
        dk_ref[...] += _tn(ds, qb)
        dv_ref[...] += _tn(p, dob)

    blk = pl.BlockSpec((None, tq, hd), lambda a, i: (a, i, 0))
    kv = pl.BlockSpec((None, ml, hd), lambda a, i: (a, 0, 0))
    kvs = jax.ShapeDtypeStruct((hh, ml, hd), F32)
    return _pcall(kern, name=name, grid=(hh, s // tq), in_specs=[blk, kv, kv, blk], out_specs=[blk, kv, kv],
                  out_shape=[jax.ShapeDtypeStruct((hh, s, hd), F32), kvs, kvs],
                  compiler_params=_params("parallel", "arbitrary"))(q, km, vm, do)


def _merge_fwd(logits, bias, ya, yb, yc, *, name):
    s, d = ya.shape
    ts = _pick(s, (512, 256))

    def kern(l0, l1, l2, b0, b1, b2, a_ref, b_ref, c_ref, o_ref):
        m = (_sigmoid(l0[...] + b0[...]) * a_ref[...] + _sigmoid(l1[...] + b1[...]) * b_ref[...]
             + _sigmoid(l2[...] + b2[...]) * c_ref[...])
        o_ref[...] = m.astype(o_ref.dtype)

    row = pl.BlockSpec((ts, d), lambda i: (i, 0))
    lg = [pl.BlockSpec((ts, d), functools.partial(lambda i, c: (i, c), c=c)) for c in range(3)]
    bs = [pl.BlockSpec((1, d), functools.partial(lambda i, c: (0, c), c=c)) for c in range(3)]
    return _pcall(kern, name=name, grid=(s // ts,), in_specs=lg + bs + [row, row, row], out_specs=row,
                  out_shape=jax.ShapeDtypeStruct((s, d), BF16),
                  compiler_params=_params("parallel"))(logits, logits, logits, bias, bias, bias, ya, yb, yc)


def _merge_bwd(logits, bias, ya, yb, yc, dm, *, name):
    s, d = ya.shape
    ts = _pick(s, (256,))

    def kern(l0, l1, l2, b0, b1, b2, a_ref, b_ref, c_ref, dm_ref, da_ref, db_ref, dc_ref, dl0, dl1, dl2, dbias0, dbias1, dbias2):
        first = pl.program_id(0) == 0
        dmv = dm_ref[...]
        for l_ref, bb_ref, y_ref, dy_ref, dl_ref, dbias_ref in ((l0, b0, a_ref, da_ref, dl0, dbias0), (l1, b1, b_ref, db_ref, dl1, dbias1),
                                                                (l2, b2, c_ref, dc_ref, dl2, dbias2)):
            g = _sigmoid(l_ref[...] + bb_ref[...])
            dy_ref[...] = (dmv * g).astype(dy_ref.dtype)
            dl = dmv * y_ref[...] * g * (1.0 - g)
            dl_ref[...] = dl.astype(dl_ref.dtype)

            @pl.when(first)
            def _():
                dbias_ref[...] = jnp.zeros_like(dbias_ref)

            dbias_ref[...] += jnp.sum(dl, axis=0, keepdims=True)

    row = pl.BlockSpec((ts, d), lambda i: (i, 0))
    lg = [pl.BlockSpec((ts, d), functools.partial(lambda i, c: (i, c), c=c)) for c in range(3)]
    bs = [pl.BlockSpec((1, d), functools.partial(lambda i, c: (0, c), c=c)) for c in range(3)]
    vec = pl.BlockSpec((1, d), lambda i: (0, 0))
    yshape = jax.ShapeDtypeStruct((s, d), BF16)
    vshape = jax.ShapeDtypeStruct((1, d), F32)
    outs = _pcall(kern, name=name, grid=(s // ts,), in_specs=lg + bs + [row, row, row, row],
                  out_specs=[row, row, row, row, row, row, vec, vec, vec],
                  out_shape=[yshape] * 6 + [vshape] * 3,
                  compiler_params=_params("arbitrary"))(logits, logits, logits, bias, bias, bias, ya, yb, yc, dm)
    return outs[0], outs[1], outs[2], outs[3:6], jnp.concatenate(outs[6:9], axis=1)


def _heads(t, n):
    s = t.shape[0]
    return t.reshape(s, n, HEAD_DIM).transpose(1, 0, 2)


def _unheads(t):
    n, s, hd = t.shape
    return t.transpose(1, 0, 2).reshape(s, n * hd)


def _to_class_major(t):
    s = t.shape[0]
    w = t.shape[1] // (DSA_HPG * len(DSA_GROUPS))
    parts = []
    for g, (_, r) in enumerate(DSA_GROUPS):
        tg = t[:, g * DSA_HPG * w:(g + 1) * DSA_HPG * w].reshape(s // r, r, DSA_HPG, w)
        parts.append(tg.transpose(2, 1, 0, 3).reshape(DSA_HPG * s, w))
    return jnp.concatenate(parts, axis=0)


def _slot_to_class_major(t):
    hh, s, w = t.shape
    parts = []
    for _, r in DSA_GROUPS:
        parts.append(t.reshape(hh, s // r, r, w).transpose(0, 2, 1, 3).reshape(hh * s, w))
    return jnp.concatenate(parts, axis=0)


def _from_class_major(t):
    rows, w = t.shape
    s = rows // 12
    out = []
    for g, (_, r) in enumerate(DSA_GROUPS):
        tg = t[g * 4 * s:(g + 1) * 4 * s].reshape(DSA_HPG, r, s // r, w)
        out.append(tg.transpose(0, 2, 1, 3).reshape(DSA_HPG, s, w))
    return out


def _pad_lanes(t):
    return jnp.concatenate([t, jnp.zeros(t.shape[:-1] + (LANES - t.shape[-1],), t.dtype)], axis=-1)


def _ffn_fwd(x, norm, w13, w2, tag):
    h = _rms_fwd(x, norm, name=f"{tag}_rms")
    ab = _matmul(h, w13, name=f"{tag}_up", out_dtype=F32)
    f = _swiglu_fwd(ab, name=f"{tag}_act")
    y = _matmul(f, w2, name=f"{tag}_down", res=x, alpha=0.5, tk=1408)
    return y, (h, ab, f)


def _ffn_bwd(x, norm, w13, w2, saved, dy, tag):
    h, ab, f = saved
    dyb = dy.astype(BF16)
    dw2 = _matmul(f, dyb, name=f"{tag}_dw2", ta=True, alpha=0.5, tm=1408, tn=1024, tk=512)
    df = _matmul(dyb, w2, name=f"{tag}_df", tb=True, alpha=0.5, out_dtype=BF16, tn=1408)
    dab = _swiglu_bwd(ab, df, name=f"{tag}_dact")
    dw13 = _matmul(h, dab, name=f"{tag}_dw13", ta=True, tm=1024, tn=1408, tk=512)
    dh = _matmul(dab, w13, name=f"{tag}_dh", tb=True, tn=1024, tk=1408)
    dx, dnorm = _rms_bwd(x, norm, dh, dy, name=f"{tag}_drms")
    return dx, dnorm, dw13, dw2


def _local_step(x, mem, w, loss_target):
    s, d = x.shape
    assert s % (QB * 16) == 0
    fdim = w['ffn1_w1'].shape[1]
    w13_1 = jnp.concatenate([w['ffn1_w1'], w['ffn1_w3']], axis=1)
    w13_2 = jnp.concatenate([w['ffn2_w1'], w['ffn2_w3']], axis=1)
    rope = _rope_tables(s)

    x1, sv1 = _ffn_fwd(x, w['ffn1_norm'], w13_1, w['ffn1_w2'], "ffn1")
    h = _rms_fwd(x1, w['mix_norm'], name="mix_rms")
    p = _matmul(h, w['w_in'], name="in_proj", out_dtype=BF16)
    logits = _matmul(h, w['w_gate'], name="gate_proj")
    c_qb, c_kb, c_vb, c_qc = 3 * SB_W, 3 * SB_W + DSA_W, 3 * SB_W + 2 * DSA_W, 3 * SB_W + 3 * DSA_W

    qa = _heads(p[:, :SB_W], SB_HEADS)
    ka, va = _key_blocks(p[:, SB_W:2 * SB_W]), _key_blocks(p[:, 2 * SB_W:3 * SB_W])
    oa = _sb_fwd(qa, ka, va, name="sb_fwd")
    oa_t = _unheads(oa)
    ya = _matmul(oa_t, w['w_branch_sb'], name="sb_out")

    qb_n = _qknorm_fwd(p, c_qb, DSA_W, w['qn_dsa'], rope, name="dsa_qnorm")
    kb_n = _qknorm_fwd(p, c_kb, DSA_W, w['kn_dsa'], rope, name="dsa_knorm")
    qb_c, kb_c = _to_class_major(qb_n), _to_class_major(kb_n)
    vb_c = _pad_lanes(_to_class_major(p[:, c_vb:c_vb + DSA_W]))
    ob_groups = _from_class_major(_dsa_fwd(qb_c, kb_c, vb_c, name="dsa_fwd"))
    comb = _dsa_combine(*ob_groups, name="dsa_combine")
    ob_t = _unheads(comb[:, :, :HEAD_DIM])
    yb = _matmul(ob_t, w['w_branch_dsa'], name="dsa_out")

    memh = _rms_fwd(mem, w['mem_norm'], name="mem_rms")
    kv = _matmul(memh, w['w_mem_kv'], name="mem_kv", out_dtype=BF16)
    km_n = _qknorm_fwd(kv, 0, MEM_W, w['kn_mem'], None, name="mem_knorm")
    qc_n = _qknorm_fwd(p, c_qc, MEM_W, w['qn_mem'], None, name="mem_qnorm")
    qc_h, km_h, vm_h = _heads(qc_n, MEM_HEADS), _heads(km_n, MEM_HEADS), _heads(kv[:, MEM_W:], MEM_HEADS)
    oc = _mem_fwd(qc_h, km_h, vm_h, name="mem_fwd")
    oc_t = _unheads(oc)
    yc = _matmul(oc_t, w['w_branch_mem'], name="mem_out")

    merged = _merge_fwd(logits, w['b_gate'], ya, yb, yc, name="merge")
    x2 = _matmul(merged, w['w_out'], name="out_proj", res=x1)
    x3, sv2 = _ffn_fwd(x2, w['ffn2_norm'], w13_2, w['ffn2_w2'], "ffn2")
    dx3, loss = _loss_head(x3, loss_target, name="loss")

    g = {}
    dx2, g['ffn2_norm'], dw13, g['ffn2_w2'] = _ffn_bwd(x2, w['ffn2_norm'], w13_2, w['ffn2_w2'], sv2, dx3, "ffn2")
    g['ffn2_w1'], g['ffn2_w3'] = dw13[:, :fdim], dw13[:, fdim:]

    dx2b = dx2.astype(BF16)
    g['w_out'] = _matmul(merged, dx2b, name="d_w_out", ta=True, tk=512)
    dm = _matmul(dx2b, w['w_out'], name="d_merged", tb=True)
    dya, dyb, dyc, dlog, g['b_gate'] = _merge_bwd(logits, w['b_gate'], ya, yb, yc, dm, name="d_merge")
    dlogits = jnp.concatenate(dlog, axis=1)

    g['w_branch_sb'] = _matmul(oa_t, dya, name="d_w_sb", ta=True, tk=512)
    g['w_branch_dsa'] = _matmul(ob_t, dyb, name="d_w_dsa", ta=True, tk=512)
    g['w_branch_mem'] = _matmul(oc_t, dyc, name="d_w_mem", ta=True, tk=512)
    doa = _matmul(dya, w['w_branch_sb'], name="d_oa", tb=True, out_dtype=BF16)
    dob = _matmul(dyb, w['w_branch_dsa'], name="d_ob", tb=True)
    doc = _matmul(dyc, w['w_branch_mem'], name="d_oc", tb=True, out_dtype=BF16)

    dqa, dka, dva = _sb_bwd(qa, ka, va, oa, _heads(doa, SB_HEADS), name="sb_bwd")

    pk = _dsa_bwd_prep(comb, _pad_lanes(_heads(dob, DSA_HPG)), name="dsa_prep")
    dq_c, dk_c, dv_c = _dsa_bwd(qb_c, kb_c, vb_c, _slot_to_class_major(pk), name="dsa_bwd")
    dqb_n = jnp.concatenate([_unheads(t) for t in _from_class_major(dq_c)], axis=1)
    dkb_n = jnp.concatenate([_unheads(t) for t in _from_class_major(dk_c)], axis=1)
    dvb = jnp.concatenate([_unheads(t[:, :, :HEAD_DIM]) for t in _from_class_major(dv_c)], axis=1).astype(BF16)
    dqb, g['qn_dsa'] = _qknorm_bwd(p, c_qb, DSA_W, w['qn_dsa'], rope, dqb_n, name="d_dsa_qnorm")
    dkb, g['kn_dsa'] = _qknorm_bwd(p, c_kb, DSA_W, w['kn_dsa'], rope, dkb_n, name="d_dsa_knorm")

    dqc_h, dkm_h, dvm_h = _mem_bwd(qc_h, km_h, vm_h, _heads(doc, MEM_HEADS), name="mem_bwd")
    dqc, g['qn_mem'] = _qknorm_bwd(p, c_qc, MEM_W, w['qn_mem'], None, _unheads(dqc_h), name="d_mem_qnorm")
    dkm, g['kn_mem'] = _qknorm_bwd(kv, 0, MEM_W, w['kn_mem'], None, _unheads(dkm_h), name="d_mem_knorm")
    dkv = jnp.concatenate([dkm, _unheads(dvm_h).astype(BF16)], axis=1)
    g['w_mem_kv'] = _matmul(memh, dkv, name="d_w_mem_kv", ta=True)
    dmemh = _matmul(dkv, w['w_mem_kv'], name="d_memh", tb=True)
    _, g['mem_norm'] = _rms_bwd(mem, w['mem_norm'], dmemh, None, name="d_mem_rms")

    dp = jnp.concatenate([_unheads(dqa).astype(BF16), _from_key_blocks(dka).astype(BF16), _from_key_blocks(dva).astype(BF16),
                          dqb, dkb, dvb, dqc], axis=1)
    g['w_in'] = _matmul(h, dp, name="d_w_in", ta=True, tk=512)
    g['w_gate'] = _matmul(h, dlogits, name="d_w_gate", ta=True, tk=512)
    dh = _matmul(dp, w['w_in'], name="d_h_in", tb=True)
    dh = _matmul(dlogits, w['w_gate'], name="d_h_gate", tb=True, res=dh)
    dx1, g['mix_norm'] = _rms_bwd(x1, w['mix_norm'], dh, dx2, name="d_mix_rms")

    dx0, g['ffn1_norm'], dw13, g['ffn1_w2'] = _ffn_bwd(x, w['ffn1_norm'], w13_1, w['ffn1_w2'], sv1, dx1, "ffn1")
    g['ffn1_w1'], g['ffn1_w3'] = dw13[:, :fdim], dw13[:, fdim:]
    return loss, dx0, g


def _pack_rows(d, names):
    return jnp.concatenate([d[n].reshape(-1, LANES) for n in names], axis=0)


def _unpack_rows(t, like, names):
    out, off = {}, 0
    for n in names:
        r = like[n].size // LANES
        out[n] = t[off:off + r].reshape(like[n].shape)
        off += r
    return out


def _unpack_gathered(t, local, names):
    out, off = {}, 0
    for n in names:
        r, c = local[n].shape
        rows = r * c // LANES
        blk = t[:, off:off + rows].reshape(N_DEV, r, c)
        out[n] = blk.reshape(N_DEV * r, c) if SHARD_AXIS[n] == 0 else blk.transpose(1, 0, 2).reshape(r, N_DEV * c)
        off += rows
    return out


def _pack_for_owners(g, local, names):
    parts = []
    for n in names:
        r, c = local[n].shape
        blk = g[n].reshape(N_DEV, r, c) if SHARD_AXIS[n] == 0 else g[n].reshape(r, N_DEV, c).transpose(1, 0, 2)
        parts.append(blk.reshape(N_DEV, r * c // LANES, LANES))
    return jnp.concatenate(parts, axis=1)


def _pack_small(d, names, extra_rows):
    parts = []
    for n in names:
        v = d[n].reshape(-1)
        pad = (-v.size) % LANES
        parts.append(jnp.concatenate([v, jnp.zeros((pad,), v.dtype)]).reshape(-1, LANES))
    t = jnp.concatenate(parts, axis=0)
    return jnp.concatenate([t, jnp.zeros((extra_rows, LANES), t.dtype)], axis=0)


def _unpack_small(t, like, names):
    out, off = {}, 0
    for n in names:
        size = like[n].size
        rows = -(-size // LANES)
        out[n] = t[off:off + rows].reshape(-1)[:size].reshape(like[n].shape)
        off += rows
    return out


def _exchange(src, per_peer, *, name):
    rows = src.shape[-2]

    def body(src_ref, out_ref, send_sems, recv_sems, local_sem):
        x, y, c = lax.axis_index("x"), lax.axis_index("y"), lax.axis_index("c")
        me = 4 * x + 2 * y + c
        mine = pltpu.make_async_copy(src_ref.at[me] if per_peer else src_ref, out_ref.at[me], local_sem)
        mine.start()
        copies = []
        for k in range(1, N_DEV):
            px = 1 - x if k & 4 else x
            py = 1 - y if k & 2 else y
            pc = 1 - c if k & 1 else c
            cp = pltpu.make_async_remote_copy(
                src_ref=src_ref.at[4 * px + 2 * py + pc] if per_peer else src_ref, dst_ref=out_ref.at[me],
                send_sem=send_sems.at[k - 1], recv_sem=recv_sems.at[k - 1],
                device_id=(px, py, pc), device_id_type=pl.DeviceIdType.MESH)
            cp.start()
            copies.append(cp)
        for cp in copies:
            cp.wait_recv()
        for cp in copies:
            cp.wait_send()
        mine.wait()

    anyspace = pl.BlockSpec(memory_space=pl.ANY)
    return _pcall(body, name=name, in_specs=[anyspace], out_specs=anyspace,
                  out_shape=jax.ShapeDtypeStruct((N_DEV, rows, LANES), src.dtype),
                  scratch_shapes=[pltpu.SemaphoreType.DMA((N_DEV - 1,)), pltpu.SemaphoreType.DMA((N_DEV - 1,)),
                                  pltpu.SemaphoreType.DMA])(src)


def _adamw(recv, w, m, v, *, name):
    rows = w.shape[0]
    tr = _pick(rows, (512, 256, 128, 64))

    def kern(r_ref, w_ref, m_ref, v_ref, g_ref, d_ref, mo_ref, vo_ref):
        g = r_ref[0]
        for p in range(1, N_DEV):
            g = g + r_ref[p]
        mn = ADAM_B1 * m_ref[...] + (1.0 - ADAM_B1) * g
        vn = ADAM_B2 * v_ref[...] + (1.0 - ADAM_B2) * (g * g)
        m_hat = mn / (1.0 - ADAM_B1 ** ADAM_STEP)
        v_hat = vn / (1.0 - ADAM_B2 ** ADAM_STEP)
        g_ref[...] = g
        d_ref[...] = -ADAM_LR * (m_hat / (jnp.sqrt(v_hat) + ADAM_EPS) + ADAM_WD * w_ref[...])
        mo_ref[...] = mn
        vo_ref[...] = vn

    row = pl.BlockSpec((tr, LANES), lambda i: (i, 0))
    shp = jax.ShapeDtypeStruct((rows, LANES), F32)
    return _pcall(kern, name=name, grid=(rows // tr,), in_specs=[pl.BlockSpec((N_DEV, tr, LANES), lambda i: (0, i, 0)), row, row, row],
                  out_specs=[row, row, row, row], out_shape=[shp, shp, shp, shp], compiler_params=_params("parallel"))(recv, w, m, v)


INPUTS = ['x', 'mem'] + WEIGHTS + ['loss_target'] + ['m_' + n for n in WEIGHTS] + ['v_' + n for n in WEIGHTS]
SMALL_PAD_ROWS = 4


def kernel(x, mem, ffn1_norm, ffn1_w1, ffn1_w3, ffn1_w2, mix_norm, mem_norm, w_in, w_mem_kv, qn_dsa, kn_dsa, qn_mem, kn_mem, w_branch_sb, w_branch_dsa, w_branch_mem, w_gate, b_gate, w_out, ffn2_norm, ffn2_w1, ffn2_w3, ffn2_w2, loss_target, m_ffn1_norm, m_ffn1_w1, m_ffn1_w3, m_ffn1_w2, m_mix_norm, m_mem_norm, m_w_in, m_w_mem_kv, m_qn_dsa, m_kn_dsa, m_qn_mem, m_kn_mem, m_w_branch_sb, m_w_branch_dsa, m_w_branch_mem, m_w_gate, m_b_gate, m_w_out, m_ffn2_norm, m_ffn2_w1, m_ffn2_w3, m_ffn2_w2, v_ffn1_norm, v_ffn1_w1, v_ffn1_w3, v_ffn1_w2, v_mix_norm, v_mem_norm, v_w_in, v_w_mem_kv, v_qn_dsa, v_kn_dsa, v_qn_mem, v_kn_mem, v_w_branch_sb, v_w_branch_dsa, v_w_branch_mem, v_w_gate, v_b_gate, v_w_out, v_ffn2_norm, v_ffn2_w1, v_ffn2_w3, v_ffn2_w2):
    given = dict(zip(INPUTS, (x, mem, ffn1_norm, ffn1_w1, ffn1_w3, ffn1_w2, mix_norm, mem_norm, w_in, w_mem_kv, qn_dsa, kn_dsa, qn_mem, kn_mem, w_branch_sb, w_branch_dsa, w_branch_mem, w_gate, b_gate, w_out, ffn2_norm, ffn2_w1, ffn2_w3, ffn2_w2, loss_target, m_ffn1_norm, m_ffn1_w1, m_ffn1_w3, m_ffn1_w2, m_mix_norm, m_mem_norm, m_w_in, m_w_mem_kv, m_qn_dsa, m_kn_dsa, m_qn_mem, m_kn_mem, m_w_branch_sb, m_w_branch_dsa, m_w_branch_mem, m_w_gate, m_b_gate, m_w_out, m_ffn2_norm, m_ffn2_w1, m_ffn2_w3, m_ffn2_w2, v_ffn1_norm, v_ffn1_w1, v_ffn1_w3, v_ffn1_w2, v_mix_norm, v_mem_norm, v_w_in, v_w_mem_kv, v_qn_dsa, v_kn_dsa, v_qn_mem, v_kn_mem, v_w_branch_sb, v_w_branch_dsa, v_w_branch_mem, v_w_gate, v_b_gate, v_w_out, v_ffn2_norm, v_ffn2_w1, v_ffn2_w3, v_ffn2_w2), strict=True))
    wl = {n: given[n][0] for n in BIG}
    ws = {n: given[n] for n in SMALL}

    gathered = _exchange(_pack_rows({n: wl[n].astype(BF16) for n in BIG}, BIG), False, name="gather_weights")
    whole = _unpack_gathered(gathered, wl, BIG)
    loss, dx, g = _local_step(x[0], mem[0], {**whole, **ws}, loss_target[0])

    recv = _exchange(_pack_for_owners(g, wl, BIG), True, name="scatter_grads")
    big = _adamw(recv, _pack_rows(wl, BIG), _pack_rows({n: given['m_' + n][0] for n in BIG}, BIG),
                 _pack_rows({n: given['v_' + n][0] for n in BIG}, BIG), name="adamw_sharded")
    big = [_unpack_rows(t, wl, BIG) for t in big]

    gs = _pack_small(g, SMALL, SMALL_PAD_ROWS)
    loss_row = gs.shape[0] - SMALL_PAD_ROWS
    gs = gs.at[loss_row, 0].set(loss[0, 0])
    recv_s = _exchange(gs, False, name="gather_small")
    small = _adamw(recv_s, _pack_small(ws, SMALL, SMALL_PAD_ROWS), _pack_small({n: given['m_' + n] for n in SMALL}, SMALL, SMALL_PAD_ROWS),
                   _pack_small({n: given['v_' + n] for n in SMALL}, SMALL, SMALL_PAD_ROWS), name="adamw_replicated")
    total_loss = small[0][loss_row, 0]
    small = [_unpack_small(t, ws, SMALL) for t in small]

    outs = [total_loss, dx[None]]
    for kind in range(4):
        outs += [big[kind][n][None] if n in wl else small[kind][n] for n in WEIGHTS]
    return tuple(outs)
```

```python
import functools
import math

import jax
import jax.numpy as jnp
from jax import lax
from jax.experimental import pallas as pl
from jax.experimental.pallas import tpu as pltpu

F32 = jnp.float32
BF16 = jnp.bfloat16
MXU_DT = jnp.bfloat16

N_DEV = 8
HEAD_DIM = 64
SB_HEADS = 8
DSA_GROUPS = ((128, 1), (512, 4), (2048, 16))
DSA_HPG = 4
MEM_HEADS = 4
SB_W = SB_HEADS * HEAD_DIM
DSA_W = DSA_HPG * len(DSA_GROUPS) * HEAD_DIM
DSA_OUT_W = DSA_HPG * HEAD_DIM
MEM_W = MEM_HEADS * HEAD_DIM
ROPE_THETA = 10000.0
NORM_EPS = 1e-6
QB = 128
SCALE = HEAD_DIM ** -0.5
ADAM_LR, ADAM_B1, ADAM_B2, ADAM_EPS, ADAM_WD, ADAM_STEP = 0.001, 0.9, 0.999, 1e-08, 0.01, 10

LANES = 128
VMEM_LIMIT = 48 * 1024 * 1024
SB_DEAD = -110.0 * 1.4426950408889634

WEIGHTS = ['ffn1_norm', 'ffn1_w1', 'ffn1_w3', 'ffn1_w2', 'mix_norm', 'mem_norm', 'w_in', 'w_mem_kv', 'qn_dsa', 'kn_dsa',
           'qn_mem', 'kn_mem', 'w_branch_sb', 'w_branch_dsa', 'w_branch_mem', 'w_gate', 'b_gate', 'w_out', 'ffn2_norm',
           'ffn2_w1', 'ffn2_w3', 'ffn2_w2']
SHARD_AXIS = {'ffn1_norm': None, 'ffn1_w1': 1, 'ffn1_w3': 1, 'ffn1_w2': 0, 'mix_norm': None, 'mem_norm': None, 'w_in': 1,
              'w_mem_kv': 0, 'qn_dsa': None, 'kn_dsa': None, 'qn_mem': None, 'kn_mem': None, 'w_branch_sb': 1,
              'w_branch_dsa': 1, 'w_branch_mem': 1, 'w_gate': 1, 'b_gate': None, 'w_out': 0, 'ffn2_norm': None,
              'ffn2_w1': 1, 'ffn2_w3': 1, 'ffn2_w2': 0}
BIG = [n for n in WEIGHTS if SHARD_AXIS[n] is not None]
SMALL = [n for n in WEIGHTS if SHARD_AXIS[n] is None]


def _pcall(kern, **kw):
    return pl.pallas_call(kern, **kw)


def _params(*sem):
    return pltpu.CompilerParams(dimension_semantics=sem, vmem_limit_bytes=VMEM_LIMIT)


def _dot(a, b, dims):
    return lax.dot_general(a.astype(MXU_DT), b.astype(MXU_DT), (dims, ((), ())), preferred_element_type=F32)


def _nn(a, b):
    return _dot(a, b, ((1,), (0,)))


def _nt(a, b):
    return _dot(a, b, ((1,), (1,)))


def _tn(a, b):
    return _dot(a, b, ((0,), (0,)))


def _pick(n, prefs):
    for p in prefs:
        if n % p == 0:
            return p
    return n


def _matmul(a, b, *, name, ta=False, tb=False, out_dtype=F32, res=None, alpha=1.0, tm=1024, tn=512, tk=1024):
    if ta:
        kdim, m = a.shape
    else:
        m, kdim = a.shape
    n = b.shape[0] if tb else b.shape[1]
    tm = _pick(m, (tm, 512, 256, 128))
    tn = _pick(n, (tn, 512, 384, 256, 128))
    tk = _pick(kdim, (tk, 1024, 512, 256, 128))
    nk = kdim // tk
    a_spec = pl.BlockSpec((tk, tm), lambda i, j, k: (k, i)) if ta else pl.BlockSpec((tm, tk), lambda i, j, k: (i, k))
    b_spec = pl.BlockSpec((tn, tk), lambda i, j, k: (j, k)) if tb else pl.BlockSpec((tk, tn), lambda i, j, k: (k, j))
    o_spec = pl.BlockSpec((tm, tn), lambda i, j, k: (i, j))
    dims = ((0 if ta else 1,), (1 if tb else 0,))

    def kern(*refs):
        if res is None:
            a_ref, b_ref, o_ref, acc_ref = refs
            r_ref = None
        else:
            a_ref, b_ref, r_ref, o_ref, acc_ref = refs
        k = pl.program_id(2)

        @pl.when(k == 0)
        def _():
            acc_ref[...] = jnp.zeros_like(acc_ref)

        acc_ref[...] += _dot(a_ref[...], b_ref[...], dims)

        @pl.when(k == nk - 1)
        def _():
            r = acc_ref[...]
            if alpha != 1.0:
                r = r * alpha
            if r_ref is not None:
                r = r_ref[...] + r
            o_ref[...] = r.astype(out_dtype)

    ins = [a, b] + ([] if res is None else [res])
    specs = [a_spec, b_spec] + ([] if res is None else [o_spec])
    return _pcall(kern, name=name, grid=(m // tm, n // tn, nk), in_specs=specs, out_specs=o_spec,
                  out_shape=jax.ShapeDtypeStruct((m, n), out_dtype), scratch_shapes=[pltpu.VMEM((tm, tn), F32)],
                  compiler_params=_params("parallel", "parallel", "arbitrary"))(*ins)


def _rms_fwd(x, g, *, name):
    s, d = x.shape
    ts = _pick(s, (512, 256))

    def kern(x_ref, g_ref, h_ref):
        xf = x_ref[...]
        r = lax.rsqrt(jnp.mean(xf * xf, axis=-1, keepdims=True) + NORM_EPS)
        h_ref[...] = (xf * r * g_ref[...]).astype(h_ref.dtype)

    return _pcall(kern, name=name, grid=(s // ts,),
                  in_specs=[pl.BlockSpec((ts, d), lambda i: (i, 0)), pl.BlockSpec((1, d), lambda i: (0, 0))],
                  out_specs=pl.BlockSpec((ts, d), lambda i: (i, 0)), out_shape=jax.ShapeDtypeStruct((s, d), BF16),
                  compiler_params=_params("parallel"))(x, g)


def _rms_bwd(x, g, dh, res, *, name):
    s, d = x.shape
    ts = _pick(s, (512, 256))

    def kern(*refs):
        if res is None:
            x_ref, g_ref, dh_ref, dx_ref, dg_ref = refs
            r_ref = None
        else:
            x_ref, g_ref, dh_ref, r_ref, dx_ref, dg_ref = refs
        xf = x_ref[...]
        r = lax.rsqrt(jnp.mean(xf * xf, axis=-1, keepdims=True) + NORM_EPS)
        xh = xf * r
        dhf = dh_ref[...].astype(F32)
        dy = dhf * g_ref[...]
        dx = r * (dy - xh * jnp.mean(dy * xh, axis=-1, keepdims=True))
        if r_ref is not None:
            dx = r_ref[...] + dx
        dx_ref[...] = dx

        @pl.when(pl.program_id(0) == 0)
        def _():
            dg_ref[...] = jnp.zeros_like(dg_ref)

        dg_ref[...] += jnp.sum(dhf * xh, axis=0, keepdims=True)

    row = pl.BlockSpec((ts, d), lambda i: (i, 0))
    vec = pl.BlockSpec((1, d), lambda i: (0, 0))
    ins = [x, g, dh] + ([] if res is None else [res])
    return _pcall(kern, name=name, grid=(s // ts,), in_specs=[row, vec, row] + ([] if res is None else [row]),
                  out_specs=[row, vec], out_shape=[jax.ShapeDtypeStruct((s, d), F32), jax.ShapeDtypeStruct((1, d), F32)],
                  compiler_params=_params("arbitrary"))(*ins)


def _sigmoid(x):
    return 1.0 / (1.0 + jnp.exp(-x))


def _swiglu_fwd(ab, *, name):
    s, f2 = ab.shape
    f = f2 // 2
    ts = _pick(s, (512, 256))
    tf = _pick(f, (1408, 1024, 512, 256, 128))
    nf = f // tf

    def kern(a_ref, b_ref, f_ref):
        a = a_ref[...].astype(F32)
        f_ref[...] = (a * _sigmoid(a) * b_ref[...].astype(F32)).astype(f_ref.dtype)

    return _pcall(kern, name=name, grid=(s // ts, nf),
                  in_specs=[pl.BlockSpec((ts, tf), lambda i, j: (i, j)), pl.BlockSpec((ts, tf), lambda i, j: (i, j + nf))],
                  out_specs=pl.BlockSpec((ts, tf), lambda i, j: (i, j)), out_shape=jax.ShapeDtypeStruct((s, f), BF16),
                  compiler_params=_params("parallel", "parallel"))(ab, ab)


def _swiglu_bwd(ab, df, *, name):
    s, f2 = ab.shape
    f = f2 // 2
    ts = _pick(s, (512, 256))
    tf = _pick(f, (1408, 1024, 512, 256, 128))
    nf = f // tf

    def kern(a_ref, b_ref, df_ref, o_ref):
        j = pl.program_id(1)
        a = a_ref[...].astype(F32)
        d = df_ref[...].astype(F32)
        sg = _sigmoid(a)

        @pl.when(j < nf)
        def _():
            o_ref[...] = (d * b_ref[...].astype(F32) * (sg + a * sg * (1.0 - sg))).astype(o_ref.dtype)

        @pl.when(j >= nf)
        def _():
            o_ref[...] = (d * (a * sg)).astype(o_ref.dtype)

    lo = pl.BlockSpec((ts, tf), lambda i, j: (i, j % nf))
    hi = pl.BlockSpec((ts, tf), lambda i, j: (i, j % nf + nf))
    return _pcall(kern, name=name, grid=(s // ts, 2 * nf), in_specs=[lo, hi, lo],
                  out_specs=pl.BlockSpec((ts, tf), lambda i, j: (i, j)), out_shape=jax.ShapeDtypeStruct((s, f2), BF16),
                  compiler_params=_params("parallel", "parallel"))(ab, ab, df)


def _loss_head(y, t, *, name):
    s, d = y.shape
    ts = _pick(s, (512, 256))
    n = s // ts

    def kern(y_ref, t_ref, dy_ref, l_ref, acc_ref):
        i = pl.program_id(0)

        @pl.when(i == 0)
        def _():
            acc_ref[...] = jnp.zeros_like(acc_ref)

        e = y_ref[...] - t_ref[...]
        dy_ref[...] = e / d
        acc_ref[...] += jnp.sum(e * e, axis=0, keepdims=True)

        @pl.when(i == n - 1)
        def _():
            l_ref[...] = jnp.sum(acc_ref[...], axis=1, keepdims=True) * (0.5 / d)

    row = pl.BlockSpec((ts, d), lambda i: (i, 0))
    return _pcall(kern, name=name, grid=(n,), in_specs=[row, row], out_specs=[row, pl.BlockSpec((1, 1), lambda i: (0, 0))],
                  out_shape=[jax.ShapeDtypeStruct((s, d), F32), jax.ShapeDtypeStruct((1, 1), F32)],
                  scratch_shapes=[pltpu.VMEM((1, d), F32)], compiler_params=_params("arbitrary"))(y, t)


def _head_mean(v, bd):
    hi = v.astype(BF16)
    lo = (v - hi.astype(F32)).astype(BF16)
    return (lax.dot_general(hi, bd, (((1,), (0,)), ((), ())), preferred_element_type=F32)
            + lax.dot_general(lo, bd, (((1,), (0,)), ((), ())), preferred_element_type=F32))


def _partner(v):
    w = v.shape[1]
    lane = lax.broadcasted_iota(jnp.int32, v.shape, 1)
    return jnp.where(lane % HEAD_DIM < HEAD_DIM // 2, pltpu.roll(v, w - HEAD_DIM // 2, 1), pltpu.roll(v, HEAD_DIM // 2, 1))


def _block_diag(w):
    r = lax.broadcasted_iota(jnp.int32, (w, w), 0) // HEAD_DIM
    c = lax.broadcasted_iota(jnp.int32, (w, w), 1) // HEAD_DIM
    return jnp.where(r == c, 1.0 / HEAD_DIM, 0.0).astype(BF16)


def _rope_tables(s):
    half = HEAD_DIM // 2
    inv_freq = jnp.power(ROPE_THETA, -jnp.arange(half, dtype=F32) / half)
    ang = jnp.arange(s).astype(F32)[:, None] * inv_freq[None, :]
    cos, sin = jnp.cos(ang), jnp.sin(ang)
    cos2 = jnp.concatenate([cos, cos, cos, cos], axis=1)
    sin2 = jnp.concatenate([-sin, sin, -sin, sin], axis=1)
    return cos2, sin2


def _qknorm_fwd(src, col0, width, gain, rope, *, name):
    s = src.shape[0]
    ts = _pick(s, (512, 256))
    cb = col0 // width
    assert col0 % width == 0
    reps = width // LANES
    g = jnp.tile(gain, (1, width // HEAD_DIM))

    def kern(*refs):
        if rope is None:
            x_ref, g_ref, o_ref = refs
        else:
            x_ref, g_ref, c_ref, s_ref, o_ref = refs
        x = x_ref[...].astype(F32)
        bd = _block_diag(width)
        r = lax.rsqrt(_head_mean(x * x, bd) + NORM_EPS)
        y = x * r * g_ref[...]
        if rope is not None:
            y = y * jnp.tile(c_ref[...], (1, reps)) + _partner(y) * jnp.tile(s_ref[...], (1, reps))
        o_ref[...] = y.astype(o_ref.dtype)

    xs = pl.BlockSpec((ts, width), lambda i: (i, cb))
    tab = pl.BlockSpec((ts, LANES), lambda i: (i, 0))
    ins = [src, g] + ([] if rope is None else list(rope))
    specs = [xs, pl.BlockSpec((1, width), lambda i: (0, 0))] + ([] if rope is None else [tab, tab])
    return _pcall(kern, name=name, grid=(s // ts,), in_specs=specs, out_specs=pl.BlockSpec((ts, width), lambda i: (i, 0)),
                  out_shape=jax.ShapeDtypeStruct((s, width), BF16), compiler_params=_params("parallel"))(*ins)


def _qknorm_bwd(src, col0, width, gain, rope, dout, *, name):
    s = src.shape[0]
    ts = _pick(s, (512, 256))
    cb = col0 // width
    reps = width // LANES
    g = jnp.tile(gain, (1, width // HEAD_DIM))

    def kern(*refs):
        if rope is None:
            x_ref, g_ref, do_ref, dx_ref, dg_ref = refs
        else:
            x_ref, g_ref, c_ref, s_ref, do_ref, dx_ref, dg_ref = refs
        x = x_ref[...].astype(F32)
        bd = _block_diag(width)
        r = lax.rsqrt(_head_mean(x * x, bd) + NORM_EPS)
        xh = x * r
        dy = do_ref[...].astype(F32)
        if rope is not None:
            dy = dy * jnp.tile(c_ref[...], (1, reps)) + _partner(dy * jnp.tile(s_ref[...], (1, reps)))
        dxh = dy * g_ref[...]
        dx_ref[...] = (r * (dxh - xh * _head_mean(dxh * xh, bd))).astype(dx_ref.dtype)

        @pl.when(pl.program_id(0) == 0)
        def _():
            dg_ref[...] = jnp.zeros_like(dg_ref)

        dg_ref[...] += jnp.sum(dy * xh, axis=0, keepdims=True)

    xs = pl.BlockSpec((ts, width), lambda i: (i, cb))
    row = pl.BlockSpec((ts, width), lambda i: (i, 0))
    vec = pl.BlockSpec((1, width), lambda i: (0, 0))
    tab = pl.BlockSpec((ts, LANES), lambda i: (i, 0))
    ins = [src, g] + ([] if rope is None else list(rope)) + [dout]
    specs = [xs, vec] + ([] if rope is None else [tab, tab]) + [row]
    dx, dg = _pcall(kern, name=name, grid=(s // ts,), in_specs=specs, out_specs=[row, vec],
                    out_shape=[jax.ShapeDtypeStruct((s, width), BF16), jax.ShapeDtypeStruct((1, width), F32)],
                    compiler_params=_params("arbitrary"))(*ins)
    return dx, jnp.sum(dg.reshape(width // HEAD_DIM, HEAD_DIM), axis=0, keepdims=True)


def _tri(strict):
    r = lax.broadcasted_iota(jnp.int32, (2 * QB, QB), 0) % QB
    c = lax.broadcasted_iota(jnp.int32, (2 * QB, QB), 1)
    return jnp.where((r > c) if strict else (r >= c), 1.0, 0.0).astype(BF16)


def _split_dot(v, t2):
    hi = v.astype(BF16)
    lo = (v - hi.astype(F32)).astype(BF16)
    return lax.dot_general(jnp.concatenate([hi, lo], axis=1), t2, (((1,), (0,)), ((), ())), preferred_element_type=F32)


LOG2E = 1.4426950408889634


def _log2_sigmoids(z2):
    lf = -(jnp.maximum(z2, 0.0) + jnp.log2(1.0 + jnp.exp2(-jnp.abs(z2))))
    return z2 + lf, lf


def _key_blocks(t):
    s = t.shape[0]
    n = t.shape[1] // HEAD_DIM
    return t.reshape(s // QB, QB, n, HEAD_DIM).transpose(2, 0, 3, 1)


def _from_key_blocks(t):
    n, nb, hd, qb = t.shape
    return t.transpose(1, 3, 0, 2).reshape(nb * qb, n * hd)


SB_SUB = 4


def _sb_fwd(q, kt, vt, *, name):
    h, s, hd = q.shape
    rq = SB_SUB * QB
    nq = s // rq
    nb = s // QB

    def kern(q_ref, k_ref, v_ref, o_ref):
        i = pl.program_id(1)
        qb = q_ref[...]
        t2 = _tri(True)
        rel = lax.broadcasted_iota(jnp.int32, (rq, QB), 1) - lax.broadcasted_iota(jnp.int32, (rq, QB), 0)

        def tile(j, carry, acc, masked):
            ls, lf = _log2_sigmoids(_nn(qb, k_ref[j]) * (SCALE * LOG2E))
            if masked:
                before = rel < i * rq - j * QB
                lf = jnp.where(before, lf, 0.0)
            w = jnp.exp2(ls + _split_dot(lf, t2) + carry)
            if masked:
                w = jnp.where(before, w, 0.0)
            return carry + jnp.sum(lf, axis=1, keepdims=True), acc + _nt(w, v_ref[j])

        carry, acc = jnp.zeros((rq, 1), F32), jnp.zeros((rq, hd), F32)
        for a in range(SB_SUB):
            carry, acc = tile(i * SB_SUB + (SB_SUB - 1 - a), carry, acc, True)

        def cond(st):
            return jnp.logical_and(st[0] >= 0, st[1] > 0)

        def body(st):
            j, _, carry, acc = st
            carry, acc = tile(j, carry, acc, False)
            return j - 1, (jnp.max(carry) > SB_DEAD).astype(jnp.int32), carry, acc

        _, _, _, acc = lax.while_loop(cond, body, (i * SB_SUB - 1, jnp.int32(1), carry, acc))
        o_ref[...] = acc

    blk = pl.BlockSpec((None, rq, hd), lambda a, i: (a, i, 0))
    full = pl.BlockSpec((None, nb, hd, QB), lambda a, i: (a, 0, 0, 0))
    return _pcall(kern, name=name, grid=(h, nq), in_specs=[blk, full, full], out_specs=blk,
                  out_shape=jax.ShapeDtypeStruct((h, s, hd), F32), compiler_params=_params("parallel", "arbitrary"))(q, kt, vt)


def _sb_bwd(q, kt, vt, o, do, *, name):
    h, s, hd = q.shape
    rq = SB_SUB * QB
    nq = s // rq
    nb = s // QB

    def kern(q_ref, k_ref, v_ref, o_ref, do_ref, dq_ref, dk_ref, dv_ref):
        i = pl.program_id(1)

        @pl.when(i == 0)
        def _():
            dk_ref[...] = jnp.zeros_like(dk_ref)
            dv_ref[...] = jnp.zeros_like(dv_ref)

        qb = q_ref[...]
        dob = do_ref[...]
        dsum = jnp.sum(dob.astype(F32) * o_ref[...], axis=1, keepdims=True)
        t_strict = _tri(True)
        t_incl = _tri(False)
        rel = lax.broadcasted_iota(jnp.int32, (rq, QB), 1) - lax.broadcasted_iota(jnp.int32, (rq, QB), 0)

        def tile(j, carry, gcarry, dq, masked):
            kb = k_ref[j]
            ls, lf = _log2_sigmoids(_nn(qb, kb) * (SCALE * LOG2E))
            if masked:
                before = rel < i * rq - j * QB
                lf = jnp.where(before, lf, 0.0)
            w = jnp.exp2(ls + _split_dot(lf, t_strict) + carry)
            if masked:
                w = jnp.where(before, w, 0.0)
            wr = w.astype(MXU_DT)
            g = _nn(dob, v_ref[j]) * wr.astype(F32)
            big_g = dsum - (_split_dot(g, t_incl) + gcarry)
            sig = jnp.exp2(ls)
            dz = g * (1.0 - sig) - sig * big_g
            if masked:
                dz = jnp.where(before, dz, 0.0)
            dz = dz * SCALE
            dk_ref[j] += _tn(qb, dz)
            dv_ref[j] += _tn(dob, wr)
            return (carry + jnp.sum(lf, axis=1, keepdims=True), gcarry + jnp.sum(g, axis=1, keepdims=True),
                    dq + _nt(dz, kb))

        carry, gcarry, dq = jnp.zeros((rq, 1), F32), jnp.zeros((rq, 1), F32), jnp.zeros((rq, hd), F32)
        for a in range(SB_SUB):
            carry, gcarry, dq = tile(i * SB_SUB + (SB_SUB - 1 - a), carry, gcarry, dq, True)

        def cond(st):
            return jnp.logical_and(st[0] >= 0, st[1] > 0)

        def body(st):
            j, _, carry, gcarry, dq = st
            carry, gcarry, dq = tile(j, carry, gcarry, dq, False)
            return j - 1, (jnp.max(carry) > SB_DEAD).astype(jnp.int32), carry, gcarry, dq

        st = lax.while_loop(cond, body, (i * SB_SUB - 1, jnp.int32(1), carry, gcarry, dq))
        dq_ref[...] = st[4]

    blk = pl.BlockSpec((None, rq, hd), lambda a, i: (a, i, 0))
    full = pl.BlockSpec((None, nb, hd, QB), lambda a, i: (a, 0, 0, 0))
    kshape = jax.ShapeDtypeStruct((h, nb, hd, QB), F32)
    return _pcall(kern, name=name, grid=(h, nq), in_specs=[blk, full, full, blk, blk], out_specs=[blk, full, full],
                  out_shape=[jax.ShapeDtypeStruct((h, s, hd), F32), kshape, kshape],
                  compiler_params=_params("parallel", "arbitrary"))(q, kt, vt, o, do)


DSA_SUB = 4


def _dsa_seq_blocks(t, s):
    steps_per_group = 4 * s // (QB * DSA_SUB)
    g = t // steps_per_group
    b0, b1, b2 = (s // (QB * r) for _, r in DSA_GROUPS)
    return jnp.where(g == 0, b0, jnp.where(g == 1, b1, b2))


def _dsa_rel():
    qi = lax.broadcasted_iota(jnp.int32, (QB, QB), 0)
    kj = lax.broadcasted_iota(jnp.int32, (QB, QB), 1)
    return kj - qi


def _prev_mask(rel, has_prev):
    return rel >= jnp.where(has_prev, 0, QB)


def _dsa_fwd(q, k, vp, *, name):
    rows = q.shape[0]
    s = rows // 12
    big = QB * DSA_SUB
    nsteps = rows // big

    def kern(q_ref, k_ref, kp_ref, v_ref, vpv_ref, o_ref):
        t = pl.program_id(0)
        bps = _dsa_seq_blocks(t, s)
        rel = _dsa_rel()
        lane = lax.broadcasted_iota(jnp.int32, (QB, LANES), 1)
        for a in range(DSA_SUB):
            qa = q_ref[pl.ds(a * QB, QB), :]
            kc = k_ref[pl.ds(a * QB, QB), :]
            vc = v_ref[pl.ds(a * QB, QB), :]
            if a == 0:
                kpv, vpv = kp_ref[...], vpv_ref[...]
            else:
                kpv, vpv = k_ref[pl.ds((a - 1) * QB, QB), :], v_ref[pl.ds((a - 1) * QB, QB), :]
            has_prev = (t * DSA_SUB + a) % bps != 0
            sc = jnp.where(rel <= 0, _nt(qa, kc) * SCALE, -jnp.inf)
            sp = jnp.where(_prev_mask(rel, has_prev), _nt(qa, kpv) * SCALE, -jnp.inf)
            m = jnp.maximum(jnp.max(sc, axis=1, keepdims=True), jnp.max(sp, axis=1, keepdims=True))
            pc = jnp.exp(sc - m)
            pp = jnp.exp(sp - m)
            den = jnp.sum(pc, axis=1, keepdims=True) + jnp.sum(pp, axis=1, keepdims=True)
            o = (_nn(pc, vc) + _nn(pp, vpv)) / den
            o_ref[pl.ds(a * QB, QB), :] = jnp.where(lane < HEAD_DIM, o, m + jnp.log(den))

    cur64 = pl.BlockSpec((big, HEAD_DIM), lambda t: (t, 0))
    prev64 = pl.BlockSpec((QB, HEAD_DIM), lambda t: (jnp.maximum(t * DSA_SUB - 1, 0), 0))
    cur128 = pl.BlockSpec((big, LANES), lambda t: (t, 0))
    prev128 = pl.BlockSpec((QB, LANES), lambda t: (jnp.maximum(t * DSA_SUB - 1, 0), 0))
    return _pcall(kern, name=name, grid=(nsteps,), in_specs=[cur64, cur64, prev64, cur128, prev128], out_specs=cur128,
                  out_shape=jax.ShapeDtypeStruct((rows, LANES), F32), compiler_params=_params("parallel"))(q, k, k, vp, vp)


def _dsa_combine(p0, p1, p2, *, name):
    hh, s, _ = p0.shape
    ts = _pick(s, (512, 256))

    def kern(a_ref, b_ref, c_ref, o_ref):
        lane = lax.broadcasted_iota(jnp.int32, (ts, LANES), 1)
        xs = [a_ref[...], b_ref[...], c_ref[...]]
        ls = [jnp.where(lane < HEAD_DIM, pltpu.roll(x, HEAD_DIM, 1), x) for x in xs]
        m = jnp.maximum(jnp.maximum(ls[0], ls[1]), ls[2])
        es = [jnp.exp(l - m) for l in ls]
        den = es[0] + es[1] + es[2]
        o = (es[0] * xs[0] + es[1] * xs[1] + es[2] * xs[2]) / den
        o_ref[...] = jnp.where(lane < HEAD_DIM, o, m + jnp.log(den))

    blk = pl.BlockSpec((None, ts, LANES), lambda a, i: (a, i, 0))
    return _pcall(kern, name=name, grid=(hh, s // ts), in_specs=[blk, blk, blk], out_specs=blk,
                  out_shape=jax.ShapeDtypeStruct((hh, s, LANES), F32), compiler_params=_params("parallel", "parallel"))(p0, p1, p2)


def _dsa_bwd_prep(comb, dop, *, name):
    hh, s, _ = comb.shape
    ts = _pick(s, (512, 256))

    def kern(c_ref, d_ref, o_ref):
        lane = lax.broadcasted_iota(jnp.int32, (ts, LANES), 1)
        c = c_ref[...]
        d = d_ref[...]
        dsum = jnp.sum(jnp.where(lane < HEAD_DIM, c * d, 0.0), axis=1, keepdims=True)
        o_ref[...] = jnp.where(lane < HEAD_DIM, d, jnp.where(lane < HEAD_DIM + 32, c, dsum))

    blk = pl.BlockSpec((None, ts, LANES), lambda a, i: (a, i, 0))
    return _pcall(kern, name=name, grid=(hh, s // ts), in_specs=[blk, blk], out_specs=blk,
                  out_shape=jax.ShapeDtypeStruct((hh, s, LANES), F32), compiler_params=_params("parallel", "parallel"))(comb, dop)


def _dsa_bwd(q, k, vp, pk, *, name):
    rows = q.shape[0]
    s = rows // 12
    big = QB * DSA_SUB
    nsteps = rows // big
    nblk = rows // QB

    def kern(q_ref, qn_ref, k_ref, kp_ref, v_ref, vpv_ref, p_ref, pn_ref, dq_ref, dk_ref, dv_ref):
        t = pl.program_id(0)
        bps = _dsa_seq_blocks(t, s)
        rel = _dsa_rel()
        lane = lax.broadcasted_iota(jnp.int32, (QB, LANES), 1)

        def stats(pa):
            lse = jnp.max(jnp.where(jnp.logical_and(lane >= HEAD_DIM, lane < HEAD_DIM + 32), pa, -jnp.inf), axis=1, keepdims=True)
            dsum = jnp.max(jnp.where(lane >= HEAD_DIM + 32, pa, -jnp.inf), axis=1, keepdims=True)
            return lse, dsum

        def pair(qa, pa, st, kb, vb, mask):
            p = jnp.where(mask, jnp.exp(_nt(qa, kb) * SCALE - st[0]), 0.0)
            ds = p * (_nt(pa, vb) - st[1]) * SCALE
            return _nn(ds, kb), _tn(ds, qa), _tn(p, pa)

        for a in range(DSA_SUB):
            qa = q_ref[pl.ds(a * QB, QB), :]
            pa = p_ref[pl.ds(a * QB, QB), :]
            st = stats(pa)
            kc = k_ref[pl.ds(a * QB, QB), :]
            vc = v_ref[pl.ds(a * QB, QB), :]
            if a == 0:
                kpv, vpv = kp_ref[...], vpv_ref[...]
            else:
                kpv, vpv = k_ref[pl.ds((a - 1) * QB, QB), :], v_ref[pl.ds((a - 1) * QB, QB), :]
            has_prev = (t * DSA_SUB + a) % bps != 0
            dq_c, dk_c, dv_c = pair(qa, pa, st, kc, vc, rel <= 0)
            dq_p, dk_p, dv_p = pair(qa, pa, st, kpv, vpv, _prev_mask(rel, has_prev))
            dq_ref[pl.ds(a * QB, QB), :] = dq_c + dq_p
            if a == 0:
                dk_ref[pl.ds(0, QB), :] = dk_c
                dv_ref[pl.ds(0, QB), :] = dv_c
            else:
                dk_ref[pl.ds(a * QB, QB), :] = dk_c
                dv_ref[pl.ds(a * QB, QB), :] = dv_c
                dk_ref[pl.ds((a - 1) * QB, QB), :] += dk_p
                dv_ref[pl.ds((a - 1) * QB, QB), :] += dv_p
        nxt = t * DSA_SUB + DSA_SUB
        has_next = jnp.logical_and(nxt < nblk, nxt % bps != 0)
        last = (DSA_SUB - 1) * QB
        pn = pn_ref[...]
        _, dk_n, dv_n = pair(qn_ref[...], pn, stats(pn), k_ref[pl.ds(last, QB), :], v_ref[pl.ds(last, QB), :],
                             _prev_mask(rel, has_next))
        dk_ref[pl.ds(last, QB), :] += dk_n
        dv_ref[pl.ds(last, QB), :] += dv_n

    def prev_map(t):
        return (jnp.maximum(t * DSA_SUB - 1, 0), 0)

    def next_map(t):
        return (jnp.minimum(t * DSA_SUB + DSA_SUB, nblk - 1), 0)

    cur64 = pl.BlockSpec((big, HEAD_DIM), lambda t: (t, 0))
    cur128 = pl.BlockSpec((big, LANES), lambda t: (t, 0))
    specs = [cur64, pl.BlockSpec((QB, HEAD_DIM), next_map), cur64, pl.BlockSpec((QB, HEAD_DIM), prev_map),
             cur128, pl.BlockSpec((QB, LANES), prev_map), cur128, pl.BlockSpec((QB, LANES), next_map)]
    return _pcall(kern, name=name, grid=(nsteps,), in_specs=specs, out_specs=[cur64, cur64, cur128],
                  out_shape=[jax.ShapeDtypeStruct((rows, HEAD_DIM), F32), jax.ShapeDtypeStruct((rows, HEAD_DIM), F32),
                             jax.ShapeDtypeStruct((rows, LANES), F32)],
                  compiler_params=_params("parallel"))(q, q, k, k, vp, vp, pk, pk)


def _mem_fwd(q, km, vm, *, name):
    hh, s, hd = q.shape
    ml = km.shape[1]
    tq = _pick(s, (512, 256))

    def kern(q_ref, k_ref, v_ref, o_ref):
        sc = _nt(q_ref[...], k_ref[...]) * SCALE
        e = jnp.exp(sc - jnp.max(sc, axis=1, keepdims=True))
        p = e / jnp.sum(e, axis=1, keepdims=True)
        o_ref[...] = _nn(p, v_ref[...])

    blk = pl.BlockSpec((None, tq, hd), lambda a, i: (a, i, 0))
    kv = pl.BlockSpec((None, ml, hd), lambda a, i: (a, 0, 0))
    return _pcall(kern, name=name, grid=(hh, s // tq), in_specs=[blk, kv, kv], out_specs=blk,
                  out_shape=jax.ShapeDtypeStruct((hh, s, hd), F32), compiler_params=_params("parallel", "parallel"))(q, km, vm)


def _mem_bwd(q, km, vm, do, *, name):
    hh, s, hd = q.shape
    ml = km.shape[1]
    tq = _pick(s, (512, 256))

    def kern(q_ref, k_ref, v_ref, do_ref, dq_ref, dk_ref, dv_ref):
        @pl.when(pl.program_id(1) == 0)
        def _():
            dk_ref[...] = jnp.zeros_like(dk_ref)
            dv_ref[...] = jnp.zeros_like(dv_ref)

        qb = q_ref[...]
        dob = do_ref[...]
        sc = _nt(qb, k_ref[...]) * SCALE
        e = jnp.exp(sc - jnp.max(sc, axis=1, keepdims=True))
        p = e / jnp.sum(e, axis=1, keepdims=True)
        dp = _nt(dob, v_ref[...])
        ds = p * (dp - jnp.sum(p * dp, axis=1, keepdims=True)) * SCALE
        dq_ref[...] = _nn(ds, k_ref[...])
        dk_ref[...] += _tn(ds, qb)
        dv_ref[...] += _tn(p, dob)

    blk = pl.BlockSpec((None, tq, hd), lambda a, i: (a, i, 0))
    kv = pl.BlockSpec((None, ml, hd), lambda a, i: (a, 0, 0))
    kvs = jax.ShapeDtypeStruct((hh, ml, hd), F32)
    return _pcall(kern, name=name, grid=(hh, s // tq), in_specs=[blk, kv, kv, blk], out_specs=[blk, kv, kv],
                  out_shape=[jax.ShapeDtypeStruct((hh, s, hd), F32), kvs, kvs],
                  compiler_params=_params("parallel", "arbitrary"))(q, km, vm, do)


def _merge_fwd(logits, bias, ya, yb, yc, *, name):
    s, d = ya.shape
    ts = _pick(s, (512, 256))

    def kern(l0, l1, l2, b0, b1, b2, a_ref, b_ref, c_ref, o_ref):
        m = (_sigmoid(l0[...] + b0[...]) * a_ref[...] + _sigmoid(l1[...] + b1[...]) * b_ref[...]
             + _sigmoid(l2[...] + b2[...]) * c_ref[...])
        o_ref[...] = m.astype(o_ref.dtype)

    row = pl.BlockSpec((ts, d), lambda i: (i, 0))
    lg = [pl.BlockSpec((ts, d), functools.partial(lambda i, c: (i, c), c=c)) for c in range(3)]
    bs = [pl.BlockSpec((1, d), functools.partial(lambda i, c: (0, c), c=c)) for c in range(3)]
    return _pcall(kern, name=name, grid=(s // ts,), in_specs=lg + bs + [row, row, row], out_specs=row,
                  out_shape=jax.ShapeDtypeStruct((s, d), BF16),
                  compiler_params=_params("parallel"))(logits, logits, logits, bias, bias, bias, ya, yb, yc)


def _merge_bwd(logits, bias, ya, yb, yc, dm, *, name):
    s, d = ya.shape
    ts = _pick(s, (256,))

    def kern(l0, l1, l2, b0, b1, b2, a_ref, b_ref, c_ref, dm_ref, da_ref, db_ref, dc_ref, dl0, dl1, dl2, dbias0, dbias1, dbias2):
        first = pl.program_id(0) == 0
        dmv = dm_ref[...]
        for l_ref, bb_ref, y_ref, dy_ref, dl_ref, dbias_ref in ((l0, b0, a_ref, da_ref, dl0, dbias0), (l1, b1, b_ref, db_ref, dl1, dbias1),
                                                                (l2, b2, c_ref, dc_ref, dl2, dbias2)):
            g = _sigmoid(l_ref[...] + bb_ref[...])
            dy_ref[...] = (dmv * g).astype(dy_ref.dtype)
            dl = dmv * y_ref[...] * g * (1.0 - g)
            dl_ref[...] = dl.astype(dl_ref.dtype)

            @pl.when(first)
            def _():
                dbias_ref[...] = jnp.zeros_like(dbias_ref)

            dbias_ref[...] += jnp.sum(dl, axis=0, keepdims=True)

    row = pl.BlockSpec((ts, d), lambda i: (i, 0))
    lg = [pl.BlockSpec((ts, d), functools.partial(lambda i, c: (i, c), c=c)) for c in range(3)]
    bs = [pl.BlockSpec((1, d), functools.partial(lambda i, c: (0, c), c=c)) for c in range(3)]
    vec = pl.BlockSpec((1, d), lambda i: (0, 0))
    yshape = jax.ShapeDtypeStruct((s, d), BF16)
    vshape = jax.ShapeDtypeStruct((1, d), F32)
    outs = _pcall(kern, name=name, grid=(s // ts,), in_specs=lg + bs + [row, row, row, row],
                  out_specs=[row, row, row, row, row, row, vec, vec, vec],
                  out_shape=[yshape] * 6 + [vshape] * 3,
                  compiler_params=_params("arbitrary"))(logits, logits, logits, bias, bias, bias, ya, yb, yc, dm)
    return outs[0], outs[1], outs[2], outs[3:6], jnp.concatenate(outs[6:9], axis=1)


def _heads(t, n):
    s = t.shape[0]
    return t.reshape(s, n, HEAD_DIM).transpose(1, 0, 2)


def _unheads(t):
    n, s, hd = t.shape
    return t.transpose(1, 0, 2).reshape(s, n * hd)


def _to_class_major(t):
    s = t.shape[0]
    w = t.shape[1] // (DSA_HPG * len(DSA_GROUPS))
    parts = []
    for g, (_, r) in enumerate(DSA_GROUPS):
        tg = t[:, g * DSA_HPG * w:(g + 1) * DSA_HPG * w].reshape(s // r, r, DSA_HPG, w)
        parts.append(tg.transpose(2, 1, 0, 3).reshape(DSA_HPG * s, w))
    return jnp.concatenate(parts, axis=0)


def _slot_to_class_major(t):
    hh, s, w = t.shape
    parts = []
    for _, r in DSA_GROUPS:
        parts.append(t.reshape(hh, s // r, r, w).transpose(0, 2, 1, 3).reshape(hh * s, w))
    return jnp.concatenate(parts, axis=0)


def _from_class_major(t):
    rows, w = t.shape
    s = rows // 12
    out = []
    for g, (_, r) in enumerate(DSA_GROUPS):
        tg = t[g * 4 * s:(g + 1) * 4 * s].reshape(DSA_HPG, r, s // r, w)
        out.append(tg.transpose(0, 2, 1, 3).reshape(DSA_HPG, s, w))
    return out


def _pad_lanes(t):
    return jnp.concatenate([t, jnp.zeros(t.shape[:-1] + (LANES - t.shape[-1],), t.dtype)], axis=-1)


def _ffn_fwd(x, norm, w13, w2, tag):
    h = _rms_fwd(x, norm, name=f"{tag}_rms")
    ab = _matmul(h, w13, name=f"{tag}_up", out_dtype=F32)
    f = _swiglu_fwd(ab, name=f"{tag}_act")
    y = _matmul(f, w2, name=f"{tag}_down", res=x, alpha=0.5, tk=1408)
    return y, (h, ab, f)


def _ffn_bwd(x, norm, w13, w2, saved, dy, tag):
    h, ab, f = saved
    dyb = dy.astype(BF16)
    dw2 = _matmul(f, dyb, name=f"{tag}_dw2", ta=True, alpha=0.5, tm=1408, tn=1024, tk=512)
    df = _matmul(dyb, w2, name=f"{tag}_df", tb=True, alpha=0.5, out_dtype=BF16, tn=1408)
    dab = _swiglu_bwd(ab, df, name=f"{tag}_dact")
    dw13 = _matmul(h, dab, name=f"{tag}_dw13", ta=True, tm=1024, tn=1408, tk=512)
    dh = _matmul(dab, w13, name=f"{tag}_dh", tb=True, tn=1024, tk=1408)
    dx, dnorm = _rms_bwd(x, norm, dh, dy, name=f"{tag}_drms")
    return dx, dnorm, dw13, dw2


def _local_step(x, mem, w, loss_target):
    s, d = x.shape
    assert s % (QB * 16) == 0
    fdim = w['ffn1_w1'].shape[1]
    w13_1 = jnp.concatenate([w['ffn1_w1'], w['ffn1_w3']], axis=1)
    w13_2 = jnp.concatenate([w['ffn2_w1'], w['ffn2_w3']], axis=1)
    rope = _rope_tables(s)

    x1, sv1 = _ffn_fwd(x, w['ffn1_norm'], w13_1, w['ffn1_w2'], "ffn1")
    h = _rms_fwd(x1, w['mix_norm'], name="mix_rms")
    p = _matmul(h, w['w_in'], name="in_proj", out_dtype=BF16)
    logits = _matmul(h, w['w_gate'], name="gate_proj")
    c_qb, c_kb, c_vb, c_qc = 3 * SB_W, 3 * SB_W + DSA_W, 3 * SB_W + 2 * DSA_W, 3 * SB_W + 3 * DSA_W

    qa = _heads(p[:, :SB_W], SB_HEADS)
    ka, va = _key_blocks(p[:, SB_W:2 * SB_W]), _key_blocks(p[:, 2 * SB_W:3 * SB_W])
    oa = _sb_fwd(qa, ka, va, name="sb_fwd")
    oa_t = _unheads(oa)
    ya = _matmul(oa_t, w['w_branch_sb'], name="sb_out")

    qb_n = _qknorm_fwd(p, c_qb, DSA_W, w['qn_dsa'], rope, name="dsa_qnorm")
    kb_n = _qknorm_fwd(p, c_kb, DSA_W, w['kn_dsa'], rope, name="dsa_knorm")
    qb_c, kb_c = _to_class_major(qb_n), _to_class_major(kb_n)
    vb_c = _pad_lanes(_to_class_major(p[:, c_vb:c_vb + DSA_W]))
    ob_groups = _from_class_major(_dsa_fwd(qb_c, kb_c, vb_c, name="dsa_fwd"))
    comb = _dsa_combine(*ob_groups, name="dsa_combine")
    ob_t = _unheads(comb[:, :, :HEAD_DIM])
    yb = _matmul(ob_t, w['w_branch_dsa'], name="dsa_out")

    memh = _rms_fwd(mem, w['mem_norm'], name="mem_rms")
    kv = _matmul(memh, w['w_mem_kv'], name="mem_kv", out_dtype=BF16)
    km_n = _qknorm_fwd(kv, 0, MEM_W, w['kn_mem'], None, name="mem_knorm")
    qc_n = _qknorm_fwd(p, c_qc, MEM_W, w['qn_mem'], None, name="mem_qnorm")
    qc_h, km_h, vm_h = _heads(qc_n, MEM_HEADS), _heads(km_n, MEM_HEADS), _heads(kv[:, MEM_W:], MEM_HEADS)
    oc = _mem_fwd(qc_h, km_h, vm_h, name="mem_fwd")
    oc_t = _unheads(oc)
    yc = _matmul(oc_t, w['w_branch_mem'], name="mem_out")

    merged = _merge_fwd(logits, w['b_gate'], ya, yb, yc, name="merge")
    x2 = _matmul(merged, w['w_out'], name="out_proj", res=x1)
    x3, sv2 = _ffn_fwd(x2, w['ffn2_norm'], w13_2, w['ffn2_w2'], "ffn2")
    dx3, loss = _loss_head(x3, loss_target, name="loss")

    g = {}
    dx2, g['ffn2_norm'], dw13, g['ffn2_w2'] = _ffn_bwd(x2, w['ffn2_norm'], w13_2, w['ffn2_w2'], sv2, dx3, "ffn2")
    g['ffn2_w1'], g['ffn2_w3'] = dw13[:, :fdim], dw13[:, fdim:]

    dx2b = dx2.astype(BF16)
    g['w_out'] = _matmul(merged, dx2b, name="d_w_out", ta=True, tk=512)
    dm = _matmul(dx2b, w['w_out'], name="d_merged", tb=True)
    dya, dyb, dyc, dlog, g['b_gate'] = _merge_bwd(logits, w['b_gate'], ya, yb, yc, dm, name="d_merge")
    dlogits = jnp.concatenate(dlog, axis=1)

    g['w_branch_sb'] = _matmul(oa_t, dya, name="d_w_sb", ta=True, tk=512)
    g['w_branch_dsa'] = _matmul(ob_t, dyb, name="d_w_dsa", ta=True, tk=512)
    g['w_branch_mem'] = _matmul(oc_t, dyc, name="d_w_mem", ta=True, tk=512)
    doa = _matmul(dya, w['w_branch_sb'], name="d_oa", tb=True, out_dtype=BF16)
    dob = _matmul(dyb, w['w_branch_dsa'], name="d_ob", tb=True)
    doc = _matmul(dyc, w['w_branch_mem'], name="d_oc", tb=True, out_dtype=BF16)

    dqa, dka, dva = _sb_bwd(qa, ka, va, oa, _heads(doa, SB_HEADS), name="sb_bwd")

    pk = _dsa_bwd_prep(comb, _pad_lanes(_heads(dob, DSA_HPG)), name="dsa_prep")
    dq_c, dk_c, dv_c = _dsa_bwd(qb_c, kb_c, vb_c, _slot_to_class_major(pk), name="dsa_bwd")
    dqb_n = jnp.concatenate([_unheads(t) for t in _from_class_major(dq_c)], axis=1)
    dkb_n = jnp.concatenate([_unheads(t) for t in _from_class_major(dk_c)], axis=1)
    dvb = jnp.concatenate([_unheads(t[:, :, :HEAD_DIM]) for t in _from_class_major(dv_c)], axis=1).astype(BF16)
    dqb, g['qn_dsa'] = _qknorm_bwd(p, c_qb, DSA_W, w['qn_dsa'], rope, dqb_n, name="d_dsa_qnorm")
    dkb, g['kn_dsa'] = _qknorm_bwd(p, c_kb, DSA_W, w['kn_dsa'], rope, dkb_n, name="d_dsa_knorm")

    dqc_h, dkm_h, dvm_h = _mem_bwd(qc_h, km_h, vm_h, _heads(doc, MEM_HEADS), name="mem_bwd")
    dqc, g['qn_mem'] = _qknorm_bwd(p, c_qc, MEM_W, w['qn_mem'], None, _unheads(dqc_h), name="d_mem_qnorm")
    dkm, g['kn_mem'] = _qknorm_bwd(kv, 0, MEM_W, w['kn_mem'], None, _unheads(dkm_h), name="d_mem_knorm")
    dkv = jnp.concatenate([dkm, _unheads(dvm_h).astype(BF16)], axis=1)
    g['w_mem_kv'] = _matmul(memh, dkv, name="d_w_mem_kv", ta=True)
    dmemh = _matmul(dkv, w['w_mem_kv'], name="d_memh", tb=True)
    _, g['mem_norm'] = _rms_bwd(mem, w['mem_norm'], dmemh, None, name="d_mem_rms")

    dp = jnp.concatenate([_unheads(dqa).astype(BF16), _from_key_blocks(dka).astype(BF16), _from_key_blocks(dva).astype(BF16),
                          dqb, dkb, dvb, dqc], axis=1)
    g['w_in'] = _matmul(h, dp, name="d_w_in", ta=True, tk=512)
    g['w_gate'] = _matmul(h, dlogits, name="d_w_gate", ta=True, tk=512)
    dh = _matmul(dp, w['w_in'], name="d_h_in", tb=True)
    dh = _matmul(dlogits, w['w_gate'], name="d_h_gate", tb=True, res=dh)
    dx1, g['mix_norm'] = _rms_bwd(x1, w['mix_norm'], dh, dx2, name="d_mix_rms")

    dx0, g['ffn1_norm'], dw13, g['ffn1_w2'] = _ffn_bwd(x, w['ffn1_norm'], w13_1, w['ffn1_w2'], sv1, dx1, "ffn1")
    g['ffn1_w1'], g['ffn1_w3'] = dw13[:, :fdim], dw13[:, fdim:]
    return loss, dx0, g


def _pack_rows(d, names):
    return jnp.concatenate([d[n].reshape(-1, LANES) for n in names], axis=0)


def _unpack_rows(t, like, names):
    out, off = {}, 0
    for n in names:
        r = like[n].size // LANES
        out[n] = t[off:off + r].reshape(like[n].shape)
        off += r
    return out


def _unpack_gathered(t, local, names):
    out, off = {}, 0
    for n in names:
        r, c = local[n].shape
        rows = r * c // LANES
        blk = t[:, off:off + rows].reshape(N_DEV, r, c)
        out[n] = blk.reshape(N_DEV * r, c) if SHARD_AXIS[n] == 0 else blk.transpose(1, 0, 2).reshape(r, N_DEV * c)
        off += rows
    return out


def _pack_for_owners(g, local, names):
    parts = []
    for n in names:
        r, c = local[n].shape
        blk = g[n].reshape(N_DEV, r, c) if SHARD_AXIS[n] == 0 else g[n].reshape(r, N_DEV, c).transpose(1, 0, 2)
        parts.append(blk.reshape(N_DEV, r * c // LANES, LANES))
    return jnp.concatenate(parts, axis=1)


def _pack_small(d, names, extra_rows):
    parts = []
    for n in names:
        v = d[n].reshape(-1)
        pad = (-v.size) % LANES
        parts.append(jnp.concatenate([v, jnp.zeros((pad,), v.dtype)]).reshape(-1, LANES))
    t = jnp.concatenate(parts, axis=0)
    return jnp.concatenate([t, jnp.zeros((extra_rows, LANES), t.dtype)], axis=0)


def _unpack_small(t, like, names):
    out, off = {}, 0
    for n in names:
        size = like[n].size
        rows = -(-size // LANES)
        out[n] = t[off:off + rows].reshape(-1)[:size].reshape(like[n].shape)
        off += rows
    return out


def _exchange(src, per_peer, *, name):
    rows = src.shape[-2]

    def body(src_ref, out_ref, send_sems, recv_sems, local_sem):
        x, y, c = lax.axis_index("x"), lax.axis_index("y"), lax.axis_index("c")
        me = 4 * x + 2 * y + c
        mine = pltpu.make_async_copy(src_ref.at[me] if per_peer else src_ref, out_ref.at[me], local_sem)
        mine.start()
        copies = []
        for k in range(1, N_DEV):
            px = 1 - x if k & 4 else x
            py = 1 - y if k & 2 else y
            pc = 1 - c if k & 1 else c
            cp = pltpu.make_async_remote_copy(
                src_ref=src_ref.at[4 * px + 2 * py + pc] if per_peer else src_ref, dst_ref=out_ref.at[me],
                send_sem=send_sems.at[k - 1], recv_sem=recv_sems.at[k - 1],
                device_id=(px, py, pc), device_id_type=pl.DeviceIdType.MESH)
            cp.start()
            copies.append(cp)
        for cp in copies:
            cp.wait_recv()
        for cp in copies:
            cp.wait_send()
        mine.wait()

    anyspace = pl.BlockSpec(memory_space=pl.ANY)
    return _pcall(body, name=name, in_specs=[anyspace], out_specs=anyspace,
                  out_shape=jax.ShapeDtypeStruct((N_DEV, rows, LANES), src.dtype),
                  scratch_shapes=[pltpu.SemaphoreType.DMA((N_DEV - 1,)), pltpu.SemaphoreType.DMA((N_DEV - 1,)),
                                  pltpu.SemaphoreType.DMA])(src)


def _adamw(recv, w, m, v, *, name):
    rows = w.shape[0]
    tr = _pick(rows, (512, 256, 128, 64))

    def kern(r_ref, w_ref, m_ref, v_ref, g_ref, d_ref, mo_ref, vo_ref):
        g = r_ref[0]
        for p in range(1, N_DEV):
            g = g + r_ref[p]
        mn = ADAM_B1 * m_ref[...] + (1.0 - ADAM_B1) * g
        vn = ADAM_B2 * v_ref[...] + (1.0 - ADAM_B2) * (g * g)
        m_hat = mn / (1.0 - ADAM_B1 ** ADAM_STEP)
        v_hat = vn / (1.0 - ADAM_B2 ** ADAM_STEP)
        g_ref[...] = g
        d_ref[...] = -ADAM_LR * (m_hat / (jnp.sqrt(v_hat) + ADAM_EPS) + ADAM_WD * w_ref[...])
        mo_ref[...] = mn
        vo_ref[...] = vn

    row = pl.BlockSpec((tr, LANES), lambda i: (i, 0))
    shp = jax.ShapeDtypeStruct((rows, LANES), F32)
    return _pcall(kern, name=name, grid=(rows // tr,), in_specs=[pl.BlockSpec((N_DEV, tr, LANES), lambda i: (0, i, 0)), row, row, row],
                  out_specs=[row, row, row, row], out_shape=[shp, shp, shp, shp], compiler_params=_params("parallel"))(recv, w, m, v)


INPUTS = ['x', 'mem'] + WEIGHTS + ['loss_target'] + ['m_' + n for n in WEIGHTS] + ['v_' + n for n in WEIGHTS]
SMALL_PAD_ROWS = 4


def kernel(x, mem, ffn1_norm, ffn1_w1, ffn1_w3, ffn1_w2, mix_norm, mem_norm, w_in, w_mem_kv, qn_dsa, kn_dsa, qn_mem, kn_mem, w_branch_sb, w_branch_dsa, w_branch_mem, w_gate, b_gate, w_out, ffn2_norm, ffn2_w1, ffn2_w3, ffn2_w2, loss_target, m_ffn1_norm, m_ffn1_w1, m_ffn1_w3, m_ffn1_w2, m_mix_norm, m_mem_norm, m_w_in, m_w_mem_kv, m_qn_dsa, m_kn_dsa, m_qn_mem, m_kn_mem, m_w_branch_sb, m_w_branch_dsa, m_w_branch_mem, m_w_gate, m_b_gate, m_w_out, m_ffn2_norm, m_ffn2_w1, m_ffn2_w3, m_ffn2_w2, v_ffn1_norm, v_ffn1_w1, v_ffn1_w3, v_ffn1_w2, v_mix_norm, v_mem_norm, v_w_in, v_w_mem_kv, v_qn_dsa, v_kn_dsa, v_qn_mem, v_kn_mem, v_w_branch_sb, v_w_branch_dsa, v_w_branch_mem, v_w_gate, v_b_gate, v_w_out, v_ffn2_norm, v_ffn2_w1, v_ffn2_w3, v_ffn2_w2):
    given = dict(zip(INPUTS, (x, mem, ffn1_norm, ffn1_w1, ffn1_w3, ffn1_w2, mix_norm, mem_norm, w_in, w_mem_kv, qn_dsa, kn_dsa, qn_mem, kn_mem, w_branch_sb, w_branch_dsa, w_branch_mem, w_gate, b_gate, w_out, ffn2_norm, ffn2_w1, ffn2_w3, ffn2_w2, loss_target, m_ffn1_norm, m_ffn1_w1, m_ffn1_w3, m_ffn1_w2, m_mix_norm, m_mem_norm, m_w_in, m_w_mem_kv, m_qn_dsa, m_kn_dsa, m_qn_mem, m_kn_mem, m_w_branch_sb, m_w_branch_dsa, m_w_branch_mem, m_w_gate, m_b_gate, m_w_out, m_ffn2_norm, m_ffn2_w1, m_ffn2_w3, m_ffn2_w2, v_ffn1_norm, v_ffn1_w1, v_ffn1_w3, v_ffn1_w2, v_mix_norm, v_mem_norm, v_w_in, v_w_mem_kv, v_qn_dsa, v_kn_dsa, v_qn_mem, v_kn_mem, v_w_branch_sb, v_w_branch_dsa, v_w_branch_mem, v_w_gate, v_b_gate, v_w_out, v_ffn2_norm, v_ffn2_w1, v_ffn2_w3, v_ffn2_w2), strict=True))
    wl = {n: given[n][0] for n in BIG}
    ws = {n: given[n] for n in SMALL}

    gathered = _exchange(_pack_rows({n: wl[n].astype(BF16) for n in BIG}, BIG), False, name="gather_weights")
    whole = _unpack_gathered(gathered, wl, BIG)
    loss, dx, g = _local_step(x[0], mem[0], {**whole, **ws}, loss_target[0])

    recv = _exchange(_pack_for_owners(g, wl, BIG), True, name="scatter_grads")
    big = _adamw(recv, _pack_rows(wl, BIG), _pack_rows({n: given['m_' + n][0] for n in BIG}, BIG),
                 _pack_rows({n: given['v_' + n][0] for n in BIG}, BIG), name="adamw_sharded")
    big = [_unpack_rows(t, wl, BIG) for t in big]

    gs = _pack_small(g, SMALL, SMALL_PAD_ROWS)
    loss_row = gs.shape[0] - SMALL_PAD_ROWS
    gs = gs.at[loss_row, 0].set(loss[0, 0])
    recv_s = _exchange(gs, False, name="gather_small")
    small = _adamw(recv_s, _pack_small(ws, SMALL, SMALL_PAD_ROWS), _pack_small({n: given['m_' + n] for n in SMALL}, SMALL, SMALL_PAD_ROWS),
                   _pack_small({n: given['v_' + n] for n in SMALL}, SMALL, SMALL_PAD_ROWS), name="adamw_replicated")
    total_loss = small[0][loss_row, 0]
    small = [_unpack_small(t, ws, SMALL) for t in small]

    outs = [total_loss, dx[None]]
    for kind in range(4):
        outs += [big[kind][n][None] if n in wl else small[kind][n] for n in WEIGHTS]
    return tuple(outs)
```

```python
import functools
import math

import jax
import jax.numpy as jnp
from jax import lax
from jax.experimental import pallas as pl
from jax.experimental.pallas import tpu as pltpu

F32 = jnp.float32
BF16 = jnp.bfloat16
MXU_DT = jnp.bfloat16

N_DEV = 8
HEAD_DIM = 64
SB_HEADS = 8
DSA_GROUPS = ((128, 1), (512, 4), (2048, 16))
DSA_HPG = 4
MEM_HEADS = 4
SB_W = SB_HEADS * HEAD_DIM
DSA_W = DSA_HPG * len(DSA_GROUPS) * HEAD_DIM
DSA_OUT_W = DSA_HPG * HEAD_DIM
MEM_W = MEM_HEADS * HEAD_DIM
ROPE_THETA = 10000.0
NORM_EPS = 1e-6
QB = 128
SCALE = HEAD_DIM ** -0.5
ADAM_LR, ADAM_B1, ADAM_B2, ADAM_EPS, ADAM_WD, ADAM_STEP = 0.001, 0.9, 0.999, 1e-08, 0.01, 10

LANES = 128
VMEM_LIMIT = 48 * 1024 * 1024
SB_DEAD = -110.0 * 1.4426950408889634

WEIGHTS = ['ffn1_norm', 'ffn1_w1', 'ffn1_w3', 'ffn1_w2', 'mix_norm', 'mem_norm', 'w_in', 'w_mem_kv', 'qn_dsa', 'kn_dsa',
           'qn_mem', 'kn_mem', 'w_branch_sb', 'w_branch_dsa', 'w_branch_mem', 'w_gate', 'b_gate', 'w_out', 'ffn2_norm',
           'ffn2_w1', 'ffn2_w3', 'ffn2_w2']
SHARD_AXIS = {'ffn1_norm': None, 'ffn1_w1': 1, 'ffn1_w3': 1, 'ffn1_w2': 0, 'mix_norm': None, 'mem_norm': None, 'w_in': 1,
              'w_mem_kv': 0, 'qn_dsa': None, 'kn_dsa': None, 'qn_mem': None, 'kn_mem': None, 'w_branch_sb': 1,
              'w_branch_dsa': 1, 'w_branch_mem': 1, 'w_gate': 1, 'b_gate': None, 'w_out': 0, 'ffn2_norm': None,
              'ffn2_w1': 1, 'ffn2_w3': 1, 'ffn2_w2': 0}
BIG = [n for n in WEIGHTS if SHARD_AXIS[n] is not None]
SMALL = [n for n in WEIGHTS if SHARD_AXIS[n] is None]


def _pcall(kern, **kw):
    return pl.pallas_call(kern, **kw)


def _params(*sem):
    return pltpu.CompilerParams(dimension_semantics=sem, vmem_limit_bytes=VMEM_LIMIT)


def _dot(a, b, dims):
    return lax.dot_general(a.astype(MXU_DT), b.astype(MXU_DT), (dims, ((), ())), preferred_element_type=F32)


def _nn(a, b):
    return _dot(a, b, ((1,), (0,)))


def _nt(a, b):
    return _dot(a, b, ((1,), (1,)))


def _tn(a, b):
    return _dot(a, b, ((0,), (0,)))


def _pick(n, prefs):
    for p in prefs:
        if n % p == 0:
            return p
    return n


def _matmul(a, b, *, name, ta=False, tb=False, out_dtype=F32, res=None, alpha=1.0, tm=1024, tn=512, tk=1024, pair2=None):
    if ta:
        kdim, m = a.shape
    else:
        m, kdim = a.shape
    n = b.shape[0] if tb else b.shape[1]
    tm = _pick(m, (tm, 512, 256, 128))
    tn = _pick(n, (tn, 512, 384, 256, 128))
    tk = _pick(kdim, (tk, 1024, 512, 256, 128))
    nk = kdim // tk
    a_spec = pl.BlockSpec((tk, tm), lambda i, j, k: (k, i)) if ta else pl.BlockSpec((tm, tk), lambda i, j, k: (i, k))
    b_spec = pl.BlockSpec((tn, tk), lambda i, j, k: (j, k)) if tb else pl.BlockSpec((tk, tn), lambda i, j, k: (k, j))
    o_spec = pl.BlockSpec((tm, tn), lambda i, j, k: (i, j))
    dims = ((0 if ta else 1,), (1 if tb else 0,))

    def kern(*refs):
        refs = list(refs)
        acc_ref = refs.pop()
        o_ref = refs.pop()
        r_ref = refs.pop() if res is not None else None
        k = pl.program_id(2)

        @pl.when(k == 0)
        def _():
            acc_ref[...] = jnp.zeros_like(acc_ref)

        part = _dot(refs[0][...], refs[1][...], dims)
        if pair2 is not None:
            part = part + _dot(refs[2][...], refs[3][...], dims)
        acc_ref[...] += part

        @pl.when(k == nk - 1)
        def _():
            r = acc_ref[...]
            if alpha != 1.0:
                r = r * alpha
            if r_ref is not None:
                r = r_ref[...] + r
            o_ref[...] = r.astype(out_dtype)

    ins = [a, b] + ([] if pair2 is None else list(pair2)) + ([] if res is None else [res])
    specs = [a_spec, b_spec] + ([] if pair2 is None else [a_spec, b_spec]) + ([] if res is None else [o_spec])
    return _pcall(kern, name=name, grid=(m // tm, n // tn, nk), in_specs=specs, out_specs=o_spec,
                  out_shape=jax.ShapeDtypeStruct((m, n), out_dtype), scratch_shapes=[pltpu.VMEM((tm, tn), F32)],
                  compiler_params=_params("parallel", "parallel", "arbitrary"))(*ins)


def _rms_fwd(x, g, *, name):
    s, d = x.shape
    ts = _pick(s, (512, 256))

    def kern(x_ref, g_ref, h_ref):
        xf = x_ref[...]
        r = lax.rsqrt(jnp.mean(xf * xf, axis=-1, keepdims=True) + NORM_EPS)
        h_ref[...] = (xf * r * g_ref[...]).astype(h_ref.dtype)

    return _pcall(kern, name=name, grid=(s // ts,),
                  in_specs=[pl.BlockSpec((ts, d), lambda i: (i, 0)), pl.BlockSpec((1, d), lambda i: (0, 0))],
                  out_specs=pl.BlockSpec((ts, d), lambda i: (i, 0)), out_shape=jax.ShapeDtypeStruct((s, d), BF16),
                  compiler_params=_params("parallel"))(x, g)


def _rms_bwd(x, g, dh, res, *, name):
    s, d = x.shape
    ts = _pick(s, (512, 256))

    def kern(*refs):
        if res is None:
            x_ref, g_ref, dh_ref, dx_ref, dxb_ref, dg_ref = refs
            r_ref = None
        else:
            x_ref, g_ref, dh_ref, r_ref, dx_ref, dxb_ref, dg_ref = refs
        xf = x_ref[...]
        r = lax.rsqrt(jnp.mean(xf * xf, axis=-1, keepdims=True) + NORM_EPS)
        xh = xf * r
        dhf = dh_ref[...].astype(F32)
        dy = dhf * g_ref[...]
        dx = r * (dy - xh * jnp.mean(dy * xh, axis=-1, keepdims=True))
        if r_ref is not None:
            dx = r_ref[...] + dx
        dx_ref[...] = dx
        dxb_ref[...] = dx.astype(dxb_ref.dtype)

        @pl.when(pl.program_id(0) == 0)
        def _():
            dg_ref[...] = jnp.zeros_like(dg_ref)

        dg_ref[...] += jnp.sum(dhf * xh, axis=0, keepdims=True)

    row = pl.BlockSpec((ts, d), lambda i: (i, 0))
    vec = pl.BlockSpec((1, d), lambda i: (0, 0))
    ins = [x, g, dh] + ([] if res is None else [res])
    return _pcall(kern, name=name, grid=(s // ts,), in_specs=[row, vec, row] + ([] if res is None else [row]),
                  out_specs=[row, row, vec],
                  out_shape=[jax.ShapeDtypeStruct((s, d), F32), jax.ShapeDtypeStruct((s, d), BF16), jax.ShapeDtypeStruct((1, d), F32)],
                  compiler_params=_params("arbitrary"))(*ins)


def _sigmoid(x):
    return 1.0 / (1.0 + jnp.exp(-x))


FFN_TM, FFN_TF = 512, 1408


def _ffn_up(h, w1, w3, *, name):
    s, d = h.shape
    fdim = w1.shape[1]
    tm, tf = _pick(s, (FFN_TM, 256)), _pick(fdim, (FFN_TF, 512, 256, 128))

    def kern(h_ref, w1_ref, w3_ref, a_ref, b_ref, f_ref):
        hb = h_ref[...]
        a = _nn(hb, w1_ref[...])
        b = _nn(hb, w3_ref[...])
        a_ref[...] = a.astype(a_ref.dtype)
        b_ref[...] = b.astype(b_ref.dtype)
        f_ref[...] = (a * _sigmoid(a) * b).astype(f_ref.dtype)

    wspec = pl.BlockSpec((d, tf), lambda i, j: (0, j))
    ospec = pl.BlockSpec((tm, tf), lambda i, j: (i, j))
    shp = jax.ShapeDtypeStruct((s, fdim), BF16)
    return _pcall(kern, name=name, grid=(s // tm, fdim // tf), in_specs=[pl.BlockSpec((tm, d), lambda i, j: (i, 0)), wspec, wspec],
                  out_specs=[ospec, ospec, ospec], out_shape=[shp, shp, shp],
                  compiler_params=_params("parallel", "parallel"))(h, w1, w3)


def _ffn_dact(dy, w2, a, b, *, name):
    s, d = dy.shape
    fdim = w2.shape[0]
    tm, tf = _pick(s, (FFN_TM, 256)), _pick(fdim, (FFN_TF, 512, 256, 128))

    def kern(dy_ref, w2_ref, a_ref, b_ref, da_ref, db_ref):
        df = _nt(dy_ref[...], w2_ref[...]) * 0.5
        av = a_ref[...].astype(F32)
        sg = _sigmoid(av)
        da_ref[...] = (df * b_ref[...].astype(F32) * (sg + av * sg * (1.0 - sg))).astype(da_ref.dtype)
        db_ref[...] = (df * (av * sg)).astype(db_ref.dtype)

    ospec = pl.BlockSpec((tm, tf), lambda i, j: (i, j))
    shp = jax.ShapeDtypeStruct((s, fdim), BF16)
    return _pcall(kern, name=name, grid=(s // tm, fdim // tf),
                  in_specs=[pl.BlockSpec((tm, d), lambda i, j: (i, 0)), pl.BlockSpec((tf, d), lambda i, j: (j, 0)), ospec, ospec],
                  out_specs=[ospec, ospec], out_shape=[shp, shp], compiler_params=_params("parallel", "parallel"))(dy, w2, a, b)


def _loss_head(y, t, *, name):
    s, d = y.shape
    ts = _pick(s, (512, 256))
    n = s // ts

    def kern(y_ref, t_ref, dy_ref, dyb_ref, l_ref, acc_ref):
        i = pl.program_id(0)

        @pl.when(i == 0)
        def _():
            acc_ref[...] = jnp.zeros_like(acc_ref)

        e = y_ref[...] - t_ref[...]
        dy_ref[...] = e / d
        dyb_ref[...] = (e / d).astype(dyb_ref.dtype)
        acc_ref[...] += jnp.sum(e * e, axis=0, keepdims=True)

        @pl.when(i == n - 1)
        def _():
            l_ref[...] = jnp.sum(acc_ref[...], axis=1, keepdims=True) * (0.5 / d)

    row = pl.BlockSpec((ts, d), lambda i: (i, 0))
    return _pcall(kern, name=name, grid=(n,), in_specs=[row, row], out_specs=[row, row, pl.BlockSpec((1, 1), lambda i: (0, 0))],
                  out_shape=[jax.ShapeDtypeStruct((s, d), F32), jax.ShapeDtypeStruct((s, d), BF16), jax.ShapeDtypeStruct((1, 1), F32)],
                  scratch_shapes=[pltpu.VMEM((1, d), F32)], compiler_params=_params("arbitrary"))(y, t)


def _head_mean(v, bd):
    hi = v.astype(BF16)
    lo = (v - hi.astype(F32)).astype(BF16)
    return (lax.dot_general(hi, bd, (((1,), (0,)), ((), ())), preferred_element_type=F32)
            + lax.dot_general(lo, bd, (((1,), (0,)), ((), ())), preferred_element_type=F32))


def _partner(v):
    w = v.shape[1]
    lane = lax.broadcasted_iota(jnp.int32, v.shape, 1)
    return jnp.where(lane % HEAD_DIM < HEAD_DIM // 2, pltpu.roll(v, w - HEAD_DIM // 2, 1), pltpu.roll(v, HEAD_DIM // 2, 1))


def _block_diag(w):
    r = lax.broadcasted_iota(jnp.int32, (w, w), 0) // HEAD_DIM
    c = lax.broadcasted_iota(jnp.int32, (w, w), 1) // HEAD_DIM
    return jnp.where(r == c, 1.0 / HEAD_DIM, 0.0).astype(BF16)


def _rope_tables(s):
    half = HEAD_DIM // 2
    inv_freq = jnp.power(ROPE_THETA, -jnp.arange(half, dtype=F32) / half)
    ang = jnp.arange(s).astype(F32)[:, None] * inv_freq[None, :]
    cos, sin = jnp.cos(ang), jnp.sin(ang)
    cos2 = jnp.concatenate([cos, cos, cos, cos], axis=1)
    sin2 = jnp.concatenate([-sin, sin, -sin, sin], axis=1)
    return cos2, sin2


def _qknorm_fwd(src, col0, width, gain, rope, *, name):
    s = src.shape[0]
    ts = _pick(s, (512, 256))
    cb = col0 // width
    assert col0 % width == 0
    reps = width // LANES
    g = jnp.tile(gain, (1, width // HEAD_DIM))

    def kern(*refs):
        if rope is None:
            x_ref, g_ref, o_ref = refs
        else:
            x_ref, g_ref, c_ref, s_ref, o_ref = refs
        x = x_ref[...].astype(F32)
        bd = _block_diag(width)
        r = lax.rsqrt(_head_mean(x * x, bd) + NORM_EPS)
        y = x * r * g_ref[...]
        if rope is not None:
            y = y * jnp.tile(c_ref[...], (1, reps)) + _partner(y) * jnp.tile(s_ref[...], (1, reps))
        o_ref[...] = y.astype(o_ref.dtype)

    xs = pl.BlockSpec((ts, width), lambda i: (i, cb))
    tab = pl.BlockSpec((ts, LANES), lambda i: (i, 0))
    ins = [src, g] + ([] if rope is None else list(rope))
    specs = [xs, pl.BlockSpec((1, width), lambda i: (0, 0))] + ([] if rope is None else [tab, tab])
    return _pcall(kern, name=name, grid=(s // ts,), in_specs=specs, out_specs=pl.BlockSpec((ts, width), lambda i: (i, 0)),
                  out_shape=jax.ShapeDtypeStruct((s, width), BF16), compiler_params=_params("parallel"))(*ins)


def _qknorm_bwd(src, col0, width, gain, rope, dout, *, name):
    s = src.shape[0]
    ts = _pick(s, (512, 256))
    cb = col0 // width
    reps = width // LANES
    g = jnp.tile(gain, (1, width // HEAD_DIM))

    def kern(*refs):
        if rope is None:
            x_ref, g_ref, do_ref, dx_ref, dg_ref = refs
        else:
            x_ref, g_ref, c_ref, s_ref, do_ref, dx_ref, dg_ref = refs
        x = x_ref[...].astype(F32)
        bd = _block_diag(width)
        r = lax.rsqrt(_head_mean(x * x, bd) + NORM_EPS)
        xh = x * r
        dy = do_ref[...].astype(F32)
        if rope is not None:
            dy = dy * jnp.tile(c_ref[...], (1, reps)) + _partner(dy * jnp.tile(s_ref[...], (1, reps)))
        dxh = dy * g_ref[...]
        dx_ref[...] = (r * (dxh - xh * _head_mean(dxh * xh, bd))).astype(dx_ref.dtype)

        @pl.when(pl.program_id(0) == 0)
        def _():
            dg_ref[...] = jnp.zeros_like(dg_ref)

        dg_ref[...] += jnp.sum(dy * xh, axis=0, keepdims=True)

    xs = pl.BlockSpec((ts, width), lambda i: (i, cb))
    row = pl.BlockSpec((ts, width), lambda i: (i, 0))
    vec = pl.BlockSpec((1, width), lambda i: (0, 0))
    tab = pl.BlockSpec((ts, LANES), lambda i: (i, 0))
    ins = [src, g] + ([] if rope is None else list(rope)) + [dout]
    specs = [xs, vec] + ([] if rope is None else [tab, tab]) + [row]
    dx, dg = _pcall(kern, name=name, grid=(s // ts,), in_specs=specs, out_specs=[row, vec],
                    out_shape=[jax.ShapeDtypeStruct((s, width), BF16), jax.ShapeDtypeStruct((1, width), F32)],
                    compiler_params=_params("arbitrary"))(*ins)
    return dx, jnp.sum(dg.reshape(width // HEAD_DIM, HEAD_DIM), axis=0, keepdims=True)


def _tri(strict):
    r = lax.broadcasted_iota(jnp.int32, (2 * QB, QB), 0) % QB
    c = lax.broadcasted_iota(jnp.int32, (2 * QB, QB), 1)
    return jnp.where((r > c) if strict else (r >= c), 1.0, 0.0).astype(BF16)


def _split_dot(v, t2):
    hi = v.astype(BF16)
    lo = (v - hi.astype(F32)).astype(BF16)
    return lax.dot_general(jnp.concatenate([hi, lo], axis=1), t2, (((1,), (0,)), ((), ())), preferred_element_type=F32)


LOG2E = 1.4426950408889634


def _log2_sigmoids(z2):
    lf = -(jnp.maximum(z2, 0.0) + jnp.log2(1.0 + jnp.exp2(-jnp.abs(z2))))
    return z2 + lf, lf


def _key_blocks(t):
    s = t.shape[0]
    n = t.shape[1] // HEAD_DIM
    return t.reshape(s // QB, QB, n, HEAD_DIM).transpose(2, 0, 3, 1)


def _from_key_blocks(t):
    n, nb, hd, qb = t.shape
    return t.transpose(1, 3, 0, 2).reshape(nb * qb, n * hd)


SB_SUB = 4


def _sb_fwd(q, kt, vt, *, name):
    h, s, hd = q.shape
    rq = SB_SUB * QB
    nq = s // rq
    nb = s // QB

    def kern(q_ref, k_ref, v_ref, o_ref):
        i = pl.program_id(1)
        qb = q_ref[...]
        t2 = _tri(True)
        rel = lax.broadcasted_iota(jnp.int32, (rq, QB), 1) - lax.broadcasted_iota(jnp.int32, (rq, QB), 0)

        def tile(j, carry, acc, masked):
            ls, lf = _log2_sigmoids(_nn(qb, k_ref[j]) * (SCALE * LOG2E))
            if masked:
                before = rel < i * rq - j * QB
                lf = jnp.where(before, lf, 0.0)
            w = jnp.exp2(ls + _split_dot(lf, t2) + carry)
            if masked:
                w = jnp.where(before, w, 0.0)
            return carry + jnp.sum(lf, axis=1, keepdims=True), acc + _nt(w, v_ref[j])

        carry, acc = jnp.zeros((rq, 1), F32), jnp.zeros((rq, hd), F32)
        for a in range(SB_SUB):
            carry, acc = tile(i * SB_SUB + (SB_SUB - 1 - a), carry, acc, True)

        def cond(st):
            return jnp.logical_and(st[0] >= 0, st[1] > 0)

        def body(st):
            j, _, carry, acc = st
            carry, acc = tile(j, carry, acc, False)
            return j - 1, (jnp.max(carry) > SB_DEAD).astype(jnp.int32), carry, acc

        _, _, _, acc = lax.while_loop(cond, body, (i * SB_SUB - 1, jnp.int32(1), carry, acc))
        o_ref[...] = acc

    blk = pl.BlockSpec((None, rq, hd), lambda a, i: (a, i, 0))
    full = pl.BlockSpec((None, nb, hd, QB), lambda a, i: (a, 0, 0, 0))
    return _pcall(kern, name=name, grid=(h, nq), in_specs=[blk, full, full], out_specs=blk,
                  out_shape=jax.ShapeDtypeStruct((h, s, hd), F32), compiler_params=_params("parallel", "arbitrary"))(q, kt, vt)


def _sb_bwd(q, kt, vt, o, do, *, name):
    h, s, hd = q.shape
    rq = SB_SUB * QB
    nq = s // rq
    nb = s // QB

    def kern(q_ref, k_ref, v_ref, o_ref, do_ref, dq_ref, dk_ref, dv_ref):
        i = pl.program_id(1)

        @pl.when(i == 0)
        def _():
            dk_ref[...] = jnp.zeros_like(dk_ref)
            dv_ref[...] = jnp.zeros_like(dv_ref)

        qb = q_ref[...]
        dob = do_ref[...]
        dsum = jnp.sum(dob.astype(F32) * o_ref[...], axis=1, keepdims=True)
        t_strict = _tri(True)
        t_incl = _tri(False)
        rel = lax.broadcasted_iota(jnp.int32, (rq, QB), 1) - lax.broadcasted_iota(jnp.int32, (rq, QB), 0)

        def tile(j, carry, gcarry, dq, masked):
            kb = k_ref[j]
            ls, lf = _log2_sigmoids(_nn(qb, kb) * (SCALE * LOG2E))
            if masked:
                before = rel < i * rq - j * QB
                lf = jnp.where(before, lf, 0.0)
            w = jnp.exp2(ls + _split_dot(lf, t_strict) + carry)
            if masked:
                w = jnp.where(before, w, 0.0)
            wr = w.astype(MXU_DT)
            g = _nn(dob, v_ref[j]) * wr.astype(F32)
            big_g = dsum - (_split_dot(g, t_incl) + gcarry)
            sig = jnp.exp2(ls)
            dz = g * (1.0 - sig) - sig * big_g
            if masked:
                dz = jnp.where(before, dz, 0.0)
            dz = dz * SCALE
            dk_ref[j] += _tn(qb, dz)
            dv_ref[j] += _tn(dob, wr)
            return (carry + jnp.sum(lf, axis=1, keepdims=True), gcarry + jnp.sum(g, axis=1, keepdims=True),
                    dq + _nt(dz, kb))

        carry, gcarry, dq = jnp.zeros((rq, 1), F32), jnp.zeros((rq, 1), F32), jnp.zeros((rq, hd), F32)
        for a in range(SB_SUB):
            carry, gcarry, dq = tile(i * SB_SUB + (SB_SUB - 1 - a), carry, gcarry, dq, True)

        def cond(st):
            return jnp.logical_and(st[0] >= 0, st[1] > 0)

        def body(st):
            j, _, carry, gcarry, dq = st
            carry, gcarry, dq = tile(j, carry, gcarry, dq, False)
            return j - 1, (jnp.max(carry) > SB_DEAD).astype(jnp.int32), carry, gcarry, dq

        st = lax.while_loop(cond, body, (i * SB_SUB - 1, jnp.int32(1), carry, gcarry, dq))
        dq_ref[...] = st[4]

    blk = pl.BlockSpec((None, rq, hd), lambda a, i: (a, i, 0))
    full = pl.BlockSpec((None, nb, hd, QB), lambda a, i: (a, 0, 0, 0))
    kshape = jax.ShapeDtypeStruct((h, nb, hd, QB), F32)
    return _pcall(kern, name=name, grid=(h, nq), in_specs=[blk, full, full, blk, blk], out_specs=[blk, full, full],
                  out_shape=[jax.ShapeDtypeStruct((h, s, hd), F32), kshape, kshape],
                  compiler_params=_params("parallel", "arbitrary"))(q, kt, vt, o, do)


DSA_SUB = 4


def _dsa_seq_blocks(t, s):
    steps_per_group = 4 * s // (QB * DSA_SUB)
    g = t // steps_per_group
    b0, b1, b2 = (s // (QB * r) for _, r in DSA_GROUPS)
    return jnp.where(g == 0, b0, jnp.where(g == 1, b1, b2))


def _dsa_rel():
    qi = lax.broadcasted_iota(jnp.int32, (QB, QB), 0)
    kj = lax.broadcasted_iota(jnp.int32, (QB, QB), 1)
    return kj - qi


def _prev_mask(rel, has_prev):
    return rel >= jnp.where(has_prev, 0, QB)


def _dsa_fwd(q, k, vp, *, name):
    rows = q.shape[0]
    s = rows // 12
    big = QB * DSA_SUB
    nsteps = rows // big

    def kern(q_ref, k_ref, kp_ref, v_ref, vpv_ref, o_ref):
        t = pl.program_id(0)
        bps = _dsa_seq_blocks(t, s)
        rel = _dsa_rel()
        lane = lax.broadcasted_iota(jnp.int32, (QB, LANES), 1)
        for a in range(DSA_SUB):
            qa = q_ref[pl.ds(a * QB, QB), :]
            kc = k_ref[pl.ds(a * QB, QB), :]
            vc = v_ref[pl.ds(a * QB, QB), :]
            if a == 0:
                kpv, vpv = kp_ref[...], vpv_ref[...]
            else:
                kpv, vpv = k_ref[pl.ds((a - 1) * QB, QB), :], v_ref[pl.ds((a - 1) * QB, QB), :]
            has_prev = (t * DSA_SUB + a) % bps != 0
            sc = jnp.where(rel <= 0, _nt(qa, kc) * SCALE, -jnp.inf)
            sp = jnp.where(_prev_mask(rel, has_prev), _nt(qa, kpv) * SCALE, -jnp.inf)
            m = jnp.maximum(jnp.max(sc, axis=1, keepdims=True), jnp.max(sp, axis=1, keepdims=True))
            pc = jnp.exp(sc - m)
            pp = jnp.exp(sp - m)
            den = jnp.sum(pc, axis=1, keepdims=True) + jnp.sum(pp, axis=1, keepdims=True)
            o = (_nn(pc, vc) + _nn(pp, vpv)) / den
            o_ref[pl.ds(a * QB, QB), :] = jnp.where(lane < HEAD_DIM, o, m + jnp.log(den))

    cur64 = pl.BlockSpec((big, HEAD_DIM), lambda t: (t, 0))
    prev64 = pl.BlockSpec((QB, HEAD_DIM), lambda t: (jnp.maximum(t * DSA_SUB - 1, 0), 0))
    cur128 = pl.BlockSpec((big, LANES), lambda t: (t, 0))
    prev128 = pl.BlockSpec((QB, LANES), lambda t: (jnp.maximum(t * DSA_SUB - 1, 0), 0))
    return _pcall(kern, name=name, grid=(nsteps,), in_specs=[cur64, cur64, prev64, cur128, prev128], out_specs=cur128,
                  out_shape=jax.ShapeDtypeStruct((rows, LANES), F32), compiler_params=_params("parallel"))(q, k, k, vp, vp)


def _dsa_combine(p0, p1, p2, *, name):
    hh, s, _ = p0.shape
    ts = _pick(s, (512, 256))

    def kern(a_ref, b_ref, c_ref, o_ref):
        lane = lax.broadcasted_iota(jnp.int32, (ts, LANES), 1)
        xs = [a_ref[...], b_ref[...], c_ref[...]]
        ls = [jnp.where(lane < HEAD_DIM, pltpu.roll(x, HEAD_DIM, 1), x) for x in xs]
        m = jnp.maximum(jnp.maximum(ls[0], ls[1]), ls[2])
        es = [jnp.exp(l - m) for l in ls]
        den = es[0] + es[1] + es[2]
        o = (es[0] * xs[0] + es[1] * xs[1] + es[2] * xs[2]) / den
        o_ref[...] = jnp.where(lane < HEAD_DIM, o, m + jnp.log(den))

    blk = pl.BlockSpec((None, ts, LANES), lambda a, i: (a, i, 0))
    return _pcall(kern, name=name, grid=(hh, s // ts), in_specs=[blk, blk, blk], out_specs=blk,
                  out_shape=jax.ShapeDtypeStruct((hh, s, LANES), F32), compiler_params=_params("parallel", "parallel"))(p0, p1, p2)


def _dsa_bwd_prep(comb, dop, *, name):
    hh, s, _ = comb.shape
    ts = _pick(s, (512, 256))

    def kern(c_ref, d_ref, o_ref):
        lane = lax.broadcasted_iota(jnp.int32, (ts, LANES), 1)
        c = c_ref[...]
        d = d_ref[...]
        dsum = jnp.sum(jnp.where(lane < HEAD_DIM, c * d, 0.0), axis=1, keepdims=True)
        o_ref[...] = jnp.where(lane < HEAD_DIM, d, jnp.where(lane < HEAD_DIM + 32, c, dsum))

    blk = pl.BlockSpec((None, ts, LANES), lambda a, i: (a, i, 0))
    return _pcall(kern, name=name, grid=(hh, s // ts), in_specs=[blk, blk], out_specs=blk,
                  out_shape=jax.ShapeDtypeStruct((hh, s, LANES), F32), compiler_params=_params("parallel", "parallel"))(comb, dop)


def _dsa_bwd(q, k, vp, pk, *, name):
    rows = q.shape[0]
    s = rows // 12
    big = QB * DSA_SUB
    nsteps = rows // big
    nblk = rows // QB

    def kern(q_ref, qn_ref, k_ref, kp_ref, v_ref, vpv_ref, p_ref, pn_ref, dq_ref, dk_ref, dv_ref):
        t = pl.program_id(0)
        bps = _dsa_seq_blocks(t, s)
        rel = _dsa_rel()
        lane = lax.broadcasted_iota(jnp.int32, (QB, LANES), 1)

        def stats(pa):
            lse = jnp.max(jnp.where(jnp.logical_and(lane >= HEAD_DIM, lane < HEAD_DIM + 32), pa, -jnp.inf), axis=1, keepdims=True)
            dsum = jnp.max(jnp.where(lane >= HEAD_DIM + 32, pa, -jnp.inf), axis=1, keepdims=True)
            return lse, dsum

        def pair(qa, pa, st, kb, vb, mask):
            p = jnp.where(mask, jnp.exp(_nt(qa, kb) * SCALE - st[0]), 0.0)
            ds = p * (_nt(pa, vb) - st[1]) * SCALE
            return _nn(ds, kb), _tn(ds, qa), _tn(p, pa)

        for a in range(DSA_SUB):
            qa = q_ref[pl.ds(a * QB, QB), :]
            pa = p_ref[pl.ds(a * QB, QB), :]
            st = stats(pa)
            kc = k_ref[pl.ds(a * QB, QB), :]
            vc = v_ref[pl.ds(a * QB, QB), :]
            if a == 0:
                kpv, vpv = kp_ref[...], vpv_ref[...]
            else:
                kpv, vpv = k_ref[pl.ds((a - 1) * QB, QB), :], v_ref[pl.ds((a - 1) * QB, QB), :]
            has_prev = (t * DSA_SUB + a) % bps != 0
            dq_c, dk_c, dv_c = pair(qa, pa, st, kc, vc, rel <= 0)
            dq_p, dk_p, dv_p = pair(qa, pa, st, kpv, vpv, _prev_mask(rel, has_prev))
            dq_ref[pl.ds(a * QB, QB), :] = dq_c + dq_p
            if a == 0:
                dk_ref[pl.ds(0, QB), :] = dk_c
                dv_ref[pl.ds(0, QB), :] = dv_c
            else:
                dk_ref[pl.ds(a * QB, QB), :] = dk_c
                dv_ref[pl.ds(a * QB, QB), :] = dv_c
                dk_ref[pl.ds((a - 1) * QB, QB), :] += dk_p
                dv_ref[pl.ds((a - 1) * QB, QB), :] += dv_p
        nxt = t * DSA_SUB + DSA_SUB
        has_next = jnp.logical_and(nxt < nblk, nxt % bps != 0)
        last = (DSA_SUB - 1) * QB
        pn = pn_ref[...]
        _, dk_n, dv_n = pair(qn_ref[...], pn, stats(pn), k_ref[pl.ds(last, QB), :], v_ref[pl.ds(last, QB), :],
                             _prev_mask(rel, has_next))
        dk_ref[pl.ds(last, QB), :] += dk_n
        dv_ref[pl.ds(last, QB), :] += dv_n

    def prev_map(t):
        return (jnp.maximum(t * DSA_SUB - 1, 0), 0)

    def next_map(t):
        return (jnp.minimum(t * DSA_SUB + DSA_SUB, nblk - 1), 0)

    cur64 = pl.BlockSpec((big, HEAD_DIM), lambda t: (t, 0))
    cur128 = pl.BlockSpec((big, LANES), lambda t: (t, 0))
    specs = [cur64, pl.BlockSpec((QB, HEAD_DIM), next_map), cur64, pl.BlockSpec((QB, HEAD_DIM), prev_map),
             cur128, pl.BlockSpec((QB, LANES), prev_map), cur128, pl.BlockSpec((QB, LANES), next_map)]
    return _pcall(kern, name=name, grid=(nsteps,), in_specs=specs, out_specs=[cur64, cur64, cur128],
                  out_shape=[jax.ShapeDtypeStruct((rows, HEAD_DIM), F32), jax.ShapeDtypeStruct((rows, HEAD_DIM), F32),
                             jax.ShapeDtypeStruct((rows, LANES), F32)],
                  compiler_params=_params("parallel"))(q, q, k, k, vp, vp, pk, pk)


def _mem_fwd(q, km, vm, *, name):
    hh, s, hd = q.shape
    ml = km.shape[1]
    tq = _pick(s, (512, 256))

    def kern(q_ref, k_ref, v_ref, o_ref):
        sc = _nt(q_ref[...], k_ref[...]) * SCALE
        e = jnp.exp(sc - jnp.max(sc, axis=1, keepdims=True))
        p = e / jnp.sum(e, axis=1, keepdims=True)
        o_ref[...] = _nn(p, v_ref[...])

    blk = pl.BlockSpec((None, tq, hd), lambda a, i: (a, i, 0))
    kv = pl.BlockSpec((None, ml, hd), lambda a, i: (a, 0, 0))
    return _pcall(kern, name=name, grid=(hh, s // tq), in_specs=[blk, kv, kv], out_specs=blk,
                  out_shape=jax.ShapeDtypeStruct((hh, s, hd), F32), compiler_params=_params("parallel", "parallel"))(q, km, vm)


def _mem_bwd(q, km, vm, do, *, name):
    hh, s, hd = q.shape
    ml = km.shape[1]
    tq = _pick(s, (512, 256))

    def kern(q_ref, k_ref, v_ref, do_ref, dq_ref, dk_ref, dv_ref):
        @pl.when(pl.program_id(1) == 0)
        def _():
            dk_ref[...] = jnp.zeros_like(dk_ref)
            dv_ref[...] = jnp.zeros_like(dv_ref)

        qb = q_ref[...]
        dob = do_ref[...]
        sc = _nt(qb, k_ref[...]) * SCALE
        e = jnp.exp(sc - jnp.max(sc, axis=1, keepdims=True))
        p = e / jnp.sum(e, axis=1, keepdims=True)
        dp = _nt(dob, v_ref[...])
        ds = p * (dp - jnp.sum(p * dp, axis=1, keepdims=True)) * SCALE
        dq_ref[...] = _nn(ds, k_ref[...])
        dk_ref[...] += _tn(ds, qb)
        dv_ref[...] += _tn(p, dob)

    blk = pl.BlockSpec((None, tq, hd), lambda a, i: (a, i, 0))
    kv = pl.BlockSpec((None, ml, hd), lambda a, i: (a, 0, 0))
    kvs = jax.ShapeDtypeStruct((hh, ml, hd), F32)
    return _pcall(kern, name=name, grid=(hh, s // tq), in_specs=[blk, kv, kv, blk], out_specs=[blk, kv, kv],
                  out_shape=[jax.ShapeDtypeStruct((hh, s, hd), F32), kvs, kvs],
                  compiler_params=_params("parallel", "arbitrary"))(q, km, vm, do)


def _merge_fwd(logits, bias, ya, yb, yc, *, name):
    s, d = ya.shape
    ts = _pick(s, (512, 256))

    def kern(l0, l1, l2, b0, b1, b2, a_ref, b_ref, c_ref, o_ref):
        m = (_sigmoid(l0[...] + b0[...]) * a_ref[...] + _sigmoid(l1[...] + b1[...]) * b_ref[...]
             + _sigmoid(l2[...] + b2[...]) * c_ref[...])
        o_ref[...] = m.astype(o_ref.dtype)

    row = pl.BlockSpec((ts, d), lambda i: (i, 0))
    lg = [pl.BlockSpec((ts, d), functools.partial(lambda i, c: (i, c), c=c)) for c in range(3)]
    bs = [pl.BlockSpec((1, d), functools.partial(lambda i, c: (0, c), c=c)) for c in range(3)]
    return _pcall(kern, name=name, grid=(s // ts,), in_specs=lg + bs + [row, row, row], out_specs=row,
                  out_shape=jax.ShapeDtypeStruct((s, d), BF16),
                  compiler_params=_params("parallel"))(logits, logits, logits, bias, bias, bias, ya, yb, yc)


def _merge_bwd(logits, bias, ya, yb, yc, dm, *, name):
    s, d = ya.shape
    ts = _pick(s, (256,))

    def kern(l0, l1, l2, b0, b1, b2, a_ref, b_ref, c_ref, dm_ref, da_ref, db_ref, dc_ref, dl0, dl1, dl2, dbias0, dbias1, dbias2):
        first = pl.program_id(0) == 0
        dmv = dm_ref[...]
        for l_ref, bb_ref, y_ref, dy_ref, dl_ref, dbias_ref in ((l0, b0, a_ref, da_ref, dl0, dbias0), (l1, b1, b_ref, db_ref, dl1, dbias1),
                                                                (l2, b2, c_ref, dc_ref, dl2, dbias2)):
            g = _sigmoid(l_ref[...] + bb_ref[...])
            dy_ref[...] = (dmv * g).astype(dy_ref.dtype)
            dl = dmv * y_ref[...] * g * (1.0 - g)
            dl_ref[...] = dl.astype(dl_ref.dtype)

            @pl.when(first)
            def _():
                dbias_ref[...] = jnp.zeros_like(dbias_ref)

            dbias_ref[...] += jnp.sum(dl, axis=0, keepdims=True)

    row = pl.BlockSpec((ts, d), lambda i: (i, 0))
    lg = [pl.BlockSpec((ts, d), functools.partial(lambda i, c: (i, c), c=c)) for c in range(3)]
    bs = [pl.BlockSpec((1, d), functools.partial(lambda i, c: (0, c), c=c)) for c in range(3)]
    vec = pl.BlockSpec((1, d), lambda i: (0, 0))
    yshape = jax.ShapeDtypeStruct((s, d), BF16)
    vshape = jax.ShapeDtypeStruct((1, d), F32)
    outs = _pcall(kern, name=name, grid=(s // ts,), in_specs=lg + bs + [row, row, row, row],
                  out_specs=[row, row, row, row, row, row, vec, vec, vec],
                  out_shape=[yshape] * 6 + [vshape] * 3,
                  compiler_params=_params("arbitrary"))(logits, logits, logits, bias, bias, bias, ya, yb, yc, dm)
    return outs[0], outs[1], outs[2], outs[3:6], jnp.concatenate(outs[6:9], axis=1)


def _heads(t, n):
    s = t.shape[0]
    return t.reshape(s, n, HEAD_DIM).transpose(1, 0, 2)


def _unheads(t):
    n, s, hd = t.shape
    return t.transpose(1, 0, 2).reshape(s, n * hd)


def _to_class_major(t):
    s = t.shape[0]
    w = t.shape[1] // (DSA_HPG * len(DSA_GROUPS))
    parts = []
    for g, (_, r) in enumerate(DSA_GROUPS):
        tg = t[:, g * DSA_HPG * w:(g + 1) * DSA_HPG * w].reshape(s // r, r, DSA_HPG, w)
        parts.append(tg.transpose(2, 1, 0, 3).reshape(DSA_HPG * s, w))
    return jnp.concatenate(parts, axis=0)


def _slot_to_class_major(t):
    hh, s, w = t.shape
    parts = []
    for _, r in DSA_GROUPS:
        parts.append(t.reshape(hh, s // r, r, w).transpose(0, 2, 1, 3).reshape(hh * s, w))
    return jnp.concatenate(parts, axis=0)


def _from_class_major(t):
    rows, w = t.shape
    s = rows // 12
    out = []
    for g, (_, r) in enumerate(DSA_GROUPS):
        tg = t[g * 4 * s:(g + 1) * 4 * s].reshape(DSA_HPG, r, s // r, w)
        out.append(tg.transpose(0, 2, 1, 3).reshape(DSA_HPG, s, w))
    return out


def _pad_lanes(t):
    return jnp.concatenate([t, jnp.zeros(t.shape[:-1] + (LANES - t.shape[-1],), t.dtype)], axis=-1)


def _ffn_fwd(x, norm, w1, w3, w2, tag):
    h = _rms_fwd(x, norm, name=f"{tag}_rms")
    a, b, f = _ffn_up(h, w1, w3, name=f"{tag}_up")
    y = _matmul(f, w2, name=f"{tag}_down", res=x, alpha=0.5, tk=1408)
    return y, (h, a, b, f)


def _ffn_bwd(x, norm, w1, w3, w2, saved, dy, dyb, tag):
    h, a, b, f = saved
    dw2 = _matmul(f, dyb, name=f"{tag}_dw2", ta=True, alpha=0.5, tm=1408, tn=1024, tk=512)
    da, db = _ffn_dact(dyb, w2, a, b, name=f"{tag}_dact")
    dw1 = _matmul(h, da, name=f"{tag}_dw1", ta=True, tm=1024, tn=1408, tk=512)
    dw3 = _matmul(h, db, name=f"{tag}_dw3", ta=True, tm=1024, tn=1408, tk=512)
    dh = _matmul(da, w1, name=f"{tag}_dh", tb=True, tn=1024, tk=1408, pair2=(db, w3))
    dx, dxb, dnorm = _rms_bwd(x, norm, dh, dy, name=f"{tag}_drms")
    return dx, dxb, dnorm, dw1, dw3, dw2


def _local_step(x, mem, w, loss_target):
    s, d = x.shape
    assert s % (QB * 16) == 0
    rope = _rope_tables(s)

    x1, sv1 = _ffn_fwd(x, w['ffn1_norm'], w['ffn1_w1'], w['ffn1_w3'], w['ffn1_w2'], "ffn1")
    h = _rms_fwd(x1, w['mix_norm'], name="mix_rms")
    p = _matmul(h, w['w_in'], name="in_proj", out_dtype=BF16)
    logits = _matmul(h, w['w_gate'], name="gate_proj")
    c_qb, c_kb, c_vb, c_qc = 3 * SB_W, 3 * SB_W + DSA_W, 3 * SB_W + 2 * DSA_W, 3 * SB_W + 3 * DSA_W

    qa = _heads(p[:, :SB_W], SB_HEADS)
    ka, va = _key_blocks(p[:, SB_W:2 * SB_W]), _key_blocks(p[:, 2 * SB_W:3 * SB_W])
    oa = _sb_fwd(qa, ka, va, name="sb_fwd")
    oa_t = _unheads(oa)
    ya = _matmul(oa_t, w['w_branch_sb'], name="sb_out")

    qb_n = _qknorm_fwd(p, c_qb, DSA_W, w['qn_dsa'], rope, name="dsa_qnorm")
    kb_n = _qknorm_fwd(p, c_kb, DSA_W, w['kn_dsa'], rope, name="dsa_knorm")
    qb_c, kb_c = _to_class_major(qb_n), _to_class_major(kb_n)
    vb_c = _pad_lanes(_to_class_major(p[:, c_vb:c_vb + DSA_W]))
    ob_groups = _from_class_major(_dsa_fwd(qb_c, kb_c, vb_c, name="dsa_fwd"))
    comb = _dsa_combine(*ob_groups, name="dsa_combine")
    ob_t = _unheads(comb[:, :, :HEAD_DIM])
    yb = _matmul(ob_t, w['w_branch_dsa'], name="dsa_out")

    memh = _rms_fwd(mem, w['mem_norm'], name="mem_rms")
    kv = _matmul(memh, w['w_mem_kv'], name="mem_kv", out_dtype=BF16)
    km_n = _qknorm_fwd(kv, 0, MEM_W, w['kn_mem'], None, name="mem_knorm")
    qc_n = _qknorm_fwd(p, c_qc, MEM_W, w['qn_mem'], None, name="mem_qnorm")
    qc_h, km_h, vm_h = _heads(qc_n, MEM_HEADS), _heads(km_n, MEM_HEADS), _heads(kv[:, MEM_W:], MEM_HEADS)
    oc = _mem_fwd(qc_h, km_h, vm_h, name="mem_fwd")
    oc_t = _unheads(oc)
    yc = _matmul(oc_t, w['w_branch_mem'], name="mem_out")

    merged = _merge_fwd(logits, w['b_gate'], ya, yb, yc, name="merge")
    x2 = _matmul(merged, w['w_out'], name="out_proj", res=x1)
    x3, sv2 = _ffn_fwd(x2, w['ffn2_norm'], w['ffn2_w1'], w['ffn2_w3'], w['ffn2_w2'], "ffn2")
    dx3, dx3b, loss = _loss_head(x3, loss_target, name="loss")

    g = {}
    dx2, dx2b, g['ffn2_norm'], g['ffn2_w1'], g['ffn2_w3'], g['ffn2_w2'] = _ffn_bwd(
        x2, w['ffn2_norm'], w['ffn2_w1'], w['ffn2_w3'], w['ffn2_w2'], sv2, dx3, dx3b, "ffn2")

    g['w_out'] = _matmul(merged, dx2b, name="d_w_out", ta=True, tk=512)
    dm = _matmul(dx2b, w['w_out'], name="d_merged", tb=True)
    dya, dyb, dyc, dlog, g['b_gate'] = _merge_bwd(logits, w['b_gate'], ya, yb, yc, dm, name="d_merge")
    dlogits = jnp.concatenate(dlog, axis=1)

    g['w_branch_sb'] = _matmul(oa_t, dya, name="d_w_sb", ta=True, tk=512)
    g['w_branch_dsa'] = _matmul(ob_t, dyb, name="d_w_dsa", ta=True, tk=512)
    g['w_branch_mem'] = _matmul(oc_t, dyc, name="d_w_mem", ta=True, tk=512)
    doa = _matmul(dya, w['w_branch_sb'], name="d_oa", tb=True, out_dtype=BF16)
    dob = _matmul(dyb, w['w_branch_dsa'], name="d_ob", tb=True)
    doc = _matmul(dyc, w['w_branch_mem'], name="d_oc", tb=True, out_dtype=BF16)

    dqa, dka, dva = _sb_bwd(qa, ka, va, oa, _heads(doa, SB_HEADS), name="sb_bwd")

    pk = _dsa_bwd_prep(comb, _pad_lanes(_heads(dob, DSA_HPG)), name="dsa_prep")
    dq_c, dk_c, dv_c = _dsa_bwd(qb_c, kb_c, vb_c, _slot_to_class_major(pk), name="dsa_bwd")
    dqb_n = jnp.concatenate([_unheads(t) for t in _from_class_major(dq_c)], axis=1)
    dkb_n = jnp.concatenate([_unheads(t) for t in _from_class_major(dk_c)], axis=1)
    dvb = jnp.concatenate([_unheads(t[:, :, :HEAD_DIM]) for t in _from_class_major(dv_c)], axis=1).astype(BF16)
    dqb, g['qn_dsa'] = _qknorm_bwd(p, c_qb, DSA_W, w['qn_dsa'], rope, dqb_n, name="d_dsa_qnorm")
    dkb, g['kn_dsa'] = _qknorm_bwd(p, c_kb, DSA_W, w['kn_dsa'], rope, dkb_n, name="d_dsa_knorm")

    dqc_h, dkm_h, dvm_h = _mem_bwd(qc_h, km_h, vm_h, _heads(doc, MEM_HEADS), name="mem_bwd")
    dqc, g['qn_mem'] = _qknorm_bwd(p, c_qc, MEM_W, w['qn_mem'], None, _unheads(dqc_h), name="d_mem_qnorm")
    dkm, g['kn_mem'] = _qknorm_bwd(kv, 0, MEM_W, w['kn_mem'], None, _unheads(dkm_h), name="d_mem_knorm")
    dkv = jnp.concatenate([dkm, _unheads(dvm_h).astype(BF16)], axis=1)
    g['w_mem_kv'] = _matmul(memh, dkv, name="d_w_mem_kv", ta=True)
    dmemh = _matmul(dkv, w['w_mem_kv'], name="d_memh", tb=True)
    _, _, g['mem_norm'] = _rms_bwd(mem, w['mem_norm'], dmemh, None, name="d_mem_rms")

    dp = jnp.concatenate([_unheads(dqa).astype(BF16), _from_key_blocks(dka).astype(BF16), _from_key_blocks(dva).astype(BF16),
                          dqb, dkb, dvb, dqc], axis=1)
    g['w_in'] = _matmul(h, dp, name="d_w_in", ta=True, tk=512)
    g['w_gate'] = _matmul(h, dlogits, name="d_w_gate", ta=True, tk=512)
    dh = _matmul(dp, w['w_in'], name="d_h_in", tb=True)
    dh = _matmul(dlogits, w['w_gate'], name="d_h_gate", tb=True, res=dh)
    dx1, dx1b, g['mix_norm'] = _rms_bwd(x1, w['mix_norm'], dh, dx2, name="d_mix_rms")

    dx0, _, g['ffn1_norm'], g['ffn1_w1'], g['ffn1_w3'], g['ffn1_w2'] = _ffn_bwd(
        x, w['ffn1_norm'], w['ffn1_w1'], w['ffn1_w3'], w['ffn1_w2'], sv1, dx1, dx1b, "ffn1")
    return loss, dx0, g


def _pack_rows(d, names):
    return jnp.concatenate([d[n].reshape(-1, LANES) for n in names], axis=0)


def _unpack_rows(t, like, names):
    out, off = {}, 0
    for n in names:
        r = like[n].size // LANES
        out[n] = t[off:off + r].reshape(like[n].shape)
        off += r
    return out


def _unpack_gathered(t, local, names):
    out, off = {}, 0
    for n in names:
        r, c = local[n].shape
        rows = r * c // LANES
        blk = t[:, off:off + rows].reshape(N_DEV, r, c)
        out[n] = blk.reshape(N_DEV * r, c) if SHARD_AXIS[n] == 0 else blk.transpose(1, 0, 2).reshape(r, N_DEV * c)
        off += rows
    return out


def _pack_for_owners(g, local, names):
    parts = []
    for n in names:
        r, c = local[n].shape
        blk = g[n].reshape(N_DEV, r, c) if SHARD_AXIS[n] == 0 else g[n].reshape(r, N_DEV, c).transpose(1, 0, 2)
        parts.append(blk.reshape(N_DEV, r * c // LANES, LANES))
    return jnp.concatenate(parts, axis=1)


def _pack_small(d, names, extra_rows):
    parts = []
    for n in names:
        v = d[n].reshape(-1)
        pad = (-v.size) % LANES
        parts.append(jnp.concatenate([v, jnp.zeros((pad,), v.dtype)]).reshape(-1, LANES))
    t = jnp.concatenate(parts, axis=0)
    return jnp.concatenate([t, jnp.zeros((extra_rows, LANES), t.dtype)], axis=0)


def _unpack_small(t, like, names):
    out, off = {}, 0
    for n in names:
        size = like[n].size
        rows = -(-size // LANES)
        out[n] = t[off:off + rows].reshape(-1)[:size].reshape(like[n].shape)
        off += rows
    return out


def _exchange(src, per_peer, *, name):
    rows = src.shape[-2]

    def body(src_ref, out_ref, send_sems, recv_sems, local_sem):
        x, y, c = lax.axis_index("x"), lax.axis_index("y"), lax.axis_index("c")
        me = 4 * x + 2 * y + c
        mine = pltpu.make_async_copy(src_ref.at[me] if per_peer else src_ref, out_ref.at[me], local_sem)
        mine.start()
        copies = []
        for k in range(1, N_DEV):
            px = 1 - x if k & 4 else x
            py = 1 - y if k & 2 else y
            pc = 1 - c if k & 1 else c
            cp = pltpu.make_async_remote_copy(
                src_ref=src_ref.at[4 * px + 2 * py + pc] if per_peer else src_ref, dst_ref=out_ref.at[me],
                send_sem=send_sems.at[k - 1], recv_sem=recv_sems.at[k - 1],
                device_id=(px, py, pc), device_id_type=pl.DeviceIdType.MESH)
            cp.start()
            copies.append(cp)
        for cp in copies:
            cp.wait_recv()
        for cp in copies:
            cp.wait_send()
        mine.wait()

    anyspace = pl.BlockSpec(memory_space=pl.ANY)
    return _pcall(body, name=name, in_specs=[anyspace], out_specs=anyspace,
                  out_shape=jax.ShapeDtypeStruct((N_DEV, rows, LANES), src.dtype),
                  scratch_shapes=[pltpu.SemaphoreType.DMA((N_DEV - 1,)), pltpu.SemaphoreType.DMA((N_DEV - 1,)),
                                  pltpu.SemaphoreType.DMA])(src)


def _gather_two_level(src, *, name):
    rows = src.shape[0]

    def body(src_ref, out_ref, send_sems, recv_sems, local_sem):
        x, y, c = lax.axis_index("x"), lax.axis_index("y"), lax.axis_index("c")
        me, sibling = (x, y, c), (x, y, 1 - c)
        chips = [(1 - x, y), (x, 1 - y), (1 - x, 1 - y)]

        def slab(px, py, pc):
            return out_ref.at[4 * px + 2 * py + pc]

        def copy(k, block, to, from_src=False):
            return pltpu.make_async_remote_copy(
                src_ref=src_ref if from_src else slab(*block), dst_ref=slab(*block),
                send_sem=send_sems.at[k], recv_sem=recv_sems.at[k], device_id=to, device_id_type=pl.DeviceIdType.MESH)

        mine = pltpu.make_async_copy(src_ref, slab(*me), local_sem)
        mine.start()
        first = [copy(0, me, sibling, True)] + [copy(1 + j, me, (*chip, c), True) for j, chip in enumerate(chips)]
        for cp in first:
            cp.start()
        passed = [copy(4 + j, (*chip, c), sibling) for j, chip in enumerate(chips)]
        for j, chip in enumerate(chips):
            copy(1 + j, (*chip, c), me).wait_recv()
            passed[j].start()
        copy(0, sibling, me).wait_recv()
        for j, chip in enumerate(chips):
            copy(4 + j, (*chip, 1 - c), me).wait_recv()
        for cp in first + passed:
            cp.wait_send()
        mine.wait()

    anyspace = pl.BlockSpec(memory_space=pl.ANY)
    return _pcall(body, name=name, in_specs=[anyspace], out_specs=anyspace,
                  out_shape=jax.ShapeDtypeStruct((N_DEV, rows, LANES), src.dtype),
                  scratch_shapes=[pltpu.SemaphoreType.DMA((N_DEV - 1,)), pltpu.SemaphoreType.DMA((N_DEV - 1,)),
                                  pltpu.SemaphoreType.DMA])(src)


def _adamw(recv, w, m, v, *, name):
    rows = w.shape[0]
    tr = _pick(rows, (512, 256, 128, 64))

    def kern(r_ref, w_ref, m_ref, v_ref, g_ref, d_ref, mo_ref, vo_ref):
        g = r_ref[0].astype(F32)
        for p in range(1, N_DEV):
            g = g + r_ref[p].astype(F32)
        mn = ADAM_B1 * m_ref[...] + (1.0 - ADAM_B1) * g
        vn = ADAM_B2 * v_ref[...] + (1.0 - ADAM_B2) * (g * g)
        m_hat = mn / (1.0 - ADAM_B1 ** ADAM_STEP)
        v_hat = vn / (1.0 - ADAM_B2 ** ADAM_STEP)
        g_ref[...] = g
        d_ref[...] = -ADAM_LR * (m_hat / (jnp.sqrt(v_hat) + ADAM_EPS) + ADAM_WD * w_ref[...])
        mo_ref[...] = mn
        vo_ref[...] = vn

    row = pl.BlockSpec((tr, LANES), lambda i: (i, 0))
    shp = jax.ShapeDtypeStruct((rows, LANES), F32)
    return _pcall(kern, name=name, grid=(rows // tr,), in_specs=[pl.BlockSpec((N_DEV, tr, LANES), lambda i: (0, i, 0)), row, row, row],
                  out_specs=[row, row, row, row], out_shape=[shp, shp, shp, shp], compiler_params=_params("parallel"))(recv, w, m, v)


INPUTS = ['x', 'mem'] + WEIGHTS + ['loss_target'] + ['m_' + n for n in WEIGHTS] + ['v_' + n for n in WEIGHTS]
SMALL_PAD_ROWS = 4


def kernel(x, mem, ffn1_norm, ffn1_w1, ffn1_w3, ffn1_w2, mix_norm, mem_norm, w_in, w_mem_kv, qn_dsa, kn_dsa, qn_mem, kn_mem, w_branch_sb, w_branch_dsa, w_branch_mem, w_gate, b_gate, w_out, ffn2_norm, ffn2_w1, ffn2_w3, ffn2_w2, loss_target, m_ffn1_norm, m_ffn1_w1, m_ffn1_w3, m_ffn1_w2, m_mix_norm, m_mem_norm, m_w_in, m_w_mem_kv, m_qn_dsa, m_kn_dsa, m_qn_mem, m_kn_mem, m_w_branch_sb, m_w_branch_dsa, m_w_branch_mem, m_w_gate, m_b_gate, m_w_out, m_ffn2_norm, m_ffn2_w1, m_ffn2_w3, m_ffn2_w2, v_ffn1_norm, v_ffn1_w1, v_ffn1_w3, v_ffn1_w2, v_mix_norm, v_mem_norm, v_w_in, v_w_mem_kv, v_qn_dsa, v_kn_dsa, v_qn_mem, v_kn_mem, v_w_branch_sb, v_w_branch_dsa, v_w_branch_mem, v_w_gate, v_b_gate, v_w_out, v_ffn2_norm, v_ffn2_w1, v_ffn2_w3, v_ffn2_w2):
    given = dict(zip(INPUTS, (x, mem, ffn1_norm, ffn1_w1, ffn1_w3, ffn1_w2, mix_norm, mem_norm, w_in, w_mem_kv, qn_dsa, kn_dsa, qn_mem, kn_mem, w_branch_sb, w_branch_dsa, w_branch_mem, w_gate, b_gate, w_out, ffn2_norm, ffn2_w1, ffn2_w3, ffn2_w2, loss_target, m_ffn1_norm, m_ffn1_w1, m_ffn1_w3, m_ffn1_w2, m_mix_norm, m_mem_norm, m_w_in, m_w_mem_kv, m_qn_dsa, m_kn_dsa, m_qn_mem, m_kn_mem, m_w_branch_sb, m_w_branch_dsa, m_w_branch_mem, m_w_gate, m_b_gate, m_w_out, m_ffn2_norm, m_ffn2_w1, m_ffn2_w3, m_ffn2_w2, v_ffn1_norm, v_ffn1_w1, v_ffn1_w3, v_ffn1_w2, v_mix_norm, v_mem_norm, v_w_in, v_w_mem_kv, v_qn_dsa, v_kn_dsa, v_qn_mem, v_kn_mem, v_w_branch_sb, v_w_branch_dsa, v_w_branch_mem, v_w_gate, v_b_gate, v_w_out, v_ffn2_norm, v_ffn2_w1, v_ffn2_w3, v_ffn2_w2), strict=True))
    wl = {n: given[n][0] for n in BIG}
    ws = {n: given[n] for n in SMALL}

    gathered = _gather_two_level(_pack_rows({n: wl[n].astype(BF16) for n in BIG}, BIG), name="gather_weights")
    whole = _unpack_gathered(gathered, wl, BIG)
    loss, dx, g = _local_step(x[0], mem[0], {**whole, **ws}, loss_target[0])

    recv = _exchange(_pack_for_owners(g, wl, BIG).astype(BF16), True, name="scatter_grads")
    big = _adamw(recv, _pack_rows(wl, BIG), _pack_rows({n: given['m_' + n][0] for n in BIG}, BIG),
                 _pack_rows({n: given['v_' + n][0] for n in BIG}, BIG), name="adamw_sharded")
    big = [_unpack_rows(t, wl, BIG) for t in big]

    gs = _pack_small(g, SMALL, SMALL_PAD_ROWS)
    loss_row = gs.shape[0] - SMALL_PAD_ROWS
    gs = gs.at[loss_row, 0].set(loss[0, 0])
    recv_s = _exchange(gs, False, name="gather_small")
    small = _adamw(recv_s, _pack_small(ws, SMALL, SMALL_PAD_ROWS), _pack_small({n: given['m_' + n] for n in SMALL}, SMALL, SMALL_PAD_ROWS),
                   _pack_small({n: given['v_' + n] for n in SMALL}, SMALL, SMALL_PAD_ROWS), name="adamw_replicated")
    total_loss = small[0][loss_row, 0]
    small = [_unpack_small(t, ws, SMALL) for t in small]

    outs = [total_loss, dx[None]]
    for kind in range(4):
        outs += [big[kind][n][None] if n in wl else small[kind][n] for n in WEIGHTS]
    return tuple(outs)
```

```python
import functools
import math

import jax
import jax.numpy as jnp
from jax import lax
from jax.experimental import pallas as pl
from jax.experimental.pallas import tpu as pltpu

F32 = jnp.float32
BF16 = jnp.bfloat16
MXU_DT = jnp.bfloat16

N_DEV = 8
HEAD_DIM = 64
SB_HEADS = 8
DSA_GROUPS = ((128, 1), (512, 4), (2048, 16))
DSA_HPG = 4
MEM_HEADS = 4
SB_W = SB_HEADS * HEAD_DIM
DSA_W = DSA_HPG * len(DSA_GROUPS) * HEAD_DIM
DSA_OUT_W = DSA_HPG * HEAD_DIM
MEM_W = MEM_HEADS * HEAD_DIM
ROPE_THETA = 10000.0
NORM_EPS = 1e-6
QB = 128
SCALE = HEAD_DIM ** -0.5
ADAM_LR, ADAM_B1, ADAM_B2, ADAM_EPS, ADAM_WD, ADAM_STEP = 0.001, 0.9, 0.999, 1e-08, 0.01, 10

LANES = 128
VMEM_LIMIT = 48 * 1024 * 1024
SB_DEAD = -110.0 * 1.4426950408889634

WEIGHTS = ['ffn1_norm', 'ffn1_w1', 'ffn1_w3', 'ffn1_w2', 'mix_norm', 'mem_norm', 'w_in', 'w_mem_kv', 'qn_dsa', 'kn_dsa',
           'qn_mem', 'kn_mem', 'w_branch_sb', 'w_branch_dsa', 'w_branch_mem', 'w_gate', 'b_gate', 'w_out', 'ffn2_norm',
           'ffn2_w1', 'ffn2_w3', 'ffn2_w2']
SHARD_AXIS = {'ffn1_norm': None, 'ffn1_w1': 1, 'ffn1_w3': 1, 'ffn1_w2': 0, 'mix_norm': None, 'mem_norm': None, 'w_in': 1,
              'w_mem_kv': 0, 'qn_dsa': None, 'kn_dsa': None, 'qn_mem': None, 'kn_mem': None, 'w_branch_sb': 1,
              'w_branch_dsa': 1, 'w_branch_mem': 1, 'w_gate': 1, 'b_gate': None, 'w_out': 0, 'ffn2_norm': None,
              'ffn2_w1': 1, 'ffn2_w3': 1, 'ffn2_w2': 0}
BIG = [n for n in WEIGHTS if SHARD_AXIS[n] is not None]
SMALL = [n for n in WEIGHTS if SHARD_AXIS[n] is None]


def _pcall(kern, **kw):
    return pl.pallas_call(kern, **kw)


def _params(*sem):
    return pltpu.CompilerParams(dimension_semantics=sem, vmem_limit_bytes=VMEM_LIMIT)


def _dot(a, b, dims):
    return lax.dot_general(a.astype(MXU_DT), b.astype(MXU_DT), (dims, ((), ())), preferred_element_type=F32)


def _nn(a, b):
    return _dot(a, b, ((1,), (0,)))


def _nt(a, b):
    return _dot(a, b, ((1,), (1,)))


def _tn(a, b):
    return _dot(a, b, ((0,), (0,)))


def _pick(n, prefs):
    for p in prefs:
        if n % p == 0:
            return p
    return n


def _matmul(a, b, *, name, ta=False, tb=False, out_dtype=F32, res=None, alpha=1.0, tm=1024, tn=512, tk=1024, pair2=None):
    if ta:
        kdim, m = a.shape
    else:
        m, kdim = a.shape
    n = b.shape[0] if tb else b.shape[1]
    tm = _pick(m, (tm, 512, 256, 128))
    tn = _pick(n, (tn, 512, 384, 256, 128))
    tk = _pick(kdim, (tk, 1024, 512, 256, 128))
    nk = kdim // tk
    a_spec = pl.BlockSpec((tk, tm), lambda i, j, k: (k, i)) if ta else pl.BlockSpec((tm, tk), lambda i, j, k: (i, k))
    b_spec = pl.BlockSpec((tn, tk), lambda i, j, k: (j, k)) if tb else pl.BlockSpec((tk, tn), lambda i, j, k: (k, j))
    o_spec = pl.BlockSpec((tm, tn), lambda i, j, k: (i, j))
    dims = ((0 if ta else 1,), (1 if tb else 0,))

    def kern(*refs):
        refs = list(refs)
        acc_ref = refs.pop()
        o_ref = refs.pop()
        r_ref = refs.pop() if res is not None else None
        k = pl.program_id(2)

        @pl.when(k == 0)
        def _():
            acc_ref[...] = jnp.zeros_like(acc_ref)

        part = _dot(refs[0][...], refs[1][...], dims)
        if pair2 is not None:
            part = part + _dot(refs[2][...], refs[3][...], dims)
        acc_ref[...] += part

        @pl.when(k == nk - 1)
        def _():
            r = acc_ref[...]
            if alpha != 1.0:
                r = r * alpha
            if r_ref is not None:
                r = r_ref[...] + r
            o_ref[...] = r.astype(out_dtype)

    ins = [a, b] + ([] if pair2 is None else list(pair2)) + ([] if res is None else [res])
    specs = [a_spec, b_spec] + ([] if pair2 is None else [a_spec, b_spec]) + ([] if res is None else [o_spec])
    return _pcall(kern, name=name, grid=(m // tm, n // tn, nk), in_specs=specs, out_specs=o_spec,
                  out_shape=jax.ShapeDtypeStruct((m, n), out_dtype), scratch_shapes=[pltpu.VMEM((tm, tn), F32)],
                  compiler_params=_params("parallel", "parallel", "arbitrary"))(*ins)


def _rms_fwd(x, g, *, name):
    s, d = x.shape
    ts = _pick(s, (512, 256))

    def kern(x_ref, g_ref, h_ref):
        xf = x_ref[...]
        r = lax.rsqrt(jnp.mean(xf * xf, axis=-1, keepdims=True) + NORM_EPS)
        h_ref[...] = (xf * r * g_ref[...]).astype(h_ref.dtype)

    return _pcall(kern, name=name, grid=(s // ts,),
                  in_specs=[pl.BlockSpec((ts, d), lambda i: (i, 0)), pl.BlockSpec((1, d), lambda i: (0, 0))],
                  out_specs=pl.BlockSpec((ts, d), lambda i: (i, 0)), out_shape=jax.ShapeDtypeStruct((s, d), BF16),
                  compiler_params=_params("parallel"))(x, g)


def _rms_bwd(x, g, dh, res, *, name):
    s, d = x.shape
    ts = _pick(s, (512, 256))

    def kern(*refs):
        if res is None:
            x_ref, g_ref, dh_ref, dx_ref, dxb_ref, dg_ref = refs
            r_ref = None
        else:
            x_ref, g_ref, dh_ref, r_ref, dx_ref, dxb_ref, dg_ref = refs
        xf = x_ref[...]
        r = lax.rsqrt(jnp.mean(xf * xf, axis=-1, keepdims=True) + NORM_EPS)
        xh = xf * r
        dhf = dh_ref[...].astype(F32)
        dy = dhf * g_ref[...]
        dx = r * (dy - xh * jnp.mean(dy * xh, axis=-1, keepdims=True))
        if r_ref is not None:
            dx = r_ref[...] + dx
        dx_ref[...] = dx
        dxb_ref[...] = dx.astype(dxb_ref.dtype)

        @pl.when(pl.program_id(0) == 0)
        def _():
            dg_ref[...] = jnp.zeros_like(dg_ref)

        dg_ref[...] += jnp.sum(dhf * xh, axis=0, keepdims=True)

    row = pl.BlockSpec((ts, d), lambda i: (i, 0))
    vec = pl.BlockSpec((1, d), lambda i: (0, 0))
    ins = [x, g, dh] + ([] if res is None else [res])
    return _pcall(kern, name=name, grid=(s // ts,), in_specs=[row, vec, row] + ([] if res is None else [row]),
                  out_specs=[row, row, vec],
                  out_shape=[jax.ShapeDtypeStruct((s, d), F32), jax.ShapeDtypeStruct((s, d), BF16), jax.ShapeDtypeStruct((1, d), F32)],
                  compiler_params=_params("arbitrary"))(*ins)


def _sigmoid(x):
    return 1.0 / (1.0 + jnp.exp(-x))


FFN_TM, FFN_TF = 512, 1408


def _ffn_up(h, w1, w3, *, name):
    s, d = h.shape
    fdim = w1.shape[1]
    tm, tf = _pick(s, (FFN_TM, 256)), _pick(fdim, (FFN_TF, 512, 256, 128))

    def kern(h_ref, w1_ref, w3_ref, a_ref, b_ref, f_ref):
        hb = h_ref[...]
        a = _nn(hb, w1_ref[...])
        b = _nn(hb, w3_ref[...])
        a_ref[...] = a.astype(a_ref.dtype)
        b_ref[...] = b.astype(b_ref.dtype)
        f_ref[...] = (a * _sigmoid(a) * b).astype(f_ref.dtype)

    wspec = pl.BlockSpec((d, tf), lambda i, j: (0, j))
    ospec = pl.BlockSpec((tm, tf), lambda i, j: (i, j))
    shp = jax.ShapeDtypeStruct((s, fdim), BF16)
    return _pcall(kern, name=name, grid=(s // tm, fdim // tf), in_specs=[pl.BlockSpec((tm, d), lambda i, j: (i, 0)), wspec, wspec],
                  out_specs=[ospec, ospec, ospec], out_shape=[shp, shp, shp],
                  compiler_params=_params("parallel", "parallel"))(h, w1, w3)


def _ffn_dact(dy, w2, a, b, *, name):
    s, d = dy.shape
    fdim = w2.shape[0]
    tm, tf = _pick(s, (FFN_TM, 256)), _pick(fdim, (FFN_TF, 512, 256, 128))

    def kern(dy_ref, w2_ref, a_ref, b_ref, da_ref, db_ref):
        df = _nt(dy_ref[...], w2_ref[...]) * 0.5
        av = a_ref[...].astype(F32)
        sg = _sigmoid(av)
        da_ref[...] = (df * b_ref[...].astype(F32) * (sg + av * sg * (1.0 - sg))).astype(da_ref.dtype)
        db_ref[...] = (df * (av * sg)).astype(db_ref.dtype)

    ospec = pl.BlockSpec((tm, tf), lambda i, j: (i, j))
    shp = jax.ShapeDtypeStruct((s, fdim), BF16)
    return _pcall(kern, name=name, grid=(s // tm, fdim // tf),
                  in_specs=[pl.BlockSpec((tm, d), lambda i, j: (i, 0)), pl.BlockSpec((tf, d), lambda i, j: (j, 0)), ospec, ospec],
                  out_specs=[ospec, ospec], out_shape=[shp, shp], compiler_params=_params("parallel", "parallel"))(dy, w2, a, b)


def _loss_head(y, t, *, name):
    s, d = y.shape
    ts = _pick(s, (512, 256))
    n = s // ts

    def kern(y_ref, t_ref, dy_ref, dyb_ref, l_ref, acc_ref):
        i = pl.program_id(0)

        @pl.when(i == 0)
        def _():
            acc_ref[...] = jnp.zeros_like(acc_ref)

        e = y_ref[...] - t_ref[...]
        dy_ref[...] = e / d
        dyb_ref[...] = (e / d).astype(dyb_ref.dtype)
        acc_ref[...] += jnp.sum(e * e, axis=0, keepdims=True)

        @pl.when(i == n - 1)
        def _():
            l_ref[...] = jnp.sum(acc_ref[...], axis=1, keepdims=True) * (0.5 / d)

    row = pl.BlockSpec((ts, d), lambda i: (i, 0))
    return _pcall(kern, name=name, grid=(n,), in_specs=[row, row], out_specs=[row, row, pl.BlockSpec((1, 1), lambda i: (0, 0))],
                  out_shape=[jax.ShapeDtypeStruct((s, d), F32), jax.ShapeDtypeStruct((s, d), BF16), jax.ShapeDtypeStruct((1, 1), F32)],
                  scratch_shapes=[pltpu.VMEM((1, d), F32)], compiler_params=_params("arbitrary"))(y, t)


def _head_mean(v, bd):
    hi = v.astype(BF16)
    lo = (v - hi.astype(F32)).astype(BF16)
    return (lax.dot_general(hi, bd, (((1,), (0,)), ((), ())), preferred_element_type=F32)
            + lax.dot_general(lo, bd, (((1,), (0,)), ((), ())), preferred_element_type=F32))


def _partner(v):
    w = v.shape[1]
    lane = lax.broadcasted_iota(jnp.int32, v.shape, 1)
    return jnp.where(lane % HEAD_DIM < HEAD_DIM // 2, pltpu.roll(v, w - HEAD_DIM // 2, 1), pltpu.roll(v, HEAD_DIM // 2, 1))


def _block_diag(w):
    r = lax.broadcasted_iota(jnp.int32, (w, w), 0) // HEAD_DIM
    c = lax.broadcasted_iota(jnp.int32, (w, w), 1) // HEAD_DIM
    return jnp.where(r == c, 1.0 / HEAD_DIM, 0.0).astype(BF16)


def _rope_tables(s):
    half = HEAD_DIM // 2
    inv_freq = jnp.power(ROPE_THETA, -jnp.arange(half, dtype=F32) / half)
    ang = jnp.arange(s).astype(F32)[:, None] * inv_freq[None, :]
    cos, sin = jnp.cos(ang), jnp.sin(ang)
    cos2 = jnp.concatenate([cos, cos, cos, cos], axis=1)
    sin2 = jnp.concatenate([-sin, sin, -sin, sin], axis=1)
    return cos2, sin2


def _qknorm_fwd(src, col0, width, gain, rope, *, name, out_dtype=BF16):
    s = src.shape[0]
    ts = _pick(s, (512, 256))
    cb = col0 // width
    assert col0 % width == 0
    reps = width // LANES
    g = jnp.tile(gain, (1, width // HEAD_DIM))

    def kern(*refs):
        if rope is None:
            x_ref, g_ref, o_ref = refs
        else:
            x_ref, g_ref, c_ref, s_ref, o_ref = refs
        x = x_ref[...].astype(F32)
        bd = _block_diag(width)
        r = lax.rsqrt(_head_mean(x * x, bd) + NORM_EPS)
        y = x * r * g_ref[...]
        if rope is not None:
            y = y * jnp.tile(c_ref[...], (1, reps)) + _partner(y) * jnp.tile(s_ref[...], (1, reps))
        o_ref[...] = y.astype(o_ref.dtype)

    xs = pl.BlockSpec((ts, width), lambda i: (i, cb))
    tab = pl.BlockSpec((ts, LANES), lambda i: (i, 0))
    ins = [src, g] + ([] if rope is None else list(rope))
    specs = [xs, pl.BlockSpec((1, width), lambda i: (0, 0))] + ([] if rope is None else [tab, tab])
    return _pcall(kern, name=name, grid=(s // ts,), in_specs=specs, out_specs=pl.BlockSpec((ts, width), lambda i: (i, 0)),
                  out_shape=jax.ShapeDtypeStruct((s, width), out_dtype), compiler_params=_params("parallel"))(*ins)


def _qknorm_bwd(src, col0, width, gain, rope, dout, *, name):
    s = src.shape[0]
    ts = _pick(s, (512, 256))
    cb = col0 // width
    reps = width // LANES
    g = jnp.tile(gain, (1, width // HEAD_DIM))

    douts = list(dout) if isinstance(dout, (list, tuple)) else [dout]
    piece = width // len(douts)

    def kern(*refs):
        refs = list(refs)
        dg_ref = refs.pop()
        dx_ref = refs.pop()
        do_refs = [refs.pop() for _ in douts][::-1]
        if rope is None:
            x_ref, g_ref = refs
        else:
            x_ref, g_ref, c_ref, s_ref = refs
        x = x_ref[...].astype(F32)
        bd = _block_diag(width)
        r = lax.rsqrt(_head_mean(x * x, bd) + NORM_EPS)
        xh = x * r
        dy = jnp.concatenate([d[...].astype(F32) for d in do_refs], axis=1) if len(do_refs) > 1 else do_refs[0][...].astype(F32)
        if rope is not None:
            dy = dy * jnp.tile(c_ref[...], (1, reps)) + _partner(dy * jnp.tile(s_ref[...], (1, reps)))
        dxh = dy * g_ref[...]
        dx_ref[...] = (r * (dxh - xh * _head_mean(dxh * xh, bd))).astype(dx_ref.dtype)

        @pl.when(pl.program_id(0) == 0)
        def _():
            dg_ref[...] = jnp.zeros_like(dg_ref)

        dg_ref[...] += jnp.sum(dy * xh, axis=0, keepdims=True)

    xs = pl.BlockSpec((ts, width), lambda i: (i, cb))
    row = pl.BlockSpec((ts, width), lambda i: (i, 0))
    vec = pl.BlockSpec((1, width), lambda i: (0, 0))
    tab = pl.BlockSpec((ts, LANES), lambda i: (i, 0))
    ins = [src, g] + ([] if rope is None else list(rope)) + douts
    specs = [xs, vec] + ([] if rope is None else [tab, tab]) + [pl.BlockSpec((ts, piece), lambda i: (i, 0))] * len(douts)
    dx, dg = _pcall(kern, name=name, grid=(s // ts,), in_specs=specs, out_specs=[row, vec],
                    out_shape=[jax.ShapeDtypeStruct((s, width), BF16), jax.ShapeDtypeStruct((1, width), F32)],
                    compiler_params=_params("arbitrary"))(*ins)
    return dx, jnp.sum(dg.reshape(width // HEAD_DIM, HEAD_DIM), axis=0, keepdims=True)


def _tri(strict):
    r = lax.broadcasted_iota(jnp.int32, (2 * QB, QB), 0) % QB
    c = lax.broadcasted_iota(jnp.int32, (2 * QB, QB), 1)
    return jnp.where((r > c) if strict else (r >= c), 1.0, 0.0).astype(BF16)


def _split_dot(v, t2):
    hi = v.astype(BF16)
    lo = (v - hi.astype(F32)).astype(BF16)
    return lax.dot_general(jnp.concatenate([hi, lo], axis=1), t2, (((1,), (0,)), ((), ())), preferred_element_type=F32)


LOG2E = 1.4426950408889634


def _log2_sigmoids(z2):
    lf = -(jnp.maximum(z2, 0.0) + jnp.log2(1.0 + jnp.exp2(-jnp.abs(z2))))
    return z2 + lf, lf


def _key_blocks(t):
    s = t.shape[0]
    n = t.shape[1] // HEAD_DIM
    return t.reshape(s // QB, QB, n, HEAD_DIM).transpose(2, 0, 3, 1)


def _from_key_blocks(t):
    n, nb, hd, qb = t.shape
    return t.transpose(1, 3, 0, 2).reshape(nb * qb, n * hd)


SB_SUB = 4
SB2_SUB = 2


def _first_half(shape):
    return lax.broadcasted_iota(jnp.int32, shape, 1) < HEAD_DIM


def _split_pair(t, first):
    zero = jnp.zeros_like(t)
    return [jnp.where(first, t, zero), jnp.where(first, zero, t)]


def _sb2_fwd(p, *, name):
    s = p.shape[0]
    rq = SB2_SUB * QB
    nq = s // rq
    npair = SB_W // LANES

    def kern(q_ref, k_ref, v_ref, o_ref):
        i = pl.program_id(1)
        first = _first_half((rq, LANES))
        qs = _split_pair(q_ref[...], first)
        t2 = _tri(True)
        rel = lax.broadcasted_iota(jnp.int32, (rq, QB), 1) - lax.broadcasted_iota(jnp.int32, (rq, QB), 0)

        def tile(j, carries, accs, masked):
            off = pl.multiple_of(j * QB, QB)
            kt = k_ref[pl.ds(off, QB), :]
            vt = v_ref[pl.ds(off, QB), :]
            out_c, out_a = [], []
            for e in range(2):
                ls, lf = _log2_sigmoids(_nt(qs[e], kt) * (SCALE * LOG2E))
                if masked:
                    before = rel < i * rq - j * QB
                    lf = jnp.where(before, lf, 0.0)
                w = jnp.exp2(ls + _split_dot(lf, t2) + carries[e])
                if masked:
                    w = jnp.where(before, w, 0.0)
                out_c.append(carries[e] + jnp.sum(lf, axis=1, keepdims=True))
                out_a.append(accs[e] + _nn(w, vt))
            return out_c, out_a

        carries = [jnp.zeros((rq, 1), F32)] * 2
        accs = [jnp.zeros((rq, LANES), F32)] * 2
        for a in range(SB2_SUB):
            carries, accs = tile(i * SB2_SUB + (SB2_SUB - 1 - a), carries, accs, True)

        def cond(st):
            return jnp.logical_and(st[0] >= 0, st[1] > 0)

        def body(st):
            carries, accs = tile(st[0], [st[2], st[3]], [st[4], st[5]], False)
            alive = jnp.maximum(jnp.max(carries[0]), jnp.max(carries[1])) > SB_DEAD
            return st[0] - 1, alive.astype(jnp.int32), carries[0], carries[1], accs[0], accs[1]

        st = lax.while_loop(cond, body, (i * SB2_SUB - 1, jnp.int32(1), carries[0], carries[1], accs[0], accs[1]))
        o_ref[...] = jnp.where(first, st[4], st[5])

    return _pcall(kern, name=name, grid=(npair, nq),
                  in_specs=[pl.BlockSpec((rq, LANES), lambda a, i: (i, a)), pl.BlockSpec((s, LANES), lambda a, i: (0, npair + a)),
                            pl.BlockSpec((s, LANES), lambda a, i: (0, 2 * npair + a))],
                  out_specs=pl.BlockSpec((rq, LANES), lambda a, i: (i, a)), out_shape=jax.ShapeDtypeStruct((s, SB_W), F32),
                  compiler_params=_params("parallel", "arbitrary"))(p, p, p)


def _sb2_bwd(p, o, do, *, name):
    s = p.shape[0]
    rq = SB2_SUB * QB
    nq = s // rq
    npair = SB_W // LANES

    def kern(q_ref, k_ref, v_ref, o_ref, do_ref, dq_ref, dk_hbm, dv_hbm, dk_acc, dv_acc, sem):
        pr = pl.program_id(0)
        i = pl.program_id(1)

        @pl.when(i == 0)
        def _():
            dk_acc[...] = jnp.zeros_like(dk_acc)
            dv_acc[...] = jnp.zeros_like(dv_acc)

        first = _first_half((rq, LANES))
        qs = _split_pair(q_ref[...], first)
        do2 = do_ref[...]
        dos = _split_pair(do2, first)
        prod = do2.astype(F32) * o_ref[...]
        dsums = [jnp.sum(jnp.where(first, prod, 0.0), axis=1, keepdims=True),
                 jnp.sum(jnp.where(first, 0.0, prod), axis=1, keepdims=True)]
        t_strict = _tri(True)
        t_incl = _tri(False)
        rel = lax.broadcasted_iota(jnp.int32, (rq, QB), 1) - lax.broadcasted_iota(jnp.int32, (rq, QB), 0)

        def tile(j, carries, gcarries, dqs, masked):
            off = pl.multiple_of(j * QB, QB)
            kt = k_ref[pl.ds(off, QB), :]
            vt = v_ref[pl.ds(off, QB), :]
            out_c, out_g, out_q = [], [], []
            dk_t = jnp.zeros((QB, LANES), F32)
            dv_t = jnp.zeros((QB, LANES), F32)
            for e in range(2):
                ls, lf = _log2_sigmoids(_nt(qs[e], kt) * (SCALE * LOG2E))
                if masked:
                    before = rel < i * rq - j * QB
                    lf = jnp.where(before, lf, 0.0)
                w = jnp.exp2(ls + _split_dot(lf, t_strict) + carries[e])
                if masked:
                    w = jnp.where(before, w, 0.0)
                wr = w.astype(MXU_DT)
                g = _nt(dos[e], vt) * wr.astype(F32)
                big_g = dsums[e] - (_split_dot(g, t_incl) + gcarries[e])
                sig = jnp.exp2(ls)
                dz = g * (1.0 - sig) - sig * big_g
                if masked:
                    dz = jnp.where(before, dz, 0.0)
                dz = dz * SCALE
                dk_t = dk_t + _tn(dz, qs[e])
                dv_t = dv_t + _tn(wr, dos[e])
                out_c.append(carries[e] + jnp.sum(lf, axis=1, keepdims=True))
                out_g.append(gcarries[e] + jnp.sum(g, axis=1, keepdims=True))
                out_q.append(dqs[e] + _nn(dz, kt))
            dk_acc[pl.ds(off, QB), :] += dk_t
            dv_acc[pl.ds(off, QB), :] += dv_t
            return out_c, out_g, out_q

        zc = [jnp.zeros((rq, 1), F32)] * 2
        carries, gcarries, dqs = zc, zc, [jnp.zeros((rq, LANES), F32)] * 2
        for a in range(SB2_SUB):
            carries, gcarries, dqs = tile(i * SB2_SUB + (SB2_SUB - 1 - a), carries, gcarries, dqs, True)

        def cond(st):
            return jnp.logical_and(st[0] >= 0, st[1] > 0)

        def body(st):
            carries, gcarries, dqs = tile(st[0], [st[2], st[3]], [st[4], st[5]], [st[6], st[7]], False)
            alive = jnp.maximum(jnp.max(carries[0]), jnp.max(carries[1])) > SB_DEAD
            return (st[0] - 1, alive.astype(jnp.int32), carries[0], carries[1], gcarries[0], gcarries[1], dqs[0], dqs[1])

        st = lax.while_loop(cond, body, (i * SB2_SUB - 1, jnp.int32(1), carries[0], carries[1], gcarries[0], gcarries[1],
                                         dqs[0], dqs[1]))
        dq_ref[...] = jnp.where(first, st[6], st[7])

        @pl.when(i == nq - 1)
        def _():
            cols = pl.ds(pl.multiple_of(pr * LANES, LANES), LANES)
            ck = pltpu.make_async_copy(dk_acc, dk_hbm.at[:, cols], sem.at[0])
            cv = pltpu.make_async_copy(dv_acc, dv_hbm.at[:, cols], sem.at[1])
            ck.start()
            cv.start()
            ck.wait()
            cv.wait()

    blk = pl.BlockSpec((rq, LANES), lambda a, i: (i, a))
    anyspace = pl.BlockSpec(memory_space=pl.ANY)
    shp = jax.ShapeDtypeStruct((s, SB_W), F32)
    return _pcall(kern, name=name, grid=(npair, nq),
                  in_specs=[blk, pl.BlockSpec((s, LANES), lambda a, i: (0, npair + a)),
                            pl.BlockSpec((s, LANES), lambda a, i: (0, 2 * npair + a)), blk, blk],
                  out_specs=[blk, anyspace, anyspace], out_shape=[shp, shp, shp],
                  scratch_shapes=[pltpu.VMEM((s, LANES), F32), pltpu.VMEM((s, LANES), F32), pltpu.SemaphoreType.DMA((2,))],
                  compiler_params=_params("arbitrary", "arbitrary"))(p, p, p, o, do)


def _sb_fwd(q, kt, vt, *, name):
    h, s, hd = q.shape
    rq = SB_SUB * QB
    nq = s // rq
    nb = s // QB

    def kern(q_ref, k_ref, v_ref, o_ref):
        i = pl.program_id(1)
        qb = q_ref[...]
        t2 = _tri(True)
        rel = lax.broadcasted_iota(jnp.int32, (rq, QB), 1) - lax.broadcasted_iota(jnp.int32, (rq, QB), 0)

        def tile(j, carry, acc, masked):
            ls, lf = _log2_sigmoids(_nn(qb, k_ref[j]) * (SCALE * LOG2E))
            if masked:
                before = rel < i * rq - j * QB
                lf = jnp.where(before, lf, 0.0)
            w = jnp.exp2(ls + _split_dot(lf, t2) + carry)
            if masked:
                w = jnp.where(before, w, 0.0)
            return carry + jnp.sum(lf, axis=1, keepdims=True), acc + _nt(w, v_ref[j])

        carry, acc = jnp.zeros((rq, 1), F32), jnp.zeros((rq, hd), F32)
        for a in range(SB_SUB):
            carry, acc = tile(i * SB_SUB + (SB_SUB - 1 - a), carry, acc, True)

        def cond(st):
            return jnp.logical_and(st[0] >= 0, st[1] > 0)

        def body(st):
            j, _, carry, acc = st
            carry, acc = tile(j, carry, acc, False)
            return j - 1, (jnp.max(carry) > SB_DEAD).astype(jnp.int32), carry, acc

        _, _, _, acc = lax.while_loop(cond, body, (i * SB_SUB - 1, jnp.int32(1), carry, acc))
        o_ref[...] = acc

    blk = pl.BlockSpec((None, rq, hd), lambda a, i: (a, i, 0))
    full = pl.BlockSpec((None, nb, hd, QB), lambda a, i: (a, 0, 0, 0))
    return _pcall(kern, name=name, grid=(h, nq), in_specs=[blk, full, full], out_specs=blk,
                  out_shape=jax.ShapeDtypeStruct((h, s, hd), F32), compiler_params=_params("parallel", "arbitrary"))(q, kt, vt)


def _sb_bwd(q, kt, vt, o, do, *, name):
    h, s, hd = q.shape
    rq = SB_SUB * QB
    nq = s // rq
    nb = s // QB

    def kern(q_ref, k_ref, v_ref, o_ref, do_ref, dq_ref, dk_ref, dv_ref):
        i = pl.program_id(1)

        @pl.when(i == 0)
        def _():
            dk_ref[...] = jnp.zeros_like(dk_ref)
            dv_ref[...] = jnp.zeros_like(dv_ref)

        qb = q_ref[...]
        dob = do_ref[...]
        dsum = jnp.sum(dob.astype(F32) * o_ref[...], axis=1, keepdims=True)
        t_strict = _tri(True)
        t_incl = _tri(False)
        rel = lax.broadcasted_iota(jnp.int32, (rq, QB), 1) - lax.broadcasted_iota(jnp.int32, (rq, QB), 0)

        def tile(j, carry, gcarry, dq, masked):
            kb = k_ref[j]
            ls, lf = _log2_sigmoids(_nn(qb, kb) * (SCALE * LOG2E))
            if masked:
                before = rel < i * rq - j * QB
                lf = jnp.where(before, lf, 0.0)
            w = jnp.exp2(ls + _split_dot(lf, t_strict) + carry)
            if masked:
                w = jnp.where(before, w, 0.0)
            wr = w.astype(MXU_DT)
            g = _nn(dob, v_ref[j]) * wr.astype(F32)
            big_g = dsum - (_split_dot(g, t_incl) + gcarry)
            sig = jnp.exp2(ls)
            dz = g * (1.0 - sig) - sig * big_g
            if masked:
                dz = jnp.where(before, dz, 0.0)
            dz = dz * SCALE
            dk_ref[j] += _tn(qb, dz)
            dv_ref[j] += _tn(dob, wr)
            return (carry + jnp.sum(lf, axis=1, keepdims=True), gcarry + jnp.sum(g, axis=1, keepdims=True),
                    dq + _nt(dz, kb))

        carry, gcarry, dq = jnp.zeros((rq, 1), F32), jnp.zeros((rq, 1), F32), jnp.zeros((rq, hd), F32)
        for a in range(SB_SUB):
            carry, gcarry, dq = tile(i * SB_SUB + (SB_SUB - 1 - a), carry, gcarry, dq, True)

        def cond(st):
            return jnp.logical_and(st[0] >= 0, st[1] > 0)

        def body(st):
            j, _, carry, gcarry, dq = st
            carry, gcarry, dq = tile(j, carry, gcarry, dq, False)
            return j - 1, (jnp.max(carry) > SB_DEAD).astype(jnp.int32), carry, gcarry, dq

        st = lax.while_loop(cond, body, (i * SB_SUB - 1, jnp.int32(1), carry, gcarry, dq))
        dq_ref[...] = st[4]

    blk = pl.BlockSpec((None, rq, hd), lambda a, i: (a, i, 0))
    full = pl.BlockSpec((None, nb, hd, QB), lambda a, i: (a, 0, 0, 0))
    kshape = jax.ShapeDtypeStruct((h, nb, hd, QB), F32)
    return _pcall(kern, name=name, grid=(h, nq), in_specs=[blk, full, full, blk, blk], out_specs=[blk, full, full],
                  out_shape=[jax.ShapeDtypeStruct((h, s, hd), F32), kshape, kshape],
                  compiler_params=_params("parallel", "arbitrary"))(q, kt, vt, o, do)


DSA_SUB = 4


def _dsa_seq_blocks(t, s):
    steps_per_group = 4 * s // (QB * DSA_SUB)
    g = t // steps_per_group
    b0, b1, b2 = (s // (QB * r) for _, r in DSA_GROUPS)
    return jnp.where(g == 0, b0, jnp.where(g == 1, b1, b2))


def _dsa_rel():
    qi = lax.broadcasted_iota(jnp.int32, (QB, QB), 0)
    kj = lax.broadcasted_iota(jnp.int32, (QB, QB), 1)
    return kj - qi


def _prev_mask(rel, has_prev):
    return rel >= jnp.where(has_prev, 0, QB)


def _dsa_fwd(q, k, vp, *, name):
    rows = q.shape[0]
    s = rows // 12
    big = QB * DSA_SUB
    nsteps = rows // big

    def kern(q_ref, k_ref, kp_ref, v_ref, vpv_ref, o_ref):
        t = pl.program_id(0)
        bps = _dsa_seq_blocks(t, s)
        rel = _dsa_rel()
        lane = lax.broadcasted_iota(jnp.int32, (QB, LANES), 1)
        for a in range(DSA_SUB):
            qa = q_ref[pl.ds(a * QB, QB), :]
            kc = k_ref[pl.ds(a * QB, QB), :]
            vc = v_ref[pl.ds(a * QB, QB), :]
            if a == 0:
                kpv, vpv = kp_ref[...], vpv_ref[...]
            else:
                kpv, vpv = k_ref[pl.ds((a - 1) * QB, QB), :], v_ref[pl.ds((a - 1) * QB, QB), :]
            has_prev = (t * DSA_SUB + a) % bps != 0
            sc = jnp.where(rel <= 0, _nt(qa, kc) * SCALE, -jnp.inf)
            sp = jnp.where(_prev_mask(rel, has_prev), _nt(qa, kpv) * SCALE, -jnp.inf)
            m = jnp.maximum(jnp.max(sc, axis=1, keepdims=True), jnp.max(sp, axis=1, keepdims=True))
            pc = jnp.exp(sc - m)
            pp = jnp.exp(sp - m)
            den = jnp.sum(pc, axis=1, keepdims=True) + jnp.sum(pp, axis=1, keepdims=True)
            o = (_nn(pc, vc) + _nn(pp, vpv)) / den
            o_ref[pl.ds(a * QB, QB), :] = jnp.where(lane < HEAD_DIM, o, m + jnp.log(den))

    cur64 = pl.BlockSpec((big, HEAD_DIM), lambda t: (t, 0))
    prev64 = pl.BlockSpec((QB, HEAD_DIM), lambda t: (jnp.maximum(t * DSA_SUB - 1, 0), 0))
    cur128 = pl.BlockSpec((big, LANES), lambda t: (t, 0))
    prev128 = pl.BlockSpec((QB, LANES), lambda t: (jnp.maximum(t * DSA_SUB - 1, 0), 0))
    return _pcall(kern, name=name, grid=(nsteps,), in_specs=[cur64, cur64, prev64, cur128, prev128], out_specs=cur128,
                  out_shape=jax.ShapeDtypeStruct((rows, LANES), F32), compiler_params=_params("parallel"))(q, k, k, vp, vp)


def _dsa_combine(p0, p1, p2, *, name):
    hh, s, _ = p0.shape
    ts = _pick(s, (512, 256))

    def kern(a_ref, b_ref, c_ref, o_ref):
        lane = lax.broadcasted_iota(jnp.int32, (ts, LANES), 1)
        xs = [a_ref[...], b_ref[...], c_ref[...]]
        ls = [jnp.where(lane < HEAD_DIM, pltpu.roll(x, HEAD_DIM, 1), x) for x in xs]
        m = jnp.maximum(jnp.maximum(ls[0], ls[1]), ls[2])
        es = [jnp.exp(l - m) for l in ls]
        den = es[0] + es[1] + es[2]
        o = (es[0] * xs[0] + es[1] * xs[1] + es[2] * xs[2]) / den
        o_ref[...] = jnp.where(lane < HEAD_DIM, o, m + jnp.log(den))

    blk = pl.BlockSpec((None, ts, LANES), lambda a, i: (a, i, 0))
    return _pcall(kern, name=name, grid=(hh, s // ts), in_specs=[blk, blk, blk], out_specs=blk,
                  out_shape=jax.ShapeDtypeStruct((hh, s, LANES), F32), compiler_params=_params("parallel", "parallel"))(p0, p1, p2)


def _dsa_bwd_prep(comb, dop, *, name):
    hh, s, _ = comb.shape
    ts = _pick(s, (512, 256))

    def kern(c_ref, d_ref, o_ref):
        lane = lax.broadcasted_iota(jnp.int32, (ts, LANES), 1)
        c = c_ref[...]
        d = d_ref[...]
        dsum = jnp.sum(jnp.where(lane < HEAD_DIM, c * d, 0.0), axis=1, keepdims=True)
        o_ref[...] = jnp.where(lane < HEAD_DIM, d, jnp.where(lane < HEAD_DIM + 32, c, dsum))

    blk = pl.BlockSpec((None, ts, LANES), lambda a, i: (a, i, 0))
    return _pcall(kern, name=name, grid=(hh, s // ts), in_specs=[blk, blk], out_specs=blk,
                  out_shape=jax.ShapeDtypeStruct((hh, s, LANES), F32), compiler_params=_params("parallel", "parallel"))(comb, dop)


def _dsa_bwd(q, k, vp, pk, *, name):
    rows = q.shape[0]
    s = rows // 12
    big = QB * DSA_SUB
    nsteps = rows // big
    nblk = rows // QB

    def kern(q_ref, qn_ref, k_ref, kp_ref, v_ref, vpv_ref, p_ref, pn_ref, dq_ref, dk_ref, dv_ref):
        t = pl.program_id(0)
        bps = _dsa_seq_blocks(t, s)
        rel = _dsa_rel()
        lane = lax.broadcasted_iota(jnp.int32, (QB, LANES), 1)

        def stats(pa):
            lse = jnp.max(jnp.where(jnp.logical_and(lane >= HEAD_DIM, lane < HEAD_DIM + 32), pa, -jnp.inf), axis=1, keepdims=True)
            dsum = jnp.max(jnp.where(lane >= HEAD_DIM + 32, pa, -jnp.inf), axis=1, keepdims=True)
            return lse, dsum

        def pair(qa, pa, st, kb, vb, mask):
            p = jnp.where(mask, jnp.exp(_nt(qa, kb) * SCALE - st[0]), 0.0)
            ds = p * (_nt(pa, vb) - st[1]) * SCALE
            return _nn(ds, kb), _tn(ds, qa), _tn(p, pa)

        for a in range(DSA_SUB):
            qa = q_ref[pl.ds(a * QB, QB), :]
            pa = p_ref[pl.ds(a * QB, QB), :]
            st = stats(pa)
            kc = k_ref[pl.ds(a * QB, QB), :]
            vc = v_ref[pl.ds(a * QB, QB), :]
            if a == 0:
                kpv, vpv = kp_ref[...], vpv_ref[...]
            else:
                kpv, vpv = k_ref[pl.ds((a - 1) * QB, QB), :], v_ref[pl.ds((a - 1) * QB, QB), :]
            has_prev = (t * DSA_SUB + a) % bps != 0
            dq_c, dk_c, dv_c = pair(qa, pa, st, kc, vc, rel <= 0)
            dq_p, dk_p, dv_p = pair(qa, pa, st, kpv, vpv, _prev_mask(rel, has_prev))
            dq_ref[pl.ds(a * QB, QB), :] = dq_c + dq_p
            if a == 0:
                dk_ref[pl.ds(0, QB), :] = dk_c
                dv_ref[pl.ds(0, QB), :] = dv_c
            else:
                dk_ref[pl.ds(a * QB, QB), :] = dk_c
                dv_ref[pl.ds(a * QB, QB), :] = dv_c
                dk_ref[pl.ds((a - 1) * QB, QB), :] += dk_p
                dv_ref[pl.ds((a - 1) * QB, QB), :] += dv_p
        nxt = t * DSA_SUB + DSA_SUB
        has_next = jnp.logical_and(nxt < nblk, nxt % bps != 0)
        last = (DSA_SUB - 1) * QB
        pn = pn_ref[...]
        _, dk_n, dv_n = pair(qn_ref[...], pn, stats(pn), k_ref[pl.ds(last, QB), :], v_ref[pl.ds(last, QB), :],
                             _prev_mask(rel, has_next))
        dk_ref[pl.ds(last, QB), :] += dk_n
        dv_ref[pl.ds(last, QB), :] += dv_n

    def prev_map(t):
        return (jnp.maximum(t * DSA_SUB - 1, 0), 0)

    def next_map(t):
        return (jnp.minimum(t * DSA_SUB + DSA_SUB, nblk - 1), 0)

    cur64 = pl.BlockSpec((big, HEAD_DIM), lambda t: (t, 0))
    cur128 = pl.BlockSpec((big, LANES), lambda t: (t, 0))
    specs = [cur64, pl.BlockSpec((QB, HEAD_DIM), next_map), cur64, pl.BlockSpec((QB, HEAD_DIM), prev_map),
             cur128, pl.BlockSpec((QB, LANES), prev_map), cur128, pl.BlockSpec((QB, LANES), next_map)]
    return _pcall(kern, name=name, grid=(nsteps,), in_specs=specs, out_specs=[cur64, cur64, cur128],
                  out_shape=[jax.ShapeDtypeStruct((rows, HEAD_DIM), F32), jax.ShapeDtypeStruct((rows, HEAD_DIM), F32),
                             jax.ShapeDtypeStruct((rows, LANES), F32)],
                  compiler_params=_params("parallel"))(q, q, k, k, vp, vp, pk, pk)


def _mem_fwd(q, km, vm, *, name):
    hh, s, hd = q.shape
    ml = km.shape[1]
    tq = _pick(s, (512, 256))

    def kern(q_ref, k_ref, v_ref, o_ref):
        sc = _nt(q_ref[...], k_ref[...]) * SCALE
        e = jnp.exp(sc - jnp.max(sc, axis=1, keepdims=True))
        p = e / jnp.sum(e, axis=1, keepdims=True)
        o_ref[...] = _nn(p, v_ref[...])

    blk = pl.BlockSpec((None, tq, hd), lambda a, i: (a, i, 0))
    kv = pl.BlockSpec((None, ml, hd), lambda a, i: (a, 0, 0))
    return _pcall(kern, name=name, grid=(hh, s // tq), in_specs=[blk, kv, kv], out_specs=blk,
                  out_shape=jax.ShapeDtypeStruct((hh, s, hd), F32), compiler_params=_params("parallel", "parallel"))(q, km, vm)


def _mem_bwd(q, km, vm, do, *, name):
    hh, s, hd = q.shape
    ml = km.shape[1]
    tq = _pick(s, (512, 256))

    def kern(q_ref, k_ref, v_ref, do_ref, dq_ref, dk_ref, dv_ref):
        @pl.when(pl.program_id(1) == 0)
        def _():
            dk_ref[...] = jnp.zeros_like(dk_ref)
            dv_ref[...] = jnp.zeros_like(dv_ref)

        qb = q_ref[...]
        dob = do_ref[...]
        sc = _nt(qb, k_ref[...]) * SCALE
        e = jnp.exp(sc - jnp.max(sc, axis=1, keepdims=True))
        p = e / jnp.sum(e, axis=1, keepdims=True)
        dp = _nt(dob, v_ref[...])
        ds = p * (dp - jnp.sum(p * dp, axis=1, keepdims=True)) * SCALE
        dq_ref[...] = _nn(ds, k_ref[...])
        dk_ref[...] += _tn(ds, qb)
        dv_ref[...] += _tn(p, dob)

    blk = pl.BlockSpec((None, tq, hd), lambda a, i: (a, i, 0))
    kv = pl.BlockSpec((None, ml, hd), lambda a, i: (a, 0, 0))
    kvs = jax.ShapeDtypeStruct((hh, ml, hd), F32)
    return _pcall(kern, name=name, grid=(hh, s // tq), in_specs=[blk, kv, kv, blk], out_specs=[blk, kv, kv],
                  out_shape=[jax.ShapeDtypeStruct((hh, s, hd), F32), kvs, kvs],
                  compiler_params=_params("parallel", "arbitrary"))(q, km, vm, do)


DSA_BT = QB * max(r for _, r in DSA_GROUPS)


def _unit_rows(r, c, b):
    return pl.ds(c + QB * r * b, QB, stride=r)


def _pair_cols(t, first):
    return [jnp.max(jnp.where(first, t, -jnp.inf), axis=1, keepdims=True),
            jnp.max(jnp.where(first, -jnp.inf, t), axis=1, keepdims=True)]


def _dsa2_fwd(qn, kn, v32, g, *, name):
    s = qn.shape[0]
    r = DSA_GROUPS[g][1]
    nbk = DSA_BT // (QB * r)
    npair = DSA_OUT_W // LANES

    def kern(q_ref, k_ref, kp_ref, v_ref, vp_ref, o_ref, l_ref):
        t = pl.program_id(1)
        first = _first_half((QB, LANES))
        rel = _dsa_rel()
        for c in range(r):
            for b in range(nbk):
                rows = _unit_rows(r, c, b)
                kc, vc = k_ref[rows, :], v_ref[rows, :]
                if b > 0:
                    prow = _unit_rows(r, c, b - 1)
                    kpv, vpv, has_prev = k_ref[prow, :], v_ref[prow, :], True
                else:
                    prow = _unit_rows(r, c, nbk - 1)
                    kpv, vpv, has_prev = kp_ref[prow, :], vp_ref[prow, :], t > 0
                outs, lses = [], []
                for qe in _split_pair(q_ref[rows, :], first):
                    sc = jnp.where(rel <= 0, _nt(qe, kc) * SCALE, -jnp.inf)
                    sp = jnp.where(_prev_mask(rel, has_prev), _nt(qe, kpv) * SCALE, -jnp.inf)
                    m = jnp.maximum(jnp.max(sc, axis=1, keepdims=True), jnp.max(sp, axis=1, keepdims=True))
                    pc = jnp.exp(sc - m)
                    pp = jnp.exp(sp - m)
                    den = jnp.sum(pc, axis=1, keepdims=True) + jnp.sum(pp, axis=1, keepdims=True)
                    outs.append((_nn(pc, vc) + _nn(pp, vpv)) / den)
                    lses.append(m + jnp.log(den))
                o_ref[rows, :] = jnp.where(first, outs[0], outs[1])
                l_ref[rows, :] = jnp.where(first, lses[0], lses[1])

    npg = DSA_HPG * HEAD_DIM // LANES
    cur = pl.BlockSpec((DSA_BT, LANES), lambda a, t: (t, npg * g + a))
    prev = pl.BlockSpec((DSA_BT, LANES), lambda a, t: (jnp.maximum(t - 1, 0), npg * g + a))
    out = pl.BlockSpec((DSA_BT, LANES), lambda a, t: (t, a))
    shp = jax.ShapeDtypeStruct((s, DSA_OUT_W), F32)
    return _pcall(kern, name=name, grid=(npair, s // DSA_BT), in_specs=[cur, cur, prev, cur, prev], out_specs=[out, out],
                  out_shape=[shp, shp], compiler_params=_params("parallel", "parallel"))(qn, kn, kn, v32, v32)


def _dsa2_combine(parts, *, name):
    s, wd = parts[0][0].shape
    ts = _pick(s, (512, 256))

    def kern(o0, l0, o1, l1, o2, l2, o_ref, l_ref):
        ls = [l0[...], l1[...], l2[...]]
        m = jnp.maximum(jnp.maximum(ls[0], ls[1]), ls[2])
        es = [jnp.exp(l - m) for l in ls]
        den = es[0] + es[1] + es[2]
        o_ref[...] = (es[0] * o0[...] + es[1] * o1[...] + es[2] * o2[...]) / den
        l_ref[...] = m + jnp.log(den)

    blk = pl.BlockSpec((ts, wd), lambda i: (i, 0))
    shp = jax.ShapeDtypeStruct((s, wd), F32)
    flat = [t for pair in parts for t in pair]
    return _pcall(kern, name=name, grid=(s // ts,), in_specs=[blk] * 6, out_specs=[blk, blk], out_shape=[shp, shp],
                  compiler_params=_params("parallel"))(*flat)


def _dsa2_prep(o, do, *, name):
    s, wd = o.shape
    ts = _pick(s, (512, 256))

    def kern(o_ref, do_ref, d_ref):
        d_ref[...] = _head_mean(do_ref[...] * o_ref[...], _block_diag(wd)) * HEAD_DIM

    blk = pl.BlockSpec((ts, wd), lambda i: (i, 0))
    return _pcall(kern, name=name, grid=(s // ts,), in_specs=[blk, blk], out_specs=blk,
                  out_shape=jax.ShapeDtypeStruct((s, wd), F32), compiler_params=_params("parallel"))(o, do)


def _dsa2_bwd(qn, kn, v32, do, lse, dd, g, *, name):
    s = qn.shape[0]
    r = DSA_GROUPS[g][1]
    nbk = DSA_BT // (QB * r)
    npair = DSA_OUT_W // LANES
    nsteps = s // DSA_BT

    def kern(q_ref, qn_ref, k_ref, kp_ref, v_ref, vp_ref, do_ref, don_ref, l_ref, ln_ref, d_ref, dn_ref,
             dq_ref, dk_ref, dv_ref):
        t = pl.program_id(1)
        first = _first_half((QB, LANES))
        rel = _dsa_rel()

        def pair(qs, dos, lcols, dcols, kb, vb, mask):
            dqs = []
            dk = jnp.zeros((QB, LANES), F32)
            dv = jnp.zeros((QB, LANES), F32)
            for e in range(2):
                p = jnp.where(mask, jnp.exp(_nt(qs[e], kb) * SCALE - lcols[e]), 0.0)
                ds = p * (_nt(dos[e], vb) - dcols[e]) * SCALE
                dqs.append(_nn(ds, kb))
                dk = dk + _tn(ds, qs[e])
                dv = dv + _tn(p, dos[e])
            return dqs, dk, dv

        def load(rows, qr, dor, lr, dr):
            return (_split_pair(qr[rows, :], first), _split_pair(dor[rows, :], first), _pair_cols(lr[rows, :], first),
                    _pair_cols(dr[rows, :], first))

        for c in range(r):
            for b in range(nbk):
                rows = _unit_rows(r, c, b)
                qs, dos, lcols, dcols = load(rows, q_ref, do_ref, l_ref, d_ref)
                dq_c, dk_c, dv_c = pair(qs, dos, lcols, dcols, k_ref[rows, :], v_ref[rows, :], rel <= 0)
                if b > 0:
                    prow = _unit_rows(r, c, b - 1)
                    dq_p, dk_p, dv_p = pair(qs, dos, lcols, dcols, k_ref[prow, :], v_ref[prow, :], _prev_mask(rel, True))
                    dk_ref[prow, :] += dk_p
                    dv_ref[prow, :] += dv_p
                else:
                    prow = _unit_rows(r, c, nbk - 1)
                    dq_p, _, _ = pair(qs, dos, lcols, dcols, kp_ref[prow, :], vp_ref[prow, :], _prev_mask(rel, t > 0))
                dq_ref[rows, :] = jnp.where(first, dq_c[0] + dq_p[0], dq_c[1] + dq_p[1])
                dk_ref[rows, :] = dk_c
                dv_ref[rows, :] = dv_c
            last = _unit_rows(r, c, nbk - 1)
            nqs, ndos, nl, nd = load(_unit_rows(r, c, 0), qn_ref, don_ref, ln_ref, dn_ref)
            _, dk_n, dv_n = pair(nqs, ndos, nl, nd, k_ref[last, :], v_ref[last, :], _prev_mask(rel, t < nsteps - 1))
            dk_ref[last, :] += dk_n
            dv_ref[last, :] += dv_n

    npg = DSA_HPG * HEAD_DIM // LANES

    def at(shift, col):
        return pl.BlockSpec((DSA_BT, LANES), lambda a, t: (jnp.clip(t + shift, 0, nsteps - 1), col(a)))

    gcol = lambda a: npg * g + a
    ocol = lambda a: a
    specs = [at(0, gcol), at(1, gcol), at(0, gcol), at(-1, gcol), at(0, gcol), at(-1, gcol),
             at(0, ocol), at(1, ocol), at(0, ocol), at(1, ocol), at(0, ocol), at(1, ocol)]
    shp = jax.ShapeDtypeStruct((s, DSA_OUT_W), F32)
    return _pcall(kern, name=name, grid=(npair, nsteps), in_specs=specs, out_specs=[at(0, ocol)] * 3, out_shape=[shp, shp, shp],
                  compiler_params=_params("parallel", "parallel"))(qn, qn, kn, kn, v32, v32, do, do, lse, lse, dd, dd)


def _mem2_fwd(qn, km, kv, *, name):
    s = qn.shape[0]
    ml = km.shape[0]
    tq = _pick(s, (512, 256))
    npair = MEM_W // LANES

    def kern(q_ref, k_ref, v_ref, o_ref):
        first = _first_half((tq, LANES))
        outs = []
        for qe in _split_pair(q_ref[...], first):
            sc = _nt(qe, k_ref[...]) * SCALE
            e = jnp.exp(sc - jnp.max(sc, axis=1, keepdims=True))
            outs.append(_nn(e / jnp.sum(e, axis=1, keepdims=True), v_ref[...]))
        o_ref[...] = jnp.where(first, outs[0], outs[1])

    blk = pl.BlockSpec((tq, LANES), lambda a, i: (i, a))
    return _pcall(kern, name=name, grid=(npair, s // tq),
                  in_specs=[blk, pl.BlockSpec((ml, LANES), lambda a, i: (0, a)), pl.BlockSpec((ml, LANES), lambda a, i: (0, npair + a))],
                  out_specs=blk, out_shape=jax.ShapeDtypeStruct((s, MEM_W), F32),
                  compiler_params=_params("parallel", "parallel"))(qn, km, kv)


def _mem2_bwd(qn, km, kv, do, *, name):
    s = qn.shape[0]
    ml = km.shape[0]
    tq = _pick(s, (512, 256))
    npair = MEM_W // LANES

    def kern(q_ref, k_ref, v_ref, do_ref, dq_ref, dk_ref, dv_ref):
        @pl.when(pl.program_id(1) == 0)
        def _():
            dk_ref[...] = jnp.zeros_like(dk_ref)
            dv_ref[...] = jnp.zeros_like(dv_ref)

        first = _first_half((tq, LANES))
        dqs = []
        for qe, doe in zip(_split_pair(q_ref[...], first), _split_pair(do_ref[...], first)):
            sc = _nt(qe, k_ref[...]) * SCALE
            e = jnp.exp(sc - jnp.max(sc, axis=1, keepdims=True))
            p = e / jnp.sum(e, axis=1, keepdims=True)
            dp = _nt(doe, v_ref[...])
            ds = p * (dp - jnp.sum(p * dp, axis=1, keepdims=True)) * SCALE
            dqs.append(_nn(ds, k_ref[...]))
            dk_ref[...] += _tn(ds, qe)
            dv_ref[...] += _tn(p, doe)
        dq_ref[...] = jnp.where(first, dqs[0], dqs[1])

    blk = pl.BlockSpec((tq, LANES), lambda a, i: (i, a))
    kblk = pl.BlockSpec((ml, LANES), lambda a, i: (0, a))
    kshape = jax.ShapeDtypeStruct((ml, MEM_W), F32)
    return _pcall(kern, name=name, grid=(npair, s // tq),
                  in_specs=[blk, kblk, pl.BlockSpec((ml, LANES), lambda a, i: (0, npair + a)), blk],
                  out_specs=[blk, kblk, kblk], out_shape=[jax.ShapeDtypeStruct((s, MEM_W), F32), kshape, kshape],
                  compiler_params=_params("parallel", "arbitrary"))(qn, km, kv, do)


def _merge_fwd(logits, bias, ya, yb, yc, *, name):
    s, d = ya.shape
    ts = _pick(s, (512, 256))

    def kern(l0, l1, l2, b0, b1, b2, a_ref, b_ref, c_ref, o_ref):
        m = (_sigmoid(l0[...] + b0[...]) * a_ref[...] + _sigmoid(l1[...] + b1[...]) * b_ref[...]
             + _sigmoid(l2[...] + b2[...]) * c_ref[...])
        o_ref[...] = m.astype(o_ref.dtype)

    row = pl.BlockSpec((ts, d), lambda i: (i, 0))
    lg = [pl.BlockSpec((ts, d), functools.partial(lambda i, c: (i, c), c=c)) for c in range(3)]
    bs = [pl.BlockSpec((1, d), functools.partial(lambda i, c: (0, c), c=c)) for c in range(3)]
    return _pcall(kern, name=name, grid=(s // ts,), in_specs=lg + bs + [row, row, row], out_specs=row,
                  out_shape=jax.ShapeDtypeStruct((s, d), BF16),
                  compiler_params=_params("parallel"))(logits, logits, logits, bias, bias, bias, ya, yb, yc)


def _merge_bwd(logits, bias, ya, yb, yc, dm, *, name):
    s, d = ya.shape
    ts = _pick(s, (256,))

    def kern(l0, l1, l2, b0, b1, b2, a_ref, b_ref, c_ref, dm_ref, da_ref, db_ref, dc_ref, dl0, dl1, dl2, dbias0, dbias1, dbias2):
        first = pl.program_id(0) == 0
        dmv = dm_ref[...]
        for l_ref, bb_ref, y_ref, dy_ref, dl_ref, dbias_ref in ((l0, b0, a_ref, da_ref, dl0, dbias0), (l1, b1, b_ref, db_ref, dl1, dbias1),
                                                                (l2, b2, c_ref, dc_ref, dl2, dbias2)):
            g = _sigmoid(l_ref[...] + bb_ref[...])
            dy_ref[...] = (dmv * g).astype(dy_ref.dtype)
            dl = dmv * y_ref[...] * g * (1.0 - g)
            dl_ref[...] = dl.astype(dl_ref.dtype)

            @pl.when(first)
            def _():
                dbias_ref[...] = jnp.zeros_like(dbias_ref)

            dbias_ref[...] += jnp.sum(dl, axis=0, keepdims=True)

    row = pl.BlockSpec((ts, d), lambda i: (i, 0))
    lg = [pl.BlockSpec((ts, d), functools.partial(lambda i, c: (i, c), c=c)) for c in range(3)]
    bs = [pl.BlockSpec((1, d), functools.partial(lambda i, c: (0, c), c=c)) for c in range(3)]
    vec = pl.BlockSpec((1, d), lambda i: (0, 0))
    yshape = jax.ShapeDtypeStruct((s, d), BF16)
    vshape = jax.ShapeDtypeStruct((1, d), F32)
    outs = _pcall(kern, name=name, grid=(s // ts,), in_specs=lg + bs + [row, row, row, row],
                  out_specs=[row, row, row, row, row, row, vec, vec, vec],
                  out_shape=[yshape] * 6 + [vshape] * 3,
                  compiler_params=_params("arbitrary"))(logits, logits, logits, bias, bias, bias, ya, yb, yc, dm)
    return outs[0], outs[1], outs[2], outs[3:6], jnp.concatenate(outs[6:9], axis=1)


def _heads(t, n):
    s = t.shape[0]
    return t.reshape(s, n, HEAD_DIM).transpose(1, 0, 2)


def _unheads(t):
    n, s, hd = t.shape
    return t.transpose(1, 0, 2).reshape(s, n * hd)


def _to_class_major(t):
    s = t.shape[0]
    w = t.shape[1] // (DSA_HPG * len(DSA_GROUPS))
    parts = []
    for g, (_, r) in enumerate(DSA_GROUPS):
        tg = t[:, g * DSA_HPG * w:(g + 1) * DSA_HPG * w].reshape(s // r, r, DSA_HPG, w)
        parts.append(tg.transpose(2, 1, 0, 3).reshape(DSA_HPG * s, w))
    return jnp.concatenate(parts, axis=0)


def _slot_to_class_major(t):
    hh, s, w = t.shape
    parts = []
    for _, r in DSA_GROUPS:
        parts.append(t.reshape(hh, s // r, r, w).transpose(0, 2, 1, 3).reshape(hh * s, w))
    return jnp.concatenate(parts, axis=0)


def _from_class_major(t):
    rows, w = t.shape
    s = rows // 12
    out = []
    for g, (_, r) in enumerate(DSA_GROUPS):
        tg = t[g * 4 * s:(g + 1) * 4 * s].reshape(DSA_HPG, r, s // r, w)
        out.append(tg.transpose(0, 2, 1, 3).reshape(DSA_HPG, s, w))
    return out


def _pad_lanes(t):
    return jnp.concatenate([t, jnp.zeros(t.shape[:-1] + (LANES - t.shape[-1],), t.dtype)], axis=-1)


def _ffn_fwd(x, norm, w1, w3, w2, tag):
    h = _rms_fwd(x, norm, name=f"{tag}_rms")
    a, b, f = _ffn_up(h, w1, w3, name=f"{tag}_up")
    y = _matmul(f, w2, name=f"{tag}_down", res=x, alpha=0.5, tk=1408)
    return y, (h, a, b, f)


def _ffn_bwd(x, norm, w1, w3, w2, saved, dy, dyb, tag):
    h, a, b, f = saved
    dw2 = _matmul(f, dyb, name=f"{tag}_dw2", ta=True, alpha=0.5, tm=1408, tn=1024, tk=512)
    da, db = _ffn_dact(dyb, w2, a, b, name=f"{tag}_dact")
    dw1 = _matmul(h, da, name=f"{tag}_dw1", ta=True, tm=1024, tn=1408, tk=512)
    dw3 = _matmul(h, db, name=f"{tag}_dw3", ta=True, tm=1024, tn=1408, tk=512)
    dh = _matmul(da, w1, name=f"{tag}_dh", tb=True, tn=1024, tk=1408, pair2=(db, w3))
    dx, dxb, dnorm = _rms_bwd(x, norm, dh, dy, name=f"{tag}_drms")
    return dx, dxb, dnorm, dw1, dw3, dw2


def _local_step(x, mem, w, loss_target):
    s, d = x.shape
    assert s % (QB * 16) == 0
    rope = _rope_tables(s)

    x1, sv1 = _ffn_fwd(x, w['ffn1_norm'], w['ffn1_w1'], w['ffn1_w3'], w['ffn1_w2'], "ffn1")
    h = _rms_fwd(x1, w['mix_norm'], name="mix_rms")
    p = _matmul(h, w['w_in'], name="in_proj", out_dtype=BF16)
    logits = _matmul(h, w['w_gate'], name="gate_proj")
    c_qb, c_kb, c_vb, c_qc = 3 * SB_W, 3 * SB_W + DSA_W, 3 * SB_W + 2 * DSA_W, 3 * SB_W + 3 * DSA_W

    oa_t = _sb2_fwd(p, name="sb_fwd")
    ya = _matmul(oa_t, w['w_branch_sb'], name="sb_out")

    qb_n = _qknorm_fwd(p, c_qb, DSA_W, w['qn_dsa'], rope, name="dsa_qnorm", out_dtype=F32)
    kb_n = _qknorm_fwd(p, c_kb, DSA_W, w['kn_dsa'], rope, name="dsa_knorm", out_dtype=F32)
    vb32 = p[:, c_vb:c_vb + DSA_W].astype(F32)
    groups = range(len(DSA_GROUPS))
    ob_t, lse_b = _dsa2_combine([_dsa2_fwd(qb_n, kb_n, vb32, gi, name=f"dsa_fwd{gi}") for gi in groups], name="dsa_combine")
    yb = _matmul(ob_t, w['w_branch_dsa'], name="dsa_out")

    memh = _rms_fwd(mem, w['mem_norm'], name="mem_rms")
    kv = _matmul(memh, w['w_mem_kv'], name="mem_kv", out_dtype=BF16)
    km_n = _qknorm_fwd(kv, 0, MEM_W, w['kn_mem'], None, name="mem_knorm")
    qc_n = _qknorm_fwd(p, c_qc, MEM_W, w['qn_mem'], None, name="mem_qnorm")
    oc_t = _mem2_fwd(qc_n, km_n, kv, name="mem_fwd")
    yc = _matmul(oc_t, w['w_branch_mem'], name="mem_out")

    merged = _merge_fwd(logits, w['b_gate'], ya, yb, yc, name="merge")
    x2 = _matmul(merged, w['w_out'], name="out_proj", res=x1)
    x3, sv2 = _ffn_fwd(x2, w['ffn2_norm'], w['ffn2_w1'], w['ffn2_w3'], w['ffn2_w2'], "ffn2")
    dx3, dx3b, loss = _loss_head(x3, loss_target, name="loss")

    g = {}
    dx2, dx2b, g['ffn2_norm'], g['ffn2_w1'], g['ffn2_w3'], g['ffn2_w2'] = _ffn_bwd(
        x2, w['ffn2_norm'], w['ffn2_w1'], w['ffn2_w3'], w['ffn2_w2'], sv2, dx3, dx3b, "ffn2")

    g['w_out'] = _matmul(merged, dx2b, name="d_w_out", ta=True, tk=512)
    dm = _matmul(dx2b, w['w_out'], name="d_merged", tb=True)
    dya, dyb, dyc, dlog, g['b_gate'] = _merge_bwd(logits, w['b_gate'], ya, yb, yc, dm, name="d_merge")
    dlogits = jnp.concatenate(dlog, axis=1)

    g['w_branch_sb'] = _matmul(oa_t, dya, name="d_w_sb", ta=True, tk=512)
    g['w_branch_dsa'] = _matmul(ob_t, dyb, name="d_w_dsa", ta=True, tk=512)
    g['w_branch_mem'] = _matmul(oc_t, dyc, name="d_w_mem", ta=True, tk=512)
    doa = _matmul(dya, w['w_branch_sb'], name="d_oa", tb=True, out_dtype=BF16)
    dob = _matmul(dyb, w['w_branch_dsa'], name="d_ob", tb=True)
    doc = _matmul(dyc, w['w_branch_mem'], name="d_oc", tb=True, out_dtype=BF16)

    dqa, dka, dva = _sb2_bwd(p, oa_t, doa, name="sb_bwd")

    dd_b = _dsa2_prep(ob_t, dob, name="dsa_prep")
    dgrp = [_dsa2_bwd(qb_n, kb_n, vb32, dob, lse_b, dd_b, gi, name=f"dsa_bwd{gi}") for gi in groups]
    dvb = jnp.concatenate([t[2] for t in dgrp], axis=1).astype(BF16)
    dqb, g['qn_dsa'] = _qknorm_bwd(p, c_qb, DSA_W, w['qn_dsa'], rope, [t[0] for t in dgrp], name="d_dsa_qnorm")
    dkb, g['kn_dsa'] = _qknorm_bwd(p, c_kb, DSA_W, w['kn_dsa'], rope, [t[1] for t in dgrp], name="d_dsa_knorm")

    dqc_n, dkm_n, dvm = _mem2_bwd(qc_n, km_n, kv, doc, name="mem_bwd")
    dqc, g['qn_mem'] = _qknorm_bwd(p, c_qc, MEM_W, w['qn_mem'], None, dqc_n, name="d_mem_qnorm")
    dkm, g['kn_mem'] = _qknorm_bwd(kv, 0, MEM_W, w['kn_mem'], None, dkm_n, name="d_mem_knorm")
    dkv = jnp.concatenate([dkm, dvm.astype(BF16)], axis=1)
    g['w_mem_kv'] = _matmul(memh, dkv, name="d_w_mem_kv", ta=True)
    dmemh = _matmul(dkv, w['w_mem_kv'], name="d_memh", tb=True)
    _, _, g['mem_norm'] = _rms_bwd(mem, w['mem_norm'], dmemh, None, name="d_mem_rms")

    dp = jnp.concatenate([dqa.astype(BF16), dka.astype(BF16), dva.astype(BF16),
                          dqb, dkb, dvb, dqc], axis=1)
    g['w_in'] = _matmul(h, dp, name="d_w_in", ta=True, tk=512)
    g['w_gate'] = _matmul(h, dlogits, name="d_w_gate", ta=True, tk=512)
    dh = _matmul(dp, w['w_in'], name="d_h_in", tb=True)
    dh = _matmul(dlogits, w['w_gate'], name="d_h_gate", tb=True, res=dh)
    dx1, dx1b, g['mix_norm'] = _rms_bwd(x1, w['mix_norm'], dh, dx2, name="d_mix_rms")

    dx0, _, g['ffn1_norm'], g['ffn1_w1'], g['ffn1_w3'], g['ffn1_w2'] = _ffn_bwd(
        x, w['ffn1_norm'], w['ffn1_w1'], w['ffn1_w3'], w['ffn1_w2'], sv1, dx1, dx1b, "ffn1")
    return loss, dx0, g


def _pack_rows(d, names):
    return jnp.concatenate([d[n].reshape(-1, LANES) for n in names], axis=0)


def _unpack_rows(t, like, names):
    out, off = {}, 0
    for n in names:
        r = like[n].size // LANES
        out[n] = t[off:off + r].reshape(like[n].shape)
        off += r
    return out


def _unpack_gathered(t, local, names):
    out, off = {}, 0
    for n in names:
        r, c = local[n].shape
        rows = r * c // LANES
        blk = t[:, off:off + rows].reshape(N_DEV, r, c)
        out[n] = blk.reshape(N_DEV * r, c) if SHARD_AXIS[n] == 0 else blk.transpose(1, 0, 2).reshape(r, N_DEV * c)
        off += rows
    return out


def _pack_for_owners(g, local, names):
    parts = []
    for n in names:
        r, c = local[n].shape
        blk = g[n].reshape(N_DEV, r, c) if SHARD_AXIS[n] == 0 else g[n].reshape(r, N_DEV, c).transpose(1, 0, 2)
        parts.append(blk.reshape(N_DEV, r * c // LANES, LANES))
    return jnp.concatenate(parts, axis=1)


def _pack_small(d, names, extra_rows):
    parts = []
    for n in names:
        v = d[n].reshape(-1)
        pad = (-v.size) % LANES
        parts.append(jnp.concatenate([v, jnp.zeros((pad,), v.dtype)]).reshape(-1, LANES))
    t = jnp.concatenate(parts, axis=0)
    return jnp.concatenate([t, jnp.zeros((extra_rows, LANES), t.dtype)], axis=0)


def _unpack_small(t, like, names):
    out, off = {}, 0
    for n in names:
        size = like[n].size
        rows = -(-size // LANES)
        out[n] = t[off:off + rows].reshape(-1)[:size].reshape(like[n].shape)
        off += rows
    return out


def _exchange(src, per_peer, *, name):
    rows = src.shape[-2]

    def body(src_ref, out_ref, send_sems, recv_sems, local_sem):
        x, y, c = lax.axis_index("x"), lax.axis_index("y"), lax.axis_index("c")
        me = 4 * x + 2 * y + c
        mine = pltpu.make_async_copy(src_ref.at[me] if per_peer else src_ref, out_ref.at[me], local_sem)
        mine.start()
        copies = []
        for k in range(1, N_DEV):
            px = 1 - x if k & 4 else x
            py = 1 - y if k & 2 else y
            pc = 1 - c if k & 1 else c
            cp = pltpu.make_async_remote_copy(
                src_ref=src_ref.at[4 * px + 2 * py + pc] if per_peer else src_ref, dst_ref=out_ref.at[me],
                send_sem=send_sems.at[k - 1], recv_sem=recv_sems.at[k - 1],
                device_id=(px, py, pc), device_id_type=pl.DeviceIdType.MESH)
            cp.start()
            copies.append(cp)
        for cp in copies:
            cp.wait_recv()
        for cp in copies:
            cp.wait_send()
        mine.wait()

    anyspace = pl.BlockSpec(memory_space=pl.ANY)
    return _pcall(body, name=name, in_specs=[anyspace], out_specs=anyspace,
                  out_shape=jax.ShapeDtypeStruct((N_DEV, rows, LANES), src.dtype),
                  scratch_shapes=[pltpu.SemaphoreType.DMA((N_DEV - 1,)), pltpu.SemaphoreType.DMA((N_DEV - 1,)),
                                  pltpu.SemaphoreType.DMA])(src)


def _gather_two_level(src, *, name):
    rows = src.shape[0]

    def body(src_ref, out_ref, send_sems, recv_sems, local_sem):
        x, y, c = lax.axis_index("x"), lax.axis_index("y"), lax.axis_index("c")
        me, sibling = (x, y, c), (x, y, 1 - c)
        chips = [(1 - x, y), (x, 1 - y), (1 - x, 1 - y)]

        def slab(px, py, pc):
            return out_ref.at[4 * px + 2 * py + pc]

        def copy(k, block, to, from_src=False):
            return pltpu.make_async_remote_copy(
                src_ref=src_ref if from_src else slab(*block), dst_ref=slab(*block),
                send_sem=send_sems.at[k], recv_sem=recv_sems.at[k], device_id=to, device_id_type=pl.DeviceIdType.MESH)

        mine = pltpu.make_async_copy(src_ref, slab(*me), local_sem)
        mine.start()
        first = [copy(0, me, sibling, True)] + [copy(1 + j, me, (*chip, c), True) for j, chip in enumerate(chips)]
        for cp in first:
            cp.start()
        passed = [copy(4 + j, (*chip, c), sibling) for j, chip in enumerate(chips)]
        for j, chip in enumerate(chips):
            copy(1 + j, (*chip, c), me).wait_recv()
            passed[j].start()
        copy(0, sibling, me).wait_recv()
        for j, chip in enumerate(chips):
            copy(4 + j, (*chip, 1 - c), me).wait_recv()
        for cp in first + passed:
            cp.wait_send()
        mine.wait()

    anyspace = pl.BlockSpec(memory_space=pl.ANY)
    return _pcall(body, name=name, in_specs=[anyspace], out_specs=anyspace,
                  out_shape=jax.ShapeDtypeStruct((N_DEV, rows, LANES), src.dtype),
                  scratch_shapes=[pltpu.SemaphoreType.DMA((N_DEV - 1,)), pltpu.SemaphoreType.DMA((N_DEV - 1,)),
                                  pltpu.SemaphoreType.DMA])(src)


def _adamw(recv, w, m, v, *, name):
    rows = w.shape[0]
    tr = _pick(rows, (512, 256, 128, 64))

    def kern(r_ref, w_ref, m_ref, v_ref, g_ref, d_ref, mo_ref, vo_ref):
        g = r_ref[0].astype(F32)
        for p in range(1, N_DEV):
            g = g + r_ref[p].astype(F32)
        mn = ADAM_B1 * m_ref[...] + (1.0 - ADAM_B1) * g
        vn = ADAM_B2 * v_ref[...] + (1.0 - ADAM_B2) * (g * g)
        m_hat = mn / (1.0 - ADAM_B1 ** ADAM_STEP)
        v_hat = vn / (1.0 - ADAM_B2 ** ADAM_STEP)
        g_ref[...] = g
        d_ref[...] = -ADAM_LR * (m_hat / (jnp.sqrt(v_hat) + ADAM_EPS) + ADAM_WD * w_ref[...])
        mo_ref[...] = mn
        vo_ref[...] = vn

    row = pl.BlockSpec((tr, LANES), lambda i: (i, 0))
    shp = jax.ShapeDtypeStruct((rows, LANES), F32)
    return _pcall(kern, name=name, grid=(rows // tr,), in_specs=[pl.BlockSpec((N_DEV, tr, LANES), lambda i: (0, i, 0)), row, row, row],
                  out_specs=[row, row, row, row], out_shape=[shp, shp, shp, shp], compiler_params=_params("parallel"))(recv, w, m, v)


INPUTS = ['x', 'mem'] + WEIGHTS + ['loss_target'] + ['m_' + n for n in WEIGHTS] + ['v_' + n for n in WEIGHTS]
SMALL_PAD_ROWS = 4


def kernel(x, mem, ffn1_norm, ffn1_w1, ffn1_w3, ffn1_w2, mix_norm, mem_norm, w_in, w_mem_kv, qn_dsa, kn_dsa, qn_mem, kn_mem, w_branch_sb, w_branch_dsa, w_branch_mem, w_gate, b_gate, w_out, ffn2_norm, ffn2_w1, ffn2_w3, ffn2_w2, loss_target, m_ffn1_norm, m_ffn1_w1, m_ffn1_w3, m_ffn1_w2, m_mix_norm, m_mem_norm, m_w_in, m_w_mem_kv, m_qn_dsa, m_kn_dsa, m_qn_mem, m_kn_mem, m_w_branch_sb, m_w_branch_dsa, m_w_branch_mem, m_w_gate, m_b_gate, m_w_out, m_ffn2_norm, m_ffn2_w1, m_ffn2_w3, m_ffn2_w2, v_ffn1_norm, v_ffn1_w1, v_ffn1_w3, v_ffn1_w2, v_mix_norm, v_mem_norm, v_w_in, v_w_mem_kv, v_qn_dsa, v_kn_dsa, v_qn_mem, v_kn_mem, v_w_branch_sb, v_w_branch_dsa, v_w_branch_mem, v_w_gate, v_b_gate, v_w_out, v_ffn2_norm, v_ffn2_w1, v_ffn2_w3, v_ffn2_w2):
    given = dict(zip(INPUTS, (x, mem, ffn1_norm, ffn1_w1, ffn1_w3, ffn1_w2, mix_norm, mem_norm, w_in, w_mem_kv, qn_dsa, kn_dsa, qn_mem, kn_mem, w_branch_sb, w_branch_dsa, w_branch_mem, w_gate, b_gate, w_out, ffn2_norm, ffn2_w1, ffn2_w3, ffn2_w2, loss_target, m_ffn1_norm, m_ffn1_w1, m_ffn1_w3, m_ffn1_w2, m_mix_norm, m_mem_norm, m_w_in, m_w_mem_kv, m_qn_dsa, m_kn_dsa, m_qn_mem, m_kn_mem, m_w_branch_sb, m_w_branch_dsa, m_w_branch_mem, m_w_gate, m_b_gate, m_w_out, m_ffn2_norm, m_ffn2_w1, m_ffn2_w3, m_ffn2_w2, v_ffn1_norm, v_ffn1_w1, v_ffn1_w3, v_ffn1_w2, v_mix_norm, v_mem_norm, v_w_in, v_w_mem_kv, v_qn_dsa, v_kn_dsa, v_qn_mem, v_kn_mem, v_w_branch_sb, v_w_branch_dsa, v_w_branch_mem, v_w_gate, v_b_gate, v_w_out, v_ffn2_norm, v_ffn2_w1, v_ffn2_w3, v_ffn2_w2), strict=True))
    wl = {n: given[n][0] for n in BIG}
    ws = {n: given[n] for n in SMALL}

    gathered = _gather_two_level(_pack_rows({n: wl[n].astype(BF16) for n in BIG}, BIG), name="gather_weights")
    whole = _unpack_gathered(gathered, wl, BIG)
    loss, dx, g = _local_step(x[0], mem[0], {**whole, **ws}, loss_target[0])

    recv = _exchange(_pack_for_owners(g, wl, BIG).astype(BF16), True, name="scatter_grads")
    big = _adamw(recv, _pack_rows(wl, BIG), _pack_rows({n: given['m_' + n][0] for n in BIG}, BIG),
                 _pack_rows({n: given['v_' + n][0] for n in BIG}, BIG), name="adamw_sharded")
    big = [_unpack_rows(t, wl, BIG) for t in big]

    gs = _pack_small(g, SMALL, SMALL_PAD_ROWS)
    loss_row = gs.shape[0] - SMALL_PAD_ROWS
    gs = gs.at[loss_row, 0].set(loss[0, 0])
    recv_s = _exchange(gs, False, name="gather_small")
    small = _adamw(recv_s, _pack_small(ws, SMALL, SMALL_PAD_ROWS), _pack_small({n: given['m_' + n] for n in SMALL}, SMALL, SMALL_PAD_ROWS),
                   _pack_small({n: given['v_' + n] for n in SMALL}, SMALL, SMALL_PAD_ROWS), name="adamw_replicated")
    total_loss = small[0][loss_row, 0]
    small = [_unpack_small(t, ws, SMALL) for t in small]

    outs = [total_loss, dx[None]]
    for kind in range(4):
        outs += [big[kind][n][None] if n in wl else small[kind][n] for n in WEIGHTS]
    return tuple(outs)
```

```python
import functools
import math

import jax
import jax.numpy as jnp
from jax import lax
from jax.experimental import pallas as pl
from jax.experimental.pallas import tpu as pltpu

F32 = jnp.float32
BF16 = jnp.bfloat16
MXU_DT = jnp.bfloat16

N_DEV = 8
HEAD_DIM = 64
SB_HEADS = 8
DSA_GROUPS = ((128, 1), (512, 4), (2048, 16))
DSA_HPG = 4
MEM_HEADS = 4
SB_W = SB_HEADS * HEAD_DIM
DSA_W = DSA_HPG * len(DSA_GROUPS) * HEAD_DIM
DSA_OUT_W = DSA_HPG * HEAD_DIM
MEM_W = MEM_HEADS * HEAD_DIM
ROPE_THETA = 10000.0
NORM_EPS = 1e-6
QB = 128
SCALE = HEAD_DIM ** -0.5
ADAM_LR, ADAM_B1, ADAM_B2, ADAM_EPS, ADAM_WD, ADAM_STEP = 0.001, 0.9, 0.999, 1e-08, 0.01, 10

LANES = 128
VMEM_LIMIT = 48 * 1024 * 1024
SB_DEAD = -110.0 * 1.4426950408889634

WEIGHTS = ['ffn1_norm', 'ffn1_w1', 'ffn1_w3', 'ffn1_w2', 'mix_norm', 'mem_norm', 'w_in', 'w_mem_kv', 'qn_dsa', 'kn_dsa',
           'qn_mem', 'kn_mem', 'w_branch_sb', 'w_branch_dsa', 'w_branch_mem', 'w_gate', 'b_gate', 'w_out', 'ffn2_norm',
           'ffn2_w1', 'ffn2_w3', 'ffn2_w2']
SHARD_AXIS = {'ffn1_norm': None, 'ffn1_w1': 1, 'ffn1_w3': 1, 'ffn1_w2': 0, 'mix_norm': None, 'mem_norm': None, 'w_in': 1,
              'w_mem_kv': 0, 'qn_dsa': None, 'kn_dsa': None, 'qn_mem': None, 'kn_mem': None, 'w_branch_sb': 1,
              'w_branch_dsa': 1, 'w_branch_mem': 1, 'w_gate': 1, 'b_gate': None, 'w_out': 0, 'ffn2_norm': None,
              'ffn2_w1': 1, 'ffn2_w3': 1, 'ffn2_w2': 0}
BIG = [n for n in WEIGHTS if SHARD_AXIS[n] is not None]
SMALL = [n for n in WEIGHTS if SHARD_AXIS[n] is None]


def _pcall(kern, **kw):
    return pl.pallas_call(kern, **kw)


def _params(*sem):
    return pltpu.CompilerParams(dimension_semantics=sem, vmem_limit_bytes=VMEM_LIMIT)


def _dot(a, b, dims):
    return lax.dot_general(a.astype(MXU_DT), b.astype(MXU_DT), (dims, ((), ())), preferred_element_type=F32)


def _nn(a, b):
    return _dot(a, b, ((1,), (0,)))


def _nt(a, b):
    return _dot(a, b, ((1,), (1,)))


def _tn(a, b):
    return _dot(a, b, ((0,), (0,)))


def _pick(n, prefs):
    for p in prefs:
        if n % p == 0:
            return p
    return n


def _matmul(a, b, *, name, ta=False, tb=False, out_dtype=F32, res=None, alpha=1.0, tm=1024, tn=512, tk=1024, pair2=None):
    if ta:
        kdim, m = a.shape
    else:
        m, kdim = a.shape
    n = b.shape[0] if tb else b.shape[1]
    tm = _pick(m, (tm, 512, 256, 128))
    tn = _pick(n, (tn, 512, 384, 256, 128))
    tk = _pick(kdim, (tk, 1024, 512, 256, 128))
    nk = kdim // tk
    a_spec = pl.BlockSpec((tk, tm), lambda i, j, k: (k, i)) if ta else pl.BlockSpec((tm, tk), lambda i, j, k: (i, k))
    b_spec = pl.BlockSpec((tn, tk), lambda i, j, k: (j, k)) if tb else pl.BlockSpec((tk, tn), lambda i, j, k: (k, j))
    o_spec = pl.BlockSpec((tm, tn), lambda i, j, k: (i, j))
    dims = ((0 if ta else 1,), (1 if tb else 0,))

    def kern(*refs):
        refs = list(refs)
        acc_ref = refs.pop() if nk > 1 else None
        o_ref = refs.pop()
        r_ref = refs.pop() if res is not None else None
        k = pl.program_id(2)

        part = _dot(refs[0][...], refs[1][...], dims)
        if pair2 is not None:
            part = part + _dot(refs[2][...], refs[3][...], dims)

        def finish(r):
            if alpha != 1.0:
                r = r * alpha
            if r_ref is not None:
                r = r_ref[...] + r
            o_ref[...] = r.astype(out_dtype)

        if nk == 1:
            finish(part)
            return

        @pl.when(k == 0)
        def _():
            acc_ref[...] = part

        @pl.when(k > 0)
        def _():
            acc_ref[...] += part

        @pl.when(k == nk - 1)
        def _():
            finish(acc_ref[...])

    ins = [a, b] + ([] if pair2 is None else list(pair2)) + ([] if res is None else [res])
    specs = [a_spec, b_spec] + ([] if pair2 is None else [a_spec, b_spec]) + ([] if res is None else [o_spec])
    return _pcall(kern, name=name, grid=(m // tm, n // tn, nk), in_specs=specs, out_specs=o_spec,
                  out_shape=jax.ShapeDtypeStruct((m, n), out_dtype), scratch_shapes=[pltpu.VMEM((tm, tn), F32)] if nk > 1 else [],
                  compiler_params=_params("parallel", "parallel", "arbitrary"))(*ins)


def _rms_fwd(x, g, *, name):
    s, d = x.shape
    ts = _pick(s, (512, 256))

    def kern(x_ref, g_ref, h_ref):
        xf = x_ref[...]
        r = lax.rsqrt(jnp.mean(xf * xf, axis=-1, keepdims=True) + NORM_EPS)
        h_ref[...] = (xf * r * g_ref[...]).astype(h_ref.dtype)

    return _pcall(kern, name=name, grid=(s // ts,),
                  in_specs=[pl.BlockSpec((ts, d), lambda i: (i, 0)), pl.BlockSpec((1, d), lambda i: (0, 0))],
                  out_specs=pl.BlockSpec((ts, d), lambda i: (i, 0)), out_shape=jax.ShapeDtypeStruct((s, d), BF16),
                  compiler_params=_params("parallel"))(x, g)


def _rms_bwd(x, g, dh, res, *, name):
    s, d = x.shape
    ts = _pick(s, (512, 256))

    def kern(*refs):
        if res is None:
            x_ref, g_ref, dh_ref, dx_ref, dxb_ref, dg_ref = refs
            r_ref = None
        else:
            x_ref, g_ref, dh_ref, r_ref, dx_ref, dxb_ref, dg_ref = refs
        xf = x_ref[...]
        r = lax.rsqrt(jnp.mean(xf * xf, axis=-1, keepdims=True) + NORM_EPS)
        xh = xf * r
        dhf = dh_ref[...].astype(F32)
        dy = dhf * g_ref[...]
        dx = r * (dy - xh * jnp.mean(dy * xh, axis=-1, keepdims=True))
        if r_ref is not None:
            dx = r_ref[...] + dx
        dx_ref[...] = dx
        dxb_ref[...] = dx.astype(dxb_ref.dtype)

        @pl.when(pl.program_id(0) == 0)
        def _():
            dg_ref[...] = jnp.zeros_like(dg_ref)

        dg_ref[...] += jnp.sum(dhf * xh, axis=0, keepdims=True)

    row = pl.BlockSpec((ts, d), lambda i: (i, 0))
    vec = pl.BlockSpec((1, d), lambda i: (0, 0))
    ins = [x, g, dh] + ([] if res is None else [res])
    return _pcall(kern, name=name, grid=(s // ts,), in_specs=[row, vec, row] + ([] if res is None else [row]),
                  out_specs=[row, row, vec],
                  out_shape=[jax.ShapeDtypeStruct((s, d), F32), jax.ShapeDtypeStruct((s, d), BF16), jax.ShapeDtypeStruct((1, d), F32)],
                  compiler_params=_params("arbitrary"))(*ins)


def _sigmoid(x):
    return 1.0 / (1.0 + jnp.exp(-x))


FFN_TM, FFN_TF = 512, 1408


def _ffn_up(h, w1, w3, *, name):
    s, d = h.shape
    fdim = w1.shape[1]
    tm, tf = _pick(s, (FFN_TM, 256)), _pick(fdim, (FFN_TF, 512, 256, 128))

    def kern(h_ref, w1_ref, w3_ref, a_ref, b_ref, f_ref):
        hb = h_ref[...]
        a = _nn(hb, w1_ref[...])
        b = _nn(hb, w3_ref[...])
        a_ref[...] = a.astype(a_ref.dtype)
        b_ref[...] = b.astype(b_ref.dtype)
        f_ref[...] = (a * _sigmoid(a) * b).astype(f_ref.dtype)

    wspec = pl.BlockSpec((d, tf), lambda i, j: (0, j))
    ospec = pl.BlockSpec((tm, tf), lambda i, j: (i, j))
    shp = jax.ShapeDtypeStruct((s, fdim), BF16)
    return _pcall(kern, name=name, grid=(s // tm, fdim // tf), in_specs=[pl.BlockSpec((tm, d), lambda i, j: (i, 0)), wspec, wspec],
                  out_specs=[ospec, ospec, ospec], out_shape=[shp, shp, shp],
                  compiler_params=_params("parallel", "parallel"))(h, w1, w3)


def _ffn_dact(dy, w2, a, b, *, name):
    s, d = dy.shape
    fdim = w2.shape[0]
    tm, tf = _pick(s, (FFN_TM, 256)), _pick(fdim, (FFN_TF, 512, 256, 128))

    def kern(dy_ref, w2_ref, a_ref, b_ref, da_ref, db_ref):
        df = _nt(dy_ref[...], w2_ref[...]) * 0.5
        av = a_ref[...].astype(F32)
        sg = _sigmoid(av)
        da_ref[...] = (df * b_ref[...].astype(F32) * (sg + av * sg * (1.0 - sg))).astype(da_ref.dtype)
        db_ref[...] = (df * (av * sg)).astype(db_ref.dtype)

    ospec = pl.BlockSpec((tm, tf), lambda i, j: (i, j))
    shp = jax.ShapeDtypeStruct((s, fdim), BF16)
    return _pcall(kern, name=name, grid=(s // tm, fdim // tf),
                  in_specs=[pl.BlockSpec((tm, d), lambda i, j: (i, 0)), pl.BlockSpec((tf, d), lambda i, j: (j, 0)), ospec, ospec],
                  out_specs=[ospec, ospec], out_shape=[shp, shp], compiler_params=_params("parallel", "parallel"))(dy, w2, a, b)


def _loss_head(y, t, *, name):
    s, d = y.shape
    ts = _pick(s, (512, 256))
    n = s // ts

    def kern(y_ref, t_ref, dy_ref, dyb_ref, l_ref, acc_ref):
        i = pl.program_id(0)

        @pl.when(i == 0)
        def _():
            acc_ref[...] = jnp.zeros_like(acc_ref)

        e = y_ref[...] - t_ref[...]
        dy_ref[...] = e / d
        dyb_ref[...] = (e / d).astype(dyb_ref.dtype)
        acc_ref[...] += jnp.sum(e * e, axis=0, keepdims=True)

        @pl.when(i == n - 1)
        def _():
            l_ref[...] = jnp.sum(acc_ref[...], axis=1, keepdims=True) * (0.5 / d)

    row = pl.BlockSpec((ts, d), lambda i: (i, 0))
    return _pcall(kern, name=name, grid=(n,), in_specs=[row, row], out_specs=[row, row, pl.BlockSpec((1, 1), lambda i: (0, 0))],
                  out_shape=[jax.ShapeDtypeStruct((s, d), F32), jax.ShapeDtypeStruct((s, d), BF16), jax.ShapeDtypeStruct((1, 1), F32)],
                  scratch_shapes=[pltpu.VMEM((1, d), F32)], compiler_params=_params("arbitrary"))(y, t)


def _head_mean(v, bd):
    hi = v.astype(BF16)
    lo = (v - hi.astype(F32)).astype(BF16)
    return (lax.dot_general(hi, bd, (((1,), (0,)), ((), ())), preferred_element_type=F32)
            + lax.dot_general(lo, bd, (((1,), (0,)), ((), ())), preferred_element_type=F32))


def _partner(v):
    w = v.shape[1]
    lane = lax.broadcasted_iota(jnp.int32, v.shape, 1)
    return jnp.where(lane % HEAD_DIM < HEAD_DIM // 2, pltpu.roll(v, w - HEAD_DIM // 2, 1), pltpu.roll(v, HEAD_DIM // 2, 1))


def _block_diag(w):
    r = lax.broadcasted_iota(jnp.int32, (w, w), 0) // HEAD_DIM
    c = lax.broadcasted_iota(jnp.int32, (w, w), 1) // HEAD_DIM
    return jnp.where(r == c, 1.0 / HEAD_DIM, 0.0).astype(BF16)


def _rope_tables(s):
    half = HEAD_DIM // 2
    inv_freq = jnp.power(ROPE_THETA, -jnp.arange(half, dtype=F32) / half)
    ang = jnp.arange(s).astype(F32)[:, None] * inv_freq[None, :]
    cos, sin = jnp.cos(ang), jnp.sin(ang)
    cos2 = jnp.concatenate([cos, cos, cos, cos], axis=1)
    sin2 = jnp.concatenate([-sin, sin, -sin, sin], axis=1)
    return cos2, sin2


def _qknorm_fwd(src, col0, width, gain, rope, *, name, out_dtype=BF16):
    s = src.shape[0]
    ts = _pick(s, (512, 256))
    cb = col0 // width
    assert col0 % width == 0
    reps = width // LANES
    g = jnp.tile(gain, (1, width // HEAD_DIM))

    def kern(*refs):
        if rope is None:
            x_ref, g_ref, o_ref = refs
        else:
            x_ref, g_ref, c_ref, s_ref, o_ref = refs
        x = x_ref[...].astype(F32)
        bd = _block_diag(width)
        r = lax.rsqrt(_head_mean(x * x, bd) + NORM_EPS)
        y = x * r * g_ref[...]
        if rope is not None:
            y = y * jnp.tile(c_ref[...], (1, reps)) + _partner(y) * jnp.tile(s_ref[...], (1, reps))
        o_ref[...] = y.astype(o_ref.dtype)

    xs = pl.BlockSpec((ts, width), lambda i: (i, cb))
    tab = pl.BlockSpec((ts, LANES), lambda i: (i, 0))
    ins = [src, g] + ([] if rope is None else list(rope))
    specs = [xs, pl.BlockSpec((1, width), lambda i: (0, 0))] + ([] if rope is None else [tab, tab])
    return _pcall(kern, name=name, grid=(s // ts,), in_specs=specs, out_specs=pl.BlockSpec((ts, width), lambda i: (i, 0)),
                  out_shape=jax.ShapeDtypeStruct((s, width), out_dtype), compiler_params=_params("parallel"))(*ins)


def _qknorm_bwd(src, col0, width, gain, rope, dout, *, name):
    s = src.shape[0]
    ts = _pick(s, (512, 256))
    cb = col0 // width
    reps = width // LANES
    g = jnp.tile(gain, (1, width // HEAD_DIM))

    douts = list(dout) if isinstance(dout, (list, tuple)) else [dout]
    piece = width // len(douts)

    def kern(*refs):
        refs = list(refs)
        dg_ref = refs.pop()
        dx_ref = refs.pop()
        do_refs = [refs.pop() for _ in douts][::-1]
        if rope is None:
            x_ref, g_ref = refs
        else:
            x_ref, g_ref, c_ref, s_ref = refs
        x = x_ref[...].astype(F32)
        bd = _block_diag(width)
        r = lax.rsqrt(_head_mean(x * x, bd) + NORM_EPS)
        xh = x * r
        dy = jnp.concatenate([d[...].astype(F32) for d in do_refs], axis=1) if len(do_refs) > 1 else do_refs[0][...].astype(F32)
        if rope is not None:
            dy = dy * jnp.tile(c_ref[...], (1, reps)) + _partner(dy * jnp.tile(s_ref[...], (1, reps)))
        dxh = dy * g_ref[...]
        dx_ref[...] = (r * (dxh - xh * _head_mean(dxh * xh, bd))).astype(dx_ref.dtype)

        @pl.when(pl.program_id(0) == 0)
        def _():
            dg_ref[...] = jnp.zeros_like(dg_ref)

        dg_ref[...] += jnp.sum(dy * xh, axis=0, keepdims=True)

    xs = pl.BlockSpec((ts, width), lambda i: (i, cb))
    row = pl.BlockSpec((ts, width), lambda i: (i, 0))
    vec = pl.BlockSpec((1, width), lambda i: (0, 0))
    tab = pl.BlockSpec((ts, LANES), lambda i: (i, 0))
    ins = [src, g] + ([] if rope is None else list(rope)) + douts
    specs = [xs, vec] + ([] if rope is None else [tab, tab]) + [pl.BlockSpec((ts, piece), lambda i: (i, 0))] * len(douts)
    dx, dg = _pcall(kern, name=name, grid=(s // ts,), in_specs=specs, out_specs=[row, vec],
                    out_shape=[jax.ShapeDtypeStruct((s, width), BF16), jax.ShapeDtypeStruct((1, width), F32)],
                    compiler_params=_params("arbitrary"))(*ins)
    return dx, jnp.sum(dg.reshape(width // HEAD_DIM, HEAD_DIM), axis=0, keepdims=True)


def _tri(strict):
    r = lax.broadcasted_iota(jnp.int32, (2 * QB, QB), 0) % QB
    c = lax.broadcasted_iota(jnp.int32, (2 * QB, QB), 1)
    return jnp.where((r > c) if strict else (r >= c), 1.0, 0.0).astype(BF16)


def _split_dot(v, t2):
    hi = v.astype(BF16)
    lo = (v - hi.astype(F32)).astype(BF16)
    return lax.dot_general(jnp.concatenate([hi, lo], axis=1), t2, (((1,), (0,)), ((), ())), preferred_element_type=F32)


LOG2E = 1.4426950408889634


def _log2_sigmoids(z2):
    lf = -(jnp.maximum(z2, 0.0) + jnp.log2(1.0 + jnp.exp2(-jnp.abs(z2))))
    return z2 + lf, lf


def _key_blocks(t):
    s = t.shape[0]
    n = t.shape[1] // HEAD_DIM
    return t.reshape(s // QB, QB, n, HEAD_DIM).transpose(2, 0, 3, 1)


def _from_key_blocks(t):
    n, nb, hd, qb = t.shape
    return t.transpose(1, 3, 0, 2).reshape(nb * qb, n * hd)


SB_SUB = 4
SB2_SUB = 2


def _first_half(shape):
    return lax.broadcasted_iota(jnp.int32, shape, 1) < HEAD_DIM


def _split_pair(t, first):
    zero = jnp.zeros_like(t)
    return [jnp.where(first, t, zero), jnp.where(first, zero, t)]


def _sb2_fwd(p, *, name):
    s = p.shape[0]
    rq = SB2_SUB * QB
    nq = s // rq
    npair = SB_W // LANES

    def kern(q_ref, k_ref, v_ref, o_ref):
        i = pl.program_id(1)
        first = _first_half((rq, LANES))
        qs = _split_pair(q_ref[...], first)
        t2 = _tri(True)
        rel = lax.broadcasted_iota(jnp.int32, (rq, QB), 1) - lax.broadcasted_iota(jnp.int32, (rq, QB), 0)

        def tile(j, carries, accs, masked):
            off = pl.multiple_of(j * QB, QB)
            kt = k_ref[pl.ds(off, QB), :]
            vt = v_ref[pl.ds(off, QB), :]
            out_c, out_a = [], []
            for e in range(2):
                ls, lf = _log2_sigmoids(_nt(qs[e], kt) * (SCALE * LOG2E))
                if masked:
                    before = rel < i * rq - j * QB
                    lf = jnp.where(before, lf, 0.0)
                w = jnp.exp2(ls + _split_dot(lf, t2) + carries[e])
                if masked:
                    w = jnp.where(before, w, 0.0)
                out_c.append(carries[e] + jnp.sum(lf, axis=1, keepdims=True))
                out_a.append(accs[e] + _nn(w, vt))
            return out_c, out_a

        carries = [jnp.zeros((rq, 1), F32)] * 2
        accs = [jnp.zeros((rq, LANES), F32)] * 2
        for a in range(SB2_SUB):
            carries, accs = tile(i * SB2_SUB + (SB2_SUB - 1 - a), carries, accs, True)

        def cond(st):
            return jnp.logical_and(st[0] >= 0, st[1] > 0)

        def body(st):
            carries, accs = tile(st[0], [st[2], st[3]], [st[4], st[5]], False)
            alive = jnp.maximum(jnp.max(carries[0]), jnp.max(carries[1])) > SB_DEAD
            return st[0] - 1, alive.astype(jnp.int32), carries[0], carries[1], accs[0], accs[1]

        st = lax.while_loop(cond, body, (i * SB2_SUB - 1, jnp.int32(1), carries[0], carries[1], accs[0], accs[1]))
        o_ref[...] = jnp.where(first, st[4], st[5])

    return _pcall(kern, name=name, grid=(npair, nq),
                  in_specs=[pl.BlockSpec((rq, LANES), lambda a, i: (i, a)), pl.BlockSpec((s, LANES), lambda a, i: (0, npair + a)),
                            pl.BlockSpec((s, LANES), lambda a, i: (0, 2 * npair + a))],
                  out_specs=pl.BlockSpec((rq, LANES), lambda a, i: (i, a)), out_shape=jax.ShapeDtypeStruct((s, SB_W), F32),
                  compiler_params=_params("parallel", "arbitrary"))(p, p, p)


def _sb2_bwd(p, o, do, *, name):
    s = p.shape[0]
    rq = SB2_SUB * QB
    nq = s // rq
    npair = SB_W // LANES

    def kern(q_ref, k_ref, v_ref, o_ref, do_ref, dq_ref, dk_hbm, dv_hbm, dk_acc, dv_acc, sem):
        pr = pl.program_id(0)
        i = pl.program_id(1)

        @pl.when(i == 0)
        def _():
            dk_acc[...] = jnp.zeros_like(dk_acc)
            dv_acc[...] = jnp.zeros_like(dv_acc)

        first = _first_half((rq, LANES))
        qs = _split_pair(q_ref[...], first)
        do2 = do_ref[...]
        dos = _split_pair(do2, first)
        prod = do2.astype(F32) * o_ref[...]
        dsums = [jnp.sum(jnp.where(first, prod, 0.0), axis=1, keepdims=True),
                 jnp.sum(jnp.where(first, 0.0, prod), axis=1, keepdims=True)]
        t_strict = _tri(True)
        t_incl = _tri(False)
        rel = lax.broadcasted_iota(jnp.int32, (rq, QB), 1) - lax.broadcasted_iota(jnp.int32, (rq, QB), 0)

        def tile(j, carries, gcarries, dqs, masked):
            off = pl.multiple_of(j * QB, QB)
            kt = k_ref[pl.ds(off, QB), :]
            vt = v_ref[pl.ds(off, QB), :]
            out_c, out_g, out_q = [], [], []
            dk_t = jnp.zeros((QB, LANES), F32)
            dv_t = jnp.zeros((QB, LANES), F32)
            for e in range(2):
                ls, lf = _log2_sigmoids(_nt(qs[e], kt) * (SCALE * LOG2E))
                if masked:
                    before = rel < i * rq - j * QB
                    lf = jnp.where(before, lf, 0.0)
                w = jnp.exp2(ls + _split_dot(lf, t_strict) + carries[e])
                if masked:
                    w = jnp.where(before, w, 0.0)
                wr = w.astype(MXU_DT)
                g = _nt(dos[e], vt) * wr.astype(F32)
                big_g = dsums[e] - (_split_dot(g, t_incl) + gcarries[e])
                sig = jnp.exp2(ls)
                dz = g * (1.0 - sig) - sig * big_g
                if masked:
                    dz = jnp.where(before, dz, 0.0)
                dz = dz * SCALE
                dk_t = dk_t + _tn(dz, qs[e])
                dv_t = dv_t + _tn(wr, dos[e])
                out_c.append(carries[e] + jnp.sum(lf, axis=1, keepdims=True))
                out_g.append(gcarries[e] + jnp.sum(g, axis=1, keepdims=True))
                out_q.append(dqs[e] + _nn(dz, kt))
            dk_acc[pl.ds(off, QB), :] += dk_t
            dv_acc[pl.ds(off, QB), :] += dv_t
            return out_c, out_g, out_q

        zc = [jnp.zeros((rq, 1), F32)] * 2
        carries, gcarries, dqs = zc, zc, [jnp.zeros((rq, LANES), F32)] * 2
        for a in range(SB2_SUB):
            carries, gcarries, dqs = tile(i * SB2_SUB + (SB2_SUB - 1 - a), carries, gcarries, dqs, True)

        def cond(st):
            return jnp.logical_and(st[0] >= 0, st[1] > 0)

        def body(st):
            carries, gcarries, dqs = tile(st[0], [st[2], st[3]], [st[4], st[5]], [st[6], st[7]], False)
            alive = jnp.maximum(jnp.max(carries[0]), jnp.max(carries[1])) > SB_DEAD
            return (st[0] - 1, alive.astype(jnp.int32), carries[0], carries[1], gcarries[0], gcarries[1], dqs[0], dqs[1])

        st = lax.while_loop(cond, body, (i * SB2_SUB - 1, jnp.int32(1), carries[0], carries[1], gcarries[0], gcarries[1],
                                         dqs[0], dqs[1]))
        dq_ref[...] = jnp.where(first, st[6], st[7])

        @pl.when(i == nq - 1)
        def _():
            cols = pl.ds(pl.multiple_of(pr * LANES, LANES), LANES)
            ck = pltpu.make_async_copy(dk_acc, dk_hbm.at[:, cols], sem.at[0])
            cv = pltpu.make_async_copy(dv_acc, dv_hbm.at[:, cols], sem.at[1])
            ck.start()
            cv.start()
            ck.wait()
            cv.wait()

    blk = pl.BlockSpec((rq, LANES), lambda a, i: (i, a))
    anyspace = pl.BlockSpec(memory_space=pl.ANY)
    shp = jax.ShapeDtypeStruct((s, SB_W), F32)
    return _pcall(kern, name=name, grid=(npair, nq),
                  in_specs=[blk, pl.BlockSpec((s, LANES), lambda a, i: (0, npair + a)),
                            pl.BlockSpec((s, LANES), lambda a, i: (0, 2 * npair + a)), blk, blk],
                  out_specs=[blk, anyspace, anyspace], out_shape=[shp, shp, shp],
                  scratch_shapes=[pltpu.VMEM((s, LANES), F32), pltpu.VMEM((s, LANES), F32), pltpu.SemaphoreType.DMA((2,))],
                  compiler_params=_params("arbitrary", "arbitrary"))(p, p, p, o, do)


def _sb_fwd(q, kt, vt, *, name):
    h, s, hd = q.shape
    rq = SB_SUB * QB
    nq = s // rq
    nb = s // QB

    def kern(q_ref, k_ref, v_ref, o_ref):
        i = pl.program_id(1)
        qb = q_ref[...]
        t2 = _tri(True)
        rel = lax.broadcasted_iota(jnp.int32, (rq, QB), 1) - lax.broadcasted_iota(jnp.int32, (rq, QB), 0)

        def tile(j, carry, acc, masked):
            ls, lf = _log2_sigmoids(_nn(qb, k_ref[j]) * (SCALE * LOG2E))
            if masked:
                before = rel < i * rq - j * QB
                lf = jnp.where(before, lf, 0.0)
            w = jnp.exp2(ls + _split_dot(lf, t2) + carry)
            if masked:
                w = jnp.where(before, w, 0.0)
            return carry + jnp.sum(lf, axis=1, keepdims=True), acc + _nt(w, v_ref[j])

        carry, acc = jnp.zeros((rq, 1), F32), jnp.zeros((rq, hd), F32)
        for a in range(SB_SUB):
            carry, acc = tile(i * SB_SUB + (SB_SUB - 1 - a), carry, acc, True)

        def cond(st):
            return jnp.logical_and(st[0] >= 0, st[1] > 0)

        def body(st):
            j, _, carry, acc = st
            carry, acc = tile(j, carry, acc, False)
            return j - 1, (jnp.max(carry) > SB_DEAD).astype(jnp.int32), carry, acc

        _, _, _, acc = lax.while_loop(cond, body, (i * SB_SUB - 1, jnp.int32(1), carry, acc))
        o_ref[...] = acc

    blk = pl.BlockSpec((None, rq, hd), lambda a, i: (a, i, 0))
    full = pl.BlockSpec((None, nb, hd, QB), lambda a, i: (a, 0, 0, 0))
    return _pcall(kern, name=name, grid=(h, nq), in_specs=[blk, full, full], out_specs=blk,
                  out_shape=jax.ShapeDtypeStruct((h, s, hd), F32), compiler_params=_params("parallel", "arbitrary"))(q, kt, vt)


def _sb_bwd(q, kt, vt, o, do, *, name):
    h, s, hd = q.shape
    rq = SB_SUB * QB
    nq = s // rq
    nb = s // QB

    def kern(q_ref, k_ref, v_ref, o_ref, do_ref, dq_ref, dk_ref, dv_ref):
        i = pl.program_id(1)

        @pl.when(i == 0)
        def _():
            dk_ref[...] = jnp.zeros_like(dk_ref)
            dv_ref[...] = jnp.zeros_like(dv_ref)

        qb = q_ref[...]
        dob = do_ref[...]
        dsum = jnp.sum(dob.astype(F32) * o_ref[...], axis=1, keepdims=True)
        t_strict = _tri(True)
        t_incl = _tri(False)
        rel = lax.broadcasted_iota(jnp.int32, (rq, QB), 1) - lax.broadcasted_iota(jnp.int32, (rq, QB), 0)

        def tile(j, carry, gcarry, dq, masked):
            kb = k_ref[j]
            ls, lf = _log2_sigmoids(_nn(qb, kb) * (SCALE * LOG2E))
            if masked:
                before = rel < i * rq - j * QB
                lf = jnp.where(before, lf, 0.0)
            w = jnp.exp2(ls + _split_dot(lf, t_strict) + carry)
            if masked:
                w = jnp.where(before, w, 0.0)
            wr = w.astype(MXU_DT)
            g = _nn(dob, v_ref[j]) * wr.astype(F32)
            big_g = dsum - (_split_dot(g, t_incl) + gcarry)
            sig = jnp.exp2(ls)
            dz = g * (1.0 - sig) - sig * big_g
            if masked:
                dz = jnp.where(before, dz, 0.0)
            dz = dz * SCALE
            dk_ref[j] += _tn(qb, dz)
            dv_ref[j] += _tn(dob, wr)
            return (carry + jnp.sum(lf, axis=1, keepdims=True), gcarry + jnp.sum(g, axis=1, keepdims=True),
                    dq + _nt(dz, kb))

        carry, gcarry, dq = jnp.zeros((rq, 1), F32), jnp.zeros((rq, 1), F32), jnp.zeros((rq, hd), F32)
        for a in range(SB_SUB):
            carry, gcarry, dq = tile(i * SB_SUB + (SB_SUB - 1 - a), carry, gcarry, dq, True)

        def cond(st):
            return jnp.logical_and(st[0] >= 0, st[1] > 0)

        def body(st):
            j, _, carry, gcarry, dq = st
            carry, gcarry, dq = tile(j, carry, gcarry, dq, False)
            return j - 1, (jnp.max(carry) > SB_DEAD).astype(jnp.int32), carry, gcarry, dq

        st = lax.while_loop(cond, body, (i * SB_SUB - 1, jnp.int32(1), carry, gcarry, dq))
        dq_ref[...] = st[4]

    blk = pl.BlockSpec((None, rq, hd), lambda a, i: (a, i, 0))
    full = pl.BlockSpec((None, nb, hd, QB), lambda a, i: (a, 0, 0, 0))
    kshape = jax.ShapeDtypeStruct((h, nb, hd, QB), F32)
    return _pcall(kern, name=name, grid=(h, nq), in_specs=[blk, full, full, blk, blk], out_specs=[blk, full, full],
                  out_shape=[jax.ShapeDtypeStruct((h, s, hd), F32), kshape, kshape],
                  compiler_params=_params("parallel", "arbitrary"))(q, kt, vt, o, do)


DSA_SUB = 4


def _dsa_seq_blocks(t, s):
    steps_per_group = 4 * s // (QB * DSA_SUB)
    g = t // steps_per_group
    b0, b1, b2 = (s // (QB * r) for _, r in DSA_GROUPS)
    return jnp.where(g == 0, b0, jnp.where(g == 1, b1, b2))


def _dsa_rel():
    qi = lax.broadcasted_iota(jnp.int32, (QB, QB), 0)
    kj = lax.broadcasted_iota(jnp.int32, (QB, QB), 1)
    return kj - qi


def _prev_mask(rel, has_prev):
    return rel >= jnp.where(has_prev, 0, QB)


def _dsa_fwd(q, k, vp, *, name):
    rows = q.shape[0]
    s = rows // 12
    big = QB * DSA_SUB
    nsteps = rows // big

    def kern(q_ref, k_ref, kp_ref, v_ref, vpv_ref, o_ref):
        t = pl.program_id(0)
        bps = _dsa_seq_blocks(t, s)
        rel = _dsa_rel()
        lane = lax.broadcasted_iota(jnp.int32, (QB, LANES), 1)
        for a in range(DSA_SUB):
            qa = q_ref[pl.ds(a * QB, QB), :]
            kc = k_ref[pl.ds(a * QB, QB), :]
            vc = v_ref[pl.ds(a * QB, QB), :]
            if a == 0:
                kpv, vpv = kp_ref[...], vpv_ref[...]
            else:
                kpv, vpv = k_ref[pl.ds((a - 1) * QB, QB), :], v_ref[pl.ds((a - 1) * QB, QB), :]
            has_prev = (t * DSA_SUB + a) % bps != 0
            sc = jnp.where(rel <= 0, _nt(qa, kc) * SCALE, -jnp.inf)
            sp = jnp.where(_prev_mask(rel, has_prev), _nt(qa, kpv) * SCALE, -jnp.inf)
            m = jnp.maximum(jnp.max(sc, axis=1, keepdims=True), jnp.max(sp, axis=1, keepdims=True))
            pc = jnp.exp(sc - m)
            pp = jnp.exp(sp - m)
            den = jnp.sum(pc, axis=1, keepdims=True) + jnp.sum(pp, axis=1, keepdims=True)
            o = (_nn(pc, vc) + _nn(pp, vpv)) / den
            o_ref[pl.ds(a * QB, QB), :] = jnp.where(lane < HEAD_DIM, o, m + jnp.log(den))

    cur64 = pl.BlockSpec((big, HEAD_DIM), lambda t: (t, 0))
    prev64 = pl.BlockSpec((QB, HEAD_DIM), lambda t: (jnp.maximum(t * DSA_SUB - 1, 0), 0))
    cur128 = pl.BlockSpec((big, LANES), lambda t: (t, 0))
    prev128 = pl.BlockSpec((QB, LANES), lambda t: (jnp.maximum(t * DSA_SUB - 1, 0), 0))
    return _pcall(kern, name=name, grid=(nsteps,), in_specs=[cur64, cur64, prev64, cur128, prev128], out_specs=cur128,
                  out_shape=jax.ShapeDtypeStruct((rows, LANES), F32), compiler_params=_params("parallel"))(q, k, k, vp, vp)


def _dsa_combine(p0, p1, p2, *, name):
    hh, s, _ = p0.shape
    ts = _pick(s, (512, 256))

    def kern(a_ref, b_ref, c_ref, o_ref):
        lane = lax.broadcasted_iota(jnp.int32, (ts, LANES), 1)
        xs = [a_ref[...], b_ref[...], c_ref[...]]
        ls = [jnp.where(lane < HEAD_DIM, pltpu.roll(x, HEAD_DIM, 1), x) for x in xs]
        m = jnp.maximum(jnp.maximum(ls[0], ls[1]), ls[2])
        es = [jnp.exp(l - m) for l in ls]
        den = es[0] + es[1] + es[2]
        o = (es[0] * xs[0] + es[1] * xs[1] + es[2] * xs[2]) / den
        o_ref[...] = jnp.where(lane < HEAD_DIM, o, m + jnp.log(den))

    blk = pl.BlockSpec((None, ts, LANES), lambda a, i: (a, i, 0))
    return _pcall(kern, name=name, grid=(hh, s // ts), in_specs=[blk, blk, blk], out_specs=blk,
                  out_shape=jax.ShapeDtypeStruct((hh, s, LANES), F32), compiler_params=_params("parallel", "parallel"))(p0, p1, p2)


def _dsa_bwd_prep(comb, dop, *, name):
    hh, s, _ = comb.shape
    ts = _pick(s, (512, 256))

    def kern(c_ref, d_ref, o_ref):
        lane = lax.broadcasted_iota(jnp.int32, (ts, LANES), 1)
        c = c_ref[...]
        d = d_ref[...]
        dsum = jnp.sum(jnp.where(lane < HEAD_DIM, c * d, 0.0), axis=1, keepdims=True)
        o_ref[...] = jnp.where(lane < HEAD_DIM, d, jnp.where(lane < HEAD_DIM + 32, c, dsum))

    blk = pl.BlockSpec((None, ts, LANES), lambda a, i: (a, i, 0))
    return _pcall(kern, name=name, grid=(hh, s // ts), in_specs=[blk, blk], out_specs=blk,
                  out_shape=jax.ShapeDtypeStruct((hh, s, LANES), F32), compiler_params=_params("parallel", "parallel"))(comb, dop)


def _dsa_bwd(q, k, vp, pk, *, name):
    rows = q.shape[0]
    s = rows // 12
    big = QB * DSA_SUB
    nsteps = rows // big
    nblk = rows // QB

    def kern(q_ref, qn_ref, k_ref, kp_ref, v_ref, vpv_ref, p_ref, pn_ref, dq_ref, dk_ref, dv_ref):
        t = pl.program_id(0)
        bps = _dsa_seq_blocks(t, s)
        rel = _dsa_rel()
        lane = lax.broadcasted_iota(jnp.int32, (QB, LANES), 1)

        def stats(pa):
            lse = jnp.max(jnp.where(jnp.logical_and(lane >= HEAD_DIM, lane < HEAD_DIM + 32), pa, -jnp.inf), axis=1, keepdims=True)
            dsum = jnp.max(jnp.where(lane >= HEAD_DIM + 32, pa, -jnp.inf), axis=1, keepdims=True)
            return lse, dsum

        def pair(qa, pa, st, kb, vb, mask):
            p = jnp.where(mask, jnp.exp(_nt(qa, kb) * SCALE - st[0]), 0.0)
            ds = p * (_nt(pa, vb) - st[1]) * SCALE
            return _nn(ds, kb), _tn(ds, qa), _tn(p, pa)

        for a in range(DSA_SUB):
            qa = q_ref[pl.ds(a * QB, QB), :]
            pa = p_ref[pl.ds(a * QB, QB), :]
            st = stats(pa)
            kc = k_ref[pl.ds(a * QB, QB), :]
            vc = v_ref[pl.ds(a * QB, QB), :]
            if a == 0:
                kpv, vpv = kp_ref[...], vpv_ref[...]
            else:
                kpv, vpv = k_ref[pl.ds((a - 1) * QB, QB), :], v_ref[pl.ds((a - 1) * QB, QB), :]
            has_prev = (t * DSA_SUB + a) % bps != 0
            dq_c, dk_c, dv_c = pair(qa, pa, st, kc, vc, rel <= 0)
            dq_p, dk_p, dv_p = pair(qa, pa, st, kpv, vpv, _prev_mask(rel, has_prev))
            dq_ref[pl.ds(a * QB, QB), :] = dq_c + dq_p
            if a == 0:
                dk_ref[pl.ds(0, QB), :] = dk_c
                dv_ref[pl.ds(0, QB), :] = dv_c
            else:
                dk_ref[pl.ds(a * QB, QB), :] = dk_c
                dv_ref[pl.ds(a * QB, QB), :] = dv_c
                dk_ref[pl.ds((a - 1) * QB, QB), :] += dk_p
                dv_ref[pl.ds((a - 1) * QB, QB), :] += dv_p
        nxt = t * DSA_SUB + DSA_SUB
        has_next = jnp.logical_and(nxt < nblk, nxt % bps != 0)
        last = (DSA_SUB - 1) * QB
        pn = pn_ref[...]
        _, dk_n, dv_n = pair(qn_ref[...], pn, stats(pn), k_ref[pl.ds(last, QB), :], v_ref[pl.ds(last, QB), :],
                             _prev_mask(rel, has_next))
        dk_ref[pl.ds(last, QB), :] += dk_n
        dv_ref[pl.ds(last, QB), :] += dv_n

    def prev_map(t):
        return (jnp.maximum(t * DSA_SUB - 1, 0), 0)

    def next_map(t):
        return (jnp.minimum(t * DSA_SUB + DSA_SUB, nblk - 1), 0)

    cur64 = pl.BlockSpec((big, HEAD_DIM), lambda t: (t, 0))
    cur128 = pl.BlockSpec((big, LANES), lambda t: (t, 0))
    specs = [cur64, pl.BlockSpec((QB, HEAD_DIM), next_map), cur64, pl.BlockSpec((QB, HEAD_DIM), prev_map),
             cur128, pl.BlockSpec((QB, LANES), prev_map), cur128, pl.BlockSpec((QB, LANES), next_map)]
    return _pcall(kern, name=name, grid=(nsteps,), in_specs=specs, out_specs=[cur64, cur64, cur128],
                  out_shape=[jax.ShapeDtypeStruct((rows, HEAD_DIM), F32), jax.ShapeDtypeStruct((rows, HEAD_DIM), F32),
                             jax.ShapeDtypeStruct((rows, LANES), F32)],
                  compiler_params=_params("parallel"))(q, q, k, k, vp, vp, pk, pk)


def _mem_fwd(q, km, vm, *, name):
    hh, s, hd = q.shape
    ml = km.shape[1]
    tq = _pick(s, (512, 256))

    def kern(q_ref, k_ref, v_ref, o_ref):
        sc = _nt(q_ref[...], k_ref[...]) * SCALE
        e = jnp.exp(sc - jnp.max(sc, axis=1, keepdims=True))
        p = e / jnp.sum(e, axis=1, keepdims=True)
        o_ref[...] = _nn(p, v_ref[...])

    blk = pl.BlockSpec((None, tq, hd), lambda a, i: (a, i, 0))
    kv = pl.BlockSpec((None, ml, hd), lambda a, i: (a, 0, 0))
    return _pcall(kern, name=name, grid=(hh, s // tq), in_specs=[blk, kv, kv], out_specs=blk,
                  out_shape=jax.ShapeDtypeStruct((hh, s, hd), F32), compiler_params=_params("parallel", "parallel"))(q, km, vm)


def _mem_bwd(q, km, vm, do, *, name):
    hh, s, hd = q.shape
    ml = km.shape[1]
    tq = _pick(s, (512, 256))

    def kern(q_ref, k_ref, v_ref, do_ref, dq_ref, dk_ref, dv_ref):
        @pl.when(pl.program_id(1) == 0)
        def _():
            dk_ref[...] = jnp.zeros_like(dk_ref)
            dv_ref[...] = jnp.zeros_like(dv_ref)

        qb = q_ref[...]
        dob = do_ref[...]
        sc = _nt(qb, k_ref[...]) * SCALE
        e = jnp.exp(sc - jnp.max(sc, axis=1, keepdims=True))
        p = e / jnp.sum(e, axis=1, keepdims=True)
        dp = _nt(dob, v_ref[...])
        ds = p * (dp - jnp.sum(p * dp, axis=1, keepdims=True)) * SCALE
        dq_ref[...] = _nn(ds, k_ref[...])
        dk_ref[...] += _tn(ds, qb)
        dv_ref[...] += _tn(p, dob)

    blk = pl.BlockSpec((None, tq, hd), lambda a, i: (a, i, 0))
    kv = pl.BlockSpec((None, ml, hd), lambda a, i: (a, 0, 0))
    kvs = jax.ShapeDtypeStruct((hh, ml, hd), F32)
    return _pcall(kern, name=name, grid=(hh, s // tq), in_specs=[blk, kv, kv, blk], out_specs=[blk, kv, kv],
                  out_shape=[jax.ShapeDtypeStruct((hh, s, hd), F32), kvs, kvs],
                  compiler_params=_params("parallel", "arbitrary"))(q, km, vm, do)


DSA_BT = QB * max(r for _, r in DSA_GROUPS)


def _unit_rows(r, c, b):
    return pl.ds(c + QB * r * b, QB, stride=r)


def _pair_cols(t, first):
    return [jnp.max(jnp.where(first, t, -jnp.inf), axis=1, keepdims=True),
            jnp.max(jnp.where(first, -jnp.inf, t), axis=1, keepdims=True)]


def _dsa2_fwd(qn, kn, v32, g, *, name):
    s = qn.shape[0]
    r = DSA_GROUPS[g][1]
    nbk = DSA_BT // (QB * r)
    npair = DSA_OUT_W // LANES

    def kern(q_ref, k_ref, kp_ref, v_ref, vp_ref, o_ref, l_ref):
        t = pl.program_id(1)
        first = _first_half((QB, LANES))
        rel = _dsa_rel()
        for c in range(r):
            for b in range(nbk):
                rows = _unit_rows(r, c, b)
                kc, vc = k_ref[rows, :], v_ref[rows, :]
                if b > 0:
                    prow = _unit_rows(r, c, b - 1)
                    kpv, vpv, has_prev = k_ref[prow, :], v_ref[prow, :], True
                else:
                    prow = _unit_rows(r, c, nbk - 1)
                    kpv, vpv, has_prev = kp_ref[prow, :], vp_ref[prow, :], t > 0
                outs, lses = [], []
                for qe in _split_pair(q_ref[rows, :], first):
                    sc = jnp.where(rel <= 0, _nt(qe, kc) * SCALE, -jnp.inf)
                    sp = jnp.where(_prev_mask(rel, has_prev), _nt(qe, kpv) * SCALE, -jnp.inf)
                    m = jnp.maximum(jnp.max(sc, axis=1, keepdims=True), jnp.max(sp, axis=1, keepdims=True))
                    pc = jnp.exp(sc - m)
                    pp = jnp.exp(sp - m)
                    den = jnp.sum(pc, axis=1, keepdims=True) + jnp.sum(pp, axis=1, keepdims=True)
                    outs.append((_nn(pc, vc) + _nn(pp, vpv)) / den)
                    lses.append(m + jnp.log(den))
                o_ref[rows, :] = jnp.where(first, outs[0], outs[1])
                l_ref[rows, :] = jnp.where(first, lses[0], lses[1])

    npg = DSA_HPG * HEAD_DIM // LANES
    cur = pl.BlockSpec((DSA_BT, LANES), lambda a, t: (t, npg * g + a))
    prev = pl.BlockSpec((DSA_BT, LANES), lambda a, t: (jnp.maximum(t - 1, 0), npg * g + a))
    out = pl.BlockSpec((DSA_BT, LANES), lambda a, t: (t, a))
    shp = jax.ShapeDtypeStruct((s, DSA_OUT_W), F32)
    return _pcall(kern, name=name, grid=(npair, s // DSA_BT), in_specs=[cur, cur, prev, cur, prev], out_specs=[out, out],
                  out_shape=[shp, shp], compiler_params=_params("parallel", "parallel"))(qn, kn, kn, v32, v32)


def _dsa2_combine(parts, *, name):
    s, wd = parts[0][0].shape
    ts = _pick(s, (512, 256))

    def kern(o0, l0, o1, l1, o2, l2, o_ref, l_ref):
        ls = [l0[...], l1[...], l2[...]]
        m = jnp.maximum(jnp.maximum(ls[0], ls[1]), ls[2])
        es = [jnp.exp(l - m) for l in ls]
        den = es[0] + es[1] + es[2]
        o_ref[...] = (es[0] * o0[...] + es[1] * o1[...] + es[2] * o2[...]) / den
        l_ref[...] = m + jnp.log(den)

    blk = pl.BlockSpec((ts, wd), lambda i: (i, 0))
    shp = jax.ShapeDtypeStruct((s, wd), F32)
    flat = [t for pair in parts for t in pair]
    return _pcall(kern, name=name, grid=(s // ts,), in_specs=[blk] * 6, out_specs=[blk, blk], out_shape=[shp, shp],
                  compiler_params=_params("parallel"))(*flat)


def _dsa2_prep(o, do, *, name):
    s, wd = o.shape
    ts = _pick(s, (512, 256))

    def kern(o_ref, do_ref, d_ref):
        d_ref[...] = _head_mean(do_ref[...] * o_ref[...], _block_diag(wd)) * HEAD_DIM

    blk = pl.BlockSpec((ts, wd), lambda i: (i, 0))
    return _pcall(kern, name=name, grid=(s // ts,), in_specs=[blk, blk], out_specs=blk,
                  out_shape=jax.ShapeDtypeStruct((s, wd), F32), compiler_params=_params("parallel"))(o, do)


def _dsa2_bwd(qn, kn, v32, do, lse, dd, g, *, name):
    s = qn.shape[0]
    r = DSA_GROUPS[g][1]
    nbk = DSA_BT // (QB * r)
    npair = DSA_OUT_W // LANES
    nsteps = s // DSA_BT

    def kern(q_ref, qn_ref, k_ref, kp_ref, v_ref, vp_ref, do_ref, don_ref, l_ref, ln_ref, d_ref, dn_ref,
             dq_ref, dk_ref, dv_ref):
        t = pl.program_id(1)
        first = _first_half((QB, LANES))
        rel = _dsa_rel()

        def pair(qs, dos, lcols, dcols, kb, vb, mask):
            dqs = []
            dk = jnp.zeros((QB, LANES), F32)
            dv = jnp.zeros((QB, LANES), F32)
            for e in range(2):
                p = jnp.where(mask, jnp.exp(_nt(qs[e], kb) * SCALE - lcols[e]), 0.0)
                ds = p * (_nt(dos[e], vb) - dcols[e]) * SCALE
                dqs.append(_nn(ds, kb))
                dk = dk + _tn(ds, qs[e])
                dv = dv + _tn(p, dos[e])
            return dqs, dk, dv

        def load(rows, qr, dor, lr, dr):
            return (_split_pair(qr[rows, :], first), _split_pair(dor[rows, :], first), _pair_cols(lr[rows, :], first),
                    _pair_cols(dr[rows, :], first))

        for c in range(r):
            for b in range(nbk):
                rows = _unit_rows(r, c, b)
                qs, dos, lcols, dcols = load(rows, q_ref, do_ref, l_ref, d_ref)
                dq_c, dk_c, dv_c = pair(qs, dos, lcols, dcols, k_ref[rows, :], v_ref[rows, :], rel <= 0)
                if b > 0:
                    prow = _unit_rows(r, c, b - 1)
                    dq_p, dk_p, dv_p = pair(qs, dos, lcols, dcols, k_ref[prow, :], v_ref[prow, :], _prev_mask(rel, True))
                    dk_ref[prow, :] += dk_p
                    dv_ref[prow, :] += dv_p
                else:
                    prow = _unit_rows(r, c, nbk - 1)
                    dq_p, _, _ = pair(qs, dos, lcols, dcols, kp_ref[prow, :], vp_ref[prow, :], _prev_mask(rel, t > 0))
                dq_ref[rows, :] = jnp.where(first, dq_c[0] + dq_p[0], dq_c[1] + dq_p[1])
                dk_ref[rows, :] = dk_c
                dv_ref[rows, :] = dv_c
            last = _unit_rows(r, c, nbk - 1)
            nqs, ndos, nl, nd = load(_unit_rows(r, c, 0), qn_ref, don_ref, ln_ref, dn_ref)
            _, dk_n, dv_n = pair(nqs, ndos, nl, nd, k_ref[last, :], v_ref[last, :], _prev_mask(rel, t < nsteps - 1))
            dk_ref[last, :] += dk_n
            dv_ref[last, :] += dv_n

    npg = DSA_HPG * HEAD_DIM // LANES

    def at(shift, col):
        return pl.BlockSpec((DSA_BT, LANES), lambda a, t: (jnp.clip(t + shift, 0, nsteps - 1), col(a)))

    gcol = lambda a: npg * g + a
    ocol = lambda a: a
    specs = [at(0, gcol), at(1, gcol), at(0, gcol), at(-1, gcol), at(0, gcol), at(-1, gcol),
             at(0, ocol), at(1, ocol), at(0, ocol), at(1, ocol), at(0, ocol), at(1, ocol)]
    shp = jax.ShapeDtypeStruct((s, DSA_OUT_W), F32)
    return _pcall(kern, name=name, grid=(npair, nsteps), in_specs=specs, out_specs=[at(0, ocol)] * 3, out_shape=[shp, shp, shp],
                  compiler_params=_params("parallel", "parallel"))(qn, qn, kn, kn, v32, v32, do, do, lse, lse, dd, dd)


def _mem2_fwd(qn, km, kv, *, name):
    s = qn.shape[0]
    ml = km.shape[0]
    tq = _pick(s, (512, 256))
    npair = MEM_W // LANES

    def kern(q_ref, k_ref, v_ref, o_ref):
        first = _first_half((tq, LANES))
        outs = []
        for qe in _split_pair(q_ref[...], first):
            sc = _nt(qe, k_ref[...]) * SCALE
            e = jnp.exp(sc - jnp.max(sc, axis=1, keepdims=True))
            outs.append(_nn(e / jnp.sum(e, axis=1, keepdims=True), v_ref[...]))
        o_ref[...] = jnp.where(first, outs[0], outs[1])

    blk = pl.BlockSpec((tq, LANES), lambda a, i: (i, a))
    return _pcall(kern, name=name, grid=(npair, s // tq),
                  in_specs=[blk, pl.BlockSpec((ml, LANES), lambda a, i: (0, a)), pl.BlockSpec((ml, LANES), lambda a, i: (0, npair + a))],
                  out_specs=blk, out_shape=jax.ShapeDtypeStruct((s, MEM_W), F32),
                  compiler_params=_params("parallel", "parallel"))(qn, km, kv)


def _mem2_bwd(qn, km, kv, do, *, name):
    s = qn.shape[0]
    ml = km.shape[0]
    tq = _pick(s, (512, 256))
    npair = MEM_W // LANES

    def kern(q_ref, k_ref, v_ref, do_ref, dq_ref, dk_ref, dv_ref):
        @pl.when(pl.program_id(1) == 0)
        def _():
            dk_ref[...] = jnp.zeros_like(dk_ref)
            dv_ref[...] = jnp.zeros_like(dv_ref)

        first = _first_half((tq, LANES))
        dqs = []
        for qe, doe in zip(_split_pair(q_ref[...], first), _split_pair(do_ref[...], first)):
            sc = _nt(qe, k_ref[...]) * SCALE
            e = jnp.exp(sc - jnp.max(sc, axis=1, keepdims=True))
            p = e / jnp.sum(e, axis=1, keepdims=True)
            dp = _nt(doe, v_ref[...])
            ds = p * (dp - jnp.sum(p * dp, axis=1, keepdims=True)) * SCALE
            dqs.append(_nn(ds, k_ref[...]))
            dk_ref[...] += _tn(ds, qe)
            dv_ref[...] += _tn(p, doe)
        dq_ref[...] = jnp.where(first, dqs[0], dqs[1])

    blk = pl.BlockSpec((tq, LANES), lambda a, i: (i, a))
    kblk = pl.BlockSpec((ml, LANES), lambda a, i: (0, a))
    kshape = jax.ShapeDtypeStruct((ml, MEM_W), F32)
    return _pcall(kern, name=name, grid=(npair, s // tq),
                  in_specs=[blk, kblk, pl.BlockSpec((ml, LANES), lambda a, i: (0, npair + a)), blk],
                  out_specs=[blk, kblk, kblk], out_shape=[jax.ShapeDtypeStruct((s, MEM_W), F32), kshape, kshape],
                  compiler_params=_params("parallel", "arbitrary"))(qn, km, kv, do)


def _merge_fwd(logits, bias, ya, yb, yc, *, name):
    s, d = ya.shape
    ts = _pick(s, (512, 256))

    def kern(l0, l1, l2, b0, b1, b2, a_ref, b_ref, c_ref, o_ref):
        m = (_sigmoid(l0[...] + b0[...]) * a_ref[...] + _sigmoid(l1[...] + b1[...]) * b_ref[...]
             + _sigmoid(l2[...] + b2[...]) * c_ref[...])
        o_ref[...] = m.astype(o_ref.dtype)

    row = pl.BlockSpec((ts, d), lambda i: (i, 0))
    lg = [pl.BlockSpec((ts, d), functools.partial(lambda i, c: (i, c), c=c)) for c in range(3)]
    bs = [pl.BlockSpec((1, d), functools.partial(lambda i, c: (0, c), c=c)) for c in range(3)]
    return _pcall(kern, name=name, grid=(s // ts,), in_specs=lg + bs + [row, row, row], out_specs=row,
                  out_shape=jax.ShapeDtypeStruct((s, d), BF16),
                  compiler_params=_params("parallel"))(logits, logits, logits, bias, bias, bias, ya, yb, yc)


def _merge_bwd(logits, bias, ya, yb, yc, dm, *, name):
    s, d = ya.shape
    ts = _pick(s, (256,))

    def kern(l0, l1, l2, b0, b1, b2, a_ref, b_ref, c_ref, dm_ref, da_ref, db_ref, dc_ref, dl0, dl1, dl2, dbias0, dbias1, dbias2):
        first = pl.program_id(0) == 0
        dmv = dm_ref[...]
        for l_ref, bb_ref, y_ref, dy_ref, dl_ref, dbias_ref in ((l0, b0, a_ref, da_ref, dl0, dbias0), (l1, b1, b_ref, db_ref, dl1, dbias1),
                                                                (l2, b2, c_ref, dc_ref, dl2, dbias2)):
            g = _sigmoid(l_ref[...] + bb_ref[...])
            dy_ref[...] = (dmv * g).astype(dy_ref.dtype)
            dl = dmv * y_ref[...] * g * (1.0 - g)
            dl_ref[...] = dl.astype(dl_ref.dtype)

            @pl.when(first)
            def _():
                dbias_ref[...] = jnp.zeros_like(dbias_ref)

            dbias_ref[...] += jnp.sum(dl, axis=0, keepdims=True)

    row = pl.BlockSpec((ts, d), lambda i: (i, 0))
    lg = [pl.BlockSpec((ts, d), functools.partial(lambda i, c: (i, c), c=c)) for c in range(3)]
    bs = [pl.BlockSpec((1, d), functools.partial(lambda i, c: (0, c), c=c)) for c in range(3)]
    vec = pl.BlockSpec((1, d), lambda i: (0, 0))
    yshape = jax.ShapeDtypeStruct((s, d), BF16)
    vshape = jax.ShapeDtypeStruct((1, d), F32)
    outs = _pcall(kern, name=name, grid=(s // ts,), in_specs=lg + bs + [row, row, row, row],
                  out_specs=[row, row, row, row, row, row, vec, vec, vec],
                  out_shape=[yshape] * 6 + [vshape] * 3,
                  compiler_params=_params("arbitrary"))(logits, logits, logits, bias, bias, bias, ya, yb, yc, dm)
    return outs[0], outs[1], outs[2], outs[3:6], jnp.concatenate(outs[6:9], axis=1)


def _heads(t, n):
    s = t.shape[0]
    return t.reshape(s, n, HEAD_DIM).transpose(1, 0, 2)


def _unheads(t):
    n, s, hd = t.shape
    return t.transpose(1, 0, 2).reshape(s, n * hd)


def _to_class_major(t):
    s = t.shape[0]
    w = t.shape[1] // (DSA_HPG * len(DSA_GROUPS))
    parts = []
    for g, (_, r) in enumerate(DSA_GROUPS):
        tg = t[:, g * DSA_HPG * w:(g + 1) * DSA_HPG * w].reshape(s // r, r, DSA_HPG, w)
        parts.append(tg.transpose(2, 1, 0, 3).reshape(DSA_HPG * s, w))
    return jnp.concatenate(parts, axis=0)


def _slot_to_class_major(t):
    hh, s, w = t.shape
    parts = []
    for _, r in DSA_GROUPS:
        parts.append(t.reshape(hh, s // r, r, w).transpose(0, 2, 1, 3).reshape(hh * s, w))
    return jnp.concatenate(parts, axis=0)


def _from_class_major(t):
    rows, w = t.shape
    s = rows // 12
    out = []
    for g, (_, r) in enumerate(DSA_GROUPS):
        tg = t[g * 4 * s:(g + 1) * 4 * s].reshape(DSA_HPG, r, s // r, w)
        out.append(tg.transpose(0, 2, 1, 3).reshape(DSA_HPG, s, w))
    return out


def _pad_lanes(t):
    return jnp.concatenate([t, jnp.zeros(t.shape[:-1] + (LANES - t.shape[-1],), t.dtype)], axis=-1)


def _ffn_fwd(x, norm, w1, w3, w2, tag):
    h = _rms_fwd(x, norm, name=f"{tag}_rms")
    a, b, f = _ffn_up(h, w1, w3, name=f"{tag}_up")
    y = _matmul(f, w2, name=f"{tag}_down", res=x, alpha=0.5, tn=1024, tk=1408)
    return y, (h, a, b, f)


def _ffn_bwd(x, norm, w1, w3, w2, saved, dy, dyb, tag):
    h, a, b, f = saved
    dw2 = _matmul(f, dyb, name=f"{tag}_dw2", ta=True, alpha=0.5, tm=1408, tn=1024, tk=512)
    da, db = _ffn_dact(dyb, w2, a, b, name=f"{tag}_dact")
    dw1 = _matmul(h, da, name=f"{tag}_dw1", ta=True, tm=1024, tn=1408, tk=512)
    dw3 = _matmul(h, db, name=f"{tag}_dw3", ta=True, tm=1024, tn=1408, tk=512)
    dh = _matmul(da, w1, name=f"{tag}_dh", tb=True, tn=1024, tk=1408, pair2=(db, w3))
    dx, dxb, dnorm = _rms_bwd(x, norm, dh, dy, name=f"{tag}_drms")
    return dx, dxb, dnorm, dw1, dw3, dw2


def _local_step(x, mem, w, loss_target):
    s, d = x.shape
    assert s % (QB * 16) == 0
    rope = _rope_tables(s)

    x1, sv1 = _ffn_fwd(x, w['ffn1_norm'], w['ffn1_w1'], w['ffn1_w3'], w['ffn1_w2'], "ffn1")
    h = _rms_fwd(x1, w['mix_norm'], name="mix_rms")
    p = _matmul(h, w['w_in'], name="in_proj", out_dtype=BF16, tn=1024)
    logits = _matmul(h, w['w_gate'], name="gate_proj", tn=1024)
    c_qb, c_kb, c_vb, c_qc = 3 * SB_W, 3 * SB_W + DSA_W, 3 * SB_W + 2 * DSA_W, 3 * SB_W + 3 * DSA_W

    oa_t = _sb2_fwd(p, name="sb_fwd")
    ya = _matmul(oa_t, w['w_branch_sb'], name="sb_out")

    qb_n = _qknorm_fwd(p, c_qb, DSA_W, w['qn_dsa'], rope, name="dsa_qnorm", out_dtype=F32)
    kb_n = _qknorm_fwd(p, c_kb, DSA_W, w['kn_dsa'], rope, name="dsa_knorm", out_dtype=F32)
    vb32 = p[:, c_vb:c_vb + DSA_W].astype(F32)
    groups = range(len(DSA_GROUPS))
    ob_t, lse_b = _dsa2_combine([_dsa2_fwd(qb_n, kb_n, vb32, gi, name=f"dsa_fwd{gi}") for gi in groups], name="dsa_combine")
    yb = _matmul(ob_t, w['w_branch_dsa'], name="dsa_out")

    memh = _rms_fwd(mem, w['mem_norm'], name="mem_rms")
    kv = _matmul(memh, w['w_mem_kv'], name="mem_kv", out_dtype=BF16)
    km_n = _qknorm_fwd(kv, 0, MEM_W, w['kn_mem'], None, name="mem_knorm")
    qc_n = _qknorm_fwd(p, c_qc, MEM_W, w['qn_mem'], None, name="mem_qnorm")
    oc_t = _mem2_fwd(qc_n, km_n, kv, name="mem_fwd")
    yc = _matmul(oc_t, w['w_branch_mem'], name="mem_out")

    merged = _merge_fwd(logits, w['b_gate'], ya, yb, yc, name="merge")
    x2 = _matmul(merged, w['w_out'], name="out_proj", res=x1, tn=1024)
    x3, sv2 = _ffn_fwd(x2, w['ffn2_norm'], w['ffn2_w1'], w['ffn2_w3'], w['ffn2_w2'], "ffn2")
    dx3, dx3b, loss = _loss_head(x3, loss_target, name="loss")

    g = {}
    dx2, dx2b, g['ffn2_norm'], g['ffn2_w1'], g['ffn2_w3'], g['ffn2_w2'] = _ffn_bwd(
        x2, w['ffn2_norm'], w['ffn2_w1'], w['ffn2_w3'], w['ffn2_w2'], sv2, dx3, dx3b, "ffn2")

    g['w_out'] = _matmul(merged, dx2b, name="d_w_out", ta=True, tn=1024, tk=512)
    dm = _matmul(dx2b, w['w_out'], name="d_merged", tb=True, tn=1024)
    dya, dyb, dyc, dlog, g['b_gate'] = _merge_bwd(logits, w['b_gate'], ya, yb, yc, dm, name="d_merge")
    dlogits = jnp.concatenate(dlog, axis=1)

    g['w_branch_sb'] = _matmul(oa_t, dya, name="d_w_sb", ta=True, tn=1024, tk=512)
    g['w_branch_dsa'] = _matmul(ob_t, dyb, name="d_w_dsa", ta=True, tk=512)
    g['w_branch_mem'] = _matmul(oc_t, dyc, name="d_w_mem", ta=True, tk=512)
    doa = _matmul(dya, w['w_branch_sb'], name="d_oa", tb=True, out_dtype=BF16)
    dob = _matmul(dyb, w['w_branch_dsa'], name="d_ob", tb=True)
    doc = _matmul(dyc, w['w_branch_mem'], name="d_oc", tb=True, out_dtype=BF16)

    dqa, dka, dva = _sb2_bwd(p, oa_t, doa, name="sb_bwd")

    dd_b = _dsa2_prep(ob_t, dob, name="dsa_prep")
    dgrp = [_dsa2_bwd(qb_n, kb_n, vb32, dob, lse_b, dd_b, gi, name=f"dsa_bwd{gi}") for gi in groups]
    dvb = jnp.concatenate([t[2] for t in dgrp], axis=1).astype(BF16)
    dqb, g['qn_dsa'] = _qknorm_bwd(p, c_qb, DSA_W, w['qn_dsa'], rope, [t[0] for t in dgrp], name="d_dsa_qnorm")
    dkb, g['kn_dsa'] = _qknorm_bwd(p, c_kb, DSA_W, w['kn_dsa'], rope, [t[1] for t in dgrp], name="d_dsa_knorm")

    dqc_n, dkm_n, dvm = _mem2_bwd(qc_n, km_n, kv, doc, name="mem_bwd")
    dqc, g['qn_mem'] = _qknorm_bwd(p, c_qc, MEM_W, w['qn_mem'], None, dqc_n, name="d_mem_qnorm")
    dkm, g['kn_mem'] = _qknorm_bwd(kv, 0, MEM_W, w['kn_mem'], None, dkm_n, name="d_mem_knorm")
    dkv = jnp.concatenate([dkm, dvm.astype(BF16)], axis=1)
    g['w_mem_kv'] = _matmul(memh, dkv, name="d_w_mem_kv", ta=True)
    dmemh = _matmul(dkv, w['w_mem_kv'], name="d_memh", tb=True)
    _, _, g['mem_norm'] = _rms_bwd(mem, w['mem_norm'], dmemh, None, name="d_mem_rms")

    dp = jnp.concatenate([dqa.astype(BF16), dka.astype(BF16), dva.astype(BF16),
                          dqb, dkb, dvb, dqc], axis=1)
    g['w_in'] = _matmul(h, dp, name="d_w_in", ta=True, tn=2048, tk=512)
    g['w_gate'] = _matmul(h, dlogits, name="d_w_gate", ta=True, tn=1536, tk=512)
    dh = _matmul(dp, w['w_in'], name="d_h_in", tb=True, tn=1024)
    dh = _matmul(dlogits, w['w_gate'], name="d_h_gate", tb=True, res=dh, tn=1024)
    dx1, dx1b, g['mix_norm'] = _rms_bwd(x1, w['mix_norm'], dh, dx2, name="d_mix_rms")

    dx0, _, g['ffn1_norm'], g['ffn1_w1'], g['ffn1_w3'], g['ffn1_w2'] = _ffn_bwd(
        x, w['ffn1_norm'], w['ffn1_w1'], w['ffn1_w3'], w['ffn1_w2'], sv1, dx1, dx1b, "ffn1")
    return loss, dx0, g


def _pack_rows(d, names):
    return jnp.concatenate([d[n].reshape(-1, LANES) for n in names], axis=0)


def _unpack_rows(t, like, names):
    out, off = {}, 0
    for n in names:
        r = like[n].size // LANES
        out[n] = t[off:off + r].reshape(like[n].shape)
        off += r
    return out


def _unpack_gathered(t, local, names):
    out, off = {}, 0
    for n in names:
        r, c = local[n].shape
        rows = r * c // LANES
        blk = t[:, off:off + rows].reshape(N_DEV, r, c)
        out[n] = blk.reshape(N_DEV * r, c) if SHARD_AXIS[n] == 0 else blk.transpose(1, 0, 2).reshape(r, N_DEV * c)
        off += rows
    return out


def _pack_for_owners(g, local, names):
    parts = []
    for n in names:
        r, c = local[n].shape
        blk = g[n].reshape(N_DEV, r, c) if SHARD_AXIS[n] == 0 else g[n].reshape(r, N_DEV, c).transpose(1, 0, 2)
        parts.append(blk.reshape(N_DEV, r * c // LANES, LANES))
    return jnp.concatenate(parts, axis=1)


def _pack_small(d, names, extra_rows):
    parts = []
    for n in names:
        v = d[n].reshape(-1)
        pad = (-v.size) % LANES
        parts.append(jnp.concatenate([v, jnp.zeros((pad,), v.dtype)]).reshape(-1, LANES))
    t = jnp.concatenate(parts, axis=0)
    return jnp.concatenate([t, jnp.zeros((extra_rows, LANES), t.dtype)], axis=0)


def _unpack_small(t, like, names):
    out, off = {}, 0
    for n in names:
        size = like[n].size
        rows = -(-size // LANES)
        out[n] = t[off:off + rows].reshape(-1)[:size].reshape(like[n].shape)
        off += rows
    return out


def _exchange(src, per_peer, *, name):
    rows = src.shape[-2]

    def body(src_ref, out_ref, send_sems, recv_sems, local_sem):
        x, y, c = lax.axis_index("x"), lax.axis_index("y"), lax.axis_index("c")
        me = 4 * x + 2 * y + c
        mine = pltpu.make_async_copy(src_ref.at[me] if per_peer else src_ref, out_ref.at[me], local_sem)
        mine.start()
        copies = []
        for k in range(1, N_DEV):
            px = 1 - x if k & 4 else x
            py = 1 - y if k & 2 else y
            pc = 1 - c if k & 1 else c
            cp = pltpu.make_async_remote_copy(
                src_ref=src_ref.at[4 * px + 2 * py + pc] if per_peer else src_ref, dst_ref=out_ref.at[me],
                send_sem=send_sems.at[k - 1], recv_sem=recv_sems.at[k - 1],
                device_id=(px, py, pc), device_id_type=pl.DeviceIdType.MESH)
            cp.start()
            copies.append(cp)
        for cp in copies:
            cp.wait_recv()
        for cp in copies:
            cp.wait_send()
        mine.wait()

    anyspace = pl.BlockSpec(memory_space=pl.ANY)
    return _pcall(body, name=name, in_specs=[anyspace], out_specs=anyspace,
                  out_shape=jax.ShapeDtypeStruct((N_DEV, rows, LANES), src.dtype),
                  scratch_shapes=[pltpu.SemaphoreType.DMA((N_DEV - 1,)), pltpu.SemaphoreType.DMA((N_DEV - 1,)),
                                  pltpu.SemaphoreType.DMA])(src)


def _gather_two_level(src, *, name):
    rows = src.shape[0]

    def body(src_ref, out_ref, send_sems, recv_sems, local_sem):
        x, y, c = lax.axis_index("x"), lax.axis_index("y"), lax.axis_index("c")
        me, sibling = (x, y, c), (x, y, 1 - c)
        chips = [(1 - x, y), (x, 1 - y), (1 - x, 1 - y)]

        def slab(px, py, pc):
            return out_ref.at[4 * px + 2 * py + pc]

        def copy(k, block, to, from_src=False):
            return pltpu.make_async_remote_copy(
                src_ref=src_ref if from_src else slab(*block), dst_ref=slab(*block),
                send_sem=send_sems.at[k], recv_sem=recv_sems.at[k], device_id=to, device_id_type=pl.DeviceIdType.MESH)

        mine = pltpu.make_async_copy(src_ref, slab(*me), local_sem)
        mine.start()
        first = [copy(0, me, sibling, True)] + [copy(1 + j, me, (*chip, c), True) for j, chip in enumerate(chips)]
        for cp in first:
            cp.start()
        passed = [copy(4 + j, (*chip, c), sibling) for j, chip in enumerate(chips)]
        for j, chip in enumerate(chips):
            copy(1 + j, (*chip, c), me).wait_recv()
            passed[j].start()
        copy(0, sibling, me).wait_recv()
        for j, chip in enumerate(chips):
            copy(4 + j, (*chip, 1 - c), me).wait_recv()
        for cp in first + passed:
            cp.wait_send()
        mine.wait()

    anyspace = pl.BlockSpec(memory_space=pl.ANY)
    return _pcall(body, name=name, in_specs=[anyspace], out_specs=anyspace,
                  out_shape=jax.ShapeDtypeStruct((N_DEV, rows, LANES), src.dtype),
                  scratch_shapes=[pltpu.SemaphoreType.DMA((N_DEV - 1,)), pltpu.SemaphoreType.DMA((N_DEV - 1,)),
                                  pltpu.SemaphoreType.DMA])(src)


def _adamw(recv, w, m, v, *, name):
    rows = w.shape[0]
    tr = _pick(rows, (512, 256, 128, 64))

    def kern(r_ref, w_ref, m_ref, v_ref, g_ref, d_ref, mo_ref, vo_ref):
        g = r_ref[0].astype(F32)
        for p in range(1, N_DEV):
            g = g + r_ref[p].astype(F32)
        mn = ADAM_B1 * m_ref[...] + (1.0 - ADAM_B1) * g
        vn = ADAM_B2 * v_ref[...] + (1.0 - ADAM_B2) * (g * g)
        m_hat = mn / (1.0 - ADAM_B1 ** ADAM_STEP)
        v_hat = vn / (1.0 - ADAM_B2 ** ADAM_STEP)
        g_ref[...] = g
        d_ref[...] = -ADAM_LR * (m_hat / (jnp.sqrt(v_hat) + ADAM_EPS) + ADAM_WD * w_ref[...])
        mo_ref[...] = mn
        vo_ref[...] = vn

    row = pl.BlockSpec((tr, LANES), lambda i: (i, 0))
    shp = jax.ShapeDtypeStruct((rows, LANES), F32)
    return _pcall(kern, name=name, grid=(rows // tr,), in_specs=[pl.BlockSpec((N_DEV, tr, LANES), lambda i: (0, i, 0)), row, row, row],
                  out_specs=[row, row, row, row], out_shape=[shp, shp, shp, shp], compiler_params=_params("parallel"))(recv, w, m, v)


INPUTS = ['x', 'mem'] + WEIGHTS + ['loss_target'] + ['m_' + n for n in WEIGHTS] + ['v_' + n for n in WEIGHTS]
SMALL_PAD_ROWS = 4


def kernel(x, mem, ffn1_norm, ffn1_w1, ffn1_w3, ffn1_w2, mix_norm, mem_norm, w_in, w_mem_kv, qn_dsa, kn_dsa, qn_mem, kn_mem, w_branch_sb, w_branch_dsa, w_branch_mem, w_gate, b_gate, w_out, ffn2_norm, ffn2_w1, ffn2_w3, ffn2_w2, loss_target, m_ffn1_norm, m_ffn1_w1, m_ffn1_w3, m_ffn1_w2, m_mix_norm, m_mem_norm, m_w_in, m_w_mem_kv, m_qn_dsa, m_kn_dsa, m_qn_mem, m_kn_mem, m_w_branch_sb, m_w_branch_dsa, m_w_branch_mem, m_w_gate, m_b_gate, m_w_out, m_ffn2_norm, m_ffn2_w1, m_ffn2_w3, m_ffn2_w2, v_ffn1_norm, v_ffn1_w1, v_ffn1_w3, v_ffn1_w2, v_mix_norm, v_mem_norm, v_w_in, v_w_mem_kv, v_qn_dsa, v_kn_dsa, v_qn_mem, v_kn_mem, v_w_branch_sb, v_w_branch_dsa, v_w_branch_mem, v_w_gate, v_b_gate, v_w_out, v_ffn2_norm, v_ffn2_w1, v_ffn2_w3, v_ffn2_w2):
    given = dict(zip(INPUTS, (x, mem, ffn1_norm, ffn1_w1, ffn1_w3, ffn1_w2, mix_norm, mem_norm, w_in, w_mem_kv, qn_dsa, kn_dsa, qn_mem, kn_mem, w_branch_sb, w_branch_dsa, w_branch_mem, w_gate, b_gate, w_out, ffn2_norm, ffn2_w1, ffn2_w3, ffn2_w2, loss_target, m_ffn1_norm, m_ffn1_w1, m_ffn1_w3, m_ffn1_w2, m_mix_norm, m_mem_norm, m_w_in, m_w_mem_kv, m_qn_dsa, m_kn_dsa, m_qn_mem, m_kn_mem, m_w_branch_sb, m_w_branch_dsa, m_w_branch_mem, m_w_gate, m_b_gate, m_w_out, m_ffn2_norm, m_ffn2_w1, m_ffn2_w3, m_ffn2_w2, v_ffn1_norm, v_ffn1_w1, v_ffn1_w3, v_ffn1_w2, v_mix_norm, v_mem_norm, v_w_in, v_w_mem_kv, v_qn_dsa, v_kn_dsa, v_qn_mem, v_kn_mem, v_w_branch_sb, v_w_branch_dsa, v_w_branch_mem, v_w_gate, v_b_gate, v_w_out, v_ffn2_norm, v_ffn2_w1, v_ffn2_w3, v_ffn2_w2), strict=True))
    wl = {n: given[n][0] for n in BIG}
    ws = {n: given[n] for n in SMALL}

    gathered = _gather_two_level(_pack_rows({n: wl[n].astype(BF16) for n in BIG}, BIG), name="gather_weights")
    whole = _unpack_gathered(gathered, wl, BIG)
    loss, dx, g = _local_step(x[0], mem[0], {**whole, **ws}, loss_target[0])

    recv = _exchange(_pack_for_owners(g, wl, BIG).astype(BF16), True, name="scatter_grads")
    big = _adamw(recv, _pack_rows(wl, BIG), _pack_rows({n: given['m_' + n][0] for n in BIG}, BIG),
                 _pack_rows({n: given['v_' + n][0] for n in BIG}, BIG), name="adamw_sharded")
    big = [_unpack_rows(t, wl, BIG) for t in big]

    gs = _pack_small(g, SMALL, SMALL_PAD_ROWS)
    loss_row = gs.shape[0] - SMALL_PAD_ROWS
    gs = gs.at[loss_row, 0].set(loss[0, 0])
    recv_s = _exchange(gs, False, name="gather_small")
    small = _adamw(recv_s, _pack_small(ws, SMALL, SMALL_PAD_ROWS), _pack_small({n: given['m_' + n] for n in SMALL}, SMALL, SMALL_PAD_ROWS),
                   _pack_small({n: given['v_' + n] for n in SMALL}, SMALL, SMALL_PAD_ROWS), name="adamw_replicated")
    total_loss = small[0][loss_row, 0]
    small = [_unpack_small(t, ws, SMALL) for t in small]

    outs = [total_loss, dx[None]]
    for kind in range(4):
        outs += [big[kind][n][None] if n in wl else small[kind][n] for n in WEIGHTS]
    return tuple(outs)
```

```python
import functools
import math

import jax
import jax.numpy as jnp
from jax import lax
from jax.experimental import pallas as pl
from jax.experimental.pallas import tpu as pltpu

F32 = jnp.float32
BF16 = jnp.bfloat16
MXU_DT = jnp.bfloat16

N_DEV = 8
HEAD_DIM = 64
SB_HEADS = 8
DSA_GROUPS = ((128, 1), (512, 4), (2048, 16))
DSA_HPG = 4
MEM_HEADS = 4
SB_W = SB_HEADS * HEAD_DIM
DSA_W = DSA_HPG * len(DSA_GROUPS) * HEAD_DIM
DSA_OUT_W = DSA_HPG * HEAD_DIM
MEM_W = MEM_HEADS * HEAD_DIM
ROPE_THETA = 10000.0
NORM_EPS = 1e-6
QB = 128
SCALE = HEAD_DIM ** -0.5
ADAM_LR, ADAM_B1, ADAM_B2, ADAM_EPS, ADAM_WD, ADAM_STEP = 0.001, 0.9, 0.999, 1e-08, 0.01, 10

LANES = 128
VMEM_LIMIT = 48 * 1024 * 1024
SB_DEAD = -110.0 * 1.4426950408889634

WEIGHTS = ['ffn1_norm', 'ffn1_w1', 'ffn1_w3', 'ffn1_w2', 'mix_norm', 'mem_norm', 'w_in', 'w_mem_kv', 'qn_dsa', 'kn_dsa',
           'qn_mem', 'kn_mem', 'w_branch_sb', 'w_branch_dsa', 'w_branch_mem', 'w_gate', 'b_gate', 'w_out', 'ffn2_norm',
           'ffn2_w1', 'ffn2_w3', 'ffn2_w2']
SHARD_AXIS = {'ffn1_norm': None, 'ffn1_w1': 1, 'ffn1_w3': 1, 'ffn1_w2': 0, 'mix_norm': None, 'mem_norm': None, 'w_in': 1,
              'w_mem_kv': 0, 'qn_dsa': None, 'kn_dsa': None, 'qn_mem': None, 'kn_mem': None, 'w_branch_sb': 1,
              'w_branch_dsa': 1, 'w_branch_mem': 1, 'w_gate': 1, 'b_gate': None, 'w_out': 0, 'ffn2_norm': None,
              'ffn2_w1': 1, 'ffn2_w3': 1, 'ffn2_w2': 0}
BIG = [n for n in WEIGHTS if SHARD_AXIS[n] is not None]
SMALL = [n for n in WEIGHTS if SHARD_AXIS[n] is None]


def _pcall(kern, **kw):
    return pl.pallas_call(kern, **kw)


def _params(*sem):
    return pltpu.CompilerParams(dimension_semantics=sem, vmem_limit_bytes=VMEM_LIMIT)


def _dot(a, b, dims):
    return lax.dot_general(a.astype(MXU_DT), b.astype(MXU_DT), (dims, ((), ())), preferred_element_type=F32)


def _nn(a, b):
    return _dot(a, b, ((1,), (0,)))


def _nt(a, b):
    return _dot(a, b, ((1,), (1,)))


def _tn(a, b):
    return _dot(a, b, ((0,), (0,)))


def _pick(n, prefs):
    for p in prefs:
        if n % p == 0:
            return p
    return n


def _matmul(a, b, *, name, ta=False, tb=False, out_dtype=F32, res=None, alpha=1.0, tm=1024, tn=512, tk=1024, pair2=None,
            epilogue=None):
    if ta:
        kdim, m = a.shape
    else:
        m, kdim = a.shape
    n = b.shape[0] if tb else b.shape[1]
    tm = _pick(m, (tm, 512, 256, 128))
    tn = _pick(n, (tn, 512, 384, 256, 128))
    tk = _pick(kdim, (tk, 1024, 512, 256, 128))
    nk = kdim // tk
    a_spec = pl.BlockSpec((tk, tm), lambda i, j, k: (k, i)) if ta else pl.BlockSpec((tm, tk), lambda i, j, k: (i, k))
    b_spec = pl.BlockSpec((tn, tk), lambda i, j, k: (j, k)) if tb else pl.BlockSpec((tk, tn), lambda i, j, k: (k, j))
    o_spec = pl.BlockSpec((tm, tn), lambda i, j, k: (i, j))
    v_spec = pl.BlockSpec((1, tn), lambda i, j, k: (0, j))
    dims = ((0 if ta else 1,), (1 if tb else 0,))
    n_mm = 2 if pair2 is None else 4
    if epilogue is None:
        row_ins, vec_ins = ([] if res is None else [res]), []
        out_dtypes, n_vec = [out_dtype], 0
    else:
        assert tn == n and res is None
        epi_fn, row_ins, vec_ins, out_dtypes, n_vec = epilogue
    n_row_out = len(out_dtypes)

    def kern(*refs):
        refs = list(refs)
        acc_ref = refs.pop() if nk > 1 else None
        mm = refs[:n_mm]
        extra = refs[n_mm:n_mm + len(row_ins) + len(vec_ins)]
        outs = refs[n_mm + len(extra):]
        i = pl.program_id(0)
        k = pl.program_id(2)

        def product():
            part = _dot(mm[0][...], mm[1][...], dims)
            if pair2 is not None:
                part = part + _dot(mm[2][...], mm[3][...], dims)
            return part

        def finish(r):
            if alpha != 1.0:
                r = r * alpha
            if epilogue is None:
                if extra:
                    r = extra[0][...] + r
                outs[0][...] = r.astype(out_dtype)
                return
            vals = epi_fn(r, *[e[...] for e in extra])
            for o_ref, v in zip(outs[:n_row_out], vals[:n_row_out]):
                o_ref[...] = v.astype(o_ref.dtype)
            for o_ref, v in zip(outs[n_row_out:], vals[n_row_out:]):
                @pl.when(i == 0)
                def _():
                    o_ref[...] = jnp.zeros_like(o_ref)

                o_ref[...] += v

        if nk == 1:
            finish(product())
            return

        @pl.when(k == 0)
        def _():
            acc_ref[...] = jnp.zeros_like(acc_ref)

        acc_ref[...] += product()

        @pl.when(k == nk - 1)
        def _():
            finish(acc_ref[...])

    ins = [a, b] + ([] if pair2 is None else list(pair2)) + list(row_ins) + list(vec_ins)
    specs = [a_spec, b_spec] * (n_mm // 2) + [o_spec] * len(row_ins) + [v_spec] * len(vec_ins)
    out_specs = [o_spec] * n_row_out + [v_spec] * n_vec
    out_shape = [jax.ShapeDtypeStruct((m, n), dt) for dt in out_dtypes] + [jax.ShapeDtypeStruct((1, n), F32)] * n_vec
    outs = _pcall(kern, name=name, grid=(m // tm, n // tn, nk), in_specs=specs, out_specs=out_specs, out_shape=out_shape,
                  scratch_shapes=[pltpu.VMEM((tm, tn), F32)] if nk > 1 else [],
                  compiler_params=_params("arbitrary" if n_vec else "parallel", "parallel", "arbitrary"))(*ins)
    return outs[0] if epilogue is None else outs


def _epi_residual_rms(r, res, gain):
    xn = res + r
    return xn, xn * lax.rsqrt(jnp.mean(xn * xn, axis=-1, keepdims=True) + NORM_EPS) * gain


def _epi_rms_bwd(r, x, dres, gain):
    rs = lax.rsqrt(jnp.mean(x * x, axis=-1, keepdims=True) + NORM_EPS)
    xh = x * rs
    dy = r * gain
    dx = dres + rs * (dy - xh * jnp.mean(dy * xh, axis=-1, keepdims=True))
    return dx, dx, jnp.sum(r * xh, axis=0, keepdims=True)


def _epi_rms_bwd_sum(r, r0, x, dres, gain):
    return _epi_rms_bwd(r + r0, x, dres, gain)


def _epi_loss(r, res, target):
    e = (res + r) - target
    dy = e / e.shape[-1]
    return dy, dy, jnp.sum(e * e, axis=0, keepdims=True)
def _rms_fwd(x, g, *, name):
    s, d = x.shape
    ts = _pick(s, (512, 256))

    def kern(x_ref, g_ref, h_ref):
        xf = x_ref[...]
        r = lax.rsqrt(jnp.mean(xf * xf, axis=-1, keepdims=True) + NORM_EPS)
        h_ref[...] = (xf * r * g_ref[...]).astype(h_ref.dtype)

    return _pcall(kern, name=name, grid=(s // ts,),
                  in_specs=[pl.BlockSpec((ts, d), lambda i: (i, 0)), pl.BlockSpec((1, d), lambda i: (0, 0))],
                  out_specs=pl.BlockSpec((ts, d), lambda i: (i, 0)), out_shape=jax.ShapeDtypeStruct((s, d), BF16),
                  compiler_params=_params("parallel"))(x, g)


def _rms_bwd(x, g, dh, res, *, name):
    s, d = x.shape
    ts = _pick(s, (512, 256))

    def kern(*refs):
        if res is None:
            x_ref, g_ref, dh_ref, dx_ref, dxb_ref, dg_ref = refs
            r_ref = None
        else:
            x_ref, g_ref, dh_ref, r_ref, dx_ref, dxb_ref, dg_ref = refs
        xf = x_ref[...]
        r = lax.rsqrt(jnp.mean(xf * xf, axis=-1, keepdims=True) + NORM_EPS)
        xh = xf * r
        dhf = dh_ref[...].astype(F32)
        dy = dhf * g_ref[...]
        dx = r * (dy - xh * jnp.mean(dy * xh, axis=-1, keepdims=True))
        if r_ref is not None:
            dx = r_ref[...] + dx
        dx_ref[...] = dx
        dxb_ref[...] = dx.astype(dxb_ref.dtype)

        @pl.when(pl.program_id(0) == 0)
        def _():
            dg_ref[...] = jnp.zeros_like(dg_ref)

        dg_ref[...] += jnp.sum(dhf * xh, axis=0, keepdims=True)

    row = pl.BlockSpec((ts, d), lambda i: (i, 0))
    vec = pl.BlockSpec((1, d), lambda i: (0, 0))
    ins = [x, g, dh] + ([] if res is None else [res])
    return _pcall(kern, name=name, grid=(s // ts,), in_specs=[row, vec, row] + ([] if res is None else [row]),
                  out_specs=[row, row, vec],
                  out_shape=[jax.ShapeDtypeStruct((s, d), F32), jax.ShapeDtypeStruct((s, d), BF16), jax.ShapeDtypeStruct((1, d), F32)],
                  compiler_params=_params("arbitrary"))(*ins)


def _sigmoid(x):
    return 1.0 / (1.0 + jnp.exp(-x))


FFN_TM, FFN_TF = 512, 1408


def _ffn_up(h, w1, w3, *, name):
    s, d = h.shape
    fdim = w1.shape[1]
    tm, tf = _pick(s, (FFN_TM, 256)), _pick(fdim, (FFN_TF, 512, 256, 128))

    def kern(h_ref, w1_ref, w3_ref, a_ref, b_ref, f_ref):
        hb = h_ref[...]
        a = _nn(hb, w1_ref[...])
        b = _nn(hb, w3_ref[...])
        a_ref[...] = a.astype(a_ref.dtype)
        b_ref[...] = b.astype(b_ref.dtype)
        f_ref[...] = (a * _sigmoid(a) * b).astype(f_ref.dtype)

    wspec = pl.BlockSpec((d, tf), lambda i, j: (0, j))
    ospec = pl.BlockSpec((tm, tf), lambda i, j: (i, j))
    shp = jax.ShapeDtypeStruct((s, fdim), BF16)
    return _pcall(kern, name=name, grid=(s // tm, fdim // tf), in_specs=[pl.BlockSpec((tm, d), lambda i, j: (i, 0)), wspec, wspec],
                  out_specs=[ospec, ospec, ospec], out_shape=[shp, shp, shp],
                  compiler_params=_params("parallel", "parallel"))(h, w1, w3)


def _ffn_dact(dy, w2, a, b, *, name):
    s, d = dy.shape
    fdim = w2.shape[0]
    tm, tf = _pick(s, (FFN_TM, 256)), _pick(fdim, (FFN_TF, 512, 256, 128))

    def kern(dy_ref, w2_ref, a_ref, b_ref, da_ref, db_ref):
        df = _nt(dy_ref[...], w2_ref[...]) * 0.5
        av = a_ref[...].astype(F32)
        sg = _sigmoid(av)
        da_ref[...] = (df * b_ref[...].astype(F32) * (sg + av * sg * (1.0 - sg))).astype(da_ref.dtype)
        db_ref[...] = (df * (av * sg)).astype(db_ref.dtype)

    ospec = pl.BlockSpec((tm, tf), lambda i, j: (i, j))
    shp = jax.ShapeDtypeStruct((s, fdim), BF16)
    return _pcall(kern, name=name, grid=(s // tm, fdim // tf),
                  in_specs=[pl.BlockSpec((tm, d), lambda i, j: (i, 0)), pl.BlockSpec((tf, d), lambda i, j: (j, 0)), ospec, ospec],
                  out_specs=[ospec, ospec], out_shape=[shp, shp], compiler_params=_params("parallel", "parallel"))(dy, w2, a, b)


def _loss_head(y, t, *, name):
    s, d = y.shape
    ts = _pick(s, (512, 256))
    n = s // ts

    def kern(y_ref, t_ref, dy_ref, dyb_ref, l_ref, acc_ref):
        i = pl.program_id(0)

        @pl.when(i == 0)
        def _():
            acc_ref[...] = jnp.zeros_like(acc_ref)

        e = y_ref[...] - t_ref[...]
        dy_ref[...] = e / d
        dyb_ref[...] = (e / d).astype(dyb_ref.dtype)
        acc_ref[...] += jnp.sum(e * e, axis=0, keepdims=True)

        @pl.when(i == n - 1)
        def _():
            l_ref[...] = jnp.sum(acc_ref[...], axis=1, keepdims=True) * (0.5 / d)

    row = pl.BlockSpec((ts, d), lambda i: (i, 0))
    return _pcall(kern, name=name, grid=(n,), in_specs=[row, row], out_specs=[row, row, pl.BlockSpec((1, 1), lambda i: (0, 0))],
                  out_shape=[jax.ShapeDtypeStruct((s, d), F32), jax.ShapeDtypeStruct((s, d), BF16), jax.ShapeDtypeStruct((1, 1), F32)],
                  scratch_shapes=[pltpu.VMEM((1, d), F32)], compiler_params=_params("arbitrary"))(y, t)


def _head_mean(v, bd):
    hi = v.astype(BF16)
    lo = (v - hi.astype(F32)).astype(BF16)
    return (lax.dot_general(hi, bd, (((1,), (0,)), ((), ())), preferred_element_type=F32)
            + lax.dot_general(lo, bd, (((1,), (0,)), ((), ())), preferred_element_type=F32))


def _partner(v):
    w = v.shape[1]
    lane = lax.broadcasted_iota(jnp.int32, v.shape, 1)
    return jnp.where(lane % HEAD_DIM < HEAD_DIM // 2, pltpu.roll(v, w - HEAD_DIM // 2, 1), pltpu.roll(v, HEAD_DIM // 2, 1))


def _block_diag(w):
    r = lax.broadcasted_iota(jnp.int32, (w, w), 0) // HEAD_DIM
    c = lax.broadcasted_iota(jnp.int32, (w, w), 1) // HEAD_DIM
    return jnp.where(r == c, 1.0 / HEAD_DIM, 0.0).astype(BF16)


def _rope_tables(s):
    half = HEAD_DIM // 2
    inv_freq = jnp.power(ROPE_THETA, -jnp.arange(half, dtype=F32) / half)
    ang = jnp.arange(s).astype(F32)[:, None] * inv_freq[None, :]
    cos, sin = jnp.cos(ang), jnp.sin(ang)
    cos2 = jnp.concatenate([cos, cos, cos, cos], axis=1)
    sin2 = jnp.concatenate([-sin, sin, -sin, sin], axis=1)
    return cos2, sin2


def _qknorm_fwd(src, col0, width, gain, rope, *, name, out_dtype=BF16):
    s = src.shape[0]
    ts = _pick(s, (512, 256))
    cb = col0 // width
    assert col0 % width == 0
    reps = width // LANES
    g = jnp.tile(gain, (1, width // HEAD_DIM))

    def kern(*refs):
        if rope is None:
            x_ref, g_ref, o_ref = refs
        else:
            x_ref, g_ref, c_ref, s_ref, o_ref = refs
        x = x_ref[...].astype(F32)
        bd = _block_diag(width)
        r = lax.rsqrt(_head_mean(x * x, bd) + NORM_EPS)
        y = x * r * g_ref[...]
        if rope is not None:
            y = y * jnp.tile(c_ref[...], (1, reps)) + _partner(y) * jnp.tile(s_ref[...], (1, reps))
        o_ref[...] = y.astype(o_ref.dtype)

    xs = pl.BlockSpec((ts, width), lambda i: (i, cb))
    tab = pl.BlockSpec((ts, LANES), lambda i: (i, 0))
    ins = [src, g] + ([] if rope is None else list(rope))
    specs = [xs, pl.BlockSpec((1, width), lambda i: (0, 0))] + ([] if rope is None else [tab, tab])
    return _pcall(kern, name=name, grid=(s // ts,), in_specs=specs, out_specs=pl.BlockSpec((ts, width), lambda i: (i, 0)),
                  out_shape=jax.ShapeDtypeStruct((s, width), out_dtype), compiler_params=_params("parallel"))(*ins)


def _qknorm_bwd(src, col0, width, gain, rope, dout, *, name):
    s = src.shape[0]
    ts = _pick(s, (512, 256))
    cb = col0 // width
    reps = width // LANES
    g = jnp.tile(gain, (1, width // HEAD_DIM))

    douts = list(dout) if isinstance(dout, (list, tuple)) else [dout]
    piece = width // len(douts)

    def kern(*refs):
        refs = list(refs)
        dg_ref = refs.pop()
        dx_ref = refs.pop()
        do_refs = [refs.pop() for _ in douts][::-1]
        if rope is None:
            x_ref, g_ref = refs
        else:
            x_ref, g_ref, c_ref, s_ref = refs
        x = x_ref[...].astype(F32)
        bd = _block_diag(width)
        r = lax.rsqrt(_head_mean(x * x, bd) + NORM_EPS)
        xh = x * r
        dy = jnp.concatenate([d[...].astype(F32) for d in do_refs], axis=1) if len(do_refs) > 1 else do_refs[0][...].astype(F32)
        if rope is not None:
            dy = dy * jnp.tile(c_ref[...], (1, reps)) + _partner(dy * jnp.tile(s_ref[...], (1, reps)))
        dxh = dy * g_ref[...]
        dx_ref[...] = (r * (dxh - xh * _head_mean(dxh * xh, bd))).astype(dx_ref.dtype)

        @pl.when(pl.program_id(0) == 0)
        def _():
            dg_ref[...] = jnp.zeros_like(dg_ref)

        dg_ref[...] += jnp.sum(dy * xh, axis=0, keepdims=True)

    xs = pl.BlockSpec((ts, width), lambda i: (i, cb))
    row = pl.BlockSpec((ts, width), lambda i: (i, 0))
    vec = pl.BlockSpec((1, width), lambda i: (0, 0))
    tab = pl.BlockSpec((ts, LANES), lambda i: (i, 0))
    ins = [src, g] + ([] if rope is None else list(rope)) + douts
    specs = [xs, vec] + ([] if rope is None else [tab, tab]) + [pl.BlockSpec((ts, piece), lambda i: (i, 0))] * len(douts)
    dx, dg = _pcall(kern, name=name, grid=(s // ts,), in_specs=specs, out_specs=[row, vec],
                    out_shape=[jax.ShapeDtypeStruct((s, width), BF16), jax.ShapeDtypeStruct((1, width), F32)],
                    compiler_params=_params("arbitrary"))(*ins)
    return dx, jnp.sum(dg.reshape(width // HEAD_DIM, HEAD_DIM), axis=0, keepdims=True)


def _tri(strict):
    r = lax.broadcasted_iota(jnp.int32, (2 * QB, QB), 0) % QB
    c = lax.broadcasted_iota(jnp.int32, (2 * QB, QB), 1)
    return jnp.where((r > c) if strict else (r >= c), 1.0, 0.0).astype(BF16)


def _split_dot(v, t2):
    hi = v.astype(BF16)
    lo = (v - hi.astype(F32)).astype(BF16)
    return lax.dot_general(jnp.concatenate([hi, lo], axis=1), t2, (((1,), (0,)), ((), ())), preferred_element_type=F32)


LOG2E = 1.4426950408889634


def _log2_sigmoids(z2):
    lf = -(jnp.maximum(z2, 0.0) + jnp.log2(1.0 + jnp.exp2(-jnp.abs(z2))))
    return z2 + lf, lf


def _key_blocks(t):
    s = t.shape[0]
    n = t.shape[1] // HEAD_DIM
    return t.reshape(s // QB, QB, n, HEAD_DIM).transpose(2, 0, 3, 1)


def _from_key_blocks(t):
    n, nb, hd, qb = t.shape
    return t.transpose(1, 3, 0, 2).reshape(nb * qb, n * hd)


SB_SUB = 4
SB2_SUB = 2


def _first_half(shape):
    return lax.broadcasted_iota(jnp.int32, shape, 1) < HEAD_DIM


def _split_pair(t, first):
    zero = jnp.zeros_like(t)
    return [jnp.where(first, t, zero), jnp.where(first, zero, t)]


def _sb2_fwd(p, *, name):
    s = p.shape[0]
    rq = SB2_SUB * QB
    nq = s // rq
    npair = SB_W // LANES

    def kern(q_ref, k_ref, v_ref, o_ref):
        i = pl.program_id(1)
        first = _first_half((rq, LANES))
        qs = _split_pair(q_ref[...], first)
        t2 = _tri(True)
        rel = lax.broadcasted_iota(jnp.int32, (rq, QB), 1) - lax.broadcasted_iota(jnp.int32, (rq, QB), 0)

        def tile(j, carries, accs, masked):
            off = pl.multiple_of(j * QB, QB)
            kt = k_ref[pl.ds(off, QB), :]
            vt = v_ref[pl.ds(off, QB), :]
            out_c, out_a = [], []
            for e in range(2):
                ls, lf = _log2_sigmoids(_nt(qs[e], kt) * (SCALE * LOG2E))
                if masked:
                    before = rel < i * rq - j * QB
                    lf = jnp.where(before, lf, 0.0)
                w = jnp.exp2(ls + _split_dot(lf, t2) + carries[e])
                if masked:
                    w = jnp.where(before, w, 0.0)
                out_c.append(carries[e] + jnp.sum(lf, axis=1, keepdims=True))
                out_a.append(accs[e] + _nn(w, vt))
            return out_c, out_a

        carries = [jnp.zeros((rq, 1), F32)] * 2
        accs = [jnp.zeros((rq, LANES), F32)] * 2
        for a in range(SB2_SUB):
            carries, accs = tile(i * SB2_SUB + (SB2_SUB - 1 - a), carries, accs, True)

        def cond(st):
            return jnp.logical_and(st[0] >= 0, st[1] > 0)

        def body(st):
            carries, accs = tile(st[0], [st[2], st[3]], [st[4], st[5]], False)
            alive = jnp.maximum(jnp.max(carries[0]), jnp.max(carries[1])) > SB_DEAD
            return st[0] - 1, alive.astype(jnp.int32), carries[0], carries[1], accs[0], accs[1]

        st = lax.while_loop(cond, body, (i * SB2_SUB - 1, jnp.int32(1), carries[0], carries[1], accs[0], accs[1]))
        o_ref[...] = jnp.where(first, st[4], st[5])

    return _pcall(kern, name=name, grid=(npair, nq),
                  in_specs=[pl.BlockSpec((rq, LANES), lambda a, i: (i, a)), pl.BlockSpec((s, LANES), lambda a, i: (0, npair + a)),
                            pl.BlockSpec((s, LANES), lambda a, i: (0, 2 * npair + a))],
                  out_specs=pl.BlockSpec((rq, LANES), lambda a, i: (i, a)), out_shape=jax.ShapeDtypeStruct((s, SB_W), F32),
                  compiler_params=_params("parallel", "arbitrary"))(p, p, p)


def _sb2_bwd(p, o, do, *, name):
    s = p.shape[0]
    rq = SB2_SUB * QB
    nq = s // rq
    npair = SB_W // LANES

    def kern(q_ref, k_ref, v_ref, o_ref, do_ref, dq_ref, dk_hbm, dv_hbm, dk_acc, dv_acc, sem):
        pr = pl.program_id(0)
        i = pl.program_id(1)

        @pl.when(i == 0)
        def _():
            dk_acc[...] = jnp.zeros_like(dk_acc)
            dv_acc[...] = jnp.zeros_like(dv_acc)

        first = _first_half((rq, LANES))
        qs = _split_pair(q_ref[...], first)
        do2 = do_ref[...]
        dos = _split_pair(do2, first)
        prod = do2.astype(F32) * o_ref[...]
        dsums = [jnp.sum(jnp.where(first, prod, 0.0), axis=1, keepdims=True),
                 jnp.sum(jnp.where(first, 0.0, prod), axis=1, keepdims=True)]
        t_strict = _tri(True)
        t_incl = _tri(False)
        rel = lax.broadcasted_iota(jnp.int32, (rq, QB), 1) - lax.broadcasted_iota(jnp.int32, (rq, QB), 0)

        def tile(j, carries, gcarries, dqs, masked):
            off = pl.multiple_of(j * QB, QB)
            kt = k_ref[pl.ds(off, QB), :]
            vt = v_ref[pl.ds(off, QB), :]
            out_c, out_g, out_q = [], [], []
            dk_t = jnp.zeros((QB, LANES), F32)
            dv_t = jnp.zeros((QB, LANES), F32)
            for e in range(2):
                ls, lf = _log2_sigmoids(_nt(qs[e], kt) * (SCALE * LOG2E))
                if masked:
                    before = rel < i * rq - j * QB
                    lf = jnp.where(before, lf, 0.0)
                w = jnp.exp2(ls + _split_dot(lf, t_strict) + carries[e])
                if masked:
                    w = jnp.where(before, w, 0.0)
                wr = w.astype(MXU_DT)
                g = _nt(dos[e], vt) * wr.astype(F32)
                big_g = dsums[e] - (_split_dot(g, t_incl) + gcarries[e])
                sig = jnp.exp2(ls)
                dz = g * (1.0 - sig) - sig * big_g
                if masked:
                    dz = jnp.where(before, dz, 0.0)
                dz = dz * SCALE
                dk_t = dk_t + _tn(dz, qs[e])
                dv_t = dv_t + _tn(wr, dos[e])
                out_c.append(carries[e] + jnp.sum(lf, axis=1, keepdims=True))
                out_g.append(gcarries[e] + jnp.sum(g, axis=1, keepdims=True))
                out_q.append(dqs[e] + _nn(dz, kt))
            dk_acc[pl.ds(off, QB), :] += dk_t
            dv_acc[pl.ds(off, QB), :] += dv_t
            return out_c, out_g, out_q

        zc = [jnp.zeros((rq, 1), F32)] * 2
        carries, gcarries, dqs = zc, zc, [jnp.zeros((rq, LANES), F32)] * 2
        for a in range(SB2_SUB):
            carries, gcarries, dqs = tile(i * SB2_SUB + (SB2_SUB - 1 - a), carries, gcarries, dqs, True)

        def cond(st):
            return jnp.logical_and(st[0] >= 0, st[1] > 0)

        def body(st):
            carries, gcarries, dqs = tile(st[0], [st[2], st[3]], [st[4], st[5]], [st[6], st[7]], False)
            alive = jnp.maximum(jnp.max(carries[0]), jnp.max(carries[1])) > SB_DEAD
            return (st[0] - 1, alive.astype(jnp.int32), carries[0], carries[1], gcarries[0], gcarries[1], dqs[0], dqs[1])

        st = lax.while_loop(cond, body, (i * SB2_SUB - 1, jnp.int32(1), carries[0], carries[1], gcarries[0], gcarries[1],
                                         dqs[0], dqs[1]))
        dq_ref[...] = jnp.where(first, st[6], st[7])

        @pl.when(i == nq - 1)
        def _():
            cols = pl.ds(pl.multiple_of(pr * LANES, LANES), LANES)
            ck = pltpu.make_async_copy(dk_acc, dk_hbm.at[:, cols], sem.at[0])
            cv = pltpu.make_async_copy(dv_acc, dv_hbm.at[:, cols], sem.at[1])
            ck.start()
            cv.start()
            ck.wait()
            cv.wait()

    blk = pl.BlockSpec((rq, LANES), lambda a, i: (i, a))
    anyspace = pl.BlockSpec(memory_space=pl.ANY)
    shp = jax.ShapeDtypeStruct((s, SB_W), F32)
    return _pcall(kern, name=name, grid=(npair, nq),
                  in_specs=[blk, pl.BlockSpec((s, LANES), lambda a, i: (0, npair + a)),
                            pl.BlockSpec((s, LANES), lambda a, i: (0, 2 * npair + a)), blk, blk],
                  out_specs=[blk, anyspace, anyspace], out_shape=[shp, shp, shp],
                  scratch_shapes=[pltpu.VMEM((s, LANES), F32), pltpu.VMEM((s, LANES), F32), pltpu.SemaphoreType.DMA((2,))],
                  compiler_params=_params("arbitrary", "arbitrary"))(p, p, p, o, do)


def _sb_fwd(q, kt, vt, *, name):
    h, s, hd = q.shape
    rq = SB_SUB * QB
    nq = s // rq
    nb = s // QB

    def kern(q_ref, k_ref, v_ref, o_ref):
        i = pl.program_id(1)
        qb = q_ref[...]
        t2 = _tri(True)
        rel = lax.broadcasted_iota(jnp.int32, (rq, QB), 1) - lax.broadcasted_iota(jnp.int32, (rq, QB), 0)

        def tile(j, carry, acc, masked):
            ls, lf = _log2_sigmoids(_nn(qb, k_ref[j]) * (SCALE * LOG2E))
            if masked:
                before = rel < i * rq - j * QB
                lf = jnp.where(before, lf, 0.0)
            w = jnp.exp2(ls + _split_dot(lf, t2) + carry)
            if masked:
                w = jnp.where(before, w, 0.0)
            return carry + jnp.sum(lf, axis=1, keepdims=True), acc + _nt(w, v_ref[j])

        carry, acc = jnp.zeros((rq, 1), F32), jnp.zeros((rq, hd), F32)
        for a in range(SB_SUB):
            carry, acc = tile(i * SB_SUB + (SB_SUB - 1 - a), carry, acc, True)

        def cond(st):
            return jnp.logical_and(st[0] >= 0, st[1] > 0)

        def body(st):
            j, _, carry, acc = st
            carry, acc = tile(j, carry, acc, False)
            return j - 1, (jnp.max(carry) > SB_DEAD).astype(jnp.int32), carry, acc

        _, _, _, acc = lax.while_loop(cond, body, (i * SB_SUB - 1, jnp.int32(1), carry, acc))
        o_ref[...] = acc

    blk = pl.BlockSpec((None, rq, hd), lambda a, i: (a, i, 0))
    full = pl.BlockSpec((None, nb, hd, QB), lambda a, i: (a, 0, 0, 0))
    return _pcall(kern, name=name, grid=(h, nq), in_specs=[blk, full, full], out_specs=blk,
                  out_shape=jax.ShapeDtypeStruct((h, s, hd), F32), compiler_params=_params("parallel", "arbitrary"))(q, kt, vt)


def _sb_bwd(q, kt, vt, o, do, *, name):
    h, s, hd = q.shape
    rq = SB_SUB * QB
    nq = s // rq
    nb = s // QB

    def kern(q_ref, k_ref, v_ref, o_ref, do_ref, dq_ref, dk_ref, dv_ref):
        i = pl.program_id(1)

        @pl.when(i == 0)
        def _():
            dk_ref[...] = jnp.zeros_like(dk_ref)
            dv_ref[...] = jnp.zeros_like(dv_ref)

        qb = q_ref[...]
        dob = do_ref[...]
        dsum = jnp.sum(dob.astype(F32) * o_ref[...], axis=1, keepdims=True)
        t_strict = _tri(True)
        t_incl = _tri(False)
        rel = lax.broadcasted_iota(jnp.int32, (rq, QB), 1) - lax.broadcasted_iota(jnp.int32, (rq, QB), 0)

        def tile(j, carry, gcarry, dq, masked):
            kb = k_ref[j]
            ls, lf = _log2_sigmoids(_nn(qb, kb) * (SCALE * LOG2E))
            if masked:
                before = rel < i * rq - j * QB
                lf = jnp.where(before, lf, 0.0)
            w = jnp.exp2(ls + _split_dot(lf, t_strict) + carry)
            if masked:
                w = jnp.where(before, w, 0.0)
            wr = w.astype(MXU_DT)
            g = _nn(dob, v_ref[j]) * wr.astype(F32)
            big_g = dsum - (_split_dot(g, t_incl) + gcarry)
            sig = jnp.exp2(ls)
            dz = g * (1.0 - sig) - sig * big_g
            if masked:
                dz = jnp.where(before, dz, 0.0)
            dz = dz * SCALE
            dk_ref[j] += _tn(qb, dz)
            dv_ref[j] += _tn(dob, wr)
            return (carry + jnp.sum(lf, axis=1, keepdims=True), gcarry + jnp.sum(g, axis=1, keepdims=True),
                    dq + _nt(dz, kb))

        carry, gcarry, dq = jnp.zeros((rq, 1), F32), jnp.zeros((rq, 1), F32), jnp.zeros((rq, hd), F32)
        for a in range(SB_SUB):
            carry, gcarry, dq = tile(i * SB_SUB + (SB_SUB - 1 - a), carry, gcarry, dq, True)

        def cond(st):
            return jnp.logical_and(st[0] >= 0, st[1] > 0)

        def body(st):
            j, _, carry, gcarry, dq = st
            carry, gcarry, dq = tile(j, carry, gcarry, dq, False)
            return j - 1, (jnp.max(carry) > SB_DEAD).astype(jnp.int32), carry, gcarry, dq

        st = lax.while_loop(cond, body, (i * SB_SUB - 1, jnp.int32(1), carry, gcarry, dq))
        dq_ref[...] = st[4]

    blk = pl.BlockSpec((None, rq, hd), lambda a, i: (a, i, 0))
    full = pl.BlockSpec((None, nb, hd, QB), lambda a, i: (a, 0, 0, 0))
    kshape = jax.ShapeDtypeStruct((h, nb, hd, QB), F32)
    return _pcall(kern, name=name, grid=(h, nq), in_specs=[blk, full, full, blk, blk], out_specs=[blk, full, full],
                  out_shape=[jax.ShapeDtypeStruct((h, s, hd), F32), kshape, kshape],
                  compiler_params=_params("parallel", "arbitrary"))(q, kt, vt, o, do)


DSA_SUB = 4


def _dsa_seq_blocks(t, s):
    steps_per_group = 4 * s // (QB * DSA_SUB)
    g = t // steps_per_group
    b0, b1, b2 = (s // (QB * r) for _, r in DSA_GROUPS)
    return jnp.where(g == 0, b0, jnp.where(g == 1, b1, b2))


def _dsa_rel():
    qi = lax.broadcasted_iota(jnp.int32, (QB, QB), 0)
    kj = lax.broadcasted_iota(jnp.int32, (QB, QB), 1)
    return kj - qi


def _prev_mask(rel, has_prev):
    return rel >= jnp.where(has_prev, 0, QB)


def _dsa_fwd(q, k, vp, *, name):
    rows = q.shape[0]
    s = rows // 12
    big = QB * DSA_SUB
    nsteps = rows // big

    def kern(q_ref, k_ref, kp_ref, v_ref, vpv_ref, o_ref):
        t = pl.program_id(0)
        bps = _dsa_seq_blocks(t, s)
        rel = _dsa_rel()
        lane = lax.broadcasted_iota(jnp.int32, (QB, LANES), 1)
        for a in range(DSA_SUB):
            qa = q_ref[pl.ds(a * QB, QB), :]
            kc = k_ref[pl.ds(a * QB, QB), :]
            vc = v_ref[pl.ds(a * QB, QB), :]
            if a == 0:
                kpv, vpv = kp_ref[...], vpv_ref[...]
            else:
                kpv, vpv = k_ref[pl.ds((a - 1) * QB, QB), :], v_ref[pl.ds((a - 1) * QB, QB), :]
            has_prev = (t * DSA_SUB + a) % bps != 0
            sc = jnp.where(rel <= 0, _nt(qa, kc) * SCALE, -jnp.inf)
            sp = jnp.where(_prev_mask(rel, has_prev), _nt(qa, kpv) * SCALE, -jnp.inf)
            m = jnp.maximum(jnp.max(sc, axis=1, keepdims=True), jnp.max(sp, axis=1, keepdims=True))
            pc = jnp.exp(sc - m)
            pp = jnp.exp(sp - m)
            den = jnp.sum(pc, axis=1, keepdims=True) + jnp.sum(pp, axis=1, keepdims=True)
            o = (_nn(pc, vc) + _nn(pp, vpv)) / den
            o_ref[pl.ds(a * QB, QB), :] = jnp.where(lane < HEAD_DIM, o, m + jnp.log(den))

    cur64 = pl.BlockSpec((big, HEAD_DIM), lambda t: (t, 0))
    prev64 = pl.BlockSpec((QB, HEAD_DIM), lambda t: (jnp.maximum(t * DSA_SUB - 1, 0), 0))
    cur128 = pl.BlockSpec((big, LANES), lambda t: (t, 0))
    prev128 = pl.BlockSpec((QB, LANES), lambda t: (jnp.maximum(t * DSA_SUB - 1, 0), 0))
    return _pcall(kern, name=name, grid=(nsteps,), in_specs=[cur64, cur64, prev64, cur128, prev128], out_specs=cur128,
                  out_shape=jax.ShapeDtypeStruct((rows, LANES), F32), compiler_params=_params("parallel"))(q, k, k, vp, vp)


def _dsa_combine(p0, p1, p2, *, name):
    hh, s, _ = p0.shape
    ts = _pick(s, (512, 256))

    def kern(a_ref, b_ref, c_ref, o_ref):
        lane = lax.broadcasted_iota(jnp.int32, (ts, LANES), 1)
        xs = [a_ref[...], b_ref[...], c_ref[...]]
        ls = [jnp.where(lane < HEAD_DIM, pltpu.roll(x, HEAD_DIM, 1), x) for x in xs]
        m = jnp.maximum(jnp.maximum(ls[0], ls[1]), ls[2])
        es = [jnp.exp(l - m) for l in ls]
        den = es[0] + es[1] + es[2]
        o = (es[0] * xs[0] + es[1] * xs[1] + es[2] * xs[2]) / den
        o_ref[...] = jnp.where(lane < HEAD_DIM, o, m + jnp.log(den))

    blk = pl.BlockSpec((None, ts, LANES), lambda a, i: (a, i, 0))
    return _pcall(kern, name=name, grid=(hh, s // ts), in_specs=[blk, blk, blk], out_specs=blk,
                  out_shape=jax.ShapeDtypeStruct((hh, s, LANES), F32), compiler_params=_params("parallel", "parallel"))(p0, p1, p2)


def _dsa_bwd_prep(comb, dop, *, name):
    hh, s, _ = comb.shape
    ts = _pick(s, (512, 256))

    def kern(c_ref, d_ref, o_ref):
        lane = lax.broadcasted_iota(jnp.int32, (ts, LANES), 1)
        c = c_ref[...]
        d = d_ref[...]
        dsum = jnp.sum(jnp.where(lane < HEAD_DIM, c * d, 0.0), axis=1, keepdims=True)
        o_ref[...] = jnp.where(lane < HEAD_DIM, d, jnp.where(lane < HEAD_DIM + 32, c, dsum))

    blk = pl.BlockSpec((None, ts, LANES), lambda a, i: (a, i, 0))
    return _pcall(kern, name=name, grid=(hh, s // ts), in_specs=[blk, blk], out_specs=blk,
                  out_shape=jax.ShapeDtypeStruct((hh, s, LANES), F32), compiler_params=_params("parallel", "parallel"))(comb, dop)


def _dsa_bwd(q, k, vp, pk, *, name):
    rows = q.shape[0]
    s = rows // 12
    big = QB * DSA_SUB
    nsteps = rows // big
    nblk = rows // QB

    def kern(q_ref, qn_ref, k_ref, kp_ref, v_ref, vpv_ref, p_ref, pn_ref, dq_ref, dk_ref, dv_ref):
        t = pl.program_id(0)
        bps = _dsa_seq_blocks(t, s)
        rel = _dsa_rel()
        lane = lax.broadcasted_iota(jnp.int32, (QB, LANES), 1)

        def stats(pa):
            lse = jnp.max(jnp.where(jnp.logical_and(lane >= HEAD_DIM, lane < HEAD_DIM + 32), pa, -jnp.inf), axis=1, keepdims=True)
            dsum = jnp.max(jnp.where(lane >= HEAD_DIM + 32, pa, -jnp.inf), axis=1, keepdims=True)
            return lse, dsum

        def pair(qa, pa, st, kb, vb, mask):
            p = jnp.where(mask, jnp.exp(_nt(qa, kb) * SCALE - st[0]), 0.0)
            ds = p * (_nt(pa, vb) - st[1]) * SCALE
            return _nn(ds, kb), _tn(ds, qa), _tn(p, pa)

        for a in range(DSA_SUB):
            qa = q_ref[pl.ds(a * QB, QB), :]
            pa = p_ref[pl.ds(a * QB, QB), :]
            st = stats(pa)
            kc = k_ref[pl.ds(a * QB, QB), :]
            vc = v_ref[pl.ds(a * QB, QB), :]
            if a == 0:
                kpv, vpv = kp_ref[...], vpv_ref[...]
            else:
                kpv, vpv = k_ref[pl.ds((a - 1) * QB, QB), :], v_ref[pl.ds((a - 1) * QB, QB), :]
            has_prev = (t * DSA_SUB + a) % bps != 0
            dq_c, dk_c, dv_c = pair(qa, pa, st, kc, vc, rel <= 0)
            dq_p, dk_p, dv_p = pair(qa, pa, st, kpv, vpv, _prev_mask(rel, has_prev))
            dq_ref[pl.ds(a * QB, QB), :] = dq_c + dq_p
            if a == 0:
                dk_ref[pl.ds(0, QB), :] = dk_c
                dv_ref[pl.ds(0, QB), :] = dv_c
            else:
                dk_ref[pl.ds(a * QB, QB), :] = dk_c
                dv_ref[pl.ds(a * QB, QB), :] = dv_c
                dk_ref[pl.ds((a - 1) * QB, QB), :] += dk_p
                dv_ref[pl.ds((a - 1) * QB, QB), :] += dv_p
        nxt = t * DSA_SUB + DSA_SUB
        has_next = jnp.logical_and(nxt < nblk, nxt % bps != 0)
        last = (DSA_SUB - 1) * QB
        pn = pn_ref[...]
        _, dk_n, dv_n = pair(qn_ref[...], pn, stats(pn), k_ref[pl.ds(last, QB), :], v_ref[pl.ds(last, QB), :],
                             _prev_mask(rel, has_next))
        dk_ref[pl.ds(last, QB), :] += dk_n
        dv_ref[pl.ds(last, QB), :] += dv_n

    def prev_map(t):
        return (jnp.maximum(t * DSA_SUB - 1, 0), 0)

    def next_map(t):
        return (jnp.minimum(t * DSA_SUB + DSA_SUB, nblk - 1), 0)

    cur64 = pl.BlockSpec((big, HEAD_DIM), lambda t: (t, 0))
    cur128 = pl.BlockSpec((big, LANES), lambda t: (t, 0))
    specs = [cur64, pl.BlockSpec((QB, HEAD_DIM), next_map), cur64, pl.BlockSpec((QB, HEAD_DIM), prev_map),
             cur128, pl.BlockSpec((QB, LANES), prev_map), cur128, pl.BlockSpec((QB, LANES), next_map)]
    return _pcall(kern, name=name, grid=(nsteps,), in_specs=specs, out_specs=[cur64, cur64, cur128],
                  out_shape=[jax.ShapeDtypeStruct((rows, HEAD_DIM), F32), jax.ShapeDtypeStruct((rows, HEAD_DIM), F32),
                             jax.ShapeDtypeStruct((rows, LANES), F32)],
                  compiler_params=_params("parallel"))(q, q, k, k, vp, vp, pk, pk)


def _mem_fwd(q, km, vm, *, name):
    hh, s, hd = q.shape
    ml = km.shape[1]
    tq = _pick(s, (512, 256))

    def kern(q_ref, k_ref, v_ref, o_ref):
        sc = _nt(q_ref[...], k_ref[...]) * SCALE
        e = jnp.exp(sc - jnp.max(sc, axis=1, keepdims=True))
        p = e / jnp.sum(e, axis=1, keepdims=True)
        o_ref[...] = _nn(p, v_ref[...])

    blk = pl.BlockSpec((None, tq, hd), lambda a, i: (a, i, 0))
    kv = pl.BlockSpec((None, ml, hd), lambda a, i: (a, 0, 0))
    return _pcall(kern, name=name, grid=(hh, s // tq), in_specs=[blk, kv, kv], out_specs=blk,
                  out_shape=jax.ShapeDtypeStruct((hh, s, hd), F32), compiler_params=_params("parallel", "parallel"))(q, km, vm)


def _mem_bwd(q, km, vm, do, *, name):
    hh, s, hd = q.shape
    ml = km.shape[1]
    tq = _pick(s, (512, 256))

    def kern(q_ref, k_ref, v_ref, do_ref, dq_ref, dk_ref, dv_ref):
        @pl.when(pl.program_id(1) == 0)
        def _():
            dk_ref[...] = jnp.zeros_like(dk_ref)
            dv_ref[...] = jnp.zeros_like(dv_ref)

        qb = q_ref[...]
        dob = do_ref[...]
        sc = _nt(qb, k_ref[...]) * SCALE
        e = jnp.exp(sc - jnp.max(sc, axis=1, keepdims=True))
        p = e / jnp.sum(e, axis=1, keepdims=True)
        dp = _nt(dob, v_ref[...])
        ds = p * (dp - jnp.sum(p * dp, axis=1, keepdims=True)) * SCALE
        dq_ref[...] = _nn(ds, k_ref[...])
        dk_ref[...] += _tn(ds, qb)
        dv_ref[...] += _tn(p, dob)

    blk = pl.BlockSpec((None, tq, hd), lambda a, i: (a, i, 0))
    kv = pl.BlockSpec((None, ml, hd), lambda a, i: (a, 0, 0))
    kvs = jax.ShapeDtypeStruct((hh, ml, hd), F32)
    return _pcall(kern, name=name, grid=(hh, s // tq), in_specs=[blk, kv, kv, blk], out_specs=[blk, kv, kv],
                  out_shape=[jax.ShapeDtypeStruct((hh, s, hd), F32), kvs, kvs],
                  compiler_params=_params("parallel", "arbitrary"))(q, km, vm, do)


DSA_BT = QB * max(r for _, r in DSA_GROUPS)


def _unit_rows(r, c, b):
    return pl.ds(c + QB * r * b, QB, stride=r)


def _pair_cols(t, first):
    return [jnp.max(jnp.where(first, t, -jnp.inf), axis=1, keepdims=True),
            jnp.max(jnp.where(first, -jnp.inf, t), axis=1, keepdims=True)]


def _dsa2_fwd(qn, kn, v32, g, *, name):
    s = qn.shape[0]
    r = DSA_GROUPS[g][1]
    nbk = DSA_BT // (QB * r)
    npair = DSA_OUT_W // LANES

    def kern(q_ref, k_ref, kp_ref, v_ref, vp_ref, o_ref, l_ref):
        t = pl.program_id(1)
        first = _first_half((QB, LANES))
        rel = _dsa_rel()
        for c in range(r):
            for b in range(nbk):
                rows = _unit_rows(r, c, b)
                kc, vc = k_ref[rows, :], v_ref[rows, :]
                if b > 0:
                    prow = _unit_rows(r, c, b - 1)
                    kpv, vpv, has_prev = k_ref[prow, :], v_ref[prow, :], True
                else:
                    prow = _unit_rows(r, c, nbk - 1)
                    kpv, vpv, has_prev = kp_ref[prow, :], vp_ref[prow, :], t > 0
                outs, lses = [], []
                for qe in _split_pair(q_ref[rows, :], first):
                    sc = jnp.where(rel <= 0, _nt(qe, kc) * SCALE, -jnp.inf)
                    sp = jnp.where(_prev_mask(rel, has_prev), _nt(qe, kpv) * SCALE, -jnp.inf)
                    m = jnp.maximum(jnp.max(sc, axis=1, keepdims=True), jnp.max(sp, axis=1, keepdims=True))
                    pc = jnp.exp(sc - m)
                    pp = jnp.exp(sp - m)
                    den = jnp.sum(pc, axis=1, keepdims=True) + jnp.sum(pp, axis=1, keepdims=True)
                    outs.append((_nn(pc, vc) + _nn(pp, vpv)) / den)
                    lses.append(m + jnp.log(den))
                o_ref[rows, :] = jnp.where(first, outs[0], outs[1])
                l_ref[rows, :] = jnp.where(first, lses[0], lses[1])

    npg = DSA_HPG * HEAD_DIM // LANES
    cur = pl.BlockSpec((DSA_BT, LANES), lambda a, t: (t, npg * g + a))
    prev = pl.BlockSpec((DSA_BT, LANES), lambda a, t: (jnp.maximum(t - 1, 0), npg * g + a))
    out = pl.BlockSpec((DSA_BT, LANES), lambda a, t: (t, a))
    shp = jax.ShapeDtypeStruct((s, DSA_OUT_W), F32)
    return _pcall(kern, name=name, grid=(npair, s // DSA_BT), in_specs=[cur, cur, prev, cur, prev], out_specs=[out, out],
                  out_shape=[shp, shp], compiler_params=_params("parallel", "parallel"))(qn, kn, kn, v32, v32)


def _dsa2_combine(parts, *, name):
    s, wd = parts[0][0].shape
    ts = _pick(s, (512, 256))

    def kern(o0, l0, o1, l1, o2, l2, o_ref, l_ref):
        ls = [l0[...], l1[...], l2[...]]
        m = jnp.maximum(jnp.maximum(ls[0], ls[1]), ls[2])
        es = [jnp.exp(l - m) for l in ls]
        den = es[0] + es[1] + es[2]
        o_ref[...] = (es[0] * o0[...] + es[1] * o1[...] + es[2] * o2[...]) / den
        l_ref[...] = m + jnp.log(den)

    blk = pl.BlockSpec((ts, wd), lambda i: (i, 0))
    shp = jax.ShapeDtypeStruct((s, wd), F32)
    flat = [t for pair in parts for t in pair]
    return _pcall(kern, name=name, grid=(s // ts,), in_specs=[blk] * 6, out_specs=[blk, blk], out_shape=[shp, shp],
                  compiler_params=_params("parallel"))(*flat)


def _dsa2_prep(o, do, *, name):
    s, wd = o.shape
    ts = _pick(s, (512, 256))

    def kern(o_ref, do_ref, d_ref):
        d_ref[...] = _head_mean(do_ref[...] * o_ref[...], _block_diag(wd)) * HEAD_DIM

    blk = pl.BlockSpec((ts, wd), lambda i: (i, 0))
    return _pcall(kern, name=name, grid=(s // ts,), in_specs=[blk, blk], out_specs=blk,
                  out_shape=jax.ShapeDtypeStruct((s, wd), F32), compiler_params=_params("parallel"))(o, do)


def _dsa2_bwd(qn, kn, v32, do, lse, dd, g, *, name):
    s = qn.shape[0]
    r = DSA_GROUPS[g][1]
    nbk = DSA_BT // (QB * r)
    npair = DSA_OUT_W // LANES
    nsteps = s // DSA_BT

    def kern(q_ref, qn_ref, k_ref, kp_ref, v_ref, vp_ref, do_ref, don_ref, l_ref, ln_ref, d_ref, dn_ref,
             dq_ref, dk_ref, dv_ref):
        t = pl.program_id(1)
        first = _first_half((QB, LANES))
        rel = _dsa_rel()

        def pair(qs, dos, lcols, dcols, kb, vb, mask):
            dqs = []
            dk = jnp.zeros((QB, LANES), F32)
            dv = jnp.zeros((QB, LANES), F32)
            for e in range(2):
                p = jnp.where(mask, jnp.exp(_nt(qs[e], kb) * SCALE - lcols[e]), 0.0)
                ds = p * (_nt(dos[e], vb) - dcols[e]) * SCALE
                dqs.append(_nn(ds, kb))
                dk = dk + _tn(ds, qs[e])
                dv = dv + _tn(p, dos[e])
            return dqs, dk, dv

        def load(rows, qr, dor, lr, dr):
            return (_split_pair(qr[rows, :], first), _split_pair(dor[rows, :], first), _pair_cols(lr[rows, :], first),
                    _pair_cols(dr[rows, :], first))

        for c in range(r):
            for b in range(nbk):
                rows = _unit_rows(r, c, b)
                qs, dos, lcols, dcols = load(rows, q_ref, do_ref, l_ref, d_ref)
                dq_c, dk_c, dv_c = pair(qs, dos, lcols, dcols, k_ref[rows, :], v_ref[rows, :], rel <= 0)
                if b > 0:
                    prow = _unit_rows(r, c, b - 1)
                    dq_p, dk_p, dv_p = pair(qs, dos, lcols, dcols, k_ref[prow, :], v_ref[prow, :], _prev_mask(rel, True))
                    dk_ref[prow, :] += dk_p
                    dv_ref[prow, :] += dv_p
                else:
                    prow = _unit_rows(r, c, nbk - 1)
                    dq_p, _, _ = pair(qs, dos, lcols, dcols, kp_ref[prow, :], vp_ref[prow, :], _prev_mask(rel, t > 0))
                dq_ref[rows, :] = jnp.where(first, dq_c[0] + dq_p[0], dq_c[1] + dq_p[1])
                dk_ref[rows, :] = dk_c
                dv_ref[rows, :] = dv_c
            last = _unit_rows(r, c, nbk - 1)
            nqs, ndos, nl, nd = load(_unit_rows(r, c, 0), qn_ref, don_ref, ln_ref, dn_ref)
            _, dk_n, dv_n = pair(nqs, ndos, nl, nd, k_ref[last, :], v_ref[last, :], _prev_mask(rel, t < nsteps - 1))
            dk_ref[last, :] += dk_n
            dv_ref[last, :] += dv_n

    npg = DSA_HPG * HEAD_DIM // LANES

    def at(shift, col):
        return pl.BlockSpec((DSA_BT, LANES), lambda a, t: (jnp.clip(t + shift, 0, nsteps - 1), col(a)))

    gcol = lambda a: npg * g + a
    ocol = lambda a: a
    specs = [at(0, gcol), at(1, gcol), at(0, gcol), at(-1, gcol), at(0, gcol), at(-1, gcol),
             at(0, ocol), at(1, ocol), at(0, ocol), at(1, ocol), at(0, ocol), at(1, ocol)]
    shp = jax.ShapeDtypeStruct((s, DSA_OUT_W), F32)
    return _pcall(kern, name=name, grid=(npair, nsteps), in_specs=specs, out_specs=[at(0, ocol)] * 3, out_shape=[shp, shp, shp],
                  compiler_params=_params("parallel", "parallel"))(qn, qn, kn, kn, v32, v32, do, do, lse, lse, dd, dd)


def _mem2_fwd(qn, km, kv, *, name):
    s = qn.shape[0]
    ml = km.shape[0]
    tq = _pick(s, (512, 256))
    npair = MEM_W // LANES

    def kern(q_ref, k_ref, v_ref, o_ref):
        first = _first_half((tq, LANES))
        outs = []
        for qe in _split_pair(q_ref[...], first):
            sc = _nt(qe, k_ref[...]) * SCALE
            e = jnp.exp(sc - jnp.max(sc, axis=1, keepdims=True))
            outs.append(_nn(e / jnp.sum(e, axis=1, keepdims=True), v_ref[...]))
        o_ref[...] = jnp.where(first, outs[0], outs[1])

    blk = pl.BlockSpec((tq, LANES), lambda a, i: (i, a))
    return _pcall(kern, name=name, grid=(npair, s // tq),
                  in_specs=[blk, pl.BlockSpec((ml, LANES), lambda a, i: (0, a)), pl.BlockSpec((ml, LANES), lambda a, i: (0, npair + a))],
                  out_specs=blk, out_shape=jax.ShapeDtypeStruct((s, MEM_W), F32),
                  compiler_params=_params("parallel", "parallel"))(qn, km, kv)


def _mem2_bwd(qn, km, kv, do, *, name):
    s = qn.shape[0]
    ml = km.shape[0]
    tq = _pick(s, (512, 256))
    npair = MEM_W // LANES

    def kern(q_ref, k_ref, v_ref, do_ref, dq_ref, dk_ref, dv_ref):
        @pl.when(pl.program_id(1) == 0)
        def _():
            dk_ref[...] = jnp.zeros_like(dk_ref)
            dv_ref[...] = jnp.zeros_like(dv_ref)

        first = _first_half((tq, LANES))
        dqs = []
        for qe, doe in zip(_split_pair(q_ref[...], first), _split_pair(do_ref[...], first)):
            sc = _nt(qe, k_ref[...]) * SCALE
            e = jnp.exp(sc - jnp.max(sc, axis=1, keepdims=True))
            p = e / jnp.sum(e, axis=1, keepdims=True)
            dp = _nt(doe, v_ref[...])
            ds = p * (dp - jnp.sum(p * dp, axis=1, keepdims=True)) * SCALE
            dqs.append(_nn(ds, k_ref[...]))
            dk_ref[...] += _tn(ds, qe)
            dv_ref[...] += _tn(p, doe)
        dq_ref[...] = jnp.where(first, dqs[0], dqs[1])

    blk = pl.BlockSpec((tq, LANES), lambda a, i: (i, a))
    kblk = pl.BlockSpec((ml, LANES), lambda a, i: (0, a))
    kshape = jax.ShapeDtypeStruct((ml, MEM_W), F32)
    return _pcall(kern, name=name, grid=(npair, s // tq),
                  in_specs=[blk, kblk, pl.BlockSpec((ml, LANES), lambda a, i: (0, npair + a)), blk],
                  out_specs=[blk, kblk, kblk], out_shape=[jax.ShapeDtypeStruct((s, MEM_W), F32), kshape, kshape],
                  compiler_params=_params("parallel", "arbitrary"))(qn, km, kv, do)


def _merge_fwd(logits, bias, ya, yb, yc, *, name):
    s, d = ya.shape
    ts = _pick(s, (512, 256))

    def kern(l0, l1, l2, b0, b1, b2, a_ref, b_ref, c_ref, o_ref):
        m = (_sigmoid(l0[...] + b0[...]) * a_ref[...] + _sigmoid(l1[...] + b1[...]) * b_ref[...]
             + _sigmoid(l2[...] + b2[...]) * c_ref[...])
        o_ref[...] = m.astype(o_ref.dtype)

    row = pl.BlockSpec((ts, d), lambda i: (i, 0))
    lg = [pl.BlockSpec((ts, d), functools.partial(lambda i, c: (i, c), c=c)) for c in range(3)]
    bs = [pl.BlockSpec((1, d), functools.partial(lambda i, c: (0, c), c=c)) for c in range(3)]
    return _pcall(kern, name=name, grid=(s // ts,), in_specs=lg + bs + [row, row, row], out_specs=row,
                  out_shape=jax.ShapeDtypeStruct((s, d), BF16),
                  compiler_params=_params("parallel"))(logits, logits, logits, bias, bias, bias, ya, yb, yc)


def _merge_bwd(logits, bias, ya, yb, yc, dm, *, name):
    s, d = ya.shape
    ts = _pick(s, (256,))

    def kern(l0, l1, l2, b0, b1, b2, a_ref, b_ref, c_ref, dm_ref, da_ref, db_ref, dc_ref, dl0, dl1, dl2, dbias0, dbias1, dbias2):
        first = pl.program_id(0) == 0
        dmv = dm_ref[...]
        for l_ref, bb_ref, y_ref, dy_ref, dl_ref, dbias_ref in ((l0, b0, a_ref, da_ref, dl0, dbias0), (l1, b1, b_ref, db_ref, dl1, dbias1),
                                                                (l2, b2, c_ref, dc_ref, dl2, dbias2)):
            g = _sigmoid(l_ref[...] + bb_ref[...])
            dy_ref[...] = (dmv * g).astype(dy_ref.dtype)
            dl = dmv * y_ref[...] * g * (1.0 - g)
            dl_ref[...] = dl.astype(dl_ref.dtype)

            @pl.when(first)
            def _():
                dbias_ref[...] = jnp.zeros_like(dbias_ref)

            dbias_ref[...] += jnp.sum(dl, axis=0, keepdims=True)

    row = pl.BlockSpec((ts, d), lambda i: (i, 0))
    lg = [pl.BlockSpec((ts, d), functools.partial(lambda i, c: (i, c), c=c)) for c in range(3)]
    bs = [pl.BlockSpec((1, d), functools.partial(lambda i, c: (0, c), c=c)) for c in range(3)]
    vec = pl.BlockSpec((1, d), lambda i: (0, 0))
    yshape = jax.ShapeDtypeStruct((s, d), BF16)
    vshape = jax.ShapeDtypeStruct((1, d), F32)
    outs = _pcall(kern, name=name, grid=(s // ts,), in_specs=lg + bs + [row, row, row, row],
                  out_specs=[row, row, row, row, row, row, vec, vec, vec],
                  out_shape=[yshape] * 6 + [vshape] * 3,
                  compiler_params=_params("arbitrary"))(logits, logits, logits, bias, bias, bias, ya, yb, yc, dm)
    return outs[0], outs[1], outs[2], outs[3:6], jnp.concatenate(outs[6:9], axis=1)


def _heads(t, n):
    s = t.shape[0]
    return t.reshape(s, n, HEAD_DIM).transpose(1, 0, 2)


def _unheads(t):
    n, s, hd = t.shape
    return t.transpose(1, 0, 2).reshape(s, n * hd)


def _to_class_major(t):
    s = t.shape[0]
    w = t.shape[1] // (DSA_HPG * len(DSA_GROUPS))
    parts = []
    for g, (_, r) in enumerate(DSA_GROUPS):
        tg = t[:, g * DSA_HPG * w:(g + 1) * DSA_HPG * w].reshape(s // r, r, DSA_HPG, w)
        parts.append(tg.transpose(2, 1, 0, 3).reshape(DSA_HPG * s, w))
    return jnp.concatenate(parts, axis=0)


def _slot_to_class_major(t):
    hh, s, w = t.shape
    parts = []
    for _, r in DSA_GROUPS:
        parts.append(t.reshape(hh, s // r, r, w).transpose(0, 2, 1, 3).reshape(hh * s, w))
    return jnp.concatenate(parts, axis=0)


def _from_class_major(t):
    rows, w = t.shape
    s = rows // 12
    out = []
    for g, (_, r) in enumerate(DSA_GROUPS):
        tg = t[g * 4 * s:(g + 1) * 4 * s].reshape(DSA_HPG, r, s // r, w)
        out.append(tg.transpose(0, 2, 1, 3).reshape(DSA_HPG, s, w))
    return out


def _pad_lanes(t):
    return jnp.concatenate([t, jnp.zeros(t.shape[:-1] + (LANES - t.shape[-1],), t.dtype)], axis=-1)


def _ffn_fwd(x, h, w1, w3, w2, tag, epilogue):
    a, b, f = _ffn_up(h, w1, w3, name=f"{tag}_up")
    outs = _matmul(f, w2, name=f"{tag}_down", alpha=0.5, tm=512, tn=1024, tk=1408, epilogue=epilogue)
    return outs, (h, a, b, f)


def _ffn_bwd(x, norm, w1, w3, w2, saved, dy, dyb, tag):
    h, a, b, f = saved
    dw2 = _matmul(f, dyb, name=f"{tag}_dw2", ta=True, alpha=0.5, tm=1408, tn=1024, tk=512)
    da, db = _ffn_dact(dyb, w2, a, b, name=f"{tag}_dact")
    dw1 = _matmul(h, da, name=f"{tag}_dw1", ta=True, tm=1024, tn=1408, tk=512)
    dw3 = _matmul(h, db, name=f"{tag}_dw3", ta=True, tm=1024, tn=1408, tk=512)
    dx, dxb, dnorm = _matmul(da, w1, name=f"{tag}_dh", tb=True, tm=512, tn=1024, tk=1408, pair2=(db, w3),
                             epilogue=(_epi_rms_bwd, [x, dy], [norm], [F32, BF16], 1))
    return dx, dxb, dnorm, dw1, dw3, dw2


def _local_step(x, mem, w, loss_target):
    s, d = x.shape
    assert s % (QB * 16) == 0
    rope = _rope_tables(s)

    h1 = _rms_fwd(x, w['ffn1_norm'], name="ffn1_rms")
    (x1, h), sv1 = _ffn_fwd(x, h1, w['ffn1_w1'], w['ffn1_w3'], w['ffn1_w2'], "ffn1",
                            (_epi_residual_rms, [x], [w['mix_norm']], [F32, BF16], 0))
    p = _matmul(h, w['w_in'], name="in_proj", out_dtype=BF16, tn=1024)
    logits = _matmul(h, w['w_gate'], name="gate_proj", tn=1024)
    c_qb, c_kb, c_vb, c_qc = 3 * SB_W, 3 * SB_W + DSA_W, 3 * SB_W + 2 * DSA_W, 3 * SB_W + 3 * DSA_W

    oa_t = _sb2_fwd(p, name="sb_fwd")
    ya = _matmul(oa_t, w['w_branch_sb'], name="sb_out")

    qb_n = _qknorm_fwd(p, c_qb, DSA_W, w['qn_dsa'], rope, name="dsa_qnorm", out_dtype=F32)
    kb_n = _qknorm_fwd(p, c_kb, DSA_W, w['kn_dsa'], rope, name="dsa_knorm", out_dtype=F32)
    vb32 = p[:, c_vb:c_vb + DSA_W].astype(F32)
    groups = range(len(DSA_GROUPS))
    ob_t, lse_b = _dsa2_combine([_dsa2_fwd(qb_n, kb_n, vb32, gi, name=f"dsa_fwd{gi}") for gi in groups], name="dsa_combine")
    yb = _matmul(ob_t, w['w_branch_dsa'], name="dsa_out")

    memh = _rms_fwd(mem, w['mem_norm'], name="mem_rms")
    kv = _matmul(memh, w['w_mem_kv'], name="mem_kv", out_dtype=BF16)
    km_n = _qknorm_fwd(kv, 0, MEM_W, w['kn_mem'], None, name="mem_knorm")
    qc_n = _qknorm_fwd(p, c_qc, MEM_W, w['qn_mem'], None, name="mem_qnorm")
    oc_t = _mem2_fwd(qc_n, km_n, kv, name="mem_fwd")
    yc = _matmul(oc_t, w['w_branch_mem'], name="mem_out")

    merged = _merge_fwd(logits, w['b_gate'], ya, yb, yc, name="merge")
    x2, h2 = _matmul(merged, w['w_out'], name="out_proj", tn=1024,
                     epilogue=(_epi_residual_rms, [x1], [w['ffn2_norm']], [F32, BF16], 0))
    (dx3, dx3b, sq), sv2 = _ffn_fwd(x2, h2, w['ffn2_w1'], w['ffn2_w3'], w['ffn2_w2'], "ffn2",
                                    (_epi_loss, [x2, loss_target], [], [F32, BF16], 1))
    loss = jnp.sum(sq) * (0.5 / d)

    g = {}
    dx2, dx2b, g['ffn2_norm'], g['ffn2_w1'], g['ffn2_w3'], g['ffn2_w2'] = _ffn_bwd(
        x2, w['ffn2_norm'], w['ffn2_w1'], w['ffn2_w3'], w['ffn2_w2'], sv2, dx3, dx3b, "ffn2")

    g['w_out'] = _matmul(merged, dx2b, name="d_w_out", ta=True, tn=1024, tk=512)
    dm = _matmul(dx2b, w['w_out'], name="d_merged", tb=True, tn=1024)
    dya, dyb, dyc, dlog, g['b_gate'] = _merge_bwd(logits, w['b_gate'], ya, yb, yc, dm, name="d_merge")
    dlogits = jnp.concatenate(dlog, axis=1)

    g['w_branch_sb'] = _matmul(oa_t, dya, name="d_w_sb", ta=True, tn=1024, tk=512)
    g['w_branch_dsa'] = _matmul(ob_t, dyb, name="d_w_dsa", ta=True, tk=512)
    g['w_branch_mem'] = _matmul(oc_t, dyc, name="d_w_mem", ta=True, tk=512)
    doa = _matmul(dya, w['w_branch_sb'], name="d_oa", tb=True, out_dtype=BF16)
    dob = _matmul(dyb, w['w_branch_dsa'], name="d_ob", tb=True)
    doc = _matmul(dyc, w['w_branch_mem'], name="d_oc", tb=True, out_dtype=BF16)

    dqa, dka, dva = _sb2_bwd(p, oa_t, doa, name="sb_bwd")

    dd_b = _dsa2_prep(ob_t, dob, name="dsa_prep")
    dgrp = [_dsa2_bwd(qb_n, kb_n, vb32, dob, lse_b, dd_b, gi, name=f"dsa_bwd{gi}") for gi in groups]
    dvb = jnp.concatenate([t[2] for t in dgrp], axis=1).astype(BF16)
    dqb, g['qn_dsa'] = _qknorm_bwd(p, c_qb, DSA_W, w['qn_dsa'], rope, [t[0] for t in dgrp], name="d_dsa_qnorm")
    dkb, g['kn_dsa'] = _qknorm_bwd(p, c_kb, DSA_W, w['kn_dsa'], rope, [t[1] for t in dgrp], name="d_dsa_knorm")

    dqc_n, dkm_n, dvm = _mem2_bwd(qc_n, km_n, kv, doc, name="mem_bwd")
    dqc, g['qn_mem'] = _qknorm_bwd(p, c_qc, MEM_W, w['qn_mem'], None, dqc_n, name="d_mem_qnorm")
    dkm, g['kn_mem'] = _qknorm_bwd(kv, 0, MEM_W, w['kn_mem'], None, dkm_n, name="d_mem_knorm")
    dkv = jnp.concatenate([dkm, dvm.astype(BF16)], axis=1)
    g['w_mem_kv'] = _matmul(memh, dkv, name="d_w_mem_kv", ta=True)
    dmemh = _matmul(dkv, w['w_mem_kv'], name="d_memh", tb=True)
    _, _, g['mem_norm'] = _rms_bwd(mem, w['mem_norm'], dmemh, None, name="d_mem_rms")

    dp = jnp.concatenate([dqa.astype(BF16), dka.astype(BF16), dva.astype(BF16),
                          dqb, dkb, dvb, dqc], axis=1)
    g['w_in'] = _matmul(h, dp, name="d_w_in", ta=True, tn=2048, tk=512)
    g['w_gate'] = _matmul(h, dlogits, name="d_w_gate", ta=True, tn=1536, tk=512)
    dh = _matmul(dp, w['w_in'], name="d_h_in", tb=True, tn=1024)
    dx1, dx1b, g['mix_norm'] = _matmul(dlogits, w['w_gate'], name="d_h_gate", tb=True, tm=512, tn=1024,
                                       epilogue=(_epi_rms_bwd_sum, [dh, x1, dx2], [w['mix_norm']], [F32, BF16], 1))

    dx0, _, g['ffn1_norm'], g['ffn1_w1'], g['ffn1_w3'], g['ffn1_w2'] = _ffn_bwd(
        x, w['ffn1_norm'], w['ffn1_w1'], w['ffn1_w3'], w['ffn1_w2'], sv1, dx1, dx1b, "ffn1")
    return loss, dx0, g


def _pack_rows(d, names):
    return jnp.concatenate([d[n].reshape(-1, LANES) for n in names], axis=0)


def _unpack_rows(t, like, names):
    out, off = {}, 0
    for n in names:
        r = like[n].size // LANES
        out[n] = t[off:off + r].reshape(like[n].shape)
        off += r
    return out


def _unpack_gathered(t, local, names):
    out, off = {}, 0
    for n in names:
        r, c = local[n].shape
        rows = r * c // LANES
        blk = t[:, off:off + rows].reshape(N_DEV, r, c)
        out[n] = blk.reshape(N_DEV * r, c) if SHARD_AXIS[n] == 0 else blk.transpose(1, 0, 2).reshape(r, N_DEV * c)
        off += rows
    return out


def _pack_for_owners(g, local, names):
    parts = []
    for n in names:
        r, c = local[n].shape
        blk = g[n].reshape(N_DEV, r, c) if SHARD_AXIS[n] == 0 else g[n].reshape(r, N_DEV, c).transpose(1, 0, 2)
        parts.append(blk.reshape(N_DEV, r * c // LANES, LANES))
    return jnp.concatenate(parts, axis=1)


def _pack_small(d, names, extra_rows):
    parts = []
    for n in names:
        v = d[n].reshape(-1)
        pad = (-v.size) % LANES
        parts.append(jnp.concatenate([v, jnp.zeros((pad,), v.dtype)]).reshape(-1, LANES))
    t = jnp.concatenate(parts, axis=0)
    return jnp.concatenate([t, jnp.zeros((extra_rows, LANES), t.dtype)], axis=0)


def _unpack_small(t, like, names):
    out, off = {}, 0
    for n in names:
        size = like[n].size
        rows = -(-size // LANES)
        out[n] = t[off:off + rows].reshape(-1)[:size].reshape(like[n].shape)
        off += rows
    return out


def _exchange(src, per_peer, *, name):
    rows = src.shape[-2]

    def body(src_ref, out_ref, send_sems, recv_sems, local_sem):
        x, y, c = lax.axis_index("x"), lax.axis_index("y"), lax.axis_index("c")
        me = 4 * x + 2 * y + c
        mine = pltpu.make_async_copy(src_ref.at[me] if per_peer else src_ref, out_ref.at[me], local_sem)
        mine.start()
        copies = []
        for k in range(1, N_DEV):
            px = 1 - x if k & 4 else x
            py = 1 - y if k & 2 else y
            pc = 1 - c if k & 1 else c
            cp = pltpu.make_async_remote_copy(
                src_ref=src_ref.at[4 * px + 2 * py + pc] if per_peer else src_ref, dst_ref=out_ref.at[me],
                send_sem=send_sems.at[k - 1], recv_sem=recv_sems.at[k - 1],
                device_id=(px, py, pc), device_id_type=pl.DeviceIdType.MESH)
            cp.start()
            copies.append(cp)
        for cp in copies:
            cp.wait_recv()
        for cp in copies:
            cp.wait_send()
        mine.wait()

    anyspace = pl.BlockSpec(memory_space=pl.ANY)
    return _pcall(body, name=name, in_specs=[anyspace], out_specs=anyspace,
                  out_shape=jax.ShapeDtypeStruct((N_DEV, rows, LANES), src.dtype),
                  scratch_shapes=[pltpu.SemaphoreType.DMA((N_DEV - 1,)), pltpu.SemaphoreType.DMA((N_DEV - 1,)),
                                  pltpu.SemaphoreType.DMA])(src)


def _gather_two_level(src, *, name):
    rows = src.shape[0]

    def body(src_ref, out_ref, send_sems, recv_sems, local_sem):
        x, y, c = lax.axis_index("x"), lax.axis_index("y"), lax.axis_index("c")
        me, sibling = (x, y, c), (x, y, 1 - c)
        chips = [(1 - x, y), (x, 1 - y), (1 - x, 1 - y)]

        def slab(px, py, pc):
            return out_ref.at[4 * px + 2 * py + pc]

        def copy(k, block, to, from_src=False):
            return pltpu.make_async_remote_copy(
                src_ref=src_ref if from_src else slab(*block), dst_ref=slab(*block),
                send_sem=send_sems.at[k], recv_sem=recv_sems.at[k], device_id=to, device_id_type=pl.DeviceIdType.MESH)

        mine = pltpu.make_async_copy(src_ref, slab(*me), local_sem)
        mine.start()
        first = [copy(0, me, sibling, True)] + [copy(1 + j, me, (*chip, c), True) for j, chip in enumerate(chips)]
        for cp in first:
            cp.start()
        passed = [copy(4 + j, (*chip, c), sibling) for j, chip in enumerate(chips)]
        for j, chip in enumerate(chips):
            copy(1 + j, (*chip, c), me).wait_recv()
            passed[j].start()
        copy(0, sibling, me).wait_recv()
        for j, chip in enumerate(chips):
            copy(4 + j, (*chip, 1 - c), me).wait_recv()
        for cp in first + passed:
            cp.wait_send()
        mine.wait()

    anyspace = pl.BlockSpec(memory_space=pl.ANY)
    return _pcall(body, name=name, in_specs=[anyspace], out_specs=anyspace,
                  out_shape=jax.ShapeDtypeStruct((N_DEV, rows, LANES), src.dtype),
                  scratch_shapes=[pltpu.SemaphoreType.DMA((N_DEV - 1,)), pltpu.SemaphoreType.DMA((N_DEV - 1,)),
                                  pltpu.SemaphoreType.DMA])(src)


def _adamw(recv, w, m, v, *, name):
    rows = w.shape[0]
    tr = _pick(rows, (512, 256, 128, 64))

    def kern(r_ref, w_ref, m_ref, v_ref, g_ref, d_ref, mo_ref, vo_ref):
        g = r_ref[0].astype(F32)
        for p in range(1, N_DEV):
            g = g + r_ref[p].astype(F32)
        mn = ADAM_B1 * m_ref[...] + (1.0 - ADAM_B1) * g
        vn = ADAM_B2 * v_ref[...] + (1.0 - ADAM_B2) * (g * g)
        m_hat = mn / (1.0 - ADAM_B1 ** ADAM_STEP)
        v_hat = vn / (1.0 - ADAM_B2 ** ADAM_STEP)
        g_ref[...] = g
        d_ref[...] = -ADAM_LR * (m_hat / (jnp.sqrt(v_hat) + ADAM_EPS) + ADAM_WD * w_ref[...])
        mo_ref[...] = mn
        vo_ref[...] = vn

    row = pl.BlockSpec((tr, LANES), lambda i: (i, 0))
    shp = jax.ShapeDtypeStruct((rows, LANES), F32)
    return _pcall(kern, name=name, grid=(rows // tr,), in_specs=[pl.BlockSpec((N_DEV, tr, LANES), lambda i: (0, i, 0)), row, row, row],
                  out_specs=[row, row, row, row], out_shape=[shp, shp, shp, shp], compiler_params=_params("parallel"))(recv, w, m, v)


INPUTS = ['x', 'mem'] + WEIGHTS + ['loss_target'] + ['m_' + n for n in WEIGHTS] + ['v_' + n for n in WEIGHTS]
SMALL_PAD_ROWS = 4


def kernel(x, mem, ffn1_norm, ffn1_w1, ffn1_w3, ffn1_w2, mix_norm, mem_norm, w_in, w_mem_kv, qn_dsa, kn_dsa, qn_mem, kn_mem, w_branch_sb, w_branch_dsa, w_branch_mem, w_gate, b_gate, w_out, ffn2_norm, ffn2_w1, ffn2_w3, ffn2_w2, loss_target, m_ffn1_norm, m_ffn1_w1, m_ffn1_w3, m_ffn1_w2, m_mix_norm, m_mem_norm, m_w_in, m_w_mem_kv, m_qn_dsa, m_kn_dsa, m_qn_mem, m_kn_mem, m_w_branch_sb, m_w_branch_dsa, m_w_branch_mem, m_w_gate, m_b_gate, m_w_out, m_ffn2_norm, m_ffn2_w1, m_ffn2_w3, m_ffn2_w2, v_ffn1_norm, v_ffn1_w1, v_ffn1_w3, v_ffn1_w2, v_mix_norm, v_mem_norm, v_w_in, v_w_mem_kv, v_qn_dsa, v_kn_dsa, v_qn_mem, v_kn_mem, v_w_branch_sb, v_w_branch_dsa, v_w_branch_mem, v_w_gate, v_b_gate, v_w_out, v_ffn2_norm, v_ffn2_w1, v_ffn2_w3, v_ffn2_w2):
    given = dict(zip(INPUTS, (x, mem, ffn1_norm, ffn1_w1, ffn1_w3, ffn1_w2, mix_norm, mem_norm, w_in, w_mem_kv, qn_dsa, kn_dsa, qn_mem, kn_mem, w_branch_sb, w_branch_dsa, w_branch_mem, w_gate, b_gate, w_out, ffn2_norm, ffn2_w1, ffn2_w3, ffn2_w2, loss_target, m_ffn1_norm, m_ffn1_w1, m_ffn1_w3, m_ffn1_w2, m_mix_norm, m_mem_norm, m_w_in, m_w_mem_kv, m_qn_dsa, m_kn_dsa, m_qn_mem, m_kn_mem, m_w_branch_sb, m_w_branch_dsa, m_w_branch_mem, m_w_gate, m_b_gate, m_w_out, m_ffn2_norm, m_ffn2_w1, m_ffn2_w3, m_ffn2_w2, v_ffn1_norm, v_ffn1_w1, v_ffn1_w3, v_ffn1_w2, v_mix_norm, v_mem_norm, v_w_in, v_w_mem_kv, v_qn_dsa, v_kn_dsa, v_qn_mem, v_kn_mem, v_w_branch_sb, v_w_branch_dsa, v_w_branch_mem, v_w_gate, v_b_gate, v_w_out, v_ffn2_norm, v_ffn2_w1, v_ffn2_w3, v_ffn2_w2), strict=True))
    wl = {n: given[n][0] for n in BIG}
    ws = {n: given[n] for n in SMALL}

    gathered = _gather_two_level(_pack_rows({n: wl[n].astype(BF16) for n in BIG}, BIG), name="gather_weights")
    whole = _unpack_gathered(gathered, wl, BIG)
    loss, dx, g = _local_step(x[0], mem[0], {**whole, **ws}, loss_target[0])

    recv = _exchange(_pack_for_owners(g, wl, BIG).astype(BF16), True, name="scatter_grads")
    big = _adamw(recv, _pack_rows(wl, BIG), _pack_rows({n: given['m_' + n][0] for n in BIG}, BIG),
                 _pack_rows({n: given['v_' + n][0] for n in BIG}, BIG), name="adamw_sharded")
    big = [_unpack_rows(t, wl, BIG) for t in big]

    gs = _pack_small(g, SMALL, SMALL_PAD_ROWS)
    loss_row = gs.shape[0] - SMALL_PAD_ROWS
    gs = gs.at[loss_row, 0].set(loss)
    recv_s = _exchange(gs, False, name="gather_small")
    small = _adamw(recv_s, _pack_small(ws, SMALL, SMALL_PAD_ROWS), _pack_small({n: given['m_' + n] for n in SMALL}, SMALL, SMALL_PAD_ROWS),
                   _pack_small({n: given['v_' + n] for n in SMALL}, SMALL, SMALL_PAD_ROWS), name="adamw_replicated")
    total_loss = small[0][loss_row, 0]
    small = [_unpack_small(t, ws, SMALL) for t in small]

    outs = [total_loss, dx[None]]
    for kind in range(4):
        outs += [big[kind][n][None] if n in wl else small[kind][n] for n in WEIGHTS]
    return tuple(outs)
```

```python
import functools
import math

import jax
import jax.numpy as jnp
from jax import lax
from jax.experimental import pallas as pl
from jax.experimental.pallas import tpu as pltpu

F32 = jnp.float32
BF16 = jnp.bfloat16
MXU_DT = jnp.bfloat16

N_DEV = 8
HEAD_DIM = 64
SB_HEADS = 8
DSA_GROUPS = ((128, 1), (512, 4), (2048, 16))
DSA_HPG = 4
MEM_HEADS = 4
SB_W = SB_HEADS * HEAD_DIM
DSA_W = DSA_HPG * len(DSA_GROUPS) * HEAD_DIM
DSA_OUT_W = DSA_HPG * HEAD_DIM
MEM_W = MEM_HEADS * HEAD_DIM
ROPE_THETA = 10000.0
NORM_EPS = 1e-6
QB = 128
SCALE = HEAD_DIM ** -0.5
ADAM_LR, ADAM_B1, ADAM_B2, ADAM_EPS, ADAM_WD, ADAM_STEP = 0.001, 0.9, 0.999, 1e-08, 0.01, 10

LANES = 128
VMEM_LIMIT = 48 * 1024 * 1024
SB_DEAD = -110.0 * 1.4426950408889634

WEIGHTS = ['ffn1_norm', 'ffn1_w1', 'ffn1_w3', 'ffn1_w2', 'mix_norm', 'mem_norm', 'w_in', 'w_mem_kv', 'qn_dsa', 'kn_dsa',
           'qn_mem', 'kn_mem', 'w_branch_sb', 'w_branch_dsa', 'w_branch_mem', 'w_gate', 'b_gate', 'w_out', 'ffn2_norm',
           'ffn2_w1', 'ffn2_w3', 'ffn2_w2']
SHARD_AXIS = {'ffn1_norm': None, 'ffn1_w1': 1, 'ffn1_w3': 1, 'ffn1_w2': 0, 'mix_norm': None, 'mem_norm': None, 'w_in': 1,
              'w_mem_kv': 0, 'qn_dsa': None, 'kn_dsa': None, 'qn_mem': None, 'kn_mem': None, 'w_branch_sb': 1,
              'w_branch_dsa': 1, 'w_branch_mem': 1, 'w_gate': 1, 'b_gate': None, 'w_out': 0, 'ffn2_norm': None,
              'ffn2_w1': 1, 'ffn2_w3': 1, 'ffn2_w2': 0}
BIG = [n for n in WEIGHTS if SHARD_AXIS[n] is not None]
SMALL = [n for n in WEIGHTS if SHARD_AXIS[n] is None]


def _pcall(kern, **kw):
    return pl.pallas_call(kern, **kw)


def _params(*sem):
    return pltpu.CompilerParams(dimension_semantics=sem, vmem_limit_bytes=VMEM_LIMIT)


def _dot(a, b, dims):
    return lax.dot_general(a.astype(MXU_DT), b.astype(MXU_DT), (dims, ((), ())), preferred_element_type=F32)


def _nn(a, b):
    return _dot(a, b, ((1,), (0,)))


def _nt(a, b):
    return _dot(a, b, ((1,), (1,)))


def _tn(a, b):
    return _dot(a, b, ((0,), (0,)))


def _pick(n, prefs):
    for p in prefs:
        if n % p == 0:
            return p
    return n


def _matmul(a, b, *, name, ta=False, tb=False, out_dtype=F32, res=None, alpha=1.0, tm=1024, tn=512, tk=1024, pair2=None,
            epilogue=None):
    if ta:
        kdim, m = a.shape
    else:
        m, kdim = a.shape
    n = b.shape[0] if tb else b.shape[1]
    tm = _pick(m, (tm, 512, 256, 128))
    tn = _pick(n, (tn, 512, 384, 256, 128))
    tk = _pick(kdim, (tk, 1024, 512, 256, 128))
    nk = kdim // tk
    a_spec = pl.BlockSpec((tk, tm), lambda i, j, k: (k, i)) if ta else pl.BlockSpec((tm, tk), lambda i, j, k: (i, k))
    b_spec = pl.BlockSpec((tn, tk), lambda i, j, k: (j, k)) if tb else pl.BlockSpec((tk, tn), lambda i, j, k: (k, j))
    o_spec = pl.BlockSpec((tm, tn), lambda i, j, k: (i, j))
    v_spec = pl.BlockSpec((1, tn), lambda i, j, k: (0, j))
    dims = ((0 if ta else 1,), (1 if tb else 0,))
    n_mm = 2 if pair2 is None else 4
    if epilogue is None:
        row_ins, vec_ins = ([] if res is None else [res]), []
        out_dtypes, n_vec = [out_dtype], 0
    else:
        assert tn == n and res is None
        epi_fn, row_ins, vec_ins, out_dtypes, n_vec = epilogue
    n_row_out = len(out_dtypes)

    def kern(*refs):
        refs = list(refs)
        acc_ref = refs.pop() if nk > 1 else None
        mm = refs[:n_mm]
        extra = refs[n_mm:n_mm + len(row_ins) + len(vec_ins)]
        outs = refs[n_mm + len(extra):]
        i = pl.program_id(0)
        k = pl.program_id(2)

        def product():
            part = _dot(mm[0][...], mm[1][...], dims)
            if pair2 is not None:
                part = part + _dot(mm[2][...], mm[3][...], dims)
            return part

        def finish(r):
            if alpha != 1.0:
                r = r * alpha
            if epilogue is None:
                if extra:
                    r = extra[0][...] + r
                outs[0][...] = r.astype(out_dtype)
                return
            vals = epi_fn(r, *[e[...] for e in extra])
            for o_ref, v in zip(outs[:n_row_out], vals[:n_row_out]):
                o_ref[...] = v.astype(o_ref.dtype)
            for o_ref, v in zip(outs[n_row_out:], vals[n_row_out:]):
                @pl.when(i == 0)
                def _():
                    o_ref[...] = jnp.zeros_like(o_ref)

                o_ref[...] += v

        if nk == 1:
            finish(product())
            return

        @pl.when(k == 0)
        def _():
            acc_ref[...] = jnp.zeros_like(acc_ref)

        acc_ref[...] += product()

        @pl.when(k == nk - 1)
        def _():
            finish(acc_ref[...])

    ins = [a, b] + ([] if pair2 is None else list(pair2)) + list(row_ins) + list(vec_ins)
    specs = [a_spec, b_spec] * (n_mm // 2) + [o_spec] * len(row_ins) + [v_spec] * len(vec_ins)
    out_specs = [o_spec] * n_row_out + [v_spec] * n_vec
    out_shape = [jax.ShapeDtypeStruct((m, n), dt) for dt in out_dtypes] + [jax.ShapeDtypeStruct((1, n), F32)] * n_vec
    outs = _pcall(kern, name=name, grid=(m // tm, n // tn, nk), in_specs=specs, out_specs=out_specs, out_shape=out_shape,
                  scratch_shapes=[pltpu.VMEM((tm, tn), F32)] if nk > 1 else [],
                  compiler_params=_params("arbitrary" if n_vec else "parallel", "parallel", "arbitrary"))(*ins)
    return outs[0] if epilogue is None else outs


def _epi_residual_rms(r, res, gain):
    xn = res + r
    return xn, xn * lax.rsqrt(jnp.mean(xn * xn, axis=-1, keepdims=True) + NORM_EPS) * gain


def _epi_rms_bwd(r, x, dres, gain):
    rs = lax.rsqrt(jnp.mean(x * x, axis=-1, keepdims=True) + NORM_EPS)
    xh = x * rs
    dy = r * gain
    dx = dres + rs * (dy - xh * jnp.mean(dy * xh, axis=-1, keepdims=True))
    return dx, dx, jnp.sum(r * xh, axis=0, keepdims=True)


def _epi_rms_bwd_sum(r, r0, x, dres, gain):
    return _epi_rms_bwd(r + r0, x, dres, gain)


def _epi_loss(r, res, target):
    e = (res + r) - target
    dy = e / e.shape[-1]
    return dy, dy, jnp.sum(e * e, axis=0, keepdims=True)
def _rms_fwd(x, g, *, name):
    s, d = x.shape
    ts = _pick(s, (512, 256))

    def kern(x_ref, g_ref, h_ref):
        xf = x_ref[...]
        r = lax.rsqrt(jnp.mean(xf * xf, axis=-1, keepdims=True) + NORM_EPS)
        h_ref[...] = (xf * r * g_ref[...]).astype(h_ref.dtype)

    return _pcall(kern, name=name, grid=(s // ts,),
                  in_specs=[pl.BlockSpec((ts, d), lambda i: (i, 0)), pl.BlockSpec((1, d), lambda i: (0, 0))],
                  out_specs=pl.BlockSpec((ts, d), lambda i: (i, 0)), out_shape=jax.ShapeDtypeStruct((s, d), BF16),
                  compiler_params=_params("parallel"))(x, g)


def _rms_bwd(x, g, dh, res, *, name):
    s, d = x.shape
    ts = _pick(s, (512, 256))

    def kern(*refs):
        if res is None:
            x_ref, g_ref, dh_ref, dx_ref, dxb_ref, dg_ref = refs
            r_ref = None
        else:
            x_ref, g_ref, dh_ref, r_ref, dx_ref, dxb_ref, dg_ref = refs
        xf = x_ref[...]
        r = lax.rsqrt(jnp.mean(xf * xf, axis=-1, keepdims=True) + NORM_EPS)
        xh = xf * r
        dhf = dh_ref[...].astype(F32)
        dy = dhf * g_ref[...]
        dx = r * (dy - xh * jnp.mean(dy * xh, axis=-1, keepdims=True))
        if r_ref is not None:
            dx = r_ref[...] + dx
        dx_ref[...] = dx
        dxb_ref[...] = dx.astype(dxb_ref.dtype)

        @pl.when(pl.program_id(0) == 0)
        def _():
            dg_ref[...] = jnp.zeros_like(dg_ref)

        dg_ref[...] += jnp.sum(dhf * xh, axis=0, keepdims=True)

    row = pl.BlockSpec((ts, d), lambda i: (i, 0))
    vec = pl.BlockSpec((1, d), lambda i: (0, 0))
    ins = [x, g, dh] + ([] if res is None else [res])
    return _pcall(kern, name=name, grid=(s // ts,), in_specs=[row, vec, row] + ([] if res is None else [row]),
                  out_specs=[row, row, vec],
                  out_shape=[jax.ShapeDtypeStruct((s, d), F32), jax.ShapeDtypeStruct((s, d), BF16), jax.ShapeDtypeStruct((1, d), F32)],
                  compiler_params=_params("arbitrary"))(*ins)


def _sigmoid(x):
    return 1.0 / (1.0 + jnp.exp(-x))


FFN_TM, FFN_TF = 512, 1408


def _ffn_up(h, w1, w3, *, name, side=None):
    s, d = h.shape
    fdim = w1.shape[1]
    tm, tf = _pick(s, (FFN_TM, 256)), _pick(fdim, (FFN_TF, 512, 256, 128))

    def kern(h_ref, w1_ref, w3_ref, a_ref, b_ref, f_ref):
        hb = h_ref[...]
        a = _nn(hb, w1_ref[...])
        b = _nn(hb, w3_ref[...])
        a_ref[...] = a.astype(a_ref.dtype)
        b_ref[...] = b.astype(b_ref.dtype)
        f_ref[...] = (a * _sigmoid(a) * b).astype(f_ref.dtype)

    wspec = pl.BlockSpec((d, tf), lambda i, j: (0, j))
    ospec = pl.BlockSpec((tm, tf), lambda i, j: (i, j))
    shp = jax.ShapeDtypeStruct((s, fdim), BF16)
    return _call_2d(kern, name=name, grid=(s // tm, fdim // tf), in_specs=[pl.BlockSpec((tm, d), lambda i, j: (i, 0)), wspec, wspec],
                    out_specs=[ospec, ospec, ospec], out_shape=[shp, shp, shp], ins=[h, w1, w3],
                    semantics=("parallel", "parallel"), side=side)


def _ffn_dact(dy, w2, a, b, *, name, side=None):
    s, d = dy.shape
    fdim = w2.shape[0]
    tm, tf = _pick(s, (FFN_TM, 256)), _pick(fdim, (FFN_TF, 512, 256, 128))

    def kern(dy_ref, w2_ref, a_ref, b_ref, da_ref, db_ref):
        df = _nt(dy_ref[...], w2_ref[...]) * 0.5
        av = a_ref[...].astype(F32)
        sg = _sigmoid(av)
        da_ref[...] = (df * b_ref[...].astype(F32) * (sg + av * sg * (1.0 - sg))).astype(da_ref.dtype)
        db_ref[...] = (df * (av * sg)).astype(db_ref.dtype)

    ospec = pl.BlockSpec((tm, tf), lambda i, j: (i, j))
    shp = jax.ShapeDtypeStruct((s, fdim), BF16)
    return _call_2d(kern, name=name, grid=(s // tm, fdim // tf),
                    in_specs=[pl.BlockSpec((tm, d), lambda i, j: (i, 0)), pl.BlockSpec((tf, d), lambda i, j: (j, 0)), ospec, ospec],
                    out_specs=[ospec, ospec], out_shape=[shp, shp], ins=[dy, w2, a, b], semantics=("parallel", "parallel"), side=side)


def _loss_head(y, t, *, name):
    s, d = y.shape
    ts = _pick(s, (512, 256))
    n = s // ts

    def kern(y_ref, t_ref, dy_ref, dyb_ref, l_ref, acc_ref):
        i = pl.program_id(0)

        @pl.when(i == 0)
        def _():
            acc_ref[...] = jnp.zeros_like(acc_ref)

        e = y_ref[...] - t_ref[...]
        dy_ref[...] = e / d
        dyb_ref[...] = (e / d).astype(dyb_ref.dtype)
        acc_ref[...] += jnp.sum(e * e, axis=0, keepdims=True)

        @pl.when(i == n - 1)
        def _():
            l_ref[...] = jnp.sum(acc_ref[...], axis=1, keepdims=True) * (0.5 / d)

    row = pl.BlockSpec((ts, d), lambda i: (i, 0))
    return _pcall(kern, name=name, grid=(n,), in_specs=[row, row], out_specs=[row, row, pl.BlockSpec((1, 1), lambda i: (0, 0))],
                  out_shape=[jax.ShapeDtypeStruct((s, d), F32), jax.ShapeDtypeStruct((s, d), BF16), jax.ShapeDtypeStruct((1, 1), F32)],
                  scratch_shapes=[pltpu.VMEM((1, d), F32)], compiler_params=_params("arbitrary"))(y, t)


def _head_mean(v, bd):
    hi = v.astype(BF16)
    lo = (v - hi.astype(F32)).astype(BF16)
    return (lax.dot_general(hi, bd, (((1,), (0,)), ((), ())), preferred_element_type=F32)
            + lax.dot_general(lo, bd, (((1,), (0,)), ((), ())), preferred_element_type=F32))


def _partner(v):
    w = v.shape[1]
    lane = lax.broadcasted_iota(jnp.int32, v.shape, 1)
    return jnp.where(lane % HEAD_DIM < HEAD_DIM // 2, pltpu.roll(v, w - HEAD_DIM // 2, 1), pltpu.roll(v, HEAD_DIM // 2, 1))


def _block_diag(w):
    r = lax.broadcasted_iota(jnp.int32, (w, w), 0) // HEAD_DIM
    c = lax.broadcasted_iota(jnp.int32, (w, w), 1) // HEAD_DIM
    return jnp.where(r == c, 1.0 / HEAD_DIM, 0.0).astype(BF16)


def _rope_tables(s):
    half = HEAD_DIM // 2
    inv_freq = jnp.power(ROPE_THETA, -jnp.arange(half, dtype=F32) / half)
    ang = jnp.arange(s).astype(F32)[:, None] * inv_freq[None, :]
    cos, sin = jnp.cos(ang), jnp.sin(ang)
    cos2 = jnp.concatenate([cos, cos, cos, cos], axis=1)
    sin2 = jnp.concatenate([-sin, sin, -sin, sin], axis=1)
    return cos2, sin2


def _qknorm_fwd(src, col0, width, gain, rope, *, name, out_dtype=BF16):
    s = src.shape[0]
    ts = _pick(s, (512, 256))
    cb = col0 // width
    assert col0 % width == 0
    reps = width // LANES
    g = jnp.tile(gain, (1, width // HEAD_DIM))

    def kern(*refs):
        if rope is None:
            x_ref, g_ref, o_ref = refs
        else:
            x_ref, g_ref, c_ref, s_ref, o_ref = refs
        x = x_ref[...].astype(F32)
        bd = _block_diag(width)
        r = lax.rsqrt(_head_mean(x * x, bd) + NORM_EPS)
        y = x * r * g_ref[...]
        if rope is not None:
            y = y * jnp.tile(c_ref[...], (1, reps)) + _partner(y) * jnp.tile(s_ref[...], (1, reps))
        o_ref[...] = y.astype(o_ref.dtype)

    xs = pl.BlockSpec((ts, width), lambda i: (i, cb))
    tab = pl.BlockSpec((ts, LANES), lambda i: (i, 0))
    ins = [src, g] + ([] if rope is None else list(rope))
    specs = [xs, pl.BlockSpec((1, width), lambda i: (0, 0))] + ([] if rope is None else [tab, tab])
    return _pcall(kern, name=name, grid=(s // ts,), in_specs=specs, out_specs=pl.BlockSpec((ts, width), lambda i: (i, 0)),
                  out_shape=jax.ShapeDtypeStruct((s, width), out_dtype), compiler_params=_params("parallel"))(*ins)


def _qknorm_bwd(src, col0, width, gain, rope, dout, *, name):
    s = src.shape[0]
    ts = _pick(s, (512, 256))
    cb = col0 // width
    reps = width // LANES
    g = jnp.tile(gain, (1, width // HEAD_DIM))

    douts = list(dout) if isinstance(dout, (list, tuple)) else [dout]
    piece = width // len(douts)

    def kern(*refs):
        refs = list(refs)
        dg_ref = refs.pop()
        dx_ref = refs.pop()
        do_refs = [refs.pop() for _ in douts][::-1]
        if rope is None:
            x_ref, g_ref = refs
        else:
            x_ref, g_ref, c_ref, s_ref = refs
        x = x_ref[...].astype(F32)
        bd = _block_diag(width)
        r = lax.rsqrt(_head_mean(x * x, bd) + NORM_EPS)
        xh = x * r
        dy = jnp.concatenate([d[...].astype(F32) for d in do_refs], axis=1) if len(do_refs) > 1 else do_refs[0][...].astype(F32)
        if rope is not None:
            dy = dy * jnp.tile(c_ref[...], (1, reps)) + _partner(dy * jnp.tile(s_ref[...], (1, reps)))
        dxh = dy * g_ref[...]
        dx_ref[...] = (r * (dxh - xh * _head_mean(dxh * xh, bd))).astype(dx_ref.dtype)

        @pl.when(pl.program_id(0) == 0)
        def _():
            dg_ref[...] = jnp.zeros_like(dg_ref)

        dg_ref[...] += jnp.sum(dy * xh, axis=0, keepdims=True)

    xs = pl.BlockSpec((ts, width), lambda i: (i, cb))
    row = pl.BlockSpec((ts, width), lambda i: (i, 0))
    vec = pl.BlockSpec((1, width), lambda i: (0, 0))
    tab = pl.BlockSpec((ts, LANES), lambda i: (i, 0))
    ins = [src, g] + ([] if rope is None else list(rope)) + douts
    specs = [xs, vec] + ([] if rope is None else [tab, tab]) + [pl.BlockSpec((ts, piece), lambda i: (i, 0))] * len(douts)
    dx, dg = _pcall(kern, name=name, grid=(s // ts,), in_specs=specs, out_specs=[row, vec],
                    out_shape=[jax.ShapeDtypeStruct((s, width), BF16), jax.ShapeDtypeStruct((1, width), F32)],
                    compiler_params=_params("arbitrary"))(*ins)
    return dx, jnp.sum(dg.reshape(width // HEAD_DIM, HEAD_DIM), axis=0, keepdims=True)


def _tri(strict):
    r = lax.broadcasted_iota(jnp.int32, (2 * QB, QB), 0) % QB
    c = lax.broadcasted_iota(jnp.int32, (2 * QB, QB), 1)
    return jnp.where((r > c) if strict else (r >= c), 1.0, 0.0).astype(BF16)


def _split_dot(v, t2):
    hi = v.astype(BF16)
    lo = (v - hi.astype(F32)).astype(BF16)
    return lax.dot_general(jnp.concatenate([hi, lo], axis=1), t2, (((1,), (0,)), ((), ())), preferred_element_type=F32)


LOG2E = 1.4426950408889634


def _log2_sigmoids(z2):
    lf = -(jnp.maximum(z2, 0.0) + jnp.log2(1.0 + jnp.exp2(-jnp.abs(z2))))
    return z2 + lf, lf


def _key_blocks(t):
    s = t.shape[0]
    n = t.shape[1] // HEAD_DIM
    return t.reshape(s // QB, QB, n, HEAD_DIM).transpose(2, 0, 3, 1)


def _from_key_blocks(t):
    n, nb, hd, qb = t.shape
    return t.transpose(1, 3, 0, 2).reshape(nb * qb, n * hd)


SB_SUB = 4
SB2_SUB = 2


def _first_half(shape):
    return lax.broadcasted_iota(jnp.int32, shape, 1) < HEAD_DIM


def _split_pair(t, first):
    zero = jnp.zeros_like(t)
    return [jnp.where(first, t, zero), jnp.where(first, zero, t)]


def _sb2_fwd(p, *, name):
    s = p.shape[0]
    rq = SB2_SUB * QB
    nq = s // rq
    npair = SB_W // LANES

    def kern(q_ref, k_ref, v_ref, o_ref):
        i = pl.program_id(1)
        first = _first_half((rq, LANES))
        qs = _split_pair(q_ref[...], first)
        t2 = _tri(True)
        rel = lax.broadcasted_iota(jnp.int32, (rq, QB), 1) - lax.broadcasted_iota(jnp.int32, (rq, QB), 0)

        def tile(j, carries, accs, masked):
            off = pl.multiple_of(j * QB, QB)
            kt = k_ref[pl.ds(off, QB), :]
            vt = v_ref[pl.ds(off, QB), :]
            out_c, out_a = [], []
            for e in range(2):
                ls, lf = _log2_sigmoids(_nt(qs[e], kt) * (SCALE * LOG2E))
                if masked:
                    before = rel < i * rq - j * QB
                    lf = jnp.where(before, lf, 0.0)
                w = jnp.exp2(ls + _split_dot(lf, t2) + carries[e])
                if masked:
                    w = jnp.where(before, w, 0.0)
                out_c.append(carries[e] + jnp.sum(lf, axis=1, keepdims=True))
                out_a.append(accs[e] + _nn(w, vt))
            return out_c, out_a

        carries = [jnp.zeros((rq, 1), F32)] * 2
        accs = [jnp.zeros((rq, LANES), F32)] * 2
        for a in range(SB2_SUB):
            carries, accs = tile(i * SB2_SUB + (SB2_SUB - 1 - a), carries, accs, True)

        def cond(st):
            return jnp.logical_and(st[0] >= 0, st[1] > 0)

        def body(st):
            carries, accs = tile(st[0], [st[2], st[3]], [st[4], st[5]], False)
            alive = jnp.maximum(jnp.max(carries[0]), jnp.max(carries[1])) > SB_DEAD
            return st[0] - 1, alive.astype(jnp.int32), carries[0], carries[1], accs[0], accs[1]

        st = lax.while_loop(cond, body, (i * SB2_SUB - 1, jnp.int32(1), carries[0], carries[1], accs[0], accs[1]))
        o_ref[...] = jnp.where(first, st[4], st[5])

    return _pcall(kern, name=name, grid=(npair, nq),
                  in_specs=[pl.BlockSpec((rq, LANES), lambda a, i: (i, a)), pl.BlockSpec((s, LANES), lambda a, i: (0, npair + a)),
                            pl.BlockSpec((s, LANES), lambda a, i: (0, 2 * npair + a))],
                  out_specs=pl.BlockSpec((rq, LANES), lambda a, i: (i, a)), out_shape=jax.ShapeDtypeStruct((s, SB_W), F32),
                  compiler_params=_params("parallel", "arbitrary"))(p, p, p)


def _sb2_bwd(p, o, do, *, name, side=None):
    s = p.shape[0]
    rq = SB2_SUB * QB
    nq = s // rq
    npair = SB_W // LANES

    def kern(q_ref, k_ref, v_ref, o_ref, do_ref, dq_ref, dk_hbm, dv_hbm, dk_acc, dv_acc, sem):
        pr = pl.program_id(0)
        i = pl.program_id(1)

        @pl.when(i == 0)
        def _():
            dk_acc[...] = jnp.zeros_like(dk_acc)
            dv_acc[...] = jnp.zeros_like(dv_acc)

        first = _first_half((rq, LANES))
        qs = _split_pair(q_ref[...], first)
        do2 = do_ref[...]
        dos = _split_pair(do2, first)
        prod = do2.astype(F32) * o_ref[...]
        dsums = [jnp.sum(jnp.where(first, prod, 0.0), axis=1, keepdims=True),
                 jnp.sum(jnp.where(first, 0.0, prod), axis=1, keepdims=True)]
        t_strict = _tri(True)
        t_incl = _tri(False)
        rel = lax.broadcasted_iota(jnp.int32, (rq, QB), 1) - lax.broadcasted_iota(jnp.int32, (rq, QB), 0)

        def tile(j, carries, gcarries, dqs, masked):
            off = pl.multiple_of(j * QB, QB)
            kt = k_ref[pl.ds(off, QB), :]
            vt = v_ref[pl.ds(off, QB), :]
            out_c, out_g, out_q = [], [], []
            dk_t = jnp.zeros((QB, LANES), F32)
            dv_t = jnp.zeros((QB, LANES), F32)
            for e in range(2):
                ls, lf = _log2_sigmoids(_nt(qs[e], kt) * (SCALE * LOG2E))
                if masked:
                    before = rel < i * rq - j * QB
                    lf = jnp.where(before, lf, 0.0)
                w = jnp.exp2(ls + _split_dot(lf, t_strict) + carries[e])
                if masked:
                    w = jnp.where(before, w, 0.0)
                wr = w.astype(MXU_DT)
                g = _nt(dos[e], vt) * wr.astype(F32)
                big_g = dsums[e] - (_split_dot(g, t_incl) + gcarries[e])
                sig = jnp.exp2(ls)
                dz = g * (1.0 - sig) - sig * big_g
                if masked:
                    dz = jnp.where(before, dz, 0.0)
                dz = dz * SCALE
                dk_t = dk_t + _tn(dz, qs[e])
                dv_t = dv_t + _tn(wr, dos[e])
                out_c.append(carries[e] + jnp.sum(lf, axis=1, keepdims=True))
                out_g.append(gcarries[e] + jnp.sum(g, axis=1, keepdims=True))
                out_q.append(dqs[e] + _nn(dz, kt))
            dk_acc[pl.ds(off, QB), :] += dk_t
            dv_acc[pl.ds(off, QB), :] += dv_t
            return out_c, out_g, out_q

        zc = [jnp.zeros((rq, 1), F32)] * 2
        carries, gcarries, dqs = zc, zc, [jnp.zeros((rq, LANES), F32)] * 2
        for a in range(SB2_SUB):
            carries, gcarries, dqs = tile(i * SB2_SUB + (SB2_SUB - 1 - a), carries, gcarries, dqs, True)

        def cond(st):
            return jnp.logical_and(st[0] >= 0, st[1] > 0)

        def body(st):
            carries, gcarries, dqs = tile(st[0], [st[2], st[3]], [st[4], st[5]], [st[6], st[7]], False)
            alive = jnp.maximum(jnp.max(carries[0]), jnp.max(carries[1])) > SB_DEAD
            return (st[0] - 1, alive.astype(jnp.int32), carries[0], carries[1], gcarries[0], gcarries[1], dqs[0], dqs[1])

        st = lax.while_loop(cond, body, (i * SB2_SUB - 1, jnp.int32(1), carries[0], carries[1], gcarries[0], gcarries[1],
                                         dqs[0], dqs[1]))
        dq_ref[...] = jnp.where(first, st[6], st[7])

        @pl.when(i == nq - 1)
        def _():
            cols = pl.ds(pl.multiple_of(pr * LANES, LANES), LANES)
            ck = pltpu.make_async_copy(dk_acc, dk_hbm.at[:, cols], sem.at[0])
            cv = pltpu.make_async_copy(dv_acc, dv_hbm.at[:, cols], sem.at[1])
            ck.start()
            cv.start()
            ck.wait()
            cv.wait()

    blk = pl.BlockSpec((rq, LANES), lambda a, i: (i, a))
    anyspace = pl.BlockSpec(memory_space=pl.ANY)
    shp = jax.ShapeDtypeStruct((s, SB_W), F32)
    return _call_2d(kern, name=name, grid=(npair, nq),
                    in_specs=[blk, pl.BlockSpec((s, LANES), lambda a, i: (0, npair + a)),
                              pl.BlockSpec((s, LANES), lambda a, i: (0, 2 * npair + a)), blk, blk],
                    out_specs=[blk, anyspace, anyspace], out_shape=[shp, shp, shp], ins=[p, p, p, o, do],
                    scratch_shapes=[pltpu.VMEM((s, LANES), F32), pltpu.VMEM((s, LANES), F32), pltpu.SemaphoreType.DMA((2,))],
                    semantics=("arbitrary", "arbitrary"), side=side)


def _sb_fwd(q, kt, vt, *, name):
    h, s, hd = q.shape
    rq = SB_SUB * QB
    nq = s // rq
    nb = s // QB

    def kern(q_ref, k_ref, v_ref, o_ref):
        i = pl.program_id(1)
        qb = q_ref[...]
        t2 = _tri(True)
        rel = lax.broadcasted_iota(jnp.int32, (rq, QB), 1) - lax.broadcasted_iota(jnp.int32, (rq, QB), 0)

        def tile(j, carry, acc, masked):
            ls, lf = _log2_sigmoids(_nn(qb, k_ref[j]) * (SCALE * LOG2E))
            if masked:
                before = rel < i * rq - j * QB
                lf = jnp.where(before, lf, 0.0)
            w = jnp.exp2(ls + _split_dot(lf, t2) + carry)
            if masked:
                w = jnp.where(before, w, 0.0)
            return carry + jnp.sum(lf, axis=1, keepdims=True), acc + _nt(w, v_ref[j])

        carry, acc = jnp.zeros((rq, 1), F32), jnp.zeros((rq, hd), F32)
        for a in range(SB_SUB):
            carry, acc = tile(i * SB_SUB + (SB_SUB - 1 - a), carry, acc, True)

        def cond(st):
            return jnp.logical_and(st[0] >= 0, st[1] > 0)

        def body(st):
            j, _, carry, acc = st
            carry, acc = tile(j, carry, acc, False)
            return j - 1, (jnp.max(carry) > SB_DEAD).astype(jnp.int32), carry, acc

        _, _, _, acc = lax.while_loop(cond, body, (i * SB_SUB - 1, jnp.int32(1), carry, acc))
        o_ref[...] = acc

    blk = pl.BlockSpec((None, rq, hd), lambda a, i: (a, i, 0))
    full = pl.BlockSpec((None, nb, hd, QB), lambda a, i: (a, 0, 0, 0))
    return _pcall(kern, name=name, grid=(h, nq), in_specs=[blk, full, full], out_specs=blk,
                  out_shape=jax.ShapeDtypeStruct((h, s, hd), F32), compiler_params=_params("parallel", "arbitrary"))(q, kt, vt)


def _sb_bwd(q, kt, vt, o, do, *, name):
    h, s, hd = q.shape
    rq = SB_SUB * QB
    nq = s // rq
    nb = s // QB

    def kern(q_ref, k_ref, v_ref, o_ref, do_ref, dq_ref, dk_ref, dv_ref):
        i = pl.program_id(1)

        @pl.when(i == 0)
        def _():
            dk_ref[...] = jnp.zeros_like(dk_ref)
            dv_ref[...] = jnp.zeros_like(dv_ref)

        qb = q_ref[...]
        dob = do_ref[...]
        dsum = jnp.sum(dob.astype(F32) * o_ref[...], axis=1, keepdims=True)
        t_strict = _tri(True)
        t_incl = _tri(False)
        rel = lax.broadcasted_iota(jnp.int32, (rq, QB), 1) - lax.broadcasted_iota(jnp.int32, (rq, QB), 0)

        def tile(j, carry, gcarry, dq, masked):
            kb = k_ref[j]
            ls, lf = _log2_sigmoids(_nn(qb, kb) * (SCALE * LOG2E))
            if masked:
                before = rel < i * rq - j * QB
                lf = jnp.where(before, lf, 0.0)
            w = jnp.exp2(ls + _split_dot(lf, t_strict) + carry)
            if masked:
                w = jnp.where(before, w, 0.0)
            wr = w.astype(MXU_DT)
            g = _nn(dob, v_ref[j]) * wr.astype(F32)
            big_g = dsum - (_split_dot(g, t_incl) + gcarry)
            sig = jnp.exp2(ls)
            dz = g * (1.0 - sig) - sig * big_g
            if masked:
                dz = jnp.where(before, dz, 0.0)
            dz = dz * SCALE
            dk_ref[j] += _tn(qb, dz)
            dv_ref[j] += _tn(dob, wr)
            return (carry + jnp.sum(lf, axis=1, keepdims=True), gcarry + jnp.sum(g, axis=1, keepdims=True),
                    dq + _nt(dz, kb))

        carry, gcarry, dq = jnp.zeros((rq, 1), F32), jnp.zeros((rq, 1), F32), jnp.zeros((rq, hd), F32)
        for a in range(SB_SUB):
            carry, gcarry, dq = tile(i * SB_SUB + (SB_SUB - 1 - a), carry, gcarry, dq, True)

        def cond(st):
            return jnp.logical_and(st[0] >= 0, st[1] > 0)

        def body(st):
            j, _, carry, gcarry, dq = st
            carry, gcarry, dq = tile(j, carry, gcarry, dq, False)
            return j - 1, (jnp.max(carry) > SB_DEAD).astype(jnp.int32), carry, gcarry, dq

        st = lax.while_loop(cond, body, (i * SB_SUB - 1, jnp.int32(1), carry, gcarry, dq))
        dq_ref[...] = st[4]

    blk = pl.BlockSpec((None, rq, hd), lambda a, i: (a, i, 0))
    full = pl.BlockSpec((None, nb, hd, QB), lambda a, i: (a, 0, 0, 0))
    kshape = jax.ShapeDtypeStruct((h, nb, hd, QB), F32)
    return _pcall(kern, name=name, grid=(h, nq), in_specs=[blk, full, full, blk, blk], out_specs=[blk, full, full],
                  out_shape=[jax.ShapeDtypeStruct((h, s, hd), F32), kshape, kshape],
                  compiler_params=_params("parallel", "arbitrary"))(q, kt, vt, o, do)


DSA_SUB = 4


def _dsa_seq_blocks(t, s):
    steps_per_group = 4 * s // (QB * DSA_SUB)
    g = t // steps_per_group
    b0, b1, b2 = (s // (QB * r) for _, r in DSA_GROUPS)
    return jnp.where(g == 0, b0, jnp.where(g == 1, b1, b2))


def _dsa_rel():
    qi = lax.broadcasted_iota(jnp.int32, (QB, QB), 0)
    kj = lax.broadcasted_iota(jnp.int32, (QB, QB), 1)
    return kj - qi


def _prev_mask(rel, has_prev):
    return rel >= jnp.where(has_prev, 0, QB)


def _dsa_fwd(q, k, vp, *, name):
    rows = q.shape[0]
    s = rows // 12
    big = QB * DSA_SUB
    nsteps = rows // big

    def kern(q_ref, k_ref, kp_ref, v_ref, vpv_ref, o_ref):
        t = pl.program_id(0)
        bps = _dsa_seq_blocks(t, s)
        rel = _dsa_rel()
        lane = lax.broadcasted_iota(jnp.int32, (QB, LANES), 1)
        for a in range(DSA_SUB):
            qa = q_ref[pl.ds(a * QB, QB), :]
            kc = k_ref[pl.ds(a * QB, QB), :]
            vc = v_ref[pl.ds(a * QB, QB), :]
            if a == 0:
                kpv, vpv = kp_ref[...], vpv_ref[...]
            else:
                kpv, vpv = k_ref[pl.ds((a - 1) * QB, QB), :], v_ref[pl.ds((a - 1) * QB, QB), :]
            has_prev = (t * DSA_SUB + a) % bps != 0
            sc = jnp.where(rel <= 0, _nt(qa, kc) * SCALE, -jnp.inf)
            sp = jnp.where(_prev_mask(rel, has_prev), _nt(qa, kpv) * SCALE, -jnp.inf)
            m = jnp.maximum(jnp.max(sc, axis=1, keepdims=True), jnp.max(sp, axis=1, keepdims=True))
            pc = jnp.exp(sc - m)
            pp = jnp.exp(sp - m)
            den = jnp.sum(pc, axis=1, keepdims=True) + jnp.sum(pp, axis=1, keepdims=True)
            o = (_nn(pc, vc) + _nn(pp, vpv)) / den
            o_ref[pl.ds(a * QB, QB), :] = jnp.where(lane < HEAD_DIM, o, m + jnp.log(den))

    cur64 = pl.BlockSpec((big, HEAD_DIM), lambda t: (t, 0))
    prev64 = pl.BlockSpec((QB, HEAD_DIM), lambda t: (jnp.maximum(t * DSA_SUB - 1, 0), 0))
    cur128 = pl.BlockSpec((big, LANES), lambda t: (t, 0))
    prev128 = pl.BlockSpec((QB, LANES), lambda t: (jnp.maximum(t * DSA_SUB - 1, 0), 0))
    return _pcall(kern, name=name, grid=(nsteps,), in_specs=[cur64, cur64, prev64, cur128, prev128], out_specs=cur128,
                  out_shape=jax.ShapeDtypeStruct((rows, LANES), F32), compiler_params=_params("parallel"))(q, k, k, vp, vp)


def _dsa_combine(p0, p1, p2, *, name):
    hh, s, _ = p0.shape
    ts = _pick(s, (512, 256))

    def kern(a_ref, b_ref, c_ref, o_ref):
        lane = lax.broadcasted_iota(jnp.int32, (ts, LANES), 1)
        xs = [a_ref[...], b_ref[...], c_ref[...]]
        ls = [jnp.where(lane < HEAD_DIM, pltpu.roll(x, HEAD_DIM, 1), x) for x in xs]
        m = jnp.maximum(jnp.maximum(ls[0], ls[1]), ls[2])
        es = [jnp.exp(l - m) for l in ls]
        den = es[0] + es[1] + es[2]
        o = (es[0] * xs[0] + es[1] * xs[1] + es[2] * xs[2]) / den
        o_ref[...] = jnp.where(lane < HEAD_DIM, o, m + jnp.log(den))

    blk = pl.BlockSpec((None, ts, LANES), lambda a, i: (a, i, 0))
    return _pcall(kern, name=name, grid=(hh, s // ts), in_specs=[blk, blk, blk], out_specs=blk,
                  out_shape=jax.ShapeDtypeStruct((hh, s, LANES), F32), compiler_params=_params("parallel", "parallel"))(p0, p1, p2)


def _dsa_bwd_prep(comb, dop, *, name):
    hh, s, _ = comb.shape
    ts = _pick(s, (512, 256))

    def kern(c_ref, d_ref, o_ref):
        lane = lax.broadcasted_iota(jnp.int32, (ts, LANES), 1)
        c = c_ref[...]
        d = d_ref[...]
        dsum = jnp.sum(jnp.where(lane < HEAD_DIM, c * d, 0.0), axis=1, keepdims=True)
        o_ref[...] = jnp.where(lane < HEAD_DIM, d, jnp.where(lane < HEAD_DIM + 32, c, dsum))

    blk = pl.BlockSpec((None, ts, LANES), lambda a, i: (a, i, 0))
    return _pcall(kern, name=name, grid=(hh, s // ts), in_specs=[blk, blk], out_specs=blk,
                  out_shape=jax.ShapeDtypeStruct((hh, s, LANES), F32), compiler_params=_params("parallel", "parallel"))(comb, dop)


def _dsa_bwd(q, k, vp, pk, *, name):
    rows = q.shape[0]
    s = rows // 12
    big = QB * DSA_SUB
    nsteps = rows // big
    nblk = rows // QB

    def kern(q_ref, qn_ref, k_ref, kp_ref, v_ref, vpv_ref, p_ref, pn_ref, dq_ref, dk_ref, dv_ref):
        t = pl.program_id(0)
        bps = _dsa_seq_blocks(t, s)
        rel = _dsa_rel()
        lane = lax.broadcasted_iota(jnp.int32, (QB, LANES), 1)

        def stats(pa):
            lse = jnp.max(jnp.where(jnp.logical_and(lane >= HEAD_DIM, lane < HEAD_DIM + 32), pa, -jnp.inf), axis=1, keepdims=True)
            dsum = jnp.max(jnp.where(lane >= HEAD_DIM + 32, pa, -jnp.inf), axis=1, keepdims=True)
            return lse, dsum

        def pair(qa, pa, st, kb, vb, mask):
            p = jnp.where(mask, jnp.exp(_nt(qa, kb) * SCALE - st[0]), 0.0)
            ds = p * (_nt(pa, vb) - st[1]) * SCALE
            return _nn(ds, kb), _tn(ds, qa), _tn(p, pa)

        for a in range(DSA_SUB):
            qa = q_ref[pl.ds(a * QB, QB), :]
            pa = p_ref[pl.ds(a * QB, QB), :]
            st = stats(pa)
            kc = k_ref[pl.ds(a * QB, QB), :]
            vc = v_ref[pl.ds(a * QB, QB), :]
            if a == 0:
                kpv, vpv = kp_ref[...], vpv_ref[...]
            else:
                kpv, vpv = k_ref[pl.ds((a - 1) * QB, QB), :], v_ref[pl.ds((a - 1) * QB, QB), :]
            has_prev = (t * DSA_SUB + a) % bps != 0
            dq_c, dk_c, dv_c = pair(qa, pa, st, kc, vc, rel <= 0)
            dq_p, dk_p, dv_p = pair(qa, pa, st, kpv, vpv, _prev_mask(rel, has_prev))
            dq_ref[pl.ds(a * QB, QB), :] = dq_c + dq_p
            if a == 0:
                dk_ref[pl.ds(0, QB), :] = dk_c
                dv_ref[pl.ds(0, QB), :] = dv_c
            else:
                dk_ref[pl.ds(a * QB, QB), :] = dk_c
                dv_ref[pl.ds(a * QB, QB), :] = dv_c
                dk_ref[pl.ds((a - 1) * QB, QB), :] += dk_p
                dv_ref[pl.ds((a - 1) * QB, QB), :] += dv_p
        nxt = t * DSA_SUB + DSA_SUB
        has_next = jnp.logical_and(nxt < nblk, nxt % bps != 0)
        last = (DSA_SUB - 1) * QB
        pn = pn_ref[...]
        _, dk_n, dv_n = pair(qn_ref[...], pn, stats(pn), k_ref[pl.ds(last, QB), :], v_ref[pl.ds(last, QB), :],
                             _prev_mask(rel, has_next))
        dk_ref[pl.ds(last, QB), :] += dk_n
        dv_ref[pl.ds(last, QB), :] += dv_n

    def prev_map(t):
        return (jnp.maximum(t * DSA_SUB - 1, 0), 0)

    def next_map(t):
        return (jnp.minimum(t * DSA_SUB + DSA_SUB, nblk - 1), 0)

    cur64 = pl.BlockSpec((big, HEAD_DIM), lambda t: (t, 0))
    cur128 = pl.BlockSpec((big, LANES), lambda t: (t, 0))
    specs = [cur64, pl.BlockSpec((QB, HEAD_DIM), next_map), cur64, pl.BlockSpec((QB, HEAD_DIM), prev_map),
             cur128, pl.BlockSpec((QB, LANES), prev_map), cur128, pl.BlockSpec((QB, LANES), next_map)]
    return _pcall(kern, name=name, grid=(nsteps,), in_specs=specs, out_specs=[cur64, cur64, cur128],
                  out_shape=[jax.ShapeDtypeStruct((rows, HEAD_DIM), F32), jax.ShapeDtypeStruct((rows, HEAD_DIM), F32),
                             jax.ShapeDtypeStruct((rows, LANES), F32)],
                  compiler_params=_params("parallel"))(q, q, k, k, vp, vp, pk, pk)


def _mem_fwd(q, km, vm, *, name):
    hh, s, hd = q.shape
    ml = km.shape[1]
    tq = _pick(s, (512, 256))

    def kern(q_ref, k_ref, v_ref, o_ref):
        sc = _nt(q_ref[...], k_ref[...]) * SCALE
        e = jnp.exp(sc - jnp.max(sc, axis=1, keepdims=True))
        p = e / jnp.sum(e, axis=1, keepdims=True)
        o_ref[...] = _nn(p, v_ref[...])

    blk = pl.BlockSpec((None, tq, hd), lambda a, i: (a, i, 0))
    kv = pl.BlockSpec((None, ml, hd), lambda a, i: (a, 0, 0))
    return _pcall(kern, name=name, grid=(hh, s // tq), in_specs=[blk, kv, kv], out_specs=blk,
                  out_shape=jax.ShapeDtypeStruct((hh, s, hd), F32), compiler_params=_params("parallel", "parallel"))(q, km, vm)


def _mem_bwd(q, km, vm, do, *, name):
    hh, s, hd = q.shape
    ml = km.shape[1]
    tq = _pick(s, (512, 256))

    def kern(q_ref, k_ref, v_ref, do_ref, dq_ref, dk_ref, dv_ref):
        @pl.when(pl.program_id(1) == 0)
        def _():
            dk_ref[...] = jnp.zeros_like(dk_ref)
            dv_ref[...] = jnp.zeros_like(dv_ref)

        qb = q_ref[...]
        dob = do_ref[...]
        sc = _nt(qb, k_ref[...]) * SCALE
        e = jnp.exp(sc - jnp.max(sc, axis=1, keepdims=True))
        p = e / jnp.sum(e, axis=1, keepdims=True)
        dp = _nt(dob, v_ref[...])
        ds = p * (dp - jnp.sum(p * dp, axis=1, keepdims=True)) * SCALE
        dq_ref[...] = _nn(ds, k_ref[...])
        dk_ref[...] += _tn(ds, qb)
        dv_ref[...] += _tn(p, dob)

    blk = pl.BlockSpec((None, tq, hd), lambda a, i: (a, i, 0))
    kv = pl.BlockSpec((None, ml, hd), lambda a, i: (a, 0, 0))
    kvs = jax.ShapeDtypeStruct((hh, ml, hd), F32)
    return _pcall(kern, name=name, grid=(hh, s // tq), in_specs=[blk, kv, kv, blk], out_specs=[blk, kv, kv],
                  out_shape=[jax.ShapeDtypeStruct((hh, s, hd), F32), kvs, kvs],
                  compiler_params=_params("parallel", "arbitrary"))(q, km, vm, do)


DSA_BT = QB * max(r for _, r in DSA_GROUPS)


def _unit_rows(r, c, b):
    return pl.ds(c + QB * r * b, QB, stride=r)


def _pair_cols(t, first):
    return [jnp.max(jnp.where(first, t, -jnp.inf), axis=1, keepdims=True),
            jnp.max(jnp.where(first, -jnp.inf, t), axis=1, keepdims=True)]


def _dsa2_fwd(qn, kn, v32, g, *, name):
    s = qn.shape[0]
    r = DSA_GROUPS[g][1]
    nbk = DSA_BT // (QB * r)
    npair = DSA_OUT_W // LANES

    def kern(q_ref, k_ref, kp_ref, v_ref, vp_ref, o_ref, l_ref):
        t = pl.program_id(1)
        first = _first_half((QB, LANES))
        rel = _dsa_rel()
        for c in range(r):
            for b in range(nbk):
                rows = _unit_rows(r, c, b)
                kc, vc = k_ref[rows, :], v_ref[rows, :]
                if b > 0:
                    prow = _unit_rows(r, c, b - 1)
                    kpv, vpv, has_prev = k_ref[prow, :], v_ref[prow, :], True
                else:
                    prow = _unit_rows(r, c, nbk - 1)
                    kpv, vpv, has_prev = kp_ref[prow, :], vp_ref[prow, :], t > 0
                outs, lses = [], []
                for qe in _split_pair(q_ref[rows, :], first):
                    sc = jnp.where(rel <= 0, _nt(qe, kc) * SCALE, -jnp.inf)
                    sp = jnp.where(_prev_mask(rel, has_prev), _nt(qe, kpv) * SCALE, -jnp.inf)
                    m = jnp.maximum(jnp.max(sc, axis=1, keepdims=True), jnp.max(sp, axis=1, keepdims=True))
                    pc = jnp.exp(sc - m)
                    pp = jnp.exp(sp - m)
                    den = jnp.sum(pc, axis=1, keepdims=True) + jnp.sum(pp, axis=1, keepdims=True)
                    outs.append((_nn(pc, vc) + _nn(pp, vpv)) / den)
                    lses.append(m + jnp.log(den))
                o_ref[rows, :] = jnp.where(first, outs[0], outs[1])
                l_ref[rows, :] = jnp.where(first, lses[0], lses[1])

    npg = DSA_HPG * HEAD_DIM // LANES
    cur = pl.BlockSpec((DSA_BT, LANES), lambda a, t: (t, npg * g + a))
    prev = pl.BlockSpec((DSA_BT, LANES), lambda a, t: (jnp.maximum(t - 1, 0), npg * g + a))
    out = pl.BlockSpec((DSA_BT, LANES), lambda a, t: (t, a))
    shp = jax.ShapeDtypeStruct((s, DSA_OUT_W), F32)
    return _pcall(kern, name=name, grid=(npair, s // DSA_BT), in_specs=[cur, cur, prev, cur, prev], out_specs=[out, out],
                  out_shape=[shp, shp], compiler_params=_params("parallel", "parallel"))(qn, kn, kn, v32, v32)


def _dsa2_combine(parts, *, name):
    s, wd = parts[0][0].shape
    ts = _pick(s, (512, 256))

    def kern(o0, l0, o1, l1, o2, l2, o_ref, l_ref):
        ls = [l0[...], l1[...], l2[...]]
        m = jnp.maximum(jnp.maximum(ls[0], ls[1]), ls[2])
        es = [jnp.exp(l - m) for l in ls]
        den = es[0] + es[1] + es[2]
        o_ref[...] = (es[0] * o0[...] + es[1] * o1[...] + es[2] * o2[...]) / den
        l_ref[...] = m + jnp.log(den)

    blk = pl.BlockSpec((ts, wd), lambda i: (i, 0))
    shp = jax.ShapeDtypeStruct((s, wd), F32)
    flat = [t for pair in parts for t in pair]
    return _pcall(kern, name=name, grid=(s // ts,), in_specs=[blk] * 6, out_specs=[blk, blk], out_shape=[shp, shp],
                  compiler_params=_params("parallel"))(*flat)


def _dsa2_prep(o, do, *, name):
    s, wd = o.shape
    ts = _pick(s, (512, 256))

    def kern(o_ref, do_ref, d_ref):
        d_ref[...] = _head_mean(do_ref[...] * o_ref[...], _block_diag(wd)) * HEAD_DIM

    blk = pl.BlockSpec((ts, wd), lambda i: (i, 0))
    return _pcall(kern, name=name, grid=(s // ts,), in_specs=[blk, blk], out_specs=blk,
                  out_shape=jax.ShapeDtypeStruct((s, wd), F32), compiler_params=_params("parallel"))(o, do)


def _dsa2_bwd(qn, kn, v32, do, lse, dd, g, *, name):
    s = qn.shape[0]
    r = DSA_GROUPS[g][1]
    nbk = DSA_BT // (QB * r)
    npair = DSA_OUT_W // LANES
    nsteps = s // DSA_BT

    def kern(q_ref, qn_ref, k_ref, kp_ref, v_ref, vp_ref, do_ref, don_ref, l_ref, ln_ref, d_ref, dn_ref,
             dq_ref, dk_ref, dv_ref):
        t = pl.program_id(1)
        first = _first_half((QB, LANES))
        rel = _dsa_rel()

        def pair(qs, dos, lcols, dcols, kb, vb, mask):
            dqs = []
            dk = jnp.zeros((QB, LANES), F32)
            dv = jnp.zeros((QB, LANES), F32)
            for e in range(2):
                p = jnp.where(mask, jnp.exp(_nt(qs[e], kb) * SCALE - lcols[e]), 0.0)
                ds = p * (_nt(dos[e], vb) - dcols[e]) * SCALE
                dqs.append(_nn(ds, kb))
                dk = dk + _tn(ds, qs[e])
                dv = dv + _tn(p, dos[e])
            return dqs, dk, dv

        def load(rows, qr, dor, lr, dr):
            return (_split_pair(qr[rows, :], first), _split_pair(dor[rows, :], first), _pair_cols(lr[rows, :], first),
                    _pair_cols(dr[rows, :], first))

        for c in range(r):
            for b in range(nbk):
                rows = _unit_rows(r, c, b)
                qs, dos, lcols, dcols = load(rows, q_ref, do_ref, l_ref, d_ref)
                dq_c, dk_c, dv_c = pair(qs, dos, lcols, dcols, k_ref[rows, :], v_ref[rows, :], rel <= 0)
                if b > 0:
                    prow = _unit_rows(r, c, b - 1)
                    dq_p, dk_p, dv_p = pair(qs, dos, lcols, dcols, k_ref[prow, :], v_ref[prow, :], _prev_mask(rel, True))
                    dk_ref[prow, :] += dk_p
                    dv_ref[prow, :] += dv_p
                else:
                    prow = _unit_rows(r, c, nbk - 1)
                    dq_p, _, _ = pair(qs, dos, lcols, dcols, kp_ref[prow, :], vp_ref[prow, :], _prev_mask(rel, t > 0))
                dq_ref[rows, :] = jnp.where(first, dq_c[0] + dq_p[0], dq_c[1] + dq_p[1])
                dk_ref[rows, :] = dk_c
                dv_ref[rows, :] = dv_c
            last = _unit_rows(r, c, nbk - 1)
            nqs, ndos, nl, nd = load(_unit_rows(r, c, 0), qn_ref, don_ref, ln_ref, dn_ref)
            _, dk_n, dv_n = pair(nqs, ndos, nl, nd, k_ref[last, :], v_ref[last, :], _prev_mask(rel, t < nsteps - 1))
            dk_ref[last, :] += dk_n
            dv_ref[last, :] += dv_n

    npg = DSA_HPG * HEAD_DIM // LANES

    def at(shift, col):
        return pl.BlockSpec((DSA_BT, LANES), lambda a, t: (jnp.clip(t + shift, 0, nsteps - 1), col(a)))

    gcol = lambda a: npg * g + a
    ocol = lambda a: a
    specs = [at(0, gcol), at(1, gcol), at(0, gcol), at(-1, gcol), at(0, gcol), at(-1, gcol),
             at(0, ocol), at(1, ocol), at(0, ocol), at(1, ocol), at(0, ocol), at(1, ocol)]
    shp = jax.ShapeDtypeStruct((s, DSA_OUT_W), F32)
    return _pcall(kern, name=name, grid=(npair, nsteps), in_specs=specs, out_specs=[at(0, ocol)] * 3, out_shape=[shp, shp, shp],
                  compiler_params=_params("parallel", "parallel"))(qn, qn, kn, kn, v32, v32, do, do, lse, lse, dd, dd)


def _mem2_fwd(qn, km, kv, *, name):
    s = qn.shape[0]
    ml = km.shape[0]
    tq = _pick(s, (512, 256))
    npair = MEM_W // LANES

    def kern(q_ref, k_ref, v_ref, o_ref):
        first = _first_half((tq, LANES))
        outs = []
        for qe in _split_pair(q_ref[...], first):
            sc = _nt(qe, k_ref[...]) * SCALE
            e = jnp.exp(sc - jnp.max(sc, axis=1, keepdims=True))
            outs.append(_nn(e / jnp.sum(e, axis=1, keepdims=True), v_ref[...]))
        o_ref[...] = jnp.where(first, outs[0], outs[1])

    blk = pl.BlockSpec((tq, LANES), lambda a, i: (i, a))
    return _pcall(kern, name=name, grid=(npair, s // tq),
                  in_specs=[blk, pl.BlockSpec((ml, LANES), lambda a, i: (0, a)), pl.BlockSpec((ml, LANES), lambda a, i: (0, npair + a))],
                  out_specs=blk, out_shape=jax.ShapeDtypeStruct((s, MEM_W), F32),
                  compiler_params=_params("parallel", "parallel"))(qn, km, kv)


def _mem2_bwd(qn, km, kv, do, *, name):
    s = qn.shape[0]
    ml = km.shape[0]
    tq = _pick(s, (512, 256))
    npair = MEM_W // LANES

    def kern(q_ref, k_ref, v_ref, do_ref, dq_ref, dk_ref, dv_ref):
        @pl.when(pl.program_id(1) == 0)
        def _():
            dk_ref[...] = jnp.zeros_like(dk_ref)
            dv_ref[...] = jnp.zeros_like(dv_ref)

        first = _first_half((tq, LANES))
        dqs = []
        for qe, doe in zip(_split_pair(q_ref[...], first), _split_pair(do_ref[...], first)):
            sc = _nt(qe, k_ref[...]) * SCALE
            e = jnp.exp(sc - jnp.max(sc, axis=1, keepdims=True))
            p = e / jnp.sum(e, axis=1, keepdims=True)
            dp = _nt(doe, v_ref[...])
            ds = p * (dp - jnp.sum(p * dp, axis=1, keepdims=True)) * SCALE
            dqs.append(_nn(ds, k_ref[...]))
            dk_ref[...] += _tn(ds, qe)
            dv_ref[...] += _tn(p, doe)
        dq_ref[...] = jnp.where(first, dqs[0], dqs[1])

    blk = pl.BlockSpec((tq, LANES), lambda a, i: (i, a))
    kblk = pl.BlockSpec((ml, LANES), lambda a, i: (0, a))
    kshape = jax.ShapeDtypeStruct((ml, MEM_W), F32)
    return _pcall(kern, name=name, grid=(npair, s // tq),
                  in_specs=[blk, kblk, pl.BlockSpec((ml, LANES), lambda a, i: (0, npair + a)), blk],
                  out_specs=[blk, kblk, kblk], out_shape=[jax.ShapeDtypeStruct((s, MEM_W), F32), kshape, kshape],
                  compiler_params=_params("parallel", "arbitrary"))(qn, km, kv, do)


def _merge_fwd(logits, bias, ya, yb, yc, *, name):
    s, d = ya.shape
    ts = _pick(s, (512, 256))

    def kern(l0, l1, l2, b0, b1, b2, a_ref, b_ref, c_ref, o_ref):
        m = (_sigmoid(l0[...] + b0[...]) * a_ref[...] + _sigmoid(l1[...] + b1[...]) * b_ref[...]
             + _sigmoid(l2[...] + b2[...]) * c_ref[...])
        o_ref[...] = m.astype(o_ref.dtype)

    row = pl.BlockSpec((ts, d), lambda i: (i, 0))
    lg = [pl.BlockSpec((ts, d), functools.partial(lambda i, c: (i, c), c=c)) for c in range(3)]
    bs = [pl.BlockSpec((1, d), functools.partial(lambda i, c: (0, c), c=c)) for c in range(3)]
    return _pcall(kern, name=name, grid=(s // ts,), in_specs=lg + bs + [row, row, row], out_specs=row,
                  out_shape=jax.ShapeDtypeStruct((s, d), BF16),
                  compiler_params=_params("parallel"))(logits, logits, logits, bias, bias, bias, ya, yb, yc)


def _merge_bwd(logits, bias, ya, yb, yc, dm, *, name):
    s, d = ya.shape
    ts = _pick(s, (256,))

    def kern(l0, l1, l2, b0, b1, b2, a_ref, b_ref, c_ref, dm_ref, da_ref, db_ref, dc_ref, dl0, dl1, dl2, dbias0, dbias1, dbias2):
        first = pl.program_id(0) == 0
        dmv = dm_ref[...]
        for l_ref, bb_ref, y_ref, dy_ref, dl_ref, dbias_ref in ((l0, b0, a_ref, da_ref, dl0, dbias0), (l1, b1, b_ref, db_ref, dl1, dbias1),
                                                                (l2, b2, c_ref, dc_ref, dl2, dbias2)):
            g = _sigmoid(l_ref[...] + bb_ref[...])
            dy_ref[...] = (dmv * g).astype(dy_ref.dtype)
            dl = dmv * y_ref[...] * g * (1.0 - g)
            dl_ref[...] = dl.astype(dl_ref.dtype)

            @pl.when(first)
            def _():
                dbias_ref[...] = jnp.zeros_like(dbias_ref)

            dbias_ref[...] += jnp.sum(dl, axis=0, keepdims=True)

    row = pl.BlockSpec((ts, d), lambda i: (i, 0))
    lg = [pl.BlockSpec((ts, d), functools.partial(lambda i, c: (i, c), c=c)) for c in range(3)]
    bs = [pl.BlockSpec((1, d), functools.partial(lambda i, c: (0, c), c=c)) for c in range(3)]
    vec = pl.BlockSpec((1, d), lambda i: (0, 0))
    yshape = jax.ShapeDtypeStruct((s, d), BF16)
    vshape = jax.ShapeDtypeStruct((1, d), F32)
    outs = _pcall(kern, name=name, grid=(s // ts,), in_specs=lg + bs + [row, row, row, row],
                  out_specs=[row, row, row, row, row, row, vec, vec, vec],
                  out_shape=[yshape] * 6 + [vshape] * 3,
                  compiler_params=_params("arbitrary"))(logits, logits, logits, bias, bias, bias, ya, yb, yc, dm)
    return outs[0], outs[1], outs[2], outs[3:6], jnp.concatenate(outs[6:9], axis=1)


def _heads(t, n):
    s = t.shape[0]
    return t.reshape(s, n, HEAD_DIM).transpose(1, 0, 2)


def _unheads(t):
    n, s, hd = t.shape
    return t.transpose(1, 0, 2).reshape(s, n * hd)


def _to_class_major(t):
    s = t.shape[0]
    w = t.shape[1] // (DSA_HPG * len(DSA_GROUPS))
    parts = []
    for g, (_, r) in enumerate(DSA_GROUPS):
        tg = t[:, g * DSA_HPG * w:(g + 1) * DSA_HPG * w].reshape(s // r, r, DSA_HPG, w)
        parts.append(tg.transpose(2, 1, 0, 3).reshape(DSA_HPG * s, w))
    return jnp.concatenate(parts, axis=0)


def _slot_to_class_major(t):
    hh, s, w = t.shape
    parts = []
    for _, r in DSA_GROUPS:
        parts.append(t.reshape(hh, s // r, r, w).transpose(0, 2, 1, 3).reshape(hh * s, w))
    return jnp.concatenate(parts, axis=0)


def _from_class_major(t):
    rows, w = t.shape
    s = rows // 12
    out = []
    for g, (_, r) in enumerate(DSA_GROUPS):
        tg = t[g * 4 * s:(g + 1) * 4 * s].reshape(DSA_HPG, r, s // r, w)
        out.append(tg.transpose(0, 2, 1, 3).reshape(DSA_HPG, s, w))
    return out


def _pad_lanes(t):
    return jnp.concatenate([t, jnp.zeros(t.shape[:-1] + (LANES - t.shape[-1],), t.dtype)], axis=-1)


G_FFN1 = ['ffn1_w1', 'ffn1_w3', 'ffn1_w2']
G_FFN2 = ['ffn2_w1', 'ffn2_w3', 'ffn2_w2']
G_MID = [n for n in BIG if n not in G_FFN1 + G_FFN2]


def _ffn_fwd(h, w1, w3, w2, tag, epilogue, side=None):
    carried = None
    if side is None:
        a, b, f = _ffn_up(h, w1, w3, name=f"{tag}_up")
    else:
        (a, b, f), carried = _ffn_up(h, w1, w3, name=f"{tag}_up", side=side)
    outs = _matmul(f, w2, name=f"{tag}_down", alpha=0.5, tm=512, tn=1024, tk=1408, epilogue=epilogue)
    return outs, (h, a, b, f), carried


def _ffn_bwd(x, norm, w1, w3, w2, saved, dy, dyb, tag, side=None):
    h, a, b, f = saved
    dw2 = _matmul(f, dyb, name=f"{tag}_dw2", ta=True, alpha=0.5, tm=1408, tn=1024, tk=512)
    carried = None
    if side is None:
        da, db = _ffn_dact(dyb, w2, a, b, name=f"{tag}_dact")
    else:
        (da, db), carried = _ffn_dact(dyb, w2, a, b, name=f"{tag}_dact", side=side)
    dw1 = _matmul(h, da, name=f"{tag}_dw1", ta=True, tm=1024, tn=1408, tk=512)
    dw3 = _matmul(h, db, name=f"{tag}_dw3", ta=True, tm=1024, tn=1408, tk=512)
    dx, dxb, dnorm = _matmul(da, w1, name=f"{tag}_dh", tb=True, tm=512, tn=1024, tk=1408, pair2=(db, w3),
                             epilogue=(_epi_rms_bwd, [x, dy], [norm], [F32, BF16], 1))
    return dx, dxb, dnorm, dw1, dw3, dw2, carried


def _local_step(x, mem, loss_target, wl, ws):
    s, d = x.shape
    assert s % (QB * 16) == 0
    rope = _rope_tables(s)
    bf = {n: wl[n].astype(BF16) for n in BIG}
    w = dict(ws)
    w.update(_unpack_gathered(_exchange(_pack_rows(bf, G_FFN1), _two_level_phases(), name="gather_ffn1"), wl, G_FFN1))

    h1 = _rms_fwd(x, w['ffn1_norm'], name="ffn1_rms")
    (x1, h), sv1, late = _ffn_fwd(h1, w['ffn1_w1'], w['ffn1_w3'], w['ffn1_w2'], "ffn1",
                                  (_epi_residual_rms, [x], [w['mix_norm']], [F32, BF16], 0),
                                  side=_side(_pack_rows(bf, G_MID + G_FFN2), _two_level_phases()))
    w.update(_unpack_gathered(late, wl, G_MID + G_FFN2))
    p = _matmul(h, w['w_in'], name="in_proj", out_dtype=BF16, tn=1024)
    logits = _matmul(h, w['w_gate'], name="gate_proj", tn=1024)
    c_qb, c_kb, c_vb, c_qc = 3 * SB_W, 3 * SB_W + DSA_W, 3 * SB_W + 2 * DSA_W, 3 * SB_W + 3 * DSA_W

    oa_t = _sb2_fwd(p, name="sb_fwd")
    ya = _matmul(oa_t, w['w_branch_sb'], name="sb_out")

    qb_n = _qknorm_fwd(p, c_qb, DSA_W, w['qn_dsa'], rope, name="dsa_qnorm", out_dtype=F32)
    kb_n = _qknorm_fwd(p, c_kb, DSA_W, w['kn_dsa'], rope, name="dsa_knorm", out_dtype=F32)
    vb32 = p[:, c_vb:c_vb + DSA_W].astype(F32)
    groups = range(len(DSA_GROUPS))
    ob_t, lse_b = _dsa2_combine([_dsa2_fwd(qb_n, kb_n, vb32, gi, name=f"dsa_fwd{gi}") for gi in groups], name="dsa_combine")
    yb = _matmul(ob_t, w['w_branch_dsa'], name="dsa_out")

    memh = _rms_fwd(mem, w['mem_norm'], name="mem_rms")
    kv = _matmul(memh, w['w_mem_kv'], name="mem_kv", out_dtype=BF16)
    km_n = _qknorm_fwd(kv, 0, MEM_W, w['kn_mem'], None, name="mem_knorm")
    qc_n = _qknorm_fwd(p, c_qc, MEM_W, w['qn_mem'], None, name="mem_qnorm")
    oc_t = _mem2_fwd(qc_n, km_n, kv, name="mem_fwd")
    yc = _matmul(oc_t, w['w_branch_mem'], name="mem_out")

    merged = _merge_fwd(logits, w['b_gate'], ya, yb, yc, name="merge")
    x2, h2 = _matmul(merged, w['w_out'], name="out_proj", tn=1024,
                     epilogue=(_epi_residual_rms, [x1], [w['ffn2_norm']], [F32, BF16], 0))
    (dx3, dx3b, sq), sv2, _ = _ffn_fwd(h2, w['ffn2_w1'], w['ffn2_w3'], w['ffn2_w2'], "ffn2",
                                       (_epi_loss, [x2, loss_target], [], [F32, BF16], 1))
    loss = jnp.sum(sq) * (0.5 / d)

    g, recv = {}, {}

    def owners(names):
        return _pack_for_owners(g, wl, names).astype(BF16)

    dx2, dx2b, g['ffn2_norm'], g['ffn2_w1'], g['ffn2_w3'], g['ffn2_w2'], _ = _ffn_bwd(
        x2, w['ffn2_norm'], w['ffn2_w1'], w['ffn2_w3'], w['ffn2_w2'], sv2, dx3, dx3b, "ffn2")

    g['w_out'] = _matmul(merged, dx2b, name="d_w_out", ta=True, tn=1024, tk=512)
    dm = _matmul(dx2b, w['w_out'], name="d_merged", tb=True, tn=1024)
    dya, dyb, dyc, dlog, g['b_gate'] = _merge_bwd(logits, w['b_gate'], ya, yb, yc, dm, name="d_merge")
    dlogits = jnp.concatenate(dlog, axis=1)

    g['w_branch_sb'] = _matmul(oa_t, dya, name="d_w_sb", ta=True, tn=1024, tk=512)
    g['w_branch_dsa'] = _matmul(ob_t, dyb, name="d_w_dsa", ta=True, tk=512)
    g['w_branch_mem'] = _matmul(oc_t, dyc, name="d_w_mem", ta=True, tk=512)
    doa = _matmul(dya, w['w_branch_sb'], name="d_oa", tb=True, out_dtype=BF16)
    dob = _matmul(dyb, w['w_branch_dsa'], name="d_ob", tb=True)
    doc = _matmul(dyc, w['w_branch_mem'], name="d_oc", tb=True, out_dtype=BF16)

    (dqa, dka, dva), recv['ffn2'] = _sb2_bwd(p, oa_t, doa, name="sb_bwd", side=_side(owners(G_FFN2), _direct_phases(True)))

    dd_b = _dsa2_prep(ob_t, dob, name="dsa_prep")
    dgrp = [_dsa2_bwd(qb_n, kb_n, vb32, dob, lse_b, dd_b, gi, name=f"dsa_bwd{gi}") for gi in groups]
    dvb = jnp.concatenate([t[2] for t in dgrp], axis=1).astype(BF16)
    dqb, g['qn_dsa'] = _qknorm_bwd(p, c_qb, DSA_W, w['qn_dsa'], rope, [t[0] for t in dgrp], name="d_dsa_qnorm")
    dkb, g['kn_dsa'] = _qknorm_bwd(p, c_kb, DSA_W, w['kn_dsa'], rope, [t[1] for t in dgrp], name="d_dsa_knorm")

    dqc_n, dkm_n, dvm = _mem2_bwd(qc_n, km_n, kv, doc, name="mem_bwd")
    dqc, g['qn_mem'] = _qknorm_bwd(p, c_qc, MEM_W, w['qn_mem'], None, dqc_n, name="d_mem_qnorm")
    dkm, g['kn_mem'] = _qknorm_bwd(kv, 0, MEM_W, w['kn_mem'], None, dkm_n, name="d_mem_knorm")
    dkv = jnp.concatenate([dkm, dvm.astype(BF16)], axis=1)
    g['w_mem_kv'] = _matmul(memh, dkv, name="d_w_mem_kv", ta=True)
    dmemh = _matmul(dkv, w['w_mem_kv'], name="d_memh", tb=True)
    _, _, g['mem_norm'] = _rms_bwd(mem, w['mem_norm'], dmemh, None, name="d_mem_rms")

    dp = jnp.concatenate([dqa.astype(BF16), dka.astype(BF16), dva.astype(BF16),
                          dqb, dkb, dvb, dqc], axis=1)
    g['w_in'] = _matmul(h, dp, name="d_w_in", ta=True, tn=2048, tk=512)
    g['w_gate'] = _matmul(h, dlogits, name="d_w_gate", ta=True, tn=1536, tk=512)
    dh = _matmul(dp, w['w_in'], name="d_h_in", tb=True, tn=1024)
    dx1, dx1b, g['mix_norm'] = _matmul(dlogits, w['w_gate'], name="d_h_gate", tb=True, tm=512, tn=1024,
                                       epilogue=(_epi_rms_bwd_sum, [dh, x1, dx2], [w['mix_norm']], [F32, BF16], 1))

    dx0, _, g['ffn1_norm'], g['ffn1_w1'], g['ffn1_w3'], g['ffn1_w2'], recv['mid'] = _ffn_bwd(
        x, w['ffn1_norm'], w['ffn1_w1'], w['ffn1_w3'], w['ffn1_w2'], sv1, dx1, dx1b, "ffn1",
        side=_side(owners(G_MID), _direct_phases(True)))
    recv['ffn1'] = _exchange(owners(G_FFN1), _direct_phases(True), name="scatter_ffn1")
    return loss, dx0, recv, {n: g[n] for n in SMALL}


def _pack_rows(d, names):
    return jnp.concatenate([d[n].reshape(-1, LANES) for n in names], axis=0)


def _unpack_rows(t, like, names):
    out, off = {}, 0
    for n in names:
        r = like[n].size // LANES
        out[n] = t[off:off + r].reshape(like[n].shape)
        off += r
    return out


def _unpack_gathered(t, local, names):
    out, off = {}, 0
    for n in names:
        r, c = local[n].shape
        rows = r * c // LANES
        blk = t[:, off:off + rows].reshape(N_DEV, r, c)
        out[n] = blk.reshape(N_DEV * r, c) if SHARD_AXIS[n] == 0 else blk.transpose(1, 0, 2).reshape(r, N_DEV * c)
        off += rows
    return out


def _pack_for_owners(g, local, names):
    parts = []
    for n in names:
        r, c = local[n].shape
        blk = g[n].reshape(N_DEV, r, c) if SHARD_AXIS[n] == 0 else g[n].reshape(r, N_DEV, c).transpose(1, 0, 2)
        parts.append(blk.reshape(N_DEV, r * c // LANES, LANES))
    return jnp.concatenate(parts, axis=1)


def _pack_small(d, names, extra_rows):
    parts = []
    for n in names:
        v = d[n].reshape(-1)
        pad = (-v.size) % LANES
        parts.append(jnp.concatenate([v, jnp.zeros((pad,), v.dtype)]).reshape(-1, LANES))
    t = jnp.concatenate(parts, axis=0)
    return jnp.concatenate([t, jnp.zeros((extra_rows, LANES), t.dtype)], axis=0)


def _unpack_small(t, like, names):
    out, off = {}, 0
    for n in names:
        size = like[n].size
        rows = -(-size // LANES)
        out[n] = t[off:off + rows].reshape(-1)[:size].reshape(like[n].shape)
        off += rows
    return out


def _direct_phases(per_peer):
    def descriptors(src_ref, out_ref, send_sems, recv_sems, local_sem):
        x, y, c = lax.axis_index("x"), lax.axis_index("y"), lax.axis_index("c")
        me = 4 * x + 2 * y + c
        mine = pltpu.make_async_copy(src_ref.at[me] if per_peer else src_ref, out_ref.at[me], local_sem)
        copies = []
        for k in range(1, N_DEV):
            px = 1 - x if k & 4 else x
            py = 1 - y if k & 2 else y
            pc = 1 - c if k & 1 else c
            copies.append(pltpu.make_async_remote_copy(
                src_ref=src_ref.at[4 * px + 2 * py + pc] if per_peer else src_ref, dst_ref=out_ref.at[me],
                send_sem=send_sems.at[k - 1], recv_sem=recv_sems.at[k - 1],
                device_id=(px, py, pc), device_id_type=pl.DeviceIdType.MESH))
        return mine, copies

    def start(*refs):
        mine, copies = descriptors(*refs)
        mine.start()
        for cp in copies:
            cp.start()

    def forward(*refs):
        pass

    def finish(*refs):
        mine, copies = descriptors(*refs)
        for cp in copies:
            cp.wait_recv()
        for cp in copies:
            cp.wait_send()
        mine.wait()

    return start, forward, finish


EXCHANGE_SEMS = [pltpu.SemaphoreType.DMA((N_DEV - 1,)), pltpu.SemaphoreType.DMA((N_DEV - 1,)), pltpu.SemaphoreType.DMA]


def _exchange(src, phases, *, name):
    rows = src.shape[-2]

    def body(*refs):
        for phase in phases:
            phase(*refs)

    anyspace = pl.BlockSpec(memory_space=pl.ANY)
    return _pcall(body, name=name, in_specs=[anyspace], out_specs=anyspace,
                  out_shape=jax.ShapeDtypeStruct((N_DEV, rows, LANES), src.dtype), scratch_shapes=list(EXCHANGE_SEMS))(src)


def _side(src, phases):
    start, forward, finish = phases

    def before(first, mid, ins, outs, scratch):
        pl.when(first)(lambda: start(ins[0], outs[0], *scratch))
        pl.when(mid)(lambda: forward(ins[0], outs[0], *scratch))

    def after(last, ins, outs, scratch):
        pl.when(last)(lambda: finish(ins[0], outs[0], *scratch))

    return [src], [jax.ShapeDtypeStruct((N_DEV, src.shape[-2], LANES), src.dtype)], list(EXCHANGE_SEMS), before, after


def _call_2d(kern, *, name, grid, in_specs, out_specs, out_shape, ins, scratch_shapes=(), semantics, side=None):
    if side is None:
        return _pcall(kern, name=name, grid=grid, in_specs=in_specs, out_specs=out_specs, out_shape=out_shape,
                      scratch_shapes=list(scratch_shapes), compiler_params=_params(*semantics))(*ins)
    s_ins, s_shapes, s_scratch, before, after = side
    n_in, n_out, n_scr = len(ins), len(out_shape), len(scratch_shapes)

    def combined(*refs):
        refs = list(refs)
        cut = [n_in, len(s_ins), n_out, len(s_shapes), n_scr, len(s_scratch)]
        parts, pos = [], 0
        for c in cut:
            parts.append(refs[pos:pos + c])
            pos += c
        m_in, c_in, m_out, c_out, m_scr, c_scr = parts
        i, j = pl.program_id(0), pl.program_id(1)
        first = jnp.logical_and(i == 0, j == 0)
        mid = jnp.logical_and(i == grid[0] // 2, j == 0)
        last = jnp.logical_and(i == grid[0] - 1, j == grid[1] - 1)
        before(first, mid, c_in, c_out, c_scr)
        kern(*m_in, *m_out, *m_scr)
        after(last, c_in, c_out, c_scr)

    anyspace = pl.BlockSpec(memory_space=pl.ANY)
    outs = _pcall(combined, name=name, grid=grid, in_specs=list(in_specs) + [anyspace] * len(s_ins),
                  out_specs=list(out_specs) + [anyspace] * len(s_shapes), out_shape=list(out_shape) + s_shapes,
                  scratch_shapes=list(scratch_shapes) + s_scratch, compiler_params=_params("arbitrary", "arbitrary"))(*ins, *s_ins)
    return outs[:n_out], outs[n_out]


def _two_level_phases():
    def parts(src_ref, out_ref, send_sems, recv_sems, local_sem):
        x, y, c = lax.axis_index("x"), lax.axis_index("y"), lax.axis_index("c")
        me, sibling = (x, y, c), (x, y, 1 - c)
        chips = [(1 - x, y), (x, 1 - y), (1 - x, 1 - y)]

        def slab(px, py, pc):
            return out_ref.at[4 * px + 2 * py + pc]

        def copy(k, block, to, from_src=False):
            return pltpu.make_async_remote_copy(
                src_ref=src_ref if from_src else slab(*block), dst_ref=slab(*block),
                send_sem=send_sems.at[k], recv_sem=recv_sems.at[k], device_id=to, device_id_type=pl.DeviceIdType.MESH)

        return dict(
            mine=lambda: pltpu.make_async_copy(src_ref, slab(*me), local_sem),
            first=lambda: [copy(0, me, sibling, True)] + [copy(1 + j, me, (*chip, c), True) for j, chip in enumerate(chips)],
            passed=lambda: [copy(4 + j, (*chip, c), sibling) for j, chip in enumerate(chips)],
            landed=lambda: [copy(1 + j, (*chip, c), me) for j, chip in enumerate(chips)],
            late=lambda: [copy(0, sibling, me)] + [copy(4 + j, (*chip, 1 - c), me) for j, chip in enumerate(chips)])

    def start(*refs):
        make = parts(*refs)
        make['mine']().start()
        for cp in make['first']():
            cp.start()

    def forward(*refs):
        make = parts(*refs)
        for arrived, onward in zip(make['landed'](), make['passed']()):
            arrived.wait_recv()
            onward.start()

    def finish(*refs):
        make = parts(*refs)
        for cp in make['late']():
            cp.wait_recv()
        for cp in make['first']() + make['passed']():
            cp.wait_send()
        make['mine']().wait()

    return start, forward, finish


def _adamw(recv, w, m, v, *, name):
    rows = w.shape[0]
    tr = _pick(rows, (512, 256, 128, 64))

    def kern(r_ref, w_ref, m_ref, v_ref, g_ref, d_ref, mo_ref, vo_ref):
        g = r_ref[0].astype(F32)
        for p in range(1, N_DEV):
            g = g + r_ref[p].astype(F32)
        mn = ADAM_B1 * m_ref[...] + (1.0 - ADAM_B1) * g
        vn = ADAM_B2 * v_ref[...] + (1.0 - ADAM_B2) * (g * g)
        m_hat = mn / (1.0 - ADAM_B1 ** ADAM_STEP)
        v_hat = vn / (1.0 - ADAM_B2 ** ADAM_STEP)
        g_ref[...] = g
        d_ref[...] = -ADAM_LR * (m_hat / (jnp.sqrt(v_hat) + ADAM_EPS) + ADAM_WD * w_ref[...])
        mo_ref[...] = mn
        vo_ref[...] = vn

    row = pl.BlockSpec((tr, LANES), lambda i: (i, 0))
    shp = jax.ShapeDtypeStruct((rows, LANES), F32)
    return _pcall(kern, name=name, grid=(rows // tr,), in_specs=[pl.BlockSpec((N_DEV, tr, LANES), lambda i: (0, i, 0)), row, row, row],
                  out_specs=[row, row, row, row], out_shape=[shp, shp, shp, shp], compiler_params=_params("parallel"))(recv, w, m, v)


INPUTS = ['x', 'mem'] + WEIGHTS + ['loss_target'] + ['m_' + n for n in WEIGHTS] + ['v_' + n for n in WEIGHTS]
SMALL_PAD_ROWS = 4


def kernel(x, mem, ffn1_norm, ffn1_w1, ffn1_w3, ffn1_w2, mix_norm, mem_norm, w_in, w_mem_kv, qn_dsa, kn_dsa, qn_mem, kn_mem, w_branch_sb, w_branch_dsa, w_branch_mem, w_gate, b_gate, w_out, ffn2_norm, ffn2_w1, ffn2_w3, ffn2_w2, loss_target, m_ffn1_norm, m_ffn1_w1, m_ffn1_w3, m_ffn1_w2, m_mix_norm, m_mem_norm, m_w_in, m_w_mem_kv, m_qn_dsa, m_kn_dsa, m_qn_mem, m_kn_mem, m_w_branch_sb, m_w_branch_dsa, m_w_branch_mem, m_w_gate, m_b_gate, m_w_out, m_ffn2_norm, m_ffn2_w1, m_ffn2_w3, m_ffn2_w2, v_ffn1_norm, v_ffn1_w1, v_ffn1_w3, v_ffn1_w2, v_mix_norm, v_mem_norm, v_w_in, v_w_mem_kv, v_qn_dsa, v_kn_dsa, v_qn_mem, v_kn_mem, v_w_branch_sb, v_w_branch_dsa, v_w_branch_mem, v_w_gate, v_b_gate, v_w_out, v_ffn2_norm, v_ffn2_w1, v_ffn2_w3, v_ffn2_w2):
    given = dict(zip(INPUTS, (x, mem, ffn1_norm, ffn1_w1, ffn1_w3, ffn1_w2, mix_norm, mem_norm, w_in, w_mem_kv, qn_dsa, kn_dsa, qn_mem, kn_mem, w_branch_sb, w_branch_dsa, w_branch_mem, w_gate, b_gate, w_out, ffn2_norm, ffn2_w1, ffn2_w3, ffn2_w2, loss_target, m_ffn1_norm, m_ffn1_w1, m_ffn1_w3, m_ffn1_w2, m_mix_norm, m_mem_norm, m_w_in, m_w_mem_kv, m_qn_dsa, m_kn_dsa, m_qn_mem, m_kn_mem, m_w_branch_sb, m_w_branch_dsa, m_w_branch_mem, m_w_gate, m_b_gate, m_w_out, m_ffn2_norm, m_ffn2_w1, m_ffn2_w3, m_ffn2_w2, v_ffn1_norm, v_ffn1_w1, v_ffn1_w3, v_ffn1_w2, v_mix_norm, v_mem_norm, v_w_in, v_w_mem_kv, v_qn_dsa, v_kn_dsa, v_qn_mem, v_kn_mem, v_w_branch_sb, v_w_branch_dsa, v_w_branch_mem, v_w_gate, v_b_gate, v_w_out, v_ffn2_norm, v_ffn2_w1, v_ffn2_w3, v_ffn2_w2), strict=True))
    wl = {n: given[n][0] for n in BIG}
    ws = {n: given[n] for n in SMALL}

    loss, dx, recv, g = _local_step(x[0], mem[0], loss_target[0], wl, ws)

    big = [{}, {}, {}, {}]
    for tag, names in (("ffn2", G_FFN2), ("mid", G_MID), ("ffn1", G_FFN1)):
        outs = _adamw(recv[tag], _pack_rows(wl, names), _pack_rows({n: given['m_' + n][0] for n in names}, names),
                      _pack_rows({n: given['v_' + n][0] for n in names}, names), name=f"adamw_{tag}")
        for kind, t in enumerate(outs):
            big[kind].update(_unpack_rows(t, wl, names))

    gs = _pack_small(g, SMALL, SMALL_PAD_ROWS)
    loss_row = gs.shape[0] - SMALL_PAD_ROWS
    gs = gs.at[loss_row, 0].set(loss)
    recv_s = _exchange(gs, _direct_phases(False), name="gather_small")
    small = _adamw(recv_s, _pack_small(ws, SMALL, SMALL_PAD_ROWS), _pack_small({n: given['m_' + n] for n in SMALL}, SMALL, SMALL_PAD_ROWS),
                   _pack_small({n: given['v_' + n] for n in SMALL}, SMALL, SMALL_PAD_ROWS), name="adamw_replicated")
    total_loss = small[0][loss_row, 0]
    small = [_unpack_small(t, ws, SMALL) for t in small]

    outs = [total_loss, dx[None]]
    for kind in range(4):
        outs += [big[kind][n][None] if n in wl else small[kind][n] for n in WEIGHTS]
    return tuple(outs)
```

```python
import functools
import math

import jax
import jax.numpy as jnp
from jax import lax
from jax.experimental import pallas as pl
from jax.experimental.pallas import tpu as pltpu

F32 = jnp.float32
BF16 = jnp.bfloat16
MXU_DT = jnp.bfloat16

N_DEV = 8
HEAD_DIM = 64
SB_HEADS = 8
DSA_GROUPS = ((128, 1), (512, 4), (2048, 16))
DSA_HPG = 4
MEM_HEADS = 4
SB_W = SB_HEADS * HEAD_DIM
DSA_W = DSA_HPG * len(DSA_GROUPS) * HEAD_DIM
DSA_OUT_W = DSA_HPG * HEAD_DIM
MEM_W = MEM_HEADS * HEAD_DIM
ROPE_THETA = 10000.0
NORM_EPS = 1e-6
QB = 128
SCALE = HEAD_DIM ** -0.5
ADAM_LR, ADAM_B1, ADAM_B2, ADAM_EPS, ADAM_WD, ADAM_STEP = 0.001, 0.9, 0.999, 1e-08, 0.01, 10

LANES = 128
VMEM_LIMIT = 48 * 1024 * 1024
SB_DEAD = -110.0 * 1.4426950408889634

WEIGHTS = ['ffn1_norm', 'ffn1_w1', 'ffn1_w3', 'ffn1_w2', 'mix_norm', 'mem_norm', 'w_in', 'w_mem_kv', 'qn_dsa', 'kn_dsa',
           'qn_mem', 'kn_mem', 'w_branch_sb', 'w_branch_dsa', 'w_branch_mem', 'w_gate', 'b_gate', 'w_out', 'ffn2_norm',
           'ffn2_w1', 'ffn2_w3', 'ffn2_w2']
SHARD_AXIS = {'ffn1_norm': None, 'ffn1_w1': 1, 'ffn1_w3': 1, 'ffn1_w2': 0, 'mix_norm': None, 'mem_norm': None, 'w_in': 1,
              'w_mem_kv': 0, 'qn_dsa': None, 'kn_dsa': None, 'qn_mem': None, 'kn_mem': None, 'w_branch_sb': 1,
              'w_branch_dsa': 1, 'w_branch_mem': 1, 'w_gate': 1, 'b_gate': None, 'w_out': 0, 'ffn2_norm': None,
              'ffn2_w1': 1, 'ffn2_w3': 1, 'ffn2_w2': 0}
BIG = [n for n in WEIGHTS if SHARD_AXIS[n] is not None]
SMALL = [n for n in WEIGHTS if SHARD_AXIS[n] is None]


def _pcall(kern, **kw):
    return pl.pallas_call(kern, **kw)


def _params(*sem):
    return pltpu.CompilerParams(dimension_semantics=sem, vmem_limit_bytes=VMEM_LIMIT)


def _dot(a, b, dims):
    return lax.dot_general(a.astype(MXU_DT), b.astype(MXU_DT), (dims, ((), ())), preferred_element_type=F32)


def _nn(a, b):
    return _dot(a, b, ((1,), (0,)))


def _nt(a, b):
    return _dot(a, b, ((1,), (1,)))


def _tn(a, b):
    return _dot(a, b, ((0,), (0,)))


def _pick(n, prefs):
    for p in prefs:
        if n % p == 0:
            return p
    return n


def _matmul(a, b, *, name, ta=False, tb=False, out_dtype=F32, res=None, alpha=1.0, tm=1024, tn=512, tk=1024, pair2=None,
            epilogue=None, side=None):
    if ta:
        kdim, m = a.shape
    else:
        m, kdim = a.shape
    n = b.shape[0] if tb else b.shape[1]
    tm = _pick(m, (tm, 512, 256, 128))
    tn = _pick(n, (tn, 512, 384, 256, 128))
    tk = _pick(kdim, (tk, 1024, 512, 256, 128))
    nk = kdim // tk
    a_spec = pl.BlockSpec((tk, tm), lambda i, j, k: (k, i)) if ta else pl.BlockSpec((tm, tk), lambda i, j, k: (i, k))
    b_spec = pl.BlockSpec((tn, tk), lambda i, j, k: (j, k)) if tb else pl.BlockSpec((tk, tn), lambda i, j, k: (k, j))
    o_spec = pl.BlockSpec((tm, tn), lambda i, j, k: (i, j))
    v_spec = pl.BlockSpec((1, tn), lambda i, j, k: (0, j))
    dims = ((0 if ta else 1,), (1 if tb else 0,))
    n_mm = 2 if pair2 is None else 4
    if epilogue is None:
        row_ins, vec_ins = ([] if res is None else [res]), []
        out_dtypes, n_vec = [out_dtype], 0
    else:
        assert tn == n and res is None
        epi_fn, row_ins, vec_ins, out_dtypes, n_vec = epilogue
    n_row_out = len(out_dtypes)

    def kern(*refs):
        refs = list(refs)
        acc_ref = refs.pop() if nk > 1 else None
        mm = refs[:n_mm]
        extra = refs[n_mm:n_mm + len(row_ins) + len(vec_ins)]
        outs = refs[n_mm + len(extra):]
        i = pl.program_id(0)
        k = pl.program_id(2)

        def product():
            part = _dot(mm[0][...], mm[1][...], dims)
            if pair2 is not None:
                part = part + _dot(mm[2][...], mm[3][...], dims)
            return part

        def finish(r):
            if alpha != 1.0:
                r = r * alpha
            if epilogue is None:
                if extra:
                    r = extra[0][...] + r
                outs[0][...] = r.astype(out_dtype)
                return
            vals = epi_fn(r, *[e[...] for e in extra])
            for o_ref, v in zip(outs[:n_row_out], vals[:n_row_out]):
                o_ref[...] = v.astype(o_ref.dtype)
            for o_ref, v in zip(outs[n_row_out:], vals[n_row_out:]):
                @pl.when(i == 0)
                def _():
                    o_ref[...] = jnp.zeros_like(o_ref)

                o_ref[...] += v

        if nk == 1:
            finish(product())
            return

        @pl.when(k == 0)
        def _():
            acc_ref[...] = jnp.zeros_like(acc_ref)

        acc_ref[...] += product()

        @pl.when(k == nk - 1)
        def _():
            finish(acc_ref[...])

    ins = [a, b] + ([] if pair2 is None else list(pair2)) + list(row_ins) + list(vec_ins)
    specs = [a_spec, b_spec] * (n_mm // 2) + [o_spec] * len(row_ins) + [v_spec] * len(vec_ins)
    out_specs = [o_spec] * n_row_out + [v_spec] * n_vec
    out_shape = [jax.ShapeDtypeStruct((m, n), dt) for dt in out_dtypes] + [jax.ShapeDtypeStruct((1, n), F32)] * n_vec
    outs = _call_2d(kern, name=name, grid=(m // tm, n // tn, nk), in_specs=specs, out_specs=out_specs, out_shape=out_shape,
                    ins=ins, scratch_shapes=[pltpu.VMEM((tm, tn), F32)] if nk > 1 else [],
                    semantics=("arbitrary" if n_vec else "parallel", "parallel", "arbitrary"), side=side)
    carried = None
    if side is not None:
        outs, carried = outs
    outs = outs[0] if epilogue is None else outs
    return outs if side is None else (outs, carried)


def _epi_residual_rms(r, res, gain):
    xn = res + r
    return xn, xn * lax.rsqrt(jnp.mean(xn * xn, axis=-1, keepdims=True) + NORM_EPS) * gain


def _epi_rms_bwd(r, x, dres, gain):
    rs = lax.rsqrt(jnp.mean(x * x, axis=-1, keepdims=True) + NORM_EPS)
    xh = x * rs
    dy = r * gain
    dx = dres + rs * (dy - xh * jnp.mean(dy * xh, axis=-1, keepdims=True))
    return dx, dx, jnp.sum(r * xh, axis=0, keepdims=True)


def _epi_rms_bwd_sum(r, r0, x, dres, gain):
    return _epi_rms_bwd(r + r0, x, dres, gain)


def _epi_loss(r, res, target):
    e = (res + r) - target
    dy = e / e.shape[-1]
    return dy, dy, jnp.sum(e * e, axis=0, keepdims=True)
def _rms_fwd(x, g, *, name):
    s, d = x.shape
    ts = _pick(s, (512, 256))

    def kern(x_ref, g_ref, h_ref):
        xf = x_ref[...]
        r = lax.rsqrt(jnp.mean(xf * xf, axis=-1, keepdims=True) + NORM_EPS)
        h_ref[...] = (xf * r * g_ref[...]).astype(h_ref.dtype)

    return _pcall(kern, name=name, grid=(s // ts,),
                  in_specs=[pl.BlockSpec((ts, d), lambda i: (i, 0)), pl.BlockSpec((1, d), lambda i: (0, 0))],
                  out_specs=pl.BlockSpec((ts, d), lambda i: (i, 0)), out_shape=jax.ShapeDtypeStruct((s, d), BF16),
                  compiler_params=_params("parallel"))(x, g)


def _rms_bwd(x, g, dh, res, *, name):
    s, d = x.shape
    ts = _pick(s, (512, 256))

    def kern(*refs):
        if res is None:
            x_ref, g_ref, dh_ref, dx_ref, dxb_ref, dg_ref = refs
            r_ref = None
        else:
            x_ref, g_ref, dh_ref, r_ref, dx_ref, dxb_ref, dg_ref = refs
        xf = x_ref[...]
        r = lax.rsqrt(jnp.mean(xf * xf, axis=-1, keepdims=True) + NORM_EPS)
        xh = xf * r
        dhf = dh_ref[...].astype(F32)
        dy = dhf * g_ref[...]
        dx = r * (dy - xh * jnp.mean(dy * xh, axis=-1, keepdims=True))
        if r_ref is not None:
            dx = r_ref[...] + dx
        dx_ref[...] = dx
        dxb_ref[...] = dx.astype(dxb_ref.dtype)

        @pl.when(pl.program_id(0) == 0)
        def _():
            dg_ref[...] = jnp.zeros_like(dg_ref)

        dg_ref[...] += jnp.sum(dhf * xh, axis=0, keepdims=True)

    row = pl.BlockSpec((ts, d), lambda i: (i, 0))
    vec = pl.BlockSpec((1, d), lambda i: (0, 0))
    ins = [x, g, dh] + ([] if res is None else [res])
    return _pcall(kern, name=name, grid=(s // ts,), in_specs=[row, vec, row] + ([] if res is None else [row]),
                  out_specs=[row, row, vec],
                  out_shape=[jax.ShapeDtypeStruct((s, d), F32), jax.ShapeDtypeStruct((s, d), BF16), jax.ShapeDtypeStruct((1, d), F32)],
                  compiler_params=_params("arbitrary"))(*ins)


def _sigmoid(x):
    return 1.0 / (1.0 + jnp.exp(-x))


FFN_TM, FFN_TF = 512, 1408


def _ffn_up(h, w1, w3, *, name, side=None):
    s, d = h.shape
    fdim = w1.shape[1]
    tm, tf = _pick(s, (FFN_TM, 256)), _pick(fdim, (FFN_TF, 512, 256, 128))

    def kern(h_ref, w1_ref, w3_ref, a_ref, b_ref, f_ref):
        hb = h_ref[...]
        a = _nn(hb, w1_ref[...])
        b = _nn(hb, w3_ref[...])
        a_ref[...] = a.astype(a_ref.dtype)
        b_ref[...] = b.astype(b_ref.dtype)
        f_ref[...] = (a * _sigmoid(a) * b).astype(f_ref.dtype)

    wspec = pl.BlockSpec((d, tf), lambda i, j: (0, j))
    ospec = pl.BlockSpec((tm, tf), lambda i, j: (i, j))
    shp = jax.ShapeDtypeStruct((s, fdim), BF16)
    return _call_2d(kern, name=name, grid=(s // tm, fdim // tf), in_specs=[pl.BlockSpec((tm, d), lambda i, j: (i, 0)), wspec, wspec],
                    out_specs=[ospec, ospec, ospec], out_shape=[shp, shp, shp], ins=[h, w1, w3],
                    semantics=("parallel", "parallel"), side=side)


def _ffn_dact(dy, w2, a, b, *, name, side=None):
    s, d = dy.shape
    fdim = w2.shape[0]
    tm, tf = _pick(s, (FFN_TM, 256)), _pick(fdim, (FFN_TF, 512, 256, 128))

    def kern(dy_ref, w2_ref, a_ref, b_ref, da_ref, db_ref):
        df = _nt(dy_ref[...], w2_ref[...]) * 0.5
        av = a_ref[...].astype(F32)
        sg = _sigmoid(av)
        da_ref[...] = (df * b_ref[...].astype(F32) * (sg + av * sg * (1.0 - sg))).astype(da_ref.dtype)
        db_ref[...] = (df * (av * sg)).astype(db_ref.dtype)

    ospec = pl.BlockSpec((tm, tf), lambda i, j: (i, j))
    shp = jax.ShapeDtypeStruct((s, fdim), BF16)
    return _call_2d(kern, name=name, grid=(s // tm, fdim // tf),
                    in_specs=[pl.BlockSpec((tm, d), lambda i, j: (i, 0)), pl.BlockSpec((tf, d), lambda i, j: (j, 0)), ospec, ospec],
                    out_specs=[ospec, ospec], out_shape=[shp, shp], ins=[dy, w2, a, b], semantics=("parallel", "parallel"), side=side)


def _loss_head(y, t, *, name):
    s, d = y.shape
    ts = _pick(s, (512, 256))
    n = s // ts

    def kern(y_ref, t_ref, dy_ref, dyb_ref, l_ref, acc_ref):
        i = pl.program_id(0)

        @pl.when(i == 0)
        def _():
            acc_ref[...] = jnp.zeros_like(acc_ref)

        e = y_ref[...] - t_ref[...]
        dy_ref[...] = e / d
        dyb_ref[...] = (e / d).astype(dyb_ref.dtype)
        acc_ref[...] += jnp.sum(e * e, axis=0, keepdims=True)

        @pl.when(i == n - 1)
        def _():
            l_ref[...] = jnp.sum(acc_ref[...], axis=1, keepdims=True) * (0.5 / d)

    row = pl.BlockSpec((ts, d), lambda i: (i, 0))
    return _pcall(kern, name=name, grid=(n,), in_specs=[row, row], out_specs=[row, row, pl.BlockSpec((1, 1), lambda i: (0, 0))],
                  out_shape=[jax.ShapeDtypeStruct((s, d), F32), jax.ShapeDtypeStruct((s, d), BF16), jax.ShapeDtypeStruct((1, 1), F32)],
                  scratch_shapes=[pltpu.VMEM((1, d), F32)], compiler_params=_params("arbitrary"))(y, t)


def _head_mean(v, bd):
    hi = v.astype(BF16)
    lo = (v - hi.astype(F32)).astype(BF16)
    return (lax.dot_general(hi, bd, (((1,), (0,)), ((), ())), preferred_element_type=F32)
            + lax.dot_general(lo, bd, (((1,), (0,)), ((), ())), preferred_element_type=F32))


def _partner(v):
    w = v.shape[1]
    lane = lax.broadcasted_iota(jnp.int32, v.shape, 1)
    return jnp.where(lane % HEAD_DIM < HEAD_DIM // 2, pltpu.roll(v, w - HEAD_DIM // 2, 1), pltpu.roll(v, HEAD_DIM // 2, 1))


def _block_diag(w):
    r = lax.broadcasted_iota(jnp.int32, (w, w), 0) // HEAD_DIM
    c = lax.broadcasted_iota(jnp.int32, (w, w), 1) // HEAD_DIM
    return jnp.where(r == c, 1.0 / HEAD_DIM, 0.0).astype(BF16)


def _rope_tables(s):
    half = HEAD_DIM // 2
    inv_freq = jnp.power(ROPE_THETA, -jnp.arange(half, dtype=F32) / half)
    ang = jnp.arange(s).astype(F32)[:, None] * inv_freq[None, :]
    cos, sin = jnp.cos(ang), jnp.sin(ang)
    cos2 = jnp.concatenate([cos, cos, cos, cos], axis=1)
    sin2 = jnp.concatenate([-sin, sin, -sin, sin], axis=1)
    return cos2, sin2


def _qknorm_fwd(src, col0, width, gain, rope, *, name, out_dtype=BF16):
    s = src.shape[0]
    ts = _pick(s, (512, 256))
    cb = col0 // width
    assert col0 % width == 0
    reps = width // LANES
    g = jnp.tile(gain, (1, width // HEAD_DIM))

    def kern(*refs):
        if rope is None:
            x_ref, g_ref, o_ref = refs
        else:
            x_ref, g_ref, c_ref, s_ref, o_ref = refs
        x = x_ref[...].astype(F32)
        bd = _block_diag(width)
        r = lax.rsqrt(_head_mean(x * x, bd) + NORM_EPS)
        y = x * r * g_ref[...]
        if rope is not None:
            y = y * jnp.tile(c_ref[...], (1, reps)) + _partner(y) * jnp.tile(s_ref[...], (1, reps))
        o_ref[...] = y.astype(o_ref.dtype)

    xs = pl.BlockSpec((ts, width), lambda i: (i, cb))
    tab = pl.BlockSpec((ts, LANES), lambda i: (i, 0))
    ins = [src, g] + ([] if rope is None else list(rope))
    specs = [xs, pl.BlockSpec((1, width), lambda i: (0, 0))] + ([] if rope is None else [tab, tab])
    return _pcall(kern, name=name, grid=(s // ts,), in_specs=specs, out_specs=pl.BlockSpec((ts, width), lambda i: (i, 0)),
                  out_shape=jax.ShapeDtypeStruct((s, width), out_dtype), compiler_params=_params("parallel"))(*ins)


def _qknorm_bwd(src, col0, width, gain, rope, dout, *, name):
    s = src.shape[0]
    ts = _pick(s, (512, 256))
    cb = col0 // width
    reps = width // LANES
    g = jnp.tile(gain, (1, width // HEAD_DIM))

    douts = list(dout) if isinstance(dout, (list, tuple)) else [dout]
    piece = width // len(douts)

    def kern(*refs):
        refs = list(refs)
        dg_ref = refs.pop()
        dx_ref = refs.pop()
        do_refs = [refs.pop() for _ in douts][::-1]
        if rope is None:
            x_ref, g_ref = refs
        else:
            x_ref, g_ref, c_ref, s_ref = refs
        x = x_ref[...].astype(F32)
        bd = _block_diag(width)
        r = lax.rsqrt(_head_mean(x * x, bd) + NORM_EPS)
        xh = x * r
        dy = jnp.concatenate([d[...].astype(F32) for d in do_refs], axis=1) if len(do_refs) > 1 else do_refs[0][...].astype(F32)
        if rope is not None:
            dy = dy * jnp.tile(c_ref[...], (1, reps)) + _partner(dy * jnp.tile(s_ref[...], (1, reps)))
        dxh = dy * g_ref[...]
        dx_ref[...] = (r * (dxh - xh * _head_mean(dxh * xh, bd))).astype(dx_ref.dtype)

        @pl.when(pl.program_id(0) == 0)
        def _():
            dg_ref[...] = jnp.zeros_like(dg_ref)

        dg_ref[...] += jnp.sum(dy * xh, axis=0, keepdims=True)

    xs = pl.BlockSpec((ts, width), lambda i: (i, cb))
    row = pl.BlockSpec((ts, width), lambda i: (i, 0))
    vec = pl.BlockSpec((1, width), lambda i: (0, 0))
    tab = pl.BlockSpec((ts, LANES), lambda i: (i, 0))
    ins = [src, g] + ([] if rope is None else list(rope)) + douts
    specs = [xs, vec] + ([] if rope is None else [tab, tab]) + [pl.BlockSpec((ts, piece), lambda i: (i, 0))] * len(douts)
    dx, dg = _pcall(kern, name=name, grid=(s // ts,), in_specs=specs, out_specs=[row, vec],
                    out_shape=[jax.ShapeDtypeStruct((s, width), BF16), jax.ShapeDtypeStruct((1, width), F32)],
                    compiler_params=_params("arbitrary"))(*ins)
    return dx, jnp.sum(dg.reshape(width // HEAD_DIM, HEAD_DIM), axis=0, keepdims=True)


def _tri(strict):
    r = lax.broadcasted_iota(jnp.int32, (2 * QB, QB), 0) % QB
    c = lax.broadcasted_iota(jnp.int32, (2 * QB, QB), 1)
    return jnp.where((r > c) if strict else (r >= c), 1.0, 0.0).astype(BF16)


def _split_dot(v, t2):
    hi = v.astype(BF16)
    lo = (v - hi.astype(F32)).astype(BF16)
    return lax.dot_general(jnp.concatenate([hi, lo], axis=1), t2, (((1,), (0,)), ((), ())), preferred_element_type=F32)


LOG2E = 1.4426950408889634


def _log2_sigmoids(z2):
    lf = -(jnp.maximum(z2, 0.0) + jnp.log2(1.0 + jnp.exp2(-jnp.abs(z2))))
    return z2 + lf, lf


def _key_blocks(t):
    s = t.shape[0]
    n = t.shape[1] // HEAD_DIM
    return t.reshape(s // QB, QB, n, HEAD_DIM).transpose(2, 0, 3, 1)


def _from_key_blocks(t):
    n, nb, hd, qb = t.shape
    return t.transpose(1, 3, 0, 2).reshape(nb * qb, n * hd)


SB_SUB = 4
SB2_SUB = 2


def _first_half(shape):
    return lax.broadcasted_iota(jnp.int32, shape, 1) < HEAD_DIM


def _split_pair(t, first):
    zero = jnp.zeros_like(t)
    return [jnp.where(first, t, zero), jnp.where(first, zero, t)]


def _sb2_fwd(p, *, name, side=None):
    s = p.shape[0]
    rq = SB2_SUB * QB
    nq = s // rq
    npair = SB_W // LANES

    def kern(q_ref, k_ref, v_ref, o_ref):
        i = pl.program_id(1)
        first = _first_half((rq, LANES))
        qs = _split_pair(q_ref[...], first)
        t2 = _tri(True)
        rel = lax.broadcasted_iota(jnp.int32, (rq, QB), 1) - lax.broadcasted_iota(jnp.int32, (rq, QB), 0)

        def tile(j, carries, accs, masked):
            off = pl.multiple_of(j * QB, QB)
            kt = k_ref[pl.ds(off, QB), :]
            vt = v_ref[pl.ds(off, QB), :]
            out_c, out_a = [], []
            for e in range(2):
                ls, lf = _log2_sigmoids(_nt(qs[e], kt) * (SCALE * LOG2E))
                if masked:
                    before = rel < i * rq - j * QB
                    lf = jnp.where(before, lf, 0.0)
                w = jnp.exp2(ls + _split_dot(lf, t2) + carries[e])
                if masked:
                    w = jnp.where(before, w, 0.0)
                out_c.append(carries[e] + jnp.sum(lf, axis=1, keepdims=True))
                out_a.append(accs[e] + _nn(w, vt))
            return out_c, out_a

        carries = [jnp.zeros((rq, 1), F32)] * 2
        accs = [jnp.zeros((rq, LANES), F32)] * 2
        for a in range(SB2_SUB):
            carries, accs = tile(i * SB2_SUB + (SB2_SUB - 1 - a), carries, accs, True)

        def cond(st):
            return jnp.logical_and(st[0] >= 0, st[1] > 0)

        def body(st):
            carries, accs = tile(st[0], [st[2], st[3]], [st[4], st[5]], False)
            alive = jnp.maximum(jnp.max(carries[0]), jnp.max(carries[1])) > SB_DEAD
            return st[0] - 1, alive.astype(jnp.int32), carries[0], carries[1], accs[0], accs[1]

        st = lax.while_loop(cond, body, (i * SB2_SUB - 1, jnp.int32(1), carries[0], carries[1], accs[0], accs[1]))
        o_ref[...] = jnp.where(first, st[4], st[5])

    outs = _call_2d(kern, name=name, grid=(npair, nq),
                    in_specs=[pl.BlockSpec((rq, LANES), lambda a, i: (i, a)), pl.BlockSpec((s, LANES), lambda a, i: (0, npair + a)),
                              pl.BlockSpec((s, LANES), lambda a, i: (0, 2 * npair + a))],
                    out_specs=[pl.BlockSpec((rq, LANES), lambda a, i: (i, a))], out_shape=[jax.ShapeDtypeStruct((s, SB_W), F32)],
                    ins=[p, p, p], semantics=("parallel", "arbitrary"), side=side)
    return outs[0] if side is None else (outs[0][0], outs[1])


def _sb2_bwd(p, o, do, *, name, side=None):
    s = p.shape[0]
    rq = SB2_SUB * QB
    nq = s // rq
    npair = SB_W // LANES

    def kern(q_ref, k_ref, v_ref, o_ref, do_ref, dq_ref, dk_hbm, dv_hbm, dk_acc, dv_acc, sem):
        pr = pl.program_id(0)
        i = pl.program_id(1)

        @pl.when(i == 0)
        def _():
            dk_acc[...] = jnp.zeros_like(dk_acc)
            dv_acc[...] = jnp.zeros_like(dv_acc)

        first = _first_half((rq, LANES))
        qs = _split_pair(q_ref[...], first)
        do2 = do_ref[...]
        dos = _split_pair(do2, first)
        prod = do2.astype(F32) * o_ref[...]
        dsums = [jnp.sum(jnp.where(first, prod, 0.0), axis=1, keepdims=True),
                 jnp.sum(jnp.where(first, 0.0, prod), axis=1, keepdims=True)]
        t_strict = _tri(True)
        t_incl = _tri(False)
        rel = lax.broadcasted_iota(jnp.int32, (rq, QB), 1) - lax.broadcasted_iota(jnp.int32, (rq, QB), 0)

        def tile(j, carries, gcarries, dqs, masked):
            off = pl.multiple_of(j * QB, QB)
            kt = k_ref[pl.ds(off, QB), :]
            vt = v_ref[pl.ds(off, QB), :]
            out_c, out_g, out_q = [], [], []
            dk_t = jnp.zeros((QB, LANES), F32)
            dv_t = jnp.zeros((QB, LANES), F32)
            for e in range(2):
                ls, lf = _log2_sigmoids(_nt(qs[e], kt) * (SCALE * LOG2E))
                if masked:
                    before = rel < i * rq - j * QB
                    lf = jnp.where(before, lf, 0.0)
                w = jnp.exp2(ls + _split_dot(lf, t_strict) + carries[e])
                if masked:
                    w = jnp.where(before, w, 0.0)
                wr = w.astype(MXU_DT)
                g = _nt(dos[e], vt) * wr.astype(F32)
                big_g = dsums[e] - (_split_dot(g, t_incl) + gcarries[e])
                sig = jnp.exp2(ls)
                dz = g * (1.0 - sig) - sig * big_g
                if masked:
                    dz = jnp.where(before, dz, 0.0)
                dz = dz * SCALE
                dk_t = dk_t + _tn(dz, qs[e])
                dv_t = dv_t + _tn(wr, dos[e])
                out_c.append(carries[e] + jnp.sum(lf, axis=1, keepdims=True))
                out_g.append(gcarries[e] + jnp.sum(g, axis=1, keepdims=True))
                out_q.append(dqs[e] + _nn(dz, kt))
            dk_acc[pl.ds(off, QB), :] += dk_t
            dv_acc[pl.ds(off, QB), :] += dv_t
            return out_c, out_g, out_q

        zc = [jnp.zeros((rq, 1), F32)] * 2
        carries, gcarries, dqs = zc, zc, [jnp.zeros((rq, LANES), F32)] * 2
        for a in range(SB2_SUB):
            carries, gcarries, dqs = tile(i * SB2_SUB + (SB2_SUB - 1 - a), carries, gcarries, dqs, True)

        def cond(st):
            return jnp.logical_and(st[0] >= 0, st[1] > 0)

        def body(st):
            carries, gcarries, dqs = tile(st[0], [st[2], st[3]], [st[4], st[5]], [st[6], st[7]], False)
            alive = jnp.maximum(jnp.max(carries[0]), jnp.max(carries[1])) > SB_DEAD
            return (st[0] - 1, alive.astype(jnp.int32), carries[0], carries[1], gcarries[0], gcarries[1], dqs[0], dqs[1])

        st = lax.while_loop(cond, body, (i * SB2_SUB - 1, jnp.int32(1), carries[0], carries[1], gcarries[0], gcarries[1],
                                         dqs[0], dqs[1]))
        dq_ref[...] = jnp.where(first, st[6], st[7])

        @pl.when(i == nq - 1)
        def _():
            cols = pl.ds(pl.multiple_of(pr * LANES, LANES), LANES)
            ck = pltpu.make_async_copy(dk_acc, dk_hbm.at[:, cols], sem.at[0])
            cv = pltpu.make_async_copy(dv_acc, dv_hbm.at[:, cols], sem.at[1])
            ck.start()
            cv.start()
            ck.wait()
            cv.wait()

    blk = pl.BlockSpec((rq, LANES), lambda a, i: (i, a))
    anyspace = pl.BlockSpec(memory_space=pl.ANY)
    shp = jax.ShapeDtypeStruct((s, SB_W), F32)
    return _call_2d(kern, name=name, grid=(npair, nq),
                    in_specs=[blk, pl.BlockSpec((s, LANES), lambda a, i: (0, npair + a)),
                              pl.BlockSpec((s, LANES), lambda a, i: (0, 2 * npair + a)), blk, blk],
                    out_specs=[blk, anyspace, anyspace], out_shape=[shp, shp, shp], ins=[p, p, p, o, do],
                    scratch_shapes=[pltpu.VMEM((s, LANES), F32), pltpu.VMEM((s, LANES), F32), pltpu.SemaphoreType.DMA((2,))],
                    semantics=("arbitrary", "arbitrary"), side=side)


def _sb_fwd(q, kt, vt, *, name):
    h, s, hd = q.shape
    rq = SB_SUB * QB
    nq = s // rq
    nb = s // QB

    def kern(q_ref, k_ref, v_ref, o_ref):
        i = pl.program_id(1)
        qb = q_ref[...]
        t2 = _tri(True)
        rel = lax.broadcasted_iota(jnp.int32, (rq, QB), 1) - lax.broadcasted_iota(jnp.int32, (rq, QB), 0)

        def tile(j, carry, acc, masked):
            ls, lf = _log2_sigmoids(_nn(qb, k_ref[j]) * (SCALE * LOG2E))
            if masked:
                before = rel < i * rq - j * QB
                lf = jnp.where(before, lf, 0.0)
            w = jnp.exp2(ls + _split_dot(lf, t2) + carry)
            if masked:
                w = jnp.where(before, w, 0.0)
            return carry + jnp.sum(lf, axis=1, keepdims=True), acc + _nt(w, v_ref[j])

        carry, acc = jnp.zeros((rq, 1), F32), jnp.zeros((rq, hd), F32)
        for a in range(SB_SUB):
            carry, acc = tile(i * SB_SUB + (SB_SUB - 1 - a), carry, acc, True)

        def cond(st):
            return jnp.logical_and(st[0] >= 0, st[1] > 0)

        def body(st):
            j, _, carry, acc = st
            carry, acc = tile(j, carry, acc, False)
            return j - 1, (jnp.max(carry) > SB_DEAD).astype(jnp.int32), carry, acc

        _, _, _, acc = lax.while_loop(cond, body, (i * SB_SUB - 1, jnp.int32(1), carry, acc))
        o_ref[...] = acc

    blk = pl.BlockSpec((None, rq, hd), lambda a, i: (a, i, 0))
    full = pl.BlockSpec((None, nb, hd, QB), lambda a, i: (a, 0, 0, 0))
    return _pcall(kern, name=name, grid=(h, nq), in_specs=[blk, full, full], out_specs=blk,
                  out_shape=jax.ShapeDtypeStruct((h, s, hd), F32), compiler_params=_params("parallel", "arbitrary"))(q, kt, vt)


def _sb_bwd(q, kt, vt, o, do, *, name):
    h, s, hd = q.shape
    rq = SB_SUB * QB
    nq = s // rq
    nb = s // QB

    def kern(q_ref, k_ref, v_ref, o_ref, do_ref, dq_ref, dk_ref, dv_ref):
        i = pl.program_id(1)

        @pl.when(i == 0)
        def _():
            dk_ref[...] = jnp.zeros_like(dk_ref)
            dv_ref[...] = jnp.zeros_like(dv_ref)

        qb = q_ref[...]
        dob = do_ref[...]
        dsum = jnp.sum(dob.astype(F32) * o_ref[...], axis=1, keepdims=True)
        t_strict = _tri(True)
        t_incl = _tri(False)
        rel = lax.broadcasted_iota(jnp.int32, (rq, QB), 1) - lax.broadcasted_iota(jnp.int32, (rq, QB), 0)

        def tile(j, carry, gcarry, dq, masked):
            kb = k_ref[j]
            ls, lf = _log2_sigmoids(_nn(qb, kb) * (SCALE * LOG2E))
            if masked:
                before = rel < i * rq - j * QB
                lf = jnp.where(before, lf, 0.0)
            w = jnp.exp2(ls + _split_dot(lf, t_strict) + carry)
            if masked:
                w = jnp.where(before, w, 0.0)
            wr = w.astype(MXU_DT)
            g = _nn(dob, v_ref[j]) * wr.astype(F32)
            big_g = dsum - (_split_dot(g, t_incl) + gcarry)
            sig = jnp.exp2(ls)
            dz = g * (1.0 - sig) - sig * big_g
            if masked:
                dz = jnp.where(before, dz, 0.0)
            dz = dz * SCALE
            dk_ref[j] += _tn(qb, dz)
            dv_ref[j] += _tn(dob, wr)
            return (carry + jnp.sum(lf, axis=1, keepdims=True), gcarry + jnp.sum(g, axis=1, keepdims=True),
                    dq + _nt(dz, kb))

        carry, gcarry, dq = jnp.zeros((rq, 1), F32), jnp.zeros((rq, 1), F32), jnp.zeros((rq, hd), F32)
        for a in range(SB_SUB):
            carry, gcarry, dq = tile(i * SB_SUB + (SB_SUB - 1 - a), carry, gcarry, dq, True)

        def cond(st):
            return jnp.logical_and(st[0] >= 0, st[1] > 0)

        def body(st):
            j, _, carry, gcarry, dq = st
            carry, gcarry, dq = tile(j, carry, gcarry, dq, False)
            return j - 1, (jnp.max(carry) > SB_DEAD).astype(jnp.int32), carry, gcarry, dq

        st = lax.while_loop(cond, body, (i * SB_SUB - 1, jnp.int32(1), carry, gcarry, dq))
        dq_ref[...] = st[4]

    blk = pl.BlockSpec((None, rq, hd), lambda a, i: (a, i, 0))
    full = pl.BlockSpec((None, nb, hd, QB), lambda a, i: (a, 0, 0, 0))
    kshape = jax.ShapeDtypeStruct((h, nb, hd, QB), F32)
    return _pcall(kern, name=name, grid=(h, nq), in_specs=[blk, full, full, blk, blk], out_specs=[blk, full, full],
                  out_shape=[jax.ShapeDtypeStruct((h, s, hd), F32), kshape, kshape],
                  compiler_params=_params("parallel", "arbitrary"))(q, kt, vt, o, do)


DSA_SUB = 4


def _dsa_seq_blocks(t, s):
    steps_per_group = 4 * s // (QB * DSA_SUB)
    g = t // steps_per_group
    b0, b1, b2 = (s // (QB * r) for _, r in DSA_GROUPS)
    return jnp.where(g == 0, b0, jnp.where(g == 1, b1, b2))


def _dsa_rel():
    qi = lax.broadcasted_iota(jnp.int32, (QB, QB), 0)
    kj = lax.broadcasted_iota(jnp.int32, (QB, QB), 1)
    return kj - qi


def _prev_mask(rel, has_prev):
    return rel >= jnp.where(has_prev, 0, QB)


def _dsa_fwd(q, k, vp, *, name):
    rows = q.shape[0]
    s = rows // 12
    big = QB * DSA_SUB
    nsteps = rows // big

    def kern(q_ref, k_ref, kp_ref, v_ref, vpv_ref, o_ref):
        t = pl.program_id(0)
        bps = _dsa_seq_blocks(t, s)
        rel = _dsa_rel()
        lane = lax.broadcasted_iota(jnp.int32, (QB, LANES), 1)
        for a in range(DSA_SUB):
            qa = q_ref[pl.ds(a * QB, QB), :]
            kc = k_ref[pl.ds(a * QB, QB), :]
            vc = v_ref[pl.ds(a * QB, QB), :]
            if a == 0:
                kpv, vpv = kp_ref[...], vpv_ref[...]
            else:
                kpv, vpv = k_ref[pl.ds((a - 1) * QB, QB), :], v_ref[pl.ds((a - 1) * QB, QB), :]
            has_prev = (t * DSA_SUB + a) % bps != 0
            sc = jnp.where(rel <= 0, _nt(qa, kc) * SCALE, -jnp.inf)
            sp = jnp.where(_prev_mask(rel, has_prev), _nt(qa, kpv) * SCALE, -jnp.inf)
            m = jnp.maximum(jnp.max(sc, axis=1, keepdims=True), jnp.max(sp, axis=1, keepdims=True))
            pc = jnp.exp(sc - m)
            pp = jnp.exp(sp - m)
            den = jnp.sum(pc, axis=1, keepdims=True) + jnp.sum(pp, axis=1, keepdims=True)
            o = (_nn(pc, vc) + _nn(pp, vpv)) / den
            o_ref[pl.ds(a * QB, QB), :] = jnp.where(lane < HEAD_DIM, o, m + jnp.log(den))

    cur64 = pl.BlockSpec((big, HEAD_DIM), lambda t: (t, 0))
    prev64 = pl.BlockSpec((QB, HEAD_DIM), lambda t: (jnp.maximum(t * DSA_SUB - 1, 0), 0))
    cur128 = pl.BlockSpec((big, LANES), lambda t: (t, 0))
    prev128 = pl.BlockSpec((QB, LANES), lambda t: (jnp.maximum(t * DSA_SUB - 1, 0), 0))
    return _pcall(kern, name=name, grid=(nsteps,), in_specs=[cur64, cur64, prev64, cur128, prev128], out_specs=cur128,
                  out_shape=jax.ShapeDtypeStruct((rows, LANES), F32), compiler_params=_params("parallel"))(q, k, k, vp, vp)


def _dsa_combine(p0, p1, p2, *, name):
    hh, s, _ = p0.shape
    ts = _pick(s, (512, 256))

    def kern(a_ref, b_ref, c_ref, o_ref):
        lane = lax.broadcasted_iota(jnp.int32, (ts, LANES), 1)
        xs = [a_ref[...], b_ref[...], c_ref[...]]
        ls = [jnp.where(lane < HEAD_DIM, pltpu.roll(x, HEAD_DIM, 1), x) for x in xs]
        m = jnp.maximum(jnp.maximum(ls[0], ls[1]), ls[2])
        es = [jnp.exp(l - m) for l in ls]
        den = es[0] + es[1] + es[2]
        o = (es[0] * xs[0] + es[1] * xs[1] + es[2] * xs[2]) / den
        o_ref[...] = jnp.where(lane < HEAD_DIM, o, m + jnp.log(den))

    blk = pl.BlockSpec((None, ts, LANES), lambda a, i: (a, i, 0))
    return _pcall(kern, name=name, grid=(hh, s // ts), in_specs=[blk, blk, blk], out_specs=blk,
                  out_shape=jax.ShapeDtypeStruct((hh, s, LANES), F32), compiler_params=_params("parallel", "parallel"))(p0, p1, p2)


def _dsa_bwd_prep(comb, dop, *, name):
    hh, s, _ = comb.shape
    ts = _pick(s, (512, 256))

    def kern(c_ref, d_ref, o_ref):
        lane = lax.broadcasted_iota(jnp.int32, (ts, LANES), 1)
        c = c_ref[...]
        d = d_ref[...]
        dsum = jnp.sum(jnp.where(lane < HEAD_DIM, c * d, 0.0), axis=1, keepdims=True)
        o_ref[...] = jnp.where(lane < HEAD_DIM, d, jnp.where(lane < HEAD_DIM + 32, c, dsum))

    blk = pl.BlockSpec((None, ts, LANES), lambda a, i: (a, i, 0))
    return _pcall(kern, name=name, grid=(hh, s // ts), in_specs=[blk, blk], out_specs=blk,
                  out_shape=jax.ShapeDtypeStruct((hh, s, LANES), F32), compiler_params=_params("parallel", "parallel"))(comb, dop)


def _dsa_bwd(q, k, vp, pk, *, name):
    rows = q.shape[0]
    s = rows // 12
    big = QB * DSA_SUB
    nsteps = rows // big
    nblk = rows // QB

    def kern(q_ref, qn_ref, k_ref, kp_ref, v_ref, vpv_ref, p_ref, pn_ref, dq_ref, dk_ref, dv_ref):
        t = pl.program_id(0)
        bps = _dsa_seq_blocks(t, s)
        rel = _dsa_rel()
        lane = lax.broadcasted_iota(jnp.int32, (QB, LANES), 1)

        def stats(pa):
            lse = jnp.max(jnp.where(jnp.logical_and(lane >= HEAD_DIM, lane < HEAD_DIM + 32), pa, -jnp.inf), axis=1, keepdims=True)
            dsum = jnp.max(jnp.where(lane >= HEAD_DIM + 32, pa, -jnp.inf), axis=1, keepdims=True)
            return lse, dsum

        def pair(qa, pa, st, kb, vb, mask):
            p = jnp.where(mask, jnp.exp(_nt(qa, kb) * SCALE - st[0]), 0.0)
            ds = p * (_nt(pa, vb) - st[1]) * SCALE
            return _nn(ds, kb), _tn(ds, qa), _tn(p, pa)

        for a in range(DSA_SUB):
            qa = q_ref[pl.ds(a * QB, QB), :]
            pa = p_ref[pl.ds(a * QB, QB), :]
            st = stats(pa)
            kc = k_ref[pl.ds(a * QB, QB), :]
            vc = v_ref[pl.ds(a * QB, QB), :]
            if a == 0:
                kpv, vpv = kp_ref[...], vpv_ref[...]
            else:
                kpv, vpv = k_ref[pl.ds((a - 1) * QB, QB), :], v_ref[pl.ds((a - 1) * QB, QB), :]
            has_prev = (t * DSA_SUB + a) % bps != 0
            dq_c, dk_c, dv_c = pair(qa, pa, st, kc, vc, rel <= 0)
            dq_p, dk_p, dv_p = pair(qa, pa, st, kpv, vpv, _prev_mask(rel, has_prev))
            dq_ref[pl.ds(a * QB, QB), :] = dq_c + dq_p
            if a == 0:
                dk_ref[pl.ds(0, QB), :] = dk_c
                dv_ref[pl.ds(0, QB), :] = dv_c
            else:
                dk_ref[pl.ds(a * QB, QB), :] = dk_c
                dv_ref[pl.ds(a * QB, QB), :] = dv_c
                dk_ref[pl.ds((a - 1) * QB, QB), :] += dk_p
                dv_ref[pl.ds((a - 1) * QB, QB), :] += dv_p
        nxt = t * DSA_SUB + DSA_SUB
        has_next = jnp.logical_and(nxt < nblk, nxt % bps != 0)
        last = (DSA_SUB - 1) * QB
        pn = pn_ref[...]
        _, dk_n, dv_n = pair(qn_ref[...], pn, stats(pn), k_ref[pl.ds(last, QB), :], v_ref[pl.ds(last, QB), :],
                             _prev_mask(rel, has_next))
        dk_ref[pl.ds(last, QB), :] += dk_n
        dv_ref[pl.ds(last, QB), :] += dv_n

    def prev_map(t):
        return (jnp.maximum(t * DSA_SUB - 1, 0), 0)

    def next_map(t):
        return (jnp.minimum(t * DSA_SUB + DSA_SUB, nblk - 1), 0)

    cur64 = pl.BlockSpec((big, HEAD_DIM), lambda t: (t, 0))
    cur128 = pl.BlockSpec((big, LANES), lambda t: (t, 0))
    specs = [cur64, pl.BlockSpec((QB, HEAD_DIM), next_map), cur64, pl.BlockSpec((QB, HEAD_DIM), prev_map),
             cur128, pl.BlockSpec((QB, LANES), prev_map), cur128, pl.BlockSpec((QB, LANES), next_map)]
    return _pcall(kern, name=name, grid=(nsteps,), in_specs=specs, out_specs=[cur64, cur64, cur128],
                  out_shape=[jax.ShapeDtypeStruct((rows, HEAD_DIM), F32), jax.ShapeDtypeStruct((rows, HEAD_DIM), F32),
                             jax.ShapeDtypeStruct((rows, LANES), F32)],
                  compiler_params=_params("parallel"))(q, q, k, k, vp, vp, pk, pk)


def _mem_fwd(q, km, vm, *, name):
    hh, s, hd = q.shape
    ml = km.shape[1]
    tq = _pick(s, (512, 256))

    def kern(q_ref, k_ref, v_ref, o_ref):
        sc = _nt(q_ref[...], k_ref[...]) * SCALE
        e = jnp.exp(sc - jnp.max(sc, axis=1, keepdims=True))
        p = e / jnp.sum(e, axis=1, keepdims=True)
        o_ref[...] = _nn(p, v_ref[...])

    blk = pl.BlockSpec((None, tq, hd), lambda a, i: (a, i, 0))
    kv = pl.BlockSpec((None, ml, hd), lambda a, i: (a, 0, 0))
    return _pcall(kern, name=name, grid=(hh, s // tq), in_specs=[blk, kv, kv], out_specs=blk,
                  out_shape=jax.ShapeDtypeStruct((hh, s, hd), F32), compiler_params=_params("parallel", "parallel"))(q, km, vm)


def _mem_bwd(q, km, vm, do, *, name):
    hh, s, hd = q.shape
    ml = km.shape[1]
    tq = _pick(s, (512, 256))

    def kern(q_ref, k_ref, v_ref, do_ref, dq_ref, dk_ref, dv_ref):
        @pl.when(pl.program_id(1) == 0)
        def _():
            dk_ref[...] = jnp.zeros_like(dk_ref)
            dv_ref[...] = jnp.zeros_like(dv_ref)

        qb = q_ref[...]
        dob = do_ref[...]
        sc = _nt(qb, k_ref[...]) * SCALE
        e = jnp.exp(sc - jnp.max(sc, axis=1, keepdims=True))
        p = e / jnp.sum(e, axis=1, keepdims=True)
        dp = _nt(dob, v_ref[...])
        ds = p * (dp - jnp.sum(p * dp, axis=1, keepdims=True)) * SCALE
        dq_ref[...] = _nn(ds, k_ref[...])
        dk_ref[...] += _tn(ds, qb)
        dv_ref[...] += _tn(p, dob)

    blk = pl.BlockSpec((None, tq, hd), lambda a, i: (a, i, 0))
    kv = pl.BlockSpec((None, ml, hd), lambda a, i: (a, 0, 0))
    kvs = jax.ShapeDtypeStruct((hh, ml, hd), F32)
    return _pcall(kern, name=name, grid=(hh, s // tq), in_specs=[blk, kv, kv, blk], out_specs=[blk, kv, kv],
                  out_shape=[jax.ShapeDtypeStruct((hh, s, hd), F32), kvs, kvs],
                  compiler_params=_params("parallel", "arbitrary"))(q, km, vm, do)


DSA_BT = QB * max(r for _, r in DSA_GROUPS)


def _unit_rows(r, c, b):
    return pl.ds(c + QB * r * b, QB, stride=r)


def _pair_cols(t, first):
    return [jnp.max(jnp.where(first, t, -jnp.inf), axis=1, keepdims=True),
            jnp.max(jnp.where(first, -jnp.inf, t), axis=1, keepdims=True)]


def _dsa2_fwd(qn, kn, v32, g, *, name):
    s = qn.shape[0]
    r = DSA_GROUPS[g][1]
    nbk = DSA_BT // (QB * r)
    npair = DSA_OUT_W // LANES

    def kern(q_ref, k_ref, kp_ref, v_ref, vp_ref, o_ref, l_ref):
        t = pl.program_id(1)
        first = _first_half((QB, LANES))
        rel = _dsa_rel()
        for c in range(r):
            for b in range(nbk):
                rows = _unit_rows(r, c, b)
                kc, vc = k_ref[rows, :], v_ref[rows, :]
                if b > 0:
                    prow = _unit_rows(r, c, b - 1)
                    kpv, vpv, has_prev = k_ref[prow, :], v_ref[prow, :], True
                else:
                    prow = _unit_rows(r, c, nbk - 1)
                    kpv, vpv, has_prev = kp_ref[prow, :], vp_ref[prow, :], t > 0
                outs, lses = [], []
                for qe in _split_pair(q_ref[rows, :], first):
                    sc = jnp.where(rel <= 0, _nt(qe, kc) * SCALE, -jnp.inf)
                    sp = jnp.where(_prev_mask(rel, has_prev), _nt(qe, kpv) * SCALE, -jnp.inf)
                    m = jnp.maximum(jnp.max(sc, axis=1, keepdims=True), jnp.max(sp, axis=1, keepdims=True))
                    pc = jnp.exp(sc - m)
                    pp = jnp.exp(sp - m)
                    den = jnp.sum(pc, axis=1, keepdims=True) + jnp.sum(pp, axis=1, keepdims=True)
                    outs.append((_nn(pc, vc) + _nn(pp, vpv)) / den)
                    lses.append(m + jnp.log(den))
                o_ref[rows, :] = jnp.where(first, outs[0], outs[1])
                l_ref[rows, :] = jnp.where(first, lses[0], lses[1])

    npg = DSA_HPG * HEAD_DIM // LANES
    cur = pl.BlockSpec((DSA_BT, LANES), lambda a, t: (t, npg * g + a))
    prev = pl.BlockSpec((DSA_BT, LANES), lambda a, t: (jnp.maximum(t - 1, 0), npg * g + a))
    out = pl.BlockSpec((DSA_BT, LANES), lambda a, t: (t, a))
    shp = jax.ShapeDtypeStruct((s, DSA_OUT_W), F32)
    return _pcall(kern, name=name, grid=(npair, s // DSA_BT), in_specs=[cur, cur, prev, cur, prev], out_specs=[out, out],
                  out_shape=[shp, shp], compiler_params=_params("parallel", "parallel"))(qn, kn, kn, v32, v32)


def _dsa2_combine(parts, *, name):
    s, wd = parts[0][0].shape
    ts = _pick(s, (512, 256))

    def kern(o0, l0, o1, l1, o2, l2, o_ref, l_ref):
        ls = [l0[...], l1[...], l2[...]]
        m = jnp.maximum(jnp.maximum(ls[0], ls[1]), ls[2])
        es = [jnp.exp(l - m) for l in ls]
        den = es[0] + es[1] + es[2]
        o_ref[...] = (es[0] * o0[...] + es[1] * o1[...] + es[2] * o2[...]) / den
        l_ref[...] = m + jnp.log(den)

    blk = pl.BlockSpec((ts, wd), lambda i: (i, 0))
    shp = jax.ShapeDtypeStruct((s, wd), F32)
    flat = [t for pair in parts for t in pair]
    return _pcall(kern, name=name, grid=(s // ts,), in_specs=[blk] * 6, out_specs=[blk, blk], out_shape=[shp, shp],
                  compiler_params=_params("parallel"))(*flat)


def _dsa2_prep(o, do, *, name):
    s, wd = o.shape
    ts = _pick(s, (512, 256))

    def kern(o_ref, do_ref, d_ref):
        d_ref[...] = _head_mean(do_ref[...] * o_ref[...], _block_diag(wd)) * HEAD_DIM

    blk = pl.BlockSpec((ts, wd), lambda i: (i, 0))
    return _pcall(kern, name=name, grid=(s // ts,), in_specs=[blk, blk], out_specs=blk,
                  out_shape=jax.ShapeDtypeStruct((s, wd), F32), compiler_params=_params("parallel"))(o, do)


def _dsa2_bwd(qn, kn, v32, do, lse, dd, g, *, name):
    s = qn.shape[0]
    r = DSA_GROUPS[g][1]
    nbk = DSA_BT // (QB * r)
    npair = DSA_OUT_W // LANES
    nsteps = s // DSA_BT

    def kern(q_ref, qn_ref, k_ref, kp_ref, v_ref, vp_ref, do_ref, don_ref, l_ref, ln_ref, d_ref, dn_ref,
             dq_ref, dk_ref, dv_ref):
        t = pl.program_id(1)
        first = _first_half((QB, LANES))
        rel = _dsa_rel()

        def pair(qs, dos, lcols, dcols, kb, vb, mask):
            dqs = []
            dk = jnp.zeros((QB, LANES), F32)
            dv = jnp.zeros((QB, LANES), F32)
            for e in range(2):
                p = jnp.where(mask, jnp.exp(_nt(qs[e], kb) * SCALE - lcols[e]), 0.0)
                ds = p * (_nt(dos[e], vb) - dcols[e]) * SCALE
                dqs.append(_nn(ds, kb))
                dk = dk + _tn(ds, qs[e])
                dv = dv + _tn(p, dos[e])
            return dqs, dk, dv

        def load(rows, qr, dor, lr, dr):
            return (_split_pair(qr[rows, :], first), _split_pair(dor[rows, :], first), _pair_cols(lr[rows, :], first),
                    _pair_cols(dr[rows, :], first))

        for c in range(r):
            for b in range(nbk):
                rows = _unit_rows(r, c, b)
                qs, dos, lcols, dcols = load(rows, q_ref, do_ref, l_ref, d_ref)
                dq_c, dk_c, dv_c = pair(qs, dos, lcols, dcols, k_ref[rows, :], v_ref[rows, :], rel <= 0)
                if b > 0:
                    prow = _unit_rows(r, c, b - 1)
                    dq_p, dk_p, dv_p = pair(qs, dos, lcols, dcols, k_ref[prow, :], v_ref[prow, :], _prev_mask(rel, True))
                    dk_ref[prow, :] += dk_p
                    dv_ref[prow, :] += dv_p
                else:
                    prow = _unit_rows(r, c, nbk - 1)
                    dq_p, _, _ = pair(qs, dos, lcols, dcols, kp_ref[prow, :], vp_ref[prow, :], _prev_mask(rel, t > 0))
                dq_ref[rows, :] = jnp.where(first, dq_c[0] + dq_p[0], dq_c[1] + dq_p[1])
                dk_ref[rows, :] = dk_c
                dv_ref[rows, :] = dv_c
            last = _unit_rows(r, c, nbk - 1)
            nqs, ndos, nl, nd = load(_unit_rows(r, c, 0), qn_ref, don_ref, ln_ref, dn_ref)
            _, dk_n, dv_n = pair(nqs, ndos, nl, nd, k_ref[last, :], v_ref[last, :], _prev_mask(rel, t < nsteps - 1))
            dk_ref[last, :] += dk_n
            dv_ref[last, :] += dv_n

    npg = DSA_HPG * HEAD_DIM // LANES

    def at(shift, col):
        return pl.BlockSpec((DSA_BT, LANES), lambda a, t: (jnp.clip(t + shift, 0, nsteps - 1), col(a)))

    gcol = lambda a: npg * g + a
    ocol = lambda a: a
    specs = [at(0, gcol), at(1, gcol), at(0, gcol), at(-1, gcol), at(0, gcol), at(-1, gcol),
             at(0, ocol), at(1, ocol), at(0, ocol), at(1, ocol), at(0, ocol), at(1, ocol)]
    shp = jax.ShapeDtypeStruct((s, DSA_OUT_W), F32)
    return _pcall(kern, name=name, grid=(npair, nsteps), in_specs=specs, out_specs=[at(0, ocol)] * 3, out_shape=[shp, shp, shp],
                  compiler_params=_params("parallel", "parallel"))(qn, qn, kn, kn, v32, v32, do, do, lse, lse, dd, dd)


def _mem2_fwd(qn, km, kv, *, name):
    s = qn.shape[0]
    ml = km.shape[0]
    tq = _pick(s, (512, 256))
    npair = MEM_W // LANES

    def kern(q_ref, k_ref, v_ref, o_ref):
        first = _first_half((tq, LANES))
        outs = []
        for qe in _split_pair(q_ref[...], first):
            sc = _nt(qe, k_ref[...]) * SCALE
            e = jnp.exp(sc - jnp.max(sc, axis=1, keepdims=True))
            outs.append(_nn(e / jnp.sum(e, axis=1, keepdims=True), v_ref[...]))
        o_ref[...] = jnp.where(first, outs[0], outs[1])

    blk = pl.BlockSpec((tq, LANES), lambda a, i: (i, a))
    return _pcall(kern, name=name, grid=(npair, s // tq),
                  in_specs=[blk, pl.BlockSpec((ml, LANES), lambda a, i: (0, a)), pl.BlockSpec((ml, LANES), lambda a, i: (0, npair + a))],
                  out_specs=blk, out_shape=jax.ShapeDtypeStruct((s, MEM_W), F32),
                  compiler_params=_params("parallel", "parallel"))(qn, km, kv)


def _mem2_bwd(qn, km, kv, do, *, name):
    s = qn.shape[0]
    ml = km.shape[0]
    tq = _pick(s, (512, 256))
    npair = MEM_W // LANES

    def kern(q_ref, k_ref, v_ref, do_ref, dq_ref, dk_ref, dv_ref):
        @pl.when(pl.program_id(1) == 0)
        def _():
            dk_ref[...] = jnp.zeros_like(dk_ref)
            dv_ref[...] = jnp.zeros_like(dv_ref)

        first = _first_half((tq, LANES))
        dqs = []
        for qe, doe in zip(_split_pair(q_ref[...], first), _split_pair(do_ref[...], first)):
            sc = _nt(qe, k_ref[...]) * SCALE
            e = jnp.exp(sc - jnp.max(sc, axis=1, keepdims=True))
            p = e / jnp.sum(e, axis=1, keepdims=True)
            dp = _nt(doe, v_ref[...])
            ds = p * (dp - jnp.sum(p * dp, axis=1, keepdims=True)) * SCALE
            dqs.append(_nn(ds, k_ref[...]))
            dk_ref[...] += _tn(ds, qe)
            dv_ref[...] += _tn(p, doe)
        dq_ref[...] = jnp.where(first, dqs[0], dqs[1])

    blk = pl.BlockSpec((tq, LANES), lambda a, i: (i, a))
    kblk = pl.BlockSpec((ml, LANES), lambda a, i: (0, a))
    kshape = jax.ShapeDtypeStruct((ml, MEM_W), F32)
    return _pcall(kern, name=name, grid=(npair, s // tq),
                  in_specs=[blk, kblk, pl.BlockSpec((ml, LANES), lambda a, i: (0, npair + a)), blk],
                  out_specs=[blk, kblk, kblk], out_shape=[jax.ShapeDtypeStruct((s, MEM_W), F32), kshape, kshape],
                  compiler_params=_params("parallel", "arbitrary"))(qn, km, kv, do)


def _merge_fwd(logits, bias, ya, yb, yc, *, name):
    s, d = ya.shape
    ts = _pick(s, (512, 256))

    def kern(l0, l1, l2, b0, b1, b2, a_ref, b_ref, c_ref, o_ref):
        m = (_sigmoid(l0[...] + b0[...]) * a_ref[...] + _sigmoid(l1[...] + b1[...]) * b_ref[...]
             + _sigmoid(l2[...] + b2[...]) * c_ref[...])
        o_ref[...] = m.astype(o_ref.dtype)

    row = pl.BlockSpec((ts, d), lambda i: (i, 0))
    lg = [pl.BlockSpec((ts, d), functools.partial(lambda i, c: (i, c), c=c)) for c in range(3)]
    bs = [pl.BlockSpec((1, d), functools.partial(lambda i, c: (0, c), c=c)) for c in range(3)]
    return _pcall(kern, name=name, grid=(s // ts,), in_specs=lg + bs + [row, row, row], out_specs=row,
                  out_shape=jax.ShapeDtypeStruct((s, d), BF16),
                  compiler_params=_params("parallel"))(logits, logits, logits, bias, bias, bias, ya, yb, yc)


def _merge_bwd(logits, bias, ya, yb, yc, dm, *, name):
    s, d = ya.shape
    ts = _pick(s, (256,))

    def kern(l0, l1, l2, b0, b1, b2, a_ref, b_ref, c_ref, dm_ref, da_ref, db_ref, dc_ref, dl0, dl1, dl2, dbias0, dbias1, dbias2):
        first = pl.program_id(0) == 0
        dmv = dm_ref[...]
        for l_ref, bb_ref, y_ref, dy_ref, dl_ref, dbias_ref in ((l0, b0, a_ref, da_ref, dl0, dbias0), (l1, b1, b_ref, db_ref, dl1, dbias1),
                                                                (l2, b2, c_ref, dc_ref, dl2, dbias2)):
            g = _sigmoid(l_ref[...] + bb_ref[...])
            dy_ref[...] = (dmv * g).astype(dy_ref.dtype)
            dl = dmv * y_ref[...] * g * (1.0 - g)
            dl_ref[...] = dl.astype(dl_ref.dtype)

            @pl.when(first)
            def _():
                dbias_ref[...] = jnp.zeros_like(dbias_ref)

            dbias_ref[...] += jnp.sum(dl, axis=0, keepdims=True)

    row = pl.BlockSpec((ts, d), lambda i: (i, 0))
    lg = [pl.BlockSpec((ts, d), functools.partial(lambda i, c: (i, c), c=c)) for c in range(3)]
    bs = [pl.BlockSpec((1, d), functools.partial(lambda i, c: (0, c), c=c)) for c in range(3)]
    vec = pl.BlockSpec((1, d), lambda i: (0, 0))
    yshape = jax.ShapeDtypeStruct((s, d), BF16)
    vshape = jax.ShapeDtypeStruct((1, d), F32)
    outs = _pcall(kern, name=name, grid=(s // ts,), in_specs=lg + bs + [row, row, row, row],
                  out_specs=[row, row, row, row, row, row, vec, vec, vec],
                  out_shape=[yshape] * 6 + [vshape] * 3,
                  compiler_params=_params("arbitrary"))(logits, logits, logits, bias, bias, bias, ya, yb, yc, dm)
    return outs[0], outs[1], outs[2], outs[3:6], jnp.concatenate(outs[6:9], axis=1)


def _heads(t, n):
    s = t.shape[0]
    return t.reshape(s, n, HEAD_DIM).transpose(1, 0, 2)


def _unheads(t):
    n, s, hd = t.shape
    return t.transpose(1, 0, 2).reshape(s, n * hd)


def _to_class_major(t):
    s = t.shape[0]
    w = t.shape[1] // (DSA_HPG * len(DSA_GROUPS))
    parts = []
    for g, (_, r) in enumerate(DSA_GROUPS):
        tg = t[:, g * DSA_HPG * w:(g + 1) * DSA_HPG * w].reshape(s // r, r, DSA_HPG, w)
        parts.append(tg.transpose(2, 1, 0, 3).reshape(DSA_HPG * s, w))
    return jnp.concatenate(parts, axis=0)


def _slot_to_class_major(t):
    hh, s, w = t.shape
    parts = []
    for _, r in DSA_GROUPS:
        parts.append(t.reshape(hh, s // r, r, w).transpose(0, 2, 1, 3).reshape(hh * s, w))
    return jnp.concatenate(parts, axis=0)


def _from_class_major(t):
    rows, w = t.shape
    s = rows // 12
    out = []
    for g, (_, r) in enumerate(DSA_GROUPS):
        tg = t[g * 4 * s:(g + 1) * 4 * s].reshape(DSA_HPG, r, s // r, w)
        out.append(tg.transpose(0, 2, 1, 3).reshape(DSA_HPG, s, w))
    return out


def _pad_lanes(t):
    return jnp.concatenate([t, jnp.zeros(t.shape[:-1] + (LANES - t.shape[-1],), t.dtype)], axis=-1)


G_FFN1 = ['ffn1_w1', 'ffn1_w3', 'ffn1_w2']
G_FFN2 = ['ffn2_w1', 'ffn2_w3', 'ffn2_w2']
G_MID = [n for n in BIG if n not in G_FFN1 + G_FFN2]


def _ffn_fwd(h, w1, w3, w2, tag, epilogue, side=None):
    carried = None
    if side is None:
        a, b, f = _ffn_up(h, w1, w3, name=f"{tag}_up")
    else:
        (a, b, f), carried = _ffn_up(h, w1, w3, name=f"{tag}_up", side=side)
    outs = _matmul(f, w2, name=f"{tag}_down", alpha=0.5, tm=512, tn=1024, tk=1408, epilogue=epilogue)
    return outs, (h, a, b, f), carried


def _ffn_bwd(x, norm, w1, w3, w2, saved, dy, dyb, tag, side=None, own_side=None):
    h, a, b, f = saved
    dw2 = _matmul(f, dyb, name=f"{tag}_dw2", ta=True, alpha=0.5, tm=1408, tn=1024, tk=512)
    carried = None
    if side is None:
        da, db = _ffn_dact(dyb, w2, a, b, name=f"{tag}_dact")
    else:
        (da, db), carried = _ffn_dact(dyb, w2, a, b, name=f"{tag}_dact", side=side)
    dw1 = _matmul(h, da, name=f"{tag}_dw1", ta=True, tm=1024, tn=1408, tk=512)
    dw3 = _matmul(h, db, name=f"{tag}_dw3", ta=True, tm=1024, tn=1408, tk=512)
    outs = _matmul(da, w1, name=f"{tag}_dh", tb=True, tm=512, tn=1024, tk=1408, pair2=(db, w3),
                   epilogue=(_epi_rms_bwd, [x, dy], [norm], [F32, BF16], 1),
                   side=None if own_side is None else own_side(dw1, dw3, dw2))
    (dx, dxb, dnorm), own = outs if own_side is not None else (outs, None)
    return dx, dxb, dnorm, dw1, dw3, dw2, carried, own


def _local_step(x, mem, loss_target, wl, ws):
    s, d = x.shape
    assert s % (QB * 16) == 0
    rope = _rope_tables(s)
    bf = {n: wl[n].astype(BF16) for n in BIG}
    w = dict(ws)
    w.update(_unpack_gathered(_exchange(_pack_rows(bf, G_FFN1), _two_level_phases(), name="gather_ffn1"), wl, G_FFN1))

    h1 = _rms_fwd(x, w['ffn1_norm'], name="ffn1_rms")
    (x1, h), sv1, late = _ffn_fwd(h1, w['ffn1_w1'], w['ffn1_w3'], w['ffn1_w2'], "ffn1",
                                  (_epi_residual_rms, [x], [w['mix_norm']], [F32, BF16], 0),
                                  side=_side(_pack_rows(bf, G_MID), _two_level_phases()))
    w.update(_unpack_gathered(late, wl, G_MID))
    p = _matmul(h, w['w_in'], name="in_proj", out_dtype=BF16, tn=1024)
    logits = _matmul(h, w['w_gate'], name="gate_proj", tn=1024)
    c_qb, c_kb, c_vb, c_qc = 3 * SB_W, 3 * SB_W + DSA_W, 3 * SB_W + 2 * DSA_W, 3 * SB_W + 3 * DSA_W

    oa_t, late = _sb2_fwd(p, name="sb_fwd", side=_side(_pack_rows(bf, G_FFN2), _two_level_phases()))
    w.update(_unpack_gathered(late, wl, G_FFN2))
    ya = _matmul(oa_t, w['w_branch_sb'], name="sb_out")

    qb_n = _qknorm_fwd(p, c_qb, DSA_W, w['qn_dsa'], rope, name="dsa_qnorm", out_dtype=F32)
    kb_n = _qknorm_fwd(p, c_kb, DSA_W, w['kn_dsa'], rope, name="dsa_knorm", out_dtype=F32)
    vb32 = p[:, c_vb:c_vb + DSA_W].astype(F32)
    groups = range(len(DSA_GROUPS))
    ob_t, lse_b = _dsa2_combine([_dsa2_fwd(qb_n, kb_n, vb32, gi, name=f"dsa_fwd{gi}") for gi in groups], name="dsa_combine")
    yb = _matmul(ob_t, w['w_branch_dsa'], name="dsa_out")

    memh = _rms_fwd(mem, w['mem_norm'], name="mem_rms")
    kv = _matmul(memh, w['w_mem_kv'], name="mem_kv", out_dtype=BF16)
    km_n = _qknorm_fwd(kv, 0, MEM_W, w['kn_mem'], None, name="mem_knorm")
    qc_n = _qknorm_fwd(p, c_qc, MEM_W, w['qn_mem'], None, name="mem_qnorm")
    oc_t = _mem2_fwd(qc_n, km_n, kv, name="mem_fwd")
    yc = _matmul(oc_t, w['w_branch_mem'], name="mem_out")

    merged = _merge_fwd(logits, w['b_gate'], ya, yb, yc, name="merge")
    x2, h2 = _matmul(merged, w['w_out'], name="out_proj", tn=1024,
                     epilogue=(_epi_residual_rms, [x1], [w['ffn2_norm']], [F32, BF16], 0))
    (dx3, dx3b, sq), sv2, _ = _ffn_fwd(h2, w['ffn2_w1'], w['ffn2_w3'], w['ffn2_w2'], "ffn2",
                                       (_epi_loss, [x2, loss_target], [], [F32, BF16], 1))
    loss = jnp.sum(sq) * (0.5 / d)

    g, recv = {}, {}

    def owners(names):
        return _pack_for_owners(g, wl, names).astype(BF16)

    dx2, dx2b, g['ffn2_norm'], g['ffn2_w1'], g['ffn2_w3'], g['ffn2_w2'], _, _ = _ffn_bwd(
        x2, w['ffn2_norm'], w['ffn2_w1'], w['ffn2_w3'], w['ffn2_w2'], sv2, dx3, dx3b, "ffn2")

    g['w_out'] = _matmul(merged, dx2b, name="d_w_out", ta=True, tn=1024, tk=512)
    dm = _matmul(dx2b, w['w_out'], name="d_merged", tb=True, tn=1024)
    dya, dyb, dyc, dlog, g['b_gate'] = _merge_bwd(logits, w['b_gate'], ya, yb, yc, dm, name="d_merge")
    dlogits = jnp.concatenate(dlog, axis=1)

    g['w_branch_sb'] = _matmul(oa_t, dya, name="d_w_sb", ta=True, tn=1024, tk=512)
    g['w_branch_dsa'] = _matmul(ob_t, dyb, name="d_w_dsa", ta=True, tk=512)
    g['w_branch_mem'] = _matmul(oc_t, dyc, name="d_w_mem", ta=True, tk=512)
    doa = _matmul(dya, w['w_branch_sb'], name="d_oa", tb=True, out_dtype=BF16)
    dob = _matmul(dyb, w['w_branch_dsa'], name="d_ob", tb=True)
    doc = _matmul(dyc, w['w_branch_mem'], name="d_oc", tb=True, out_dtype=BF16)

    (dqa, dka, dva), recv['ffn2'] = _sb2_bwd(p, oa_t, doa, name="sb_bwd", side=_side(owners(G_FFN2), _direct_phases(True)))

    dd_b = _dsa2_prep(ob_t, dob, name="dsa_prep")
    dgrp = [_dsa2_bwd(qb_n, kb_n, vb32, dob, lse_b, dd_b, gi, name=f"dsa_bwd{gi}") for gi in groups]
    dvb = jnp.concatenate([t[2] for t in dgrp], axis=1).astype(BF16)
    dqb, g['qn_dsa'] = _qknorm_bwd(p, c_qb, DSA_W, w['qn_dsa'], rope, [t[0] for t in dgrp], name="d_dsa_qnorm")
    dkb, g['kn_dsa'] = _qknorm_bwd(p, c_kb, DSA_W, w['kn_dsa'], rope, [t[1] for t in dgrp], name="d_dsa_knorm")

    dqc_n, dkm_n, dvm = _mem2_bwd(qc_n, km_n, kv, doc, name="mem_bwd")
    dqc, g['qn_mem'] = _qknorm_bwd(p, c_qc, MEM_W, w['qn_mem'], None, dqc_n, name="d_mem_qnorm")
    dkm, g['kn_mem'] = _qknorm_bwd(kv, 0, MEM_W, w['kn_mem'], None, dkm_n, name="d_mem_knorm")
    dkv = jnp.concatenate([dkm, dvm.astype(BF16)], axis=1)
    g['w_mem_kv'] = _matmul(memh, dkv, name="d_w_mem_kv", ta=True)
    dmemh = _matmul(dkv, w['w_mem_kv'], name="d_memh", tb=True)
    _, _, g['mem_norm'] = _rms_bwd(mem, w['mem_norm'], dmemh, None, name="d_mem_rms")

    dp = jnp.concatenate([dqa.astype(BF16), dka.astype(BF16), dva.astype(BF16),
                          dqb, dkb, dvb, dqc], axis=1)
    g['w_in'] = _matmul(h, dp, name="d_w_in", ta=True, tn=2048, tk=512)
    g['w_gate'] = _matmul(h, dlogits, name="d_w_gate", ta=True, tn=1536, tk=512)
    dh = _matmul(dp, w['w_in'], name="d_h_in", tb=True, tn=1024)
    dx1, dx1b, g['mix_norm'] = _matmul(dlogits, w['w_gate'], name="d_h_gate", tb=True, tm=512, tn=1024,
                                       epilogue=(_epi_rms_bwd_sum, [dh, x1, dx2], [w['mix_norm']], [F32, BF16], 1))

    def own_side(dw1, dw3, dw2):
        g.update(ffn1_w1=dw1, ffn1_w3=dw3, ffn1_w2=dw2)
        return _side(owners(G_FFN1), _direct_phases(True))

    dx0, _, g['ffn1_norm'], _, _, _, recv['mid'], recv['ffn1'] = _ffn_bwd(
        x, w['ffn1_norm'], w['ffn1_w1'], w['ffn1_w3'], w['ffn1_w2'], sv1, dx1, dx1b, "ffn1",
        side=_side(owners(G_MID), _direct_phases(True)), own_side=own_side)
    return loss, dx0, recv, {n: g[n] for n in SMALL}


def _pack_rows(d, names):
    return jnp.concatenate([d[n].reshape(-1, LANES) for n in names], axis=0)


def _unpack_rows(t, like, names):
    out, off = {}, 0
    for n in names:
        r = like[n].size // LANES
        out[n] = t[off:off + r].reshape(like[n].shape)
        off += r
    return out


def _unpack_gathered(t, local, names):
    out, off = {}, 0
    for n in names:
        r, c = local[n].shape
        rows = r * c // LANES
        blk = t[:, off:off + rows].reshape(N_DEV, r, c)
        out[n] = blk.reshape(N_DEV * r, c) if SHARD_AXIS[n] == 0 else blk.transpose(1, 0, 2).reshape(r, N_DEV * c)
        off += rows
    return out


def _pack_for_owners(g, local, names):
    parts = []
    for n in names:
        r, c = local[n].shape
        blk = g[n].reshape(N_DEV, r, c) if SHARD_AXIS[n] == 0 else g[n].reshape(r, N_DEV, c).transpose(1, 0, 2)
        parts.append(blk.reshape(N_DEV, r * c // LANES, LANES))
    return jnp.concatenate(parts, axis=1)


def _pack_small(d, names, extra_rows):
    parts = []
    for n in names:
        v = d[n].reshape(-1)
        pad = (-v.size) % LANES
        parts.append(jnp.concatenate([v, jnp.zeros((pad,), v.dtype)]).reshape(-1, LANES))
    t = jnp.concatenate(parts, axis=0)
    return jnp.concatenate([t, jnp.zeros((extra_rows, LANES), t.dtype)], axis=0)


def _unpack_small(t, like, names):
    out, off = {}, 0
    for n in names:
        size = like[n].size
        rows = -(-size // LANES)
        out[n] = t[off:off + rows].reshape(-1)[:size].reshape(like[n].shape)
        off += rows
    return out


def _direct_phases(per_peer):
    def descriptors(src_ref, out_ref, send_sems, recv_sems, local_sem):
        x, y, c = lax.axis_index("x"), lax.axis_index("y"), lax.axis_index("c")
        me = 4 * x + 2 * y + c
        mine = pltpu.make_async_copy(src_ref.at[me] if per_peer else src_ref, out_ref.at[me], local_sem)
        copies = []
        for k in range(1, N_DEV):
            px = 1 - x if k & 4 else x
            py = 1 - y if k & 2 else y
            pc = 1 - c if k & 1 else c
            copies.append(pltpu.make_async_remote_copy(
                src_ref=src_ref.at[4 * px + 2 * py + pc] if per_peer else src_ref, dst_ref=out_ref.at[me],
                send_sem=send_sems.at[k - 1], recv_sem=recv_sems.at[k - 1],
                device_id=(px, py, pc), device_id_type=pl.DeviceIdType.MESH))
        return mine, copies

    def start(*refs):
        mine, copies = descriptors(*refs)
        mine.start()
        for cp in copies:
            cp.start()

    def forward(*refs):
        pass

    def finish(*refs):
        mine, copies = descriptors(*refs)
        for cp in copies:
            cp.wait_recv()
        for cp in copies:
            cp.wait_send()
        mine.wait()

    return start, forward, finish


EXCHANGE_SEMS = [pltpu.SemaphoreType.DMA((N_DEV - 1,)), pltpu.SemaphoreType.DMA((N_DEV - 1,)), pltpu.SemaphoreType.DMA]


def _exchange(src, phases, *, name):
    rows = src.shape[-2]

    def body(*refs):
        for phase in phases:
            phase(*refs)

    anyspace = pl.BlockSpec(memory_space=pl.ANY)
    return _pcall(body, name=name, in_specs=[anyspace], out_specs=anyspace,
                  out_shape=jax.ShapeDtypeStruct((N_DEV, rows, LANES), src.dtype), scratch_shapes=list(EXCHANGE_SEMS))(src)


def _side(src, phases):
    start, forward, finish = phases

    def before(first, mid, ins, outs, scratch):
        pl.when(first)(lambda: start(ins[0], outs[0], *scratch))
        pl.when(mid)(lambda: forward(ins[0], outs[0], *scratch))

    def after(last, ins, outs, scratch):
        pl.when(last)(lambda: finish(ins[0], outs[0], *scratch))

    return [src], [jax.ShapeDtypeStruct((N_DEV, src.shape[-2], LANES), src.dtype)], list(EXCHANGE_SEMS), before, after


def _call_2d(kern, *, name, grid, in_specs, out_specs, out_shape, ins, scratch_shapes=(), semantics, side=None):
    if side is None:
        return _pcall(kern, name=name, grid=grid, in_specs=in_specs, out_specs=out_specs, out_shape=out_shape,
                      scratch_shapes=list(scratch_shapes), compiler_params=_params(*semantics))(*ins)
    s_ins, s_shapes, s_scratch, before, after = side
    n_in, n_out, n_scr = len(ins), len(out_shape), len(scratch_shapes)

    def combined(*refs):
        refs = list(refs)
        cut = [n_in, len(s_ins), n_out, len(s_shapes), n_scr, len(s_scratch)]
        parts, pos = [], 0
        for c in cut:
            parts.append(refs[pos:pos + c])
            pos += c
        m_in, c_in, m_out, c_out, m_scr, c_scr = parts
        ids = [pl.program_id(a) for a in range(len(grid))]
        inner_zero = functools.reduce(jnp.logical_and, [i == 0 for i in ids[1:]])
        first = jnp.logical_and(ids[0] == 0, inner_zero)
        mid = jnp.logical_and(ids[0] == grid[0] // 2, inner_zero)
        last = functools.reduce(jnp.logical_and, [i == n - 1 for i, n in zip(ids, grid)])
        before(first, mid, c_in, c_out, c_scr)
        kern(*m_in, *m_out, *m_scr)
        after(last, c_in, c_out, c_scr)

    anyspace = pl.BlockSpec(memory_space=pl.ANY)
    outs = _pcall(combined, name=name, grid=grid, in_specs=list(in_specs) + [anyspace] * len(s_ins),
                  out_specs=list(out_specs) + [anyspace] * len(s_shapes), out_shape=list(out_shape) + s_shapes,
                  scratch_shapes=list(scratch_shapes) + s_scratch, compiler_params=_params(*["arbitrary"] * len(grid)))(*ins, *s_ins)
    return outs[:n_out], outs[n_out]


def _two_level_phases():
    def parts(src_ref, out_ref, send_sems, recv_sems, local_sem):
        x, y, c = lax.axis_index("x"), lax.axis_index("y"), lax.axis_index("c")
        me, sibling = (x, y, c), (x, y, 1 - c)
        chips = [(1 - x, y), (x, 1 - y), (1 - x, 1 - y)]

        def slab(px, py, pc):
            return out_ref.at[4 * px + 2 * py + pc]

        def copy(k, block, to, from_src=False):
            return pltpu.make_async_remote_copy(
                src_ref=src_ref if from_src else slab(*block), dst_ref=slab(*block),
                send_sem=send_sems.at[k], recv_sem=recv_sems.at[k], device_id=to, device_id_type=pl.DeviceIdType.MESH)

        return dict(
            mine=lambda: pltpu.make_async_copy(src_ref, slab(*me), local_sem),
            first=lambda: [copy(0, me, sibling, True)] + [copy(1 + j, me, (*chip, c), True) for j, chip in enumerate(chips)],
            passed=lambda: [copy(4 + j, (*chip, c), sibling) for j, chip in enumerate(chips)],
            landed=lambda: [copy(1 + j, (*chip, c), me) for j, chip in enumerate(chips)],
            late=lambda: [copy(0, sibling, me)] + [copy(4 + j, (*chip, 1 - c), me) for j, chip in enumerate(chips)])

    def start(*refs):
        make = parts(*refs)
        make['mine']().start()
        for cp in make['first']():
            cp.start()

    def forward(*refs):
        make = parts(*refs)
        for arrived, onward in zip(make['landed'](), make['passed']()):
            arrived.wait_recv()
            onward.start()

    def finish(*refs):
        make = parts(*refs)
        for cp in make['late']():
            cp.wait_recv()
        for cp in make['first']() + make['passed']():
            cp.wait_send()
        make['mine']().wait()

    return start, forward, finish


def _adamw(recv, w, m, v, *, name):
    rows = w.shape[0]
    tr = _pick(rows, (512, 256, 128, 64))

    def kern(r_ref, w_ref, m_ref, v_ref, g_ref, d_ref, mo_ref, vo_ref):
        g = r_ref[0].astype(F32)
        for p in range(1, N_DEV):
            g = g + r_ref[p].astype(F32)
        mn = ADAM_B1 * m_ref[...] + (1.0 - ADAM_B1) * g
        vn = ADAM_B2 * v_ref[...] + (1.0 - ADAM_B2) * (g * g)
        m_hat = mn / (1.0 - ADAM_B1 ** ADAM_STEP)
        v_hat = vn / (1.0 - ADAM_B2 ** ADAM_STEP)
        g_ref[...] = g
        d_ref[...] = -ADAM_LR * (m_hat / (jnp.sqrt(v_hat) + ADAM_EPS) + ADAM_WD * w_ref[...])
        mo_ref[...] = mn
        vo_ref[...] = vn

    row = pl.BlockSpec((tr, LANES), lambda i: (i, 0))
    shp = jax.ShapeDtypeStruct((rows, LANES), F32)
    return _pcall(kern, name=name, grid=(rows // tr,), in_specs=[pl.BlockSpec((N_DEV, tr, LANES), lambda i: (0, i, 0)), row, row, row],
                  out_specs=[row, row, row, row], out_shape=[shp, shp, shp, shp], compiler_params=_params("parallel"))(recv, w, m, v)


INPUTS = ['x', 'mem'] + WEIGHTS + ['loss_target'] + ['m_' + n for n in WEIGHTS] + ['v_' + n for n in WEIGHTS]
SMALL_PAD_ROWS = 4


def kernel(x, mem, ffn1_norm, ffn1_w1, ffn1_w3, ffn1_w2, mix_norm, mem_norm, w_in, w_mem_kv, qn_dsa, kn_dsa, qn_mem, kn_mem, w_branch_sb, w_branch_dsa, w_branch_mem, w_gate, b_gate, w_out, ffn2_norm, ffn2_w1, ffn2_w3, ffn2_w2, loss_target, m_ffn1_norm, m_ffn1_w1, m_ffn1_w3, m_ffn1_w2, m_mix_norm, m_mem_norm, m_w_in, m_w_mem_kv, m_qn_dsa, m_kn_dsa, m_qn_mem, m_kn_mem, m_w_branch_sb, m_w_branch_dsa, m_w_branch_mem, m_w_gate, m_b_gate, m_w_out, m_ffn2_norm, m_ffn2_w1, m_ffn2_w3, m_ffn2_w2, v_ffn1_norm, v_ffn1_w1, v_ffn1_w3, v_ffn1_w2, v_mix_norm, v_mem_norm, v_w_in, v_w_mem_kv, v_qn_dsa, v_kn_dsa, v_qn_mem, v_kn_mem, v_w_branch_sb, v_w_branch_dsa, v_w_branch_mem, v_w_gate, v_b_gate, v_w_out, v_ffn2_norm, v_ffn2_w1, v_ffn2_w3, v_ffn2_w2):
    given = dict(zip(INPUTS, (x, mem, ffn1_norm, ffn1_w1, ffn1_w3, ffn1_w2, mix_norm, mem_norm, w_in, w_mem_kv, qn_dsa, kn_dsa, qn_mem, kn_mem, w_branch_sb, w_branch_dsa, w_branch_mem, w_gate, b_gate, w_out, ffn2_norm, ffn2_w1, ffn2_w3, ffn2_w2, loss_target, m_ffn1_norm, m_ffn1_w1, m_ffn1_w3, m_ffn1_w2, m_mix_norm, m_mem_norm, m_w_in, m_w_mem_kv, m_qn_dsa, m_kn_dsa, m_qn_mem, m_kn_mem, m_w_branch_sb, m_w_branch_dsa, m_w_branch_mem, m_w_gate, m_b_gate, m_w_out, m_ffn2_norm, m_ffn2_w1, m_ffn2_w3, m_ffn2_w2, v_ffn1_norm, v_ffn1_w1, v_ffn1_w3, v_ffn1_w2, v_mix_norm, v_mem_norm, v_w_in, v_w_mem_kv, v_qn_dsa, v_kn_dsa, v_qn_mem, v_kn_mem, v_w_branch_sb, v_w_branch_dsa, v_w_branch_mem, v_w_gate, v_b_gate, v_w_out, v_ffn2_norm, v_ffn2_w1, v_ffn2_w3, v_ffn2_w2), strict=True))
    wl = {n: given[n][0] for n in BIG}
    ws = {n: given[n] for n in SMALL}

    loss, dx, recv, g = _local_step(x[0], mem[0], loss_target[0], wl, ws)

    big = [{}, {}, {}, {}]
    for tag, names in (("ffn2", G_FFN2), ("mid", G_MID), ("ffn1", G_FFN1)):
        outs = _adamw(recv[tag], _pack_rows(wl, names), _pack_rows({n: given['m_' + n][0] for n in names}, names),
                      _pack_rows({n: given['v_' + n][0] for n in names}, names), name=f"adamw_{tag}")
        for kind, t in enumerate(outs):
            big[kind].update(_unpack_rows(t, wl, names))

    gs = _pack_small(g, SMALL, SMALL_PAD_ROWS)
    loss_row = gs.shape[0] - SMALL_PAD_ROWS
    gs = gs.at[loss_row, 0].set(loss)
    recv_s = _exchange(gs, _direct_phases(False), name="gather_small")
    small = _adamw(recv_s, _pack_small(ws, SMALL, SMALL_PAD_ROWS), _pack_small({n: given['m_' + n] for n in SMALL}, SMALL, SMALL_PAD_ROWS),
                   _pack_small({n: given['v_' + n] for n in SMALL}, SMALL, SMALL_PAD_ROWS), name="adamw_replicated")
    total_loss = small[0][loss_row, 0]
    small = [_unpack_small(t, ws, SMALL) for t in small]

    outs = [total_loss, dx[None]]
    for kind in range(4):
        outs += [big[kind][n][None] if n in wl else small[kind][n] for n in WEIGHTS]
    return tuple(outs)
```

```python
import functools
import math

import jax
import jax.numpy as jnp
from jax import lax
from jax.experimental import pallas as pl
from jax.experimental.pallas import tpu as pltpu

F32 = jnp.float32
BF16 = jnp.bfloat16
MXU_DT = jnp.bfloat16

N_DEV = 8
HEAD_DIM = 64
SB_HEADS = 8
DSA_GROUPS = ((128, 1), (512, 4), (2048, 16))
DSA_HPG = 4
MEM_HEADS = 4
SB_W = SB_HEADS * HEAD_DIM
DSA_W = DSA_HPG * len(DSA_GROUPS) * HEAD_DIM
DSA_OUT_W = DSA_HPG * HEAD_DIM
MEM_W = MEM_HEADS * HEAD_DIM
ROPE_THETA = 10000.0
NORM_EPS = 1e-6
QB = 128
SCALE = HEAD_DIM ** -0.5
ADAM_LR, ADAM_B1, ADAM_B2, ADAM_EPS, ADAM_WD, ADAM_STEP = 0.001, 0.9, 0.999, 1e-08, 0.01, 10

LANES = 128
VMEM_LIMIT = 48 * 1024 * 1024
SB_DEAD = -110.0 * 1.4426950408889634

WEIGHTS = ['ffn1_norm', 'ffn1_w1', 'ffn1_w3', 'ffn1_w2', 'mix_norm', 'mem_norm', 'w_in', 'w_mem_kv', 'qn_dsa', 'kn_dsa',
           'qn_mem', 'kn_mem', 'w_branch_sb', 'w_branch_dsa', 'w_branch_mem', 'w_gate', 'b_gate', 'w_out', 'ffn2_norm',
           'ffn2_w1', 'ffn2_w3', 'ffn2_w2']
SHARD_AXIS = {'ffn1_norm': None, 'ffn1_w1': 1, 'ffn1_w3': 1, 'ffn1_w2': 0, 'mix_norm': None, 'mem_norm': None, 'w_in': 1,
              'w_mem_kv': 0, 'qn_dsa': None, 'kn_dsa': None, 'qn_mem': None, 'kn_mem': None, 'w_branch_sb': 1,
              'w_branch_dsa': 1, 'w_branch_mem': 1, 'w_gate': 1, 'b_gate': None, 'w_out': 0, 'ffn2_norm': None,
              'ffn2_w1': 1, 'ffn2_w3': 1, 'ffn2_w2': 0}
BIG = [n for n in WEIGHTS if SHARD_AXIS[n] is not None]
SMALL = [n for n in WEIGHTS if SHARD_AXIS[n] is None]


def _pcall(kern, **kw):
    return pl.pallas_call(kern, **kw)


def _params(*sem):
    return pltpu.CompilerParams(dimension_semantics=sem, vmem_limit_bytes=VMEM_LIMIT)


def _dot(a, b, dims):
    return lax.dot_general(a.astype(MXU_DT), b.astype(MXU_DT), (dims, ((), ())), preferred_element_type=F32)


def _nn(a, b):
    return _dot(a, b, ((1,), (0,)))


def _nt(a, b):
    return _dot(a, b, ((1,), (1,)))


def _tn(a, b):
    return _dot(a, b, ((0,), (0,)))


def _pick(n, prefs):
    for p in prefs:
        if n % p == 0:
            return p
    return n


def _matmul(a, b, *, name, ta=False, tb=False, out_dtype=F32, res=None, alpha=1.0, tm=1024, tn=512, tk=1024, pair2=None,
            epilogue=None, side=None):
    if ta:
        kdim, m = a.shape
    else:
        m, kdim = a.shape
    n = b.shape[0] if tb else b.shape[1]
    tm = _pick(m, (tm, 512, 256, 128))
    tn = _pick(n, (tn, 512, 384, 256, 128))
    tk = _pick(kdim, (tk, 1024, 512, 256, 128))
    nk = kdim // tk
    a_spec = pl.BlockSpec((tk, tm), lambda i, j, k: (k, i)) if ta else pl.BlockSpec((tm, tk), lambda i, j, k: (i, k))
    b_spec = pl.BlockSpec((tn, tk), lambda i, j, k: (j, k)) if tb else pl.BlockSpec((tk, tn), lambda i, j, k: (k, j))
    o_spec = pl.BlockSpec((tm, tn), lambda i, j, k: (i, j))
    v_spec = pl.BlockSpec((1, tn), lambda i, j, k: (0, j))
    dims = ((0 if ta else 1,), (1 if tb else 0,))
    n_mm = 2 if pair2 is None else 4
    if epilogue is None:
        row_ins, vec_ins = ([] if res is None else [res]), []
        out_dtypes, n_vec = [out_dtype], 0
    else:
        assert tn == n and res is None
        epi_fn, row_ins, vec_ins, out_dtypes, n_vec = epilogue
    n_row_out = len(out_dtypes)

    def kern(*refs):
        refs = list(refs)
        acc_ref = refs.pop() if nk > 1 else None
        mm = refs[:n_mm]
        extra = refs[n_mm:n_mm + len(row_ins) + len(vec_ins)]
        outs = refs[n_mm + len(extra):]
        i = pl.program_id(0)
        k = pl.program_id(2)

        def product():
            part = _dot(mm[0][...], mm[1][...], dims)
            if pair2 is not None:
                part = part + _dot(mm[2][...], mm[3][...], dims)
            return part

        def finish(r):
            if alpha != 1.0:
                r = r * alpha
            if epilogue is None:
                if extra:
                    r = extra[0][...] + r
                outs[0][...] = r.astype(out_dtype)
                return
            vals = epi_fn(r, *[e[...] for e in extra])
            for o_ref, v in zip(outs[:n_row_out], vals[:n_row_out]):
                o_ref[...] = v.astype(o_ref.dtype)
            for o_ref, v in zip(outs[n_row_out:], vals[n_row_out:]):
                @pl.when(i == 0)
                def _():
                    o_ref[...] = jnp.zeros_like(o_ref)

                o_ref[...] += v

        if nk == 1:
            finish(product())
            return

        @pl.when(k == 0)
        def _():
            acc_ref[...] = jnp.zeros_like(acc_ref)

        acc_ref[...] += product()

        @pl.when(k == nk - 1)
        def _():
            finish(acc_ref[...])

    ins = [a, b] + ([] if pair2 is None else list(pair2)) + list(row_ins) + list(vec_ins)
    specs = [a_spec, b_spec] * (n_mm // 2) + [o_spec] * len(row_ins) + [v_spec] * len(vec_ins)
    out_specs = [o_spec] * n_row_out + [v_spec] * n_vec
    out_shape = [jax.ShapeDtypeStruct((m, n), dt) for dt in out_dtypes] + [jax.ShapeDtypeStruct((1, n), F32)] * n_vec
    outs = _call_2d(kern, name=name, grid=(m // tm, n // tn, nk), in_specs=specs, out_specs=out_specs, out_shape=out_shape,
                    ins=ins, scratch_shapes=[pltpu.VMEM((tm, tn), F32)] if nk > 1 else [],
                    semantics=("arbitrary" if n_vec else "parallel", "parallel", "arbitrary"), side=side)
    carried = None
    if side is not None:
        outs, carried = outs
    outs = outs[0] if epilogue is None else outs
    return outs if side is None else (outs, carried)


def _epi_residual_rms(r, res, gain):
    xn = res + r
    return xn, xn * lax.rsqrt(jnp.mean(xn * xn, axis=-1, keepdims=True) + NORM_EPS) * gain


def _epi_rms_bwd(r, x, dres, gain):
    rs = lax.rsqrt(jnp.mean(x * x, axis=-1, keepdims=True) + NORM_EPS)
    xh = x * rs
    dy = r * gain
    dx = dres + rs * (dy - xh * jnp.mean(dy * xh, axis=-1, keepdims=True))
    return dx, dx, jnp.sum(r * xh, axis=0, keepdims=True)


def _epi_rms_bwd_sum(r, r0, x, dres, gain):
    return _epi_rms_bwd(r + r0, x, dres, gain)


def _epi_loss(r, res, target):
    e = (res + r) - target
    dy = e / e.shape[-1]
    return dy, dy, jnp.sum(e * e, axis=0, keepdims=True)
def _rms_fwd(x, g, *, name):
    s, d = x.shape
    ts = _pick(s, (512, 256))

    def kern(x_ref, g_ref, h_ref):
        xf = x_ref[...]
        r = lax.rsqrt(jnp.mean(xf * xf, axis=-1, keepdims=True) + NORM_EPS)
        h_ref[...] = (xf * r * g_ref[...]).astype(h_ref.dtype)

    return _pcall(kern, name=name, grid=(s // ts,),
                  in_specs=[pl.BlockSpec((ts, d), lambda i: (i, 0)), pl.BlockSpec((1, d), lambda i: (0, 0))],
                  out_specs=pl.BlockSpec((ts, d), lambda i: (i, 0)), out_shape=jax.ShapeDtypeStruct((s, d), BF16),
                  compiler_params=_params("parallel"))(x, g)


def _rms_bwd(x, g, dh, res, *, name):
    s, d = x.shape
    ts = _pick(s, (512, 256))

    def kern(*refs):
        if res is None:
            x_ref, g_ref, dh_ref, dx_ref, dxb_ref, dg_ref = refs
            r_ref = None
        else:
            x_ref, g_ref, dh_ref, r_ref, dx_ref, dxb_ref, dg_ref = refs
        xf = x_ref[...]
        r = lax.rsqrt(jnp.mean(xf * xf, axis=-1, keepdims=True) + NORM_EPS)
        xh = xf * r
        dhf = dh_ref[...].astype(F32)
        dy = dhf * g_ref[...]
        dx = r * (dy - xh * jnp.mean(dy * xh, axis=-1, keepdims=True))
        if r_ref is not None:
            dx = r_ref[...] + dx
        dx_ref[...] = dx
        dxb_ref[...] = dx.astype(dxb_ref.dtype)

        @pl.when(pl.program_id(0) == 0)
        def _():
            dg_ref[...] = jnp.zeros_like(dg_ref)

        dg_ref[...] += jnp.sum(dhf * xh, axis=0, keepdims=True)

    row = pl.BlockSpec((ts, d), lambda i: (i, 0))
    vec = pl.BlockSpec((1, d), lambda i: (0, 0))
    ins = [x, g, dh] + ([] if res is None else [res])
    return _pcall(kern, name=name, grid=(s // ts,), in_specs=[row, vec, row] + ([] if res is None else [row]),
                  out_specs=[row, row, vec],
                  out_shape=[jax.ShapeDtypeStruct((s, d), F32), jax.ShapeDtypeStruct((s, d), BF16), jax.ShapeDtypeStruct((1, d), F32)],
                  compiler_params=_params("arbitrary"))(*ins)


def _sigmoid(x):
    return 1.0 / (1.0 + jnp.exp(-x))


FFN_TM, FFN_TF = 512, 1408


def _ffn_up(h, w1, w3, *, name, side=None):
    s, d = h.shape
    fdim = w1.shape[1]
    tm, tf = _pick(s, (FFN_TM, 256)), _pick(fdim, (FFN_TF, 512, 256, 128))

    def kern(h_ref, w1_ref, w3_ref, a_ref, b_ref, f_ref):
        hb = h_ref[...]
        a = _nn(hb, w1_ref[...])
        b = _nn(hb, w3_ref[...])
        a_ref[...] = a.astype(a_ref.dtype)
        b_ref[...] = b.astype(b_ref.dtype)
        f_ref[...] = (a * _sigmoid(a) * b).astype(f_ref.dtype)

    wspec = pl.BlockSpec((d, tf), lambda i, j: (0, j))
    ospec = pl.BlockSpec((tm, tf), lambda i, j: (i, j))
    shp = jax.ShapeDtypeStruct((s, fdim), BF16)
    return _call_2d(kern, name=name, grid=(s // tm, fdim // tf), in_specs=[pl.BlockSpec((tm, d), lambda i, j: (i, 0)), wspec, wspec],
                    out_specs=[ospec, ospec, ospec], out_shape=[shp, shp, shp], ins=[h, w1, w3],
                    semantics=("parallel", "parallel"), side=side)


def _ffn_dact(dy, w2, a, b, *, name, side=None):
    s, d = dy.shape
    fdim = w2.shape[0]
    tm, tf = _pick(s, (FFN_TM, 256)), _pick(fdim, (FFN_TF, 512, 256, 128))

    def kern(dy_ref, w2_ref, a_ref, b_ref, da_ref, db_ref):
        df = _nt(dy_ref[...], w2_ref[...]) * 0.5
        av = a_ref[...].astype(F32)
        sg = _sigmoid(av)
        da_ref[...] = (df * b_ref[...].astype(F32) * (sg + av * sg * (1.0 - sg))).astype(da_ref.dtype)
        db_ref[...] = (df * (av * sg)).astype(db_ref.dtype)

    ospec = pl.BlockSpec((tm, tf), lambda i, j: (i, j))
    shp = jax.ShapeDtypeStruct((s, fdim), BF16)
    return _call_2d(kern, name=name, grid=(s // tm, fdim // tf),
                    in_specs=[pl.BlockSpec((tm, d), lambda i, j: (i, 0)), pl.BlockSpec((tf, d), lambda i, j: (j, 0)), ospec, ospec],
                    out_specs=[ospec, ospec], out_shape=[shp, shp], ins=[dy, w2, a, b], semantics=("parallel", "parallel"), side=side)


def _loss_head(y, t, *, name):
    s, d = y.shape
    ts = _pick(s, (512, 256))
    n = s // ts

    def kern(y_ref, t_ref, dy_ref, dyb_ref, l_ref, acc_ref):
        i = pl.program_id(0)

        @pl.when(i == 0)
        def _():
            acc_ref[...] = jnp.zeros_like(acc_ref)

        e = y_ref[...] - t_ref[...]
        dy_ref[...] = e / d
        dyb_ref[...] = (e / d).astype(dyb_ref.dtype)
        acc_ref[...] += jnp.sum(e * e, axis=0, keepdims=True)

        @pl.when(i == n - 1)
        def _():
            l_ref[...] = jnp.sum(acc_ref[...], axis=1, keepdims=True) * (0.5 / d)

    row = pl.BlockSpec((ts, d), lambda i: (i, 0))
    return _pcall(kern, name=name, grid=(n,), in_specs=[row, row], out_specs=[row, row, pl.BlockSpec((1, 1), lambda i: (0, 0))],
                  out_shape=[jax.ShapeDtypeStruct((s, d), F32), jax.ShapeDtypeStruct((s, d), BF16), jax.ShapeDtypeStruct((1, 1), F32)],
                  scratch_shapes=[pltpu.VMEM((1, d), F32)], compiler_params=_params("arbitrary"))(y, t)


def _head_mean(v, bd):
    hi = v.astype(BF16)
    lo = (v - hi.astype(F32)).astype(BF16)
    return (lax.dot_general(hi, bd, (((1,), (0,)), ((), ())), preferred_element_type=F32)
            + lax.dot_general(lo, bd, (((1,), (0,)), ((), ())), preferred_element_type=F32))


def _partner(v):
    w = v.shape[1]
    lane = lax.broadcasted_iota(jnp.int32, v.shape, 1)
    return jnp.where(lane % HEAD_DIM < HEAD_DIM // 2, pltpu.roll(v, w - HEAD_DIM // 2, 1), pltpu.roll(v, HEAD_DIM // 2, 1))


def _block_diag(w):
    r = lax.broadcasted_iota(jnp.int32, (w, w), 0) // HEAD_DIM
    c = lax.broadcasted_iota(jnp.int32, (w, w), 1) // HEAD_DIM
    return jnp.where(r == c, 1.0 / HEAD_DIM, 0.0).astype(BF16)


def _rope_tables(s):
    half = HEAD_DIM // 2
    inv_freq = jnp.power(ROPE_THETA, -jnp.arange(half, dtype=F32) / half)
    ang = jnp.arange(s).astype(F32)[:, None] * inv_freq[None, :]
    cos, sin = jnp.cos(ang), jnp.sin(ang)
    cos2 = jnp.concatenate([cos, cos, cos, cos], axis=1)
    sin2 = jnp.concatenate([-sin, sin, -sin, sin], axis=1)
    return cos2, sin2


def _qknorm_fwd(src, col0, width, gain, rope, *, name, out_dtype=BF16):
    s = src.shape[0]
    ts = _pick(s, (512, 256))
    cb = col0 // width
    assert col0 % width == 0
    reps = width // LANES
    g = jnp.tile(gain, (1, width // HEAD_DIM))

    def kern(*refs):
        if rope is None:
            x_ref, g_ref, o_ref = refs
        else:
            x_ref, g_ref, c_ref, s_ref, o_ref = refs
        x = x_ref[...].astype(F32)
        bd = _block_diag(width)
        r = lax.rsqrt(_head_mean(x * x, bd) + NORM_EPS)
        y = x * r * g_ref[...]
        if rope is not None:
            y = y * jnp.tile(c_ref[...], (1, reps)) + _partner(y) * jnp.tile(s_ref[...], (1, reps))
        o_ref[...] = y.astype(o_ref.dtype)

    xs = pl.BlockSpec((ts, width), lambda i: (i, cb))
    tab = pl.BlockSpec((ts, LANES), lambda i: (i, 0))
    ins = [src, g] + ([] if rope is None else list(rope))
    specs = [xs, pl.BlockSpec((1, width), lambda i: (0, 0))] + ([] if rope is None else [tab, tab])
    return _pcall(kern, name=name, grid=(s // ts,), in_specs=specs, out_specs=pl.BlockSpec((ts, width), lambda i: (i, 0)),
                  out_shape=jax.ShapeDtypeStruct((s, width), out_dtype), compiler_params=_params("parallel"))(*ins)


def _qknorm_bwd(src, col0, width, gain, rope, dout, *, name):
    s = src.shape[0]
    ts = _pick(s, (512, 256))
    cb = col0 // width
    reps = width // LANES
    g = jnp.tile(gain, (1, width // HEAD_DIM))

    douts = list(dout) if isinstance(dout, (list, tuple)) else [dout]
    piece = width // len(douts)

    def kern(*refs):
        refs = list(refs)
        dg_ref = refs.pop()
        dx_ref = refs.pop()
        do_refs = [refs.pop() for _ in douts][::-1]
        if rope is None:
            x_ref, g_ref = refs
        else:
            x_ref, g_ref, c_ref, s_ref = refs
        x = x_ref[...].astype(F32)
        bd = _block_diag(width)
        r = lax.rsqrt(_head_mean(x * x, bd) + NORM_EPS)
        xh = x * r
        dy = jnp.concatenate([d[...].astype(F32) for d in do_refs], axis=1) if len(do_refs) > 1 else do_refs[0][...].astype(F32)
        if rope is not None:
            dy = dy * jnp.tile(c_ref[...], (1, reps)) + _partner(dy * jnp.tile(s_ref[...], (1, reps)))
        dxh = dy * g_ref[...]
        dx_ref[...] = (r * (dxh - xh * _head_mean(dxh * xh, bd))).astype(dx_ref.dtype)

        @pl.when(pl.program_id(0) == 0)
        def _():
            dg_ref[...] = jnp.zeros_like(dg_ref)

        dg_ref[...] += jnp.sum(dy * xh, axis=0, keepdims=True)

    xs = pl.BlockSpec((ts, width), lambda i: (i, cb))
    row = pl.BlockSpec((ts, width), lambda i: (i, 0))
    vec = pl.BlockSpec((1, width), lambda i: (0, 0))
    tab = pl.BlockSpec((ts, LANES), lambda i: (i, 0))
    ins = [src, g] + ([] if rope is None else list(rope)) + douts
    specs = [xs, vec] + ([] if rope is None else [tab, tab]) + [pl.BlockSpec((ts, piece), lambda i: (i, 0))] * len(douts)
    dx, dg = _pcall(kern, name=name, grid=(s // ts,), in_specs=specs, out_specs=[row, vec],
                    out_shape=[jax.ShapeDtypeStruct((s, width), BF16), jax.ShapeDtypeStruct((1, width), F32)],
                    compiler_params=_params("arbitrary"))(*ins)
    return dx, jnp.sum(dg.reshape(width // HEAD_DIM, HEAD_DIM), axis=0, keepdims=True)


def _tri(strict):
    r = lax.broadcasted_iota(jnp.int32, (2 * QB, QB), 0) % QB
    c = lax.broadcasted_iota(jnp.int32, (2 * QB, QB), 1)
    return jnp.where((r > c) if strict else (r >= c), 1.0, 0.0).astype(BF16)


def _split_dot(v, t2):
    hi = v.astype(BF16)
    lo = (v - hi.astype(F32)).astype(BF16)
    return lax.dot_general(jnp.concatenate([hi, lo], axis=1), t2, (((1,), (0,)), ((), ())), preferred_element_type=F32)


LOG2E = 1.4426950408889634


def _log2_sigmoids(z2):
    lf = -(jnp.maximum(z2, 0.0) + jnp.log2(1.0 + jnp.exp2(-jnp.abs(z2))))
    return z2 + lf, lf


def _key_blocks(t):
    s = t.shape[0]
    n = t.shape[1] // HEAD_DIM
    return t.reshape(s // QB, QB, n, HEAD_DIM).transpose(2, 0, 3, 1)


def _from_key_blocks(t):
    n, nb, hd, qb = t.shape
    return t.transpose(1, 3, 0, 2).reshape(nb * qb, n * hd)


SB_SUB = 4
SB2_SUB = 2


def _first_half(shape):
    return lax.broadcasted_iota(jnp.int32, shape, 1) < HEAD_DIM


def _split_pair(t, first):
    zero = jnp.zeros_like(t)
    return [jnp.where(first, t, zero), jnp.where(first, zero, t)]


def _sb2_fwd(p, *, name, side=None):
    s = p.shape[0]
    rq = SB2_SUB * QB
    nq = s // rq
    npair = SB_W // LANES

    def kern(q_ref, k_ref, v_ref, o_ref):
        i = pl.program_id(1)
        first = _first_half((rq, LANES))
        qs = _split_pair(q_ref[...], first)
        t2 = _tri(True)
        rel = lax.broadcasted_iota(jnp.int32, (rq, QB), 1) - lax.broadcasted_iota(jnp.int32, (rq, QB), 0)

        def tile(j, carries, accs, masked):
            off = pl.multiple_of(j * QB, QB)
            kt = k_ref[pl.ds(off, QB), :]
            vt = v_ref[pl.ds(off, QB), :]
            out_c, out_a = [], []
            for e in range(2):
                ls, lf = _log2_sigmoids(_nt(qs[e], kt) * (SCALE * LOG2E))
                if masked:
                    before = rel < i * rq - j * QB
                    lf = jnp.where(before, lf, 0.0)
                w = jnp.exp2(ls + _split_dot(lf, t2) + carries[e])
                if masked:
                    w = jnp.where(before, w, 0.0)
                out_c.append(carries[e] + jnp.sum(lf, axis=1, keepdims=True))
                out_a.append(accs[e] + _nn(w, vt))
            return out_c, out_a

        carries = [jnp.zeros((rq, 1), F32)] * 2
        accs = [jnp.zeros((rq, LANES), F32)] * 2
        for a in range(SB2_SUB):
            carries, accs = tile(i * SB2_SUB + (SB2_SUB - 1 - a), carries, accs, True)

        def cond(st):
            return jnp.logical_and(st[0] >= 0, st[1] > 0)

        def body(st):
            carries, accs = tile(st[0], [st[2], st[3]], [st[4], st[5]], False)
            alive = jnp.maximum(jnp.max(carries[0]), jnp.max(carries[1])) > SB_DEAD
            return st[0] - 1, alive.astype(jnp.int32), carries[0], carries[1], accs[0], accs[1]

        st = lax.while_loop(cond, body, (i * SB2_SUB - 1, jnp.int32(1), carries[0], carries[1], accs[0], accs[1]))
        o_ref[...] = jnp.where(first, st[4], st[5])

    outs = _call_2d(kern, name=name, grid=(npair, nq),
                    in_specs=[pl.BlockSpec((rq, LANES), lambda a, i: (i, a)), pl.BlockSpec((s, LANES), lambda a, i: (0, npair + a)),
                              pl.BlockSpec((s, LANES), lambda a, i: (0, 2 * npair + a))],
                    out_specs=[pl.BlockSpec((rq, LANES), lambda a, i: (i, a))], out_shape=[jax.ShapeDtypeStruct((s, SB_W), F32)],
                    ins=[p, p, p], semantics=("parallel", "arbitrary"), side=side)
    return outs[0] if side is None else (outs[0][0], outs[1])


def _sb2_bwd(p, o, do, *, name, side=None):
    s = p.shape[0]
    rq = SB2_SUB * QB
    nq = s // rq
    npair = SB_W // LANES

    def kern(q_ref, k_ref, v_ref, o_ref, do_ref, dq_ref, dk_hbm, dv_hbm, dk_acc, dv_acc, sem):
        pr = pl.program_id(0)
        i = pl.program_id(1)

        @pl.when(i == 0)
        def _():
            dk_acc[...] = jnp.zeros_like(dk_acc)
            dv_acc[...] = jnp.zeros_like(dv_acc)

        first = _first_half((rq, LANES))
        qs = _split_pair(q_ref[...], first)
        do2 = do_ref[...]
        dos = _split_pair(do2, first)
        prod = do2.astype(F32) * o_ref[...]
        dsums = [jnp.sum(jnp.where(first, prod, 0.0), axis=1, keepdims=True),
                 jnp.sum(jnp.where(first, 0.0, prod), axis=1, keepdims=True)]
        t_strict = _tri(True)
        t_incl = _tri(False)
        rel = lax.broadcasted_iota(jnp.int32, (rq, QB), 1) - lax.broadcasted_iota(jnp.int32, (rq, QB), 0)

        def tile(j, carries, gcarries, dqs, masked):
            off = pl.multiple_of(j * QB, QB)
            kt = k_ref[pl.ds(off, QB), :]
            vt = v_ref[pl.ds(off, QB), :]
            out_c, out_g, out_q = [], [], []
            dk_t = jnp.zeros((QB, LANES), F32)
            dv_t = jnp.zeros((QB, LANES), F32)
            for e in range(2):
                ls, lf = _log2_sigmoids(_nt(qs[e], kt) * (SCALE * LOG2E))
                if masked:
                    before = rel < i * rq - j * QB
                    lf = jnp.where(before, lf, 0.0)
                w = jnp.exp2(ls + _split_dot(lf, t_strict) + carries[e])
                if masked:
                    w = jnp.where(before, w, 0.0)
                wr = w.astype(MXU_DT)
                g = _nt(dos[e], vt) * wr.astype(F32)
                big_g = dsums[e] - (_split_dot(g, t_incl) + gcarries[e])
                sig = jnp.exp2(ls)
                dz = g * (1.0 - sig) - sig * big_g
                if masked:
                    dz = jnp.where(before, dz, 0.0)
                dz = dz * SCALE
                dk_t = dk_t + _tn(dz, qs[e])
                dv_t = dv_t + _tn(wr, dos[e])
                out_c.append(carries[e] + jnp.sum(lf, axis=1, keepdims=True))
                out_g.append(gcarries[e] + jnp.sum(g, axis=1, keepdims=True))
                out_q.append(dqs[e] + _nn(dz, kt))
            dk_acc[pl.ds(off, QB), :] += dk_t
            dv_acc[pl.ds(off, QB), :] += dv_t
            return out_c, out_g, out_q

        zc = [jnp.zeros((rq, 1), F32)] * 2
        carries, gcarries, dqs = zc, zc, [jnp.zeros((rq, LANES), F32)] * 2
        for a in range(SB2_SUB):
            carries, gcarries, dqs = tile(i * SB2_SUB + (SB2_SUB - 1 - a), carries, gcarries, dqs, True)

        def cond(st):
            return jnp.logical_and(st[0] >= 0, st[1] > 0)

        def body(st):
            carries, gcarries, dqs = tile(st[0], [st[2], st[3]], [st[4], st[5]], [st[6], st[7]], False)
            alive = jnp.maximum(jnp.max(carries[0]), jnp.max(carries[1])) > SB_DEAD
            return (st[0] - 1, alive.astype(jnp.int32), carries[0], carries[1], gcarries[0], gcarries[1], dqs[0], dqs[1])

        st = lax.while_loop(cond, body, (i * SB2_SUB - 1, jnp.int32(1), carries[0], carries[1], gcarries[0], gcarries[1],
                                         dqs[0], dqs[1]))
        dq_ref[...] = jnp.where(first, st[6], st[7])

        @pl.when(i == nq - 1)
        def _():
            cols = pl.ds(pl.multiple_of(pr * LANES, LANES), LANES)
            ck = pltpu.make_async_copy(dk_acc, dk_hbm.at[:, cols], sem.at[0])
            cv = pltpu.make_async_copy(dv_acc, dv_hbm.at[:, cols], sem.at[1])
            ck.start()
            cv.start()
            ck.wait()
            cv.wait()

    blk = pl.BlockSpec((rq, LANES), lambda a, i: (i, a))
    anyspace = pl.BlockSpec(memory_space=pl.ANY)
    shp = jax.ShapeDtypeStruct((s, SB_W), F32)
    return _call_2d(kern, name=name, grid=(npair, nq),
                    in_specs=[blk, pl.BlockSpec((s, LANES), lambda a, i: (0, npair + a)),
                              pl.BlockSpec((s, LANES), lambda a, i: (0, 2 * npair + a)), blk, blk],
                    out_specs=[blk, anyspace, anyspace], out_shape=[shp, shp, shp], ins=[p, p, p, o, do],
                    scratch_shapes=[pltpu.VMEM((s, LANES), F32), pltpu.VMEM((s, LANES), F32), pltpu.SemaphoreType.DMA((2,))],
                    semantics=("arbitrary", "arbitrary"), side=side)


def _sb_fwd(q, kt, vt, *, name):
    h, s, hd = q.shape
    rq = SB_SUB * QB
    nq = s // rq
    nb = s // QB

    def kern(q_ref, k_ref, v_ref, o_ref):
        i = pl.program_id(1)
        qb = q_ref[...]
        t2 = _tri(True)
        rel = lax.broadcasted_iota(jnp.int32, (rq, QB), 1) - lax.broadcasted_iota(jnp.int32, (rq, QB), 0)

        def tile(j, carry, acc, masked):
            ls, lf = _log2_sigmoids(_nn(qb, k_ref[j]) * (SCALE * LOG2E))
            if masked:
                before = rel < i * rq - j * QB
                lf = jnp.where(before, lf, 0.0)
            w = jnp.exp2(ls + _split_dot(lf, t2) + carry)
            if masked:
                w = jnp.where(before, w, 0.0)
            return carry + jnp.sum(lf, axis=1, keepdims=True), acc + _nt(w, v_ref[j])

        carry, acc = jnp.zeros((rq, 1), F32), jnp.zeros((rq, hd), F32)
        for a in range(SB_SUB):
            carry, acc = tile(i * SB_SUB + (SB_SUB - 1 - a), carry, acc, True)

        def cond(st):
            return jnp.logical_and(st[0] >= 0, st[1] > 0)

        def body(st):
            j, _, carry, acc = st
            carry, acc = tile(j, carry, acc, False)
            return j - 1, (jnp.max(carry) > SB_DEAD).astype(jnp.int32), carry, acc

        _, _, _, acc = lax.while_loop(cond, body, (i * SB_SUB - 1, jnp.int32(1), carry, acc))
        o_ref[...] = acc

    blk = pl.BlockSpec((None, rq, hd), lambda a, i: (a, i, 0))
    full = pl.BlockSpec((None, nb, hd, QB), lambda a, i: (a, 0, 0, 0))
    return _pcall(kern, name=name, grid=(h, nq), in_specs=[blk, full, full], out_specs=blk,
                  out_shape=jax.ShapeDtypeStruct((h, s, hd), F32), compiler_params=_params("parallel", "arbitrary"))(q, kt, vt)


def _sb_bwd(q, kt, vt, o, do, *, name):
    h, s, hd = q.shape
    rq = SB_SUB * QB
    nq = s // rq
    nb = s // QB

    def kern(q_ref, k_ref, v_ref, o_ref, do_ref, dq_ref, dk_ref, dv_ref):
        i = pl.program_id(1)

        @pl.when(i == 0)
        def _():
            dk_ref[...] = jnp.zeros_like(dk_ref)
            dv_ref[...] = jnp.zeros_like(dv_ref)

        qb = q_ref[...]
        dob = do_ref[...]
        dsum = jnp.sum(dob.astype(F32) * o_ref[...], axis=1, keepdims=True)
        t_strict = _tri(True)
        t_incl = _tri(False)
        rel = lax.broadcasted_iota(jnp.int32, (rq, QB), 1) - lax.broadcasted_iota(jnp.int32, (rq, QB), 0)

        def tile(j, carry, gcarry, dq, masked):
            kb = k_ref[j]
            ls, lf = _log2_sigmoids(_nn(qb, kb) * (SCALE * LOG2E))
            if masked:
                before = rel < i * rq - j * QB
                lf = jnp.where(before, lf, 0.0)
            w = jnp.exp2(ls + _split_dot(lf, t_strict) + carry)
            if masked:
                w = jnp.where(before, w, 0.0)
            wr = w.astype(MXU_DT)
            g = _nn(dob, v_ref[j]) * wr.astype(F32)
            big_g = dsum - (_split_dot(g, t_incl) + gcarry)
            sig = jnp.exp2(ls)
            dz = g * (1.0 - sig) - sig * big_g
            if masked:
                dz = jnp.where(before, dz, 0.0)
            dz = dz * SCALE
            dk_ref[j] += _tn(qb, dz)
            dv_ref[j] += _tn(dob, wr)
            return (carry + jnp.sum(lf, axis=1, keepdims=True), gcarry + jnp.sum(g, axis=1, keepdims=True),
                    dq + _nt(dz, kb))

        carry, gcarry, dq = jnp.zeros((rq, 1), F32), jnp.zeros((rq, 1), F32), jnp.zeros((rq, hd), F32)
        for a in range(SB_SUB):
            carry, gcarry, dq = tile(i * SB_SUB + (SB_SUB - 1 - a), carry, gcarry, dq, True)

        def cond(st):
            return jnp.logical_and(st[0] >= 0, st[1] > 0)

        def body(st):
            j, _, carry, gcarry, dq = st
            carry, gcarry, dq = tile(j, carry, gcarry, dq, False)
            return j - 1, (jnp.max(carry) > SB_DEAD).astype(jnp.int32), carry, gcarry, dq

        st = lax.while_loop(cond, body, (i * SB_SUB - 1, jnp.int32(1), carry, gcarry, dq))
        dq_ref[...] = st[4]

    blk = pl.BlockSpec((None, rq, hd), lambda a, i: (a, i, 0))
    full = pl.BlockSpec((None, nb, hd, QB), lambda a, i: (a, 0, 0, 0))
    kshape = jax.ShapeDtypeStruct((h, nb, hd, QB), F32)
    return _pcall(kern, name=name, grid=(h, nq), in_specs=[blk, full, full, blk, blk], out_specs=[blk, full, full],
                  out_shape=[jax.ShapeDtypeStruct((h, s, hd), F32), kshape, kshape],
                  compiler_params=_params("parallel", "arbitrary"))(q, kt, vt, o, do)


DSA_SUB = 4


def _dsa_seq_blocks(t, s):
    steps_per_group = 4 * s // (QB * DSA_SUB)
    g = t // steps_per_group
    b0, b1, b2 = (s // (QB * r) for _, r in DSA_GROUPS)
    return jnp.where(g == 0, b0, jnp.where(g == 1, b1, b2))


def _dsa_rel():
    qi = lax.broadcasted_iota(jnp.int32, (QB, QB), 0)
    kj = lax.broadcasted_iota(jnp.int32, (QB, QB), 1)
    return kj - qi


def _prev_mask(rel, has_prev):
    return rel >= jnp.where(has_prev, 0, QB)


def _dsa_fwd(q, k, vp, *, name):
    rows = q.shape[0]
    s = rows // 12
    big = QB * DSA_SUB
    nsteps = rows // big

    def kern(q_ref, k_ref, kp_ref, v_ref, vpv_ref, o_ref):
        t = pl.program_id(0)
        bps = _dsa_seq_blocks(t, s)
        rel = _dsa_rel()
        lane = lax.broadcasted_iota(jnp.int32, (QB, LANES), 1)
        for a in range(DSA_SUB):
            qa = q_ref[pl.ds(a * QB, QB), :]
            kc = k_ref[pl.ds(a * QB, QB), :]
            vc = v_ref[pl.ds(a * QB, QB), :]
            if a == 0:
                kpv, vpv = kp_ref[...], vpv_ref[...]
            else:
                kpv, vpv = k_ref[pl.ds((a - 1) * QB, QB), :], v_ref[pl.ds((a - 1) * QB, QB), :]
            has_prev = (t * DSA_SUB + a) % bps != 0
            sc = jnp.where(rel <= 0, _nt(qa, kc) * SCALE, -jnp.inf)
            sp = jnp.where(_prev_mask(rel, has_prev), _nt(qa, kpv) * SCALE, -jnp.inf)
            m = jnp.maximum(jnp.max(sc, axis=1, keepdims=True), jnp.max(sp, axis=1, keepdims=True))
            pc = jnp.exp(sc - m)
            pp = jnp.exp(sp - m)
            den = jnp.sum(pc, axis=1, keepdims=True) + jnp.sum(pp, axis=1, keepdims=True)
            o = (_nn(pc, vc) + _nn(pp, vpv)) / den
            o_ref[pl.ds(a * QB, QB), :] = jnp.where(lane < HEAD_DIM, o, m + jnp.log(den))

    cur64 = pl.BlockSpec((big, HEAD_DIM), lambda t: (t, 0))
    prev64 = pl.BlockSpec((QB, HEAD_DIM), lambda t: (jnp.maximum(t * DSA_SUB - 1, 0), 0))
    cur128 = pl.BlockSpec((big, LANES), lambda t: (t, 0))
    prev128 = pl.BlockSpec((QB, LANES), lambda t: (jnp.maximum(t * DSA_SUB - 1, 0), 0))
    return _pcall(kern, name=name, grid=(nsteps,), in_specs=[cur64, cur64, prev64, cur128, prev128], out_specs=cur128,
                  out_shape=jax.ShapeDtypeStruct((rows, LANES), F32), compiler_params=_params("parallel"))(q, k, k, vp, vp)


def _dsa_combine(p0, p1, p2, *, name):
    hh, s, _ = p0.shape
    ts = _pick(s, (512, 256))

    def kern(a_ref, b_ref, c_ref, o_ref):
        lane = lax.broadcasted_iota(jnp.int32, (ts, LANES), 1)
        xs = [a_ref[...], b_ref[...], c_ref[...]]
        ls = [jnp.where(lane < HEAD_DIM, pltpu.roll(x, HEAD_DIM, 1), x) for x in xs]
        m = jnp.maximum(jnp.maximum(ls[0], ls[1]), ls[2])
        es = [jnp.exp(l - m) for l in ls]
        den = es[0] + es[1] + es[2]
        o = (es[0] * xs[0] + es[1] * xs[1] + es[2] * xs[2]) / den
        o_ref[...] = jnp.where(lane < HEAD_DIM, o, m + jnp.log(den))

    blk = pl.BlockSpec((None, ts, LANES), lambda a, i: (a, i, 0))
    return _pcall(kern, name=name, grid=(hh, s // ts), in_specs=[blk, blk, blk], out_specs=blk,
                  out_shape=jax.ShapeDtypeStruct((hh, s, LANES), F32), compiler_params=_params("parallel", "parallel"))(p0, p1, p2)


def _dsa_bwd_prep(comb, dop, *, name):
    hh, s, _ = comb.shape
    ts = _pick(s, (512, 256))

    def kern(c_ref, d_ref, o_ref):
        lane = lax.broadcasted_iota(jnp.int32, (ts, LANES), 1)
        c = c_ref[...]
        d = d_ref[...]
        dsum = jnp.sum(jnp.where(lane < HEAD_DIM, c * d, 0.0), axis=1, keepdims=True)
        o_ref[...] = jnp.where(lane < HEAD_DIM, d, jnp.where(lane < HEAD_DIM + 32, c, dsum))

    blk = pl.BlockSpec((None, ts, LANES), lambda a, i: (a, i, 0))
    return _pcall(kern, name=name, grid=(hh, s // ts), in_specs=[blk, blk], out_specs=blk,
                  out_shape=jax.ShapeDtypeStruct((hh, s, LANES), F32), compiler_params=_params("parallel", "parallel"))(comb, dop)


def _dsa_bwd(q, k, vp, pk, *, name):
    rows = q.shape[0]
    s = rows // 12
    big = QB * DSA_SUB
    nsteps = rows // big
    nblk = rows // QB

    def kern(q_ref, qn_ref, k_ref, kp_ref, v_ref, vpv_ref, p_ref, pn_ref, dq_ref, dk_ref, dv_ref):
        t = pl.program_id(0)
        bps = _dsa_seq_blocks(t, s)
        rel = _dsa_rel()
        lane = lax.broadcasted_iota(jnp.int32, (QB, LANES), 1)

        def stats(pa):
            lse = jnp.max(jnp.where(jnp.logical_and(lane >= HEAD_DIM, lane < HEAD_DIM + 32), pa, -jnp.inf), axis=1, keepdims=True)
            dsum = jnp.max(jnp.where(lane >= HEAD_DIM + 32, pa, -jnp.inf), axis=1, keepdims=True)
            return lse, dsum

        def pair(qa, pa, st, kb, vb, mask):
            p = jnp.where(mask, jnp.exp(_nt(qa, kb) * SCALE - st[0]), 0.0)
            ds = p * (_nt(pa, vb) - st[1]) * SCALE
            return _nn(ds, kb), _tn(ds, qa), _tn(p, pa)

        for a in range(DSA_SUB):
            qa = q_ref[pl.ds(a * QB, QB), :]
            pa = p_ref[pl.ds(a * QB, QB), :]
            st = stats(pa)
            kc = k_ref[pl.ds(a * QB, QB), :]
            vc = v_ref[pl.ds(a * QB, QB), :]
            if a == 0:
                kpv, vpv = kp_ref[...], vpv_ref[...]
            else:
                kpv, vpv = k_ref[pl.ds((a - 1) * QB, QB), :], v_ref[pl.ds((a - 1) * QB, QB), :]
            has_prev = (t * DSA_SUB + a) % bps != 0
            dq_c, dk_c, dv_c = pair(qa, pa, st, kc, vc, rel <= 0)
            dq_p, dk_p, dv_p = pair(qa, pa, st, kpv, vpv, _prev_mask(rel, has_prev))
            dq_ref[pl.ds(a * QB, QB), :] = dq_c + dq_p
            if a == 0:
                dk_ref[pl.ds(0, QB), :] = dk_c
                dv_ref[pl.ds(0, QB), :] = dv_c
            else:
                dk_ref[pl.ds(a * QB, QB), :] = dk_c
                dv_ref[pl.ds(a * QB, QB), :] = dv_c
                dk_ref[pl.ds((a - 1) * QB, QB), :] += dk_p
                dv_ref[pl.ds((a - 1) * QB, QB), :] += dv_p
        nxt = t * DSA_SUB + DSA_SUB
        has_next = jnp.logical_and(nxt < nblk, nxt % bps != 0)
        last = (DSA_SUB - 1) * QB
        pn = pn_ref[...]
        _, dk_n, dv_n = pair(qn_ref[...], pn, stats(pn), k_ref[pl.ds(last, QB), :], v_ref[pl.ds(last, QB), :],
                             _prev_mask(rel, has_next))
        dk_ref[pl.ds(last, QB), :] += dk_n
        dv_ref[pl.ds(last, QB), :] += dv_n

    def prev_map(t):
        return (jnp.maximum(t * DSA_SUB - 1, 0), 0)

    def next_map(t):
        return (jnp.minimum(t * DSA_SUB + DSA_SUB, nblk - 1), 0)

    cur64 = pl.BlockSpec((big, HEAD_DIM), lambda t: (t, 0))
    cur128 = pl.BlockSpec((big, LANES), lambda t: (t, 0))
    specs = [cur64, pl.BlockSpec((QB, HEAD_DIM), next_map), cur64, pl.BlockSpec((QB, HEAD_DIM), prev_map),
             cur128, pl.BlockSpec((QB, LANES), prev_map), cur128, pl.BlockSpec((QB, LANES), next_map)]
    return _pcall(kern, name=name, grid=(nsteps,), in_specs=specs, out_specs=[cur64, cur64, cur128],
                  out_shape=[jax.ShapeDtypeStruct((rows, HEAD_DIM), F32), jax.ShapeDtypeStruct((rows, HEAD_DIM), F32),
                             jax.ShapeDtypeStruct((rows, LANES), F32)],
                  compiler_params=_params("parallel"))(q, q, k, k, vp, vp, pk, pk)


def _mem_fwd(q, km, vm, *, name):
    hh, s, hd = q.shape
    ml = km.shape[1]
    tq = _pick(s, (512, 256))

    def kern(q_ref, k_ref, v_ref, o_ref):
        sc = _nt(q_ref[...], k_ref[...]) * SCALE
        e = jnp.exp(sc - jnp.max(sc, axis=1, keepdims=True))
        p = e / jnp.sum(e, axis=1, keepdims=True)
        o_ref[...] = _nn(p, v_ref[...])

    blk = pl.BlockSpec((None, tq, hd), lambda a, i: (a, i, 0))
    kv = pl.BlockSpec((None, ml, hd), lambda a, i: (a, 0, 0))
    return _pcall(kern, name=name, grid=(hh, s // tq), in_specs=[blk, kv, kv], out_specs=blk,
                  out_shape=jax.ShapeDtypeStruct((hh, s, hd), F32), compiler_params=_params("parallel", "parallel"))(q, km, vm)


def _mem_bwd(q, km, vm, do, *, name):
    hh, s, hd = q.shape
    ml = km.shape[1]
    tq = _pick(s, (512, 256))

    def kern(q_ref, k_ref, v_ref, do_ref, dq_ref, dk_ref, dv_ref):
        @pl.when(pl.program_id(1) == 0)
        def _():
            dk_ref[...] = jnp.zeros_like(dk_ref)
            dv_ref[...] = jnp.zeros_like(dv_ref)

        qb = q_ref[...]
        dob = do_ref[...]
        sc = _nt(qb, k_ref[...]) * SCALE
        e = jnp.exp(sc - jnp.max(sc, axis=1, keepdims=True))
        p = e / jnp.sum(e, axis=1, keepdims=True)
        dp = _nt(dob, v_ref[...])
        ds = p * (dp - jnp.sum(p * dp, axis=1, keepdims=True)) * SCALE
        dq_ref[...] = _nn(ds, k_ref[...])
        dk_ref[...] += _tn(ds, qb)
        dv_ref[...] += _tn(p, dob)

    blk = pl.BlockSpec((None, tq, hd), lambda a, i: (a, i, 0))
    kv = pl.BlockSpec((None, ml, hd), lambda a, i: (a, 0, 0))
    kvs = jax.ShapeDtypeStruct((hh, ml, hd), F32)
    return _pcall(kern, name=name, grid=(hh, s // tq), in_specs=[blk, kv, kv, blk], out_specs=[blk, kv, kv],
                  out_shape=[jax.ShapeDtypeStruct((hh, s, hd), F32), kvs, kvs],
                  compiler_params=_params("parallel", "arbitrary"))(q, km, vm, do)


DSA_BT = QB * max(r for _, r in DSA_GROUPS)
DSA_UB = 4


def _bdot(a, b, ca, cb):
    return lax.dot_general(a.astype(MXU_DT), b.astype(MXU_DT), (((ca,), (cb,)), ((0,), (0,))), preferred_element_type=F32)


def _bnt(a, b):
    return _bdot(a, b, 2, 2)


def _bnn(a, b):
    return _bdot(a, b, 2, 1)


def _btn(a, b):
    return _bdot(a, b, 1, 1)


def _unit_rows(r, c, b):
    return pl.ds(c + QB * r * b, QB, stride=r)


def _pair_cols(t, first):
    return [jnp.max(jnp.where(first, t, -jnp.inf), axis=1, keepdims=True),
            jnp.max(jnp.where(first, -jnp.inf, t), axis=1, keepdims=True)]


def _dsa2_fwd(qn, kn, v32, g, *, name):
    s = qn.shape[0]
    r = DSA_GROUPS[g][1]
    nbk = DSA_BT // (QB * r)
    npair = DSA_OUT_W // LANES

    def kern(q_ref, k_ref, kp_ref, v_ref, vp_ref, o_ref, l_ref):
        t = pl.program_id(1)
        first = _first_half((QB, LANES))
        rel = _dsa_rel()
        units = [(c, b) for c in range(r) for b in range(nbk)]
        for u0 in range(0, len(units), DSA_UB):
            batch = units[u0:u0 + DSA_UB]
            qs, kcs, vcs, kps, vps, masks = [], [], [], [], [], []
            for c, b in batch:
                rows = _unit_rows(r, c, b)
                kc, vc = k_ref[rows, :].astype(MXU_DT), v_ref[rows, :].astype(MXU_DT)
                if b > 0:
                    prow = _unit_rows(r, c, b - 1)
                    kpv, vpv, has_prev = k_ref[prow, :], v_ref[prow, :], True
                else:
                    prow = _unit_rows(r, c, nbk - 1)
                    kpv, vpv, has_prev = kp_ref[prow, :], vp_ref[prow, :], t > 0
                for qe in _split_pair(q_ref[rows, :], first):
                    qs.append(qe.astype(MXU_DT))
                    kcs.append(kc)
                    vcs.append(vc)
                    kps.append(kpv.astype(MXU_DT))
                    vps.append(vpv.astype(MXU_DT))
                    masks.append(_prev_mask(rel, has_prev))
            qq = jnp.stack(qs)
            sc = jnp.where(rel <= 0, _bnt(qq, jnp.stack(kcs)) * SCALE, -jnp.inf)
            sp = _bnt(qq, jnp.stack(kps)) * SCALE
            sp = jnp.stack([jnp.where(mk, sp[n], -jnp.inf) for n, mk in enumerate(masks)])
            m = jnp.maximum(jnp.max(sc, axis=2, keepdims=True), jnp.max(sp, axis=2, keepdims=True))
            pc = jnp.exp(sc - m)
            pp = jnp.exp(sp - m)
            den = jnp.sum(pc, axis=2, keepdims=True) + jnp.sum(pp, axis=2, keepdims=True)
            out = (_bnn(pc, jnp.stack(vcs)) + _bnn(pp, jnp.stack(vps))) / den
            lse = m + jnp.log(den)
            for idx, (c, b) in enumerate(batch):
                rows = _unit_rows(r, c, b)
                o_ref[rows, :] = jnp.where(first, out[2 * idx], out[2 * idx + 1])
                l_ref[rows, :] = jnp.where(first, lse[2 * idx], lse[2 * idx + 1])

    npg = DSA_HPG * HEAD_DIM // LANES
    cur = pl.BlockSpec((DSA_BT, LANES), lambda a, t: (t, npg * g + a))
    prev = pl.BlockSpec((DSA_BT, LANES), lambda a, t: (jnp.maximum(t - 1, 0), npg * g + a))
    out = pl.BlockSpec((DSA_BT, LANES), lambda a, t: (t, a))
    shp = jax.ShapeDtypeStruct((s, DSA_OUT_W), F32)
    return _pcall(kern, name=name, grid=(npair, s // DSA_BT), in_specs=[cur, cur, prev, cur, prev], out_specs=[out, out],
                  out_shape=[shp, shp], compiler_params=_params("parallel", "parallel"))(qn, kn, kn, v32, v32)


def _dsa2_combine(parts, *, name):
    s, wd = parts[0][0].shape
    ts = _pick(s, (512, 256))

    def kern(o0, l0, o1, l1, o2, l2, o_ref, l_ref):
        ls = [l0[...], l1[...], l2[...]]
        m = jnp.maximum(jnp.maximum(ls[0], ls[1]), ls[2])
        es = [jnp.exp(l - m) for l in ls]
        den = es[0] + es[1] + es[2]
        o_ref[...] = (es[0] * o0[...] + es[1] * o1[...] + es[2] * o2[...]) / den
        l_ref[...] = m + jnp.log(den)

    blk = pl.BlockSpec((ts, wd), lambda i: (i, 0))
    shp = jax.ShapeDtypeStruct((s, wd), F32)
    flat = [t for pair in parts for t in pair]
    return _pcall(kern, name=name, grid=(s // ts,), in_specs=[blk] * 6, out_specs=[blk, blk], out_shape=[shp, shp],
                  compiler_params=_params("parallel"))(*flat)


def _dsa2_prep(o, do, *, name):
    s, wd = o.shape
    ts = _pick(s, (512, 256))

    def kern(o_ref, do_ref, d_ref):
        d_ref[...] = _head_mean(do_ref[...] * o_ref[...], _block_diag(wd)) * HEAD_DIM

    blk = pl.BlockSpec((ts, wd), lambda i: (i, 0))
    return _pcall(kern, name=name, grid=(s // ts,), in_specs=[blk, blk], out_specs=blk,
                  out_shape=jax.ShapeDtypeStruct((s, wd), F32), compiler_params=_params("parallel"))(o, do)


def _dsa2_bwd(qn, kn, v32, do, lse, dd, g, *, name):
    s = qn.shape[0]
    r = DSA_GROUPS[g][1]
    nbk = DSA_BT // (QB * r)
    npair = DSA_OUT_W // LANES
    nsteps = s // DSA_BT

    def kern(q_ref, qn_ref, k_ref, kp_ref, v_ref, vp_ref, do_ref, don_ref, l_ref, ln_ref, d_ref, dn_ref,
             dq_ref, dk_ref, dv_ref):
        t = pl.program_id(1)
        first = _first_half((QB, LANES))
        rel = _dsa_rel()

        def pairs(items):
            qq = jnp.stack([it[0].astype(MXU_DT) for it in items])
            dd = jnp.stack([it[1].astype(MXU_DT) for it in items])
            kk = jnp.stack([it[4].astype(MXU_DT) for it in items])
            vv = jnp.stack([it[5].astype(MXU_DT) for it in items])
            p = jnp.exp(_bnt(qq, kk) * SCALE - jnp.stack([it[2] for it in items]))
            p = jnp.stack([jnp.where(it[6], p[n], 0.0) for n, it in enumerate(items)])
            ds = p * (_bnt(dd, vv) - jnp.stack([it[3] for it in items])) * SCALE
            return _bnn(ds, kk), _btn(ds, qq), _btn(p, dd)

        def heads(rows, qr, dor, lr, dr):
            return list(zip(_split_pair(qr[rows, :], first), _split_pair(dor[rows, :], first),
                            _pair_cols(lr[rows, :], first), _pair_cols(dr[rows, :], first)))

        units = [(c, b) for c in range(r) for b in range(nbk)]
        dk_of, dv_of = [None] * len(units), [None] * len(units)
        for u0 in range(0, len(units), DSA_UB // 2):
            batch = list(enumerate(units))[u0:u0 + DSA_UB // 2]
            items = []
            for u, (c, b) in batch:
                rows = _unit_rows(r, c, b)
                kc, vc = k_ref[rows, :], v_ref[rows, :]
                if b > 0:
                    prow = _unit_rows(r, c, b - 1)
                    kpv, vpv, pmask = k_ref[prow, :], v_ref[prow, :], _prev_mask(rel, True)
                else:
                    prow = _unit_rows(r, c, nbk - 1)
                    kpv, vpv, pmask = kp_ref[prow, :], vp_ref[prow, :], _prev_mask(rel, t > 0)
                for hd in heads(rows, q_ref, do_ref, l_ref, d_ref):
                    items.append(hd + (kc, vc, rel <= 0))
                    items.append(hd + (kpv, vpv, pmask))
            dq, dk, dv = pairs(items)
            for n, (u, (c, b)) in enumerate(batch):
                dq_ref[_unit_rows(r, c, b), :] = jnp.where(first, dq[4 * n] + dq[4 * n + 1], dq[4 * n + 2] + dq[4 * n + 3])
                dk_of[u] = dk[4 * n] + dk[4 * n + 2]
                dv_of[u] = dv[4 * n] + dv[4 * n + 2]
                if b > 0:
                    dk_of[u - 1] = dk_of[u - 1] + (dk[4 * n + 1] + dk[4 * n + 3])
                    dv_of[u - 1] = dv_of[u - 1] + (dv[4 * n + 1] + dv[4 * n + 3])
        lasts = [c * nbk + nbk - 1 for c in range(r)]
        for c0 in range(0, r, DSA_UB):
            chunk = list(range(c0, min(c0 + DSA_UB, r)))
            items = []
            for c in chunk:
                last = _unit_rows(r, c, nbk - 1)
                for hd in heads(_unit_rows(r, c, 0), qn_ref, don_ref, ln_ref, dn_ref):
                    items.append(hd + (k_ref[last, :], v_ref[last, :], _prev_mask(rel, t < nsteps - 1)))
            _, dk, dv = pairs(items)
            for n, c in enumerate(chunk):
                dk_of[lasts[c]] = dk_of[lasts[c]] + (dk[2 * n] + dk[2 * n + 1])
                dv_of[lasts[c]] = dv_of[lasts[c]] + (dv[2 * n] + dv[2 * n + 1])
        for u, (c, b) in enumerate(units):
            dk_ref[_unit_rows(r, c, b), :] = dk_of[u]
            dv_ref[_unit_rows(r, c, b), :] = dv_of[u]

    npg = DSA_HPG * HEAD_DIM // LANES

    def at(shift, col):
        return pl.BlockSpec((DSA_BT, LANES), lambda a, t: (jnp.clip(t + shift, 0, nsteps - 1), col(a)))

    gcol = lambda a: npg * g + a
    ocol = lambda a: a
    specs = [at(0, gcol), at(1, gcol), at(0, gcol), at(-1, gcol), at(0, gcol), at(-1, gcol),
             at(0, ocol), at(1, ocol), at(0, ocol), at(1, ocol), at(0, ocol), at(1, ocol)]
    shp = jax.ShapeDtypeStruct((s, DSA_OUT_W), F32)
    return _pcall(kern, name=name, grid=(npair, nsteps), in_specs=specs, out_specs=[at(0, ocol)] * 3, out_shape=[shp, shp, shp],
                  compiler_params=_params("parallel", "parallel"))(qn, qn, kn, kn, v32, v32, do, do, lse, lse, dd, dd)


def _mem2_fwd(qn, km, kv, *, name):
    s = qn.shape[0]
    ml = km.shape[0]
    tq = _pick(s, (512, 256))
    npair = MEM_W // LANES

    def kern(q_ref, k_ref, v_ref, o_ref):
        first = _first_half((tq, LANES))
        outs = []
        for qe in _split_pair(q_ref[...], first):
            sc = _nt(qe, k_ref[...]) * SCALE
            e = jnp.exp(sc - jnp.max(sc, axis=1, keepdims=True))
            outs.append(_nn(e / jnp.sum(e, axis=1, keepdims=True), v_ref[...]))
        o_ref[...] = jnp.where(first, outs[0], outs[1])

    blk = pl.BlockSpec((tq, LANES), lambda a, i: (i, a))
    return _pcall(kern, name=name, grid=(npair, s // tq),
                  in_specs=[blk, pl.BlockSpec((ml, LANES), lambda a, i: (0, a)), pl.BlockSpec((ml, LANES), lambda a, i: (0, npair + a))],
                  out_specs=blk, out_shape=jax.ShapeDtypeStruct((s, MEM_W), F32),
                  compiler_params=_params("parallel", "parallel"))(qn, km, kv)


def _mem2_bwd(qn, km, kv, do, *, name):
    s = qn.shape[0]
    ml = km.shape[0]
    tq = _pick(s, (512, 256))
    npair = MEM_W // LANES

    def kern(q_ref, k_ref, v_ref, do_ref, dq_ref, dk_ref, dv_ref):
        @pl.when(pl.program_id(1) == 0)
        def _():
            dk_ref[...] = jnp.zeros_like(dk_ref)
            dv_ref[...] = jnp.zeros_like(dv_ref)

        first = _first_half((tq, LANES))
        dqs = []
        for qe, doe in zip(_split_pair(q_ref[...], first), _split_pair(do_ref[...], first)):
            sc = _nt(qe, k_ref[...]) * SCALE
            e = jnp.exp(sc - jnp.max(sc, axis=1, keepdims=True))
            p = e / jnp.sum(e, axis=1, keepdims=True)
            dp = _nt(doe, v_ref[...])
            ds = p * (dp - jnp.sum(p * dp, axis=1, keepdims=True)) * SCALE
            dqs.append(_nn(ds, k_ref[...]))
            dk_ref[...] += _tn(ds, qe)
            dv_ref[...] += _tn(p, doe)
        dq_ref[...] = jnp.where(first, dqs[0], dqs[1])

    blk = pl.BlockSpec((tq, LANES), lambda a, i: (i, a))
    kblk = pl.BlockSpec((ml, LANES), lambda a, i: (0, a))
    kshape = jax.ShapeDtypeStruct((ml, MEM_W), F32)
    return _pcall(kern, name=name, grid=(npair, s // tq),
                  in_specs=[blk, kblk, pl.BlockSpec((ml, LANES), lambda a, i: (0, npair + a)), blk],
                  out_specs=[blk, kblk, kblk], out_shape=[jax.ShapeDtypeStruct((s, MEM_W), F32), kshape, kshape],
                  compiler_params=_params("parallel", "arbitrary"))(qn, km, kv, do)


def _merge_fwd(logits, bias, ya, yb, yc, *, name):
    s, d = ya.shape
    ts = _pick(s, (512, 256))

    def kern(l0, l1, l2, b0, b1, b2, a_ref, b_ref, c_ref, o_ref):
        m = (_sigmoid(l0[...] + b0[...]) * a_ref[...] + _sigmoid(l1[...] + b1[...]) * b_ref[...]
             + _sigmoid(l2[...] + b2[...]) * c_ref[...])
        o_ref[...] = m.astype(o_ref.dtype)

    row = pl.BlockSpec((ts, d), lambda i: (i, 0))
    lg = [pl.BlockSpec((ts, d), functools.partial(lambda i, c: (i, c), c=c)) for c in range(3)]
    bs = [pl.BlockSpec((1, d), functools.partial(lambda i, c: (0, c), c=c)) for c in range(3)]
    return _pcall(kern, name=name, grid=(s // ts,), in_specs=lg + bs + [row, row, row], out_specs=row,
                  out_shape=jax.ShapeDtypeStruct((s, d), BF16),
                  compiler_params=_params("parallel"))(logits, logits, logits, bias, bias, bias, ya, yb, yc)


def _merge_bwd(logits, bias, ya, yb, yc, dm, *, name):
    s, d = ya.shape
    ts = _pick(s, (256,))

    def kern(l0, l1, l2, b0, b1, b2, a_ref, b_ref, c_ref, dm_ref, da_ref, db_ref, dc_ref, dl0, dl1, dl2, dbias0, dbias1, dbias2):
        first = pl.program_id(0) == 0
        dmv = dm_ref[...]
        for l_ref, bb_ref, y_ref, dy_ref, dl_ref, dbias_ref in ((l0, b0, a_ref, da_ref, dl0, dbias0), (l1, b1, b_ref, db_ref, dl1, dbias1),
                                                                (l2, b2, c_ref, dc_ref, dl2, dbias2)):
            g = _sigmoid(l_ref[...] + bb_ref[...])
            dy_ref[...] = (dmv * g).astype(dy_ref.dtype)
            dl = dmv * y_ref[...] * g * (1.0 - g)
            dl_ref[...] = dl.astype(dl_ref.dtype)

            @pl.when(first)
            def _():
                dbias_ref[...] = jnp.zeros_like(dbias_ref)

            dbias_ref[...] += jnp.sum(dl, axis=0, keepdims=True)

    row = pl.BlockSpec((ts, d), lambda i: (i, 0))
    lg = [pl.BlockSpec((ts, d), functools.partial(lambda i, c: (i, c), c=c)) for c in range(3)]
    bs = [pl.BlockSpec((1, d), functools.partial(lambda i, c: (0, c), c=c)) for c in range(3)]
    vec = pl.BlockSpec((1, d), lambda i: (0, 0))
    yshape = jax.ShapeDtypeStruct((s, d), BF16)
    vshape = jax.ShapeDtypeStruct((1, d), F32)
    outs = _pcall(kern, name=name, grid=(s // ts,), in_specs=lg + bs + [row, row, row, row],
                  out_specs=[row, row, row, row, row, row, vec, vec, vec],
                  out_shape=[yshape] * 6 + [vshape] * 3,
                  compiler_params=_params("arbitrary"))(logits, logits, logits, bias, bias, bias, ya, yb, yc, dm)
    return outs[0], outs[1], outs[2], outs[3:6], jnp.concatenate(outs[6:9], axis=1)


def _heads(t, n):
    s = t.shape[0]
    return t.reshape(s, n, HEAD_DIM).transpose(1, 0, 2)


def _unheads(t):
    n, s, hd = t.shape
    return t.transpose(1, 0, 2).reshape(s, n * hd)


def _to_class_major(t):
    s = t.shape[0]
    w = t.shape[1] // (DSA_HPG * len(DSA_GROUPS))
    parts = []
    for g, (_, r) in enumerate(DSA_GROUPS):
        tg = t[:, g * DSA_HPG * w:(g + 1) * DSA_HPG * w].reshape(s // r, r, DSA_HPG, w)
        parts.append(tg.transpose(2, 1, 0, 3).reshape(DSA_HPG * s, w))
    return jnp.concatenate(parts, axis=0)


def _slot_to_class_major(t):
    hh, s, w = t.shape
    parts = []
    for _, r in DSA_GROUPS:
        parts.append(t.reshape(hh, s // r, r, w).transpose(0, 2, 1, 3).reshape(hh * s, w))
    return jnp.concatenate(parts, axis=0)


def _from_class_major(t):
    rows, w = t.shape
    s = rows // 12
    out = []
    for g, (_, r) in enumerate(DSA_GROUPS):
        tg = t[g * 4 * s:(g + 1) * 4 * s].reshape(DSA_HPG, r, s // r, w)
        out.append(tg.transpose(0, 2, 1, 3).reshape(DSA_HPG, s, w))
    return out


def _pad_lanes(t):
    return jnp.concatenate([t, jnp.zeros(t.shape[:-1] + (LANES - t.shape[-1],), t.dtype)], axis=-1)


G_FFN1 = ['ffn1_w1', 'ffn1_w3', 'ffn1_w2']
G_FFN2 = ['ffn2_w1', 'ffn2_w3', 'ffn2_w2']
G_MID = [n for n in BIG if n not in G_FFN1 + G_FFN2]


def _ffn_fwd(h, w1, w3, w2, tag, epilogue, side=None):
    carried = None
    if side is None:
        a, b, f = _ffn_up(h, w1, w3, name=f"{tag}_up")
    else:
        (a, b, f), carried = _ffn_up(h, w1, w3, name=f"{tag}_up", side=side)
    outs = _matmul(f, w2, name=f"{tag}_down", alpha=0.5, tm=512, tn=1024, tk=2816, epilogue=epilogue)
    return outs, (h, a, b, f), carried


def _ffn_bwd(x, norm, w1, w3, w2, saved, dy, dyb, tag, side=None, own_side=None):
    h, a, b, f = saved
    dw2 = _matmul(f, dyb, name=f"{tag}_dw2", ta=True, alpha=0.5, tm=1408, tn=1024, tk=2048)
    carried = None
    if side is None:
        da, db = _ffn_dact(dyb, w2, a, b, name=f"{tag}_dact")
    else:
        (da, db), carried = _ffn_dact(dyb, w2, a, b, name=f"{tag}_dact", side=side)
    dw1 = _matmul(h, da, name=f"{tag}_dw1", ta=True, tm=1024, tn=1408, tk=2048)
    dw3 = _matmul(h, db, name=f"{tag}_dw3", ta=True, tm=1024, tn=1408, tk=2048)
    outs = _matmul(da, w1, name=f"{tag}_dh", tb=True, tm=512, tn=1024, tk=1408, pair2=(db, w3),
                   epilogue=(_epi_rms_bwd, [x, dy], [norm], [F32, BF16], 1),
                   side=None if own_side is None else own_side(dw1, dw3, dw2))
    (dx, dxb, dnorm), own = outs if own_side is not None else (outs, None)
    return dx, dxb, dnorm, dw1, dw3, dw2, carried, own


def _local_step(x, mem, loss_target, wl, ws):
    s, d = x.shape
    assert s % (QB * 16) == 0
    rope = _rope_tables(s)
    bf = {n: wl[n].astype(BF16) for n in BIG}
    w = dict(ws)
    w.update(_unpack_gathered(_exchange(_pack_rows(bf, G_FFN1), _two_level_phases(), name="gather_ffn1"), wl, G_FFN1))

    h1 = _rms_fwd(x, w['ffn1_norm'], name="ffn1_rms")
    (x1, h), sv1, late = _ffn_fwd(h1, w['ffn1_w1'], w['ffn1_w3'], w['ffn1_w2'], "ffn1",
                                  (_epi_residual_rms, [x], [w['mix_norm']], [F32, BF16], 0),
                                  side=_side(_pack_rows(bf, G_MID), _two_level_phases()))
    w.update(_unpack_gathered(late, wl, G_MID))
    p = _matmul(h, w['w_in'], name="in_proj", out_dtype=BF16, tn=1024)
    logits = _matmul(h, w['w_gate'], name="gate_proj", tn=1024)
    c_qb, c_kb, c_vb, c_qc = 3 * SB_W, 3 * SB_W + DSA_W, 3 * SB_W + 2 * DSA_W, 3 * SB_W + 3 * DSA_W

    oa_t, late = _sb2_fwd(p, name="sb_fwd", side=_side(_pack_rows(bf, G_FFN2), _two_level_phases()))
    w.update(_unpack_gathered(late, wl, G_FFN2))
    ya = _matmul(oa_t, w['w_branch_sb'], name="sb_out")

    qb_n = _qknorm_fwd(p, c_qb, DSA_W, w['qn_dsa'], rope, name="dsa_qnorm", out_dtype=F32)
    kb_n = _qknorm_fwd(p, c_kb, DSA_W, w['kn_dsa'], rope, name="dsa_knorm", out_dtype=F32)
    vb32 = p[:, c_vb:c_vb + DSA_W].astype(F32)
    groups = range(len(DSA_GROUPS))
    ob_t, lse_b = _dsa2_combine([_dsa2_fwd(qb_n, kb_n, vb32, gi, name=f"dsa_fwd{gi}") for gi in groups], name="dsa_combine")
    yb = _matmul(ob_t, w['w_branch_dsa'], name="dsa_out")

    memh = _rms_fwd(mem, w['mem_norm'], name="mem_rms")
    kv = _matmul(memh, w['w_mem_kv'], name="mem_kv", out_dtype=BF16)
    km_n = _qknorm_fwd(kv, 0, MEM_W, w['kn_mem'], None, name="mem_knorm")
    qc_n = _qknorm_fwd(p, c_qc, MEM_W, w['qn_mem'], None, name="mem_qnorm")
    oc_t = _mem2_fwd(qc_n, km_n, kv, name="mem_fwd")
    yc = _matmul(oc_t, w['w_branch_mem'], name="mem_out")

    merged = _merge_fwd(logits, w['b_gate'], ya, yb, yc, name="merge")
    x2, h2 = _matmul(merged, w['w_out'], name="out_proj", tn=1024,
                     epilogue=(_epi_residual_rms, [x1], [w['ffn2_norm']], [F32, BF16], 0))
    (dx3, dx3b, sq), sv2, _ = _ffn_fwd(h2, w['ffn2_w1'], w['ffn2_w3'], w['ffn2_w2'], "ffn2",
                                       (_epi_loss, [x2, loss_target], [], [F32, BF16], 1))
    loss = jnp.sum(sq) * (0.5 / d)

    g, recv = {}, {}

    def owners(names):
        return _pack_for_owners(g, wl, names).astype(BF16)

    dx2, dx2b, g['ffn2_norm'], g['ffn2_w1'], g['ffn2_w3'], g['ffn2_w2'], _, _ = _ffn_bwd(
        x2, w['ffn2_norm'], w['ffn2_w1'], w['ffn2_w3'], w['ffn2_w2'], sv2, dx3, dx3b, "ffn2")

    g['w_out'] = _matmul(merged, dx2b, name="d_w_out", ta=True, tn=1024, tk=512)
    dm = _matmul(dx2b, w['w_out'], name="d_merged", tb=True, tn=1024)
    dya, dyb, dyc, dlog, g['b_gate'] = _merge_bwd(logits, w['b_gate'], ya, yb, yc, dm, name="d_merge")
    dlogits = jnp.concatenate(dlog, axis=1)

    g['w_branch_sb'] = _matmul(oa_t, dya, name="d_w_sb", ta=True, tn=1024, tk=512)
    g['w_branch_dsa'] = _matmul(ob_t, dyb, name="d_w_dsa", ta=True, tk=512)
    g['w_branch_mem'] = _matmul(oc_t, dyc, name="d_w_mem", ta=True, tk=512)
    doa = _matmul(dya, w['w_branch_sb'], name="d_oa", tb=True, out_dtype=BF16)
    dob = _matmul(dyb, w['w_branch_dsa'], name="d_ob", tb=True)
    doc = _matmul(dyc, w['w_branch_mem'], name="d_oc", tb=True, out_dtype=BF16)

    (dqa, dka, dva), recv['ffn2'] = _sb2_bwd(p, oa_t, doa, name="sb_bwd", side=_side(owners(G_FFN2), _direct_phases(True)))

    dd_b = _dsa2_prep(ob_t, dob, name="dsa_prep")
    dgrp = [_dsa2_bwd(qb_n, kb_n, vb32, dob, lse_b, dd_b, gi, name=f"dsa_bwd{gi}") for gi in groups]
    dvb = jnp.concatenate([t[2] for t in dgrp], axis=1).astype(BF16)
    dqb, g['qn_dsa'] = _qknorm_bwd(p, c_qb, DSA_W, w['qn_dsa'], rope, [t[0] for t in dgrp], name="d_dsa_qnorm")
    dkb, g['kn_dsa'] = _qknorm_bwd(p, c_kb, DSA_W, w['kn_dsa'], rope, [t[1] for t in dgrp], name="d_dsa_knorm")

    dqc_n, dkm_n, dvm = _mem2_bwd(qc_n, km_n, kv, doc, name="mem_bwd")
    dqc, g['qn_mem'] = _qknorm_bwd(p, c_qc, MEM_W, w['qn_mem'], None, dqc_n, name="d_mem_qnorm")
    dkm, g['kn_mem'] = _qknorm_bwd(kv, 0, MEM_W, w['kn_mem'], None, dkm_n, name="d_mem_knorm")
    dkv = jnp.concatenate([dkm, dvm.astype(BF16)], axis=1)
    g['w_mem_kv'] = _matmul(memh, dkv, name="d_w_mem_kv", ta=True)
    dmemh = _matmul(dkv, w['w_mem_kv'], name="d_memh", tb=True)
    _, _, g['mem_norm'] = _rms_bwd(mem, w['mem_norm'], dmemh, None, name="d_mem_rms")

    dp = jnp.concatenate([dqa.astype(BF16), dka.astype(BF16), dva.astype(BF16),
                          dqb, dkb, dvb, dqc], axis=1)
    g['w_in'] = _matmul(h, dp, name="d_w_in", ta=True, tn=2048, tk=1024)
    g['w_gate'] = _matmul(h, dlogits, name="d_w_gate", ta=True, tn=1536, tk=1024)
    dh = _matmul(dp, w['w_in'], name="d_h_in", tb=True, tn=1024, tk=2048)
    dx1, dx1b, g['mix_norm'] = _matmul(dlogits, w['w_gate'], name="d_h_gate", tb=True, tm=512, tn=1024, tk=3072,
                                       epilogue=(_epi_rms_bwd_sum, [dh, x1, dx2], [w['mix_norm']], [F32, BF16], 1))

    def own_side(dw1, dw3, dw2):
        g.update(ffn1_w1=dw1, ffn1_w3=dw3, ffn1_w2=dw2)
        return _side(owners(G_FFN1), _direct_phases(True))

    dx0, _, g['ffn1_norm'], _, _, _, recv['mid'], recv['ffn1'] = _ffn_bwd(
        x, w['ffn1_norm'], w['ffn1_w1'], w['ffn1_w3'], w['ffn1_w2'], sv1, dx1, dx1b, "ffn1",
        side=_side(owners(G_MID), _direct_phases(True)), own_side=own_side)
    return loss, dx0, recv, {n: g[n] for n in SMALL}


def _pack_rows(d, names):
    return jnp.concatenate([d[n].reshape(-1, LANES) for n in names], axis=0)


def _unpack_rows(t, like, names):
    out, off = {}, 0
    for n in names:
        r = like[n].size // LANES
        out[n] = t[off:off + r].reshape(like[n].shape)
        off += r
    return out


def _unpack_gathered(t, local, names):
    out, off = {}, 0
    for n in names:
        r, c = local[n].shape
        rows = r * c // LANES
        blk = t[:, off:off + rows].reshape(N_DEV, r, c)
        out[n] = blk.reshape(N_DEV * r, c) if SHARD_AXIS[n] == 0 else blk.transpose(1, 0, 2).reshape(r, N_DEV * c)
        off += rows
    return out


def _pack_for_owners(g, local, names):
    parts = []
    for n in names:
        r, c = local[n].shape
        blk = g[n].reshape(N_DEV, r, c) if SHARD_AXIS[n] == 0 else g[n].reshape(r, N_DEV, c).transpose(1, 0, 2)
        parts.append(blk.reshape(N_DEV, r * c // LANES, LANES))
    return jnp.concatenate(parts, axis=1)


def _pack_small(d, names, extra_rows):
    parts = []
    for n in names:
        v = d[n].reshape(-1)
        pad = (-v.size) % LANES
        parts.append(jnp.concatenate([v, jnp.zeros((pad,), v.dtype)]).reshape(-1, LANES))
    t = jnp.concatenate(parts, axis=0)
    return jnp.concatenate([t, jnp.zeros((extra_rows, LANES), t.dtype)], axis=0)


def _unpack_small(t, like, names):
    out, off = {}, 0
    for n in names:
        size = like[n].size
        rows = -(-size // LANES)
        out[n] = t[off:off + rows].reshape(-1)[:size].reshape(like[n].shape)
        off += rows
    return out


def _direct_phases(per_peer):
    def descriptors(src_ref, out_ref, send_sems, recv_sems, local_sem):
        x, y, c = lax.axis_index("x"), lax.axis_index("y"), lax.axis_index("c")
        me = 4 * x + 2 * y + c
        mine = pltpu.make_async_copy(src_ref.at[me] if per_peer else src_ref, out_ref.at[me], local_sem)
        copies = []
        for k in range(1, N_DEV):
            px = 1 - x if k & 4 else x
            py = 1 - y if k & 2 else y
            pc = 1 - c if k & 1 else c
            copies.append(pltpu.make_async_remote_copy(
                src_ref=src_ref.at[4 * px + 2 * py + pc] if per_peer else src_ref, dst_ref=out_ref.at[me],
                send_sem=send_sems.at[k - 1], recv_sem=recv_sems.at[k - 1],
                device_id=(px, py, pc), device_id_type=pl.DeviceIdType.MESH))
        return mine, copies

    def start(*refs):
        mine, copies = descriptors(*refs)
        mine.start()
        for cp in copies:
            cp.start()

    def forward(*refs):
        pass

    def finish(*refs):
        mine, copies = descriptors(*refs)
        for cp in copies:
            cp.wait_recv()
        for cp in copies:
            cp.wait_send()
        mine.wait()

    return start, forward, finish


EXCHANGE_SEMS = [pltpu.SemaphoreType.DMA((N_DEV - 1,)), pltpu.SemaphoreType.DMA((N_DEV - 1,)), pltpu.SemaphoreType.DMA]


def _exchange(src, phases, *, name):
    rows = src.shape[-2]

    def body(*refs):
        for phase in phases:
            phase(*refs)

    anyspace = pl.BlockSpec(memory_space=pl.ANY)
    return _pcall(body, name=name, in_specs=[anyspace], out_specs=anyspace,
                  out_shape=jax.ShapeDtypeStruct((N_DEV, rows, LANES), src.dtype), scratch_shapes=list(EXCHANGE_SEMS))(src)


def _side(src, phases):
    start, forward, finish = phases

    def before(first, mid, ins, outs, scratch):
        pl.when(first)(lambda: start(ins[0], outs[0], *scratch))
        pl.when(mid)(lambda: forward(ins[0], outs[0], *scratch))

    def after(last, ins, outs, scratch):
        pl.when(last)(lambda: finish(ins[0], outs[0], *scratch))

    return [src], [jax.ShapeDtypeStruct((N_DEV, src.shape[-2], LANES), src.dtype)], list(EXCHANGE_SEMS), before, after


def _call_2d(kern, *, name, grid, in_specs, out_specs, out_shape, ins, scratch_shapes=(), semantics, side=None):
    if side is None:
        return _pcall(kern, name=name, grid=grid, in_specs=in_specs, out_specs=out_specs, out_shape=out_shape,
                      scratch_shapes=list(scratch_shapes), compiler_params=_params(*semantics))(*ins)
    s_ins, s_shapes, s_scratch, before, after = side
    n_in, n_out, n_scr = len(ins), len(out_shape), len(scratch_shapes)

    def combined(*refs):
        refs = list(refs)
        cut = [n_in, len(s_ins), n_out, len(s_shapes), n_scr, len(s_scratch)]
        parts, pos = [], 0
        for c in cut:
            parts.append(refs[pos:pos + c])
            pos += c
        m_in, c_in, m_out, c_out, m_scr, c_scr = parts
        ids = [pl.program_id(a) for a in range(len(grid))]
        inner_zero = functools.reduce(jnp.logical_and, [i == 0 for i in ids[1:]])
        first = jnp.logical_and(ids[0] == 0, inner_zero)
        mid = jnp.logical_and(ids[0] == grid[0] // 2, inner_zero)
        last = functools.reduce(jnp.logical_and, [i == n - 1 for i, n in zip(ids, grid)])
        before(first, mid, c_in, c_out, c_scr)
        kern(*m_in, *m_out, *m_scr)
        after(last, c_in, c_out, c_scr)

    anyspace = pl.BlockSpec(memory_space=pl.ANY)
    outs = _pcall(combined, name=name, grid=grid, in_specs=list(in_specs) + [anyspace] * len(s_ins),
                  out_specs=list(out_specs) + [anyspace] * len(s_shapes), out_shape=list(out_shape) + s_shapes,
                  scratch_shapes=list(scratch_shapes) + s_scratch, compiler_params=_params(*["arbitrary"] * len(grid)))(*ins, *s_ins)
    return outs[:n_out], outs[n_out]


def _two_level_phases():
    def parts(src_ref, out_ref, send_sems, recv_sems, local_sem):
        x, y, c = lax.axis_index("x"), lax.axis_index("y"), lax.axis_index("c")
        me, sibling = (x, y, c), (x, y, 1 - c)
        chips = [(1 - x, y), (x, 1 - y), (1 - x, 1 - y)]

        def slab(px, py, pc):
            return out_ref.at[4 * px + 2 * py + pc]

        def copy(k, block, to, from_src=False):
            return pltpu.make_async_remote_copy(
                src_ref=src_ref if from_src else slab(*block), dst_ref=slab(*block),
                send_sem=send_sems.at[k], recv_sem=recv_sems.at[k], device_id=to, device_id_type=pl.DeviceIdType.MESH)

        return dict(
            mine=lambda: pltpu.make_async_copy(src_ref, slab(*me), local_sem),
            first=lambda: [copy(0, me, sibling, True)] + [copy(1 + j, me, (*chip, c), True) for j, chip in enumerate(chips)],
            passed=lambda: [copy(4 + j, (*chip, c), sibling) for j, chip in enumerate(chips)],
            landed=lambda: [copy(1 + j, (*chip, c), me) for j, chip in enumerate(chips)],
            late=lambda: [copy(0, sibling, me)] + [copy(4 + j, (*chip, 1 - c), me) for j, chip in enumerate(chips)])

    def start(*refs):
        make = parts(*refs)
        make['mine']().start()
        for cp in make['first']():
            cp.start()

    def forward(*refs):
        make = parts(*refs)
        for arrived, onward in zip(make['landed'](), make['passed']()):
            arrived.wait_recv()
            onward.start()

    def finish(*refs):
        make = parts(*refs)
        for cp in make['late']():
            cp.wait_recv()
        for cp in make['first']() + make['passed']():
            cp.wait_send()
        make['mine']().wait()

    return start, forward, finish


def _adamw(recv, w, m, v, *, name):
    rows = w.shape[0]
    tr = _pick(rows, (512, 256, 128, 64))

    def kern(r_ref, w_ref, m_ref, v_ref, g_ref, d_ref, mo_ref, vo_ref):
        g = r_ref[0].astype(F32)
        for p in range(1, N_DEV):
            g = g + r_ref[p].astype(F32)
        mn = ADAM_B1 * m_ref[...] + (1.0 - ADAM_B1) * g
        vn = ADAM_B2 * v_ref[...] + (1.0 - ADAM_B2) * (g * g)
        m_hat = mn / (1.0 - ADAM_B1 ** ADAM_STEP)
        v_hat = vn / (1.0 - ADAM_B2 ** ADAM_STEP)
        g_ref[...] = g
        d_ref[...] = -ADAM_LR * (m_hat / (jnp.sqrt(v_hat) + ADAM_EPS) + ADAM_WD * w_ref[...])
        mo_ref[...] = mn
        vo_ref[...] = vn

    row = pl.BlockSpec((tr, LANES), lambda i: (i, 0))
    shp = jax.ShapeDtypeStruct((rows, LANES), F32)
    return _pcall(kern, name=name, grid=(rows // tr,), in_specs=[pl.BlockSpec((N_DEV, tr, LANES), lambda i: (0, i, 0)), row, row, row],
                  out_specs=[row, row, row, row], out_shape=[shp, shp, shp, shp], compiler_params=_params("parallel"))(recv, w, m, v)


INPUTS = ['x', 'mem'] + WEIGHTS + ['loss_target'] + ['m_' + n for n in WEIGHTS] + ['v_' + n for n in WEIGHTS]
SMALL_PAD_ROWS = 4


def kernel(x, mem, ffn1_norm, ffn1_w1, ffn1_w3, ffn1_w2, mix_norm, mem_norm, w_in, w_mem_kv, qn_dsa, kn_dsa, qn_mem, kn_mem, w_branch_sb, w_branch_dsa, w_branch_mem, w_gate, b_gate, w_out, ffn2_norm, ffn2_w1, ffn2_w3, ffn2_w2, loss_target, m_ffn1_norm, m_ffn1_w1, m_ffn1_w3, m_ffn1_w2, m_mix_norm, m_mem_norm, m_w_in, m_w_mem_kv, m_qn_dsa, m_kn_dsa, m_qn_mem, m_kn_mem, m_w_branch_sb, m_w_branch_dsa, m_w_branch_mem, m_w_gate, m_b_gate, m_w_out, m_ffn2_norm, m_ffn2_w1, m_ffn2_w3, m_ffn2_w2, v_ffn1_norm, v_ffn1_w1, v_ffn1_w3, v_ffn1_w2, v_mix_norm, v_mem_norm, v_w_in, v_w_mem_kv, v_qn_dsa, v_kn_dsa, v_qn_mem, v_kn_mem, v_w_branch_sb, v_w_branch_dsa, v_w_branch_mem, v_w_gate, v_b_gate, v_w_out, v_ffn2_norm, v_ffn2_w1, v_ffn2_w3, v_ffn2_w2):
    given = dict(zip(INPUTS, (x, mem, ffn1_norm, ffn1_w1, ffn1_w3, ffn1_w2, mix_norm, mem_norm, w_in, w_mem_kv, qn_dsa, kn_dsa, qn_mem, kn_mem, w_branch_sb, w_branch_dsa, w_branch_mem, w_gate, b_gate, w_out, ffn2_norm, ffn2_w1, ffn2_w3, ffn2_w2, loss_target, m_ffn1_norm, m_ffn1_w1, m_ffn1_w3, m_ffn1_w2, m_mix_norm, m_mem_norm, m_w_in, m_w_mem_kv, m_qn_dsa, m_kn_dsa, m_qn_mem, m_kn_mem, m_w_branch_sb, m_w_branch_dsa, m_w_branch_mem, m_w_gate, m_b_gate, m_w_out, m_ffn2_norm, m_ffn2_w1, m_ffn2_w3, m_ffn2_w2, v_ffn1_norm, v_ffn1_w1, v_ffn1_w3, v_ffn1_w2, v_mix_norm, v_mem_norm, v_w_in, v_w_mem_kv, v_qn_dsa, v_kn_dsa, v_qn_mem, v_kn_mem, v_w_branch_sb, v_w_branch_dsa, v_w_branch_mem, v_w_gate, v_b_gate, v_w_out, v_ffn2_norm, v_ffn2_w1, v_ffn2_w3, v_ffn2_w2), strict=True))
    wl = {n: given[n][0] for n in BIG}
    ws = {n: given[n] for n in SMALL}

    loss, dx, recv, g = _local_step(x[0], mem[0], loss_target[0], wl, ws)

    big = [{}, {}, {}, {}]
    for tag, names in (("ffn2", G_FFN2), ("mid", G_MID), ("ffn1", G_FFN1)):
        outs = _adamw(recv[tag], _pack_rows(wl, names), _pack_rows({n: given['m_' + n][0] for n in names}, names),
                      _pack_rows({n: given['v_' + n][0] for n in names}, names), name=f"adamw_{tag}")
        for kind, t in enumerate(outs):
            big[kind].update(_unpack_rows(t, wl, names))

    gs = _pack_small(g, SMALL, SMALL_PAD_ROWS)
    loss_row = gs.shape[0] - SMALL_PAD_ROWS
    gs = gs.at[loss_row, 0].set(loss)
    recv_s = _exchange(gs, _direct_phases(False), name="gather_small")
    small = _adamw(recv_s, _pack_small(ws, SMALL, SMALL_PAD_ROWS), _pack_small({n: given['m_' + n] for n in SMALL}, SMALL, SMALL_PAD_ROWS),
                   _pack_small({n: given['v_' + n] for n in SMALL}, SMALL, SMALL_PAD_ROWS), name="adamw_replicated")
    total_loss = small[0][loss_row, 0]
    small = [_unpack_small(t, ws, SMALL) for t in small]

    outs = [total_loss, dx[None]]
    for kind in range(4):
        outs += [big[kind][n][None] if n in wl else small[kind][n] for n in WEIGHTS]
    return tuple(outs)
```

```python
import functools
import math

import jax
import jax.numpy as jnp
from jax import lax
from jax.experimental import pallas as pl
from jax.experimental.pallas import tpu as pltpu

F32 = jnp.float32
BF16 = jnp.bfloat16
MXU_DT = jnp.bfloat16

N_DEV = 8
HEAD_DIM = 64
SB_HEADS = 8
DSA_GROUPS = ((128, 1), (512, 4), (2048, 16))
DSA_HPG = 4
MEM_HEADS = 4
SB_W = SB_HEADS * HEAD_DIM
DSA_W = DSA_HPG * len(DSA_GROUPS) * HEAD_DIM
DSA_OUT_W = DSA_HPG * HEAD_DIM
MEM_W = MEM_HEADS * HEAD_DIM
ROPE_THETA = 10000.0
NORM_EPS = 1e-6
QB = 128
SCALE = HEAD_DIM ** -0.5
ADAM_LR, ADAM_B1, ADAM_B2, ADAM_EPS, ADAM_WD, ADAM_STEP = 0.001, 0.9, 0.999, 1e-08, 0.01, 10

LANES = 128
VMEM_LIMIT = 48 * 1024 * 1024
SB_DEAD = -110.0 * 1.4426950408889634

WEIGHTS = ['ffn1_norm', 'ffn1_w1', 'ffn1_w3', 'ffn1_w2', 'mix_norm', 'mem_norm', 'w_in', 'w_mem_kv', 'qn_dsa', 'kn_dsa',
           'qn_mem', 'kn_mem', 'w_branch_sb', 'w_branch_dsa', 'w_branch_mem', 'w_gate', 'b_gate', 'w_out', 'ffn2_norm',
           'ffn2_w1', 'ffn2_w3', 'ffn2_w2']
SHARD_AXIS = {'ffn1_norm': None, 'ffn1_w1': 1, 'ffn1_w3': 1, 'ffn1_w2': 0, 'mix_norm': None, 'mem_norm': None, 'w_in': 1,
              'w_mem_kv': 0, 'qn_dsa': None, 'kn_dsa': None, 'qn_mem': None, 'kn_mem': None, 'w_branch_sb': 1,
              'w_branch_dsa': 1, 'w_branch_mem': 1, 'w_gate': 1, 'b_gate': None, 'w_out': 0, 'ffn2_norm': None,
              'ffn2_w1': 1, 'ffn2_w3': 1, 'ffn2_w2': 0}
BIG = [n for n in WEIGHTS if SHARD_AXIS[n] is not None]
SMALL = [n for n in WEIGHTS if SHARD_AXIS[n] is None]


def _pcall(kern, **kw):
    return pl.pallas_call(kern, **kw)


def _params(*sem):
    return pltpu.CompilerParams(dimension_semantics=sem, vmem_limit_bytes=VMEM_LIMIT)


def _dot(a, b, dims):
    return lax.dot_general(a.astype(MXU_DT), b.astype(MXU_DT), (dims, ((), ())), preferred_element_type=F32)


def _nn(a, b):
    return _dot(a, b, ((1,), (0,)))


def _nt(a, b):
    return _dot(a, b, ((1,), (1,)))


def _tn(a, b):
    return _dot(a, b, ((0,), (0,)))


def _pick(n, prefs):
    for p in prefs:
        if n % p == 0:
            return p
    return n


def _matmul(a, b, *, name, ta=False, tb=False, out_dtype=F32, res=None, alpha=1.0, tm=1024, tn=512, tk=1024, pair2=None,
            epilogue=None, side=None):
    if ta:
        kdim, m = a.shape
    else:
        m, kdim = a.shape
    n = b.shape[0] if tb else b.shape[1]
    tm = _pick(m, (tm, 512, 256, 128))
    tn = _pick(n, (tn, 512, 384, 256, 128))
    tk = _pick(kdim, (tk, 1024, 512, 256, 128))
    nk = kdim // tk
    a_spec = pl.BlockSpec((tk, tm), lambda i, j, k: (k, i)) if ta else pl.BlockSpec((tm, tk), lambda i, j, k: (i, k))
    b_spec = pl.BlockSpec((tn, tk), lambda i, j, k: (j, k)) if tb else pl.BlockSpec((tk, tn), lambda i, j, k: (k, j))
    o_spec = pl.BlockSpec((tm, tn), lambda i, j, k: (i, j))
    v_spec = pl.BlockSpec((1, tn), lambda i, j, k: (0, j))
    dims = ((0 if ta else 1,), (1 if tb else 0,))
    n_mm = 2 if pair2 is None else 4
    if epilogue is None:
        row_ins, vec_ins = ([] if res is None else [res]), []
        out_dtypes, n_vec = [out_dtype], 0
    else:
        assert tn == n and res is None
        epi_fn, row_ins, vec_ins, out_dtypes, n_vec = epilogue
    n_row_out = len(out_dtypes)

    def kern(*refs):
        refs = list(refs)
        acc_ref = refs.pop() if nk > 1 else None
        mm = refs[:n_mm]
        extra = refs[n_mm:n_mm + len(row_ins) + len(vec_ins)]
        outs = refs[n_mm + len(extra):]
        i = pl.program_id(0)
        k = pl.program_id(2)

        def product():
            part = _dot(mm[0][...], mm[1][...], dims)
            if pair2 is not None:
                part = part + _dot(mm[2][...], mm[3][...], dims)
            return part

        def finish(r):
            if alpha != 1.0:
                r = r * alpha
            if epilogue is None:
                if extra:
                    r = extra[0][...] + r
                outs[0][...] = r.astype(out_dtype)
                return
            vals = epi_fn(r, *[e[...] for e in extra])
            for o_ref, v in zip(outs[:n_row_out], vals[:n_row_out]):
                o_ref[...] = v.astype(o_ref.dtype)
            for o_ref, v in zip(outs[n_row_out:], vals[n_row_out:]):
                @pl.when(i == 0)
                def _():
                    o_ref[...] = jnp.zeros_like(o_ref)

                o_ref[...] += v

        if nk == 1:
            finish(product())
            return

        @pl.when(k == 0)
        def _():
            acc_ref[...] = jnp.zeros_like(acc_ref)

        acc_ref[...] += product()

        @pl.when(k == nk - 1)
        def _():
            finish(acc_ref[...])

    ins = [a, b] + ([] if pair2 is None else list(pair2)) + list(row_ins) + list(vec_ins)
    specs = [a_spec, b_spec] * (n_mm // 2) + [o_spec] * len(row_ins) + [v_spec] * len(vec_ins)
    out_specs = [o_spec] * n_row_out + [v_spec] * n_vec
    out_shape = [jax.ShapeDtypeStruct((m, n), dt) for dt in out_dtypes] + [jax.ShapeDtypeStruct((1, n), F32)] * n_vec
    outs = _call_2d(kern, name=name, grid=(m // tm, n // tn, nk), in_specs=specs, out_specs=out_specs, out_shape=out_shape,
                    ins=ins, scratch_shapes=[pltpu.VMEM((tm, tn), F32)] if nk > 1 else [],
                    semantics=("arbitrary" if n_vec else "parallel", "parallel", "arbitrary"), side=side)
    carried = None
    if side is not None:
        outs, carried = outs
    outs = outs[0] if epilogue is None else outs
    return outs if side is None else (outs, carried)


def _epi_residual_rms(r, res, gain):
    xn = res + r
    return xn, xn * lax.rsqrt(jnp.mean(xn * xn, axis=-1, keepdims=True) + NORM_EPS) * gain


def _epi_rms_bwd(r, x, dres, gain):
    rs = lax.rsqrt(jnp.mean(x * x, axis=-1, keepdims=True) + NORM_EPS)
    xh = x * rs
    dy = r * gain
    dx = dres + rs * (dy - xh * jnp.mean(dy * xh, axis=-1, keepdims=True))
    return dx, dx, jnp.sum(r * xh, axis=0, keepdims=True)


def _epi_rms_bwd_sum(r, r0, x, dres, gain):
    return _epi_rms_bwd(r + r0, x, dres, gain)


def _epi_loss(r, res, target):
    e = (res + r) - target
    dy = e / e.shape[-1]
    return dy, dy, jnp.sum(e * e, axis=0, keepdims=True)
def _rms_fwd(x, g, *, name):
    s, d = x.shape
    ts = _pick(s, (512, 256))

    def kern(x_ref, g_ref, h_ref):
        xf = x_ref[...]
        r = lax.rsqrt(jnp.mean(xf * xf, axis=-1, keepdims=True) + NORM_EPS)
        h_ref[...] = (xf * r * g_ref[...]).astype(h_ref.dtype)

    return _pcall(kern, name=name, grid=(s // ts,),
                  in_specs=[pl.BlockSpec((ts, d), lambda i: (i, 0)), pl.BlockSpec((1, d), lambda i: (0, 0))],
                  out_specs=pl.BlockSpec((ts, d), lambda i: (i, 0)), out_shape=jax.ShapeDtypeStruct((s, d), BF16),
                  compiler_params=_params("parallel"))(x, g)


def _rms_bwd(x, g, dh, res, *, name):
    s, d = x.shape
    ts = _pick(s, (512, 256))

    def kern(*refs):
        if res is None:
            x_ref, g_ref, dh_ref, dx_ref, dxb_ref, dg_ref = refs
            r_ref = None
        else:
            x_ref, g_ref, dh_ref, r_ref, dx_ref, dxb_ref, dg_ref = refs
        xf = x_ref[...]
        r = lax.rsqrt(jnp.mean(xf * xf, axis=-1, keepdims=True) + NORM_EPS)
        xh = xf * r
        dhf = dh_ref[...].astype(F32)
        dy = dhf * g_ref[...]
        dx = r * (dy - xh * jnp.mean(dy * xh, axis=-1, keepdims=True))
        if r_ref is not None:
            dx = r_ref[...] + dx
        dx_ref[...] = dx
        dxb_ref[...] = dx.astype(dxb_ref.dtype)

        @pl.when(pl.program_id(0) == 0)
        def _():
            dg_ref[...] = jnp.zeros_like(dg_ref)

        dg_ref[...] += jnp.sum(dhf * xh, axis=0, keepdims=True)

    row = pl.BlockSpec((ts, d), lambda i: (i, 0))
    vec = pl.BlockSpec((1, d), lambda i: (0, 0))
    ins = [x, g, dh] + ([] if res is None else [res])
    return _pcall(kern, name=name, grid=(s // ts,), in_specs=[row, vec, row] + ([] if res is None else [row]),
                  out_specs=[row, row, vec],
                  out_shape=[jax.ShapeDtypeStruct((s, d), F32), jax.ShapeDtypeStruct((s, d), BF16), jax.ShapeDtypeStruct((1, d), F32)],
                  compiler_params=_params("arbitrary"))(*ins)


def _sigmoid(x):
    return 1.0 / (1.0 + jnp.exp(-x))


FFN_TM, FFN_TF = 512, 1408


def _ffn_up(h, w1, w3, *, name, side=None):
    s, d = h.shape
    fdim = w1.shape[1]
    tm, tf = _pick(s, (FFN_TM, 256)), _pick(fdim, (FFN_TF, 512, 256, 128))

    def kern(h_ref, w1_ref, w3_ref, a_ref, b_ref, f_ref):
        hb = h_ref[...]
        a = _nn(hb, w1_ref[...])
        b = _nn(hb, w3_ref[...])
        a_ref[...] = a.astype(a_ref.dtype)
        b_ref[...] = b.astype(b_ref.dtype)
        f_ref[...] = (a * _sigmoid(a) * b).astype(f_ref.dtype)

    wspec = pl.BlockSpec((d, tf), lambda i, j: (0, j))
    ospec = pl.BlockSpec((tm, tf), lambda i, j: (i, j))
    shp = jax.ShapeDtypeStruct((s, fdim), BF16)
    return _call_2d(kern, name=name, grid=(s // tm, fdim // tf), in_specs=[pl.BlockSpec((tm, d), lambda i, j: (i, 0)), wspec, wspec],
                    out_specs=[ospec, ospec, ospec], out_shape=[shp, shp, shp], ins=[h, w1, w3],
                    semantics=("parallel", "parallel"), side=side)


def _ffn_dact(dy, w2, a, b, *, name, side=None):
    s, d = dy.shape
    fdim = w2.shape[0]
    tm, tf = _pick(s, (FFN_TM, 256)), _pick(fdim, (FFN_TF, 512, 256, 128))

    def kern(dy_ref, w2_ref, a_ref, b_ref, da_ref, db_ref):
        df = _nt(dy_ref[...], w2_ref[...]) * 0.5
        av = a_ref[...].astype(F32)
        sg = _sigmoid(av)
        da_ref[...] = (df * b_ref[...].astype(F32) * (sg + av * sg * (1.0 - sg))).astype(da_ref.dtype)
        db_ref[...] = (df * (av * sg)).astype(db_ref.dtype)

    ospec = pl.BlockSpec((tm, tf), lambda i, j: (i, j))
    shp = jax.ShapeDtypeStruct((s, fdim), BF16)
    return _call_2d(kern, name=name, grid=(s // tm, fdim // tf),
                    in_specs=[pl.BlockSpec((tm, d), lambda i, j: (i, 0)), pl.BlockSpec((tf, d), lambda i, j: (j, 0)), ospec, ospec],
                    out_specs=[ospec, ospec], out_shape=[shp, shp], ins=[dy, w2, a, b], semantics=("parallel", "parallel"), side=side)


def _loss_head(y, t, *, name):
    s, d = y.shape
    ts = _pick(s, (512, 256))
    n = s // ts

    def kern(y_ref, t_ref, dy_ref, dyb_ref, l_ref, acc_ref):
        i = pl.program_id(0)

        @pl.when(i == 0)
        def _():
            acc_ref[...] = jnp.zeros_like(acc_ref)

        e = y_ref[...] - t_ref[...]
        dy_ref[...] = e / d
        dyb_ref[...] = (e / d).astype(dyb_ref.dtype)
        acc_ref[...] += jnp.sum(e * e, axis=0, keepdims=True)

        @pl.when(i == n - 1)
        def _():
            l_ref[...] = jnp.sum(acc_ref[...], axis=1, keepdims=True) * (0.5 / d)

    row = pl.BlockSpec((ts, d), lambda i: (i, 0))
    return _pcall(kern, name=name, grid=(n,), in_specs=[row, row], out_specs=[row, row, pl.BlockSpec((1, 1), lambda i: (0, 0))],
                  out_shape=[jax.ShapeDtypeStruct((s, d), F32), jax.ShapeDtypeStruct((s, d), BF16), jax.ShapeDtypeStruct((1, 1), F32)],
                  scratch_shapes=[pltpu.VMEM((1, d), F32)], compiler_params=_params("arbitrary"))(y, t)


def _head_mean(v, bd):
    hi = v.astype(BF16)
    lo = (v - hi.astype(F32)).astype(BF16)
    return (lax.dot_general(hi, bd, (((1,), (0,)), ((), ())), preferred_element_type=F32)
            + lax.dot_general(lo, bd, (((1,), (0,)), ((), ())), preferred_element_type=F32))


def _partner(v):
    w = v.shape[1]
    lane = lax.broadcasted_iota(jnp.int32, v.shape, 1)
    return jnp.where(lane % HEAD_DIM < HEAD_DIM // 2, pltpu.roll(v, w - HEAD_DIM // 2, 1), pltpu.roll(v, HEAD_DIM // 2, 1))


def _block_diag(w):
    r = lax.broadcasted_iota(jnp.int32, (w, w), 0) // HEAD_DIM
    c = lax.broadcasted_iota(jnp.int32, (w, w), 1) // HEAD_DIM
    return jnp.where(r == c, 1.0 / HEAD_DIM, 0.0).astype(BF16)


def _rope_tables(s):
    half = HEAD_DIM // 2
    inv_freq = jnp.power(ROPE_THETA, -jnp.arange(half, dtype=F32) / half)
    ang = jnp.arange(s).astype(F32)[:, None] * inv_freq[None, :]
    cos, sin = jnp.cos(ang), jnp.sin(ang)
    cos2 = jnp.concatenate([cos, cos, cos, cos], axis=1)
    sin2 = jnp.concatenate([-sin, sin, -sin, sin], axis=1)
    return cos2, sin2


def _qknorm_fwd(src, col0, width, gain, rope, *, name, out_dtype=BF16):
    s = src.shape[0]
    ts = _pick(s, (512, 256))
    cb = col0 // width
    assert col0 % width == 0
    reps = width // LANES
    g = jnp.tile(gain, (1, width // HEAD_DIM))

    def kern(*refs):
        if rope is None:
            x_ref, g_ref, o_ref = refs
        else:
            x_ref, g_ref, c_ref, s_ref, o_ref = refs
        x = x_ref[...].astype(F32)
        bd = _block_diag(width)
        r = lax.rsqrt(_head_mean(x * x, bd) + NORM_EPS)
        y = x * r * g_ref[...]
        if rope is not None:
            y = y * jnp.tile(c_ref[...], (1, reps)) + _partner(y) * jnp.tile(s_ref[...], (1, reps))
        o_ref[...] = y.astype(o_ref.dtype)

    xs = pl.BlockSpec((ts, width), lambda i: (i, cb))
    tab = pl.BlockSpec((ts, LANES), lambda i: (i, 0))
    ins = [src, g] + ([] if rope is None else list(rope))
    specs = [xs, pl.BlockSpec((1, width), lambda i: (0, 0))] + ([] if rope is None else [tab, tab])
    return _pcall(kern, name=name, grid=(s // ts,), in_specs=specs, out_specs=pl.BlockSpec((ts, width), lambda i: (i, 0)),
                  out_shape=jax.ShapeDtypeStruct((s, width), out_dtype), compiler_params=_params("parallel"))(*ins)


def _qknorm_bwd(src, col0, width, gain, rope, dout, *, name):
    s = src.shape[0]
    ts = _pick(s, (512, 256))
    cb = col0 // width
    reps = width // LANES
    g = jnp.tile(gain, (1, width // HEAD_DIM))

    douts = list(dout) if isinstance(dout, (list, tuple)) else [dout]
    piece = width // len(douts)

    def kern(*refs):
        refs = list(refs)
        dg_ref = refs.pop()
        dx_ref = refs.pop()
        do_refs = [refs.pop() for _ in douts][::-1]
        if rope is None:
            x_ref, g_ref = refs
        else:
            x_ref, g_ref, c_ref, s_ref = refs
        x = x_ref[...].astype(F32)
        bd = _block_diag(width)
        r = lax.rsqrt(_head_mean(x * x, bd) + NORM_EPS)
        xh = x * r
        dy = jnp.concatenate([d[...].astype(F32) for d in do_refs], axis=1) if len(do_refs) > 1 else do_refs[0][...].astype(F32)
        if rope is not None:
            dy = dy * jnp.tile(c_ref[...], (1, reps)) + _partner(dy * jnp.tile(s_ref[...], (1, reps)))
        dxh = dy * g_ref[...]
        dx_ref[...] = (r * (dxh - xh * _head_mean(dxh * xh, bd))).astype(dx_ref.dtype)

        @pl.when(pl.program_id(0) == 0)
        def _():
            dg_ref[...] = jnp.zeros_like(dg_ref)

        dg_ref[...] += jnp.sum(dy * xh, axis=0, keepdims=True)

    xs = pl.BlockSpec((ts, width), lambda i: (i, cb))
    row = pl.BlockSpec((ts, width), lambda i: (i, 0))
    vec = pl.BlockSpec((1, width), lambda i: (0, 0))
    tab = pl.BlockSpec((ts, LANES), lambda i: (i, 0))
    ins = [src, g] + ([] if rope is None else list(rope)) + douts
    specs = [xs, vec] + ([] if rope is None else [tab, tab]) + [pl.BlockSpec((ts, piece), lambda i: (i, 0))] * len(douts)
    dx, dg = _pcall(kern, name=name, grid=(s // ts,), in_specs=specs, out_specs=[row, vec],
                    out_shape=[jax.ShapeDtypeStruct((s, width), BF16), jax.ShapeDtypeStruct((1, width), F32)],
                    compiler_params=_params("arbitrary"))(*ins)
    return dx, jnp.sum(dg.reshape(width // HEAD_DIM, HEAD_DIM), axis=0, keepdims=True)


def _tri(strict):
    r = lax.broadcasted_iota(jnp.int32, (2 * QB, QB), 0) % QB
    c = lax.broadcasted_iota(jnp.int32, (2 * QB, QB), 1)
    return jnp.where((r > c) if strict else (r >= c), 1.0, 0.0).astype(BF16)


def _split_dot(v, t2):
    hi = v.astype(BF16)
    lo = (v - hi.astype(F32)).astype(BF16)
    return lax.dot_general(jnp.concatenate([hi, lo], axis=1), t2, (((1,), (0,)), ((), ())), preferred_element_type=F32)


LOG2E = 1.4426950408889634


def _log2_sigmoids(z2):
    lf = -(jnp.maximum(z2, 0.0) + jnp.log2(1.0 + jnp.exp2(-jnp.abs(z2))))
    return z2 + lf, lf


def _key_blocks(t):
    s = t.shape[0]
    n = t.shape[1] // HEAD_DIM
    return t.reshape(s // QB, QB, n, HEAD_DIM).transpose(2, 0, 3, 1)


def _from_key_blocks(t):
    n, nb, hd, qb = t.shape
    return t.transpose(1, 3, 0, 2).reshape(nb * qb, n * hd)


SB_SUB = 4
SB2_SUB = 2


def _first_half(shape):
    return lax.broadcasted_iota(jnp.int32, shape, 1) < HEAD_DIM


def _split_pair(t, first):
    zero = jnp.zeros_like(t)
    return [jnp.where(first, t, zero), jnp.where(first, zero, t)]


def _sb2_fwd(p, *, name, side=None):
    s = p.shape[0]
    rq = SB2_SUB * QB
    nq = s // rq
    npair = SB_W // LANES

    def kern(q_ref, k_ref, v_ref, o_ref):
        i = pl.program_id(1)
        first = _first_half((rq, LANES))
        q2 = jnp.concatenate(_split_pair(q_ref[...], first), axis=0)
        t2 = _tri(True)
        rel = lax.broadcasted_iota(jnp.int32, (2 * rq, QB), 1) - lax.broadcasted_iota(jnp.int32, (2 * rq, QB), 0) % rq

        def tile(j, carry, acc, masked):
            off = pl.multiple_of(j * QB, QB)
            ls, lf = _log2_sigmoids(_nt(q2, k_ref[pl.ds(off, QB), :]) * (SCALE * LOG2E))
            if masked:
                before = rel < i * rq - j * QB
                lf = jnp.where(before, lf, 0.0)
            w = jnp.exp2(ls + _split_dot(lf, t2) + carry)
            if masked:
                w = jnp.where(before, w, 0.0)
            return carry + jnp.sum(lf, axis=1, keepdims=True), acc + _nn(w, v_ref[pl.ds(off, QB), :])

        carry, acc = jnp.zeros((2 * rq, 1), F32), jnp.zeros((2 * rq, LANES), F32)
        for a in range(SB2_SUB):
            carry, acc = tile(i * SB2_SUB + (SB2_SUB - 1 - a), carry, acc, True)

        def cond(st):
            return jnp.logical_and(st[0] >= 0, st[1] > 0)

        def body(st):
            carry, acc = tile(st[0], st[2], st[3], False)
            return st[0] - 1, (jnp.max(carry) > SB_DEAD).astype(jnp.int32), carry, acc

        st = lax.while_loop(cond, body, (i * SB2_SUB - 1, jnp.int32(1), carry, acc))
        o_ref[...] = jnp.where(first, st[3][:rq], st[3][rq:])

    outs = _call_2d(kern, name=name, grid=(npair, nq),
                    in_specs=[pl.BlockSpec((rq, LANES), lambda a, i: (i, a)), pl.BlockSpec((s, LANES), lambda a, i: (0, npair + a)),
                              pl.BlockSpec((s, LANES), lambda a, i: (0, 2 * npair + a))],
                    out_specs=[pl.BlockSpec((rq, LANES), lambda a, i: (i, a))], out_shape=[jax.ShapeDtypeStruct((s, SB_W), F32)],
                    ins=[p, p, p], semantics=("parallel", "arbitrary"), side=side)
    return outs[0] if side is None else (outs[0][0], outs[1])


def _sb2_bwd(p, o, do, *, name, side=None):
    s = p.shape[0]
    rq = SB2_SUB * QB
    nq = s // rq
    npair = SB_W // LANES

    def kern(q_ref, k_ref, v_ref, o_ref, do_ref, dq_ref, dk_hbm, dv_hbm, dk_acc, dv_acc, sem):
        pr = pl.program_id(0)
        i = pl.program_id(1)

        @pl.when(i == 0)
        def _():
            dk_acc[...] = jnp.zeros_like(dk_acc)
            dv_acc[...] = jnp.zeros_like(dv_acc)

        first = _first_half((rq, LANES))
        q2 = jnp.concatenate(_split_pair(q_ref[...], first), axis=0)
        do2 = jnp.concatenate(_split_pair(do_ref[...], first), axis=0)
        o2 = o_ref[...]
        dsum = jnp.sum(do2.astype(F32) * jnp.concatenate([o2, o2], axis=0), axis=1, keepdims=True)
        t_strict = _tri(True)
        t_incl = _tri(False)
        rel = lax.broadcasted_iota(jnp.int32, (2 * rq, QB), 1) - lax.broadcasted_iota(jnp.int32, (2 * rq, QB), 0) % rq

        def tile(j, carry, gcarry, dq, masked):
            off = pl.multiple_of(j * QB, QB)
            kt = k_ref[pl.ds(off, QB), :]
            ls, lf = _log2_sigmoids(_nt(q2, kt) * (SCALE * LOG2E))
            if masked:
                before = rel < i * rq - j * QB
                lf = jnp.where(before, lf, 0.0)
            w = jnp.exp2(ls + _split_dot(lf, t_strict) + carry)
            if masked:
                w = jnp.where(before, w, 0.0)
            wr = w.astype(MXU_DT)
            g = _nt(do2, v_ref[pl.ds(off, QB), :]) * wr.astype(F32)
            big_g = dsum - (_split_dot(g, t_incl) + gcarry)
            sig = jnp.exp2(ls)
            dz = g * (1.0 - sig) - sig * big_g
            if masked:
                dz = jnp.where(before, dz, 0.0)
            dz = dz * SCALE
            dk_acc[pl.ds(off, QB), :] += _tn(dz, q2)
            dv_acc[pl.ds(off, QB), :] += _tn(wr, do2)
            return (carry + jnp.sum(lf, axis=1, keepdims=True), gcarry + jnp.sum(g, axis=1, keepdims=True),
                    dq + _nn(dz, kt))

        zc = jnp.zeros((2 * rq, 1), F32)
        carry, gcarry, dq = zc, zc, jnp.zeros((2 * rq, LANES), F32)
        for a in range(SB2_SUB):
            carry, gcarry, dq = tile(i * SB2_SUB + (SB2_SUB - 1 - a), carry, gcarry, dq, True)

        def cond(st):
            return jnp.logical_and(st[0] >= 0, st[1] > 0)

        def body(st):
            carry, gcarry, dq = tile(st[0], st[2], st[3], st[4], False)
            return st[0] - 1, (jnp.max(carry) > SB_DEAD).astype(jnp.int32), carry, gcarry, dq

        st = lax.while_loop(cond, body, (i * SB2_SUB - 1, jnp.int32(1), carry, gcarry, dq))
        dq_ref[...] = jnp.where(first, st[4][:rq], st[4][rq:])

        @pl.when(i == nq - 1)
        def _():
            cols = pl.ds(pl.multiple_of(pr * LANES, LANES), LANES)
            ck = pltpu.make_async_copy(dk_acc, dk_hbm.at[:, cols], sem.at[0])
            cv = pltpu.make_async_copy(dv_acc, dv_hbm.at[:, cols], sem.at[1])
            ck.start()
            cv.start()
            ck.wait()
            cv.wait()

    blk = pl.BlockSpec((rq, LANES), lambda a, i: (i, a))
    anyspace = pl.BlockSpec(memory_space=pl.ANY)
    shp = jax.ShapeDtypeStruct((s, SB_W), F32)
    return _call_2d(kern, name=name, grid=(npair, nq),
                    in_specs=[blk, pl.BlockSpec((s, LANES), lambda a, i: (0, npair + a)),
                              pl.BlockSpec((s, LANES), lambda a, i: (0, 2 * npair + a)), blk, blk],
                    out_specs=[blk, anyspace, anyspace], out_shape=[shp, shp, shp], ins=[p, p, p, o, do],
                    scratch_shapes=[pltpu.VMEM((s, LANES), F32), pltpu.VMEM((s, LANES), F32), pltpu.SemaphoreType.DMA((2,))],
                    semantics=("arbitrary", "arbitrary"), side=side)


def _sb_fwd(q, kt, vt, *, name):
    h, s, hd = q.shape
    rq = SB_SUB * QB
    nq = s // rq
    nb = s // QB

    def kern(q_ref, k_ref, v_ref, o_ref):
        i = pl.program_id(1)
        qb = q_ref[...]
        t2 = _tri(True)
        rel = lax.broadcasted_iota(jnp.int32, (rq, QB), 1) - lax.broadcasted_iota(jnp.int32, (rq, QB), 0)

        def tile(j, carry, acc, masked):
            ls, lf = _log2_sigmoids(_nn(qb, k_ref[j]) * (SCALE * LOG2E))
            if masked:
                before = rel < i * rq - j * QB
                lf = jnp.where(before, lf, 0.0)
            w = jnp.exp2(ls + _split_dot(lf, t2) + carry)
            if masked:
                w = jnp.where(before, w, 0.0)
            return carry + jnp.sum(lf, axis=1, keepdims=True), acc + _nt(w, v_ref[j])

        carry, acc = jnp.zeros((rq, 1), F32), jnp.zeros((rq, hd), F32)
        for a in range(SB_SUB):
            carry, acc = tile(i * SB_SUB + (SB_SUB - 1 - a), carry, acc, True)

        def cond(st):
            return jnp.logical_and(st[0] >= 0, st[1] > 0)

        def body(st):
            j, _, carry, acc = st
            carry, acc = tile(j, carry, acc, False)
            return j - 1, (jnp.max(carry) > SB_DEAD).astype(jnp.int32), carry, acc

        _, _, _, acc = lax.while_loop(cond, body, (i * SB_SUB - 1, jnp.int32(1), carry, acc))
        o_ref[...] = acc

    blk = pl.BlockSpec((None, rq, hd), lambda a, i: (a, i, 0))
    full = pl.BlockSpec((None, nb, hd, QB), lambda a, i: (a, 0, 0, 0))
    return _pcall(kern, name=name, grid=(h, nq), in_specs=[blk, full, full], out_specs=blk,
                  out_shape=jax.ShapeDtypeStruct((h, s, hd), F32), compiler_params=_params("parallel", "arbitrary"))(q, kt, vt)


def _sb_bwd(q, kt, vt, o, do, *, name):
    h, s, hd = q.shape
    rq = SB_SUB * QB
    nq = s // rq
    nb = s // QB

    def kern(q_ref, k_ref, v_ref, o_ref, do_ref, dq_ref, dk_ref, dv_ref):
        i = pl.program_id(1)

        @pl.when(i == 0)
        def _():
            dk_ref[...] = jnp.zeros_like(dk_ref)
            dv_ref[...] = jnp.zeros_like(dv_ref)

        qb = q_ref[...]
        dob = do_ref[...]
        dsum = jnp.sum(dob.astype(F32) * o_ref[...], axis=1, keepdims=True)
        t_strict = _tri(True)
        t_incl = _tri(False)
        rel = lax.broadcasted_iota(jnp.int32, (rq, QB), 1) - lax.broadcasted_iota(jnp.int32, (rq, QB), 0)

        def tile(j, carry, gcarry, dq, masked):
            kb = k_ref[j]
            ls, lf = _log2_sigmoids(_nn(qb, kb) * (SCALE * LOG2E))
            if masked:
                before = rel < i * rq - j * QB
                lf = jnp.where(before, lf, 0.0)
            w = jnp.exp2(ls + _split_dot(lf, t_strict) + carry)
            if masked:
                w = jnp.where(before, w, 0.0)
            wr = w.astype(MXU_DT)
            g = _nn(dob, v_ref[j]) * wr.astype(F32)
            big_g = dsum - (_split_dot(g, t_incl) + gcarry)
            sig = jnp.exp2(ls)
            dz = g * (1.0 - sig) - sig * big_g
            if masked:
                dz = jnp.where(before, dz, 0.0)
            dz = dz * SCALE
            dk_ref[j] += _tn(qb, dz)
            dv_ref[j] += _tn(dob, wr)
            return (carry + jnp.sum(lf, axis=1, keepdims=True), gcarry + jnp.sum(g, axis=1, keepdims=True),
                    dq + _nt(dz, kb))

        carry, gcarry, dq = jnp.zeros((rq, 1), F32), jnp.zeros((rq, 1), F32), jnp.zeros((rq, hd), F32)
        for a in range(SB_SUB):
            carry, gcarry, dq = tile(i * SB_SUB + (SB_SUB - 1 - a), carry, gcarry, dq, True)

        def cond(st):
            return jnp.logical_and(st[0] >= 0, st[1] > 0)

        def body(st):
            j, _, carry, gcarry, dq = st
            carry, gcarry, dq = tile(j, carry, gcarry, dq, False)
            return j - 1, (jnp.max(carry) > SB_DEAD).astype(jnp.int32), carry, gcarry, dq

        st = lax.while_loop(cond, body, (i * SB_SUB - 1, jnp.int32(1), carry, gcarry, dq))
        dq_ref[...] = st[4]

    blk = pl.BlockSpec((None, rq, hd), lambda a, i: (a, i, 0))
    full = pl.BlockSpec((None, nb, hd, QB), lambda a, i: (a, 0, 0, 0))
    kshape = jax.ShapeDtypeStruct((h, nb, hd, QB), F32)
    return _pcall(kern, name=name, grid=(h, nq), in_specs=[blk, full, full, blk, blk], out_specs=[blk, full, full],
                  out_shape=[jax.ShapeDtypeStruct((h, s, hd), F32), kshape, kshape],
                  compiler_params=_params("parallel", "arbitrary"))(q, kt, vt, o, do)


DSA_SUB = 4


def _dsa_seq_blocks(t, s):
    steps_per_group = 4 * s // (QB * DSA_SUB)
    g = t // steps_per_group
    b0, b1, b2 = (s // (QB * r) for _, r in DSA_GROUPS)
    return jnp.where(g == 0, b0, jnp.where(g == 1, b1, b2))


def _dsa_rel():
    qi = lax.broadcasted_iota(jnp.int32, (QB, QB), 0)
    kj = lax.broadcasted_iota(jnp.int32, (QB, QB), 1)
    return kj - qi


def _prev_mask(rel, has_prev):
    return rel >= jnp.where(has_prev, 0, QB)


def _dsa_fwd(q, k, vp, *, name):
    rows = q.shape[0]
    s = rows // 12
    big = QB * DSA_SUB
    nsteps = rows // big

    def kern(q_ref, k_ref, kp_ref, v_ref, vpv_ref, o_ref):
        t = pl.program_id(0)
        bps = _dsa_seq_blocks(t, s)
        rel = _dsa_rel()
        lane = lax.broadcasted_iota(jnp.int32, (QB, LANES), 1)
        for a in range(DSA_SUB):
            qa = q_ref[pl.ds(a * QB, QB), :]
            kc = k_ref[pl.ds(a * QB, QB), :]
            vc = v_ref[pl.ds(a * QB, QB), :]
            if a == 0:
                kpv, vpv = kp_ref[...], vpv_ref[...]
            else:
                kpv, vpv = k_ref[pl.ds((a - 1) * QB, QB), :], v_ref[pl.ds((a - 1) * QB, QB), :]
            has_prev = (t * DSA_SUB + a) % bps != 0
            sc = jnp.where(rel <= 0, _nt(qa, kc) * SCALE, -jnp.inf)
            sp = jnp.where(_prev_mask(rel, has_prev), _nt(qa, kpv) * SCALE, -jnp.inf)
            m = jnp.maximum(jnp.max(sc, axis=1, keepdims=True), jnp.max(sp, axis=1, keepdims=True))
            pc = jnp.exp(sc - m)
            pp = jnp.exp(sp - m)
            den = jnp.sum(pc, axis=1, keepdims=True) + jnp.sum(pp, axis=1, keepdims=True)
            o = (_nn(pc, vc) + _nn(pp, vpv)) / den
            o_ref[pl.ds(a * QB, QB), :] = jnp.where(lane < HEAD_DIM, o, m + jnp.log(den))

    cur64 = pl.BlockSpec((big, HEAD_DIM), lambda t: (t, 0))
    prev64 = pl.BlockSpec((QB, HEAD_DIM), lambda t: (jnp.maximum(t * DSA_SUB - 1, 0), 0))
    cur128 = pl.BlockSpec((big, LANES), lambda t: (t, 0))
    prev128 = pl.BlockSpec((QB, LANES), lambda t: (jnp.maximum(t * DSA_SUB - 1, 0), 0))
    return _pcall(kern, name=name, grid=(nsteps,), in_specs=[cur64, cur64, prev64, cur128, prev128], out_specs=cur128,
                  out_shape=jax.ShapeDtypeStruct((rows, LANES), F32), compiler_params=_params("parallel"))(q, k, k, vp, vp)


def _dsa_combine(p0, p1, p2, *, name):
    hh, s, _ = p0.shape
    ts = _pick(s, (512, 256))

    def kern(a_ref, b_ref, c_ref, o_ref):
        lane = lax.broadcasted_iota(jnp.int32, (ts, LANES), 1)
        xs = [a_ref[...], b_ref[...], c_ref[...]]
        ls = [jnp.where(lane < HEAD_DIM, pltpu.roll(x, HEAD_DIM, 1), x) for x in xs]
        m = jnp.maximum(jnp.maximum(ls[0], ls[1]), ls[2])
        es = [jnp.exp(l - m) for l in ls]
        den = es[0] + es[1] + es[2]
        o = (es[0] * xs[0] + es[1] * xs[1] + es[2] * xs[2]) / den
        o_ref[...] = jnp.where(lane < HEAD_DIM, o, m + jnp.log(den))

    blk = pl.BlockSpec((None, ts, LANES), lambda a, i: (a, i, 0))
    return _pcall(kern, name=name, grid=(hh, s // ts), in_specs=[blk, blk, blk], out_specs=blk,
                  out_shape=jax.ShapeDtypeStruct((hh, s, LANES), F32), compiler_params=_params("parallel", "parallel"))(p0, p1, p2)


def _dsa_bwd_prep(comb, dop, *, name):
    hh, s, _ = comb.shape
    ts = _pick(s, (512, 256))

    def kern(c_ref, d_ref, o_ref):
        lane = lax.broadcasted_iota(jnp.int32, (ts, LANES), 1)
        c = c_ref[...]
        d = d_ref[...]
        dsum = jnp.sum(jnp.where(lane < HEAD_DIM, c * d, 0.0), axis=1, keepdims=True)
        o_ref[...] = jnp.where(lane < HEAD_DIM, d, jnp.where(lane < HEAD_DIM + 32, c, dsum))

    blk = pl.BlockSpec((None, ts, LANES), lambda a, i: (a, i, 0))
    return _pcall(kern, name=name, grid=(hh, s // ts), in_specs=[blk, blk], out_specs=blk,
                  out_shape=jax.ShapeDtypeStruct((hh, s, LANES), F32), compiler_params=_params("parallel", "parallel"))(comb, dop)


def _dsa_bwd(q, k, vp, pk, *, name):
    rows = q.shape[0]
    s = rows // 12
    big = QB * DSA_SUB
    nsteps = rows // big
    nblk = rows // QB

    def kern(q_ref, qn_ref, k_ref, kp_ref, v_ref, vpv_ref, p_ref, pn_ref, dq_ref, dk_ref, dv_ref):
        t = pl.program_id(0)
        bps = _dsa_seq_blocks(t, s)
        rel = _dsa_rel()
        lane = lax.broadcasted_iota(jnp.int32, (QB, LANES), 1)

        def stats(pa):
            lse = jnp.max(jnp.where(jnp.logical_and(lane >= HEAD_DIM, lane < HEAD_DIM + 32), pa, -jnp.inf), axis=1, keepdims=True)
            dsum = jnp.max(jnp.where(lane >= HEAD_DIM + 32, pa, -jnp.inf), axis=1, keepdims=True)
            return lse, dsum

        def pair(qa, pa, st, kb, vb, mask):
            p = jnp.where(mask, jnp.exp(_nt(qa, kb) * SCALE - st[0]), 0.0)
            ds = p * (_nt(pa, vb) - st[1]) * SCALE
            return _nn(ds, kb), _tn(ds, qa), _tn(p, pa)

        for a in range(DSA_SUB):
            qa = q_ref[pl.ds(a * QB, QB), :]
            pa = p_ref[pl.ds(a * QB, QB), :]
            st = stats(pa)
            kc = k_ref[pl.ds(a * QB, QB), :]
            vc = v_ref[pl.ds(a * QB, QB), :]
            if a == 0:
                kpv, vpv = kp_ref[...], vpv_ref[...]
            else:
                kpv, vpv = k_ref[pl.ds((a - 1) * QB, QB), :], v_ref[pl.ds((a - 1) * QB, QB), :]
            has_prev = (t * DSA_SUB + a) % bps != 0
            dq_c, dk_c, dv_c = pair(qa, pa, st, kc, vc, rel <= 0)
            dq_p, dk_p, dv_p = pair(qa, pa, st, kpv, vpv, _prev_mask(rel, has_prev))
            dq_ref[pl.ds(a * QB, QB), :] = dq_c + dq_p
            if a == 0:
                dk_ref[pl.ds(0, QB), :] = dk_c
                dv_ref[pl.ds(0, QB), :] = dv_c
            else:
                dk_ref[pl.ds(a * QB, QB), :] = dk_c
                dv_ref[pl.ds(a * QB, QB), :] = dv_c
                dk_ref[pl.ds((a - 1) * QB, QB), :] += dk_p
                dv_ref[pl.ds((a - 1) * QB, QB), :] += dv_p
        nxt = t * DSA_SUB + DSA_SUB
        has_next = jnp.logical_and(nxt < nblk, nxt % bps != 0)
        last = (DSA_SUB - 1) * QB
        pn = pn_ref[...]
        _, dk_n, dv_n = pair(qn_ref[...], pn, stats(pn), k_ref[pl.ds(last, QB), :], v_ref[pl.ds(last, QB), :],
                             _prev_mask(rel, has_next))
        dk_ref[pl.ds(last, QB), :] += dk_n
        dv_ref[pl.ds(last, QB), :] += dv_n

    def prev_map(t):
        return (jnp.maximum(t * DSA_SUB - 1, 0), 0)

    def next_map(t):
        return (jnp.minimum(t * DSA_SUB + DSA_SUB, nblk - 1), 0)

    cur64 = pl.BlockSpec((big, HEAD_DIM), lambda t: (t, 0))
    cur128 = pl.BlockSpec((big, LANES), lambda t: (t, 0))
    specs = [cur64, pl.BlockSpec((QB, HEAD_DIM), next_map), cur64, pl.BlockSpec((QB, HEAD_DIM), prev_map),
             cur128, pl.BlockSpec((QB, LANES), prev_map), cur128, pl.BlockSpec((QB, LANES), next_map)]
    return _pcall(kern, name=name, grid=(nsteps,), in_specs=specs, out_specs=[cur64, cur64, cur128],
                  out_shape=[jax.ShapeDtypeStruct((rows, HEAD_DIM), F32), jax.ShapeDtypeStruct((rows, HEAD_DIM), F32),
                             jax.ShapeDtypeStruct((rows, LANES), F32)],
                  compiler_params=_params("parallel"))(q, q, k, k, vp, vp, pk, pk)


def _mem_fwd(q, km, vm, *, name):
    hh, s, hd = q.shape
    ml = km.shape[1]
    tq = _pick(s, (512, 256))

    def kern(q_ref, k_ref, v_ref, o_ref):
        sc = _nt(q_ref[...], k_ref[...]) * SCALE
        e = jnp.exp(sc - jnp.max(sc, axis=1, keepdims=True))
        p = e / jnp.sum(e, axis=1, keepdims=True)
        o_ref[...] = _nn(p, v_ref[...])

    blk = pl.BlockSpec((None, tq, hd), lambda a, i: (a, i, 0))
    kv = pl.BlockSpec((None, ml, hd), lambda a, i: (a, 0, 0))
    return _pcall(kern, name=name, grid=(hh, s // tq), in_specs=[blk, kv, kv], out_specs=blk,
                  out_shape=jax.ShapeDtypeStruct((hh, s, hd), F32), compiler_params=_params("parallel", "parallel"))(q, km, vm)


def _mem_bwd(q, km, vm, do, *, name):
    hh, s, hd = q.shape
    ml = km.shape[1]
    tq = _pick(s, (512, 256))

    def kern(q_ref, k_ref, v_ref, do_ref, dq_ref, dk_ref, dv_ref):
        @pl.when(pl.program_id(1) == 0)
        def _():
            dk_ref[...] = jnp.zeros_like(dk_ref)
            dv_ref[...] = jnp.zeros_like(dv_ref)

        qb = q_ref[...]
        dob = do_ref[...]
        sc = _nt(qb, k_ref[...]) * SCALE
        e = jnp.exp(sc - jnp.max(sc, axis=1, keepdims=True))
        p = e / jnp.sum(e, axis=1, keepdims=True)
        dp = _nt(dob, v_ref[...])
        ds = p * (dp - jnp.sum(p * dp, axis=1, keepdims=True)) * SCALE
        dq_ref[...] = _nn(ds, k_ref[...])
        dk_ref[...] += _tn(ds, qb)
        dv_ref[...] += _tn(p, dob)

    blk = pl.BlockSpec((None, tq, hd), lambda a, i: (a, i, 0))
    kv = pl.BlockSpec((None, ml, hd), lambda a, i: (a, 0, 0))
    kvs = jax.ShapeDtypeStruct((hh, ml, hd), F32)
    return _pcall(kern, name=name, grid=(hh, s // tq), in_specs=[blk, kv, kv, blk], out_specs=[blk, kv, kv],
                  out_shape=[jax.ShapeDtypeStruct((hh, s, hd), F32), kvs, kvs],
                  compiler_params=_params("parallel", "arbitrary"))(q, km, vm, do)


DSA_BT = QB * max(r for _, r in DSA_GROUPS)
DSA_UB = 4


def _bdot(a, b, ca, cb):
    return lax.dot_general(a.astype(MXU_DT), b.astype(MXU_DT), (((ca,), (cb,)), ((0,), (0,))), preferred_element_type=F32)


def _bnt(a, b):
    return _bdot(a, b, 2, 2)


def _bnn(a, b):
    return _bdot(a, b, 2, 1)


def _btn(a, b):
    return _bdot(a, b, 1, 1)


def _unit_rows(r, c, b):
    return pl.ds(c + QB * r * b, QB, stride=r)


def _pair_cols(t, first):
    return [jnp.max(jnp.where(first, t, -jnp.inf), axis=1, keepdims=True),
            jnp.max(jnp.where(first, -jnp.inf, t), axis=1, keepdims=True)]


def _dsa2_fwd(qn, kn, v32, g, *, name):
    s = qn.shape[0]
    r = DSA_GROUPS[g][1]
    nbk = DSA_BT // (QB * r)
    npair = DSA_OUT_W // LANES

    def kern(q_ref, k_ref, kp_ref, v_ref, vp_ref, o_ref, l_ref):
        t = pl.program_id(1)
        first = _first_half((QB, LANES))
        rel = _dsa_rel()
        units = [(c, b) for c in range(r) for b in range(nbk)]
        for u0 in range(0, len(units), DSA_UB):
            batch = units[u0:u0 + DSA_UB]
            qs, kcs, vcs, kps, vps, masks = [], [], [], [], [], []
            for c, b in batch:
                rows = _unit_rows(r, c, b)
                kc, vc = k_ref[rows, :].astype(MXU_DT), v_ref[rows, :].astype(MXU_DT)
                if b > 0:
                    prow = _unit_rows(r, c, b - 1)
                    kpv, vpv, has_prev = k_ref[prow, :], v_ref[prow, :], True
                else:
                    prow = _unit_rows(r, c, nbk - 1)
                    kpv, vpv, has_prev = kp_ref[prow, :], vp_ref[prow, :], t > 0
                for qe in _split_pair(q_ref[rows, :], first):
                    qs.append(qe.astype(MXU_DT))
                    kcs.append(kc)
                    vcs.append(vc)
                    kps.append(kpv.astype(MXU_DT))
                    vps.append(vpv.astype(MXU_DT))
                    masks.append(_prev_mask(rel, has_prev))
            qq = jnp.stack(qs)
            sc = jnp.where(rel <= 0, _bnt(qq, jnp.stack(kcs)) * SCALE, -jnp.inf)
            sp = _bnt(qq, jnp.stack(kps)) * SCALE
            sp = jnp.stack([jnp.where(mk, sp[n], -jnp.inf) for n, mk in enumerate(masks)])
            m = jnp.maximum(jnp.max(sc, axis=2, keepdims=True), jnp.max(sp, axis=2, keepdims=True))
            pc = jnp.exp(sc - m)
            pp = jnp.exp(sp - m)
            den = jnp.sum(pc, axis=2, keepdims=True) + jnp.sum(pp, axis=2, keepdims=True)
            out = (_bnn(pc, jnp.stack(vcs)) + _bnn(pp, jnp.stack(vps))) / den
            lse = m + jnp.log(den)
            for idx, (c, b) in enumerate(batch):
                rows = _unit_rows(r, c, b)
                o_ref[rows, :] = jnp.where(first, out[2 * idx], out[2 * idx + 1])
                l_ref[rows, :] = jnp.where(first, lse[2 * idx], lse[2 * idx + 1])

    npg = DSA_HPG * HEAD_DIM // LANES
    cur = pl.BlockSpec((DSA_BT, LANES), lambda a, t: (t, npg * g + a))
    prev = pl.BlockSpec((DSA_BT, LANES), lambda a, t: (jnp.maximum(t - 1, 0), npg * g + a))
    out = pl.BlockSpec((DSA_BT, LANES), lambda a, t: (t, a))
    shp = jax.ShapeDtypeStruct((s, DSA_OUT_W), F32)
    return _pcall(kern, name=name, grid=(npair, s // DSA_BT), in_specs=[cur, cur, prev, cur, prev], out_specs=[out, out],
                  out_shape=[shp, shp], compiler_params=_params("parallel", "parallel"))(qn, kn, kn, v32, v32)


def _dsa2_combine(parts, *, name):
    s, wd = parts[0][0].shape
    ts = _pick(s, (512, 256))

    def kern(o0, l0, o1, l1, o2, l2, o_ref, l_ref):
        ls = [l0[...], l1[...], l2[...]]
        m = jnp.maximum(jnp.maximum(ls[0], ls[1]), ls[2])
        es = [jnp.exp(l - m) for l in ls]
        den = es[0] + es[1] + es[2]
        o_ref[...] = (es[0] * o0[...] + es[1] * o1[...] + es[2] * o2[...]) / den
        l_ref[...] = m + jnp.log(den)

    blk = pl.BlockSpec((ts, wd), lambda i: (i, 0))
    shp = jax.ShapeDtypeStruct((s, wd), F32)
    flat = [t for pair in parts for t in pair]
    return _pcall(kern, name=name, grid=(s // ts,), in_specs=[blk] * 6, out_specs=[blk, blk], out_shape=[shp, shp],
                  compiler_params=_params("parallel"))(*flat)


def _dsa2_prep(o, do, *, name):
    s, wd = o.shape
    ts = _pick(s, (512, 256))

    def kern(o_ref, do_ref, d_ref):
        d_ref[...] = _head_mean(do_ref[...] * o_ref[...], _block_diag(wd)) * HEAD_DIM

    blk = pl.BlockSpec((ts, wd), lambda i: (i, 0))
    return _pcall(kern, name=name, grid=(s // ts,), in_specs=[blk, blk], out_specs=blk,
                  out_shape=jax.ShapeDtypeStruct((s, wd), F32), compiler_params=_params("parallel"))(o, do)


def _dsa2_bwd(qn, kn, v32, do, lse, dd, g, *, name):
    s = qn.shape[0]
    r = DSA_GROUPS[g][1]
    nbk = DSA_BT // (QB * r)
    npair = DSA_OUT_W // LANES
    nsteps = s // DSA_BT

    def kern(q_ref, qn_ref, k_ref, kp_ref, v_ref, vp_ref, do_ref, don_ref, l_ref, ln_ref, d_ref, dn_ref,
             dq_ref, dk_ref, dv_ref):
        t = pl.program_id(1)
        first = _first_half((QB, LANES))
        rel = _dsa_rel()

        def pairs(items):
            qq = jnp.stack([it[0].astype(MXU_DT) for it in items])
            dd = jnp.stack([it[1].astype(MXU_DT) for it in items])
            kk = jnp.stack([it[4].astype(MXU_DT) for it in items])
            vv = jnp.stack([it[5].astype(MXU_DT) for it in items])
            p = jnp.exp(_bnt(qq, kk) * SCALE - jnp.stack([it[2] for it in items]))
            p = jnp.stack([jnp.where(it[6], p[n], 0.0) for n, it in enumerate(items)])
            ds = p * (_bnt(dd, vv) - jnp.stack([it[3] for it in items])) * SCALE
            return _bnn(ds, kk), _btn(ds, qq), _btn(p, dd)

        def heads(rows, qr, dor, lr, dr):
            return list(zip(_split_pair(qr[rows, :], first), _split_pair(dor[rows, :], first),
                            _pair_cols(lr[rows, :], first), _pair_cols(dr[rows, :], first)))

        units = [(c, b) for c in range(r) for b in range(nbk)]
        dk_of, dv_of = [None] * len(units), [None] * len(units)
        for u0 in range(0, len(units), DSA_UB // 2):
            batch = list(enumerate(units))[u0:u0 + DSA_UB // 2]
            items = []
            for u, (c, b) in batch:
                rows = _unit_rows(r, c, b)
                kc, vc = k_ref[rows, :], v_ref[rows, :]
                if b > 0:
                    prow = _unit_rows(r, c, b - 1)
                    kpv, vpv, pmask = k_ref[prow, :], v_ref[prow, :], _prev_mask(rel, True)
                else:
                    prow = _unit_rows(r, c, nbk - 1)
                    kpv, vpv, pmask = kp_ref[prow, :], vp_ref[prow, :], _prev_mask(rel, t > 0)
                for hd in heads(rows, q_ref, do_ref, l_ref, d_ref):
                    items.append(hd + (kc, vc, rel <= 0))
                    items.append(hd + (kpv, vpv, pmask))
            dq, dk, dv = pairs(items)
            for n, (u, (c, b)) in enumerate(batch):
                dq_ref[_unit_rows(r, c, b), :] = jnp.where(first, dq[4 * n] + dq[4 * n + 1], dq[4 * n + 2] + dq[4 * n + 3])
                dk_of[u] = dk[4 * n] + dk[4 * n + 2]
                dv_of[u] = dv[4 * n] + dv[4 * n + 2]
                if b > 0:
                    dk_of[u - 1] = dk_of[u - 1] + (dk[4 * n + 1] + dk[4 * n + 3])
                    dv_of[u - 1] = dv_of[u - 1] + (dv[4 * n + 1] + dv[4 * n + 3])
        lasts = [c * nbk + nbk - 1 for c in range(r)]
        for c0 in range(0, r, DSA_UB):
            chunk = list(range(c0, min(c0 + DSA_UB, r)))
            items = []
            for c in chunk:
                last = _unit_rows(r, c, nbk - 1)
                for hd in heads(_unit_rows(r, c, 0), qn_ref, don_ref, ln_ref, dn_ref):
                    items.append(hd + (k_ref[last, :], v_ref[last, :], _prev_mask(rel, t < nsteps - 1)))
            _, dk, dv = pairs(items)
            for n, c in enumerate(chunk):
                dk_of[lasts[c]] = dk_of[lasts[c]] + (dk[2 * n] + dk[2 * n + 1])
                dv_of[lasts[c]] = dv_of[lasts[c]] + (dv[2 * n] + dv[2 * n + 1])
        for u, (c, b) in enumerate(units):
            dk_ref[_unit_rows(r, c, b), :] = dk_of[u]
            dv_ref[_unit_rows(r, c, b), :] = dv_of[u]

    npg = DSA_HPG * HEAD_DIM // LANES

    def at(shift, col):
        return pl.BlockSpec((DSA_BT, LANES), lambda a, t: (jnp.clip(t + shift, 0, nsteps - 1), col(a)))

    gcol = lambda a: npg * g + a
    ocol = lambda a: a
    specs = [at(0, gcol), at(1, gcol), at(0, gcol), at(-1, gcol), at(0, gcol), at(-1, gcol),
             at(0, ocol), at(1, ocol), at(0, ocol), at(1, ocol), at(0, ocol), at(1, ocol)]
    shp = jax.ShapeDtypeStruct((s, DSA_OUT_W), F32)
    return _pcall(kern, name=name, grid=(npair, nsteps), in_specs=specs, out_specs=[at(0, ocol)] * 3, out_shape=[shp, shp, shp],
                  compiler_params=_params("parallel", "parallel"))(qn, qn, kn, kn, v32, v32, do, do, lse, lse, dd, dd)


def _mem2_fwd(qn, km, kv, *, name):
    s = qn.shape[0]
    ml = km.shape[0]
    tq = _pick(s, (512, 256))
    npair = MEM_W // LANES

    def kern(q_ref, k_ref, v_ref, o_ref):
        first = _first_half((tq, LANES))
        outs = []
        for qe in _split_pair(q_ref[...], first):
            sc = _nt(qe, k_ref[...]) * SCALE
            e = jnp.exp(sc - jnp.max(sc, axis=1, keepdims=True))
            outs.append(_nn(e / jnp.sum(e, axis=1, keepdims=True), v_ref[...]))
        o_ref[...] = jnp.where(first, outs[0], outs[1])

    blk = pl.BlockSpec((tq, LANES), lambda a, i: (i, a))
    return _pcall(kern, name=name, grid=(npair, s // tq),
                  in_specs=[blk, pl.BlockSpec((ml, LANES), lambda a, i: (0, a)), pl.BlockSpec((ml, LANES), lambda a, i: (0, npair + a))],
                  out_specs=blk, out_shape=jax.ShapeDtypeStruct((s, MEM_W), F32),
                  compiler_params=_params("parallel", "parallel"))(qn, km, kv)


def _mem2_bwd(qn, km, kv, do, *, name):
    s = qn.shape[0]
    ml = km.shape[0]
    tq = _pick(s, (512, 256))
    npair = MEM_W // LANES

    def kern(q_ref, k_ref, v_ref, do_ref, dq_ref, dk_ref, dv_ref):
        @pl.when(pl.program_id(1) == 0)
        def _():
            dk_ref[...] = jnp.zeros_like(dk_ref)
            dv_ref[...] = jnp.zeros_like(dv_ref)

        first = _first_half((tq, LANES))
        dqs = []
        for qe, doe in zip(_split_pair(q_ref[...], first), _split_pair(do_ref[...], first)):
            sc = _nt(qe, k_ref[...]) * SCALE
            e = jnp.exp(sc - jnp.max(sc, axis=1, keepdims=True))
            p = e / jnp.sum(e, axis=1, keepdims=True)
            dp = _nt(doe, v_ref[...])
            ds = p * (dp - jnp.sum(p * dp, axis=1, keepdims=True)) * SCALE
            dqs.append(_nn(ds, k_ref[...]))
            dk_ref[...] += _tn(ds, qe)
            dv_ref[...] += _tn(p, doe)
        dq_ref[...] = jnp.where(first, dqs[0], dqs[1])

    blk = pl.BlockSpec((tq, LANES), lambda a, i: (i, a))
    kblk = pl.BlockSpec((ml, LANES), lambda a, i: (0, a))
    kshape = jax.ShapeDtypeStruct((ml, MEM_W), F32)
    return _pcall(kern, name=name, grid=(npair, s // tq),
                  in_specs=[blk, kblk, pl.BlockSpec((ml, LANES), lambda a, i: (0, npair + a)), blk],
                  out_specs=[blk, kblk, kblk], out_shape=[jax.ShapeDtypeStruct((s, MEM_W), F32), kshape, kshape],
                  compiler_params=_params("parallel", "arbitrary"))(qn, km, kv, do)


def _merge_fwd(logits, bias, ya, yb, yc, *, name):
    s, d = ya.shape
    ts = _pick(s, (512, 256))

    def kern(l0, l1, l2, b0, b1, b2, a_ref, b_ref, c_ref, o_ref):
        m = (_sigmoid(l0[...] + b0[...]) * a_ref[...] + _sigmoid(l1[...] + b1[...]) * b_ref[...]
             + _sigmoid(l2[...] + b2[...]) * c_ref[...])
        o_ref[...] = m.astype(o_ref.dtype)

    row = pl.BlockSpec((ts, d), lambda i: (i, 0))
    lg = [pl.BlockSpec((ts, d), functools.partial(lambda i, c: (i, c), c=c)) for c in range(3)]
    bs = [pl.BlockSpec((1, d), functools.partial(lambda i, c: (0, c), c=c)) for c in range(3)]
    return _pcall(kern, name=name, grid=(s // ts,), in_specs=lg + bs + [row, row, row], out_specs=row,
                  out_shape=jax.ShapeDtypeStruct((s, d), BF16),
                  compiler_params=_params("parallel"))(logits, logits, logits, bias, bias, bias, ya, yb, yc)


def _merge_bwd(logits, bias, ya, yb, yc, dm, *, name):
    s, d = ya.shape
    ts = _pick(s, (256,))

    def kern(l0, l1, l2, b0, b1, b2, a_ref, b_ref, c_ref, dm_ref, da_ref, db_ref, dc_ref, dl0, dl1, dl2, dbias0, dbias1, dbias2):
        first = pl.program_id(0) == 0
        dmv = dm_ref[...]
        for l_ref, bb_ref, y_ref, dy_ref, dl_ref, dbias_ref in ((l0, b0, a_ref, da_ref, dl0, dbias0), (l1, b1, b_ref, db_ref, dl1, dbias1),
                                                                (l2, b2, c_ref, dc_ref, dl2, dbias2)):
            g = _sigmoid(l_ref[...] + bb_ref[...])
            dy_ref[...] = (dmv * g).astype(dy_ref.dtype)
            dl = dmv * y_ref[...] * g * (1.0 - g)
            dl_ref[...] = dl.astype(dl_ref.dtype)

            @pl.when(first)
            def _():
                dbias_ref[...] = jnp.zeros_like(dbias_ref)

            dbias_ref[...] += jnp.sum(dl, axis=0, keepdims=True)

    row = pl.BlockSpec((ts, d), lambda i: (i, 0))
    lg = [pl.BlockSpec((ts, d), functools.partial(lambda i, c: (i, c), c=c)) for c in range(3)]
    bs = [pl.BlockSpec((1, d), functools.partial(lambda i, c: (0, c), c=c)) for c in range(3)]
    vec = pl.BlockSpec((1, d), lambda i: (0, 0))
    yshape = jax.ShapeDtypeStruct((s, d), BF16)
    vshape = jax.ShapeDtypeStruct((1, d), F32)
    outs = _pcall(kern, name=name, grid=(s // ts,), in_specs=lg + bs + [row, row, row, row],
                  out_specs=[row, row, row, row, row, row, vec, vec, vec],
                  out_shape=[yshape] * 6 + [vshape] * 3,
                  compiler_params=_params("arbitrary"))(logits, logits, logits, bias, bias, bias, ya, yb, yc, dm)
    return outs[0], outs[1], outs[2], outs[3:6], jnp.concatenate(outs[6:9], axis=1)


def _heads(t, n):
    s = t.shape[0]
    return t.reshape(s, n, HEAD_DIM).transpose(1, 0, 2)


def _unheads(t):
    n, s, hd = t.shape
    return t.transpose(1, 0, 2).reshape(s, n * hd)


def _to_class_major(t):
    s = t.shape[0]
    w = t.shape[1] // (DSA_HPG * len(DSA_GROUPS))
    parts = []
    for g, (_, r) in enumerate(DSA_GROUPS):
        tg = t[:, g * DSA_HPG * w:(g + 1) * DSA_HPG * w].reshape(s // r, r, DSA_HPG, w)
        parts.append(tg.transpose(2, 1, 0, 3).reshape(DSA_HPG * s, w))
    return jnp.concatenate(parts, axis=0)


def _slot_to_class_major(t):
    hh, s, w = t.shape
    parts = []
    for _, r in DSA_GROUPS:
        parts.append(t.reshape(hh, s // r, r, w).transpose(0, 2, 1, 3).reshape(hh * s, w))
    return jnp.concatenate(parts, axis=0)


def _from_class_major(t):
    rows, w = t.shape
    s = rows // 12
    out = []
    for g, (_, r) in enumerate(DSA_GROUPS):
        tg = t[g * 4 * s:(g + 1) * 4 * s].reshape(DSA_HPG, r, s // r, w)
        out.append(tg.transpose(0, 2, 1, 3).reshape(DSA_HPG, s, w))
    return out


def _pad_lanes(t):
    return jnp.concatenate([t, jnp.zeros(t.shape[:-1] + (LANES - t.shape[-1],), t.dtype)], axis=-1)


G_FFN1 = ['ffn1_w1', 'ffn1_w3', 'ffn1_w2']
G_FFN2 = ['ffn2_w1', 'ffn2_w3', 'ffn2_w2']
G_MID = [n for n in BIG if n not in G_FFN1 + G_FFN2]


def _ffn_fwd(h, w1, w3, w2, tag, epilogue, side=None):
    carried = None
    if side is None:
        a, b, f = _ffn_up(h, w1, w3, name=f"{tag}_up")
    else:
        (a, b, f), carried = _ffn_up(h, w1, w3, name=f"{tag}_up", side=side)
    outs = _matmul(f, w2, name=f"{tag}_down", alpha=0.5, tm=512, tn=1024, tk=2816, epilogue=epilogue)
    return outs, (h, a, b, f), carried


def _ffn_bwd(x, norm, w1, w3, w2, saved, dy, dyb, tag, side=None, own_side=None):
    h, a, b, f = saved
    dw2 = _matmul(f, dyb, name=f"{tag}_dw2", ta=True, alpha=0.5, tm=1408, tn=1024, tk=2048)
    carried = None
    if side is None:
        da, db = _ffn_dact(dyb, w2, a, b, name=f"{tag}_dact")
    else:
        (da, db), carried = _ffn_dact(dyb, w2, a, b, name=f"{tag}_dact", side=side)
    dw1 = _matmul(h, da, name=f"{tag}_dw1", ta=True, tm=1024, tn=1408, tk=2048)
    dw3 = _matmul(h, db, name=f"{tag}_dw3", ta=True, tm=1024, tn=1408, tk=2048)
    outs = _matmul(da, w1, name=f"{tag}_dh", tb=True, tm=512, tn=1024, tk=1408, pair2=(db, w3),
                   epilogue=(_epi_rms_bwd, [x, dy], [norm], [F32, BF16], 1),
                   side=None if own_side is None else own_side(dw1, dw3, dw2))
    (dx, dxb, dnorm), own = outs if own_side is not None else (outs, None)
    return dx, dxb, dnorm, dw1, dw3, dw2, carried, own


def _local_step(x, mem, loss_target, wl, ws):
    s, d = x.shape
    assert s % (QB * 16) == 0
    rope = _rope_tables(s)
    bf = {n: wl[n].astype(BF16) for n in BIG}
    w = dict(ws)
    w.update(_unpack_gathered(_exchange(_pack_rows(bf, G_FFN1), _two_level_phases(), name="gather_ffn1"), wl, G_FFN1))

    h1 = _rms_fwd(x, w['ffn1_norm'], name="ffn1_rms")
    (x1, h), sv1, late = _ffn_fwd(h1, w['ffn1_w1'], w['ffn1_w3'], w['ffn1_w2'], "ffn1",
                                  (_epi_residual_rms, [x], [w['mix_norm']], [F32, BF16], 0),
                                  side=_side(_pack_rows(bf, G_MID), _two_level_phases()))
    w.update(_unpack_gathered(late, wl, G_MID))
    p = _matmul(h, w['w_in'], name="in_proj", out_dtype=BF16, tn=1024)
    logits = _matmul(h, w['w_gate'], name="gate_proj", tn=1024)
    c_qb, c_kb, c_vb, c_qc = 3 * SB_W, 3 * SB_W + DSA_W, 3 * SB_W + 2 * DSA_W, 3 * SB_W + 3 * DSA_W

    oa_t, late = _sb2_fwd(p, name="sb_fwd", side=_side(_pack_rows(bf, G_FFN2), _two_level_phases()))
    w.update(_unpack_gathered(late, wl, G_FFN2))
    ya = _matmul(oa_t, w['w_branch_sb'], name="sb_out")

    qb_n = _qknorm_fwd(p, c_qb, DSA_W, w['qn_dsa'], rope, name="dsa_qnorm", out_dtype=F32)
    kb_n = _qknorm_fwd(p, c_kb, DSA_W, w['kn_dsa'], rope, name="dsa_knorm", out_dtype=F32)
    vb32 = p[:, c_vb:c_vb + DSA_W].astype(F32)
    groups = range(len(DSA_GROUPS))
    ob_t, lse_b = _dsa2_combine([_dsa2_fwd(qb_n, kb_n, vb32, gi, name=f"dsa_fwd{gi}") for gi in groups], name="dsa_combine")
    yb = _matmul(ob_t, w['w_branch_dsa'], name="dsa_out")

    memh = _rms_fwd(mem, w['mem_norm'], name="mem_rms")
    kv = _matmul(memh, w['w_mem_kv'], name="mem_kv", out_dtype=BF16)
    km_n = _qknorm_fwd(kv, 0, MEM_W, w['kn_mem'], None, name="mem_knorm")
    qc_n = _qknorm_fwd(p, c_qc, MEM_W, w['qn_mem'], None, name="mem_qnorm")
    oc_t = _mem2_fwd(qc_n, km_n, kv, name="mem_fwd")
    yc = _matmul(oc_t, w['w_branch_mem'], name="mem_out")

    merged = _merge_fwd(logits, w['b_gate'], ya, yb, yc, name="merge")
    x2, h2 = _matmul(merged, w['w_out'], name="out_proj", tn=1024,
                     epilogue=(_epi_residual_rms, [x1], [w['ffn2_norm']], [F32, BF16], 0))
    (dx3, dx3b, sq), sv2, _ = _ffn_fwd(h2, w['ffn2_w1'], w['ffn2_w3'], w['ffn2_w2'], "ffn2",
                                       (_epi_loss, [x2, loss_target], [], [F32, BF16], 1))
    loss = jnp.sum(sq) * (0.5 / d)

    g, recv = {}, {}

    def owners(names):
        return _pack_for_owners(g, wl, names).astype(BF16)

    dx2, dx2b, g['ffn2_norm'], g['ffn2_w1'], g['ffn2_w3'], g['ffn2_w2'], _, _ = _ffn_bwd(
        x2, w['ffn2_norm'], w['ffn2_w1'], w['ffn2_w3'], w['ffn2_w2'], sv2, dx3, dx3b, "ffn2")

    g['w_out'] = _matmul(merged, dx2b, name="d_w_out", ta=True, tn=1024, tk=512)
    dm = _matmul(dx2b, w['w_out'], name="d_merged", tb=True, tn=1024)
    dya, dyb, dyc, dlog, g['b_gate'] = _merge_bwd(logits, w['b_gate'], ya, yb, yc, dm, name="d_merge")
    dlogits = jnp.concatenate(dlog, axis=1)

    g['w_branch_sb'] = _matmul(oa_t, dya, name="d_w_sb", ta=True, tn=1024, tk=512)
    g['w_branch_dsa'] = _matmul(ob_t, dyb, name="d_w_dsa", ta=True, tk=512)
    g['w_branch_mem'] = _matmul(oc_t, dyc, name="d_w_mem", ta=True, tk=512)
    doa = _matmul(dya, w['w_branch_sb'], name="d_oa", tb=True, out_dtype=BF16)
    dob = _matmul(dyb, w['w_branch_dsa'], name="d_ob", tb=True)
    doc = _matmul(dyc, w['w_branch_mem'], name="d_oc", tb=True, out_dtype=BF16)

    (dqa, dka, dva), recv['ffn2'] = _sb2_bwd(p, oa_t, doa, name="sb_bwd", side=_side(owners(G_FFN2), _direct_phases(True)))

    dd_b = _dsa2_prep(ob_t, dob, name="dsa_prep")
    dgrp = [_dsa2_bwd(qb_n, kb_n, vb32, dob, lse_b, dd_b, gi, name=f"dsa_bwd{gi}") for gi in groups]
    dvb = jnp.concatenate([t[2] for t in dgrp], axis=1).astype(BF16)
    dqb, g['qn_dsa'] = _qknorm_bwd(p, c_qb, DSA_W, w['qn_dsa'], rope, [t[0] for t in dgrp], name="d_dsa_qnorm")
    dkb, g['kn_dsa'] = _qknorm_bwd(p, c_kb, DSA_W, w['kn_dsa'], rope, [t[1] for t in dgrp], name="d_dsa_knorm")

    dqc_n, dkm_n, dvm = _mem2_bwd(qc_n, km_n, kv, doc, name="mem_bwd")
    dqc, g['qn_mem'] = _qknorm_bwd(p, c_qc, MEM_W, w['qn_mem'], None, dqc_n, name="d_mem_qnorm")
    dkm, g['kn_mem'] = _qknorm_bwd(kv, 0, MEM_W, w['kn_mem'], None, dkm_n, name="d_mem_knorm")
    dkv = jnp.concatenate([dkm, dvm.astype(BF16)], axis=1)
    g['w_mem_kv'] = _matmul(memh, dkv, name="d_w_mem_kv", ta=True)
    dmemh = _matmul(dkv, w['w_mem_kv'], name="d_memh", tb=True)
    _, _, g['mem_norm'] = _rms_bwd(mem, w['mem_norm'], dmemh, None, name="d_mem_rms")

    dp = jnp.concatenate([dqa.astype(BF16), dka.astype(BF16), dva.astype(BF16),
                          dqb, dkb, dvb, dqc], axis=1)
    g['w_in'] = _matmul(h, dp, name="d_w_in", ta=True, tn=2048, tk=1024)
    g['w_gate'] = _matmul(h, dlogits, name="d_w_gate", ta=True, tn=1536, tk=1024)
    dh = _matmul(dp, w['w_in'], name="d_h_in", tb=True, tn=1024, tk=2048)
    dx1, dx1b, g['mix_norm'] = _matmul(dlogits, w['w_gate'], name="d_h_gate", tb=True, tm=512, tn=1024, tk=3072,
                                       epilogue=(_epi_rms_bwd_sum, [dh, x1, dx2], [w['mix_norm']], [F32, BF16], 1))

    def own_side(dw1, dw3, dw2):
        g.update(ffn1_w1=dw1, ffn1_w3=dw3, ffn1_w2=dw2)
        return _side(owners(G_FFN1), _direct_phases(True))

    dx0, _, g['ffn1_norm'], _, _, _, recv['mid'], recv['ffn1'] = _ffn_bwd(
        x, w['ffn1_norm'], w['ffn1_w1'], w['ffn1_w3'], w['ffn1_w2'], sv1, dx1, dx1b, "ffn1",
        side=_side(owners(G_MID), _direct_phases(True)), own_side=own_side)
    return loss, dx0, recv, {n: g[n] for n in SMALL}


def _pack_rows(d, names):
    return jnp.concatenate([d[n].reshape(-1, LANES) for n in names], axis=0)


def _unpack_rows(t, like, names):
    out, off = {}, 0
    for n in names:
        r = like[n].size // LANES
        out[n] = t[off:off + r].reshape(like[n].shape)
        off += r
    return out


def _unpack_gathered(t, local, names):
    out, off = {}, 0
    for n in names:
        r, c = local[n].shape
        rows = r * c // LANES
        blk = t[:, off:off + rows].reshape(N_DEV, r, c)
        out[n] = blk.reshape(N_DEV * r, c) if SHARD_AXIS[n] == 0 else blk.transpose(1, 0, 2).reshape(r, N_DEV * c)
        off += rows
    return out


def _pack_for_owners(g, local, names):
    parts = []
    for n in names:
        r, c = local[n].shape
        blk = g[n].reshape(N_DEV, r, c) if SHARD_AXIS[n] == 0 else g[n].reshape(r, N_DEV, c).transpose(1, 0, 2)
        parts.append(blk.reshape(N_DEV, r * c // LANES, LANES))
    return jnp.concatenate(parts, axis=1)


def _pack_small(d, names, extra_rows):
    parts = []
    for n in names:
        v = d[n].reshape(-1)
        pad = (-v.size) % LANES
        parts.append(jnp.concatenate([v, jnp.zeros((pad,), v.dtype)]).reshape(-1, LANES))
    t = jnp.concatenate(parts, axis=0)
    return jnp.concatenate([t, jnp.zeros((extra_rows, LANES), t.dtype)], axis=0)


def _unpack_small(t, like, names):
    out, off = {}, 0
    for n in names:
        size = like[n].size
        rows = -(-size // LANES)
        out[n] = t[off:off + rows].reshape(-1)[:size].reshape(like[n].shape)
        off += rows
    return out


def _direct_phases(per_peer):
    def descriptors(src_ref, out_ref, send_sems, recv_sems, local_sem):
        x, y, c = lax.axis_index("x"), lax.axis_index("y"), lax.axis_index("c")
        me = 4 * x + 2 * y + c
        mine = pltpu.make_async_copy(src_ref.at[me] if per_peer else src_ref, out_ref.at[me], local_sem)
        copies = []
        for k in range(1, N_DEV):
            px = 1 - x if k & 4 else x
            py = 1 - y if k & 2 else y
            pc = 1 - c if k & 1 else c
            copies.append(pltpu.make_async_remote_copy(
                src_ref=src_ref.at[4 * px + 2 * py + pc] if per_peer else src_ref, dst_ref=out_ref.at[me],
                send_sem=send_sems.at[k - 1], recv_sem=recv_sems.at[k - 1],
                device_id=(px, py, pc), device_id_type=pl.DeviceIdType.MESH))
        return mine, copies

    def start(*refs):
        mine, copies = descriptors(*refs)
        mine.start()
        for cp in copies:
            cp.start()

    def forward(*refs):
        pass

    def finish(*refs):
        mine, copies = descriptors(*refs)
        for cp in copies:
            cp.wait_recv()
        for cp in copies:
            cp.wait_send()
        mine.wait()

    return start, forward, finish


EXCHANGE_SEMS = [pltpu.SemaphoreType.DMA((N_DEV - 1,)), pltpu.SemaphoreType.DMA((N_DEV - 1,)), pltpu.SemaphoreType.DMA]


def _exchange(src, phases, *, name):
    rows = src.shape[-2]

    def body(*refs):
        for phase in phases:
            phase(*refs)

    anyspace = pl.BlockSpec(memory_space=pl.ANY)
    return _pcall(body, name=name, in_specs=[anyspace], out_specs=anyspace,
                  out_shape=jax.ShapeDtypeStruct((N_DEV, rows, LANES), src.dtype), scratch_shapes=list(EXCHANGE_SEMS))(src)


def _side(src, phases):
    start, forward, finish = phases

    def before(first, mid, ins, outs, scratch):
        pl.when(first)(lambda: start(ins[0], outs[0], *scratch))
        pl.when(mid)(lambda: forward(ins[0], outs[0], *scratch))

    def after(last, ins, outs, scratch):
        pl.when(last)(lambda: finish(ins[0], outs[0], *scratch))

    return [src], [jax.ShapeDtypeStruct((N_DEV, src.shape[-2], LANES), src.dtype)], list(EXCHANGE_SEMS), before, after


def _call_2d(kern, *, name, grid, in_specs, out_specs, out_shape, ins, scratch_shapes=(), semantics, side=None):
    if side is None:
        return _pcall(kern, name=name, grid=grid, in_specs=in_specs, out_specs=out_specs, out_shape=out_shape,
                      scratch_shapes=list(scratch_shapes), compiler_params=_params(*semantics))(*ins)
    s_ins, s_shapes, s_scratch, before, after = side
    n_in, n_out, n_scr = len(ins), len(out_shape), len(scratch_shapes)

    def combined(*refs):
        refs = list(refs)
        cut = [n_in, len(s_ins), n_out, len(s_shapes), n_scr, len(s_scratch)]
        parts, pos = [], 0
        for c in cut:
            parts.append(refs[pos:pos + c])
            pos += c
        m_in, c_in, m_out, c_out, m_scr, c_scr = parts
        ids = [pl.program_id(a) for a in range(len(grid))]
        inner_zero = functools.reduce(jnp.logical_and, [i == 0 for i in ids[1:]])
        first = jnp.logical_and(ids[0] == 0, inner_zero)
        mid = jnp.logical_and(ids[0] == grid[0] // 2, inner_zero)
        last = functools.reduce(jnp.logical_and, [i == n - 1 for i, n in zip(ids, grid)])
        before(first, mid, c_in, c_out, c_scr)
        kern(*m_in, *m_out, *m_scr)
        after(last, c_in, c_out, c_scr)

    anyspace = pl.BlockSpec(memory_space=pl.ANY)
    outs = _pcall(combined, name=name, grid=grid, in_specs=list(in_specs) + [anyspace] * len(s_ins),
                  out_specs=list(out_specs) + [anyspace] * len(s_shapes), out_shape=list(out_shape) + s_shapes,
                  scratch_shapes=list(scratch_shapes) + s_scratch, compiler_params=_params(*["arbitrary"] * len(grid)))(*ins, *s_ins)
    return outs[:n_out], outs[n_out]


def _two_level_phases():
    def parts(src_ref, out_ref, send_sems, recv_sems, local_sem):
        x, y, c = lax.axis_index("x"), lax.axis_index("y"), lax.axis_index("c")
        me, sibling = (x, y, c), (x, y, 1 - c)
        chips = [(1 - x, y), (x, 1 - y), (1 - x, 1 - y)]

        def slab(px, py, pc):
            return out_ref.at[4 * px + 2 * py + pc]

        def copy(k, block, to, from_src=False):
            return pltpu.make_async_remote_copy(
                src_ref=src_ref if from_src else slab(*block), dst_ref=slab(*block),
                send_sem=send_sems.at[k], recv_sem=recv_sems.at[k], device_id=to, device_id_type=pl.DeviceIdType.MESH)

        return dict(
            mine=lambda: pltpu.make_async_copy(src_ref, slab(*me), local_sem),
            first=lambda: [copy(0, me, sibling, True)] + [copy(1 + j, me, (*chip, c), True) for j, chip in enumerate(chips)],
            passed=lambda: [copy(4 + j, (*chip, c), sibling) for j, chip in enumerate(chips)],
            landed=lambda: [copy(1 + j, (*chip, c), me) for j, chip in enumerate(chips)],
            late=lambda: [copy(0, sibling, me)] + [copy(4 + j, (*chip, 1 - c), me) for j, chip in enumerate(chips)])

    def start(*refs):
        make = parts(*refs)
        make['mine']().start()
        for cp in make['first']():
            cp.start()

    def forward(*refs):
        make = parts(*refs)
        for arrived, onward in zip(make['landed'](), make['passed']()):
            arrived.wait_recv()
            onward.start()

    def finish(*refs):
        make = parts(*refs)
        for cp in make['late']():
            cp.wait_recv()
        for cp in make['first']() + make['passed']():
            cp.wait_send()
        make['mine']().wait()

    return start, forward, finish


def _adamw(recv, w, m, v, *, name):
    rows = w.shape[0]
    tr = _pick(rows, (512, 256, 128, 64))

    def kern(r_ref, w_ref, m_ref, v_ref, g_ref, d_ref, mo_ref, vo_ref):
        g = r_ref[0].astype(F32)
        for p in range(1, N_DEV):
            g = g + r_ref[p].astype(F32)
        mn = ADAM_B1 * m_ref[...] + (1.0 - ADAM_B1) * g
        vn = ADAM_B2 * v_ref[...] + (1.0 - ADAM_B2) * (g * g)
        m_hat = mn / (1.0 - ADAM_B1 ** ADAM_STEP)
        v_hat = vn / (1.0 - ADAM_B2 ** ADAM_STEP)
        g_ref[...] = g
        d_ref[...] = -ADAM_LR * (m_hat / (jnp.sqrt(v_hat) + ADAM_EPS) + ADAM_WD * w_ref[...])
        mo_ref[...] = mn
        vo_ref[...] = vn

    row = pl.BlockSpec((tr, LANES), lambda i: (i, 0))
    shp = jax.ShapeDtypeStruct((rows, LANES), F32)
    return _pcall(kern, name=name, grid=(rows // tr,), in_specs=[pl.BlockSpec((N_DEV, tr, LANES), lambda i: (0, i, 0)), row, row, row],
                  out_specs=[row, row, row, row], out_shape=[shp, shp, shp, shp], compiler_params=_params("parallel"))(recv, w, m, v)


INPUTS = ['x', 'mem'] + WEIGHTS + ['loss_target'] + ['m_' + n for n in WEIGHTS] + ['v_' + n for n in WEIGHTS]
SMALL_PAD_ROWS = 4


def kernel(x, mem, ffn1_norm, ffn1_w1, ffn1_w3, ffn1_w2, mix_norm, mem_norm, w_in, w_mem_kv, qn_dsa, kn_dsa, qn_mem, kn_mem, w_branch_sb, w_branch_dsa, w_branch_mem, w_gate, b_gate, w_out, ffn2_norm, ffn2_w1, ffn2_w3, ffn2_w2, loss_target, m_ffn1_norm, m_ffn1_w1, m_ffn1_w3, m_ffn1_w2, m_mix_norm, m_mem_norm, m_w_in, m_w_mem_kv, m_qn_dsa, m_kn_dsa, m_qn_mem, m_kn_mem, m_w_branch_sb, m_w_branch_dsa, m_w_branch_mem, m_w_gate, m_b_gate, m_w_out, m_ffn2_norm, m_ffn2_w1, m_ffn2_w3, m_ffn2_w2, v_ffn1_norm, v_ffn1_w1, v_ffn1_w3, v_ffn1_w2, v_mix_norm, v_mem_norm, v_w_in, v_w_mem_kv, v_qn_dsa, v_kn_dsa, v_qn_mem, v_kn_mem, v_w_branch_sb, v_w_branch_dsa, v_w_branch_mem, v_w_gate, v_b_gate, v_w_out, v_ffn2_norm, v_ffn2_w1, v_ffn2_w3, v_ffn2_w2):
    given = dict(zip(INPUTS, (x, mem, ffn1_norm, ffn1_w1, ffn1_w3, ffn1_w2, mix_norm, mem_norm, w_in, w_mem_kv, qn_dsa, kn_dsa, qn_mem, kn_mem, w_branch_sb, w_branch_dsa, w_branch_mem, w_gate, b_gate, w_out, ffn2_norm, ffn2_w1, ffn2_w3, ffn2_w2, loss_target, m_ffn1_norm, m_ffn1_w1, m_ffn1_w3, m_ffn1_w2, m_mix_norm, m_mem_norm, m_w_in, m_w_mem_kv, m_qn_dsa, m_kn_dsa, m_qn_mem, m_kn_mem, m_w_branch_sb, m_w_branch_dsa, m_w_branch_mem, m_w_gate, m_b_gate, m_w_out, m_ffn2_norm, m_ffn2_w1, m_ffn2_w3, m_ffn2_w2, v_ffn1_norm, v_ffn1_w1, v_ffn1_w3, v_ffn1_w2, v_mix_norm, v_mem_norm, v_w_in, v_w_mem_kv, v_qn_dsa, v_kn_dsa, v_qn_mem, v_kn_mem, v_w_branch_sb, v_w_branch_dsa, v_w_branch_mem, v_w_gate, v_b_gate, v_w_out, v_ffn2_norm, v_ffn2_w1, v_ffn2_w3, v_ffn2_w2), strict=True))
    wl = {n: given[n][0] for n in BIG}
    ws = {n: given[n] for n in SMALL}

    loss, dx, recv, g = _local_step(x[0], mem[0], loss_target[0], wl, ws)

    big = [{}, {}, {}, {}]
    for tag, names in (("ffn2", G_FFN2), ("mid", G_MID), ("ffn1", G_FFN1)):
        outs = _adamw(recv[tag], _pack_rows(wl, names), _pack_rows({n: given['m_' + n][0] for n in names}, names),
                      _pack_rows({n: given['v_' + n][0] for n in names}, names), name=f"adamw_{tag}")
        for kind, t in enumerate(outs):
            big[kind].update(_unpack_rows(t, wl, names))

    gs = _pack_small(g, SMALL, SMALL_PAD_ROWS)
    loss_row = gs.shape[0] - SMALL_PAD_ROWS
    gs = gs.at[loss_row, 0].set(loss)
    recv_s = _exchange(gs, _direct_phases(False), name="gather_small")
    small = _adamw(recv_s, _pack_small(ws, SMALL, SMALL_PAD_ROWS), _pack_small({n: given['m_' + n] for n in SMALL}, SMALL, SMALL_PAD_ROWS),
                   _pack_small({n: given['v_' + n] for n in SMALL}, SMALL, SMALL_PAD_ROWS), name="adamw_replicated")
    total_loss = small[0][loss_row, 0]
    small = [_unpack_small(t, ws, SMALL) for t in small]

    outs = [total_loss, dx[None]]
    for kind in range(4):
        outs += [big[kind][n][None] if n in wl else small[kind][n] for n in WEIGHTS]
    return tuple(outs)
```

```python
import functools
import math

import jax
import jax.numpy as jnp
from jax import lax
from jax.experimental import pallas as pl
from jax.experimental.pallas import tpu as pltpu

F32 = jnp.float32
BF16 = jnp.bfloat16
MXU_DT = jnp.bfloat16

N_DEV = 8
HEAD_DIM = 64
SB_HEADS = 8
DSA_GROUPS = ((128, 1), (512, 4), (2048, 16))
DSA_HPG = 4
MEM_HEADS = 4
SB_W = SB_HEADS * HEAD_DIM
DSA_W = DSA_HPG * len(DSA_GROUPS) * HEAD_DIM
DSA_OUT_W = DSA_HPG * HEAD_DIM
MEM_W = MEM_HEADS * HEAD_DIM
ROPE_THETA = 10000.0
NORM_EPS = 1e-6
QB = 128
SCALE = HEAD_DIM ** -0.5
ADAM_LR, ADAM_B1, ADAM_B2, ADAM_EPS, ADAM_WD, ADAM_STEP = 0.001, 0.9, 0.999, 1e-08, 0.01, 10

LANES = 128
VMEM_LIMIT = 48 * 1024 * 1024
SB_DEAD = -110.0 * 1.4426950408889634

WEIGHTS = ['ffn1_norm', 'ffn1_w1', 'ffn1_w3', 'ffn1_w2', 'mix_norm', 'mem_norm', 'w_in', 'w_mem_kv', 'qn_dsa', 'kn_dsa',
           'qn_mem', 'kn_mem', 'w_branch_sb', 'w_branch_dsa', 'w_branch_mem', 'w_gate', 'b_gate', 'w_out', 'ffn2_norm',
           'ffn2_w1', 'ffn2_w3', 'ffn2_w2']
SHARD_AXIS = {'ffn1_norm': None, 'ffn1_w1': 1, 'ffn1_w3': 1, 'ffn1_w2': 0, 'mix_norm': None, 'mem_norm': None, 'w_in': 1,
              'w_mem_kv': 0, 'qn_dsa': None, 'kn_dsa': None, 'qn_mem': None, 'kn_mem': None, 'w_branch_sb': 1,
              'w_branch_dsa': 1, 'w_branch_mem': 1, 'w_gate': 1, 'b_gate': None, 'w_out': 0, 'ffn2_norm': None,
              'ffn2_w1': 1, 'ffn2_w3': 1, 'ffn2_w2': 0}
BIG = [n for n in WEIGHTS if SHARD_AXIS[n] is not None]
SMALL = [n for n in WEIGHTS if SHARD_AXIS[n] is None]


def _pcall(kern, **kw):
    return pl.pallas_call(kern, **kw)


def _params(*sem):
    return pltpu.CompilerParams(dimension_semantics=sem, vmem_limit_bytes=VMEM_LIMIT)


def _dot(a, b, dims):
    return lax.dot_general(a.astype(MXU_DT), b.astype(MXU_DT), (dims, ((), ())), preferred_element_type=F32)


def _nn(a, b):
    return _dot(a, b, ((1,), (0,)))


def _nt(a, b):
    return _dot(a, b, ((1,), (1,)))


def _tn(a, b):
    return _dot(a, b, ((0,), (0,)))


def _pick(n, prefs):
    for p in prefs:
        if n % p == 0:
            return p
    return n


def _matmul(a, b, *, name, ta=False, tb=False, out_dtype=F32, res=None, alpha=1.0, tm=1024, tn=512, tk=1024, pair2=None,
            epilogue=None, side=None):
    if ta:
        kdim, m = a.shape
    else:
        m, kdim = a.shape
    n = b.shape[0] if tb else b.shape[1]
    tm = _pick(m, (tm, 512, 256, 128))
    tn = _pick(n, (tn, 512, 384, 256, 128))
    tk = _pick(kdim, (tk, 1024, 512, 256, 128))
    nk = kdim // tk
    a_spec = pl.BlockSpec((tk, tm), lambda i, j, k: (k, i)) if ta else pl.BlockSpec((tm, tk), lambda i, j, k: (i, k))
    b_spec = pl.BlockSpec((tn, tk), lambda i, j, k: (j, k)) if tb else pl.BlockSpec((tk, tn), lambda i, j, k: (k, j))
    o_spec = pl.BlockSpec((tm, tn), lambda i, j, k: (i, j))
    v_spec = pl.BlockSpec((1, tn), lambda i, j, k: (0, j))
    dims = ((0 if ta else 1,), (1 if tb else 0,))
    n_mm = 2 if pair2 is None else 4
    if epilogue is None:
        row_ins, vec_ins = ([] if res is None else [res]), []
        out_dtypes, n_vec = [out_dtype], 0
    else:
        assert tn == n and res is None
        epi_fn, row_ins, vec_ins, out_dtypes, n_vec = epilogue
    n_row_out = len(out_dtypes)

    def kern(*refs):
        refs = list(refs)
        acc_ref = refs.pop() if nk > 1 else None
        mm = refs[:n_mm]
        extra = refs[n_mm:n_mm + len(row_ins) + len(vec_ins)]
        outs = refs[n_mm + len(extra):]
        i = pl.program_id(0)
        k = pl.program_id(2)

        def product():
            part = _dot(mm[0][...], mm[1][...], dims)
            if pair2 is not None:
                part = part + _dot(mm[2][...], mm[3][...], dims)
            return part

        def finish(r):
            if alpha != 1.0:
                r = r * alpha
            if epilogue is None:
                if extra:
                    r = extra[0][...] + r
                outs[0][...] = r.astype(out_dtype)
                return
            vals = epi_fn(r, *[e[...] for e in extra])
            for o_ref, v in zip(outs[:n_row_out], vals[:n_row_out]):
                o_ref[...] = v.astype(o_ref.dtype)
            for o_ref, v in zip(outs[n_row_out:], vals[n_row_out:]):
                @pl.when(i == 0)
                def _():
                    o_ref[...] = jnp.zeros_like(o_ref)

                o_ref[...] += v

        if nk == 1:
            finish(product())
            return

        @pl.when(k == 0)
        def _():
            acc_ref[...] = jnp.zeros_like(acc_ref)

        acc_ref[...] += product()

        @pl.when(k == nk - 1)
        def _():
            finish(acc_ref[...])

    ins = [a, b] + ([] if pair2 is None else list(pair2)) + list(row_ins) + list(vec_ins)
    specs = [a_spec, b_spec] * (n_mm // 2) + [o_spec] * len(row_ins) + [v_spec] * len(vec_ins)
    out_specs = [o_spec] * n_row_out + [v_spec] * n_vec
    out_shape = [jax.ShapeDtypeStruct((m, n), dt) for dt in out_dtypes] + [jax.ShapeDtypeStruct((1, n), F32)] * n_vec
    outs = _call_2d(kern, name=name, grid=(m // tm, n // tn, nk), in_specs=specs, out_specs=out_specs, out_shape=out_shape,
                    ins=ins, scratch_shapes=[pltpu.VMEM((tm, tn), F32)] if nk > 1 else [],
                    semantics=("arbitrary" if n_vec else "parallel", "parallel", "arbitrary"), side=side)
    carried = None
    if side is not None:
        outs, carried = outs
    outs = outs[0] if epilogue is None else outs
    return outs if side is None else (outs, carried)


def _epi_residual_rms(r, res, gain):
    xn = res + r
    return xn, xn * lax.rsqrt(jnp.mean(xn * xn, axis=-1, keepdims=True) + NORM_EPS) * gain


def _epi_rms_bwd(r, x, dres, gain):
    rs = lax.rsqrt(jnp.mean(x * x, axis=-1, keepdims=True) + NORM_EPS)
    xh = x * rs
    dy = r * gain
    dx = dres + rs * (dy - xh * jnp.mean(dy * xh, axis=-1, keepdims=True))
    return dx, dx, jnp.sum(r * xh, axis=0, keepdims=True)


def _epi_rms_bwd_sum(r, r0, x, dres, gain):
    return _epi_rms_bwd(r + r0, x, dres, gain)


def _epi_loss(r, res, target):
    e = (res + r) - target
    dy = e / e.shape[-1]
    return dy, dy, jnp.sum(e * e, axis=0, keepdims=True)
def _rms_fwd(x, g, *, name, side=None):
    s, d = x.shape
    ts = _pick(s, (512, 256))

    def kern(x_ref, g_ref, h_ref):
        xf = x_ref[...]
        r = lax.rsqrt(jnp.mean(xf * xf, axis=-1, keepdims=True) + NORM_EPS)
        h_ref[...] = (xf * r * g_ref[...]).astype(h_ref.dtype)

    outs = _call_2d(kern, name=name, grid=(s // ts,),
                    in_specs=[pl.BlockSpec((ts, d), lambda i: (i, 0)), pl.BlockSpec((1, d), lambda i: (0, 0))],
                    out_specs=[pl.BlockSpec((ts, d), lambda i: (i, 0))], out_shape=[jax.ShapeDtypeStruct((s, d), BF16)],
                    ins=[x, g], semantics=("parallel",), side=side)
    return outs[0] if side is None else (outs[0][0], outs[1])


def _rms_bwd(x, g, dh, res, *, name):
    s, d = x.shape
    ts = _pick(s, (512, 256))

    def kern(*refs):
        if res is None:
            x_ref, g_ref, dh_ref, dx_ref, dxb_ref, dg_ref = refs
            r_ref = None
        else:
            x_ref, g_ref, dh_ref, r_ref, dx_ref, dxb_ref, dg_ref = refs
        xf = x_ref[...]
        r = lax.rsqrt(jnp.mean(xf * xf, axis=-1, keepdims=True) + NORM_EPS)
        xh = xf * r
        dhf = dh_ref[...].astype(F32)
        dy = dhf * g_ref[...]
        dx = r * (dy - xh * jnp.mean(dy * xh, axis=-1, keepdims=True))
        if r_ref is not None:
            dx = r_ref[...] + dx
        dx_ref[...] = dx
        dxb_ref[...] = dx.astype(dxb_ref.dtype)

        @pl.when(pl.program_id(0) == 0)
        def _():
            dg_ref[...] = jnp.zeros_like(dg_ref)

        dg_ref[...] += jnp.sum(dhf * xh, axis=0, keepdims=True)

    row = pl.BlockSpec((ts, d), lambda i: (i, 0))
    vec = pl.BlockSpec((1, d), lambda i: (0, 0))
    ins = [x, g, dh] + ([] if res is None else [res])
    return _pcall(kern, name=name, grid=(s // ts,), in_specs=[row, vec, row] + ([] if res is None else [row]),
                  out_specs=[row, row, vec],
                  out_shape=[jax.ShapeDtypeStruct((s, d), F32), jax.ShapeDtypeStruct((s, d), BF16), jax.ShapeDtypeStruct((1, d), F32)],
                  compiler_params=_params("arbitrary"))(*ins)


def _sigmoid(x):
    return 1.0 / (1.0 + jnp.exp(-x))


FFN_TM, FFN_TF = 512, 1408


def _ffn_up(h, w1, w3, *, name, side=None):
    s, d = h.shape
    fdim = w1.shape[1]
    tm, tf = _pick(s, (FFN_TM, 256)), _pick(fdim, (FFN_TF, 512, 256, 128))

    def kern(h_ref, w1_ref, w3_ref, a_ref, b_ref, f_ref):
        hb = h_ref[...]
        a = _nn(hb, w1_ref[...])
        b = _nn(hb, w3_ref[...])
        a_ref[...] = a.astype(a_ref.dtype)
        b_ref[...] = b.astype(b_ref.dtype)
        f_ref[...] = (a * _sigmoid(a) * b).astype(f_ref.dtype)

    wspec = pl.BlockSpec((d, tf), lambda i, j: (0, j))
    ospec = pl.BlockSpec((tm, tf), lambda i, j: (i, j))
    shp = jax.ShapeDtypeStruct((s, fdim), BF16)
    return _call_2d(kern, name=name, grid=(s // tm, fdim // tf), in_specs=[pl.BlockSpec((tm, d), lambda i, j: (i, 0)), wspec, wspec],
                    out_specs=[ospec, ospec, ospec], out_shape=[shp, shp, shp], ins=[h, w1, w3],
                    semantics=("parallel", "parallel"), side=side)


def _ffn_dact(dy, w2, a, b, *, name, side=None):
    s, d = dy.shape
    fdim = w2.shape[0]
    tm, tf = _pick(s, (FFN_TM, 256)), _pick(fdim, (FFN_TF, 512, 256, 128))

    half = (tf // LANES + 1) // 2 * LANES

    def kern(dy_ref, w2_ref, a_ref, b_ref, da_ref, db_ref):
        dyb = dy_ref[...]
        for lo, hi in ((0, half), (half, tf)):
            df = _nt(dyb, w2_ref[lo:hi, :]) * 0.5
            av = a_ref[:, lo:hi].astype(F32)
            sg = _sigmoid(av)
            da_ref[:, lo:hi] = (df * b_ref[:, lo:hi].astype(F32) * (sg + av * sg * (1.0 - sg))).astype(da_ref.dtype)
            db_ref[:, lo:hi] = (df * (av * sg)).astype(db_ref.dtype)

    ospec = pl.BlockSpec((tm, tf), lambda i, j: (i, j))
    shp = jax.ShapeDtypeStruct((s, fdim), BF16)
    return _call_2d(kern, name=name, grid=(s // tm, fdim // tf),
                    in_specs=[pl.BlockSpec((tm, d), lambda i, j: (i, 0)), pl.BlockSpec((tf, d), lambda i, j: (j, 0)), ospec, ospec],
                    out_specs=[ospec, ospec], out_shape=[shp, shp], ins=[dy, w2, a, b], semantics=("parallel", "parallel"), side=side)


def _loss_head(y, t, *, name):
    s, d = y.shape
    ts = _pick(s, (512, 256))
    n = s // ts

    def kern(y_ref, t_ref, dy_ref, dyb_ref, l_ref, acc_ref):
        i = pl.program_id(0)

        @pl.when(i == 0)
        def _():
            acc_ref[...] = jnp.zeros_like(acc_ref)

        e = y_ref[...] - t_ref[...]
        dy_ref[...] = e / d
        dyb_ref[...] = (e / d).astype(dyb_ref.dtype)
        acc_ref[...] += jnp.sum(e * e, axis=0, keepdims=True)

        @pl.when(i == n - 1)
        def _():
            l_ref[...] = jnp.sum(acc_ref[...], axis=1, keepdims=True) * (0.5 / d)

    row = pl.BlockSpec((ts, d), lambda i: (i, 0))
    return _pcall(kern, name=name, grid=(n,), in_specs=[row, row], out_specs=[row, row, pl.BlockSpec((1, 1), lambda i: (0, 0))],
                  out_shape=[jax.ShapeDtypeStruct((s, d), F32), jax.ShapeDtypeStruct((s, d), BF16), jax.ShapeDtypeStruct((1, 1), F32)],
                  scratch_shapes=[pltpu.VMEM((1, d), F32)], compiler_params=_params("arbitrary"))(y, t)


def _head_mean(v, bd):
    outs = []
    for c in range(v.shape[1] // LANES):
        x = v[:, c * LANES:(c + 1) * LANES]
        hi = x.astype(BF16)
        lo = (x - hi.astype(F32)).astype(BF16)
        outs.append(lax.dot_general(jnp.concatenate([hi, lo], axis=1), bd, (((1,), (0,)), ((), ())), preferred_element_type=F32))
    return outs[0] if len(outs) == 1 else jnp.concatenate(outs, axis=1)


def _partner(v):
    w = v.shape[1]
    lane = lax.broadcasted_iota(jnp.int32, v.shape, 1)
    return jnp.where(lane % HEAD_DIM < HEAD_DIM // 2, pltpu.roll(v, w - HEAD_DIM // 2, 1), pltpu.roll(v, HEAD_DIM // 2, 1))


def _block_diag(w=None):
    r = (lax.broadcasted_iota(jnp.int32, (2 * LANES, LANES), 0) % LANES) // HEAD_DIM
    c = lax.broadcasted_iota(jnp.int32, (2 * LANES, LANES), 1) // HEAD_DIM
    return jnp.where(r == c, 1.0 / HEAD_DIM, 0.0).astype(BF16)


def _rope_tables(s):
    half = HEAD_DIM // 2
    inv_freq = jnp.power(ROPE_THETA, -jnp.arange(half, dtype=F32) / half)
    ang = jnp.arange(s).astype(F32)[:, None] * inv_freq[None, :]
    cos, sin = jnp.cos(ang), jnp.sin(ang)
    cos2 = jnp.concatenate([cos, cos, cos, cos], axis=1)
    sin2 = jnp.concatenate([-sin, sin, -sin, sin], axis=1)
    return cos2, sin2


def _qknorm_fwd(src, col0, width, gain, rope, *, name, out_dtype=BF16):
    s = src.shape[0]
    ts = _pick(s, (512, 256))
    cb = col0 // width
    assert col0 % width == 0
    reps = width // LANES
    g = jnp.tile(gain, (1, width // HEAD_DIM))

    def kern(*refs):
        if rope is None:
            x_ref, g_ref, o_ref = refs
        else:
            x_ref, g_ref, c_ref, s_ref, o_ref = refs
        x = x_ref[...].astype(F32)
        bd = _block_diag(width)
        r = lax.rsqrt(_head_mean(x * x, bd) + NORM_EPS)
        y = x * r * g_ref[...]
        if rope is not None:
            y = y * jnp.tile(c_ref[...], (1, reps)) + _partner(y) * jnp.tile(s_ref[...], (1, reps))
        o_ref[...] = y.astype(o_ref.dtype)

    xs = pl.BlockSpec((ts, width), lambda i: (i, cb))
    tab = pl.BlockSpec((ts, LANES), lambda i: (i, 0))
    ins = [src, g] + ([] if rope is None else list(rope))
    specs = [xs, pl.BlockSpec((1, width), lambda i: (0, 0))] + ([] if rope is None else [tab, tab])
    return _pcall(kern, name=name, grid=(s // ts,), in_specs=specs, out_specs=pl.BlockSpec((ts, width), lambda i: (i, 0)),
                  out_shape=jax.ShapeDtypeStruct((s, width), out_dtype), compiler_params=_params("parallel"))(*ins)


def _qknorm_bwd(src, col0, width, gain, rope, dout, *, name):
    s = src.shape[0]
    ts = _pick(s, (512, 256))
    cb = col0 // width
    reps = width // LANES
    g = jnp.tile(gain, (1, width // HEAD_DIM))

    douts = list(dout) if isinstance(dout, (list, tuple)) else [dout]
    piece = width // len(douts)

    def kern(*refs):
        refs = list(refs)
        dg_ref = refs.pop()
        dx_ref = refs.pop()
        do_refs = [refs.pop() for _ in douts][::-1]
        if rope is None:
            x_ref, g_ref = refs
        else:
            x_ref, g_ref, c_ref, s_ref = refs
        x = x_ref[...].astype(F32)
        bd = _block_diag(width)
        r = lax.rsqrt(_head_mean(x * x, bd) + NORM_EPS)
        xh = x * r
        dy = jnp.concatenate([d[...].astype(F32) for d in do_refs], axis=1) if len(do_refs) > 1 else do_refs[0][...].astype(F32)
        if rope is not None:
            dy = dy * jnp.tile(c_ref[...], (1, reps)) + _partner(dy * jnp.tile(s_ref[...], (1, reps)))
        dxh = dy * g_ref[...]
        dx_ref[...] = (r * (dxh - xh * _head_mean(dxh * xh, bd))).astype(dx_ref.dtype)

        @pl.when(pl.program_id(0) == 0)
        def _():
            dg_ref[...] = jnp.zeros_like(dg_ref)

        dg_ref[...] += jnp.sum(dy * xh, axis=0, keepdims=True)

    xs = pl.BlockSpec((ts, width), lambda i: (i, cb))
    row = pl.BlockSpec((ts, width), lambda i: (i, 0))
    vec = pl.BlockSpec((1, width), lambda i: (0, 0))
    tab = pl.BlockSpec((ts, LANES), lambda i: (i, 0))
    ins = [src, g] + ([] if rope is None else list(rope)) + douts
    specs = [xs, vec] + ([] if rope is None else [tab, tab]) + [pl.BlockSpec((ts, piece), lambda i: (i, 0))] * len(douts)
    dx, dg = _pcall(kern, name=name, grid=(s // ts,), in_specs=specs, out_specs=[row, vec],
                    out_shape=[jax.ShapeDtypeStruct((s, width), BF16), jax.ShapeDtypeStruct((1, width), F32)],
                    compiler_params=_params("arbitrary"))(*ins)
    return dx, jnp.sum(dg.reshape(width // HEAD_DIM, HEAD_DIM), axis=0, keepdims=True)


def _tri(strict):
    r = lax.broadcasted_iota(jnp.int32, (2 * QB, QB), 0) % QB
    c = lax.broadcasted_iota(jnp.int32, (2 * QB, QB), 1)
    return jnp.where((r > c) if strict else (r >= c), 1.0, 0.0).astype(BF16)


def _split_dot(v, t2):
    hi = v.astype(BF16)
    lo = (v - hi.astype(F32)).astype(BF16)
    return lax.dot_general(jnp.concatenate([hi, lo], axis=1), t2, (((1,), (0,)), ((), ())), preferred_element_type=F32)


LOG2E = 1.4426950408889634


def _log2_sigmoids(z2):
    lf = -(jnp.maximum(z2, 0.0) + jnp.log2(1.0 + jnp.exp2(-jnp.abs(z2))))
    return z2 + lf, lf


def _key_blocks(t):
    s = t.shape[0]
    n = t.shape[1] // HEAD_DIM
    return t.reshape(s // QB, QB, n, HEAD_DIM).transpose(2, 0, 3, 1)


def _from_key_blocks(t):
    n, nb, hd, qb = t.shape
    return t.transpose(1, 3, 0, 2).reshape(nb * qb, n * hd)


SB_SUB = 4
SB2_SUB = 2


def _first_half(shape):
    return lax.broadcasted_iota(jnp.int32, shape, 1) < HEAD_DIM


def _split_pair(t, first):
    zero = jnp.zeros_like(t)
    return [jnp.where(first, t, zero), jnp.where(first, zero, t)]


def _sb2_fwd(p, *, name, side=None):
    s = p.shape[0]
    rq = SB2_SUB * QB
    nq = s // rq
    npair = SB_W // LANES

    def kern(q_ref, k_ref, v_ref, o_ref):
        i = pl.program_id(1)
        first = _first_half((rq, LANES))
        q2 = jnp.concatenate(_split_pair(q_ref[...], first), axis=0)
        t2 = _tri(True)
        rel = lax.broadcasted_iota(jnp.int32, (2 * rq, QB), 1) - lax.broadcasted_iota(jnp.int32, (2 * rq, QB), 0) % rq

        def tile(j, carry, acc, masked):
            off = pl.multiple_of(j * QB, QB)
            ls, lf = _log2_sigmoids(_nt(q2, k_ref[pl.ds(off, QB), :]) * (SCALE * LOG2E))
            if masked:
                before = rel < i * rq - j * QB
                lf = jnp.where(before, lf, 0.0)
            w = jnp.exp2(ls + _split_dot(lf, t2) + carry)
            if masked:
                w = jnp.where(before, w, 0.0)
            return carry + jnp.sum(lf, axis=1, keepdims=True), acc + _nn(w, v_ref[pl.ds(off, QB), :])

        carry, acc = jnp.zeros((2 * rq, 1), F32), jnp.zeros((2 * rq, LANES), F32)
        for a in range(SB2_SUB):
            carry, acc = tile(i * SB2_SUB + (SB2_SUB - 1 - a), carry, acc, True)

        def cond(st):
            return jnp.logical_and(st[0] >= 0, st[1] > 0)

        def body(st):
            carry, acc = tile(st[0], st[2], st[3], False)
            return st[0] - 1, (jnp.max(carry) > SB_DEAD).astype(jnp.int32), carry, acc

        st = lax.while_loop(cond, body, (i * SB2_SUB - 1, jnp.int32(1), carry, acc))
        o_ref[...] = jnp.where(first, st[3][:rq], st[3][rq:])

    outs = _call_2d(kern, name=name, grid=(npair, nq),
                    in_specs=[pl.BlockSpec((rq, LANES), lambda a, i: (i, a)), pl.BlockSpec((s, LANES), lambda a, i: (0, npair + a)),
                              pl.BlockSpec((s, LANES), lambda a, i: (0, 2 * npair + a))],
                    out_specs=[pl.BlockSpec((rq, LANES), lambda a, i: (i, a))], out_shape=[jax.ShapeDtypeStruct((s, SB_W), F32)],
                    ins=[p, p, p], semantics=("parallel", "arbitrary"), side=side)
    return outs[0] if side is None else (outs[0][0], outs[1])


def _sb2_bwd(p, o, do, *, name, side=None):
    s = p.shape[0]
    rq = SB2_SUB * QB
    nq = s // rq
    npair = SB_W // LANES

    def kern(q_ref, k_ref, v_ref, o_ref, do_ref, dq_ref, dk_hbm, dv_hbm, dk_acc, dv_acc, sem):
        pr = pl.program_id(0)
        i = pl.program_id(1)

        @pl.when(i == 0)
        def _():
            dk_acc[...] = jnp.zeros_like(dk_acc)
            dv_acc[...] = jnp.zeros_like(dv_acc)

        first = _first_half((rq, LANES))
        q2 = jnp.concatenate(_split_pair(q_ref[...], first), axis=0)
        do2 = jnp.concatenate(_split_pair(do_ref[...], first), axis=0)
        o2 = o_ref[...]
        dsum = jnp.sum(do2.astype(F32) * jnp.concatenate([o2, o2], axis=0), axis=1, keepdims=True)
        t_strict = _tri(True)
        t_incl = _tri(False)
        rel = lax.broadcasted_iota(jnp.int32, (2 * rq, QB), 1) - lax.broadcasted_iota(jnp.int32, (2 * rq, QB), 0) % rq

        def tile(j, carry, gcarry, dq, masked):
            off = pl.multiple_of(j * QB, QB)
            kt = k_ref[pl.ds(off, QB), :]
            ls, lf = _log2_sigmoids(_nt(q2, kt) * (SCALE * LOG2E))
            if masked:
                before = rel < i * rq - j * QB
                lf = jnp.where(before, lf, 0.0)
            w = jnp.exp2(ls + _split_dot(lf, t_strict) + carry)
            if masked:
                w = jnp.where(before, w, 0.0)
            wr = w.astype(MXU_DT)
            g = _nt(do2, v_ref[pl.ds(off, QB), :]) * wr.astype(F32)
            big_g = dsum - (_split_dot(g, t_incl) + gcarry)
            sig = jnp.exp2(ls)
            dz = g * (1.0 - sig) - sig * big_g
            if masked:
                dz = jnp.where(before, dz, 0.0)
            dz = dz * SCALE
            dk_acc[pl.ds(off, QB), :] += _tn(dz, q2)
            dv_acc[pl.ds(off, QB), :] += _tn(wr, do2)
            return (carry + jnp.sum(lf, axis=1, keepdims=True), gcarry + jnp.sum(g, axis=1, keepdims=True),
                    dq + _nn(dz, kt))

        zc = jnp.zeros((2 * rq, 1), F32)
        carry, gcarry, dq = zc, zc, jnp.zeros((2 * rq, LANES), F32)
        for a in range(SB2_SUB):
            carry, gcarry, dq = tile(i * SB2_SUB + (SB2_SUB - 1 - a), carry, gcarry, dq, True)

        def cond(st):
            return jnp.logical_and(st[0] >= 0, st[1] > 0)

        def body(st):
            carry, gcarry, dq = tile(st[0], st[2], st[3], st[4], False)
            return st[0] - 1, (jnp.max(carry) > SB_DEAD).astype(jnp.int32), carry, gcarry, dq

        st = lax.while_loop(cond, body, (i * SB2_SUB - 1, jnp.int32(1), carry, gcarry, dq))
        dq_ref[...] = jnp.where(first, st[4][:rq], st[4][rq:])

        @pl.when(i == nq - 1)
        def _():
            cols = pl.ds(pl.multiple_of(pr * LANES, LANES), LANES)
            ck = pltpu.make_async_copy(dk_acc, dk_hbm.at[:, cols], sem.at[0])
            cv = pltpu.make_async_copy(dv_acc, dv_hbm.at[:, cols], sem.at[1])
            ck.start()
            cv.start()
            ck.wait()
            cv.wait()

    blk = pl.BlockSpec((rq, LANES), lambda a, i: (i, a))
    anyspace = pl.BlockSpec(memory_space=pl.ANY)
    shp = jax.ShapeDtypeStruct((s, SB_W), F32)
    return _call_2d(kern, name=name, grid=(npair, nq),
                    in_specs=[blk, pl.BlockSpec((s, LANES), lambda a, i: (0, npair + a)),
                              pl.BlockSpec((s, LANES), lambda a, i: (0, 2 * npair + a)), blk, blk],
                    out_specs=[blk, anyspace, anyspace], out_shape=[shp, shp, shp], ins=[p, p, p, o, do],
                    scratch_shapes=[pltpu.VMEM((s, LANES), F32), pltpu.VMEM((s, LANES), F32), pltpu.SemaphoreType.DMA((2,))],
                    semantics=("arbitrary", "arbitrary"), side=side)


def _sb_fwd(q, kt, vt, *, name):
    h, s, hd = q.shape
    rq = SB_SUB * QB
    nq = s // rq
    nb = s // QB

    def kern(q_ref, k_ref, v_ref, o_ref):
        i = pl.program_id(1)
        qb = q_ref[...]
        t2 = _tri(True)
        rel = lax.broadcasted_iota(jnp.int32, (rq, QB), 1) - lax.broadcasted_iota(jnp.int32, (rq, QB), 0)

        def tile(j, carry, acc, masked):
            ls, lf = _log2_sigmoids(_nn(qb, k_ref[j]) * (SCALE * LOG2E))
            if masked:
                before = rel < i * rq - j * QB
                lf = jnp.where(before, lf, 0.0)
            w = jnp.exp2(ls + _split_dot(lf, t2) + carry)
            if masked:
                w = jnp.where(before, w, 0.0)
            return carry + jnp.sum(lf, axis=1, keepdims=True), acc + _nt(w, v_ref[j])

        carry, acc = jnp.zeros((rq, 1), F32), jnp.zeros((rq, hd), F32)
        for a in range(SB_SUB):
            carry, acc = tile(i * SB_SUB + (SB_SUB - 1 - a), carry, acc, True)

        def cond(st):
            return jnp.logical_and(st[0] >= 0, st[1] > 0)

        def body(st):
            j, _, carry, acc = st
            carry, acc = tile(j, carry, acc, False)
            return j - 1, (jnp.max(carry) > SB_DEAD).astype(jnp.int32), carry, acc

        _, _, _, acc = lax.while_loop(cond, body, (i * SB_SUB - 1, jnp.int32(1), carry, acc))
        o_ref[...] = acc

    blk = pl.BlockSpec((None, rq, hd), lambda a, i: (a, i, 0))
    full = pl.BlockSpec((None, nb, hd, QB), lambda a, i: (a, 0, 0, 0))
    return _pcall(kern, name=name, grid=(h, nq), in_specs=[blk, full, full], out_specs=blk,
                  out_shape=jax.ShapeDtypeStruct((h, s, hd), F32), compiler_params=_params("parallel", "arbitrary"))(q, kt, vt)


def _sb_bwd(q, kt, vt, o, do, *, name):
    h, s, hd = q.shape
    rq = SB_SUB * QB
    nq = s // rq
    nb = s // QB

    def kern(q_ref, k_ref, v_ref, o_ref, do_ref, dq_ref, dk_ref, dv_ref):
        i = pl.program_id(1)

        @pl.when(i == 0)
        def _():
            dk_ref[...] = jnp.zeros_like(dk_ref)
            dv_ref[...] = jnp.zeros_like(dv_ref)

        qb = q_ref[...]
        dob = do_ref[...]
        dsum = jnp.sum(dob.astype(F32) * o_ref[...], axis=1, keepdims=True)
        t_strict = _tri(True)
        t_incl = _tri(False)
        rel = lax.broadcasted_iota(jnp.int32, (rq, QB), 1) - lax.broadcasted_iota(jnp.int32, (rq, QB), 0)

        def tile(j, carry, gcarry, dq, masked):
            kb = k_ref[j]
            ls, lf = _log2_sigmoids(_nn(qb, kb) * (SCALE * LOG2E))
            if masked:
                before = rel < i * rq - j * QB
                lf = jnp.where(before, lf, 0.0)
            w = jnp.exp2(ls + _split_dot(lf, t_strict) + carry)
            if masked:
                w = jnp.where(before, w, 0.0)
            wr = w.astype(MXU_DT)
            g = _nn(dob, v_ref[j]) * wr.astype(F32)
            big_g = dsum - (_split_dot(g, t_incl) + gcarry)
            sig = jnp.exp2(ls)
            dz = g * (1.0 - sig) - sig * big_g
            if masked:
                dz = jnp.where(before, dz, 0.0)
            dz = dz * SCALE
            dk_ref[j] += _tn(qb, dz)
            dv_ref[j] += _tn(dob, wr)
            return (carry + jnp.sum(lf, axis=1, keepdims=True), gcarry + jnp.sum(g, axis=1, keepdims=True),
                    dq + _nt(dz, kb))

        carry, gcarry, dq = jnp.zeros((rq, 1), F32), jnp.zeros((rq, 1), F32), jnp.zeros((rq, hd), F32)
        for a in range(SB_SUB):
            carry, gcarry, dq = tile(i * SB_SUB + (SB_SUB - 1 - a), carry, gcarry, dq, True)

        def cond(st):
            return jnp.logical_and(st[0] >= 0, st[1] > 0)

        def body(st):
            j, _, carry, gcarry, dq = st
            carry, gcarry, dq = tile(j, carry, gcarry, dq, False)
            return j - 1, (jnp.max(carry) > SB_DEAD).astype(jnp.int32), carry, gcarry, dq

        st = lax.while_loop(cond, body, (i * SB_SUB - 1, jnp.int32(1), carry, gcarry, dq))
        dq_ref[...] = st[4]

    blk = pl.BlockSpec((None, rq, hd), lambda a, i: (a, i, 0))
    full = pl.BlockSpec((None, nb, hd, QB), lambda a, i: (a, 0, 0, 0))
    kshape = jax.ShapeDtypeStruct((h, nb, hd, QB), F32)
    return _pcall(kern, name=name, grid=(h, nq), in_specs=[blk, full, full, blk, blk], out_specs=[blk, full, full],
                  out_shape=[jax.ShapeDtypeStruct((h, s, hd), F32), kshape, kshape],
                  compiler_params=_params("parallel", "arbitrary"))(q, kt, vt, o, do)


DSA_SUB = 4


def _dsa_seq_blocks(t, s):
    steps_per_group = 4 * s // (QB * DSA_SUB)
    g = t // steps_per_group
    b0, b1, b2 = (s // (QB * r) for _, r in DSA_GROUPS)
    return jnp.where(g == 0, b0, jnp.where(g == 1, b1, b2))


def _dsa_rel():
    qi = lax.broadcasted_iota(jnp.int32, (QB, QB), 0)
    kj = lax.broadcasted_iota(jnp.int32, (QB, QB), 1)
    return kj - qi


def _prev_mask(rel, has_prev):
    return rel >= jnp.where(has_prev, 0, QB)


def _dsa_fwd(q, k, vp, *, name):
    rows = q.shape[0]
    s = rows // 12
    big = QB * DSA_SUB
    nsteps = rows // big

    def kern(q_ref, k_ref, kp_ref, v_ref, vpv_ref, o_ref):
        t = pl.program_id(0)
        bps = _dsa_seq_blocks(t, s)
        rel = _dsa_rel()
        lane = lax.broadcasted_iota(jnp.int32, (QB, LANES), 1)
        for a in range(DSA_SUB):
            qa = q_ref[pl.ds(a * QB, QB), :]
            kc = k_ref[pl.ds(a * QB, QB), :]
            vc = v_ref[pl.ds(a * QB, QB), :]
            if a == 0:
                kpv, vpv = kp_ref[...], vpv_ref[...]
            else:
                kpv, vpv = k_ref[pl.ds((a - 1) * QB, QB), :], v_ref[pl.ds((a - 1) * QB, QB), :]
            has_prev = (t * DSA_SUB + a) % bps != 0
            sc = jnp.where(rel <= 0, _nt(qa, kc) * SCALE, -jnp.inf)
            sp = jnp.where(_prev_mask(rel, has_prev), _nt(qa, kpv) * SCALE, -jnp.inf)
            m = jnp.maximum(jnp.max(sc, axis=1, keepdims=True), jnp.max(sp, axis=1, keepdims=True))
            pc = jnp.exp(sc - m)
            pp = jnp.exp(sp - m)
            den = jnp.sum(pc, axis=1, keepdims=True) + jnp.sum(pp, axis=1, keepdims=True)
            o = (_nn(pc, vc) + _nn(pp, vpv)) / den
            o_ref[pl.ds(a * QB, QB), :] = jnp.where(lane < HEAD_DIM, o, m + jnp.log(den))

    cur64 = pl.BlockSpec((big, HEAD_DIM), lambda t: (t, 0))
    prev64 = pl.BlockSpec((QB, HEAD_DIM), lambda t: (jnp.maximum(t * DSA_SUB - 1, 0), 0))
    cur128 = pl.BlockSpec((big, LANES), lambda t: (t, 0))
    prev128 = pl.BlockSpec((QB, LANES), lambda t: (jnp.maximum(t * DSA_SUB - 1, 0), 0))
    return _pcall(kern, name=name, grid=(nsteps,), in_specs=[cur64, cur64, prev64, cur128, prev128], out_specs=cur128,
                  out_shape=jax.ShapeDtypeStruct((rows, LANES), F32), compiler_params=_params("parallel"))(q, k, k, vp, vp)


def _dsa_combine(p0, p1, p2, *, name):
    hh, s, _ = p0.shape
    ts = _pick(s, (512, 256))

    def kern(a_ref, b_ref, c_ref, o_ref):
        lane = lax.broadcasted_iota(jnp.int32, (ts, LANES), 1)
        xs = [a_ref[...], b_ref[...], c_ref[...]]
        ls = [jnp.where(lane < HEAD_DIM, pltpu.roll(x, HEAD_DIM, 1), x) for x in xs]
        m = jnp.maximum(jnp.maximum(ls[0], ls[1]), ls[2])
        es = [jnp.exp(l - m) for l in ls]
        den = es[0] + es[1] + es[2]
        o = (es[0] * xs[0] + es[1] * xs[1] + es[2] * xs[2]) / den
        o_ref[...] = jnp.where(lane < HEAD_DIM, o, m + jnp.log(den))

    blk = pl.BlockSpec((None, ts, LANES), lambda a, i: (a, i, 0))
    return _pcall(kern, name=name, grid=(hh, s // ts), in_specs=[blk, blk, blk], out_specs=blk,
                  out_shape=jax.ShapeDtypeStruct((hh, s, LANES), F32), compiler_params=_params("parallel", "parallel"))(p0, p1, p2)


def _dsa_bwd_prep(comb, dop, *, name):
    hh, s, _ = comb.shape
    ts = _pick(s, (512, 256))

    def kern(c_ref, d_ref, o_ref):
        lane = lax.broadcasted_iota(jnp.int32, (ts, LANES), 1)
        c = c_ref[...]
        d = d_ref[...]
        dsum = jnp.sum(jnp.where(lane < HEAD_DIM, c * d, 0.0), axis=1, keepdims=True)
        o_ref[...] = jnp.where(lane < HEAD_DIM, d, jnp.where(lane < HEAD_DIM + 32, c, dsum))

    blk = pl.BlockSpec((None, ts, LANES), lambda a, i: (a, i, 0))
    return _pcall(kern, name=name, grid=(hh, s // ts), in_specs=[blk, blk], out_specs=blk,
                  out_shape=jax.ShapeDtypeStruct((hh, s, LANES), F32), compiler_params=_params("parallel", "parallel"))(comb, dop)


def _dsa_bwd(q, k, vp, pk, *, name):
    rows = q.shape[0]
    s = rows // 12
    big = QB * DSA_SUB
    nsteps = rows // big
    nblk = rows // QB

    def kern(q_ref, qn_ref, k_ref, kp_ref, v_ref, vpv_ref, p_ref, pn_ref, dq_ref, dk_ref, dv_ref):
        t = pl.program_id(0)
        bps = _dsa_seq_blocks(t, s)
        rel = _dsa_rel()
        lane = lax.broadcasted_iota(jnp.int32, (QB, LANES), 1)

        def stats(pa):
            lse = jnp.max(jnp.where(jnp.logical_and(lane >= HEAD_DIM, lane < HEAD_DIM + 32), pa, -jnp.inf), axis=1, keepdims=True)
            dsum = jnp.max(jnp.where(lane >= HEAD_DIM + 32, pa, -jnp.inf), axis=1, keepdims=True)
            return lse, dsum

        def pair(qa, pa, st, kb, vb, mask):
            p = jnp.where(mask, jnp.exp(_nt(qa, kb) * SCALE - st[0]), 0.0)
            ds = p * (_nt(pa, vb) - st[1]) * SCALE
            return _nn(ds, kb), _tn(ds, qa), _tn(p, pa)

        for a in range(DSA_SUB):
            qa = q_ref[pl.ds(a * QB, QB), :]
            pa = p_ref[pl.ds(a * QB, QB), :]
            st = stats(pa)
            kc = k_ref[pl.ds(a * QB, QB), :]
            vc = v_ref[pl.ds(a * QB, QB), :]
            if a == 0:
                kpv, vpv = kp_ref[...], vpv_ref[...]
            else:
                kpv, vpv = k_ref[pl.ds((a - 1) * QB, QB), :], v_ref[pl.ds((a - 1) * QB, QB), :]
            has_prev = (t * DSA_SUB + a) % bps != 0
            dq_c, dk_c, dv_c = pair(qa, pa, st, kc, vc, rel <= 0)
            dq_p, dk_p, dv_p = pair(qa, pa, st, kpv, vpv, _prev_mask(rel, has_prev))
            dq_ref[pl.ds(a * QB, QB), :] = dq_c + dq_p
            if a == 0:
                dk_ref[pl.ds(0, QB), :] = dk_c
                dv_ref[pl.ds(0, QB), :] = dv_c
            else:
                dk_ref[pl.ds(a * QB, QB), :] = dk_c
                dv_ref[pl.ds(a * QB, QB), :] = dv_c
                dk_ref[pl.ds((a - 1) * QB, QB), :] += dk_p
                dv_ref[pl.ds((a - 1) * QB, QB), :] += dv_p
        nxt = t * DSA_SUB + DSA_SUB
        has_next = jnp.logical_and(nxt < nblk, nxt % bps != 0)
        last = (DSA_SUB - 1) * QB
        pn = pn_ref[...]
        _, dk_n, dv_n = pair(qn_ref[...], pn, stats(pn), k_ref[pl.ds(last, QB), :], v_ref[pl.ds(last, QB), :],
                             _prev_mask(rel, has_next))
        dk_ref[pl.ds(last, QB), :] += dk_n
        dv_ref[pl.ds(last, QB), :] += dv_n

    def prev_map(t):
        return (jnp.maximum(t * DSA_SUB - 1, 0), 0)

    def next_map(t):
        return (jnp.minimum(t * DSA_SUB + DSA_SUB, nblk - 1), 0)

    cur64 = pl.BlockSpec((big, HEAD_DIM), lambda t: (t, 0))
    cur128 = pl.BlockSpec((big, LANES), lambda t: (t, 0))
    specs = [cur64, pl.BlockSpec((QB, HEAD_DIM), next_map), cur64, pl.BlockSpec((QB, HEAD_DIM), prev_map),
             cur128, pl.BlockSpec((QB, LANES), prev_map), cur128, pl.BlockSpec((QB, LANES), next_map)]
    return _pcall(kern, name=name, grid=(nsteps,), in_specs=specs, out_specs=[cur64, cur64, cur128],
                  out_shape=[jax.ShapeDtypeStruct((rows, HEAD_DIM), F32), jax.ShapeDtypeStruct((rows, HEAD_DIM), F32),
                             jax.ShapeDtypeStruct((rows, LANES), F32)],
                  compiler_params=_params("parallel"))(q, q, k, k, vp, vp, pk, pk)


def _mem_fwd(q, km, vm, *, name):
    hh, s, hd = q.shape
    ml = km.shape[1]
    tq = _pick(s, (512, 256))

    def kern(q_ref, k_ref, v_ref, o_ref):
        sc = _nt(q_ref[...], k_ref[...]) * SCALE
        e = jnp.exp(sc - jnp.max(sc, axis=1, keepdims=True))
        p = e / jnp.sum(e, axis=1, keepdims=True)
        o_ref[...] = _nn(p, v_ref[...])

    blk = pl.BlockSpec((None, tq, hd), lambda a, i: (a, i, 0))
    kv = pl.BlockSpec((None, ml, hd), lambda a, i: (a, 0, 0))
    return _pcall(kern, name=name, grid=(hh, s // tq), in_specs=[blk, kv, kv], out_specs=blk,
                  out_shape=jax.ShapeDtypeStruct((hh, s, hd), F32), compiler_params=_params("parallel", "parallel"))(q, km, vm)


def _mem_bwd(q, km, vm, do, *, name):
    hh, s, hd = q.shape
    ml = km.shape[1]
    tq = _pick(s, (512, 256))

    def kern(q_ref, k_ref, v_ref, do_ref, dq_ref, dk_ref, dv_ref):
        @pl.when(pl.program_id(1) == 0)
        def _():
            dk_ref[...] = jnp.zeros_like(dk_ref)
            dv_ref[...] = jnp.zeros_like(dv_ref)

        qb = q_ref[...]
        dob = do_ref[...]
        sc = _nt(qb, k_ref[...]) * SCALE
        e = jnp.exp(sc - jnp.max(sc, axis=1, keepdims=True))
        p = e / jnp.sum(e, axis=1, keepdims=True)
        dp = _nt(dob, v_ref[...])
        ds = p * (dp - jnp.sum(p * dp, axis=1, keepdims=True)) * SCALE
        dq_ref[...] = _nn(ds, k_ref[...])
        dk_ref[...] += _tn(ds, qb)
        dv_ref[...] += _tn(p, dob)

    blk = pl.BlockSpec((None, tq, hd), lambda a, i: (a, i, 0))
    kv = pl.BlockSpec((None, ml, hd), lambda a, i: (a, 0, 0))
    kvs = jax.ShapeDtypeStruct((hh, ml, hd), F32)
    return _pcall(kern, name=name, grid=(hh, s // tq), in_specs=[blk, kv, kv, blk], out_specs=[blk, kv, kv],
                  out_shape=[jax.ShapeDtypeStruct((hh, s, hd), F32), kvs, kvs],
                  compiler_params=_params("parallel", "arbitrary"))(q, km, vm, do)


DSA_BT = QB * max(r for _, r in DSA_GROUPS)
DSA_UB = 4


def _bdot(a, b, ca, cb):
    return lax.dot_general(a.astype(MXU_DT), b.astype(MXU_DT), (((ca,), (cb,)), ((0,), (0,))), preferred_element_type=F32)


def _bnt(a, b):
    return _bdot(a, b, 2, 2)


def _bnn(a, b):
    return _bdot(a, b, 2, 1)


def _btn(a, b):
    return _bdot(a, b, 1, 1)


def _unit_rows(r, c, b):
    return pl.ds(c + QB * r * b, QB, stride=r)


def _pair_cols(t, first):
    return [jnp.max(jnp.where(first, t, -jnp.inf), axis=1, keepdims=True),
            jnp.max(jnp.where(first, -jnp.inf, t), axis=1, keepdims=True)]


def _dsa2_fwd(qn, kn, v32, g, *, name):
    s = qn.shape[0]
    r = DSA_GROUPS[g][1]
    nbk = DSA_BT // (QB * r)
    npair = DSA_OUT_W // LANES

    def kern(q_ref, k_ref, kp_ref, v_ref, vp_ref, o_ref, l_ref):
        t = pl.program_id(1)
        first = _first_half((QB, LANES))
        rel = _dsa_rel()
        units = [(c, b) for c in range(r) for b in range(nbk)]
        for u0 in range(0, len(units), DSA_UB):
            batch = units[u0:u0 + DSA_UB]
            qs, kcs, vcs, kps, vps, masks = [], [], [], [], [], []
            for c, b in batch:
                rows = _unit_rows(r, c, b)
                kc, vc = k_ref[rows, :].astype(MXU_DT), v_ref[rows, :].astype(MXU_DT)
                if b > 0:
                    prow = _unit_rows(r, c, b - 1)
                    kpv, vpv, has_prev = k_ref[prow, :], v_ref[prow, :], True
                else:
                    prow = _unit_rows(r, c, nbk - 1)
                    kpv, vpv, has_prev = kp_ref[prow, :], vp_ref[prow, :], t > 0
                for qe in _split_pair(q_ref[rows, :], first):
                    qs.append(qe.astype(MXU_DT))
                    kcs.append(kc)
                    vcs.append(vc)
                    kps.append(kpv.astype(MXU_DT))
                    vps.append(vpv.astype(MXU_DT))
                    masks.append(_prev_mask(rel, has_prev))
            qq = jnp.stack(qs)
            sc = jnp.where(rel <= 0, _bnt(qq, jnp.stack(kcs)) * SCALE, -jnp.inf)
            sp = _bnt(qq, jnp.stack(kps)) * SCALE
            sp = jnp.stack([jnp.where(mk, sp[n], -jnp.inf) for n, mk in enumerate(masks)])
            m = jnp.maximum(jnp.max(sc, axis=2, keepdims=True), jnp.max(sp, axis=2, keepdims=True))
            pc = jnp.exp(sc - m)
            pp = jnp.exp(sp - m)
            den = jnp.sum(pc, axis=2, keepdims=True) + jnp.sum(pp, axis=2, keepdims=True)
            out = (_bnn(pc, jnp.stack(vcs)) + _bnn(pp, jnp.stack(vps))) / den
            lse = m + jnp.log(den)
            for idx, (c, b) in enumerate(batch):
                rows = _unit_rows(r, c, b)
                o_ref[rows, :] = jnp.where(first, out[2 * idx], out[2 * idx + 1])
                l_ref[rows, :] = jnp.where(first, lse[2 * idx], lse[2 * idx + 1])

    npg = DSA_HPG * HEAD_DIM // LANES
    cur = pl.BlockSpec((DSA_BT, LANES), lambda a, t: (t, npg * g + a))
    prev = pl.BlockSpec((DSA_BT, LANES), lambda a, t: (jnp.maximum(t - 1, 0), npg * g + a))
    out = pl.BlockSpec((DSA_BT, LANES), lambda a, t: (t, a))
    shp = jax.ShapeDtypeStruct((s, DSA_OUT_W), F32)
    return _pcall(kern, name=name, grid=(npair, s // DSA_BT), in_specs=[cur, cur, prev, cur, prev], out_specs=[out, out],
                  out_shape=[shp, shp], compiler_params=_params("parallel", "parallel"))(qn, kn, kn, v32, v32)


def _dsa2_combine(parts, *, name):
    s, wd = parts[0][0].shape
    ts = _pick(s, (512, 256))

    def kern(o0, l0, o1, l1, o2, l2, o_ref, l_ref):
        ls = [l0[...], l1[...], l2[...]]
        m = jnp.maximum(jnp.maximum(ls[0], ls[1]), ls[2])
        es = [jnp.exp(l - m) for l in ls]
        den = es[0] + es[1] + es[2]
        o_ref[...] = (es[0] * o0[...] + es[1] * o1[...] + es[2] * o2[...]) / den
        l_ref[...] = m + jnp.log(den)

    blk = pl.BlockSpec((ts, wd), lambda i: (i, 0))
    shp = jax.ShapeDtypeStruct((s, wd), F32)
    flat = [t for pair in parts for t in pair]
    return _pcall(kern, name=name, grid=(s // ts,), in_specs=[blk] * 6, out_specs=[blk, blk], out_shape=[shp, shp],
                  compiler_params=_params("parallel"))(*flat)


def _dsa2_prep(o, do, *, name):
    s, wd = o.shape
    ts = _pick(s, (512, 256))

    def kern(o_ref, do_ref, d_ref):
        d_ref[...] = _head_mean(do_ref[...] * o_ref[...], _block_diag(wd)) * HEAD_DIM

    blk = pl.BlockSpec((ts, wd), lambda i: (i, 0))
    return _pcall(kern, name=name, grid=(s // ts,), in_specs=[blk, blk], out_specs=blk,
                  out_shape=jax.ShapeDtypeStruct((s, wd), F32), compiler_params=_params("parallel"))(o, do)


def _dsa2_bwd(qn, kn, v32, do, lse, dd, g, *, name):
    s = qn.shape[0]
    r = DSA_GROUPS[g][1]
    nbk = DSA_BT // (QB * r)
    npair = DSA_OUT_W // LANES
    nsteps = s // DSA_BT

    def kern(q_ref, qn_ref, k_ref, kp_ref, v_ref, vp_ref, do_ref, don_ref, l_ref, ln_ref, d_ref, dn_ref,
             dq_ref, dk_ref, dv_ref):
        t = pl.program_id(1)
        first = _first_half((QB, LANES))
        rel = _dsa_rel()

        def pairs(items):
            qq = jnp.stack([it[0].astype(MXU_DT) for it in items])
            dd = jnp.stack([it[1].astype(MXU_DT) for it in items])
            kk = jnp.stack([it[4].astype(MXU_DT) for it in items])
            vv = jnp.stack([it[5].astype(MXU_DT) for it in items])
            p = jnp.exp(_bnt(qq, kk) * SCALE - jnp.stack([it[2] for it in items]))
            p = jnp.stack([jnp.where(it[6], p[n], 0.0) for n, it in enumerate(items)])
            ds = p * (_bnt(dd, vv) - jnp.stack([it[3] for it in items])) * SCALE
            return _bnn(ds, kk), _btn(ds, qq), _btn(p, dd)

        def heads(rows, qr, dor, lr, dr):
            return list(zip(_split_pair(qr[rows, :], first), _split_pair(dor[rows, :], first),
                            _pair_cols(lr[rows, :], first), _pair_cols(dr[rows, :], first)))

        units = [(c, b) for c in range(r) for b in range(nbk)]
        dk_of, dv_of = [None] * len(units), [None] * len(units)
        for u0 in range(0, len(units), DSA_UB // 2):
            batch = list(enumerate(units))[u0:u0 + DSA_UB // 2]
            items = []
            for u, (c, b) in batch:
                rows = _unit_rows(r, c, b)
                kc, vc = k_ref[rows, :], v_ref[rows, :]
                if b > 0:
                    prow = _unit_rows(r, c, b - 1)
                    kpv, vpv, pmask = k_ref[prow, :], v_ref[prow, :], _prev_mask(rel, True)
                else:
                    prow = _unit_rows(r, c, nbk - 1)
                    kpv, vpv, pmask = kp_ref[prow, :], vp_ref[prow, :], _prev_mask(rel, t > 0)
                for hd in heads(rows, q_ref, do_ref, l_ref, d_ref):
                    items.append(hd + (kc, vc, rel <= 0))
                    items.append(hd + (kpv, vpv, pmask))
            dq, dk, dv = pairs(items)
            for n, (u, (c, b)) in enumerate(batch):
                dq_ref[_unit_rows(r, c, b), :] = jnp.where(first, dq[4 * n] + dq[4 * n + 1], dq[4 * n + 2] + dq[4 * n + 3])
                dk_of[u] = dk[4 * n] + dk[4 * n + 2]
                dv_of[u] = dv[4 * n] + dv[4 * n + 2]
                if b > 0:
                    dk_of[u - 1] = dk_of[u - 1] + (dk[4 * n + 1] + dk[4 * n + 3])
                    dv_of[u - 1] = dv_of[u - 1] + (dv[4 * n + 1] + dv[4 * n + 3])
        lasts = [c * nbk + nbk - 1 for c in range(r)]
        for c0 in range(0, r, DSA_UB):
            chunk = list(range(c0, min(c0 + DSA_UB, r)))
            items = []
            for c in chunk:
                last = _unit_rows(r, c, nbk - 1)
                for hd in heads(_unit_rows(r, c, 0), qn_ref, don_ref, ln_ref, dn_ref):
                    items.append(hd + (k_ref[last, :], v_ref[last, :], _prev_mask(rel, t < nsteps - 1)))
            _, dk, dv = pairs(items)
            for n, c in enumerate(chunk):
                dk_of[lasts[c]] = dk_of[lasts[c]] + (dk[2 * n] + dk[2 * n + 1])
                dv_of[lasts[c]] = dv_of[lasts[c]] + (dv[2 * n] + dv[2 * n + 1])
        for u, (c, b) in enumerate(units):
            dk_ref[_unit_rows(r, c, b), :] = dk_of[u]
            dv_ref[_unit_rows(r, c, b), :] = dv_of[u]

    npg = DSA_HPG * HEAD_DIM // LANES

    def at(shift, col):
        return pl.BlockSpec((DSA_BT, LANES), lambda a, t: (jnp.clip(t + shift, 0, nsteps - 1), col(a)))

    gcol = lambda a: npg * g + a
    ocol = lambda a: a
    specs = [at(0, gcol), at(1, gcol), at(0, gcol), at(-1, gcol), at(0, gcol), at(-1, gcol),
             at(0, ocol), at(1, ocol), at(0, ocol), at(1, ocol), at(0, ocol), at(1, ocol)]
    shp = jax.ShapeDtypeStruct((s, DSA_OUT_W), F32)
    return _pcall(kern, name=name, grid=(npair, nsteps), in_specs=specs, out_specs=[at(0, ocol)] * 3, out_shape=[shp, shp, shp],
                  compiler_params=_params("parallel", "parallel"))(qn, qn, kn, kn, v32, v32, do, do, lse, lse, dd, dd)


def _mem2_fwd(qn, km, kv, *, name):
    s = qn.shape[0]
    ml = km.shape[0]
    tq = _pick(s, (512, 256))
    npair = MEM_W // LANES

    def kern(q_ref, k_ref, v_ref, o_ref):
        first = _first_half((tq, LANES))
        outs = []
        for qe in _split_pair(q_ref[...], first):
            sc = _nt(qe, k_ref[...]) * SCALE
            e = jnp.exp(sc - jnp.max(sc, axis=1, keepdims=True))
            outs.append(_nn(e / jnp.sum(e, axis=1, keepdims=True), v_ref[...]))
        o_ref[...] = jnp.where(first, outs[0], outs[1])

    blk = pl.BlockSpec((tq, LANES), lambda a, i: (i, a))
    return _pcall(kern, name=name, grid=(npair, s // tq),
                  in_specs=[blk, pl.BlockSpec((ml, LANES), lambda a, i: (0, a)), pl.BlockSpec((ml, LANES), lambda a, i: (0, npair + a))],
                  out_specs=blk, out_shape=jax.ShapeDtypeStruct((s, MEM_W), F32),
                  compiler_params=_params("parallel", "parallel"))(qn, km, kv)


def _mem2_bwd(qn, km, kv, do, *, name):
    s = qn.shape[0]
    ml = km.shape[0]
    tq = _pick(s, (512, 256))
    npair = MEM_W // LANES

    def kern(q_ref, k_ref, v_ref, do_ref, dq_ref, dk_ref, dv_ref):
        @pl.when(pl.program_id(1) == 0)
        def _():
            dk_ref[...] = jnp.zeros_like(dk_ref)
            dv_ref[...] = jnp.zeros_like(dv_ref)

        first = _first_half((tq, LANES))
        dqs = []
        for qe, doe in zip(_split_pair(q_ref[...], first), _split_pair(do_ref[...], first)):
            sc = _nt(qe, k_ref[...]) * SCALE
            e = jnp.exp(sc - jnp.max(sc, axis=1, keepdims=True))
            p = e / jnp.sum(e, axis=1, keepdims=True)
            dp = _nt(doe, v_ref[...])
            ds = p * (dp - jnp.sum(p * dp, axis=1, keepdims=True)) * SCALE
            dqs.append(_nn(ds, k_ref[...]))
            dk_ref[...] += _tn(ds, qe)
            dv_ref[...] += _tn(p, doe)
        dq_ref[...] = jnp.where(first, dqs[0], dqs[1])

    blk = pl.BlockSpec((tq, LANES), lambda a, i: (i, a))
    kblk = pl.BlockSpec((ml, LANES), lambda a, i: (0, a))
    kshape = jax.ShapeDtypeStruct((ml, MEM_W), F32)
    return _pcall(kern, name=name, grid=(npair, s // tq),
                  in_specs=[blk, kblk, pl.BlockSpec((ml, LANES), lambda a, i: (0, npair + a)), blk],
                  out_specs=[blk, kblk, kblk], out_shape=[jax.ShapeDtypeStruct((s, MEM_W), F32), kshape, kshape],
                  compiler_params=_params("parallel", "arbitrary"))(qn, km, kv, do)


def _merge_fwd(logits, bias, ya, yb, yc, *, name):
    s, d = ya.shape
    ts = _pick(s, (512, 256))

    def kern(l0, l1, l2, b0, b1, b2, a_ref, b_ref, c_ref, o_ref):
        m = (_sigmoid(l0[...] + b0[...]) * a_ref[...] + _sigmoid(l1[...] + b1[...]) * b_ref[...]
             + _sigmoid(l2[...] + b2[...]) * c_ref[...])
        o_ref[...] = m.astype(o_ref.dtype)

    row = pl.BlockSpec((ts, d), lambda i: (i, 0))
    lg = [pl.BlockSpec((ts, d), functools.partial(lambda i, c: (i, c), c=c)) for c in range(3)]
    bs = [pl.BlockSpec((1, d), functools.partial(lambda i, c: (0, c), c=c)) for c in range(3)]
    return _pcall(kern, name=name, grid=(s // ts,), in_specs=lg + bs + [row, row, row], out_specs=row,
                  out_shape=jax.ShapeDtypeStruct((s, d), BF16),
                  compiler_params=_params("parallel"))(logits, logits, logits, bias, bias, bias, ya, yb, yc)


def _merge_bwd(logits, bias, ya, yb, yc, dm, *, name):
    s, d = ya.shape
    ts = _pick(s, (256,))

    def kern(l0, l1, l2, b0, b1, b2, a_ref, b_ref, c_ref, dm_ref, da_ref, db_ref, dc_ref, dl0, dl1, dl2, dbias0, dbias1, dbias2):
        first = pl.program_id(0) == 0
        dmv = dm_ref[...]
        for l_ref, bb_ref, y_ref, dy_ref, dl_ref, dbias_ref in ((l0, b0, a_ref, da_ref, dl0, dbias0), (l1, b1, b_ref, db_ref, dl1, dbias1),
                                                                (l2, b2, c_ref, dc_ref, dl2, dbias2)):
            g = _sigmoid(l_ref[...] + bb_ref[...])
            dy_ref[...] = (dmv * g).astype(dy_ref.dtype)
            dl = dmv * y_ref[...] * g * (1.0 - g)
            dl_ref[...] = dl.astype(dl_ref.dtype)

            @pl.when(first)
            def _():
                dbias_ref[...] = jnp.zeros_like(dbias_ref)

            dbias_ref[...] += jnp.sum(dl, axis=0, keepdims=True)

    row = pl.BlockSpec((ts, d), lambda i: (i, 0))
    lg = [pl.BlockSpec((ts, d), functools.partial(lambda i, c: (i, c), c=c)) for c in range(3)]
    bs = [pl.BlockSpec((1, d), functools.partial(lambda i, c: (0, c), c=c)) for c in range(3)]
    vec = pl.BlockSpec((1, d), lambda i: (0, 0))
    yshape = jax.ShapeDtypeStruct((s, d), BF16)
    vshape = jax.ShapeDtypeStruct((1, d), F32)
    outs = _pcall(kern, name=name, grid=(s // ts,), in_specs=lg + bs + [row, row, row, row],
                  out_specs=[row, row, row, row, row, row, vec, vec, vec],
                  out_shape=[yshape] * 6 + [vshape] * 3,
                  compiler_params=_params("arbitrary"))(logits, logits, logits, bias, bias, bias, ya, yb, yc, dm)
    return outs[0], outs[1], outs[2], outs[3:6], jnp.concatenate(outs[6:9], axis=1)


def _heads(t, n):
    s = t.shape[0]
    return t.reshape(s, n, HEAD_DIM).transpose(1, 0, 2)


def _unheads(t):
    n, s, hd = t.shape
    return t.transpose(1, 0, 2).reshape(s, n * hd)


def _to_class_major(t):
    s = t.shape[0]
    w = t.shape[1] // (DSA_HPG * len(DSA_GROUPS))
    parts = []
    for g, (_, r) in enumerate(DSA_GROUPS):
        tg = t[:, g * DSA_HPG * w:(g + 1) * DSA_HPG * w].reshape(s // r, r, DSA_HPG, w)
        parts.append(tg.transpose(2, 1, 0, 3).reshape(DSA_HPG * s, w))
    return jnp.concatenate(parts, axis=0)


def _slot_to_class_major(t):
    hh, s, w = t.shape
    parts = []
    for _, r in DSA_GROUPS:
        parts.append(t.reshape(hh, s // r, r, w).transpose(0, 2, 1, 3).reshape(hh * s, w))
    return jnp.concatenate(parts, axis=0)


def _from_class_major(t):
    rows, w = t.shape
    s = rows // 12
    out = []
    for g, (_, r) in enumerate(DSA_GROUPS):
        tg = t[g * 4 * s:(g + 1) * 4 * s].reshape(DSA_HPG, r, s // r, w)
        out.append(tg.transpose(0, 2, 1, 3).reshape(DSA_HPG, s, w))
    return out


def _pad_lanes(t):
    return jnp.concatenate([t, jnp.zeros(t.shape[:-1] + (LANES - t.shape[-1],), t.dtype)], axis=-1)


G_FFN1 = ['ffn1_w1', 'ffn1_w3', 'ffn1_w2']
G_FFN2 = ['ffn2_w1', 'ffn2_w3', 'ffn2_w2']
G_MID = [n for n in BIG if n not in G_FFN1 + G_FFN2]


def _ffn_fwd(h, w1, w3, w2, tag, epilogue, side=None):
    carried = None
    if side is None:
        a, b, f = _ffn_up(h, w1, w3, name=f"{tag}_up")
    else:
        (a, b, f), carried = _ffn_up(h, w1, w3, name=f"{tag}_up", side=side)
    outs = _matmul(f, w2, name=f"{tag}_down", alpha=0.5, tm=512, tn=1024, tk=2816, epilogue=epilogue)
    return outs, (h, a, b, f), carried


def _ffn_bwd(x, norm, w1, w3, w2, saved, dy, dyb, tag, side=None, own_side=None):
    h, a, b, f = saved
    dw2 = _matmul(f, dyb, name=f"{tag}_dw2", ta=True, alpha=0.5, tm=1408, tn=1024, tk=2048)
    carried = None
    if side is None:
        da, db = _ffn_dact(dyb, w2, a, b, name=f"{tag}_dact")
    else:
        (da, db), carried = _ffn_dact(dyb, w2, a, b, name=f"{tag}_dact", side=side)
    dw1 = _matmul(h, da, name=f"{tag}_dw1", ta=True, tm=1024, tn=1408, tk=2048)
    dw3 = _matmul(h, db, name=f"{tag}_dw3", ta=True, tm=1024, tn=1408, tk=2048)
    outs = _matmul(da, w1, name=f"{tag}_dh", tb=True, tm=512, tn=1024, tk=1408, pair2=(db, w3),
                   epilogue=(_epi_rms_bwd, [x, dy], [norm], [F32, BF16], 1),
                   side=None if own_side is None else own_side(dw1, dw3, dw2))
    (dx, dxb, dnorm), own = outs if own_side is not None else (outs, None)
    return dx, dxb, dnorm, dw1, dw3, dw2, carried, own


def _local_step(x, mem, loss_target, wl, ws):
    s, d = x.shape
    assert s % (QB * 16) == 0
    rope = _rope_tables(s)
    bf = {n: wl[n].astype(BF16) for n in BIG}
    w = dict(ws)

    h1, early = _rms_fwd(x, w['ffn1_norm'], name="ffn1_rms", side=_side(_pack_rows(bf, G_FFN1), _two_level_phases()))
    w.update(_unpack_gathered(early, wl, G_FFN1))
    (x1, h), sv1, late = _ffn_fwd(h1, w['ffn1_w1'], w['ffn1_w3'], w['ffn1_w2'], "ffn1",
                                  (_epi_residual_rms, [x], [w['mix_norm']], [F32, BF16], 0),
                                  side=_side(_pack_rows(bf, G_MID), _two_level_phases()))
    w.update(_unpack_gathered(late, wl, G_MID))
    p = _matmul(h, w['w_in'], name="in_proj", out_dtype=BF16, tn=1024)
    logits = _matmul(h, w['w_gate'], name="gate_proj", tn=1024)
    c_qb, c_kb, c_vb, c_qc = 3 * SB_W, 3 * SB_W + DSA_W, 3 * SB_W + 2 * DSA_W, 3 * SB_W + 3 * DSA_W

    oa_t, late = _sb2_fwd(p, name="sb_fwd", side=_side(_pack_rows(bf, G_FFN2), _two_level_phases()))
    w.update(_unpack_gathered(late, wl, G_FFN2))
    ya = _matmul(oa_t, w['w_branch_sb'], name="sb_out")

    qb_n = _qknorm_fwd(p, c_qb, DSA_W, w['qn_dsa'], rope, name="dsa_qnorm", out_dtype=F32)
    kb_n = _qknorm_fwd(p, c_kb, DSA_W, w['kn_dsa'], rope, name="dsa_knorm", out_dtype=F32)
    vb32 = p[:, c_vb:c_vb + DSA_W].astype(F32)
    groups = range(len(DSA_GROUPS))
    ob_t, lse_b = _dsa2_combine([_dsa2_fwd(qb_n, kb_n, vb32, gi, name=f"dsa_fwd{gi}") for gi in groups], name="dsa_combine")
    yb = _matmul(ob_t, w['w_branch_dsa'], name="dsa_out")

    memh = _rms_fwd(mem, w['mem_norm'], name="mem_rms")
    kv = _matmul(memh, w['w_mem_kv'], name="mem_kv", out_dtype=BF16)
    km_n = _qknorm_fwd(kv, 0, MEM_W, w['kn_mem'], None, name="mem_knorm")
    qc_n = _qknorm_fwd(p, c_qc, MEM_W, w['qn_mem'], None, name="mem_qnorm")
    oc_t = _mem2_fwd(qc_n, km_n, kv, name="mem_fwd")
    yc = _matmul(oc_t, w['w_branch_mem'], name="mem_out")

    merged = _merge_fwd(logits, w['b_gate'], ya, yb, yc, name="merge")
    x2, h2 = _matmul(merged, w['w_out'], name="out_proj", tn=1024,
                     epilogue=(_epi_residual_rms, [x1], [w['ffn2_norm']], [F32, BF16], 0))
    (dx3, dx3b, sq), sv2, _ = _ffn_fwd(h2, w['ffn2_w1'], w['ffn2_w3'], w['ffn2_w2'], "ffn2",
                                       (_epi_loss, [x2, loss_target], [], [F32, BF16], 1))
    loss = jnp.sum(sq) * (0.5 / d)

    g, recv = {}, {}

    def owners(names):
        return _pack_for_owners(g, wl, names).astype(BF16)

    dx2, dx2b, g['ffn2_norm'], g['ffn2_w1'], g['ffn2_w3'], g['ffn2_w2'], _, _ = _ffn_bwd(
        x2, w['ffn2_norm'], w['ffn2_w1'], w['ffn2_w3'], w['ffn2_w2'], sv2, dx3, dx3b, "ffn2")

    g['w_out'] = _matmul(merged, dx2b, name="d_w_out", ta=True, tn=1024, tk=512)
    dm = _matmul(dx2b, w['w_out'], name="d_merged", tb=True, tn=1024)
    dya, dyb, dyc, dlog, g['b_gate'] = _merge_bwd(logits, w['b_gate'], ya, yb, yc, dm, name="d_merge")
    dlogits = jnp.concatenate(dlog, axis=1)

    g['w_branch_sb'] = _matmul(oa_t, dya, name="d_w_sb", ta=True, tn=1024, tk=512)
    g['w_branch_dsa'] = _matmul(ob_t, dyb, name="d_w_dsa", ta=True, tk=512)
    g['w_branch_mem'] = _matmul(oc_t, dyc, name="d_w_mem", ta=True, tk=512)
    doa = _matmul(dya, w['w_branch_sb'], name="d_oa", tb=True, out_dtype=BF16)
    dob = _matmul(dyb, w['w_branch_dsa'], name="d_ob", tb=True)
    doc = _matmul(dyc, w['w_branch_mem'], name="d_oc", tb=True, out_dtype=BF16)

    (dqa, dka, dva), recv['ffn2'] = _sb2_bwd(p, oa_t, doa, name="sb_bwd", side=_side(owners(G_FFN2), _direct_phases(True)))

    dd_b = _dsa2_prep(ob_t, dob, name="dsa_prep")
    dgrp = [_dsa2_bwd(qb_n, kb_n, vb32, dob, lse_b, dd_b, gi, name=f"dsa_bwd{gi}") for gi in groups]
    dvb = jnp.concatenate([t[2] for t in dgrp], axis=1).astype(BF16)
    dqb, g['qn_dsa'] = _qknorm_bwd(p, c_qb, DSA_W, w['qn_dsa'], rope, [t[0] for t in dgrp], name="d_dsa_qnorm")
    dkb, g['kn_dsa'] = _qknorm_bwd(p, c_kb, DSA_W, w['kn_dsa'], rope, [t[1] for t in dgrp], name="d_dsa_knorm")

    dqc_n, dkm_n, dvm = _mem2_bwd(qc_n, km_n, kv, doc, name="mem_bwd")
    dqc, g['qn_mem'] = _qknorm_bwd(p, c_qc, MEM_W, w['qn_mem'], None, dqc_n, name="d_mem_qnorm")
    dkm, g['kn_mem'] = _qknorm_bwd(kv, 0, MEM_W, w['kn_mem'], None, dkm_n, name="d_mem_knorm")
    dkv = jnp.concatenate([dkm, dvm.astype(BF16)], axis=1)
    g['w_mem_kv'] = _matmul(memh, dkv, name="d_w_mem_kv", ta=True)
    dmemh = _matmul(dkv, w['w_mem_kv'], name="d_memh", tb=True)
    _, _, g['mem_norm'] = _rms_bwd(mem, w['mem_norm'], dmemh, None, name="d_mem_rms")

    dp = jnp.concatenate([dqa.astype(BF16), dka.astype(BF16), dva.astype(BF16),
                          dqb, dkb, dvb, dqc], axis=1)
    g['w_in'] = _matmul(h, dp, name="d_w_in", ta=True, tn=2048, tk=1024)
    g['w_gate'] = _matmul(h, dlogits, name="d_w_gate", ta=True, tn=1536, tk=1024)
    dh = _matmul(dp, w['w_in'], name="d_h_in", tb=True, tn=1024, tk=2048)
    dx1, dx1b, g['mix_norm'] = _matmul(dlogits, w['w_gate'], name="d_h_gate", tb=True, tm=512, tn=1024, tk=3072,
                                       epilogue=(_epi_rms_bwd_sum, [dh, x1, dx2], [w['mix_norm']], [F32, BF16], 1))

    def own_side(dw1, dw3, dw2):
        g.update(ffn1_w1=dw1, ffn1_w3=dw3, ffn1_w2=dw2)
        return _side(owners(G_FFN1), _direct_phases(True))

    dx0, _, g['ffn1_norm'], _, _, _, recv['mid'], recv['ffn1'] = _ffn_bwd(
        x, w['ffn1_norm'], w['ffn1_w1'], w['ffn1_w3'], w['ffn1_w2'], sv1, dx1, dx1b, "ffn1",
        side=_side(owners(G_MID), _direct_phases(True)), own_side=own_side)
    return loss, dx0, recv, {n: g[n] for n in SMALL}


def _pack_rows(d, names):
    return jnp.concatenate([d[n].reshape(-1, LANES) for n in names], axis=0)


def _unpack_rows(t, like, names):
    out, off = {}, 0
    for n in names:
        r = like[n].size // LANES
        out[n] = t[off:off + r].reshape(like[n].shape)
        off += r
    return out


def _unpack_gathered(t, local, names):
    out, off = {}, 0
    for n in names:
        r, c = local[n].shape
        rows = r * c // LANES
        blk = t[:, off:off + rows].reshape(N_DEV, r, c)
        out[n] = blk.reshape(N_DEV * r, c) if SHARD_AXIS[n] == 0 else blk.transpose(1, 0, 2).reshape(r, N_DEV * c)
        off += rows
    return out


def _pack_for_owners(g, local, names):
    parts = []
    for n in names:
        r, c = local[n].shape
        blk = g[n].reshape(N_DEV, r, c) if SHARD_AXIS[n] == 0 else g[n].reshape(r, N_DEV, c).transpose(1, 0, 2)
        parts.append(blk.reshape(N_DEV, r * c // LANES, LANES))
    return jnp.concatenate(parts, axis=1)


def _pack_small(d, names, extra_rows):
    parts = []
    for n in names:
        v = d[n].reshape(-1)
        pad = (-v.size) % LANES
        parts.append(jnp.concatenate([v, jnp.zeros((pad,), v.dtype)]).reshape(-1, LANES))
    t = jnp.concatenate(parts, axis=0)
    return jnp.concatenate([t, jnp.zeros((extra_rows, LANES), t.dtype)], axis=0)


def _unpack_small(t, like, names):
    out, off = {}, 0
    for n in names:
        size = like[n].size
        rows = -(-size // LANES)
        out[n] = t[off:off + rows].reshape(-1)[:size].reshape(like[n].shape)
        off += rows
    return out


def _direct_phases(per_peer):
    def descriptors(src_ref, out_ref, send_sems, recv_sems, local_sem):
        x, y, c = lax.axis_index("x"), lax.axis_index("y"), lax.axis_index("c")
        me = 4 * x + 2 * y + c
        mine = pltpu.make_async_copy(src_ref.at[me] if per_peer else src_ref, out_ref.at[me], local_sem)
        copies = []
        for k in range(1, N_DEV):
            px = 1 - x if k & 4 else x
            py = 1 - y if k & 2 else y
            pc = 1 - c if k & 1 else c
            copies.append(pltpu.make_async_remote_copy(
                src_ref=src_ref.at[4 * px + 2 * py + pc] if per_peer else src_ref, dst_ref=out_ref.at[me],
                send_sem=send_sems.at[k - 1], recv_sem=recv_sems.at[k - 1],
                device_id=(px, py, pc), device_id_type=pl.DeviceIdType.MESH))
        return mine, copies

    def start(*refs):
        mine, copies = descriptors(*refs)
        mine.start()
        for cp in copies:
            cp.start()

    def forward(*refs):
        pass

    def finish(*refs):
        mine, copies = descriptors(*refs)
        for cp in copies:
            cp.wait_recv()
        for cp in copies:
            cp.wait_send()
        mine.wait()

    return start, forward, finish


EXCHANGE_SEMS = [pltpu.SemaphoreType.DMA((N_DEV - 1,)), pltpu.SemaphoreType.DMA((N_DEV - 1,)), pltpu.SemaphoreType.DMA]


def _exchange(src, phases, *, name):
    rows = src.shape[-2]

    def body(*refs):
        for phase in phases:
            phase(*refs)

    anyspace = pl.BlockSpec(memory_space=pl.ANY)
    return _pcall(body, name=name, in_specs=[anyspace], out_specs=anyspace,
                  out_shape=jax.ShapeDtypeStruct((N_DEV, rows, LANES), src.dtype), scratch_shapes=list(EXCHANGE_SEMS))(src)


def _side(src, phases):
    start, forward, finish = phases

    def before(first, mid, ins, outs, scratch):
        pl.when(first)(lambda: start(ins[0], outs[0], *scratch))
        pl.when(mid)(lambda: forward(ins[0], outs[0], *scratch))

    def after(last, ins, outs, scratch):
        pl.when(last)(lambda: finish(ins[0], outs[0], *scratch))

    return [src], [jax.ShapeDtypeStruct((N_DEV, src.shape[-2], LANES), src.dtype)], list(EXCHANGE_SEMS), before, after


def _call_2d(kern, *, name, grid, in_specs, out_specs, out_shape, ins, scratch_shapes=(), semantics, side=None):
    if side is None:
        return _pcall(kern, name=name, grid=grid, in_specs=in_specs, out_specs=out_specs, out_shape=out_shape,
                      scratch_shapes=list(scratch_shapes), compiler_params=_params(*semantics))(*ins)
    s_ins, s_shapes, s_scratch, before, after = side
    n_in, n_out, n_scr = len(ins), len(out_shape), len(scratch_shapes)

    def combined(*refs):
        refs = list(refs)
        cut = [n_in, len(s_ins), n_out, len(s_shapes), n_scr, len(s_scratch)]
        parts, pos = [], 0
        for c in cut:
            parts.append(refs[pos:pos + c])
            pos += c
        m_in, c_in, m_out, c_out, m_scr, c_scr = parts
        ids = [pl.program_id(a) for a in range(len(grid))]
        inner_zero = functools.reduce(jnp.logical_and, [i == 0 for i in ids[1:]], True)
        first = jnp.logical_and(ids[0] == 0, inner_zero)
        mid = jnp.logical_and(ids[0] == grid[0] // 2, inner_zero)
        last = functools.reduce(jnp.logical_and, [i == n - 1 for i, n in zip(ids, grid)])
        before(first, mid, c_in, c_out, c_scr)
        kern(*m_in, *m_out, *m_scr)
        after(last, c_in, c_out, c_scr)

    anyspace = pl.BlockSpec(memory_space=pl.ANY)
    outs = _pcall(combined, name=name, grid=grid, in_specs=list(in_specs) + [anyspace] * len(s_ins),
                  out_specs=list(out_specs) + [anyspace] * len(s_shapes), out_shape=list(out_shape) + s_shapes,
                  scratch_shapes=list(scratch_shapes) + s_scratch, compiler_params=_params(*["arbitrary"] * len(grid)))(*ins, *s_ins)
    return outs[:n_out], outs[n_out]


def _two_level_phases():
    def parts(src_ref, out_ref, send_sems, recv_sems, local_sem):
        x, y, c = lax.axis_index("x"), lax.axis_index("y"), lax.axis_index("c")
        me, sibling = (x, y, c), (x, y, 1 - c)
        chips = [(1 - x, y), (x, 1 - y), (1 - x, 1 - y)]

        def slab(px, py, pc):
            return out_ref.at[4 * px + 2 * py + pc]

        def copy(k, block, to, from_src=False):
            return pltpu.make_async_remote_copy(
                src_ref=src_ref if from_src else slab(*block), dst_ref=slab(*block),
                send_sem=send_sems.at[k], recv_sem=recv_sems.at[k], device_id=to, device_id_type=pl.DeviceIdType.MESH)

        return dict(
            mine=lambda: pltpu.make_async_copy(src_ref, slab(*me), local_sem),
            first=lambda: [copy(0, me, sibling, True)] + [copy(1 + j, me, (*chip, c), True) for j, chip in enumerate(chips)],
            passed=lambda: [copy(4 + j, (*chip, c), sibling) for j, chip in enumerate(chips)],
            landed=lambda: [copy(1 + j, (*chip, c), me) for j, chip in enumerate(chips)],
            late=lambda: [copy(0, sibling, me)] + [copy(4 + j, (*chip, 1 - c), me) for j, chip in enumerate(chips)])

    def start(*refs):
        make = parts(*refs)
        make['mine']().start()
        for cp in make['first']():
            cp.start()

    def forward(*refs):
        make = parts(*refs)
        for arrived, onward in zip(make['landed'](), make['passed']()):
            arrived.wait_recv()
            onward.start()

    def finish(*refs):
        make = parts(*refs)
        for cp in make['late']():
            cp.wait_recv()
        for cp in make['first']() + make['passed']():
            cp.wait_send()
        make['mine']().wait()

    return start, forward, finish


def _adamw(recv, w, m, v, *, name):
    rows = w.shape[0]
    tr = _pick(rows, (512, 256, 128, 64))

    def kern(r_ref, w_ref, m_ref, v_ref, g_ref, d_ref, mo_ref, vo_ref):
        g = r_ref[0].astype(F32)
        for p in range(1, N_DEV):
            g = g + r_ref[p].astype(F32)
        mn = ADAM_B1 * m_ref[...] + (1.0 - ADAM_B1) * g
        vn = ADAM_B2 * v_ref[...] + (1.0 - ADAM_B2) * (g * g)
        m_hat = mn / (1.0 - ADAM_B1 ** ADAM_STEP)
        v_hat = vn / (1.0 - ADAM_B2 ** ADAM_STEP)
        g_ref[...] = g
        d_ref[...] = -ADAM_LR * (m_hat / (jnp.sqrt(v_hat) + ADAM_EPS) + ADAM_WD * w_ref[...])
        mo_ref[...] = mn
        vo_ref[...] = vn

    row = pl.BlockSpec((tr, LANES), lambda i: (i, 0))
    shp = jax.ShapeDtypeStruct((rows, LANES), F32)
    return _pcall(kern, name=name, grid=(rows // tr,), in_specs=[pl.BlockSpec((N_DEV, tr, LANES), lambda i: (0, i, 0)), row, row, row],
                  out_specs=[row, row, row, row], out_shape=[shp, shp, shp, shp], compiler_params=_params("parallel"))(recv, w, m, v)


INPUTS = ['x', 'mem'] + WEIGHTS + ['loss_target'] + ['m_' + n for n in WEIGHTS] + ['v_' + n for n in WEIGHTS]
SMALL_PAD_ROWS = 4


def kernel(x, mem, ffn1_norm, ffn1_w1, ffn1_w3, ffn1_w2, mix_norm, mem_norm, w_in, w_mem_kv, qn_dsa, kn_dsa, qn_mem, kn_mem, w_branch_sb, w_branch_dsa, w_branch_mem, w_gate, b_gate, w_out, ffn2_norm, ffn2_w1, ffn2_w3, ffn2_w2, loss_target, m_ffn1_norm, m_ffn1_w1, m_ffn1_w3, m_ffn1_w2, m_mix_norm, m_mem_norm, m_w_in, m_w_mem_kv, m_qn_dsa, m_kn_dsa, m_qn_mem, m_kn_mem, m_w_branch_sb, m_w_branch_dsa, m_w_branch_mem, m_w_gate, m_b_gate, m_w_out, m_ffn2_norm, m_ffn2_w1, m_ffn2_w3, m_ffn2_w2, v_ffn1_norm, v_ffn1_w1, v_ffn1_w3, v_ffn1_w2, v_mix_norm, v_mem_norm, v_w_in, v_w_mem_kv, v_qn_dsa, v_kn_dsa, v_qn_mem, v_kn_mem, v_w_branch_sb, v_w_branch_dsa, v_w_branch_mem, v_w_gate, v_b_gate, v_w_out, v_ffn2_norm, v_ffn2_w1, v_ffn2_w3, v_ffn2_w2):
    given = dict(zip(INPUTS, (x, mem, ffn1_norm, ffn1_w1, ffn1_w3, ffn1_w2, mix_norm, mem_norm, w_in, w_mem_kv, qn_dsa, kn_dsa, qn_mem, kn_mem, w_branch_sb, w_branch_dsa, w_branch_mem, w_gate, b_gate, w_out, ffn2_norm, ffn2_w1, ffn2_w3, ffn2_w2, loss_target, m_ffn1_norm, m_ffn1_w1, m_ffn1_w3, m_ffn1_w2, m_mix_norm, m_mem_norm, m_w_in, m_w_mem_kv, m_qn_dsa, m_kn_dsa, m_qn_mem, m_kn_mem, m_w_branch_sb, m_w_branch_dsa, m_w_branch_mem, m_w_gate, m_b_gate, m_w_out, m_ffn2_norm, m_ffn2_w1, m_ffn2_w3, m_ffn2_w2, v_ffn1_norm, v_ffn1_w1, v_ffn1_w3, v_ffn1_w2, v_mix_norm, v_mem_norm, v_w_in, v_w_mem_kv, v_qn_dsa, v_kn_dsa, v_qn_mem, v_kn_mem, v_w_branch_sb, v_w_branch_dsa, v_w_branch_mem, v_w_gate, v_b_gate, v_w_out, v_ffn2_norm, v_ffn2_w1, v_ffn2_w3, v_ffn2_w2), strict=True))
    wl = {n: given[n][0] for n in BIG}
    ws = {n: given[n] for n in SMALL}

    loss, dx, recv, g = _local_step(x[0], mem[0], loss_target[0], wl, ws)

    big = [{}, {}, {}, {}]
    for tag, names in (("ffn2", G_FFN2), ("mid", G_MID), ("ffn1", G_FFN1)):
        outs = _adamw(recv[tag], _pack_rows(wl, names), _pack_rows({n: given['m_' + n][0] for n in names}, names),
                      _pack_rows({n: given['v_' + n][0] for n in names}, names), name=f"adamw_{tag}")
        for kind, t in enumerate(outs):
            big[kind].update(_unpack_rows(t, wl, names))

    gs = _pack_small(g, SMALL, SMALL_PAD_ROWS)
    loss_row = gs.shape[0] - SMALL_PAD_ROWS
    gs = gs.at[loss_row, 0].set(loss)
    recv_s = _exchange(gs, _direct_phases(False), name="gather_small")
    small = _adamw(recv_s, _pack_small(ws, SMALL, SMALL_PAD_ROWS), _pack_small({n: given['m_' + n] for n in SMALL}, SMALL, SMALL_PAD_ROWS),
                   _pack_small({n: given['v_' + n] for n in SMALL}, SMALL, SMALL_PAD_ROWS), name="adamw_replicated")
    total_loss = small[0][loss_row, 0]
    small = [_unpack_small(t, ws, SMALL) for t in small]

    outs = [total_loss, dx[None]]
    for kind in range(4):
        outs += [big[kind][n][None] if n in wl else small[kind][n] for n in WEIGHTS]
    return tuple(outs)
```

```python
import functools

import jax
import jax.numpy as jnp
from jax import lax
from jax.experimental import pallas as pl
from jax.experimental.pallas import tpu as pltpu

F32 = jnp.float32
BF16 = jnp.bfloat16
MXU_DT = jnp.bfloat16

N_DEV = 8
HEAD_DIM = 64
SB_HEADS = 8
DSA_GROUPS = ((128, 1), (512, 4), (2048, 16))
DSA_HPG = 4
MEM_HEADS = 4
SB_W = SB_HEADS * HEAD_DIM
DSA_W = DSA_HPG * len(DSA_GROUPS) * HEAD_DIM
DSA_OUT_W = DSA_HPG * HEAD_DIM
MEM_W = MEM_HEADS * HEAD_DIM
ROPE_THETA = 10000.0
NORM_EPS = 1e-6
QB = 128
SCALE = HEAD_DIM ** -0.5
ADAM_LR, ADAM_B1, ADAM_B2, ADAM_EPS, ADAM_WD, ADAM_STEP = 0.001, 0.9, 0.999, 1e-08, 0.01, 10

LANES = 128
VMEM_LIMIT = 48 * 1024 * 1024
SB_DEAD = -110.0 * 1.4426950408889634

WEIGHTS = ['ffn1_norm', 'ffn1_w1', 'ffn1_w3', 'ffn1_w2', 'mix_norm', 'mem_norm', 'w_in', 'w_mem_kv', 'qn_dsa', 'kn_dsa',
           'qn_mem', 'kn_mem', 'w_branch_sb', 'w_branch_dsa', 'w_branch_mem', 'w_gate', 'b_gate', 'w_out', 'ffn2_norm',
           'ffn2_w1', 'ffn2_w3', 'ffn2_w2']
SHARD_AXIS = {'ffn1_norm': None, 'ffn1_w1': 1, 'ffn1_w3': 1, 'ffn1_w2': 0, 'mix_norm': None, 'mem_norm': None, 'w_in': 1,
              'w_mem_kv': 0, 'qn_dsa': None, 'kn_dsa': None, 'qn_mem': None, 'kn_mem': None, 'w_branch_sb': 1,
              'w_branch_dsa': 1, 'w_branch_mem': 1, 'w_gate': 1, 'b_gate': None, 'w_out': 0, 'ffn2_norm': None,
              'ffn2_w1': 1, 'ffn2_w3': 1, 'ffn2_w2': 0}
BIG = [n for n in WEIGHTS if SHARD_AXIS[n] is not None]
SMALL = [n for n in WEIGHTS if SHARD_AXIS[n] is None]


def _pcall(kern, **kw):
    return pl.pallas_call(kern, **kw)


def _params(*sem):
    return pltpu.CompilerParams(dimension_semantics=sem, vmem_limit_bytes=VMEM_LIMIT)


def _dot(a, b, dims):
    return lax.dot_general(a.astype(MXU_DT), b.astype(MXU_DT), (dims, ((), ())), preferred_element_type=F32)


def _nn(a, b):
    return _dot(a, b, ((1,), (0,)))


def _nt(a, b):
    return _dot(a, b, ((1,), (1,)))


def _tn(a, b):
    return _dot(a, b, ((0,), (0,)))


def _pick(n, prefs):
    for p in prefs:
        if n % p == 0:
            return p
    return n


def _matmul(a, b, *, name, ta=False, tb=False, out_dtype=F32, res=None, alpha=1.0, tm=1024, tn=512, tk=1024, pair2=None,
            epilogue=None, side=None):
    if ta:
        kdim, m = a.shape
    else:
        m, kdim = a.shape
    n = b.shape[0] if tb else b.shape[1]
    tm = _pick(m, (tm, 512, 256, 128))
    tn = _pick(n, (tn, 512, 384, 256, 128))
    tk = _pick(kdim, (tk, 1024, 512, 256, 128))
    nk = kdim // tk
    a_spec = pl.BlockSpec((tk, tm), lambda i, j, k: (k, i)) if ta else pl.BlockSpec((tm, tk), lambda i, j, k: (i, k))
    b_spec = pl.BlockSpec((tn, tk), lambda i, j, k: (j, k)) if tb else pl.BlockSpec((tk, tn), lambda i, j, k: (k, j))
    o_spec = pl.BlockSpec((tm, tn), lambda i, j, k: (i, j))
    v_spec = pl.BlockSpec((1, tn), lambda i, j, k: (0, j))
    dims = ((0 if ta else 1,), (1 if tb else 0,))
    n_mm = 2 if pair2 is None else 4
    if epilogue is None:
        row_ins, vec_ins = ([] if res is None else [res]), []
        out_dtypes, n_vec = [out_dtype], 0
    else:
        assert tn == n and res is None
        epi_fn, row_ins, vec_ins, out_dtypes, n_vec = epilogue
    n_row_out = len(out_dtypes)

    def kern(*refs):
        refs = list(refs)
        acc_ref = refs.pop() if nk > 1 else None
        mm = refs[:n_mm]
        extra = refs[n_mm:n_mm + len(row_ins) + len(vec_ins)]
        outs = refs[n_mm + len(extra):]
        i = pl.program_id(0)
        k = pl.program_id(2)

        def product():
            part = _dot(mm[0][...], mm[1][...], dims)
            if pair2 is not None:
                part = part + _dot(mm[2][...], mm[3][...], dims)
            return part

        def finish(r):
            if alpha != 1.0:
                r = r * alpha
            if epilogue is None:
                if extra:
                    r = extra[0][...] + r
                outs[0][...] = r.astype(out_dtype)
                return
            vals = epi_fn(r, *[e[...] for e in extra])
            for o_ref, v in zip(outs[:n_row_out], vals[:n_row_out]):
                o_ref[...] = v.astype(o_ref.dtype)
            for o_ref, v in zip(outs[n_row_out:], vals[n_row_out:]):
                @pl.when(i == 0)
                def _():
                    o_ref[...] = jnp.zeros_like(o_ref)

                o_ref[...] += v

        if nk == 1:
            finish(product())
            return

        @pl.when(k == 0)
        def _():
            acc_ref[...] = jnp.zeros_like(acc_ref)

        acc_ref[...] += product()

        @pl.when(k == nk - 1)
        def _():
            finish(acc_ref[...])

    ins = [a, b] + ([] if pair2 is None else list(pair2)) + list(row_ins) + list(vec_ins)
    specs = [a_spec, b_spec] * (n_mm // 2) + [o_spec] * len(row_ins) + [v_spec] * len(vec_ins)
    out_specs = [o_spec] * n_row_out + [v_spec] * n_vec
    out_shape = [jax.ShapeDtypeStruct((m, n), dt) for dt in out_dtypes] + [jax.ShapeDtypeStruct((1, n), F32)] * n_vec
    outs = _call_2d(kern, name=name, grid=(m // tm, n // tn, nk), in_specs=specs, out_specs=out_specs, out_shape=out_shape,
                    ins=ins, scratch_shapes=[pltpu.VMEM((tm, tn), F32)] if nk > 1 else [],
                    semantics=("arbitrary" if n_vec else "parallel", "parallel", "arbitrary"), side=side)
    carried = None
    if side is not None:
        outs, carried = outs
    outs = outs[0] if epilogue is None else outs
    return outs if side is None else (outs, carried)


def _epi_residual_rms(r, res, gain):
    xn = res + r
    return xn, xn * lax.rsqrt(jnp.mean(xn * xn, axis=-1, keepdims=True) + NORM_EPS) * gain


def _epi_rms_bwd(r, x, dres, gain):
    rs = lax.rsqrt(jnp.mean(x * x, axis=-1, keepdims=True) + NORM_EPS)
    xh = x * rs
    dy = r * gain
    dx = dres + rs * (dy - xh * jnp.mean(dy * xh, axis=-1, keepdims=True))
    return dx, dx, jnp.sum(r * xh, axis=0, keepdims=True)


def _epi_rms_bwd_sum(r, r0, x, dres, gain):
    return _epi_rms_bwd(r + r0, x, dres, gain)


def _epi_loss(r, res, target):
    e = (res + r) - target
    dy = e / e.shape[-1]
    return dy, dy, jnp.sum(e * e, axis=0, keepdims=True)
def _rms_fwd(x, g, *, name, side=None):
    s, d = x.shape
    ts = _pick(s, (512, 256))

    def kern(x_ref, g_ref, h_ref):
        xf = x_ref[...]
        r = lax.rsqrt(jnp.mean(xf * xf, axis=-1, keepdims=True) + NORM_EPS)
        h_ref[...] = (xf * r * g_ref[...]).astype(h_ref.dtype)

    outs = _call_2d(kern, name=name, grid=(s // ts,),
                    in_specs=[pl.BlockSpec((ts, d), lambda i: (i, 0)), pl.BlockSpec((1, d), lambda i: (0, 0))],
                    out_specs=[pl.BlockSpec((ts, d), lambda i: (i, 0))], out_shape=[jax.ShapeDtypeStruct((s, d), BF16)],
                    ins=[x, g], semantics=("parallel",), side=side)
    return outs[0] if side is None else (outs[0][0], outs[1])


def _rms_bwd(x, g, dh, res, *, name):
    s, d = x.shape
    ts = _pick(s, (512, 256))

    def kern(*refs):
        if res is None:
            x_ref, g_ref, dh_ref, dx_ref, dxb_ref, dg_ref = refs
            r_ref = None
        else:
            x_ref, g_ref, dh_ref, r_ref, dx_ref, dxb_ref, dg_ref = refs
        xf = x_ref[...]
        r = lax.rsqrt(jnp.mean(xf * xf, axis=-1, keepdims=True) + NORM_EPS)
        xh = xf * r
        dhf = dh_ref[...].astype(F32)
        dy = dhf * g_ref[...]
        dx = r * (dy - xh * jnp.mean(dy * xh, axis=-1, keepdims=True))
        if r_ref is not None:
            dx = r_ref[...] + dx
        dx_ref[...] = dx
        dxb_ref[...] = dx.astype(dxb_ref.dtype)

        @pl.when(pl.program_id(0) == 0)
        def _():
            dg_ref[...] = jnp.zeros_like(dg_ref)

        dg_ref[...] += jnp.sum(dhf * xh, axis=0, keepdims=True)

    row = pl.BlockSpec((ts, d), lambda i: (i, 0))
    vec = pl.BlockSpec((1, d), lambda i: (0, 0))
    ins = [x, g, dh] + ([] if res is None else [res])
    return _pcall(kern, name=name, grid=(s // ts,), in_specs=[row, vec, row] + ([] if res is None else [row]),
                  out_specs=[row, row, vec],
                  out_shape=[jax.ShapeDtypeStruct((s, d), F32), jax.ShapeDtypeStruct((s, d), BF16), jax.ShapeDtypeStruct((1, d), F32)],
                  compiler_params=_params("arbitrary"))(*ins)


def _sigmoid(x):
    return 1.0 / (1.0 + jnp.exp(-x))


FFN_TM, FFN_TF = 512, 1408


def _ffn_up(h, w1, w3, *, name, side=None):
    s, d = h.shape
    fdim = w1.shape[1]
    tm, tf = _pick(s, (FFN_TM, 256)), _pick(fdim, (FFN_TF, 512, 256, 128))

    def kern(h_ref, w1_ref, w3_ref, a_ref, b_ref, f_ref):
        hb = h_ref[...]
        a = _nn(hb, w1_ref[...])
        b = _nn(hb, w3_ref[...])
        a_ref[...] = a.astype(a_ref.dtype)
        b_ref[...] = b.astype(b_ref.dtype)
        f_ref[...] = (a * _sigmoid(a) * b).astype(f_ref.dtype)

    wspec = pl.BlockSpec((d, tf), lambda i, j: (0, j))
    ospec = pl.BlockSpec((tm, tf), lambda i, j: (i, j))
    shp = jax.ShapeDtypeStruct((s, fdim), BF16)
    return _call_2d(kern, name=name, grid=(s // tm, fdim // tf), in_specs=[pl.BlockSpec((tm, d), lambda i, j: (i, 0)), wspec, wspec],
                    out_specs=[ospec, ospec, ospec], out_shape=[shp, shp, shp], ins=[h, w1, w3],
                    semantics=("parallel", "parallel"), side=side)


def _ffn_dact(dy, w2, a, b, *, name, side=None):
    s, d = dy.shape
    fdim = w2.shape[0]
    tm, tf = _pick(s, (FFN_TM, 256)), _pick(fdim, (FFN_TF, 512, 256, 128))

    half = (tf // LANES + 1) // 2 * LANES

    def kern(dy_ref, w2_ref, a_ref, b_ref, da_ref, db_ref):
        dyb = dy_ref[...]
        pieces = ((0, half), (half, tf))
        dfs = [_nt(dyb, w2_ref[lo:hi, :]) * 0.5 for lo, hi in pieces]
        for (lo, hi), df in zip(pieces, dfs):
            av = a_ref[:, lo:hi].astype(F32)
            sg = _sigmoid(av)
            da_ref[:, lo:hi] = (df * b_ref[:, lo:hi].astype(F32) * (sg + av * sg * (1.0 - sg))).astype(da_ref.dtype)
            db_ref[:, lo:hi] = (df * (av * sg)).astype(db_ref.dtype)

    ospec = pl.BlockSpec((tm, tf), lambda i, j: (i, j))
    shp = jax.ShapeDtypeStruct((s, fdim), BF16)
    return _call_2d(kern, name=name, grid=(s // tm, fdim // tf),
                    in_specs=[pl.BlockSpec((tm, d), lambda i, j: (i, 0)), pl.BlockSpec((tf, d), lambda i, j: (j, 0)), ospec, ospec],
                    out_specs=[ospec, ospec], out_shape=[shp, shp], ins=[dy, w2, a, b], semantics=("parallel", "parallel"), side=side)


def _head_mean(v, bd):
    outs = []
    for c in range(v.shape[1] // LANES):
        x = v[:, c * LANES:(c + 1) * LANES]
        hi = x.astype(BF16)
        lo = (x - hi.astype(F32)).astype(BF16)
        outs.append(lax.dot_general(jnp.concatenate([hi, lo], axis=1), bd, (((1,), (0,)), ((), ())), preferred_element_type=F32))
    return outs[0] if len(outs) == 1 else jnp.concatenate(outs, axis=1)


def _partner(v):
    w = v.shape[1]
    lane = lax.broadcasted_iota(jnp.int32, v.shape, 1)
    return jnp.where(lane % HEAD_DIM < HEAD_DIM // 2, pltpu.roll(v, w - HEAD_DIM // 2, 1), pltpu.roll(v, HEAD_DIM // 2, 1))


def _block_diag(w=None):
    r = (lax.broadcasted_iota(jnp.int32, (2 * LANES, LANES), 0) % LANES) // HEAD_DIM
    c = lax.broadcasted_iota(jnp.int32, (2 * LANES, LANES), 1) // HEAD_DIM
    return jnp.where(r == c, 1.0 / HEAD_DIM, 0.0).astype(BF16)


def _rope_tables(s):
    half = HEAD_DIM // 2
    inv_freq = jnp.power(ROPE_THETA, -jnp.arange(half, dtype=F32) / half)
    ang = jnp.arange(s).astype(F32)[:, None] * inv_freq[None, :]
    cos, sin = jnp.cos(ang), jnp.sin(ang)
    cos2 = jnp.concatenate([cos, cos, cos, cos], axis=1)
    sin2 = jnp.concatenate([-sin, sin, -sin, sin], axis=1)
    return cos2, sin2


def _qknorm_fwd(src, col0, width, gain, rope, *, name, out_dtype=BF16):
    s = src.shape[0]
    ts = _pick(s, (512, 256))
    cb = col0 // width
    assert col0 % width == 0
    reps = width // LANES
    g = jnp.tile(gain, (1, width // HEAD_DIM))

    def kern(*refs):
        if rope is None:
            x_ref, g_ref, o_ref = refs
        else:
            x_ref, g_ref, c_ref, s_ref, o_ref = refs
        x = x_ref[...].astype(F32)
        bd = _block_diag(width)
        r = lax.rsqrt(_head_mean(x * x, bd) + NORM_EPS)
        y = x * r * g_ref[...]
        if rope is not None:
            y = y * jnp.tile(c_ref[...], (1, reps)) + _partner(y) * jnp.tile(s_ref[...], (1, reps))
        o_ref[...] = y.astype(o_ref.dtype)

    xs = pl.BlockSpec((ts, width), lambda i: (i, cb))
    tab = pl.BlockSpec((ts, LANES), lambda i: (i, 0))
    ins = [src, g] + ([] if rope is None else list(rope))
    specs = [xs, pl.BlockSpec((1, width), lambda i: (0, 0))] + ([] if rope is None else [tab, tab])
    return _pcall(kern, name=name, grid=(s // ts,), in_specs=specs, out_specs=pl.BlockSpec((ts, width), lambda i: (i, 0)),
                  out_shape=jax.ShapeDtypeStruct((s, width), out_dtype), compiler_params=_params("parallel"))(*ins)


def _qknorm_bwd(src, col0, width, gain, rope, dout, *, name):
    s = src.shape[0]
    ts = _pick(s, (512, 256))
    cb = col0 // width
    reps = width // LANES
    g = jnp.tile(gain, (1, width // HEAD_DIM))

    douts = list(dout) if isinstance(dout, (list, tuple)) else [dout]
    piece = width // len(douts)

    def kern(*refs):
        refs = list(refs)
        dg_ref = refs.pop()
        dx_ref = refs.pop()
        do_refs = [refs.pop() for _ in douts][::-1]
        if rope is None:
            x_ref, g_ref = refs
        else:
            x_ref, g_ref, c_ref, s_ref = refs
        x = x_ref[...].astype(F32)
        bd = _block_diag(width)
        r = lax.rsqrt(_head_mean(x * x, bd) + NORM_EPS)
        xh = x * r
        dy = jnp.concatenate([d[...].astype(F32) for d in do_refs], axis=1) if len(do_refs) > 1 else do_refs[0][...].astype(F32)
        if rope is not None:
            dy = dy * jnp.tile(c_ref[...], (1, reps)) + _partner(dy * jnp.tile(s_ref[...], (1, reps)))
        dxh = dy * g_ref[...]
        dx_ref[...] = (r * (dxh - xh * _head_mean(dxh * xh, bd))).astype(dx_ref.dtype)

        @pl.when(pl.program_id(0) == 0)
        def _():
            dg_ref[...] = jnp.zeros_like(dg_ref)

        dg_ref[...] += jnp.sum(dy * xh, axis=0, keepdims=True)

    xs = pl.BlockSpec((ts, width), lambda i: (i, cb))
    row = pl.BlockSpec((ts, width), lambda i: (i, 0))
    vec = pl.BlockSpec((1, width), lambda i: (0, 0))
    tab = pl.BlockSpec((ts, LANES), lambda i: (i, 0))
    ins = [src, g] + ([] if rope is None else list(rope)) + douts
    specs = [xs, vec] + ([] if rope is None else [tab, tab]) + [pl.BlockSpec((ts, piece), lambda i: (i, 0))] * len(douts)
    dx, dg = _pcall(kern, name=name, grid=(s // ts,), in_specs=specs, out_specs=[row, vec],
                    out_shape=[jax.ShapeDtypeStruct((s, width), BF16), jax.ShapeDtypeStruct((1, width), F32)],
                    compiler_params=_params("arbitrary"))(*ins)
    return dx, jnp.sum(dg.reshape(width // HEAD_DIM, HEAD_DIM), axis=0, keepdims=True)


def _tri(strict):
    r = lax.broadcasted_iota(jnp.int32, (2 * QB, QB), 0) % QB
    c = lax.broadcasted_iota(jnp.int32, (2 * QB, QB), 1)
    return jnp.where((r > c) if strict else (r >= c), 1.0, 0.0).astype(BF16)


def _split_dot(v, t2):
    hi = v.astype(BF16)
    lo = (v - hi.astype(F32)).astype(BF16)
    return lax.dot_general(jnp.concatenate([hi, lo], axis=1), t2, (((1,), (0,)), ((), ())), preferred_element_type=F32)


LOG2E = 1.4426950408889634


def _log2_sigmoids(z2):
    lf = -(jnp.maximum(z2, 0.0) + jnp.log2(1.0 + jnp.exp2(-jnp.abs(z2))))
    return z2 + lf, lf


SB2_SUB = 2


def _first_half(shape):
    return lax.broadcasted_iota(jnp.int32, shape, 1) < HEAD_DIM


def _split_pair(t, first):
    zero = jnp.zeros_like(t)
    return [jnp.where(first, t, zero), jnp.where(first, zero, t)]


def _sb2_fwd(p, *, name, side=None):
    s = p.shape[0]
    rq = SB2_SUB * QB
    nq = s // rq
    npair = SB_W // LANES

    def kern(q_ref, k_ref, v_ref, o_ref):
        i = pl.program_id(1)
        first = _first_half((rq, LANES))
        q2 = jnp.concatenate(_split_pair(q_ref[...], first), axis=0)
        t2 = _tri(True)
        rel = lax.broadcasted_iota(jnp.int32, (2 * rq, QB), 1) - lax.broadcasted_iota(jnp.int32, (2 * rq, QB), 0) % rq

        def tile(j, q, rel, carry, acc, masked):
            off = pl.multiple_of(j * QB, QB)
            ls, lf = _log2_sigmoids(_nt(q, k_ref[pl.ds(off, QB), :]) * (SCALE * LOG2E))
            if masked:
                before = rel < i * rq - j * QB
                lf = jnp.where(before, lf, 0.0)
            w = jnp.exp2(ls + _split_dot(lf, t2) + carry)
            if masked:
                w = jnp.where(before, w, 0.0)
            return carry + jnp.sum(lf, axis=1, keepdims=True), acc + _nn(w, v_ref[pl.ds(off, QB), :])

        carry, acc = jnp.zeros((2 * rq, 1), F32), jnp.zeros((2 * rq, LANES), F32)
        for a in range(SB2_SUB):
            carry, acc = tile(i * SB2_SUB + (SB2_SUB - 1 - a), q2, rel, carry, acc, True)

        def cond(st):
            return jnp.logical_and(st[0] >= 0, st[1] > 0)

        def body(st):
            carry, acc = tile(st[0], q2, rel, st[2], st[3], False)
            return st[0] - 1, (jnp.max(carry) > SB_DEAD).astype(jnp.int32), carry, acc

        st = lax.while_loop(cond, body, (i * SB2_SUB - 1, jnp.int32(1), carry, acc))
        o_ref[...] = jnp.where(first, st[3][:rq], st[3][rq:])

    outs = _call_2d(kern, name=name, grid=(npair, nq),
                    in_specs=[pl.BlockSpec((rq, LANES), lambda a, i: (i, a)), pl.BlockSpec((s, LANES), lambda a, i: (0, npair + a)),
                              pl.BlockSpec((s, LANES), lambda a, i: (0, 2 * npair + a))],
                    out_specs=[pl.BlockSpec((rq, LANES), lambda a, i: (i, a))], out_shape=[jax.ShapeDtypeStruct((s, SB_W), F32)],
                    ins=[p, p, p], semantics=("parallel", "arbitrary"), side=side)
    return outs[0] if side is None else (outs[0][0], outs[1])


def _sb2_bwd(p, o, do, *, name, side=None):
    s = p.shape[0]
    rq = SB2_SUB * QB
    nq = s // rq
    npair = SB_W // LANES

    def kern(q_ref, k_ref, v_ref, o_ref, do_ref, dq_ref, dk_hbm, dv_hbm, dk_acc, dv_acc, sem):
        pr = pl.program_id(0)
        i = pl.program_id(1)

        @pl.when(i == 0)
        def _():
            dk_acc[...] = jnp.zeros_like(dk_acc)
            dv_acc[...] = jnp.zeros_like(dv_acc)

        first = _first_half((rq, LANES))
        q2 = jnp.concatenate(_split_pair(q_ref[...], first), axis=0)
        do2 = jnp.concatenate(_split_pair(do_ref[...], first), axis=0)
        o2 = o_ref[...]
        dsum = jnp.sum(do2.astype(F32) * jnp.concatenate([o2, o2], axis=0), axis=1, keepdims=True)
        t_strict = _tri(True)
        t_incl = _tri(False)
        rel = lax.broadcasted_iota(jnp.int32, (2 * rq, QB), 1) - lax.broadcasted_iota(jnp.int32, (2 * rq, QB), 0) % rq

        def tile(j, rows, carry, gcarry, dq, masked):
            q, dob, dsm, rel = rows
            off = pl.multiple_of(j * QB, QB)
            kt = k_ref[pl.ds(off, QB), :]
            ls, lf = _log2_sigmoids(_nt(q, kt) * (SCALE * LOG2E))
            if masked:
                before = rel < i * rq - j * QB
                lf = jnp.where(before, lf, 0.0)
            w = jnp.exp2(ls + _split_dot(lf, t_strict) + carry)
            if masked:
                w = jnp.where(before, w, 0.0)
            wr = w.astype(MXU_DT)
            g = _nt(dob, v_ref[pl.ds(off, QB), :]) * wr.astype(F32)
            big_g = dsm - (_split_dot(g, t_incl) + gcarry)
            sig = jnp.exp2(ls)
            dz = g * (1.0 - sig) - sig * big_g
            if masked:
                dz = jnp.where(before, dz, 0.0)
            dz = dz * SCALE
            dk_acc[pl.ds(off, QB), :] += _tn(dz, q)
            dv_acc[pl.ds(off, QB), :] += _tn(wr, dob)
            return (carry + jnp.sum(lf, axis=1, keepdims=True), gcarry + jnp.sum(g, axis=1, keepdims=True),
                    dq + _nn(dz, kt))

        zc = jnp.zeros((2 * rq, 1), F32)
        carry, gcarry, dq = zc, zc, jnp.zeros((2 * rq, LANES), F32)
        whole = (q2, do2, dsum, rel)
        for a in range(SB2_SUB):
            carry, gcarry, dq = tile(i * SB2_SUB + (SB2_SUB - 1 - a), whole, carry, gcarry, dq, True)

        def cond(st):
            return jnp.logical_and(st[0] >= 0, st[1] > 0)

        def body(st):
            carry, gcarry, dq = tile(st[0], whole, st[2], st[3], st[4], False)
            return st[0] - 1, (jnp.max(carry) > SB_DEAD).astype(jnp.int32), carry, gcarry, dq

        st = lax.while_loop(cond, body, (i * SB2_SUB - 1, jnp.int32(1), carry, gcarry, dq))
        dq_ref[...] = jnp.where(first, st[4][:rq], st[4][rq:])

        @pl.when(i == nq - 1)
        def _():
            cols = pl.ds(pl.multiple_of(pr * LANES, LANES), LANES)
            ck = pltpu.make_async_copy(dk_acc, dk_hbm.at[:, cols], sem.at[0])
            cv = pltpu.make_async_copy(dv_acc, dv_hbm.at[:, cols], sem.at[1])
            ck.start()
            cv.start()
            ck.wait()
            cv.wait()

    blk = pl.BlockSpec((rq, LANES), lambda a, i: (i, a))
    anyspace = pl.BlockSpec(memory_space=pl.ANY)
    shp = jax.ShapeDtypeStruct((s, SB_W), F32)
    return _call_2d(kern, name=name, grid=(npair, nq),
                    in_specs=[blk, pl.BlockSpec((s, LANES), lambda a, i: (0, npair + a)),
                              pl.BlockSpec((s, LANES), lambda a, i: (0, 2 * npair + a)), blk, blk],
                    out_specs=[blk, anyspace, anyspace], out_shape=[shp, shp, shp], ins=[p, p, p, o, do],
                    scratch_shapes=[pltpu.VMEM((s, LANES), F32), pltpu.VMEM((s, LANES), F32), pltpu.SemaphoreType.DMA((2,))],
                    semantics=("arbitrary", "arbitrary"), side=side)


def _dsa_rel():
    qi = lax.broadcasted_iota(jnp.int32, (QB, QB), 0)
    kj = lax.broadcasted_iota(jnp.int32, (QB, QB), 1)
    return kj - qi


def _prev_mask(rel, has_prev):
    return rel >= jnp.where(has_prev, 0, QB)


DSA_BT = QB * max(r for _, r in DSA_GROUPS)
DSA_UB = 4


def _bdot(a, b, ca, cb):
    return lax.dot_general(a.astype(MXU_DT), b.astype(MXU_DT), (((ca,), (cb,)), ((0,), (0,))), preferred_element_type=F32)


def _bnt(a, b):
    return _bdot(a, b, 2, 2)


def _bnn(a, b):
    return _bdot(a, b, 2, 1)


def _btn(a, b):
    return _bdot(a, b, 1, 1)


def _unit_rows(r, c, b):
    return pl.ds(c + QB * r * b, QB, stride=r)


def _pair_cols(t, first):
    return [jnp.max(jnp.where(first, t, -jnp.inf), axis=1, keepdims=True),
            jnp.max(jnp.where(first, -jnp.inf, t), axis=1, keepdims=True)]


def _dsa2_fwd(qn, kn, v32, g, *, name):
    s = qn.shape[0]
    r = DSA_GROUPS[g][1]
    nbk = DSA_BT // (QB * r)
    npair = DSA_OUT_W // LANES

    def kern(q_ref, k_ref, kp_ref, v_ref, vp_ref, o_ref, l_ref):
        t = pl.program_id(1)
        first = _first_half((QB, LANES))
        rel = _dsa_rel()
        units = [(c, b) for c in range(r) for b in range(nbk)]
        for u0 in range(0, len(units), DSA_UB):
            batch = units[u0:u0 + DSA_UB]
            qs, kcs, vcs, kps, vps, masks = [], [], [], [], [], []
            for c, b in batch:
                rows = _unit_rows(r, c, b)
                kc, vc = k_ref[rows, :].astype(MXU_DT), v_ref[rows, :].astype(MXU_DT)
                if b > 0:
                    prow = _unit_rows(r, c, b - 1)
                    kpv, vpv, has_prev = k_ref[prow, :], v_ref[prow, :], True
                else:
                    prow = _unit_rows(r, c, nbk - 1)
                    kpv, vpv, has_prev = kp_ref[prow, :], vp_ref[prow, :], t > 0
                for qe in _split_pair(q_ref[rows, :], first):
                    qs.append(qe.astype(MXU_DT))
                    kcs.append(kc)
                    vcs.append(vc)
                    kps.append(kpv.astype(MXU_DT))
                    vps.append(vpv.astype(MXU_DT))
                    masks.append(_prev_mask(rel, has_prev))
            qq = jnp.stack(qs)
            sc = jnp.where(rel <= 0, _bnt(qq, jnp.stack(kcs)) * SCALE, -jnp.inf)
            sp = _bnt(qq, jnp.stack(kps)) * SCALE
            sp = jnp.stack([jnp.where(mk, sp[n], -jnp.inf) for n, mk in enumerate(masks)])
            m = jnp.maximum(jnp.max(sc, axis=2, keepdims=True), jnp.max(sp, axis=2, keepdims=True))
            pc = jnp.exp(sc - m)
            pp = jnp.exp(sp - m)
            den = jnp.sum(pc, axis=2, keepdims=True) + jnp.sum(pp, axis=2, keepdims=True)
            out = (_bnn(pc, jnp.stack(vcs)) + _bnn(pp, jnp.stack(vps))) / den
            lse = m + jnp.log(den)
            for idx, (c, b) in enumerate(batch):
                rows = _unit_rows(r, c, b)
                o_ref[rows, :] = jnp.where(first, out[2 * idx], out[2 * idx + 1])
                l_ref[rows, :] = jnp.where(first, lse[2 * idx], lse[2 * idx + 1])

    npg = DSA_HPG * HEAD_DIM // LANES
    cur = pl.BlockSpec((DSA_BT, LANES), lambda a, t: (t, npg * g + a))
    prev = pl.BlockSpec((DSA_BT, LANES), lambda a, t: (jnp.maximum(t - 1, 0), npg * g + a))
    out = pl.BlockSpec((DSA_BT, LANES), lambda a, t: (t, a))
    shp = jax.ShapeDtypeStruct((s, DSA_OUT_W), F32)
    return _pcall(kern, name=name, grid=(npair, s // DSA_BT), in_specs=[cur, cur, prev, cur, prev], out_specs=[out, out],
                  out_shape=[shp, shp], compiler_params=_params("parallel", "parallel"))(qn, kn, kn, v32, v32)


def _dsa2_combine(parts, *, name):
    s, wd = parts[0][0].shape
    ts = _pick(s, (512, 256))

    def kern(o0, l0, o1, l1, o2, l2, o_ref, l_ref):
        ls = [l0[...], l1[...], l2[...]]
        m = jnp.maximum(jnp.maximum(ls[0], ls[1]), ls[2])
        es = [jnp.exp(l - m) for l in ls]
        den = es[0] + es[1] + es[2]
        o_ref[...] = (es[0] * o0[...] + es[1] * o1[...] + es[2] * o2[...]) / den
        l_ref[...] = m + jnp.log(den)

    blk = pl.BlockSpec((ts, wd), lambda i: (i, 0))
    shp = jax.ShapeDtypeStruct((s, wd), F32)
    flat = [t for pair in parts for t in pair]
    return _pcall(kern, name=name, grid=(s // ts,), in_specs=[blk] * 6, out_specs=[blk, blk], out_shape=[shp, shp],
                  compiler_params=_params("parallel"))(*flat)


def _dsa2_prep(o, do, *, name):
    s, wd = o.shape
    ts = _pick(s, (512, 256))

    def kern(o_ref, do_ref, d_ref):
        d_ref[...] = _head_mean(do_ref[...] * o_ref[...], _block_diag(wd)) * HEAD_DIM

    blk = pl.BlockSpec((ts, wd), lambda i: (i, 0))
    return _pcall(kern, name=name, grid=(s // ts,), in_specs=[blk, blk], out_specs=blk,
                  out_shape=jax.ShapeDtypeStruct((s, wd), F32), compiler_params=_params("parallel"))(o, do)


def _dsa2_bwd(qn, kn, v32, do, lse, dd, g, *, name):
    s = qn.shape[0]
    r = DSA_GROUPS[g][1]
    nbk = DSA_BT // (QB * r)
    npair = DSA_OUT_W // LANES
    nsteps = s // DSA_BT

    def kern(q_ref, qn_ref, k_ref, kp_ref, v_ref, vp_ref, do_ref, don_ref, l_ref, ln_ref, d_ref, dn_ref,
             dq_ref, dk_ref, dv_ref):
        t = pl.program_id(1)
        first = _first_half((QB, LANES))
        rel = _dsa_rel()

        def pairs(items):
            qq = jnp.stack([it[0].astype(MXU_DT) for it in items])
            dd = jnp.stack([it[1].astype(MXU_DT) for it in items])
            kk = jnp.stack([it[4].astype(MXU_DT) for it in items])
            vv = jnp.stack([it[5].astype(MXU_DT) for it in items])
            p = jnp.exp(_bnt(qq, kk) * SCALE - jnp.stack([it[2] for it in items]))
            p = jnp.stack([jnp.where(it[6], p[n], 0.0) for n, it in enumerate(items)])
            ds = p * (_bnt(dd, vv) - jnp.stack([it[3] for it in items])) * SCALE
            return _bnn(ds, kk), _btn(ds, qq), _btn(p, dd)

        def heads(rows, qr, dor, lr, dr):
            return list(zip(_split_pair(qr[rows, :], first), _split_pair(dor[rows, :], first),
                            _pair_cols(lr[rows, :], first), _pair_cols(dr[rows, :], first)))

        units = [(c, b) for c in range(r) for b in range(nbk)]
        dk_of, dv_of = [None] * len(units), [None] * len(units)
        for u0 in range(0, len(units), DSA_UB // 2):
            batch = list(enumerate(units))[u0:u0 + DSA_UB // 2]
            items = []
            for u, (c, b) in batch:
                rows = _unit_rows(r, c, b)
                kc, vc = k_ref[rows, :], v_ref[rows, :]
                if b > 0:
                    prow = _unit_rows(r, c, b - 1)
                    kpv, vpv, pmask = k_ref[prow, :], v_ref[prow, :], _prev_mask(rel, True)
                else:
                    prow = _unit_rows(r, c, nbk - 1)
                    kpv, vpv, pmask = kp_ref[prow, :], vp_ref[prow, :], _prev_mask(rel, t > 0)
                for hd in heads(rows, q_ref, do_ref, l_ref, d_ref):
                    items.append(hd + (kc, vc, rel <= 0))
                    items.append(hd + (kpv, vpv, pmask))
            dq, dk, dv = pairs(items)
            for n, (u, (c, b)) in enumerate(batch):
                dq_ref[_unit_rows(r, c, b), :] = jnp.where(first, dq[4 * n] + dq[4 * n + 1], dq[4 * n + 2] + dq[4 * n + 3])
                dk_of[u] = dk[4 * n] + dk[4 * n + 2]
                dv_of[u] = dv[4 * n] + dv[4 * n + 2]
                if b > 0:
                    dk_of[u - 1] = dk_of[u - 1] + (dk[4 * n + 1] + dk[4 * n + 3])
                    dv_of[u - 1] = dv_of[u - 1] + (dv[4 * n + 1] + dv[4 * n + 3])
        lasts = [c * nbk + nbk - 1 for c in range(r)]
        for c0 in range(0, r, DSA_UB):
            chunk = list(range(c0, min(c0 + DSA_UB, r)))
            items = []
            for c in chunk:
                last = _unit_rows(r, c, nbk - 1)
                for hd in heads(_unit_rows(r, c, 0), qn_ref, don_ref, ln_ref, dn_ref):
                    items.append(hd + (k_ref[last, :], v_ref[last, :], _prev_mask(rel, t < nsteps - 1)))
            _, dk, dv = pairs(items)
            for n, c in enumerate(chunk):
                dk_of[lasts[c]] = dk_of[lasts[c]] + (dk[2 * n] + dk[2 * n + 1])
                dv_of[lasts[c]] = dv_of[lasts[c]] + (dv[2 * n] + dv[2 * n + 1])
        for u, (c, b) in enumerate(units):
            dk_ref[_unit_rows(r, c, b), :] = dk_of[u]
            dv_ref[_unit_rows(r, c, b), :] = dv_of[u]

    npg = DSA_HPG * HEAD_DIM // LANES

    def at(shift, col):
        return pl.BlockSpec((DSA_BT, LANES), lambda a, t: (jnp.clip(t + shift, 0, nsteps - 1), col(a)))

    gcol = lambda a: npg * g + a
    ocol = lambda a: a
    specs = [at(0, gcol), at(1, gcol), at(0, gcol), at(-1, gcol), at(0, gcol), at(-1, gcol),
             at(0, ocol), at(1, ocol), at(0, ocol), at(1, ocol), at(0, ocol), at(1, ocol)]
    shp = jax.ShapeDtypeStruct((s, DSA_OUT_W), F32)
    return _pcall(kern, name=name, grid=(npair, nsteps), in_specs=specs, out_specs=[at(0, ocol)] * 3, out_shape=[shp, shp, shp],
                  compiler_params=_params("parallel", "parallel"))(qn, qn, kn, kn, v32, v32, do, do, lse, lse, dd, dd)


def _mem2_fwd(qn, km, kv, *, name):
    s = qn.shape[0]
    ml = km.shape[0]
    tq = _pick(s, (512, 256))
    npair = MEM_W // LANES

    def kern(q_ref, k_ref, v_ref, o_ref):
        first = _first_half((tq, LANES))
        q2 = jnp.concatenate(_split_pair(q_ref[...], first), axis=0)
        sc = _nt(q2, k_ref[...]) * SCALE
        e = jnp.exp(sc - jnp.max(sc, axis=1, keepdims=True))
        o2 = _nn(e / jnp.sum(e, axis=1, keepdims=True), v_ref[...])
        o_ref[...] = jnp.where(first, o2[:tq], o2[tq:])

    blk = pl.BlockSpec((tq, LANES), lambda a, i: (i, a))
    return _pcall(kern, name=name, grid=(npair, s // tq),
                  in_specs=[blk, pl.BlockSpec((ml, LANES), lambda a, i: (0, a)), pl.BlockSpec((ml, LANES), lambda a, i: (0, npair + a))],
                  out_specs=blk, out_shape=jax.ShapeDtypeStruct((s, MEM_W), F32),
                  compiler_params=_params("parallel", "parallel"))(qn, km, kv)


def _mem2_bwd(qn, km, kv, do, *, name):
    s = qn.shape[0]
    ml = km.shape[0]
    tq = _pick(s, (512, 256))
    npair = MEM_W // LANES

    def kern(q_ref, k_ref, v_ref, do_ref, dq_ref, dk_ref, dv_ref):
        @pl.when(pl.program_id(1) == 0)
        def _():
            dk_ref[...] = jnp.zeros_like(dk_ref)
            dv_ref[...] = jnp.zeros_like(dv_ref)

        first = _first_half((tq, LANES))
        q2 = jnp.concatenate(_split_pair(q_ref[...], first), axis=0)
        do2 = jnp.concatenate(_split_pair(do_ref[...], first), axis=0)
        sc = _nt(q2, k_ref[...]) * SCALE
        e = jnp.exp(sc - jnp.max(sc, axis=1, keepdims=True))
        p = e / jnp.sum(e, axis=1, keepdims=True)
        dp = _nt(do2, v_ref[...])
        ds = p * (dp - jnp.sum(p * dp, axis=1, keepdims=True)) * SCALE
        dq2 = _nn(ds, k_ref[...])
        dk_ref[...] += _tn(ds, q2)
        dv_ref[...] += _tn(p, do2)
        dq_ref[...] = jnp.where(first, dq2[:tq], dq2[tq:])

    blk = pl.BlockSpec((tq, LANES), lambda a, i: (i, a))
    kblk = pl.BlockSpec((ml, LANES), lambda a, i: (0, a))
    kshape = jax.ShapeDtypeStruct((ml, MEM_W), F32)
    return _pcall(kern, name=name, grid=(npair, s // tq),
                  in_specs=[blk, kblk, pl.BlockSpec((ml, LANES), lambda a, i: (0, npair + a)), blk],
                  out_specs=[blk, kblk, kblk], out_shape=[jax.ShapeDtypeStruct((s, MEM_W), F32), kshape, kshape],
                  compiler_params=_params("parallel", "arbitrary"))(qn, km, kv, do)


def _merge_fwd(logits, bias, ya, yb, yc, *, name):
    s, d = ya.shape
    ts = _pick(s, (512, 256))

    def kern(l0, l1, l2, b0, b1, b2, a_ref, b_ref, c_ref, o_ref):
        m = 0.0
        for l_ref, bb_ref, y_ref in ((l0, b0, a_ref), (l1, b1, b_ref), (l2, b2, c_ref)):
            m = m + _sigmoid(l_ref[...].astype(F32) + bb_ref[...]) * y_ref[...].astype(F32)
        o_ref[...] = m.astype(o_ref.dtype)

    row = pl.BlockSpec((ts, d), lambda i: (i, 0))
    lg = [pl.BlockSpec((ts, d), functools.partial(lambda i, c: (i, c), c=c)) for c in range(3)]
    bs = [pl.BlockSpec((1, d), functools.partial(lambda i, c: (0, c), c=c)) for c in range(3)]
    return _pcall(kern, name=name, grid=(s // ts,), in_specs=lg + bs + [row, row, row], out_specs=row,
                  out_shape=jax.ShapeDtypeStruct((s, d), BF16),
                  compiler_params=_params("parallel"))(logits, logits, logits, bias, bias, bias, ya, yb, yc)


def _merge_bwd(logits, bias, ya, yb, yc, dm, *, name):
    s, d = ya.shape
    ts = _pick(s, (256,))

    def kern(l0, l1, l2, b0, b1, b2, a_ref, b_ref, c_ref, dm_ref, da_ref, db_ref, dc_ref, dl0, dl1, dl2, dbias0, dbias1, dbias2):
        first = pl.program_id(0) == 0
        dmv = dm_ref[...]
        for l_ref, bb_ref, y_ref, dy_ref, dl_ref, dbias_ref in ((l0, b0, a_ref, da_ref, dl0, dbias0), (l1, b1, b_ref, db_ref, dl1, dbias1),
                                                                (l2, b2, c_ref, dc_ref, dl2, dbias2)):
            g = _sigmoid(l_ref[...].astype(F32) + bb_ref[...])
            dy_ref[...] = (dmv * g).astype(dy_ref.dtype)
            dl = dmv * y_ref[...].astype(F32) * g * (1.0 - g)
            dl_ref[...] = dl.astype(dl_ref.dtype)

            @pl.when(first)
            def _():
                dbias_ref[...] = jnp.zeros_like(dbias_ref)

            dbias_ref[...] += jnp.sum(dl, axis=0, keepdims=True)

    row = pl.BlockSpec((ts, d), lambda i: (i, 0))
    lg = [pl.BlockSpec((ts, d), functools.partial(lambda i, c: (i, c), c=c)) for c in range(3)]
    bs = [pl.BlockSpec((1, d), functools.partial(lambda i, c: (0, c), c=c)) for c in range(3)]
    vec = pl.BlockSpec((1, d), lambda i: (0, 0))
    yshape = jax.ShapeDtypeStruct((s, d), BF16)
    vshape = jax.ShapeDtypeStruct((1, d), F32)
    outs = _pcall(kern, name=name, grid=(s // ts,), in_specs=lg + bs + [row, row, row, row],
                  out_specs=[row, row, row, row, row, row, vec, vec, vec],
                  out_shape=[yshape] * 6 + [vshape] * 3,
                  compiler_params=_params("arbitrary"))(logits, logits, logits, bias, bias, bias, ya, yb, yc, dm)
    return outs[0], outs[1], outs[2], outs[3:6], jnp.concatenate(outs[6:9], axis=1)


G_FFN1 = ['ffn1_w1', 'ffn1_w3', 'ffn1_w2']
G_FFN2 = ['ffn2_w1', 'ffn2_w3', 'ffn2_w2']
G_MID = [n for n in BIG if n not in G_FFN1 + G_FFN2]


def _ffn_fwd(h, w1, w3, w2, tag, epilogue, side=None):
    carried = None
    if side is None:
        a, b, f = _ffn_up(h, w1, w3, name=f"{tag}_up")
    else:
        (a, b, f), carried = _ffn_up(h, w1, w3, name=f"{tag}_up", side=side)
    outs = _matmul(f, w2, name=f"{tag}_down", alpha=0.5, tm=512, tn=1024, tk=2816, epilogue=epilogue)
    return outs, (h, a, b, f), carried


def _ffn_bwd(x, norm, w1, w3, w2, saved, dy, dyb, tag, side=None, own_side=None):
    h, a, b, f = saved
    dw2 = _matmul(f, dyb, name=f"{tag}_dw2", ta=True, alpha=0.5, tm=1408, tn=1024, tk=2048)
    carried = None
    if side is None:
        da, db = _ffn_dact(dyb, w2, a, b, name=f"{tag}_dact")
    else:
        (da, db), carried = _ffn_dact(dyb, w2, a, b, name=f"{tag}_dact", side=side)
    dw1 = _matmul(h, da, name=f"{tag}_dw1", ta=True, tm=1024, tn=1408, tk=2048)
    dw3 = _matmul(h, db, name=f"{tag}_dw3", ta=True, tm=1024, tn=1408, tk=2048)
    outs = _matmul(da, w1, name=f"{tag}_dh", tb=True, tm=512, tn=1024, tk=1408, pair2=(db, w3),
                   epilogue=(_epi_rms_bwd, [x, dy], [norm], [F32, BF16], 1),
                   side=None if own_side is None else own_side(dw1, dw3, dw2))
    (dx, dxb, dnorm), own = outs if own_side is not None else (outs, None)
    return dx, dxb, dnorm, dw1, dw3, dw2, carried, own


def _local_step(x, mem, loss_target, wl, ws):
    s, d = x.shape
    assert s % (QB * 16) == 0
    rope = _rope_tables(s)
    bf = {n: wl[n].astype(BF16) for n in BIG}
    w = dict(ws)

    h1, early = _rms_fwd(x, w['ffn1_norm'], name="ffn1_rms", side=_side(_pack_rows(bf, G_FFN1), _two_level_phases()))
    w.update(_unpack_gathered(early, wl, G_FFN1))
    (x1, h), sv1, late = _ffn_fwd(h1, w['ffn1_w1'], w['ffn1_w3'], w['ffn1_w2'], "ffn1",
                                  (_epi_residual_rms, [x], [w['mix_norm']], [F32, BF16], 0),
                                  side=_side(_pack_rows(bf, G_MID), _two_level_phases()))
    w.update(_unpack_gathered(late, wl, G_MID))
    p = _matmul(h, w['w_in'], name="in_proj", out_dtype=BF16, tn=1024)
    logits = _matmul(h, w['w_gate'], name="gate_proj", out_dtype=BF16, tn=1024)
    c_qb, c_kb, c_vb, c_qc = 3 * SB_W, 3 * SB_W + DSA_W, 3 * SB_W + 2 * DSA_W, 3 * SB_W + 3 * DSA_W

    oa_t, late = _sb2_fwd(p, name="sb_fwd", side=_side(_pack_rows(bf, G_FFN2), _two_level_phases()))
    w.update(_unpack_gathered(late, wl, G_FFN2))
    ya = _matmul(oa_t, w['w_branch_sb'], name="sb_out", out_dtype=BF16)

    qb_n = _qknorm_fwd(p, c_qb, DSA_W, w['qn_dsa'], rope, name="dsa_qnorm", out_dtype=F32)
    kb_n = _qknorm_fwd(p, c_kb, DSA_W, w['kn_dsa'], rope, name="dsa_knorm", out_dtype=F32)
    vb32 = p[:, c_vb:c_vb + DSA_W].astype(F32)
    groups = range(len(DSA_GROUPS))
    ob_t, lse_b = _dsa2_combine([_dsa2_fwd(qb_n, kb_n, vb32, gi, name=f"dsa_fwd{gi}") for gi in groups], name="dsa_combine")
    yb = _matmul(ob_t, w['w_branch_dsa'], name="dsa_out", out_dtype=BF16)

    memh = _rms_fwd(mem, w['mem_norm'], name="mem_rms")
    kv = _matmul(memh, w['w_mem_kv'], name="mem_kv", out_dtype=BF16)
    km_n = _qknorm_fwd(kv, 0, MEM_W, w['kn_mem'], None, name="mem_knorm")
    qc_n = _qknorm_fwd(p, c_qc, MEM_W, w['qn_mem'], None, name="mem_qnorm")
    oc_t = _mem2_fwd(qc_n, km_n, kv, name="mem_fwd")
    yc = _matmul(oc_t, w['w_branch_mem'], name="mem_out", out_dtype=BF16)

    merged = _merge_fwd(logits, w['b_gate'], ya, yb, yc, name="merge")
    x2, h2 = _matmul(merged, w['w_out'], name="out_proj", tn=1024,
                     epilogue=(_epi_residual_rms, [x1], [w['ffn2_norm']], [F32, BF16], 0))
    (dx3, dx3b, sq), sv2, _ = _ffn_fwd(h2, w['ffn2_w1'], w['ffn2_w3'], w['ffn2_w2'], "ffn2",
                                       (_epi_loss, [x2, loss_target], [], [F32, BF16], 1))
    loss = jnp.sum(sq) * (0.5 / d)

    g, recv = {}, {}

    def owners(names):
        return _pack_for_owners(g, wl, names).astype(BF16)

    dx2, dx2b, g['ffn2_norm'], g['ffn2_w1'], g['ffn2_w3'], g['ffn2_w2'], _, _ = _ffn_bwd(
        x2, w['ffn2_norm'], w['ffn2_w1'], w['ffn2_w3'], w['ffn2_w2'], sv2, dx3, dx3b, "ffn2")

    g['w_out'] = _matmul(merged, dx2b, name="d_w_out", ta=True, tn=1024, tk=512)
    dm = _matmul(dx2b, w['w_out'], name="d_merged", tb=True, tn=1024)
    dya, dyb, dyc, dlog, g['b_gate'] = _merge_bwd(logits, w['b_gate'], ya, yb, yc, dm, name="d_merge")
    dlogits = jnp.concatenate(dlog, axis=1)

    g['w_branch_sb'] = _matmul(oa_t, dya, name="d_w_sb", ta=True, tn=1024, tk=512)
    g['w_branch_dsa'] = _matmul(ob_t, dyb, name="d_w_dsa", ta=True, tk=512)
    g['w_branch_mem'] = _matmul(oc_t, dyc, name="d_w_mem", ta=True, tk=512)
    doa = _matmul(dya, w['w_branch_sb'], name="d_oa", tb=True, out_dtype=BF16)
    dob = _matmul(dyb, w['w_branch_dsa'], name="d_ob", tb=True)
    doc = _matmul(dyc, w['w_branch_mem'], name="d_oc", tb=True, out_dtype=BF16)

    (dqa, dka, dva), recv['ffn2'] = _sb2_bwd(p, oa_t, doa, name="sb_bwd", side=_side(owners(G_FFN2), _direct_phases(True)))

    dd_b = _dsa2_prep(ob_t, dob, name="dsa_prep")
    dgrp = [_dsa2_bwd(qb_n, kb_n, vb32, dob, lse_b, dd_b, gi, name=f"dsa_bwd{gi}") for gi in groups]
    dvb = jnp.concatenate([t[2] for t in dgrp], axis=1).astype(BF16)
    dqb, g['qn_dsa'] = _qknorm_bwd(p, c_qb, DSA_W, w['qn_dsa'], rope, [t[0] for t in dgrp], name="d_dsa_qnorm")
    dkb, g['kn_dsa'] = _qknorm_bwd(p, c_kb, DSA_W, w['kn_dsa'], rope, [t[1] for t in dgrp], name="d_dsa_knorm")

    dqc_n, dkm_n, dvm = _mem2_bwd(qc_n, km_n, kv, doc, name="mem_bwd")
    dqc, g['qn_mem'] = _qknorm_bwd(p, c_qc, MEM_W, w['qn_mem'], None, dqc_n, name="d_mem_qnorm")
    dkm, g['kn_mem'] = _qknorm_bwd(kv, 0, MEM_W, w['kn_mem'], None, dkm_n, name="d_mem_knorm")
    dkv = jnp.concatenate([dkm, dvm.astype(BF16)], axis=1)
    g['w_mem_kv'] = _matmul(memh, dkv, name="d_w_mem_kv", ta=True)
    dmemh = _matmul(dkv, w['w_mem_kv'], name="d_memh", tb=True)
    _, _, g['mem_norm'] = _rms_bwd(mem, w['mem_norm'], dmemh, None, name="d_mem_rms")

    dp = jnp.concatenate([dqa.astype(BF16), dka.astype(BF16), dva.astype(BF16),
                          dqb, dkb, dvb, dqc], axis=1)
    g['w_in'] = _matmul(h, dp, name="d_w_in", ta=True, tn=2048, tk=1024)
    g['w_gate'] = _matmul(h, dlogits, name="d_w_gate", ta=True, tn=1536, tk=1024)
    dh = _matmul(dp, w['w_in'], name="d_h_in", tb=True, tn=1024, tk=2048)
    dx1, dx1b, g['mix_norm'] = _matmul(dlogits, w['w_gate'], name="d_h_gate", tb=True, tm=512, tn=1024, tk=3072,
                                       epilogue=(_epi_rms_bwd_sum, [dh, x1, dx2], [w['mix_norm']], [F32, BF16], 1))

    def own_side(dw1, dw3, dw2):
        g.update(ffn1_w1=dw1, ffn1_w3=dw3, ffn1_w2=dw2)
        return _side(owners(G_FFN1), _direct_phases(True))

    dx0, _, g['ffn1_norm'], _, _, _, recv['mid'], recv['ffn1'] = _ffn_bwd(
        x, w['ffn1_norm'], w['ffn1_w1'], w['ffn1_w3'], w['ffn1_w2'], sv1, dx1, dx1b, "ffn1",
        side=_side(owners(G_MID), _direct_phases(True)), own_side=own_side)
    return loss, dx0, recv, {n: g[n] for n in SMALL}


def _pack_rows(d, names):
    return jnp.concatenate([d[n].reshape(-1, LANES) for n in names], axis=0)


def _unpack_rows(t, like, names):
    out, off = {}, 0
    for n in names:
        r = like[n].size // LANES
        out[n] = t[off:off + r].reshape(like[n].shape)
        off += r
    return out


def _unpack_gathered(t, local, names):
    out, off = {}, 0
    for n in names:
        r, c = local[n].shape
        rows = r * c // LANES
        blk = t[:, off:off + rows].reshape(N_DEV, r, c)
        out[n] = blk.reshape(N_DEV * r, c) if SHARD_AXIS[n] == 0 else blk.transpose(1, 0, 2).reshape(r, N_DEV * c)
        off += rows
    return out


def _pack_for_owners(g, local, names):
    parts = []
    for n in names:
        r, c = local[n].shape
        blk = g[n].reshape(N_DEV, r, c) if SHARD_AXIS[n] == 0 else g[n].reshape(r, N_DEV, c).transpose(1, 0, 2)
        parts.append(blk.reshape(N_DEV, r * c // LANES, LANES))
    return jnp.concatenate(parts, axis=1)


def _pack_small(d, names, extra_rows):
    parts = []
    for n in names:
        v = d[n].reshape(-1)
        pad = (-v.size) % LANES
        parts.append(jnp.concatenate([v, jnp.zeros((pad,), v.dtype)]).reshape(-1, LANES))
    t = jnp.concatenate(parts, axis=0)
    return jnp.concatenate([t, jnp.zeros((extra_rows, LANES), t.dtype)], axis=0)


def _unpack_small(t, like, names):
    out, off = {}, 0
    for n in names:
        size = like[n].size
        rows = -(-size // LANES)
        out[n] = t[off:off + rows].reshape(-1)[:size].reshape(like[n].shape)
        off += rows
    return out


def _direct_phases(per_peer):
    def descriptors(src_ref, out_ref, send_sems, recv_sems, local_sem):
        x, y, c = lax.axis_index("x"), lax.axis_index("y"), lax.axis_index("c")
        me = 4 * x + 2 * y + c
        mine = pltpu.make_async_copy(src_ref.at[me] if per_peer else src_ref, out_ref.at[me], local_sem)
        copies = []
        for k in range(1, N_DEV):
            px = 1 - x if k & 4 else x
            py = 1 - y if k & 2 else y
            pc = 1 - c if k & 1 else c
            copies.append(pltpu.make_async_remote_copy(
                src_ref=src_ref.at[4 * px + 2 * py + pc] if per_peer else src_ref, dst_ref=out_ref.at[me],
                send_sem=send_sems.at[k - 1], recv_sem=recv_sems.at[k - 1],
                device_id=(px, py, pc), device_id_type=pl.DeviceIdType.MESH))
        return mine, copies

    def start(*refs):
        mine, copies = descriptors(*refs)
        mine.start()
        for cp in copies:
            cp.start()

    def forward(*refs):
        pass

    def finish(*refs):
        mine, copies = descriptors(*refs)
        for cp in copies:
            cp.wait_recv()
        for cp in copies:
            cp.wait_send()
        mine.wait()

    return start, forward, finish


EXCHANGE_SEMS = [pltpu.SemaphoreType.DMA((N_DEV - 1,)), pltpu.SemaphoreType.DMA((N_DEV - 1,)), pltpu.SemaphoreType.DMA]


def _exchange(src, phases, *, name):
    rows = src.shape[-2]

    def body(*refs):
        for phase in phases:
            phase(*refs)

    anyspace = pl.BlockSpec(memory_space=pl.ANY)
    return _pcall(body, name=name, in_specs=[anyspace], out_specs=anyspace,
                  out_shape=jax.ShapeDtypeStruct((N_DEV, rows, LANES), src.dtype), scratch_shapes=list(EXCHANGE_SEMS))(src)


def _side(src, phases):
    start, forward, finish = phases

    def before(first, mid, ins, outs, scratch):
        pl.when(first)(lambda: start(ins[0], outs[0], *scratch))
        pl.when(mid)(lambda: forward(ins[0], outs[0], *scratch))

    def after(last, ins, outs, scratch):
        pl.when(last)(lambda: finish(ins[0], outs[0], *scratch))

    return [src], [jax.ShapeDtypeStruct((N_DEV, src.shape[-2], LANES), src.dtype)], list(EXCHANGE_SEMS), before, after


def _call_2d(kern, *, name, grid, in_specs, out_specs, out_shape, ins, scratch_shapes=(), semantics, side=None):
    if side is None:
        return _pcall(kern, name=name, grid=grid, in_specs=in_specs, out_specs=out_specs, out_shape=out_shape,
                      scratch_shapes=list(scratch_shapes), compiler_params=_params(*semantics))(*ins)
    s_ins, s_shapes, s_scratch, before, after = side
    n_in, n_out, n_scr = len(ins), len(out_shape), len(scratch_shapes)

    def combined(*refs):
        refs = list(refs)
        cut = [n_in, len(s_ins), n_out, len(s_shapes), n_scr, len(s_scratch)]
        parts, pos = [], 0
        for c in cut:
            parts.append(refs[pos:pos + c])
            pos += c
        m_in, c_in, m_out, c_out, m_scr, c_scr = parts
        ids = [pl.program_id(a) for a in range(len(grid))]
        inner_zero = functools.reduce(jnp.logical_and, [i == 0 for i in ids[1:]], True)
        first = jnp.logical_and(ids[0] == 0, inner_zero)
        mid = jnp.logical_and(ids[0] == grid[0] // 2, inner_zero)
        last = functools.reduce(jnp.logical_and, [i == n - 1 for i, n in zip(ids, grid)])
        before(first, mid, c_in, c_out, c_scr)
        kern(*m_in, *m_out, *m_scr)
        after(last, c_in, c_out, c_scr)

    anyspace = pl.BlockSpec(memory_space=pl.ANY)
    outs = _pcall(combined, name=name, grid=grid, in_specs=list(in_specs) + [anyspace] * len(s_ins),
                  out_specs=list(out_specs) + [anyspace] * len(s_shapes), out_shape=list(out_shape) + s_shapes,
                  scratch_shapes=list(scratch_shapes) + s_scratch, compiler_params=_params(*["arbitrary"] * len(grid)))(*ins, *s_ins)
    return outs[:n_out], outs[n_out]


def _two_level_phases():
    def parts(src_ref, out_ref, send_sems, recv_sems, local_sem):
        x, y, c = lax.axis_index("x"), lax.axis_index("y"), lax.axis_index("c")
        me, sibling = (x, y, c), (x, y, 1 - c)
        chips = [(1 - x, y), (x, 1 - y), (1 - x, 1 - y)]

        def slab(px, py, pc):
            return out_ref.at[4 * px + 2 * py + pc]

        def copy(k, block, to, from_src=False):
            return pltpu.make_async_remote_copy(
                src_ref=src_ref if from_src else slab(*block), dst_ref=slab(*block),
                send_sem=send_sems.at[k], recv_sem=recv_sems.at[k], device_id=to, device_id_type=pl.DeviceIdType.MESH)

        return dict(
            mine=lambda: pltpu.make_async_copy(src_ref, slab(*me), local_sem),
            first=lambda: [copy(0, me, sibling, True)] + [copy(1 + j, me, (*chip, c), True) for j, chip in enumerate(chips)],
            passed=lambda: [copy(4 + j, (*chip, c), sibling) for j, chip in enumerate(chips)],
            landed=lambda: [copy(1 + j, (*chip, c), me) for j, chip in enumerate(chips)],
            late=lambda: [copy(0, sibling, me)] + [copy(4 + j, (*chip, 1 - c), me) for j, chip in enumerate(chips)])

    def start(*refs):
        make = parts(*refs)
        make['mine']().start()
        for cp in make['first']():
            cp.start()

    def forward(*refs):
        make = parts(*refs)
        for arrived, onward in zip(make['landed'](), make['passed']()):
            arrived.wait_recv()
            onward.start()

    def finish(*refs):
        make = parts(*refs)
        for cp in make['late']():
            cp.wait_recv()
        for cp in make['first']() + make['passed']():
            cp.wait_send()
        make['mine']().wait()

    return start, forward, finish


def _adamw(recv, w, m, v, *, name):
    rows = w.shape[0]
    tr = _pick(rows, (512, 256, 128, 64))

    def kern(r_ref, w_ref, m_ref, v_ref, g_ref, d_ref, mo_ref, vo_ref):
        g = r_ref[0].astype(F32)
        for p in range(1, N_DEV):
            g = g + r_ref[p].astype(F32)
        mn = ADAM_B1 * m_ref[...] + (1.0 - ADAM_B1) * g
        vn = ADAM_B2 * v_ref[...] + (1.0 - ADAM_B2) * (g * g)
        m_hat = mn / (1.0 - ADAM_B1 ** ADAM_STEP)
        v_hat = vn / (1.0 - ADAM_B2 ** ADAM_STEP)
        g_ref[...] = g
        d_ref[...] = -ADAM_LR * (m_hat / (jnp.sqrt(v_hat) + ADAM_EPS) + ADAM_WD * w_ref[...])
        mo_ref[...] = mn
        vo_ref[...] = vn

    row = pl.BlockSpec((tr, LANES), lambda i: (i, 0))
    shp = jax.ShapeDtypeStruct((rows, LANES), F32)
    return _pcall(kern, name=name, grid=(rows // tr,), in_specs=[pl.BlockSpec((N_DEV, tr, LANES), lambda i: (0, i, 0)), row, row, row],
                  out_specs=[row, row, row, row], out_shape=[shp, shp, shp, shp], compiler_params=_params("parallel"))(recv, w, m, v)


INPUTS = ['x', 'mem'] + WEIGHTS + ['loss_target'] + ['m_' + n for n in WEIGHTS] + ['v_' + n for n in WEIGHTS]
SMALL_PAD_ROWS = 4


def kernel(x, mem, ffn1_norm, ffn1_w1, ffn1_w3, ffn1_w2, mix_norm, mem_norm, w_in, w_mem_kv, qn_dsa, kn_dsa, qn_mem, kn_mem, w_branch_sb, w_branch_dsa, w_branch_mem, w_gate, b_gate, w_out, ffn2_norm, ffn2_w1, ffn2_w3, ffn2_w2, loss_target, m_ffn1_norm, m_ffn1_w1, m_ffn1_w3, m_ffn1_w2, m_mix_norm, m_mem_norm, m_w_in, m_w_mem_kv, m_qn_dsa, m_kn_dsa, m_qn_mem, m_kn_mem, m_w_branch_sb, m_w_branch_dsa, m_w_branch_mem, m_w_gate, m_b_gate, m_w_out, m_ffn2_norm, m_ffn2_w1, m_ffn2_w3, m_ffn2_w2, v_ffn1_norm, v_ffn1_w1, v_ffn1_w3, v_ffn1_w2, v_mix_norm, v_mem_norm, v_w_in, v_w_mem_kv, v_qn_dsa, v_kn_dsa, v_qn_mem, v_kn_mem, v_w_branch_sb, v_w_branch_dsa, v_w_branch_mem, v_w_gate, v_b_gate, v_w_out, v_ffn2_norm, v_ffn2_w1, v_ffn2_w3, v_ffn2_w2):
    given = dict(zip(INPUTS, (x, mem, ffn1_norm, ffn1_w1, ffn1_w3, ffn1_w2, mix_norm, mem_norm, w_in, w_mem_kv, qn_dsa, kn_dsa, qn_mem, kn_mem, w_branch_sb, w_branch_dsa, w_branch_mem, w_gate, b_gate, w_out, ffn2_norm, ffn2_w1, ffn2_w3, ffn2_w2, loss_target, m_ffn1_norm, m_ffn1_w1, m_ffn1_w3, m_ffn1_w2, m_mix_norm, m_mem_norm, m_w_in, m_w_mem_kv, m_qn_dsa, m_kn_dsa, m_qn_mem, m_kn_mem, m_w_branch_sb, m_w_branch_dsa, m_w_branch_mem, m_w_gate, m_b_gate, m_w_out, m_ffn2_norm, m_ffn2_w1, m_ffn2_w3, m_ffn2_w2, v_ffn1_norm, v_ffn1_w1, v_ffn1_w3, v_ffn1_w2, v_mix_norm, v_mem_norm, v_w_in, v_w_mem_kv, v_qn_dsa, v_kn_dsa, v_qn_mem, v_kn_mem, v_w_branch_sb, v_w_branch_dsa, v_w_branch_mem, v_w_gate, v_b_gate, v_w_out, v_ffn2_norm, v_ffn2_w1, v_ffn2_w3, v_ffn2_w2), strict=True))
    wl = {n: given[n][0] for n in BIG}
    ws = {n: given[n] for n in SMALL}

    loss, dx, recv, g = _local_step(x[0], mem[0], loss_target[0], wl, ws)

    big = [{}, {}, {}, {}]
    for tag, names in (("ffn2", G_FFN2), ("mid", G_MID), ("ffn1", G_FFN1)):
        outs = _adamw(recv[tag], _pack_rows(wl, names), _pack_rows({n: given['m_' + n][0] for n in names}, names),
                      _pack_rows({n: given['v_' + n][0] for n in names}, names), name=f"adamw_{tag}")
        for kind, t in enumerate(outs):
            big[kind].update(_unpack_rows(t, wl, names))

    gs = _pack_small(g, SMALL, SMALL_PAD_ROWS)
    loss_row = gs.shape[0] - SMALL_PAD_ROWS
    gs = gs.at[loss_row, 0].set(loss)
    recv_s = _exchange(gs, _direct_phases(False), name="gather_small")
    small = _adamw(recv_s, _pack_small(ws, SMALL, SMALL_PAD_ROWS), _pack_small({n: given['m_' + n] for n in SMALL}, SMALL, SMALL_PAD_ROWS),
                   _pack_small({n: given['v_' + n] for n in SMALL}, SMALL, SMALL_PAD_ROWS), name="adamw_replicated")
    total_loss = small[0][loss_row, 0]
    small = [_unpack_small(t, ws, SMALL) for t in small]

    outs = [total_loss, dx[None]]
    for kind in range(4):
        outs += [big[kind][n][None] if n in wl else small[kind][n] for n in WEIGHTS]
    return tuple(outs)
```

```python
import functools

import jax
import jax.numpy as jnp
from jax import lax
from jax.experimental import pallas as pl
from jax.experimental.pallas import tpu as pltpu

F32 = jnp.float32
BF16 = jnp.bfloat16
MXU_DT = jnp.bfloat16

N_DEV = 8
HEAD_DIM = 64
SB_HEADS = 8
DSA_GROUPS = ((128, 1), (512, 4), (2048, 16))
DSA_HPG = 4
MEM_HEADS = 4
SB_W = SB_HEADS * HEAD_DIM
DSA_W = DSA_HPG * len(DSA_GROUPS) * HEAD_DIM
DSA_OUT_W = DSA_HPG * HEAD_DIM
MEM_W = MEM_HEADS * HEAD_DIM
ROPE_THETA = 10000.0
NORM_EPS = 1e-6
QB = 128
SCALE = HEAD_DIM ** -0.5
ADAM_LR, ADAM_B1, ADAM_B2, ADAM_EPS, ADAM_WD, ADAM_STEP = 0.001, 0.9, 0.999, 1e-08, 0.01, 10

LANES = 128
VMEM_LIMIT = 48 * 1024 * 1024
SB_DEAD = -110.0 * 1.4426950408889634

WEIGHTS = ['ffn1_norm', 'ffn1_w1', 'ffn1_w3', 'ffn1_w2', 'mix_norm', 'mem_norm', 'w_in', 'w_mem_kv', 'qn_dsa', 'kn_dsa',
           'qn_mem', 'kn_mem', 'w_branch_sb', 'w_branch_dsa', 'w_branch_mem', 'w_gate', 'b_gate', 'w_out', 'ffn2_norm',
           'ffn2_w1', 'ffn2_w3', 'ffn2_w2']
SHARD_AXIS = {'ffn1_norm': None, 'ffn1_w1': 1, 'ffn1_w3': 1, 'ffn1_w2': 0, 'mix_norm': None, 'mem_norm': None, 'w_in': 1,
              'w_mem_kv': 0, 'qn_dsa': None, 'kn_dsa': None, 'qn_mem': None, 'kn_mem': None, 'w_branch_sb': 1,
              'w_branch_dsa': 1, 'w_branch_mem': 1, 'w_gate': 1, 'b_gate': None, 'w_out': 0, 'ffn2_norm': None,
              'ffn2_w1': 1, 'ffn2_w3': 1, 'ffn2_w2': 0}
BIG = [n for n in WEIGHTS if SHARD_AXIS[n] is not None]
SMALL = [n for n in WEIGHTS if SHARD_AXIS[n] is None]


def _pcall(kern, **kw):
    return pl.pallas_call(kern, **kw)


def _params(*sem):
    return pltpu.CompilerParams(dimension_semantics=sem, vmem_limit_bytes=VMEM_LIMIT)


def _dot(a, b, dims):
    return lax.dot_general(a.astype(MXU_DT), b.astype(MXU_DT), (dims, ((), ())), preferred_element_type=F32)


def _nn(a, b):
    return _dot(a, b, ((1,), (0,)))


def _nt(a, b):
    return _dot(a, b, ((1,), (1,)))


def _tn(a, b):
    return _dot(a, b, ((0,), (0,)))


def _pick(n, prefs):
    for p in prefs:
        if n % p == 0:
            return p
    return n


def _matmul(a, b, *, name, ta=False, tb=False, out_dtype=F32, res=None, alpha=1.0, tm=1024, tn=512, tk=1024, pair2=None,
            epilogue=None, side=None):
    if ta:
        kdim, m = a.shape
    else:
        m, kdim = a.shape
    n = b.shape[0] if tb else b.shape[1]
    tm = _pick(m, (tm, 512, 256, 128))
    tn = _pick(n, (tn, 512, 384, 256, 128))
    tk = _pick(kdim, (tk, 1024, 512, 256, 128))
    nk = kdim // tk
    a_spec = pl.BlockSpec((tk, tm), lambda i, j, k: (k, i)) if ta else pl.BlockSpec((tm, tk), lambda i, j, k: (i, k))
    b_spec = pl.BlockSpec((tn, tk), lambda i, j, k: (j, k)) if tb else pl.BlockSpec((tk, tn), lambda i, j, k: (k, j))
    o_spec = pl.BlockSpec((tm, tn), lambda i, j, k: (i, j))
    v_spec = pl.BlockSpec((1, tn), lambda i, j, k: (0, j))
    dims = ((0 if ta else 1,), (1 if tb else 0,))
    n_mm = 2 if pair2 is None else 4
    if epilogue is None:
        row_ins, vec_ins = ([] if res is None else [res]), []
        out_dtypes, n_vec = [out_dtype], 0
    else:
        assert tn == n and res is None
        epi_fn, row_ins, vec_ins, out_dtypes, n_vec = epilogue
    n_row_out = len(out_dtypes)

    def kern(*refs):
        refs = list(refs)
        acc_ref = refs.pop() if nk > 1 else None
        mm = refs[:n_mm]
        extra = refs[n_mm:n_mm + len(row_ins) + len(vec_ins)]
        outs = refs[n_mm + len(extra):]
        i = pl.program_id(0)
        k = pl.program_id(2)

        def product():
            part = _dot(mm[0][...], mm[1][...], dims)
            if pair2 is not None:
                part = part + _dot(mm[2][...], mm[3][...], dims)
            return part

        def finish(r):
            if alpha != 1.0:
                r = r * alpha
            if epilogue is None:
                if extra:
                    r = extra[0][...] + r
                outs[0][...] = r.astype(out_dtype)
                return
            vals = epi_fn(r, *[e[...] for e in extra])
            for o_ref, v in zip(outs[:n_row_out], vals[:n_row_out]):
                o_ref[...] = v.astype(o_ref.dtype)
            for o_ref, v in zip(outs[n_row_out:], vals[n_row_out:]):
                @pl.when(i == 0)
                def _():
                    o_ref[...] = jnp.zeros_like(o_ref)

                o_ref[...] += v

        if nk == 1:
            finish(product())
            return

        @pl.when(k == 0)
        def _():
            acc_ref[...] = jnp.zeros_like(acc_ref)

        acc_ref[...] += product()

        @pl.when(k == nk - 1)
        def _():
            finish(acc_ref[...])

    ins = [a, b] + ([] if pair2 is None else list(pair2)) + list(row_ins) + list(vec_ins)
    specs = [a_spec, b_spec] * (n_mm // 2) + [o_spec] * len(row_ins) + [v_spec] * len(vec_ins)
    out_specs = [o_spec] * n_row_out + [v_spec] * n_vec
    out_shape = [jax.ShapeDtypeStruct((m, n), dt) for dt in out_dtypes] + [jax.ShapeDtypeStruct((1, n), F32)] * n_vec
    outs = _call_2d(kern, name=name, grid=(m // tm, n // tn, nk), in_specs=specs, out_specs=out_specs, out_shape=out_shape,
                    ins=ins, scratch_shapes=[pltpu.VMEM((tm, tn), F32)] if nk > 1 else [],
                    semantics=("arbitrary" if n_vec else "parallel", "parallel", "arbitrary"), side=side)
    carried = None
    if side is not None:
        outs, carried = outs
    outs = outs[0] if epilogue is None else outs
    return outs if side is None else (outs, carried)


def _epi_residual_rms(r, res, gain):
    xn = res + r
    return xn, xn * lax.rsqrt(jnp.mean(xn * xn, axis=-1, keepdims=True) + NORM_EPS) * gain


def _epi_rms_bwd(r, x, dres, gain):
    rs = lax.rsqrt(jnp.mean(x * x, axis=-1, keepdims=True) + NORM_EPS)
    xh = x * rs
    dy = r * gain
    dx = dres + rs * (dy - xh * jnp.mean(dy * xh, axis=-1, keepdims=True))
    return dx, dx, jnp.sum(r * xh, axis=0, keepdims=True)


def _epi_rms_bwd_sum(r, r0, x, dres, gain):
    return _epi_rms_bwd(r + r0, x, dres, gain)


def _epi_loss(r, res, target):
    e = (res + r) - target
    dy = e / e.shape[-1]
    return dy, dy, jnp.sum(e * e, axis=0, keepdims=True)
def _rms_fwd(x, g, *, name, side=None):
    s, d = x.shape
    ts = _pick(s, (512, 256))

    def kern(x_ref, g_ref, h_ref):
        xf = x_ref[...]
        r = lax.rsqrt(jnp.mean(xf * xf, axis=-1, keepdims=True) + NORM_EPS)
        h_ref[...] = (xf * r * g_ref[...]).astype(h_ref.dtype)

    outs = _call_2d(kern, name=name, grid=(s // ts,),
                    in_specs=[pl.BlockSpec((ts, d), lambda i: (i, 0)), pl.BlockSpec((1, d), lambda i: (0, 0))],
                    out_specs=[pl.BlockSpec((ts, d), lambda i: (i, 0))], out_shape=[jax.ShapeDtypeStruct((s, d), BF16)],
                    ins=[x, g], semantics=("parallel",), side=side)
    return outs[0] if side is None else (outs[0][0], outs[1])


def _rms_bwd(x, g, dh, res, *, name):
    s, d = x.shape
    ts = _pick(s, (512, 256))

    def kern(*refs):
        if res is None:
            x_ref, g_ref, dh_ref, dx_ref, dxb_ref, dg_ref = refs
            r_ref = None
        else:
            x_ref, g_ref, dh_ref, r_ref, dx_ref, dxb_ref, dg_ref = refs
        xf = x_ref[...]
        r = lax.rsqrt(jnp.mean(xf * xf, axis=-1, keepdims=True) + NORM_EPS)
        xh = xf * r
        dhf = dh_ref[...].astype(F32)
        dy = dhf * g_ref[...]
        dx = r * (dy - xh * jnp.mean(dy * xh, axis=-1, keepdims=True))
        if r_ref is not None:
            dx = r_ref[...] + dx
        dx_ref[...] = dx
        dxb_ref[...] = dx.astype(dxb_ref.dtype)

        @pl.when(pl.program_id(0) == 0)
        def _():
            dg_ref[...] = jnp.zeros_like(dg_ref)

        dg_ref[...] += jnp.sum(dhf * xh, axis=0, keepdims=True)

    row = pl.BlockSpec((ts, d), lambda i: (i, 0))
    vec = pl.BlockSpec((1, d), lambda i: (0, 0))
    ins = [x, g, dh] + ([] if res is None else [res])
    return _pcall(kern, name=name, grid=(s // ts,), in_specs=[row, vec, row] + ([] if res is None else [row]),
                  out_specs=[row, row, vec],
                  out_shape=[jax.ShapeDtypeStruct((s, d), F32), jax.ShapeDtypeStruct((s, d), BF16), jax.ShapeDtypeStruct((1, d), F32)],
                  compiler_params=_params("arbitrary"))(*ins)


def _sigmoid(x):
    return 1.0 / (1.0 + jnp.exp(-x))


FFN_TM, FFN_TF = 512, 1408


def _ffn_up(h, w1, w3, *, name, side=None):
    s, d = h.shape
    fdim = w1.shape[1]
    tm, tf = _pick(s, (FFN_TM, 256)), _pick(fdim, (FFN_TF, 512, 256, 128))

    def kern(h_ref, w1_ref, w3_ref, a_ref, b_ref, f_ref):
        hb = h_ref[...]
        a = _nn(hb, w1_ref[...])
        b = _nn(hb, w3_ref[...])
        a_ref[...] = a.astype(a_ref.dtype)
        b_ref[...] = b.astype(b_ref.dtype)
        f_ref[...] = (a * _sigmoid(a) * b).astype(f_ref.dtype)

    wspec = pl.BlockSpec((d, tf), lambda i, j: (0, j))
    ospec = pl.BlockSpec((tm, tf), lambda i, j: (i, j))
    shp = jax.ShapeDtypeStruct((s, fdim), BF16)
    return _call_2d(kern, name=name, grid=(s // tm, fdim // tf), in_specs=[pl.BlockSpec((tm, d), lambda i, j: (i, 0)), wspec, wspec],
                    out_specs=[ospec, ospec, ospec], out_shape=[shp, shp, shp], ins=[h, w1, w3],
                    semantics=("parallel", "parallel"), side=side)


def _ffn_dact(dy, w2, a, b, *, name, side=None):
    s, d = dy.shape
    fdim = w2.shape[0]
    tm, tf = _pick(s, (FFN_TM, 256)), _pick(fdim, (FFN_TF, 512, 256, 128))

    half = (tf // LANES + 1) // 2 * LANES

    def kern(dy_ref, w2_ref, a_ref, b_ref, da_ref, db_ref):
        dyb = dy_ref[...]
        pieces = ((0, half), (half, tf))
        dfs = [_nt(dyb, w2_ref[lo:hi, :]) * 0.5 for lo, hi in pieces]
        for (lo, hi), df in zip(pieces, dfs):
            av = a_ref[:, lo:hi].astype(F32)
            sg = _sigmoid(av)
            da_ref[:, lo:hi] = (df * b_ref[:, lo:hi].astype(F32) * (sg + av * sg * (1.0 - sg))).astype(da_ref.dtype)
            db_ref[:, lo:hi] = (df * (av * sg)).astype(db_ref.dtype)

    ospec = pl.BlockSpec((tm, tf), lambda i, j: (i, j))
    shp = jax.ShapeDtypeStruct((s, fdim), BF16)
    return _call_2d(kern, name=name, grid=(s // tm, fdim // tf),
                    in_specs=[pl.BlockSpec((tm, d), lambda i, j: (i, 0)), pl.BlockSpec((tf, d), lambda i, j: (j, 0)), ospec, ospec],
                    out_specs=[ospec, ospec], out_shape=[shp, shp], ins=[dy, w2, a, b], semantics=("parallel", "parallel"), side=side)


def _head_mean(v, bd):
    outs = []
    for c in range(v.shape[1] // LANES):
        x = v[:, c * LANES:(c + 1) * LANES]
        hi = x.astype(BF16)
        lo = (x - hi.astype(F32)).astype(BF16)
        outs.append(lax.dot_general(jnp.concatenate([hi, lo], axis=1), bd, (((1,), (0,)), ((), ())), preferred_element_type=F32))
    return outs[0] if len(outs) == 1 else jnp.concatenate(outs, axis=1)


def _partner(v):
    w = v.shape[1]
    lane = lax.broadcasted_iota(jnp.int32, v.shape, 1)
    return jnp.where(lane % HEAD_DIM < HEAD_DIM // 2, pltpu.roll(v, w - HEAD_DIM // 2, 1), pltpu.roll(v, HEAD_DIM // 2, 1))


def _block_diag(w=None):
    r = (lax.broadcasted_iota(jnp.int32, (2 * LANES, LANES), 0) % LANES) // HEAD_DIM
    c = lax.broadcasted_iota(jnp.int32, (2 * LANES, LANES), 1) // HEAD_DIM
    return jnp.where(r == c, 1.0 / HEAD_DIM, 0.0).astype(BF16)


def _rope_tables(s):
    half = HEAD_DIM // 2
    inv_freq = jnp.power(ROPE_THETA, -jnp.arange(half, dtype=F32) / half)
    ang = jnp.arange(s).astype(F32)[:, None] * inv_freq[None, :]
    cos, sin = jnp.cos(ang), jnp.sin(ang)
    cos2 = jnp.concatenate([cos, cos, cos, cos], axis=1)
    sin2 = jnp.concatenate([-sin, sin, -sin, sin], axis=1)
    return cos2, sin2


def _qknorm_fwd(src, col0, width, gain, rope, *, name, out_dtype=BF16):
    s = src.shape[0]
    ts = _pick(s, (512, 256))
    cb = col0 // width
    assert col0 % width == 0
    reps = width // LANES
    g = jnp.tile(gain, (1, width // HEAD_DIM))

    def kern(*refs):
        if rope is None:
            x_ref, g_ref, o_ref = refs
        else:
            x_ref, g_ref, c_ref, s_ref, o_ref = refs
        x = x_ref[...].astype(F32)
        bd = _block_diag(width)
        r = lax.rsqrt(_head_mean(x * x, bd) + NORM_EPS)
        y = x * r * g_ref[...]
        if rope is not None:
            y = y * jnp.tile(c_ref[...], (1, reps)) + _partner(y) * jnp.tile(s_ref[...], (1, reps))
        o_ref[...] = y.astype(o_ref.dtype)

    xs = pl.BlockSpec((ts, width), lambda i: (i, cb))
    tab = pl.BlockSpec((ts, LANES), lambda i: (i, 0))
    ins = [src, g] + ([] if rope is None else list(rope))
    specs = [xs, pl.BlockSpec((1, width), lambda i: (0, 0))] + ([] if rope is None else [tab, tab])
    return _pcall(kern, name=name, grid=(s // ts,), in_specs=specs, out_specs=pl.BlockSpec((ts, width), lambda i: (i, 0)),
                  out_shape=jax.ShapeDtypeStruct((s, width), out_dtype), compiler_params=_params("parallel"))(*ins)


def _qknorm_bwd(src, col0, width, gain, rope, dout, *, name):
    s = src.shape[0]
    ts = _pick(s, (512, 256))
    cb = col0 // width
    reps = width // LANES
    g = jnp.tile(gain, (1, width // HEAD_DIM))

    douts = list(dout) if isinstance(dout, (list, tuple)) else [dout]
    piece = width // len(douts)

    def kern(*refs):
        refs = list(refs)
        dg_ref = refs.pop()
        dx_ref = refs.pop()
        do_refs = [refs.pop() for _ in douts][::-1]
        if rope is None:
            x_ref, g_ref = refs
        else:
            x_ref, g_ref, c_ref, s_ref = refs
        x = x_ref[...].astype(F32)
        bd = _block_diag(width)
        r = lax.rsqrt(_head_mean(x * x, bd) + NORM_EPS)
        xh = x * r
        dy = jnp.concatenate([d[...].astype(F32) for d in do_refs], axis=1) if len(do_refs) > 1 else do_refs[0][...].astype(F32)
        if rope is not None:
            dy = dy * jnp.tile(c_ref[...], (1, reps)) + _partner(dy * jnp.tile(s_ref[...], (1, reps)))
        dxh = dy * g_ref[...]
        dx_ref[...] = (r * (dxh - xh * _head_mean(dxh * xh, bd))).astype(dx_ref.dtype)

        @pl.when(pl.program_id(0) == 0)
        def _():
            dg_ref[...] = jnp.zeros_like(dg_ref)

        dg_ref[...] += jnp.sum(dy * xh, axis=0, keepdims=True)

    xs = pl.BlockSpec((ts, width), lambda i: (i, cb))
    row = pl.BlockSpec((ts, width), lambda i: (i, 0))
    vec = pl.BlockSpec((1, width), lambda i: (0, 0))
    tab = pl.BlockSpec((ts, LANES), lambda i: (i, 0))
    ins = [src, g] + ([] if rope is None else list(rope)) + douts
    specs = [xs, vec] + ([] if rope is None else [tab, tab]) + [pl.BlockSpec((ts, piece), lambda i: (i, 0))] * len(douts)
    dx, dg = _pcall(kern, name=name, grid=(s // ts,), in_specs=specs, out_specs=[row, vec],
                    out_shape=[jax.ShapeDtypeStruct((s, width), BF16), jax.ShapeDtypeStruct((1, width), F32)],
                    compiler_params=_params("arbitrary"))(*ins)
    return dx, jnp.sum(dg.reshape(width // HEAD_DIM, HEAD_DIM), axis=0, keepdims=True)


def _tri(strict):
    r = lax.broadcasted_iota(jnp.int32, (2 * QB, QB), 0) % QB
    c = lax.broadcasted_iota(jnp.int32, (2 * QB, QB), 1)
    return jnp.where((r > c) if strict else (r >= c), 1.0, 0.0).astype(BF16)


def _split_dot(v, t2):
    hi = v.astype(BF16)
    lo = (v - hi.astype(F32)).astype(BF16)
    return lax.dot_general(jnp.concatenate([hi, lo], axis=1), t2, (((1,), (0,)), ((), ())), preferred_element_type=F32)


LOG2E = 1.4426950408889634


def _log2_sigmoids(z2):
    lf = -(jnp.maximum(z2, 0.0) + jnp.log2(1.0 + jnp.exp2(-jnp.abs(z2))))
    return z2 + lf, lf


SB2_SUB = 2


def _first_half(shape):
    return lax.broadcasted_iota(jnp.int32, shape, 1) < HEAD_DIM


def _split_pair(t, first):
    zero = jnp.zeros_like(t)
    return [jnp.where(first, t, zero), jnp.where(first, zero, t)]


def _sb2_fwd(p, *, name, side=None):
    s = p.shape[0]
    rq = SB2_SUB * QB
    nq = s // rq
    npair = SB_W // LANES

    def kern(q_ref, k_ref, v_ref, o_ref):
        i = pl.program_id(1)
        first = _first_half((rq, LANES))
        q2 = jnp.concatenate(_split_pair(q_ref[...], first), axis=0)
        t2 = _tri(True)
        rel = lax.broadcasted_iota(jnp.int32, (2 * rq, QB), 1) - lax.broadcasted_iota(jnp.int32, (2 * rq, QB), 0) % rq

        def tile(j, q, rel, carry, acc, masked):
            off = pl.multiple_of(j * QB, QB)
            ls, lf = _log2_sigmoids(_nt(q, k_ref[pl.ds(off, QB), :]) * (SCALE * LOG2E))
            if masked:
                before = rel < i * rq - j * QB
                lf = jnp.where(before, lf, 0.0)
            w = jnp.exp2(ls + _split_dot(lf, t2) + carry)
            if masked:
                w = jnp.where(before, w, 0.0)
            return carry + jnp.sum(lf, axis=1, keepdims=True), acc + _nn(w, v_ref[pl.ds(off, QB), :])

        carry, acc = jnp.zeros((2 * rq, 1), F32), jnp.zeros((2 * rq, LANES), F32)
        for a in range(SB2_SUB):
            carry, acc = tile(i * SB2_SUB + (SB2_SUB - 1 - a), q2, rel, carry, acc, True)

        def cond(st):
            return jnp.logical_and(st[0] >= 0, st[1] > 0)

        def body(st):
            carry, acc = tile(st[0], q2, rel, st[2], st[3], False)
            return st[0] - 1, (jnp.max(carry) > SB_DEAD).astype(jnp.int32), carry, acc

        st = lax.while_loop(cond, body, (i * SB2_SUB - 1, jnp.int32(1), carry, acc))
        o_ref[...] = jnp.where(first, st[3][:rq], st[3][rq:])

    outs = _call_2d(kern, name=name, grid=(npair, nq),
                    in_specs=[pl.BlockSpec((rq, LANES), lambda a, i: (i, a)), pl.BlockSpec((s, LANES), lambda a, i: (0, npair + a)),
                              pl.BlockSpec((s, LANES), lambda a, i: (0, 2 * npair + a))],
                    out_specs=[pl.BlockSpec((rq, LANES), lambda a, i: (i, a))], out_shape=[jax.ShapeDtypeStruct((s, SB_W), F32)],
                    ins=[p, p, p], semantics=("parallel", "arbitrary"), side=side)
    return outs[0] if side is None else (outs[0][0], outs[1])


def _sb2_bwd(p, o, do, *, name, side=None):
    s = p.shape[0]
    rq = SB2_SUB * QB
    nq = s // rq
    npair = SB_W // LANES

    def kern(q_ref, k_ref, v_ref, o_ref, do_ref, dq_ref, dk_hbm, dv_hbm, dk_acc, dv_acc, sem):
        pr = pl.program_id(0)
        i = pl.program_id(1)

        @pl.when(i == 0)
        def _():
            dk_acc[...] = jnp.zeros_like(dk_acc)
            dv_acc[...] = jnp.zeros_like(dv_acc)

        first = _first_half((rq, LANES))
        q2 = jnp.concatenate(_split_pair(q_ref[...], first), axis=0)
        do2 = jnp.concatenate(_split_pair(do_ref[...], first), axis=0)
        o2 = o_ref[...]
        dsum = jnp.sum(do2.astype(F32) * jnp.concatenate([o2, o2], axis=0), axis=1, keepdims=True)
        t_strict = _tri(True)
        t_incl = _tri(False)
        rel = lax.broadcasted_iota(jnp.int32, (2 * rq, QB), 1) - lax.broadcasted_iota(jnp.int32, (2 * rq, QB), 0) % rq

        def tile(j, rows, carry, gcarry, dq, masked):
            q, dob, dsm, rel = rows
            off = pl.multiple_of(j * QB, QB)
            kt = k_ref[pl.ds(off, QB), :]
            ls, lf = _log2_sigmoids(_nt(q, kt) * (SCALE * LOG2E))
            if masked:
                before = rel < i * rq - j * QB
                lf = jnp.where(before, lf, 0.0)
            w = jnp.exp2(ls + _split_dot(lf, t_strict) + carry)
            if masked:
                w = jnp.where(before, w, 0.0)
            wr = w.astype(MXU_DT)
            g = _nt(dob, v_ref[pl.ds(off, QB), :]) * wr.astype(F32)
            big_g = dsm - (_split_dot(g, t_incl) + gcarry)
            sig = jnp.exp2(ls)
            dz = g * (1.0 - sig) - sig * big_g
            if masked:
                dz = jnp.where(before, dz, 0.0)
            dz = dz * SCALE
            dk_acc[pl.ds(off, QB), :] += _tn(dz, q)
            dv_acc[pl.ds(off, QB), :] += _tn(wr, dob)
            return (carry + jnp.sum(lf, axis=1, keepdims=True), gcarry + jnp.sum(g, axis=1, keepdims=True),
                    dq + _nn(dz, kt))

        zc = jnp.zeros((2 * rq, 1), F32)
        carry, gcarry, dq = zc, zc, jnp.zeros((2 * rq, LANES), F32)
        whole = (q2, do2, dsum, rel)
        for a in range(SB2_SUB):
            carry, gcarry, dq = tile(i * SB2_SUB + (SB2_SUB - 1 - a), whole, carry, gcarry, dq, True)

        def cond(st):
            return jnp.logical_and(st[0] >= 0, st[1] > 0)

        def body(st):
            carry, gcarry, dq = tile(st[0], whole, st[2], st[3], st[4], False)
            return st[0] - 1, (jnp.max(carry) > SB_DEAD).astype(jnp.int32), carry, gcarry, dq

        st = lax.while_loop(cond, body, (i * SB2_SUB - 1, jnp.int32(1), carry, gcarry, dq))
        dq_ref[...] = jnp.where(first, st[4][:rq], st[4][rq:])

        @pl.when(i == nq - 1)
        def _():
            cols = pl.ds(pl.multiple_of(pr * LANES, LANES), LANES)
            ck = pltpu.make_async_copy(dk_acc, dk_hbm.at[:, cols], sem.at[0])
            cv = pltpu.make_async_copy(dv_acc, dv_hbm.at[:, cols], sem.at[1])
            ck.start()
            cv.start()
            ck.wait()
            cv.wait()

    blk = pl.BlockSpec((rq, LANES), lambda a, i: (i, a))
    anyspace = pl.BlockSpec(memory_space=pl.ANY)
    shp = jax.ShapeDtypeStruct((s, SB_W), F32)
    return _call_2d(kern, name=name, grid=(npair, nq),
                    in_specs=[blk, pl.BlockSpec((s, LANES), lambda a, i: (0, npair + a)),
                              pl.BlockSpec((s, LANES), lambda a, i: (0, 2 * npair + a)), blk, blk],
                    out_specs=[blk, anyspace, anyspace], out_shape=[shp, shp, shp], ins=[p, p, p, o, do],
                    scratch_shapes=[pltpu.VMEM((s, LANES), F32), pltpu.VMEM((s, LANES), F32), pltpu.SemaphoreType.DMA((2,))],
                    semantics=("arbitrary", "arbitrary"), side=side)


def _dsa_rel():
    qi = lax.broadcasted_iota(jnp.int32, (QB, QB), 0)
    kj = lax.broadcasted_iota(jnp.int32, (QB, QB), 1)
    return kj - qi


def _prev_mask(rel, has_prev):
    return rel >= jnp.where(has_prev, 0, QB)


DSA_BT = QB * max(r for _, r in DSA_GROUPS)
DSA_UB = 4


def _bdot(a, b, ca, cb):
    return lax.dot_general(a.astype(MXU_DT), b.astype(MXU_DT), (((ca,), (cb,)), ((0,), (0,))), preferred_element_type=F32)


def _bnt(a, b):
    return _bdot(a, b, 2, 2)


def _bnn(a, b):
    return _bdot(a, b, 2, 1)


def _btn(a, b):
    return _bdot(a, b, 1, 1)


def _unit_rows(r, c, b):
    return pl.ds(c + QB * r * b, QB, stride=r)


def _pair_cols(t, first):
    return [jnp.max(jnp.where(first, t, -jnp.inf), axis=1, keepdims=True),
            jnp.max(jnp.where(first, -jnp.inf, t), axis=1, keepdims=True)]


def _dsa2_fwd(qn, kn, v32, g, *, name):
    s = qn.shape[0]
    r = DSA_GROUPS[g][1]
    nbk = DSA_BT // (QB * r)
    npair = DSA_OUT_W // LANES

    def kern(q_ref, k_ref, kp_ref, v_ref, vp_ref, o_ref, l_ref):
        t = pl.program_id(1)
        first = _first_half((QB, LANES))
        rel = _dsa_rel()
        units = [(c, b) for c in range(r) for b in range(nbk)]
        for u0 in range(0, len(units), DSA_UB):
            batch = units[u0:u0 + DSA_UB]
            qs, kcs, vcs, kps, vps, masks = [], [], [], [], [], []
            for c, b in batch:
                rows = _unit_rows(r, c, b)
                kc, vc = k_ref[rows, :].astype(MXU_DT), v_ref[rows, :].astype(MXU_DT)
                if b > 0:
                    prow = _unit_rows(r, c, b - 1)
                    kpv, vpv, has_prev = k_ref[prow, :], v_ref[prow, :], True
                else:
                    prow = _unit_rows(r, c, nbk - 1)
                    kpv, vpv, has_prev = kp_ref[prow, :], vp_ref[prow, :], t > 0
                for qe in _split_pair(q_ref[rows, :], first):
                    qs.append(qe.astype(MXU_DT))
                    kcs.append(kc)
                    vcs.append(vc)
                    kps.append(kpv.astype(MXU_DT))
                    vps.append(vpv.astype(MXU_DT))
                    masks.append(_prev_mask(rel, has_prev))
            qq = jnp.stack(qs)
            sc = jnp.where(rel <= 0, _bnt(qq, jnp.stack(kcs)) * SCALE, -jnp.inf)
            sp = _bnt(qq, jnp.stack(kps)) * SCALE
            sp = jnp.stack([jnp.where(mk, sp[n], -jnp.inf) for n, mk in enumerate(masks)])
            m = jnp.maximum(jnp.max(sc, axis=2, keepdims=True), jnp.max(sp, axis=2, keepdims=True))
            pc = jnp.exp(sc - m)
            pp = jnp.exp(sp - m)
            den = jnp.sum(pc, axis=2, keepdims=True) + jnp.sum(pp, axis=2, keepdims=True)
            out = (_bnn(pc, jnp.stack(vcs)) + _bnn(pp, jnp.stack(vps))) / den
            lse = m + jnp.log(den)
            for idx, (c, b) in enumerate(batch):
                rows = _unit_rows(r, c, b)
                o_ref[rows, :] = jnp.where(first, out[2 * idx], out[2 * idx + 1])
                l_ref[rows, :] = jnp.where(first, lse[2 * idx], lse[2 * idx + 1])

    npg = DSA_HPG * HEAD_DIM // LANES
    cur = pl.BlockSpec((DSA_BT, LANES), lambda a, t: (t, npg * g + a))
    prev = pl.BlockSpec((DSA_BT, LANES), lambda a, t: (jnp.maximum(t - 1, 0), npg * g + a))
    out = pl.BlockSpec((DSA_BT, LANES), lambda a, t: (t, a))
    shp = jax.ShapeDtypeStruct((s, DSA_OUT_W), F32)
    return _pcall(kern, name=name, grid=(npair, s // DSA_BT), in_specs=[cur, cur, prev, cur, prev], out_specs=[out, out],
                  out_shape=[shp, shp], compiler_params=_params("parallel", "parallel"))(qn, kn, kn, v32, v32)


def _dsa2_combine(parts, *, name):
    s, wd = parts[0][0].shape
    ts = _pick(s, (512, 256))

    def kern(o0, l0, o1, l1, o2, l2, o_ref, l_ref):
        ls = [l0[...], l1[...], l2[...]]
        m = jnp.maximum(jnp.maximum(ls[0], ls[1]), ls[2])
        es = [jnp.exp(l - m) for l in ls]
        den = es[0] + es[1] + es[2]
        o_ref[...] = (es[0] * o0[...] + es[1] * o1[...] + es[2] * o2[...]) / den
        l_ref[...] = m + jnp.log(den)

    blk = pl.BlockSpec((ts, wd), lambda i: (i, 0))
    shp = jax.ShapeDtypeStruct((s, wd), F32)
    flat = [t for pair in parts for t in pair]
    return _pcall(kern, name=name, grid=(s // ts,), in_specs=[blk] * 6, out_specs=[blk, blk], out_shape=[shp, shp],
                  compiler_params=_params("parallel"))(*flat)


def _dsa2_prep(o, do, *, name):
    s, wd = o.shape
    ts = _pick(s, (512, 256))

    def kern(o_ref, do_ref, d_ref):
        d_ref[...] = _head_mean(do_ref[...] * o_ref[...], _block_diag(wd)) * HEAD_DIM

    blk = pl.BlockSpec((ts, wd), lambda i: (i, 0))
    return _pcall(kern, name=name, grid=(s // ts,), in_specs=[blk, blk], out_specs=blk,
                  out_shape=jax.ShapeDtypeStruct((s, wd), F32), compiler_params=_params("parallel"))(o, do)


def _dsa2_bwd(qn, kn, v32, do, lse, dd, g, *, name):
    s = qn.shape[0]
    r = DSA_GROUPS[g][1]
    nbk = DSA_BT // (QB * r)
    npair = DSA_OUT_W // LANES
    nsteps = s // DSA_BT

    def kern(q_ref, qn_ref, k_ref, kp_ref, v_ref, vp_ref, do_ref, don_ref, l_ref, ln_ref, d_ref, dn_ref,
             dq_ref, dk_ref, dv_ref):
        t = pl.program_id(1)
        first = _first_half((QB, LANES))
        rel = _dsa_rel()

        def pairs(items):
            qq = jnp.stack([it[0].astype(MXU_DT) for it in items])
            dd = jnp.stack([it[1].astype(MXU_DT) for it in items])
            kk = jnp.stack([it[4].astype(MXU_DT) for it in items])
            vv = jnp.stack([it[5].astype(MXU_DT) for it in items])
            p = jnp.exp(_bnt(qq, kk) * SCALE - jnp.stack([it[2] for it in items]))
            p = jnp.stack([jnp.where(it[6], p[n], 0.0) for n, it in enumerate(items)])
            ds = p * (_bnt(dd, vv) - jnp.stack([it[3] for it in items])) * SCALE
            return _bnn(ds, kk), _btn(ds, qq), _btn(p, dd)

        def heads(rows, qr, dor, lr, dr):
            return list(zip(_split_pair(qr[rows, :], first), _split_pair(dor[rows, :], first),
                            _pair_cols(lr[rows, :], first), _pair_cols(dr[rows, :], first)))

        units = [(c, b) for c in range(r) for b in range(nbk)]
        dk_of, dv_of = [None] * len(units), [None] * len(units)
        for u0 in range(0, len(units), DSA_UB // 2):
            batch = list(enumerate(units))[u0:u0 + DSA_UB // 2]
            items = []
            for u, (c, b) in batch:
                rows = _unit_rows(r, c, b)
                kc, vc = k_ref[rows, :], v_ref[rows, :]
                if b > 0:
                    prow = _unit_rows(r, c, b - 1)
                    kpv, vpv, pmask = k_ref[prow, :], v_ref[prow, :], _prev_mask(rel, True)
                else:
                    prow = _unit_rows(r, c, nbk - 1)
                    kpv, vpv, pmask = kp_ref[prow, :], vp_ref[prow, :], _prev_mask(rel, t > 0)
                for hd in heads(rows, q_ref, do_ref, l_ref, d_ref):
                    items.append(hd + (kc, vc, rel <= 0))
                    items.append(hd + (kpv, vpv, pmask))
            dq, dk, dv = pairs(items)
            for n, (u, (c, b)) in enumerate(batch):
                dq_ref[_unit_rows(r, c, b), :] = jnp.where(first, dq[4 * n] + dq[4 * n + 1], dq[4 * n + 2] + dq[4 * n + 3])
                dk_of[u] = dk[4 * n] + dk[4 * n + 2]
                dv_of[u] = dv[4 * n] + dv[4 * n + 2]
                if b > 0:
                    dk_of[u - 1] = dk_of[u - 1] + (dk[4 * n + 1] + dk[4 * n + 3])
                    dv_of[u - 1] = dv_of[u - 1] + (dv[4 * n + 1] + dv[4 * n + 3])
        lasts = [c * nbk + nbk - 1 for c in range(r)]
        for c0 in range(0, r, DSA_UB):
            chunk = list(range(c0, min(c0 + DSA_UB, r)))
            items = []
            for c in chunk:
                last = _unit_rows(r, c, nbk - 1)
                for hd in heads(_unit_rows(r, c, 0), qn_ref, don_ref, ln_ref, dn_ref):
                    items.append(hd + (k_ref[last, :], v_ref[last, :], _prev_mask(rel, t < nsteps - 1)))
            _, dk, dv = pairs(items)
            for n, c in enumerate(chunk):
                dk_of[lasts[c]] = dk_of[lasts[c]] + (dk[2 * n] + dk[2 * n + 1])
                dv_of[lasts[c]] = dv_of[lasts[c]] + (dv[2 * n] + dv[2 * n + 1])
        for u, (c, b) in enumerate(units):
            dk_ref[_unit_rows(r, c, b), :] = dk_of[u]
            dv_ref[_unit_rows(r, c, b), :] = dv_of[u]

    npg = DSA_HPG * HEAD_DIM // LANES

    def at(shift, col):
        return pl.BlockSpec((DSA_BT, LANES), lambda a, t: (jnp.clip(t + shift, 0, nsteps - 1), col(a)))

    gcol = lambda a: npg * g + a
    ocol = lambda a: a
    specs = [at(0, gcol), at(1, gcol), at(0, gcol), at(-1, gcol), at(0, gcol), at(-1, gcol),
             at(0, ocol), at(1, ocol), at(0, ocol), at(1, ocol), at(0, ocol), at(1, ocol)]
    shp = jax.ShapeDtypeStruct((s, DSA_OUT_W), F32)
    return _pcall(kern, name=name, grid=(npair, nsteps), in_specs=specs, out_specs=[at(0, ocol)] * 3, out_shape=[shp, shp, shp],
                  compiler_params=_params("parallel", "parallel"))(qn, qn, kn, kn, v32, v32, do, do, lse, lse, dd, dd)


def _mem2_fwd(qn, km, kv, *, name):
    s = qn.shape[0]
    ml = km.shape[0]
    tq = _pick(s, (512, 256))
    npair = MEM_W // LANES

    def kern(q_ref, k_ref, v_ref, o_ref):
        first = _first_half((tq, LANES))
        q2 = jnp.concatenate(_split_pair(q_ref[...], first), axis=0)
        sc = _nt(q2, k_ref[...]) * SCALE
        e = jnp.exp(sc - jnp.max(sc, axis=1, keepdims=True))
        o2 = _nn(e / jnp.sum(e, axis=1, keepdims=True), v_ref[...])
        o_ref[...] = jnp.where(first, o2[:tq], o2[tq:])

    blk = pl.BlockSpec((tq, LANES), lambda a, i: (i, a))
    return _pcall(kern, name=name, grid=(npair, s // tq),
                  in_specs=[blk, pl.BlockSpec((ml, LANES), lambda a, i: (0, a)), pl.BlockSpec((ml, LANES), lambda a, i: (0, npair + a))],
                  out_specs=blk, out_shape=jax.ShapeDtypeStruct((s, MEM_W), F32),
                  compiler_params=_params("parallel", "parallel"))(qn, km, kv)


def _mem2_bwd(qn, km, kv, do, *, name):
    s = qn.shape[0]
    ml = km.shape[0]
    tq = _pick(s, (512, 256))
    npair = MEM_W // LANES

    def kern(q_ref, k_ref, v_ref, do_ref, dq_ref, dk_ref, dv_ref):
        @pl.when(pl.program_id(1) == 0)
        def _():
            dk_ref[...] = jnp.zeros_like(dk_ref)
            dv_ref[...] = jnp.zeros_like(dv_ref)

        first = _first_half((tq, LANES))
        q2 = jnp.concatenate(_split_pair(q_ref[...], first), axis=0)
        do2 = jnp.concatenate(_split_pair(do_ref[...], first), axis=0)
        sc = _nt(q2, k_ref[...]) * SCALE
        e = jnp.exp(sc - jnp.max(sc, axis=1, keepdims=True))
        p = e / jnp.sum(e, axis=1, keepdims=True)
        dp = _nt(do2, v_ref[...])
        ds = p * (dp - jnp.sum(p * dp, axis=1, keepdims=True)) * SCALE
        dq2 = _nn(ds, k_ref[...])
        dk_ref[...] += _tn(ds, q2)
        dv_ref[...] += _tn(p, do2)
        dq_ref[...] = jnp.where(first, dq2[:tq], dq2[tq:])

    blk = pl.BlockSpec((tq, LANES), lambda a, i: (i, a))
    kblk = pl.BlockSpec((ml, LANES), lambda a, i: (0, a))
    kshape = jax.ShapeDtypeStruct((ml, MEM_W), F32)
    return _pcall(kern, name=name, grid=(npair, s // tq),
                  in_specs=[blk, kblk, pl.BlockSpec((ml, LANES), lambda a, i: (0, npair + a)), blk],
                  out_specs=[blk, kblk, kblk], out_shape=[jax.ShapeDtypeStruct((s, MEM_W), F32), kshape, kshape],
                  compiler_params=_params("parallel", "arbitrary"))(qn, km, kv, do)


def _merge_fwd(logits, bias, ya, yb, yc, *, name):
    s, d = ya.shape
    ts = _pick(s, (512, 256))

    def kern(l0, l1, l2, b0, b1, b2, a_ref, b_ref, c_ref, o_ref):
        m = 0.0
        for l_ref, bb_ref, y_ref in ((l0, b0, a_ref), (l1, b1, b_ref), (l2, b2, c_ref)):
            m = m + _sigmoid(l_ref[...].astype(F32) + bb_ref[...]) * y_ref[...].astype(F32)
        o_ref[...] = m.astype(o_ref.dtype)

    row = pl.BlockSpec((ts, d), lambda i: (i, 0))
    lg = [pl.BlockSpec((ts, d), functools.partial(lambda i, c: (i, c), c=c)) for c in range(3)]
    bs = [pl.BlockSpec((1, d), functools.partial(lambda i, c: (0, c), c=c)) for c in range(3)]
    return _pcall(kern, name=name, grid=(s // ts,), in_specs=lg + bs + [row, row, row], out_specs=row,
                  out_shape=jax.ShapeDtypeStruct((s, d), BF16),
                  compiler_params=_params("parallel"))(logits, logits, logits, bias, bias, bias, ya, yb, yc)


def _merge_bwd(logits, bias, ya, yb, yc, dm, *, name):
    s, d = ya.shape
    ts = _pick(s, (256,))

    def kern(l0, l1, l2, b0, b1, b2, a_ref, b_ref, c_ref, dm_ref, da_ref, db_ref, dc_ref, dl0, dl1, dl2, dbias0, dbias1, dbias2):
        first = pl.program_id(0) == 0
        dmv = dm_ref[...]
        for l_ref, bb_ref, y_ref, dy_ref, dl_ref, dbias_ref in ((l0, b0, a_ref, da_ref, dl0, dbias0), (l1, b1, b_ref, db_ref, dl1, dbias1),
                                                                (l2, b2, c_ref, dc_ref, dl2, dbias2)):
            g = _sigmoid(l_ref[...].astype(F32) + bb_ref[...])
            dy_ref[...] = (dmv * g).astype(dy_ref.dtype)
            dl = dmv * y_ref[...].astype(F32) * g * (1.0 - g)
            dl_ref[...] = dl.astype(dl_ref.dtype)

            @pl.when(first)
            def _():
                dbias_ref[...] = jnp.zeros_like(dbias_ref)

            dbias_ref[...] += jnp.sum(dl, axis=0, keepdims=True)

    row = pl.BlockSpec((ts, d), lambda i: (i, 0))
    lg = [pl.BlockSpec((ts, d), functools.partial(lambda i, c: (i, c), c=c)) for c in range(3)]
    bs = [pl.BlockSpec((1, d), functools.partial(lambda i, c: (0, c), c=c)) for c in range(3)]
    vec = pl.BlockSpec((1, d), lambda i: (0, 0))
    yshape = jax.ShapeDtypeStruct((s, d), BF16)
    vshape = jax.ShapeDtypeStruct((1, d), F32)
    outs = _pcall(kern, name=name, grid=(s // ts,), in_specs=lg + bs + [row, row, row, row],
                  out_specs=[row, row, row, row, row, row, vec, vec, vec],
                  out_shape=[yshape] * 6 + [vshape] * 3,
                  compiler_params=_params("arbitrary"))(logits, logits, logits, bias, bias, bias, ya, yb, yc, dm)
    return outs[0], outs[1], outs[2], outs[3:6], jnp.concatenate(outs[6:9], axis=1)


G_FFN1 = ['ffn1_w1', 'ffn1_w3', 'ffn1_w2']
G_FFN2 = ['ffn2_w1', 'ffn2_w3', 'ffn2_w2']
G_MID = [n for n in BIG if n not in G_FFN1 + G_FFN2]


def _ffn_fwd(h, w1, w3, w2, tag, epilogue, side=None):
    carried = None
    if side is None:
        a, b, f = _ffn_up(h, w1, w3, name=f"{tag}_up")
    else:
        (a, b, f), carried = _ffn_up(h, w1, w3, name=f"{tag}_up", side=side)
    outs = _matmul(f, w2, name=f"{tag}_down", alpha=0.5, tm=512, tn=1024, tk=2816, epilogue=epilogue)
    return outs, (h, a, b, f), carried


def _ffn_bwd(x, norm, w1, w3, w2, saved, dy, dyb, tag, side=None, own_side=None):
    h, a, b, f = saved
    dw2 = _matmul(f, dyb, name=f"{tag}_dw2", ta=True, alpha=0.5, tm=1408, tn=1024, tk=2048)
    carried = None
    if side is None:
        da, db = _ffn_dact(dyb, w2, a, b, name=f"{tag}_dact")
    else:
        (da, db), carried = _ffn_dact(dyb, w2, a, b, name=f"{tag}_dact", side=side)
    dw1 = _matmul(h, da, name=f"{tag}_dw1", ta=True, tm=1024, tn=1408, tk=2048)
    dw3 = _matmul(h, db, name=f"{tag}_dw3", ta=True, tm=1024, tn=1408, tk=2048)
    outs = _matmul(da, w1, name=f"{tag}_dh", tb=True, tm=512, tn=1024, tk=1408, pair2=(db, w3),
                   epilogue=(_epi_rms_bwd, [x, dy], [norm], [F32, BF16], 1),
                   side=None if own_side is None else own_side(dw1, dw3, dw2))
    (dx, dxb, dnorm), own = outs if own_side is not None else (outs, None)
    return dx, dxb, dnorm, dw1, dw3, dw2, carried, own


def _local_step(x, mem, loss_target, wl, ws):
    s, d = x.shape
    assert s % (QB * 16) == 0
    rope = _rope_tables(s)
    bf = {n: wl[n].astype(BF16) for n in BIG}
    w = dict(ws)

    def gather(names):
        return _side([bf[n] for n in names], _two_level_phases())

    def whole(names, gathered):
        return {n: _whole_weight(n, t) for n, t in zip(names, gathered)}

    h1, early = _rms_fwd(x, w['ffn1_norm'], name="ffn1_rms", side=gather(G_FFN1))
    w.update(whole(G_FFN1, early))
    (x1, h), sv1, late = _ffn_fwd(h1, w['ffn1_w1'], w['ffn1_w3'], w['ffn1_w2'], "ffn1",
                                  (_epi_residual_rms, [x], [w['mix_norm']], [F32, BF16], 0),
                                  side=gather(G_MID))
    w.update(whole(G_MID, late))
    p = _matmul(h, w['w_in'], name="in_proj", out_dtype=BF16, tn=1024)
    logits = _matmul(h, w['w_gate'], name="gate_proj", out_dtype=BF16, tn=1024)
    c_qb, c_kb, c_vb, c_qc = 3 * SB_W, 3 * SB_W + DSA_W, 3 * SB_W + 2 * DSA_W, 3 * SB_W + 3 * DSA_W

    oa_t, late = _sb2_fwd(p, name="sb_fwd", side=gather(G_FFN2))
    w.update(whole(G_FFN2, late))
    ya = _matmul(oa_t, w['w_branch_sb'], name="sb_out", out_dtype=BF16)

    qb_n = _qknorm_fwd(p, c_qb, DSA_W, w['qn_dsa'], rope, name="dsa_qnorm", out_dtype=F32)
    kb_n = _qknorm_fwd(p, c_kb, DSA_W, w['kn_dsa'], rope, name="dsa_knorm", out_dtype=F32)
    vb32 = p[:, c_vb:c_vb + DSA_W].astype(F32)
    groups = range(len(DSA_GROUPS))
    ob_t, lse_b = _dsa2_combine([_dsa2_fwd(qb_n, kb_n, vb32, gi, name=f"dsa_fwd{gi}") for gi in groups], name="dsa_combine")
    yb = _matmul(ob_t, w['w_branch_dsa'], name="dsa_out", out_dtype=BF16)

    memh = _rms_fwd(mem, w['mem_norm'], name="mem_rms")
    kv = _matmul(memh, w['w_mem_kv'], name="mem_kv", out_dtype=BF16)
    km_n = _qknorm_fwd(kv, 0, MEM_W, w['kn_mem'], None, name="mem_knorm")
    qc_n = _qknorm_fwd(p, c_qc, MEM_W, w['qn_mem'], None, name="mem_qnorm")
    oc_t = _mem2_fwd(qc_n, km_n, kv, name="mem_fwd")
    yc = _matmul(oc_t, w['w_branch_mem'], name="mem_out", out_dtype=BF16)

    merged = _merge_fwd(logits, w['b_gate'], ya, yb, yc, name="merge")
    x2, h2 = _matmul(merged, w['w_out'], name="out_proj", tn=1024,
                     epilogue=(_epi_residual_rms, [x1], [w['ffn2_norm']], [F32, BF16], 0))
    (dx3, dx3b, sq), sv2, _ = _ffn_fwd(h2, w['ffn2_w1'], w['ffn2_w3'], w['ffn2_w2'], "ffn2",
                                       (_epi_loss, [x2, loss_target], [], [F32, BF16], 1))
    loss = jnp.sum(sq) * (0.5 / d)

    g, recv = {}, {}

    def owners(names):
        return [_for_owners(n, g[n], wl[n].shape) for n in names]

    dx2, dx2b, g['ffn2_norm'], g['ffn2_w1'], g['ffn2_w3'], g['ffn2_w2'], _, _ = _ffn_bwd(
        x2, w['ffn2_norm'], w['ffn2_w1'], w['ffn2_w3'], w['ffn2_w2'], sv2, dx3, dx3b, "ffn2")

    g['w_out'] = _matmul(merged, dx2b, name="d_w_out", ta=True, tn=1024, tk=512)
    dm = _matmul(dx2b, w['w_out'], name="d_merged", tb=True, tn=1024)
    dya, dyb, dyc, dlog, g['b_gate'] = _merge_bwd(logits, w['b_gate'], ya, yb, yc, dm, name="d_merge")
    dlogits = jnp.concatenate(dlog, axis=1)

    g['w_branch_sb'] = _matmul(oa_t, dya, name="d_w_sb", ta=True, tn=1024, tk=512)
    g['w_branch_dsa'] = _matmul(ob_t, dyb, name="d_w_dsa", ta=True, tk=512)
    g['w_branch_mem'] = _matmul(oc_t, dyc, name="d_w_mem", ta=True, tk=512)
    doa = _matmul(dya, w['w_branch_sb'], name="d_oa", tb=True, out_dtype=BF16)
    dob = _matmul(dyb, w['w_branch_dsa'], name="d_ob", tb=True)
    doc = _matmul(dyc, w['w_branch_mem'], name="d_oc", tb=True, out_dtype=BF16)

    (dqa, dka, dva), got = _sb2_bwd(p, oa_t, doa, name="sb_bwd", side=_side(owners(G_FFN2), _direct_phases(True)))
    recv.update(zip(G_FFN2, got))

    dd_b = _dsa2_prep(ob_t, dob, name="dsa_prep")
    dgrp = [_dsa2_bwd(qb_n, kb_n, vb32, dob, lse_b, dd_b, gi, name=f"dsa_bwd{gi}") for gi in groups]
    dvb = jnp.concatenate([t[2] for t in dgrp], axis=1).astype(BF16)
    dqb, g['qn_dsa'] = _qknorm_bwd(p, c_qb, DSA_W, w['qn_dsa'], rope, [t[0] for t in dgrp], name="d_dsa_qnorm")
    dkb, g['kn_dsa'] = _qknorm_bwd(p, c_kb, DSA_W, w['kn_dsa'], rope, [t[1] for t in dgrp], name="d_dsa_knorm")

    dqc_n, dkm_n, dvm = _mem2_bwd(qc_n, km_n, kv, doc, name="mem_bwd")
    dqc, g['qn_mem'] = _qknorm_bwd(p, c_qc, MEM_W, w['qn_mem'], None, dqc_n, name="d_mem_qnorm")
    dkm, g['kn_mem'] = _qknorm_bwd(kv, 0, MEM_W, w['kn_mem'], None, dkm_n, name="d_mem_knorm")
    dkv = jnp.concatenate([dkm, dvm.astype(BF16)], axis=1)
    g['w_mem_kv'] = _matmul(memh, dkv, name="d_w_mem_kv", ta=True)
    dmemh = _matmul(dkv, w['w_mem_kv'], name="d_memh", tb=True)
    _, _, g['mem_norm'] = _rms_bwd(mem, w['mem_norm'], dmemh, None, name="d_mem_rms")

    dp = jnp.concatenate([dqa.astype(BF16), dka.astype(BF16), dva.astype(BF16),
                          dqb, dkb, dvb, dqc], axis=1)
    g['w_in'] = _matmul(h, dp, name="d_w_in", ta=True, tn=2048, tk=1024)
    g['w_gate'] = _matmul(h, dlogits, name="d_w_gate", ta=True, tn=1536, tk=1024)
    dh = _matmul(dp, w['w_in'], name="d_h_in", tb=True, tn=1024, tk=2048)
    dx1, dx1b, g['mix_norm'] = _matmul(dlogits, w['w_gate'], name="d_h_gate", tb=True, tm=512, tn=1024, tk=3072,
                                       epilogue=(_epi_rms_bwd_sum, [dh, x1, dx2], [w['mix_norm']], [F32, BF16], 1))

    def own_side(dw1, dw3, dw2):
        g.update(ffn1_w1=dw1, ffn1_w3=dw3, ffn1_w2=dw2)
        return _side(owners(G_FFN1), _direct_phases(True))

    dx0, _, g['ffn1_norm'], _, _, _, got_mid, got_own = _ffn_bwd(
        x, w['ffn1_norm'], w['ffn1_w1'], w['ffn1_w3'], w['ffn1_w2'], sv1, dx1, dx1b, "ffn1",
        side=_side(owners(G_MID), _direct_phases(True)), own_side=own_side)
    recv.update(zip(G_MID, got_mid))
    recv.update(zip(G_FFN1, got_own))
    return loss, dx0, recv, {n: g[n] for n in SMALL}


def _whole_weight(name, gathered):
    _, r, c = gathered.shape
    return gathered.reshape(N_DEV * r, c) if SHARD_AXIS[name] == 0 else gathered.transpose(1, 0, 2).reshape(r, N_DEV * c)


def _for_owners(name, grad, shard_shape):
    r, c = shard_shape
    blk = grad.reshape(N_DEV, r, c) if SHARD_AXIS[name] == 0 else grad.reshape(r, N_DEV, c).transpose(1, 0, 2)
    return blk.astype(BF16)


def _pack_small(d, names, extra_rows):
    parts = []
    for n in names:
        v = d[n].reshape(-1)
        pad = (-v.size) % LANES
        parts.append(jnp.concatenate([v, jnp.zeros((pad,), v.dtype)]).reshape(-1, LANES))
    t = jnp.concatenate(parts, axis=0)
    return jnp.concatenate([t, jnp.zeros((extra_rows, LANES), t.dtype)], axis=0)


def _unpack_small(t, like, names):
    out, off = {}, 0
    for n in names:
        size = like[n].size
        rows = -(-size // LANES)
        out[n] = t[off:off + rows].reshape(-1)[:size].reshape(like[n].shape)
        off += rows
    return out


def _direct_phases(per_peer):
    def descriptors(src_ref, out_ref, send_sems, recv_sems, local_sem):
        x, y, c = lax.axis_index("x"), lax.axis_index("y"), lax.axis_index("c")
        me = 4 * x + 2 * y + c
        mine = pltpu.make_async_copy(src_ref.at[me] if per_peer else src_ref, out_ref.at[me], local_sem)
        copies = []
        for k in range(1, N_DEV):
            px = 1 - x if k & 4 else x
            py = 1 - y if k & 2 else y
            pc = 1 - c if k & 1 else c
            copies.append(pltpu.make_async_remote_copy(
                src_ref=src_ref.at[4 * px + 2 * py + pc] if per_peer else src_ref, dst_ref=out_ref.at[me],
                send_sem=send_sems.at[k - 1], recv_sem=recv_sems.at[k - 1],
                device_id=(px, py, pc), device_id_type=pl.DeviceIdType.MESH))
        return mine, copies

    def start(*refs):
        mine, copies = descriptors(*refs)
        mine.start()
        for cp in copies:
            cp.start()

    def forward(*refs):
        pass

    def finish(*refs):
        mine, copies = descriptors(*refs)
        for cp in copies:
            cp.wait_recv()
        for cp in copies:
            cp.wait_send()
        mine.wait()

    return start, forward, finish


def _exchange_parts(srcs, phases):
    n = len(srcs)
    shapes = [jax.ShapeDtypeStruct((N_DEV,) + tuple(s.shape[-2:]), s.dtype) for s in srcs]
    sems = [pltpu.SemaphoreType.DMA((n, N_DEV - 1)), pltpu.SemaphoreType.DMA((n, N_DEV - 1)), pltpu.SemaphoreType.DMA((n,))]

    def lift(phase):
        def run(src_refs, out_refs, send, recv, local):
            for a, (s_ref, o_ref) in enumerate(zip(src_refs, out_refs)):
                phase(s_ref, o_ref, send.at[a], recv.at[a], local.at[a])
        return run

    return shapes, sems, [lift(p) for p in phases]


def _exchange(srcs, phases, *, name):
    shapes, sems, runs = _exchange_parts(srcs, phases)
    n = len(srcs)

    def body(*refs):
        for run in runs:
            run(refs[:n], refs[n:2 * n], *refs[2 * n:])

    anyspace = pl.BlockSpec(memory_space=pl.ANY)
    return _pcall(body, name=name, in_specs=[anyspace] * n, out_specs=[anyspace] * n, out_shape=shapes, scratch_shapes=sems)(*srcs)


def _side(srcs, phases):
    shapes, sems, (start, forward, finish) = _exchange_parts(srcs, phases)

    def before(first, mid, ins, outs, scratch):
        pl.when(first)(lambda: start(ins, outs, *scratch))
        pl.when(mid)(lambda: forward(ins, outs, *scratch))

    def after(last, ins, outs, scratch):
        pl.when(last)(lambda: finish(ins, outs, *scratch))

    return list(srcs), shapes, sems, before, after


def _call_2d(kern, *, name, grid, in_specs, out_specs, out_shape, ins, scratch_shapes=(), semantics, side=None):
    if side is None:
        return _pcall(kern, name=name, grid=grid, in_specs=in_specs, out_specs=out_specs, out_shape=out_shape,
                      scratch_shapes=list(scratch_shapes), compiler_params=_params(*semantics))(*ins)
    s_ins, s_shapes, s_scratch, before, after = side
    n_in, n_out, n_scr = len(ins), len(out_shape), len(scratch_shapes)

    def combined(*refs):
        refs = list(refs)
        cut = [n_in, len(s_ins), n_out, len(s_shapes), n_scr, len(s_scratch)]
        parts, pos = [], 0
        for c in cut:
            parts.append(refs[pos:pos + c])
            pos += c
        m_in, c_in, m_out, c_out, m_scr, c_scr = parts
        ids = [pl.program_id(a) for a in range(len(grid))]
        inner_zero = functools.reduce(jnp.logical_and, [i == 0 for i in ids[1:]], True)
        first = jnp.logical_and(ids[0] == 0, inner_zero)
        mid = jnp.logical_and(ids[0] == grid[0] // 2, inner_zero)
        last = functools.reduce(jnp.logical_and, [i == n - 1 for i, n in zip(ids, grid)])
        before(first, mid, c_in, c_out, c_scr)
        kern(*m_in, *m_out, *m_scr)
        after(last, c_in, c_out, c_scr)

    anyspace = pl.BlockSpec(memory_space=pl.ANY)
    outs = _pcall(combined, name=name, grid=grid, in_specs=list(in_specs) + [anyspace] * len(s_ins),
                  out_specs=list(out_specs) + [anyspace] * len(s_shapes), out_shape=list(out_shape) + s_shapes,
                  scratch_shapes=list(scratch_shapes) + s_scratch, compiler_params=_params(*["arbitrary"] * len(grid)))(*ins, *s_ins)
    return outs[:n_out], outs[n_out:]


def _two_level_phases():
    def parts(src_ref, out_ref, send_sems, recv_sems, local_sem):
        x, y, c = lax.axis_index("x"), lax.axis_index("y"), lax.axis_index("c")
        me, sibling = (x, y, c), (x, y, 1 - c)
        chips = [(1 - x, y), (x, 1 - y), (1 - x, 1 - y)]

        def slab(px, py, pc):
            return out_ref.at[4 * px + 2 * py + pc]

        def copy(k, block, to, from_src=False):
            return pltpu.make_async_remote_copy(
                src_ref=src_ref if from_src else slab(*block), dst_ref=slab(*block),
                send_sem=send_sems.at[k], recv_sem=recv_sems.at[k], device_id=to, device_id_type=pl.DeviceIdType.MESH)

        return dict(
            mine=lambda: pltpu.make_async_copy(src_ref, slab(*me), local_sem),
            first=lambda: [copy(0, me, sibling, True)] + [copy(1 + j, me, (*chip, c), True) for j, chip in enumerate(chips)],
            passed=lambda: [copy(4 + j, (*chip, c), sibling) for j, chip in enumerate(chips)],
            landed=lambda: [copy(1 + j, (*chip, c), me) for j, chip in enumerate(chips)],
            late=lambda: [copy(0, sibling, me)] + [copy(4 + j, (*chip, 1 - c), me) for j, chip in enumerate(chips)])

    def start(*refs):
        make = parts(*refs)
        make['mine']().start()
        for cp in make['first']():
            cp.start()

    def forward(*refs):
        make = parts(*refs)
        for arrived, onward in zip(make['landed'](), make['passed']()):
            arrived.wait_recv()
            onward.start()

    def finish(*refs):
        make = parts(*refs)
        for cp in make['late']():
            cp.wait_recv()
        for cp in make['first']() + make['passed']():
            cp.wait_send()
        make['mine']().wait()

    return start, forward, finish


def _adamw(recv, w, m, v, *, name):
    rows, cols = w.shape
    tr = _pick(rows, (256, 128, 64))

    def kern(r_ref, w_ref, m_ref, v_ref, g_ref, d_ref, mo_ref, vo_ref):
        g = r_ref[0].astype(F32)
        for p in range(1, N_DEV):
            g = g + r_ref[p].astype(F32)
        mn = ADAM_B1 * m_ref[...] + (1.0 - ADAM_B1) * g
        vn = ADAM_B2 * v_ref[...] + (1.0 - ADAM_B2) * (g * g)
        m_hat = mn / (1.0 - ADAM_B1 ** ADAM_STEP)
        v_hat = vn / (1.0 - ADAM_B2 ** ADAM_STEP)
        g_ref[...] = g
        d_ref[...] = -ADAM_LR * (m_hat / (jnp.sqrt(v_hat) + ADAM_EPS) + ADAM_WD * w_ref[...])
        mo_ref[...] = mn
        vo_ref[...] = vn

    row = pl.BlockSpec((tr, cols), lambda i: (i, 0))
    shp = jax.ShapeDtypeStruct((rows, cols), F32)
    return _pcall(kern, name=name, grid=(rows // tr,), in_specs=[pl.BlockSpec((N_DEV, tr, cols), lambda i: (0, i, 0)), row, row, row],
                  out_specs=[row, row, row, row], out_shape=[shp, shp, shp, shp], compiler_params=_params("parallel"))(recv, w, m, v)


INPUTS = ['x', 'mem'] + WEIGHTS + ['loss_target'] + ['m_' + n for n in WEIGHTS] + ['v_' + n for n in WEIGHTS]
SMALL_PAD_ROWS = 4


def kernel(x, mem, ffn1_norm, ffn1_w1, ffn1_w3, ffn1_w2, mix_norm, mem_norm, w_in, w_mem_kv, qn_dsa, kn_dsa, qn_mem, kn_mem, w_branch_sb, w_branch_dsa, w_branch_mem, w_gate, b_gate, w_out, ffn2_norm, ffn2_w1, ffn2_w3, ffn2_w2, loss_target, m_ffn1_norm, m_ffn1_w1, m_ffn1_w3, m_ffn1_w2, m_mix_norm, m_mem_norm, m_w_in, m_w_mem_kv, m_qn_dsa, m_kn_dsa, m_qn_mem, m_kn_mem, m_w_branch_sb, m_w_branch_dsa, m_w_branch_mem, m_w_gate, m_b_gate, m_w_out, m_ffn2_norm, m_ffn2_w1, m_ffn2_w3, m_ffn2_w2, v_ffn1_norm, v_ffn1_w1, v_ffn1_w3, v_ffn1_w2, v_mix_norm, v_mem_norm, v_w_in, v_w_mem_kv, v_qn_dsa, v_kn_dsa, v_qn_mem, v_kn_mem, v_w_branch_sb, v_w_branch_dsa, v_w_branch_mem, v_w_gate, v_b_gate, v_w_out, v_ffn2_norm, v_ffn2_w1, v_ffn2_w3, v_ffn2_w2):
    given = dict(zip(INPUTS, (x, mem, ffn1_norm, ffn1_w1, ffn1_w3, ffn1_w2, mix_norm, mem_norm, w_in, w_mem_kv, qn_dsa, kn_dsa, qn_mem, kn_mem, w_branch_sb, w_branch_dsa, w_branch_mem, w_gate, b_gate, w_out, ffn2_norm, ffn2_w1, ffn2_w3, ffn2_w2, loss_target, m_ffn1_norm, m_ffn1_w1, m_ffn1_w3, m_ffn1_w2, m_mix_norm, m_mem_norm, m_w_in, m_w_mem_kv, m_qn_dsa, m_kn_dsa, m_qn_mem, m_kn_mem, m_w_branch_sb, m_w_branch_dsa, m_w_branch_mem, m_w_gate, m_b_gate, m_w_out, m_ffn2_norm, m_ffn2_w1, m_ffn2_w3, m_ffn2_w2, v_ffn1_norm, v_ffn1_w1, v_ffn1_w3, v_ffn1_w2, v_mix_norm, v_mem_norm, v_w_in, v_w_mem_kv, v_qn_dsa, v_kn_dsa, v_qn_mem, v_kn_mem, v_w_branch_sb, v_w_branch_dsa, v_w_branch_mem, v_w_gate, v_b_gate, v_w_out, v_ffn2_norm, v_ffn2_w1, v_ffn2_w3, v_ffn2_w2), strict=True))
    wl = {n: given[n][0] for n in BIG}
    ws = {n: given[n] for n in SMALL}

    loss, dx, recv, g = _local_step(x[0], mem[0], loss_target[0], wl, ws)

    big = [{}, {}, {}, {}]
    for n in G_FFN2 + G_MID + G_FFN1:
        outs = _adamw(recv[n], wl[n], given['m_' + n][0], given['v_' + n][0], name=f"adamw_{n}")
        for kind, t in enumerate(outs):
            big[kind][n] = t

    gs = _pack_small(g, SMALL, SMALL_PAD_ROWS)
    loss_row = gs.shape[0] - SMALL_PAD_ROWS
    gs = gs.at[loss_row, 0].set(loss)
    recv_s = _exchange([gs], _direct_phases(False), name="gather_small")[0]
    small = _adamw(recv_s, _pack_small(ws, SMALL, SMALL_PAD_ROWS), _pack_small({n: given['m_' + n] for n in SMALL}, SMALL, SMALL_PAD_ROWS),
                   _pack_small({n: given['v_' + n] for n in SMALL}, SMALL, SMALL_PAD_ROWS), name="adamw_replicated")
    total_loss = small[0][loss_row, 0]
    small = [_unpack_small(t, ws, SMALL) for t in small]

    outs = [total_loss, dx[None]]
    for kind in range(4):
        outs += [big[kind][n][None] if n in wl else small[kind][n] for n in WEIGHTS]
    return tuple(outs)
```

```python
import functools

import jax
import jax.numpy as jnp
from jax import lax
from jax.experimental import pallas as pl
from jax.experimental.pallas import tpu as pltpu

F32 = jnp.float32
BF16 = jnp.bfloat16
MXU_DT = jnp.bfloat16

N_DEV = 8
HEAD_DIM = 64
SB_HEADS = 8
DSA_GROUPS = ((128, 1), (512, 4), (2048, 16))
DSA_HPG = 4
MEM_HEADS = 4
SB_W = SB_HEADS * HEAD_DIM
DSA_W = DSA_HPG * len(DSA_GROUPS) * HEAD_DIM
DSA_OUT_W = DSA_HPG * HEAD_DIM
MEM_W = MEM_HEADS * HEAD_DIM
ROPE_THETA = 10000.0
NORM_EPS = 1e-6
QB = 128
SCALE = HEAD_DIM ** -0.5
ADAM_LR, ADAM_B1, ADAM_B2, ADAM_EPS, ADAM_WD, ADAM_STEP = 0.001, 0.9, 0.999, 1e-08, 0.01, 10

LANES = 128
VMEM_LIMIT = 48 * 1024 * 1024
SB_DEAD = -110.0 * 1.4426950408889634

WEIGHTS = ['ffn1_norm', 'ffn1_w1', 'ffn1_w3', 'ffn1_w2', 'mix_norm', 'mem_norm', 'w_in', 'w_mem_kv', 'qn_dsa', 'kn_dsa',
           'qn_mem', 'kn_mem', 'w_branch_sb', 'w_branch_dsa', 'w_branch_mem', 'w_gate', 'b_gate', 'w_out', 'ffn2_norm',
           'ffn2_w1', 'ffn2_w3', 'ffn2_w2']
SHARD_AXIS = {'ffn1_norm': None, 'ffn1_w1': 1, 'ffn1_w3': 1, 'ffn1_w2': 0, 'mix_norm': None, 'mem_norm': None, 'w_in': 1,
              'w_mem_kv': 0, 'qn_dsa': None, 'kn_dsa': None, 'qn_mem': None, 'kn_mem': None, 'w_branch_sb': 1,
              'w_branch_dsa': 1, 'w_branch_mem': 1, 'w_gate': 1, 'b_gate': None, 'w_out': 0, 'ffn2_norm': None,
              'ffn2_w1': 1, 'ffn2_w3': 1, 'ffn2_w2': 0}
BIG = [n for n in WEIGHTS if SHARD_AXIS[n] is not None]
SMALL = [n for n in WEIGHTS if SHARD_AXIS[n] is None]


def _pcall(kern, **kw):
    return pl.pallas_call(kern, **kw)


def _params(*sem):
    return pltpu.CompilerParams(dimension_semantics=sem, vmem_limit_bytes=VMEM_LIMIT)


def _dot(a, b, dims):
    return lax.dot_general(a.astype(MXU_DT), b.astype(MXU_DT), (dims, ((), ())), preferred_element_type=F32)


def _nn(a, b):
    return _dot(a, b, ((1,), (0,)))


def _nt(a, b):
    return _dot(a, b, ((1,), (1,)))


def _tn(a, b):
    return _dot(a, b, ((0,), (0,)))


def _pick(n, prefs):
    for p in prefs:
        if n % p == 0:
            return p
    return n


def _matmul(a, b, *, name, ta=False, tb=False, out_dtype=F32, res=None, alpha=1.0, tm=1024, tn=512, tk=1024, pair2=None,
            epilogue=None, side=None):
    if ta:
        kdim, m = a.shape
    else:
        m, kdim = a.shape
    n = b.shape[0] if tb else b.shape[1]
    tm = _pick(m, (tm, 512, 256, 128))
    tn = _pick(n, (tn, 512, 384, 256, 128))
    tk = _pick(kdim, (tk, 1024, 512, 256, 128))
    nk = kdim // tk
    a_spec = pl.BlockSpec((tk, tm), lambda i, j, k: (k, i)) if ta else pl.BlockSpec((tm, tk), lambda i, j, k: (i, k))
    b_spec = pl.BlockSpec((tn, tk), lambda i, j, k: (j, k)) if tb else pl.BlockSpec((tk, tn), lambda i, j, k: (k, j))
    o_spec = pl.BlockSpec((tm, tn), lambda i, j, k: (i, j))
    v_spec = pl.BlockSpec((1, tn), lambda i, j, k: (0, j))
    dims = ((0 if ta else 1,), (1 if tb else 0,))
    n_mm = 2 if pair2 is None else 4
    if epilogue is None:
        row_ins, vec_ins = ([] if res is None else [res]), []
        out_dtypes, n_vec = [out_dtype], 0
    else:
        assert tn == n and res is None
        epi_fn, row_ins, vec_ins, out_dtypes, n_vec = epilogue
    n_row_out = len(out_dtypes)

    def kern(*refs):
        refs = list(refs)
        acc_ref = refs.pop() if nk > 1 else None
        mm = refs[:n_mm]
        extra = refs[n_mm:n_mm + len(row_ins) + len(vec_ins)]
        outs = refs[n_mm + len(extra):]
        i = pl.program_id(0)
        k = pl.program_id(2)

        def product():
            part = _dot(mm[0][...], mm[1][...], dims)
            if pair2 is not None:
                part = part + _dot(mm[2][...], mm[3][...], dims)
            return part

        def finish(r):
            if alpha != 1.0:
                r = r * alpha
            if epilogue is None:
                if extra:
                    r = extra[0][...] + r
                outs[0][...] = r.astype(out_dtype)
                return
            vals = epi_fn(r, *[e[...] for e in extra])
            for o_ref, v in zip(outs[:n_row_out], vals[:n_row_out]):
                o_ref[...] = v.astype(o_ref.dtype)
            for o_ref, v in zip(outs[n_row_out:], vals[n_row_out:]):
                @pl.when(i == 0)
                def _():
                    o_ref[...] = jnp.zeros_like(o_ref)

                o_ref[...] += v

        if nk == 1:
            finish(product())
            return

        @pl.when(k == 0)
        def _():
            acc_ref[...] = jnp.zeros_like(acc_ref)

        acc_ref[...] += product()

        @pl.when(k == nk - 1)
        def _():
            finish(acc_ref[...])

    ins = [a, b] + ([] if pair2 is None else list(pair2)) + list(row_ins) + list(vec_ins)
    specs = [a_spec, b_spec] * (n_mm // 2) + [o_spec] * len(row_ins) + [v_spec] * len(vec_ins)
    out_specs = [o_spec] * n_row_out + [v_spec] * n_vec
    out_shape = [jax.ShapeDtypeStruct((m, n), dt) for dt in out_dtypes] + [jax.ShapeDtypeStruct((1, n), F32)] * n_vec
    outs = _call_2d(kern, name=name, grid=(m // tm, n // tn, nk), in_specs=specs, out_specs=out_specs, out_shape=out_shape,
                    ins=ins, scratch_shapes=[pltpu.VMEM((tm, tn), F32)] if nk > 1 else [],
                    semantics=("arbitrary" if n_vec else "parallel", "parallel", "arbitrary"), side=side)
    carried = None
    if side is not None:
        outs, carried = outs
    outs = outs[0] if epilogue is None else outs
    return outs if side is None else (outs, carried)


def _epi_residual_rms(r, res, gain):
    xn = res + r
    return xn, xn * lax.rsqrt(jnp.mean(xn * xn, axis=-1, keepdims=True) + NORM_EPS) * gain


def _epi_rms_bwd(r, x, dres, gain):
    rs = lax.rsqrt(jnp.mean(x * x, axis=-1, keepdims=True) + NORM_EPS)
    xh = x * rs
    dy = r * gain
    dx = dres + rs * (dy - xh * jnp.mean(dy * xh, axis=-1, keepdims=True))
    return dx, dx, jnp.sum(r * xh, axis=0, keepdims=True)


def _epi_rms_bwd_sum(r, r0, x, dres, gain):
    return _epi_rms_bwd(r + r0, x, dres, gain)


def _epi_loss(r, res, target):
    e = (res + r) - target
    dy = e / e.shape[-1]
    return dy, dy, jnp.sum(e * e, axis=0, keepdims=True)
def _rms_fwd(x, g, *, name, side=None):
    s, d = x.shape
    ts = _pick(s, (512, 256))

    def kern(x_ref, g_ref, h_ref):
        xf = x_ref[...]
        r = lax.rsqrt(jnp.mean(xf * xf, axis=-1, keepdims=True) + NORM_EPS)
        h_ref[...] = (xf * r * g_ref[...]).astype(h_ref.dtype)

    outs = _call_2d(kern, name=name, grid=(s // ts,),
                    in_specs=[pl.BlockSpec((ts, d), lambda i: (i, 0)), pl.BlockSpec((1, d), lambda i: (0, 0))],
                    out_specs=[pl.BlockSpec((ts, d), lambda i: (i, 0))], out_shape=[jax.ShapeDtypeStruct((s, d), BF16)],
                    ins=[x, g], semantics=("parallel",), side=side)
    return outs[0] if side is None else (outs[0][0], outs[1])


def _rms_bwd(x, g, dh, res, *, name):
    s, d = x.shape
    ts = _pick(s, (512, 256))

    def kern(*refs):
        if res is None:
            x_ref, g_ref, dh_ref, dx_ref, dxb_ref, dg_ref = refs
            r_ref = None
        else:
            x_ref, g_ref, dh_ref, r_ref, dx_ref, dxb_ref, dg_ref = refs
        xf = x_ref[...]
        r = lax.rsqrt(jnp.mean(xf * xf, axis=-1, keepdims=True) + NORM_EPS)
        xh = xf * r
        dhf = dh_ref[...].astype(F32)
        dy = dhf * g_ref[...]
        dx = r * (dy - xh * jnp.mean(dy * xh, axis=-1, keepdims=True))
        if r_ref is not None:
            dx = r_ref[...] + dx
        dx_ref[...] = dx
        dxb_ref[...] = dx.astype(dxb_ref.dtype)

        @pl.when(pl.program_id(0) == 0)
        def _():
            dg_ref[...] = jnp.zeros_like(dg_ref)

        dg_ref[...] += jnp.sum(dhf * xh, axis=0, keepdims=True)

    row = pl.BlockSpec((ts, d), lambda i: (i, 0))
    vec = pl.BlockSpec((1, d), lambda i: (0, 0))
    ins = [x, g, dh] + ([] if res is None else [res])
    return _pcall(kern, name=name, grid=(s // ts,), in_specs=[row, vec, row] + ([] if res is None else [row]),
                  out_specs=[row, row, vec],
                  out_shape=[jax.ShapeDtypeStruct((s, d), F32), jax.ShapeDtypeStruct((s, d), BF16), jax.ShapeDtypeStruct((1, d), F32)],
                  compiler_params=_params("arbitrary"))(*ins)


def _sigmoid(x):
    return 1.0 / (1.0 + jnp.exp(-x))


FFN_TM, FFN_TF = 512, 1408


def _ffn_up(h, w1, w3, *, name, side=None):
    s, d = h.shape
    fdim = w1.shape[1]
    tm, tf = _pick(s, (FFN_TM, 256)), _pick(fdim, (FFN_TF, 512, 256, 128))

    def kern(h_ref, w1_ref, w3_ref, a_ref, b_ref, f_ref):
        hb = h_ref[...]
        a = _nn(hb, w1_ref[...])
        b = _nn(hb, w3_ref[...])
        a_ref[...] = a.astype(a_ref.dtype)
        b_ref[...] = b.astype(b_ref.dtype)
        f_ref[...] = (a * _sigmoid(a) * b).astype(f_ref.dtype)

    wspec = pl.BlockSpec((d, tf), lambda i, j: (0, j))
    ospec = pl.BlockSpec((tm, tf), lambda i, j: (i, j))
    shp = jax.ShapeDtypeStruct((s, fdim), BF16)
    return _call_2d(kern, name=name, grid=(s // tm, fdim // tf), in_specs=[pl.BlockSpec((tm, d), lambda i, j: (i, 0)), wspec, wspec],
                    out_specs=[ospec, ospec, ospec], out_shape=[shp, shp, shp], ins=[h, w1, w3],
                    semantics=("parallel", "parallel"), side=side)


def _ffn_dact(dy, w2, a, b, *, name, side=None):
    s, d = dy.shape
    fdim = w2.shape[0]
    tm, tf = _pick(s, (FFN_TM, 256)), _pick(fdim, (FFN_TF, 512, 256, 128))

    half = (tf // LANES + 1) // 2 * LANES

    def kern(dy_ref, w2_ref, a_ref, b_ref, da_ref, db_ref):
        dyb = dy_ref[...]
        pieces = ((0, half), (half, tf))
        dfs = [_nt(dyb, w2_ref[lo:hi, :]) * 0.5 for lo, hi in pieces]
        for (lo, hi), df in zip(pieces, dfs):
            av = a_ref[:, lo:hi].astype(F32)
            sg = _sigmoid(av)
            da_ref[:, lo:hi] = (df * b_ref[:, lo:hi].astype(F32) * (sg + av * sg * (1.0 - sg))).astype(da_ref.dtype)
            db_ref[:, lo:hi] = (df * (av * sg)).astype(db_ref.dtype)

    ospec = pl.BlockSpec((tm, tf), lambda i, j: (i, j))
    shp = jax.ShapeDtypeStruct((s, fdim), BF16)
    return _call_2d(kern, name=name, grid=(s // tm, fdim // tf),
                    in_specs=[pl.BlockSpec((tm, d), lambda i, j: (i, 0)), pl.BlockSpec((tf, d), lambda i, j: (j, 0)), ospec, ospec],
                    out_specs=[ospec, ospec], out_shape=[shp, shp], ins=[dy, w2, a, b], semantics=("parallel", "parallel"), side=side)


def _head_mean(v, bd):
    outs = []
    for c in range(v.shape[1] // LANES):
        x = v[:, c * LANES:(c + 1) * LANES]
        hi = x.astype(BF16)
        lo = (x - hi.astype(F32)).astype(BF16)
        outs.append(lax.dot_general(jnp.concatenate([hi, lo], axis=1), bd, (((1,), (0,)), ((), ())), preferred_element_type=F32))
    return outs[0] if len(outs) == 1 else jnp.concatenate(outs, axis=1)


def _partner(v):
    w = v.shape[1]
    lane = lax.broadcasted_iota(jnp.int32, v.shape, 1)
    return jnp.where(lane % HEAD_DIM < HEAD_DIM // 2, pltpu.roll(v, w - HEAD_DIM // 2, 1), pltpu.roll(v, HEAD_DIM // 2, 1))


def _block_diag(w=None):
    r = (lax.broadcasted_iota(jnp.int32, (2 * LANES, LANES), 0) % LANES) // HEAD_DIM
    c = lax.broadcasted_iota(jnp.int32, (2 * LANES, LANES), 1) // HEAD_DIM
    return jnp.where(r == c, 1.0 / HEAD_DIM, 0.0).astype(BF16)


def _rope_tables(s):
    half = HEAD_DIM // 2
    inv_freq = jnp.power(ROPE_THETA, -jnp.arange(half, dtype=F32) / half)
    ang = jnp.arange(s).astype(F32)[:, None] * inv_freq[None, :]
    cos, sin = jnp.cos(ang), jnp.sin(ang)
    cos2 = jnp.concatenate([cos, cos, cos, cos], axis=1)
    sin2 = jnp.concatenate([-sin, sin, -sin, sin], axis=1)
    return cos2, sin2


def _qknorm_fwd(src, col0, width, gain, rope, *, name, out_dtype=BF16):
    s = src.shape[0]
    ts = _pick(s, (512, 256))
    cb = col0 // width
    assert col0 % width == 0
    reps = width // LANES
    g = jnp.tile(gain, (1, width // HEAD_DIM))

    def kern(*refs):
        if rope is None:
            x_ref, g_ref, o_ref = refs
        else:
            x_ref, g_ref, c_ref, s_ref, o_ref = refs
        x = x_ref[...].astype(F32)
        bd = _block_diag(width)
        r = lax.rsqrt(_head_mean(x * x, bd) + NORM_EPS)
        y = x * r * g_ref[...]
        if rope is not None:
            y = y * jnp.tile(c_ref[...], (1, reps)) + _partner(y) * jnp.tile(s_ref[...], (1, reps))
        o_ref[...] = y.astype(o_ref.dtype)

    xs = pl.BlockSpec((ts, width), lambda i: (i, cb))
    tab = pl.BlockSpec((ts, LANES), lambda i: (i, 0))
    ins = [src, g] + ([] if rope is None else list(rope))
    specs = [xs, pl.BlockSpec((1, width), lambda i: (0, 0))] + ([] if rope is None else [tab, tab])
    return _pcall(kern, name=name, grid=(s // ts,), in_specs=specs, out_specs=pl.BlockSpec((ts, width), lambda i: (i, 0)),
                  out_shape=jax.ShapeDtypeStruct((s, width), out_dtype), compiler_params=_params("parallel"))(*ins)


def _qknorm_bwd(src, col0, width, gain, rope, dout, *, name):
    s = src.shape[0]
    ts = _pick(s, (512, 256))
    cb = col0 // width
    reps = width // LANES
    g = jnp.tile(gain, (1, width // HEAD_DIM))

    douts = list(dout) if isinstance(dout, (list, tuple)) else [dout]
    piece = width // len(douts)

    def kern(*refs):
        refs = list(refs)
        dg_ref = refs.pop()
        dx_ref = refs.pop()
        do_refs = [refs.pop() for _ in douts][::-1]
        if rope is None:
            x_ref, g_ref = refs
        else:
            x_ref, g_ref, c_ref, s_ref = refs
        x = x_ref[...].astype(F32)
        bd = _block_diag(width)
        r = lax.rsqrt(_head_mean(x * x, bd) + NORM_EPS)
        xh = x * r
        dy = jnp.concatenate([d[...].astype(F32) for d in do_refs], axis=1) if len(do_refs) > 1 else do_refs[0][...].astype(F32)
        if rope is not None:
            dy = dy * jnp.tile(c_ref[...], (1, reps)) + _partner(dy * jnp.tile(s_ref[...], (1, reps)))
        dxh = dy * g_ref[...]
        dx_ref[...] = (r * (dxh - xh * _head_mean(dxh * xh, bd))).astype(dx_ref.dtype)

        @pl.when(pl.program_id(0) == 0)
        def _():
            dg_ref[...] = jnp.zeros_like(dg_ref)

        dg_ref[...] += jnp.sum(dy * xh, axis=0, keepdims=True)

    xs = pl.BlockSpec((ts, width), lambda i: (i, cb))
    row = pl.BlockSpec((ts, width), lambda i: (i, 0))
    vec = pl.BlockSpec((1, width), lambda i: (0, 0))
    tab = pl.BlockSpec((ts, LANES), lambda i: (i, 0))
    ins = [src, g] + ([] if rope is None else list(rope)) + douts
    specs = [xs, vec] + ([] if rope is None else [tab, tab]) + [pl.BlockSpec((ts, piece), lambda i: (i, 0))] * len(douts)
    dx, dg = _pcall(kern, name=name, grid=(s // ts,), in_specs=specs, out_specs=[row, vec],
                    out_shape=[jax.ShapeDtypeStruct((s, width), BF16), jax.ShapeDtypeStruct((1, width), F32)],
                    compiler_params=_params("arbitrary"))(*ins)
    return dx, jnp.sum(dg.reshape(width // HEAD_DIM, HEAD_DIM), axis=0, keepdims=True)


def _tri(strict, n):
    r = lax.broadcasted_iota(jnp.int32, (2 * n, n), 0) % n
    c = lax.broadcasted_iota(jnp.int32, (2 * n, n), 1)
    return jnp.where((r > c) if strict else (r >= c), 1.0, 0.0).astype(BF16)


def _split_dot(v, t2):
    hi = v.astype(BF16)
    lo = (v - hi.astype(F32)).astype(BF16)
    return lax.dot_general(jnp.concatenate([hi, lo], axis=1), t2, (((1,), (0,)), ((), ())), preferred_element_type=F32)


LOG2E = 1.4426950408889634


def _log2_sigmoids(z2):
    lf = -(jnp.maximum(z2, 0.0) + jnp.log2(1.0 + jnp.exp2(-jnp.abs(z2))))
    return z2 + lf, lf


SB2_SUB = 2
SB_KT = 128


def _first_half(shape):
    return lax.broadcasted_iota(jnp.int32, shape, 1) < HEAD_DIM


def _split_pair(t, first):
    zero = jnp.zeros_like(t)
    return [jnp.where(first, t, zero), jnp.where(first, zero, t)]


def _sb2_fwd(p, *, name, side=None):
    s = p.shape[0]
    rq = SB2_SUB * QB
    nq = s // rq
    npair = SB_W // LANES

    def kern(q_ref, k_ref, v_ref, o_ref):
        i = pl.program_id(1)
        first = _first_half((rq, LANES))
        q2 = jnp.concatenate(_split_pair(q_ref[...], first), axis=0)
        t2 = _tri(True, SB_KT)
        rel = lax.broadcasted_iota(jnp.int32, (2 * rq, SB_KT), 1) - lax.broadcasted_iota(jnp.int32, (2 * rq, SB_KT), 0) % rq

        def tile(j, q, rel, carry, acc, masked):
            off = pl.multiple_of(j * SB_KT, SB_KT)
            ls, lf = _log2_sigmoids(_nt(q, k_ref[pl.ds(off, SB_KT), :]) * (SCALE * LOG2E))
            if masked:
                before = rel < i * rq - j * SB_KT
                lf = jnp.where(before, lf, 0.0)
            w = jnp.exp2(ls + _split_dot(lf, t2) + carry)
            if masked:
                w = jnp.where(before, w, 0.0)
            return carry + jnp.sum(lf, axis=1, keepdims=True), acc + _nn(w, v_ref[pl.ds(off, SB_KT), :])

        carry, acc = jnp.zeros((2 * rq, 1), F32), jnp.zeros((2 * rq, LANES), F32)
        for a in range(rq // SB_KT):
            carry, acc = tile(i * (rq // SB_KT) + (rq // SB_KT - 1 - a), q2, rel, carry, acc, True)

        def cond(st):
            return jnp.logical_and(st[0] >= 0, st[1] > 0)

        def body(st):
            carry, acc = tile(st[0], q2, rel, st[2], st[3], False)
            return st[0] - 1, (jnp.max(carry) > SB_DEAD).astype(jnp.int32), carry, acc

        st = lax.while_loop(cond, body, (i * (rq // SB_KT) - 1, jnp.int32(1), carry, acc))
        o_ref[...] = jnp.where(first, st[3][:rq], st[3][rq:])

    outs = _call_2d(kern, name=name, grid=(npair, nq),
                    in_specs=[pl.BlockSpec((rq, LANES), lambda a, i: (i, a)), pl.BlockSpec((s, LANES), lambda a, i: (0, npair + a)),
                              pl.BlockSpec((s, LANES), lambda a, i: (0, 2 * npair + a))],
                    out_specs=[pl.BlockSpec((rq, LANES), lambda a, i: (i, a))], out_shape=[jax.ShapeDtypeStruct((s, SB_W), F32)],
                    ins=[p, p, p], semantics=("parallel", "arbitrary"), side=side)
    return outs[0] if side is None else (outs[0][0], outs[1])


def _sb2_bwd(p, o, do, *, name, side=None):
    s = p.shape[0]
    rq = SB2_SUB * QB
    nq = s // rq
    npair = SB_W // LANES

    def kern(q_ref, k_ref, v_ref, o_ref, do_ref, dq_ref, dk_hbm, dv_hbm, dk_acc, dv_acc, sem):
        pr = pl.program_id(0)
        i = pl.program_id(1)

        @pl.when(i == 0)
        def _():
            dk_acc[...] = jnp.zeros_like(dk_acc)
            dv_acc[...] = jnp.zeros_like(dv_acc)

        first = _first_half((rq, LANES))
        q2 = jnp.concatenate(_split_pair(q_ref[...], first), axis=0)
        do2 = jnp.concatenate(_split_pair(do_ref[...], first), axis=0)
        o2 = o_ref[...]
        dsum = jnp.sum(do2.astype(F32) * jnp.concatenate([o2, o2], axis=0), axis=1, keepdims=True)
        t_strict = _tri(True, SB_KT)
        t_incl = _tri(False, SB_KT)
        rel = lax.broadcasted_iota(jnp.int32, (2 * rq, SB_KT), 1) - lax.broadcasted_iota(jnp.int32, (2 * rq, SB_KT), 0) % rq

        def tile(j, rows, carry, gcarry, dq, masked):
            q, dob, dsm, rel = rows
            off = pl.multiple_of(j * SB_KT, SB_KT)
            kt = k_ref[pl.ds(off, SB_KT), :]
            ls, lf = _log2_sigmoids(_nt(q, kt) * (SCALE * LOG2E))
            if masked:
                before = rel < i * rq - j * SB_KT
                lf = jnp.where(before, lf, 0.0)
            w = jnp.exp2(ls + _split_dot(lf, t_strict) + carry)
            if masked:
                w = jnp.where(before, w, 0.0)
            wr = w.astype(MXU_DT)
            g = _nt(dob, v_ref[pl.ds(off, SB_KT), :]) * wr.astype(F32)
            big_g = dsm - (_split_dot(g, t_incl) + gcarry)
            sig = jnp.exp2(ls)
            dz = g * (1.0 - sig) - sig * big_g
            if masked:
                dz = jnp.where(before, dz, 0.0)
            dz = dz * SCALE
            dk_acc[pl.ds(off, SB_KT), :] += _tn(dz, q)
            dv_acc[pl.ds(off, SB_KT), :] += _tn(wr, dob)
            return (carry + jnp.sum(lf, axis=1, keepdims=True), gcarry + jnp.sum(g, axis=1, keepdims=True),
                    dq + _nn(dz, kt))

        zc = jnp.zeros((2 * rq, 1), F32)
        carry, gcarry, dq = zc, zc, jnp.zeros((2 * rq, LANES), F32)
        whole = (q2, do2, dsum, rel)
        for a in range(rq // SB_KT):
            carry, gcarry, dq = tile(i * (rq // SB_KT) + (rq // SB_KT - 1 - a), whole, carry, gcarry, dq, True)

        def cond(st):
            return jnp.logical_and(st[0] >= 0, st[1] > 0)

        def body(st):
            carry, gcarry, dq = tile(st[0], whole, st[2], st[3], st[4], False)
            return st[0] - 1, (jnp.max(carry) > SB_DEAD).astype(jnp.int32), carry, gcarry, dq

        st = lax.while_loop(cond, body, (i * (rq // SB_KT) - 1, jnp.int32(1), carry, gcarry, dq))
        dq_ref[...] = jnp.where(first, st[4][:rq], st[4][rq:])

        @pl.when(i == nq - 1)
        def _():
            cols = pl.ds(pl.multiple_of(pr * LANES, LANES), LANES)
            ck = pltpu.make_async_copy(dk_acc, dk_hbm.at[:, cols], sem.at[0])
            cv = pltpu.make_async_copy(dv_acc, dv_hbm.at[:, cols], sem.at[1])
            ck.start()
            cv.start()
            ck.wait()
            cv.wait()

    blk = pl.BlockSpec((rq, LANES), lambda a, i: (i, a))
    anyspace = pl.BlockSpec(memory_space=pl.ANY)
    shp = jax.ShapeDtypeStruct((s, SB_W), F32)
    return _call_2d(kern, name=name, grid=(npair, nq),
                    in_specs=[blk, pl.BlockSpec((s, LANES), lambda a, i: (0, npair + a)),
                              pl.BlockSpec((s, LANES), lambda a, i: (0, 2 * npair + a)), blk, blk],
                    out_specs=[blk, anyspace, anyspace], out_shape=[shp, shp, shp], ins=[p, p, p, o, do],
                    scratch_shapes=[pltpu.VMEM((s, LANES), F32), pltpu.VMEM((s, LANES), F32), pltpu.SemaphoreType.DMA((2,))],
                    semantics=("arbitrary", "arbitrary"), side=side)


def _dsa_rel():
    qi = lax.broadcasted_iota(jnp.int32, (QB, QB), 0)
    kj = lax.broadcasted_iota(jnp.int32, (QB, QB), 1)
    return kj - qi


def _prev_mask(rel, has_prev):
    return rel >= jnp.where(has_prev, 0, QB)


DSA_BT = QB * max(r for _, r in DSA_GROUPS)
DSA_UB = 4


def _bdot(a, b, ca, cb):
    return lax.dot_general(a.astype(MXU_DT), b.astype(MXU_DT), (((ca,), (cb,)), ((0,), (0,))), preferred_element_type=F32)


def _bnt(a, b):
    return _bdot(a, b, 2, 2)


def _bnn(a, b):
    return _bdot(a, b, 2, 1)


def _btn(a, b):
    return _bdot(a, b, 1, 1)


def _unit_rows(r, c, b):
    return pl.ds(c + QB * r * b, QB, stride=r)


def _pair_cols(t, first):
    return [jnp.max(jnp.where(first, t, -jnp.inf), axis=1, keepdims=True),
            jnp.max(jnp.where(first, -jnp.inf, t), axis=1, keepdims=True)]


def _dsa2_fwd(qn, kn, v32, g, *, name):
    s = qn.shape[0]
    r = DSA_GROUPS[g][1]
    nbk = DSA_BT // (QB * r)
    npair = DSA_OUT_W // LANES

    def kern(q_ref, k_ref, kp_ref, v_ref, vp_ref, o_ref, l_ref):
        t = pl.program_id(1)
        first = _first_half((QB, LANES))
        rel = _dsa_rel()
        units = [(c, b) for c in range(r) for b in range(nbk)]
        for u0 in range(0, len(units), DSA_UB):
            batch = units[u0:u0 + DSA_UB]
            qs, kcs, vcs, kps, vps, masks = [], [], [], [], [], []
            for c, b in batch:
                rows = _unit_rows(r, c, b)
                kc, vc = k_ref[rows, :].astype(MXU_DT), v_ref[rows, :].astype(MXU_DT)
                if b > 0:
                    prow = _unit_rows(r, c, b - 1)
                    kpv, vpv, has_prev = k_ref[prow, :], v_ref[prow, :], True
                else:
                    prow = _unit_rows(r, c, nbk - 1)
                    kpv, vpv, has_prev = kp_ref[prow, :], vp_ref[prow, :], t > 0
                for qe in _split_pair(q_ref[rows, :], first):
                    qs.append(qe.astype(MXU_DT))
                    kcs.append(kc)
                    vcs.append(vc)
                    kps.append(kpv.astype(MXU_DT))
                    vps.append(vpv.astype(MXU_DT))
                    masks.append(_prev_mask(rel, has_prev))
            qq = jnp.stack(qs)
            sc = jnp.where(rel <= 0, _bnt(qq, jnp.stack(kcs)) * SCALE, -jnp.inf)
            sp = _bnt(qq, jnp.stack(kps)) * SCALE
            sp = jnp.stack([jnp.where(mk, sp[n], -jnp.inf) for n, mk in enumerate(masks)])
            m = jnp.maximum(jnp.max(sc, axis=2, keepdims=True), jnp.max(sp, axis=2, keepdims=True))
            pc = jnp.exp(sc - m)
            pp = jnp.exp(sp - m)
            den = jnp.sum(pc, axis=2, keepdims=True) + jnp.sum(pp, axis=2, keepdims=True)
            out = (_bnn(pc, jnp.stack(vcs)) + _bnn(pp, jnp.stack(vps))) / den
            lse = m + jnp.log(den)
            for idx, (c, b) in enumerate(batch):
                rows = _unit_rows(r, c, b)
                o_ref[rows, :] = jnp.where(first, out[2 * idx], out[2 * idx + 1])
                l_ref[rows, :] = jnp.where(first, lse[2 * idx], lse[2 * idx + 1])

    npg = DSA_HPG * HEAD_DIM // LANES
    cur = pl.BlockSpec((DSA_BT, LANES), lambda a, t: (t, npg * g + a))
    prev = pl.BlockSpec((DSA_BT, LANES), lambda a, t: (jnp.maximum(t - 1, 0), npg * g + a))
    out = pl.BlockSpec((DSA_BT, LANES), lambda a, t: (t, a))
    shp = jax.ShapeDtypeStruct((s, DSA_OUT_W), F32)
    return _pcall(kern, name=name, grid=(npair, s // DSA_BT), in_specs=[cur, cur, prev, cur, prev], out_specs=[out, out],
                  out_shape=[shp, shp], compiler_params=_params("parallel", "parallel"))(qn, kn, kn, v32, v32)


def _dsa2_combine(parts, *, name):
    s, wd = parts[0][0].shape
    ts = _pick(s, (512, 256))

    def kern(o0, l0, o1, l1, o2, l2, o_ref, l_ref):
        ls = [l0[...], l1[...], l2[...]]
        m = jnp.maximum(jnp.maximum(ls[0], ls[1]), ls[2])
        es = [jnp.exp(l - m) for l in ls]
        den = es[0] + es[1] + es[2]
        o_ref[...] = (es[0] * o0[...] + es[1] * o1[...] + es[2] * o2[...]) / den
        l_ref[...] = m + jnp.log(den)

    blk = pl.BlockSpec((ts, wd), lambda i: (i, 0))
    shp = jax.ShapeDtypeStruct((s, wd), F32)
    flat = [t for pair in parts for t in pair]
    return _pcall(kern, name=name, grid=(s // ts,), in_specs=[blk] * 6, out_specs=[blk, blk], out_shape=[shp, shp],
                  compiler_params=_params("parallel"))(*flat)


def _dsa2_prep(o, do, *, name):
    s, wd = o.shape
    ts = _pick(s, (512, 256))

    def kern(o_ref, do_ref, d_ref):
        d_ref[...] = _head_mean(do_ref[...] * o_ref[...], _block_diag(wd)) * HEAD_DIM

    blk = pl.BlockSpec((ts, wd), lambda i: (i, 0))
    return _pcall(kern, name=name, grid=(s // ts,), in_specs=[blk, blk], out_specs=blk,
                  out_shape=jax.ShapeDtypeStruct((s, wd), F32), compiler_params=_params("parallel"))(o, do)


def _dsa2_bwd(qn, kn, v32, do, lse, dd, g, *, name):
    s = qn.shape[0]
    r = DSA_GROUPS[g][1]
    nbk = DSA_BT // (QB * r)
    npair = DSA_OUT_W // LANES
    nsteps = s // DSA_BT

    def kern(q_ref, qn_ref, k_ref, kp_ref, v_ref, vp_ref, do_ref, don_ref, l_ref, ln_ref, d_ref, dn_ref,
             dq_ref, dk_ref, dv_ref):
        t = pl.program_id(1)
        first = _first_half((QB, LANES))
        rel = _dsa_rel()

        def pairs(items):
            qq = jnp.stack([it[0].astype(MXU_DT) for it in items])
            dd = jnp.stack([it[1].astype(MXU_DT) for it in items])
            kk = jnp.stack([it[4].astype(MXU_DT) for it in items])
            vv = jnp.stack([it[5].astype(MXU_DT) for it in items])
            p = jnp.exp(_bnt(qq, kk) * SCALE - jnp.stack([it[2] for it in items]))
            p = jnp.stack([jnp.where(it[6], p[n], 0.0) for n, it in enumerate(items)])
            ds = p * (_bnt(dd, vv) - jnp.stack([it[3] for it in items])) * SCALE
            return _bnn(ds, kk), _btn(ds, qq), _btn(p, dd)

        def heads(rows, qr, dor, lr, dr):
            return list(zip(_split_pair(qr[rows, :], first), _split_pair(dor[rows, :], first),
                            _pair_cols(lr[rows, :], first), _pair_cols(dr[rows, :], first)))

        units = [(c, b) for c in range(r) for b in range(nbk)]
        dk_of, dv_of = [None] * len(units), [None] * len(units)
        for u0 in range(0, len(units), DSA_UB // 2):
            batch = list(enumerate(units))[u0:u0 + DSA_UB // 2]
            items = []
            for u, (c, b) in batch:
                rows = _unit_rows(r, c, b)
                kc, vc = k_ref[rows, :], v_ref[rows, :]
                if b > 0:
                    prow = _unit_rows(r, c, b - 1)
                    kpv, vpv, pmask = k_ref[prow, :], v_ref[prow, :], _prev_mask(rel, True)
                else:
                    prow = _unit_rows(r, c, nbk - 1)
                    kpv, vpv, pmask = kp_ref[prow, :], vp_ref[prow, :], _prev_mask(rel, t > 0)
                for hd in heads(rows, q_ref, do_ref, l_ref, d_ref):
                    items.append(hd + (kc, vc, rel <= 0))
                    items.append(hd + (kpv, vpv, pmask))
            dq, dk, dv = pairs(items)
            for n, (u, (c, b)) in enumerate(batch):
                dq_ref[_unit_rows(r, c, b), :] = jnp.where(first, dq[4 * n] + dq[4 * n + 1], dq[4 * n + 2] + dq[4 * n + 3])
                dk_of[u] = dk[4 * n] + dk[4 * n + 2]
                dv_of[u] = dv[4 * n] + dv[4 * n + 2]
                if b > 0:
                    dk_of[u - 1] = dk_of[u - 1] + (dk[4 * n + 1] + dk[4 * n + 3])
                    dv_of[u - 1] = dv_of[u - 1] + (dv[4 * n + 1] + dv[4 * n + 3])
        lasts = [c * nbk + nbk - 1 for c in range(r)]
        for c0 in range(0, r, DSA_UB):
            chunk = list(range(c0, min(c0 + DSA_UB, r)))
            items = []
            for c in chunk:
                last = _unit_rows(r, c, nbk - 1)
                for hd in heads(_unit_rows(r, c, 0), qn_ref, don_ref, ln_ref, dn_ref):
                    items.append(hd + (k_ref[last, :], v_ref[last, :], _prev_mask(rel, t < nsteps - 1)))
            _, dk, dv = pairs(items)
            for n, c in enumerate(chunk):
                dk_of[lasts[c]] = dk_of[lasts[c]] + (dk[2 * n] + dk[2 * n + 1])
                dv_of[lasts[c]] = dv_of[lasts[c]] + (dv[2 * n] + dv[2 * n + 1])
        for u, (c, b) in enumerate(units):
            dk_ref[_unit_rows(r, c, b), :] = dk_of[u]
            dv_ref[_unit_rows(r, c, b), :] = dv_of[u]

    npg = DSA_HPG * HEAD_DIM // LANES

    def at(shift, col):
        return pl.BlockSpec((DSA_BT, LANES), lambda a, t: (jnp.clip(t + shift, 0, nsteps - 1), col(a)))

    gcol = lambda a: npg * g + a
    ocol = lambda a: a
    specs = [at(0, gcol), at(1, gcol), at(0, gcol), at(-1, gcol), at(0, gcol), at(-1, gcol),
             at(0, ocol), at(1, ocol), at(0, ocol), at(1, ocol), at(0, ocol), at(1, ocol)]
    shp = jax.ShapeDtypeStruct((s, DSA_OUT_W), F32)
    return _pcall(kern, name=name, grid=(npair, nsteps), in_specs=specs, out_specs=[at(0, ocol)] * 3, out_shape=[shp, shp, shp],
                  compiler_params=_params("parallel", "parallel"))(qn, qn, kn, kn, v32, v32, do, do, lse, lse, dd, dd)


def _mem2_fwd(qn, km, kv, *, name):
    s = qn.shape[0]
    ml = km.shape[0]
    tq = _pick(s, (512, 256))
    npair = MEM_W // LANES

    def kern(q_ref, k_ref, v_ref, o_ref):
        first = _first_half((tq, LANES))
        q2 = jnp.concatenate(_split_pair(q_ref[...], first), axis=0)
        sc = _nt(q2, k_ref[...]) * SCALE
        e = jnp.exp(sc - jnp.max(sc, axis=1, keepdims=True))
        o2 = _nn(e / jnp.sum(e, axis=1, keepdims=True), v_ref[...])
        o_ref[...] = jnp.where(first, o2[:tq], o2[tq:])

    blk = pl.BlockSpec((tq, LANES), lambda a, i: (i, a))
    return _pcall(kern, name=name, grid=(npair, s // tq),
                  in_specs=[blk, pl.BlockSpec((ml, LANES), lambda a, i: (0, a)), pl.BlockSpec((ml, LANES), lambda a, i: (0, npair + a))],
                  out_specs=blk, out_shape=jax.ShapeDtypeStruct((s, MEM_W), F32),
                  compiler_params=_params("parallel", "parallel"))(qn, km, kv)


def _mem2_bwd(qn, km, kv, do, *, name):
    s = qn.shape[0]
    ml = km.shape[0]
    tq = _pick(s, (512, 256))
    npair = MEM_W // LANES

    def kern(q_ref, k_ref, v_ref, do_ref, dq_ref, dk_ref, dv_ref):
        @pl.when(pl.program_id(1) == 0)
        def _():
            dk_ref[...] = jnp.zeros_like(dk_ref)
            dv_ref[...] = jnp.zeros_like(dv_ref)

        first = _first_half((tq, LANES))
        q2 = jnp.concatenate(_split_pair(q_ref[...], first), axis=0)
        do2 = jnp.concatenate(_split_pair(do_ref[...], first), axis=0)
        sc = _nt(q2, k_ref[...]) * SCALE
        e = jnp.exp(sc - jnp.max(sc, axis=1, keepdims=True))
        p = e / jnp.sum(e, axis=1, keepdims=True)
        dp = _nt(do2, v_ref[...])
        ds = p * (dp - jnp.sum(p * dp, axis=1, keepdims=True)) * SCALE
        dq2 = _nn(ds, k_ref[...])
        dk_ref[...] += _tn(ds, q2)
        dv_ref[...] += _tn(p, do2)
        dq_ref[...] = jnp.where(first, dq2[:tq], dq2[tq:])

    blk = pl.BlockSpec((tq, LANES), lambda a, i: (i, a))
    kblk = pl.BlockSpec((ml, LANES), lambda a, i: (0, a))
    kshape = jax.ShapeDtypeStruct((ml, MEM_W), F32)
    return _pcall(kern, name=name, grid=(npair, s // tq),
                  in_specs=[blk, kblk, pl.BlockSpec((ml, LANES), lambda a, i: (0, npair + a)), blk],
                  out_specs=[blk, kblk, kblk], out_shape=[jax.ShapeDtypeStruct((s, MEM_W), F32), kshape, kshape],
                  compiler_params=_params("parallel", "arbitrary"))(qn, km, kv, do)


def _merge_fwd(logits, bias, ya, yb, yc, *, name):
    s, d = ya.shape
    ts = _pick(s, (512, 256))

    def kern(l0, l1, l2, b0, b1, b2, a_ref, b_ref, c_ref, o_ref):
        m = 0.0
        for l_ref, bb_ref, y_ref in ((l0, b0, a_ref), (l1, b1, b_ref), (l2, b2, c_ref)):
            m = m + _sigmoid(l_ref[...].astype(F32) + bb_ref[...]) * y_ref[...].astype(F32)
        o_ref[...] = m.astype(o_ref.dtype)

    row = pl.BlockSpec((ts, d), lambda i: (i, 0))
    lg = [pl.BlockSpec((ts, d), functools.partial(lambda i, c: (i, c), c=c)) for c in range(3)]
    bs = [pl.BlockSpec((1, d), functools.partial(lambda i, c: (0, c), c=c)) for c in range(3)]
    return _pcall(kern, name=name, grid=(s // ts,), in_specs=lg + bs + [row, row, row], out_specs=row,
                  out_shape=jax.ShapeDtypeStruct((s, d), BF16),
                  compiler_params=_params("parallel"))(logits, logits, logits, bias, bias, bias, ya, yb, yc)


def _merge_bwd(logits, bias, ya, yb, yc, dm, *, name):
    s, d = ya.shape
    ts = _pick(s, (256,))

    def kern(l0, l1, l2, b0, b1, b2, a_ref, b_ref, c_ref, dm_ref, da_ref, db_ref, dc_ref, dl_ref, dbias_ref):
        dmv = dm_ref[...]

        @pl.when(pl.program_id(0) == 0)
        def _():
            dbias_ref[...] = jnp.zeros_like(dbias_ref)

        for c, (l_ref, bb_ref, y_ref, dy_ref) in enumerate(((l0, b0, a_ref, da_ref), (l1, b1, b_ref, db_ref), (l2, b2, c_ref, dc_ref))):
            g = _sigmoid(l_ref[...].astype(F32) + bb_ref[...])
            dy_ref[...] = (dmv * g).astype(dy_ref.dtype)
            dl = dmv * y_ref[...].astype(F32) * g * (1.0 - g)
            dl_ref[:, c * d:(c + 1) * d] = dl.astype(dl_ref.dtype)
            dbias_ref[:, c * d:(c + 1) * d] += jnp.sum(dl, axis=0, keepdims=True)

    row = pl.BlockSpec((ts, d), lambda i: (i, 0))
    lg = [pl.BlockSpec((ts, d), functools.partial(lambda i, c: (i, c), c=c)) for c in range(3)]
    bs = [pl.BlockSpec((1, d), functools.partial(lambda i, c: (0, c), c=c)) for c in range(3)]
    yshape = jax.ShapeDtypeStruct((s, d), BF16)
    return _pcall(kern, name=name, grid=(s // ts,), in_specs=lg + bs + [row, row, row, row],
                  out_specs=[row, row, row, pl.BlockSpec((ts, 3 * d), lambda i: (i, 0)), pl.BlockSpec((1, 3 * d), lambda i: (0, 0))],
                  out_shape=[yshape] * 3 + [jax.ShapeDtypeStruct((s, 3 * d), BF16), jax.ShapeDtypeStruct((1, 3 * d), F32)],
                  compiler_params=_params("arbitrary"))(logits, logits, logits, bias, bias, bias, ya, yb, yc, dm)


G_FFN1 = ['ffn1_w1', 'ffn1_w3', 'ffn1_w2']
G_FFN2 = ['ffn2_w1', 'ffn2_w3', 'ffn2_w2']
G_MID = [n for n in BIG if n not in G_FFN1 + G_FFN2]


def _ffn_fwd(h, w1, w3, w2, tag, epilogue, side=None):
    carried = None
    if side is None:
        a, b, f = _ffn_up(h, w1, w3, name=f"{tag}_up")
    else:
        (a, b, f), carried = _ffn_up(h, w1, w3, name=f"{tag}_up", side=side)
    outs = _matmul(f, w2, name=f"{tag}_down", alpha=0.5, tm=512, tn=1024, tk=2816, epilogue=epilogue)
    return outs, (h, a, b, f), carried


def _ffn_bwd(x, norm, w1, w3, w2, saved, dy, dyb, tag, side=None, own_side=None):
    h, a, b, f = saved
    dw2 = _matmul(f, dyb, name=f"{tag}_dw2", ta=True, alpha=0.5, tm=1408, tn=1024, tk=2048)
    carried = None
    if side is None:
        da, db = _ffn_dact(dyb, w2, a, b, name=f"{tag}_dact")
    else:
        (da, db), carried = _ffn_dact(dyb, w2, a, b, name=f"{tag}_dact", side=side)
    dw1 = _matmul(h, da, name=f"{tag}_dw1", ta=True, tm=1024, tn=1408, tk=2048)
    dw3 = _matmul(h, db, name=f"{tag}_dw3", ta=True, tm=1024, tn=1408, tk=2048)
    outs = _matmul(da, w1, name=f"{tag}_dh", tb=True, tm=512, tn=1024, tk=1408, pair2=(db, w3),
                   epilogue=(_epi_rms_bwd, [x, dy], [norm], [F32, BF16], 1),
                   side=None if own_side is None else own_side(dw1, dw3, dw2))
    (dx, dxb, dnorm), own = outs if own_side is not None else (outs, None)
    return dx, dxb, dnorm, dw1, dw3, dw2, carried, own


def _local_step(x, mem, loss_target, wl, ws):
    s, d = x.shape
    assert s % (QB * 16) == 0
    rope = _rope_tables(s)
    bf = {n: wl[n].astype(BF16) for n in BIG}
    w = dict(ws)

    def gather(names):
        return _side([bf[n] for n in names], _two_level_phases())

    def whole(names, gathered):
        return {n: _whole_weight(n, t) for n, t in zip(names, gathered)}

    h1, early = _rms_fwd(x, w['ffn1_norm'], name="ffn1_rms", side=gather(G_FFN1))
    w.update(whole(G_FFN1, early))
    (x1, h), sv1, late = _ffn_fwd(h1, w['ffn1_w1'], w['ffn1_w3'], w['ffn1_w2'], "ffn1",
                                  (_epi_residual_rms, [x], [w['mix_norm']], [F32, BF16], 0),
                                  side=gather(G_MID))
    w.update(whole(G_MID, late))
    p = _matmul(h, w['w_in'], name="in_proj", out_dtype=BF16, tn=1024)
    logits = _matmul(h, w['w_gate'], name="gate_proj", out_dtype=BF16, tn=1024)
    c_qb, c_kb, c_vb, c_qc = 3 * SB_W, 3 * SB_W + DSA_W, 3 * SB_W + 2 * DSA_W, 3 * SB_W + 3 * DSA_W

    oa_t, late = _sb2_fwd(p, name="sb_fwd", side=gather(G_FFN2))
    w.update(whole(G_FFN2, late))
    ya = _matmul(oa_t, w['w_branch_sb'], name="sb_out", out_dtype=BF16)

    qb_n = _qknorm_fwd(p, c_qb, DSA_W, w['qn_dsa'], rope, name="dsa_qnorm", out_dtype=F32)
    kb_n = _qknorm_fwd(p, c_kb, DSA_W, w['kn_dsa'], rope, name="dsa_knorm", out_dtype=F32)
    vb32 = p[:, c_vb:c_vb + DSA_W].astype(F32)
    groups = range(len(DSA_GROUPS))
    ob_t, lse_b = _dsa2_combine([_dsa2_fwd(qb_n, kb_n, vb32, gi, name=f"dsa_fwd{gi}") for gi in groups], name="dsa_combine")
    yb = _matmul(ob_t, w['w_branch_dsa'], name="dsa_out", out_dtype=BF16)

    memh = _rms_fwd(mem, w['mem_norm'], name="mem_rms")
    kv = _matmul(memh, w['w_mem_kv'], name="mem_kv", out_dtype=BF16)
    km_n = _qknorm_fwd(kv, 0, MEM_W, w['kn_mem'], None, name="mem_knorm")
    qc_n = _qknorm_fwd(p, c_qc, MEM_W, w['qn_mem'], None, name="mem_qnorm")
    oc_t = _mem2_fwd(qc_n, km_n, kv, name="mem_fwd")
    yc = _matmul(oc_t, w['w_branch_mem'], name="mem_out", out_dtype=BF16)

    merged = _merge_fwd(logits, w['b_gate'], ya, yb, yc, name="merge")
    x2, h2 = _matmul(merged, w['w_out'], name="out_proj", tn=1024,
                     epilogue=(_epi_residual_rms, [x1], [w['ffn2_norm']], [F32, BF16], 0))
    (dx3, dx3b, sq), sv2, _ = _ffn_fwd(h2, w['ffn2_w1'], w['ffn2_w3'], w['ffn2_w2'], "ffn2",
                                       (_epi_loss, [x2, loss_target], [], [F32, BF16], 1))
    loss = jnp.sum(sq) * (0.5 / d)

    g, recv = {}, {}

    def owners(names):
        return [_for_owners(n, g[n], wl[n].shape) for n in names]

    dx2, dx2b, g['ffn2_norm'], g['ffn2_w1'], g['ffn2_w3'], g['ffn2_w2'], _, _ = _ffn_bwd(
        x2, w['ffn2_norm'], w['ffn2_w1'], w['ffn2_w3'], w['ffn2_w2'], sv2, dx3, dx3b, "ffn2")

    g['w_out'] = _matmul(merged, dx2b, name="d_w_out", ta=True, tn=1024, tk=512)
    dm = _matmul(dx2b, w['w_out'], name="d_merged", tb=True, tn=1024)
    dya, dyb, dyc, dlogits, g['b_gate'] = _merge_bwd(logits, w['b_gate'], ya, yb, yc, dm, name="d_merge")

    g['w_branch_sb'] = _matmul(oa_t, dya, name="d_w_sb", ta=True, tn=1024, tk=512)
    g['w_branch_dsa'] = _matmul(ob_t, dyb, name="d_w_dsa", ta=True, tk=512)
    g['w_branch_mem'] = _matmul(oc_t, dyc, name="d_w_mem", ta=True, tk=512)
    doa = _matmul(dya, w['w_branch_sb'], name="d_oa", tb=True, out_dtype=BF16)
    dob = _matmul(dyb, w['w_branch_dsa'], name="d_ob", tb=True)
    doc = _matmul(dyc, w['w_branch_mem'], name="d_oc", tb=True, out_dtype=BF16)

    (dqa, dka, dva), got = _sb2_bwd(p, oa_t, doa, name="sb_bwd", side=_side(owners(G_FFN2), _direct_phases(True)))
    recv.update(zip(G_FFN2, got))

    dd_b = _dsa2_prep(ob_t, dob, name="dsa_prep")
    dgrp = [_dsa2_bwd(qb_n, kb_n, vb32, dob, lse_b, dd_b, gi, name=f"dsa_bwd{gi}") for gi in groups]
    dvb = jnp.concatenate([t[2] for t in dgrp], axis=1).astype(BF16)
    dqb, g['qn_dsa'] = _qknorm_bwd(p, c_qb, DSA_W, w['qn_dsa'], rope, [t[0] for t in dgrp], name="d_dsa_qnorm")
    dkb, g['kn_dsa'] = _qknorm_bwd(p, c_kb, DSA_W, w['kn_dsa'], rope, [t[1] for t in dgrp], name="d_dsa_knorm")

    dqc_n, dkm_n, dvm = _mem2_bwd(qc_n, km_n, kv, doc, name="mem_bwd")
    dqc, g['qn_mem'] = _qknorm_bwd(p, c_qc, MEM_W, w['qn_mem'], None, dqc_n, name="d_mem_qnorm")
    dkm, g['kn_mem'] = _qknorm_bwd(kv, 0, MEM_W, w['kn_mem'], None, dkm_n, name="d_mem_knorm")
    dkv = jnp.concatenate([dkm, dvm.astype(BF16)], axis=1)
    g['w_mem_kv'] = _matmul(memh, dkv, name="d_w_mem_kv", ta=True)
    dmemh = _matmul(dkv, w['w_mem_kv'], name="d_memh", tb=True)
    _, _, g['mem_norm'] = _rms_bwd(mem, w['mem_norm'], dmemh, None, name="d_mem_rms")

    dp = jnp.concatenate([dqa.astype(BF16), dka.astype(BF16), dva.astype(BF16),
                          dqb, dkb, dvb, dqc], axis=1)
    g['w_in'] = _matmul(h, dp, name="d_w_in", ta=True, tn=2048, tk=1024)
    g['w_gate'] = _matmul(h, dlogits, name="d_w_gate", ta=True, tn=1536, tk=1024)
    dh = _matmul(dp, w['w_in'], name="d_h_in", tb=True, tn=1024, tk=2048)
    dx1, dx1b, g['mix_norm'] = _matmul(dlogits, w['w_gate'], name="d_h_gate", tb=True, tm=512, tn=1024, tk=3072,
                                       epilogue=(_epi_rms_bwd_sum, [dh, x1, dx2], [w['mix_norm']], [F32, BF16], 1))

    def own_side(dw1, dw3, dw2):
        g.update(ffn1_w1=dw1, ffn1_w3=dw3, ffn1_w2=dw2)
        return _side(owners(G_FFN1), _direct_phases(True))

    dx0, _, g['ffn1_norm'], _, _, _, got_mid, got_own = _ffn_bwd(
        x, w['ffn1_norm'], w['ffn1_w1'], w['ffn1_w3'], w['ffn1_w2'], sv1, dx1, dx1b, "ffn1",
        side=_side(owners(G_MID), _direct_phases(True)), own_side=own_side)
    recv.update(zip(G_MID, got_mid))
    recv.update(zip(G_FFN1, got_own))
    return loss, dx0, recv, {n: g[n] for n in SMALL}


def _whole_weight(name, gathered):
    _, r, c = gathered.shape
    return gathered.reshape(N_DEV * r, c) if SHARD_AXIS[name] == 0 else gathered.transpose(1, 0, 2).reshape(r, N_DEV * c)


def _for_owners(name, grad, shard_shape):
    r, c = shard_shape
    blk = grad.reshape(N_DEV, r, c) if SHARD_AXIS[name] == 0 else grad.reshape(r, N_DEV, c).transpose(1, 0, 2)
    return blk.astype(BF16)


def _pack_small(d, names, extra_rows):
    parts = []
    for n in names:
        v = d[n].reshape(-1)
        pad = (-v.size) % LANES
        parts.append(jnp.concatenate([v, jnp.zeros((pad,), v.dtype)]).reshape(-1, LANES))
    t = jnp.concatenate(parts, axis=0)
    return jnp.concatenate([t, jnp.zeros((extra_rows, LANES), t.dtype)], axis=0)


def _unpack_small(t, like, names):
    out, off = {}, 0
    for n in names:
        size = like[n].size
        rows = -(-size // LANES)
        out[n] = t[off:off + rows].reshape(-1)[:size].reshape(like[n].shape)
        off += rows
    return out


def _direct_phases(per_peer):
    def descriptors(src_ref, out_ref, send_sems, recv_sems, local_sem):
        x, y, c = lax.axis_index("x"), lax.axis_index("y"), lax.axis_index("c")
        me = 4 * x + 2 * y + c
        mine = pltpu.make_async_copy(src_ref.at[me] if per_peer else src_ref, out_ref.at[me], local_sem)
        copies = []
        for k in range(1, N_DEV):
            px = 1 - x if k & 4 else x
            py = 1 - y if k & 2 else y
            pc = 1 - c if k & 1 else c
            copies.append(pltpu.make_async_remote_copy(
                src_ref=src_ref.at[4 * px + 2 * py + pc] if per_peer else src_ref, dst_ref=out_ref.at[me],
                send_sem=send_sems.at[k - 1], recv_sem=recv_sems.at[k - 1],
                device_id=(px, py, pc), device_id_type=pl.DeviceIdType.MESH))
        return mine, copies

    def start(*refs):
        mine, copies = descriptors(*refs)
        mine.start()
        for cp in copies:
            cp.start()

    def forward(*refs):
        pass

    def finish(*refs):
        mine, copies = descriptors(*refs)
        for cp in copies:
            cp.wait_recv()
        for cp in copies:
            cp.wait_send()
        mine.wait()

    return start, forward, finish


def _exchange_parts(srcs, phases):
    n = len(srcs)
    shapes = [jax.ShapeDtypeStruct((N_DEV,) + tuple(s.shape[-2:]), s.dtype) for s in srcs]
    sems = [pltpu.SemaphoreType.DMA((n, N_DEV - 1)), pltpu.SemaphoreType.DMA((n, N_DEV - 1)), pltpu.SemaphoreType.DMA((n,))]

    def lift(phase):
        def run(src_refs, out_refs, send, recv, local):
            for a, (s_ref, o_ref) in enumerate(zip(src_refs, out_refs)):
                phase(s_ref, o_ref, send.at[a], recv.at[a], local.at[a])
        return run

    return shapes, sems, [lift(p) for p in phases]


def _exchange(srcs, phases, *, name):
    shapes, sems, runs = _exchange_parts(srcs, phases)
    n = len(srcs)

    def body(*refs):
        for run in runs:
            run(refs[:n], refs[n:2 * n], *refs[2 * n:])

    anyspace = pl.BlockSpec(memory_space=pl.ANY)
    return _pcall(body, name=name, in_specs=[anyspace] * n, out_specs=[anyspace] * n, out_shape=shapes, scratch_shapes=sems)(*srcs)


def _side(srcs, phases):
    shapes, sems, (start, forward, finish) = _exchange_parts(srcs, phases)

    def before(first, mid, ins, outs, scratch):
        pl.when(first)(lambda: start(ins, outs, *scratch))
        pl.when(mid)(lambda: forward(ins, outs, *scratch))

    def after(last, ins, outs, scratch):
        pl.when(last)(lambda: finish(ins, outs, *scratch))

    return list(srcs), shapes, sems, before, after


def _call_2d(kern, *, name, grid, in_specs, out_specs, out_shape, ins, scratch_shapes=(), semantics, side=None):
    if side is None:
        return _pcall(kern, name=name, grid=grid, in_specs=in_specs, out_specs=out_specs, out_shape=out_shape,
                      scratch_shapes=list(scratch_shapes), compiler_params=_params(*semantics))(*ins)
    s_ins, s_shapes, s_scratch, before, after = side
    n_in, n_out, n_scr = len(ins), len(out_shape), len(scratch_shapes)

    def combined(*refs):
        refs = list(refs)
        cut = [n_in, len(s_ins), n_out, len(s_shapes), n_scr, len(s_scratch)]
        parts, pos = [], 0
        for c in cut:
            parts.append(refs[pos:pos + c])
            pos += c
        m_in, c_in, m_out, c_out, m_scr, c_scr = parts
        ids = [pl.program_id(a) for a in range(len(grid))]
        inner_zero = functools.reduce(jnp.logical_and, [i == 0 for i in ids[1:]], True)
        first = jnp.logical_and(ids[0] == 0, inner_zero)
        mid = jnp.logical_and(ids[0] == grid[0] // 2, inner_zero)
        last = functools.reduce(jnp.logical_and, [i == n - 1 for i, n in zip(ids, grid)])
        before(first, mid, c_in, c_out, c_scr)
        kern(*m_in, *m_out, *m_scr)
        after(last, c_in, c_out, c_scr)

    anyspace = pl.BlockSpec(memory_space=pl.ANY)
    outs = _pcall(combined, name=name, grid=grid, in_specs=list(in_specs) + [anyspace] * len(s_ins),
                  out_specs=list(out_specs) + [anyspace] * len(s_shapes), out_shape=list(out_shape) + s_shapes,
                  scratch_shapes=list(scratch_shapes) + s_scratch, compiler_params=_params(*["arbitrary"] * len(grid)))(*ins, *s_ins)
    return outs[:n_out], outs[n_out:]


def _two_level_phases():
    def parts(src_ref, out_ref, send_sems, recv_sems, local_sem):
        x, y, c = lax.axis_index("x"), lax.axis_index("y"), lax.axis_index("c")
        me, sibling = (x, y, c), (x, y, 1 - c)
        chips = [(1 - x, y), (x, 1 - y), (1 - x, 1 - y)]

        def slab(px, py, pc):
            return out_ref.at[4 * px + 2 * py + pc]

        def copy(k, block, to, from_src=False):
            return pltpu.make_async_remote_copy(
                src_ref=src_ref if from_src else slab(*block), dst_ref=slab(*block),
                send_sem=send_sems.at[k], recv_sem=recv_sems.at[k], device_id=to, device_id_type=pl.DeviceIdType.MESH)

        return dict(
            mine=lambda: pltpu.make_async_copy(src_ref, slab(*me), local_sem),
            first=lambda: [copy(0, me, sibling, True)] + [copy(1 + j, me, (*chip, c), True) for j, chip in enumerate(chips)],
            passed=lambda: [copy(4 + j, (*chip, c), sibling) for j, chip in enumerate(chips)],
            landed=lambda: [copy(1 + j, (*chip, c), me) for j, chip in enumerate(chips)],
            late=lambda: [copy(0, sibling, me)] + [copy(4 + j, (*chip, 1 - c), me) for j, chip in enumerate(chips)])

    def start(*refs):
        make = parts(*refs)
        make['mine']().start()
        for cp in make['first']():
            cp.start()

    def forward(*refs):
        make = parts(*refs)
        for arrived, onward in zip(make['landed'](), make['passed']()):
            arrived.wait_recv()
            onward.start()

    def finish(*refs):
        make = parts(*refs)
        for cp in make['late']():
            cp.wait_recv()
        for cp in make['first']() + make['passed']():
            cp.wait_send()
        make['mine']().wait()

    return start, forward, finish


def _adamw(recv, w, m, v, *, name):
    rows, cols = w.shape
    tr = _pick(rows, (256, 128, 64))

    def kern(r_ref, w_ref, m_ref, v_ref, g_ref, d_ref, mo_ref, vo_ref):
        g = r_ref[0].astype(F32)
        for p in range(1, N_DEV):
            g = g + r_ref[p].astype(F32)
        mn = ADAM_B1 * m_ref[...] + (1.0 - ADAM_B1) * g
        vn = ADAM_B2 * v_ref[...] + (1.0 - ADAM_B2) * (g * g)
        m_hat = mn / (1.0 - ADAM_B1 ** ADAM_STEP)
        v_hat = vn / (1.0 - ADAM_B2 ** ADAM_STEP)
        g_ref[...] = g
        d_ref[...] = -ADAM_LR * (m_hat / (jnp.sqrt(v_hat) + ADAM_EPS) + ADAM_WD * w_ref[...])
        mo_ref[...] = mn
        vo_ref[...] = vn

    row = pl.BlockSpec((tr, cols), lambda i: (i, 0))
    shp = jax.ShapeDtypeStruct((rows, cols), F32)
    return _pcall(kern, name=name, grid=(rows // tr,), in_specs=[pl.BlockSpec((N_DEV, tr, cols), lambda i: (0, i, 0)), row, row, row],
                  out_specs=[row, row, row, row], out_shape=[shp, shp, shp, shp], compiler_params=_params("parallel"))(recv, w, m, v)


INPUTS = ['x', 'mem'] + WEIGHTS + ['loss_target'] + ['m_' + n for n in WEIGHTS] + ['v_' + n for n in WEIGHTS]
SMALL_PAD_ROWS = 4


def kernel(x, mem, ffn1_norm, ffn1_w1, ffn1_w3, ffn1_w2, mix_norm, mem_norm, w_in, w_mem_kv, qn_dsa, kn_dsa, qn_mem, kn_mem, w_branch_sb, w_branch_dsa, w_branch_mem, w_gate, b_gate, w_out, ffn2_norm, ffn2_w1, ffn2_w3, ffn2_w2, loss_target, m_ffn1_norm, m_ffn1_w1, m_ffn1_w3, m_ffn1_w2, m_mix_norm, m_mem_norm, m_w_in, m_w_mem_kv, m_qn_dsa, m_kn_dsa, m_qn_mem, m_kn_mem, m_w_branch_sb, m_w_branch_dsa, m_w_branch_mem, m_w_gate, m_b_gate, m_w_out, m_ffn2_norm, m_ffn2_w1, m_ffn2_w3, m_ffn2_w2, v_ffn1_norm, v_ffn1_w1, v_ffn1_w3, v_ffn1_w2, v_mix_norm, v_mem_norm, v_w_in, v_w_mem_kv, v_qn_dsa, v_kn_dsa, v_qn_mem, v_kn_mem, v_w_branch_sb, v_w_branch_dsa, v_w_branch_mem, v_w_gate, v_b_gate, v_w_out, v_ffn2_norm, v_ffn2_w1, v_ffn2_w3, v_ffn2_w2):
    given = dict(zip(INPUTS, (x, mem, ffn1_norm, ffn1_w1, ffn1_w3, ffn1_w2, mix_norm, mem_norm, w_in, w_mem_kv, qn_dsa, kn_dsa, qn_mem, kn_mem, w_branch_sb, w_branch_dsa, w_branch_mem, w_gate, b_gate, w_out, ffn2_norm, ffn2_w1, ffn2_w3, ffn2_w2, loss_target, m_ffn1_norm, m_ffn1_w1, m_ffn1_w3, m_ffn1_w2, m_mix_norm, m_mem_norm, m_w_in, m_w_mem_kv, m_qn_dsa, m_kn_dsa, m_qn_mem, m_kn_mem, m_w_branch_sb, m_w_branch_dsa, m_w_branch_mem, m_w_gate, m_b_gate, m_w_out, m_ffn2_norm, m_ffn2_w1, m_ffn2_w3, m_ffn2_w2, v_ffn1_norm, v_ffn1_w1, v_ffn1_w3, v_ffn1_w2, v_mix_norm, v_mem_norm, v_w_in, v_w_mem_kv, v_qn_dsa, v_kn_dsa, v_qn_mem, v_kn_mem, v_w_branch_sb, v_w_branch_dsa, v_w_branch_mem, v_w_gate, v_b_gate, v_w_out, v_ffn2_norm, v_ffn2_w1, v_ffn2_w3, v_ffn2_w2), strict=True))
    wl = {n: given[n][0] for n in BIG}
    ws = {n: given[n] for n in SMALL}

    loss, dx, recv, g = _local_step(x[0], mem[0], loss_target[0], wl, ws)

    big = [{}, {}, {}, {}]
    for n in G_FFN2 + G_MID + G_FFN1:
        outs = _adamw(recv[n], wl[n], given['m_' + n][0], given['v_' + n][0], name=f"adamw_{n}")
        for kind, t in enumerate(outs):
            big[kind][n] = t

    gs = _pack_small(g, SMALL, SMALL_PAD_ROWS)
    loss_row = gs.shape[0] - SMALL_PAD_ROWS
    gs = gs.at[loss_row, 0].set(loss)
    recv_s = _exchange([gs], _direct_phases(False), name="gather_small")[0]
    small = _adamw(recv_s, _pack_small(ws, SMALL, SMALL_PAD_ROWS), _pack_small({n: given['m_' + n] for n in SMALL}, SMALL, SMALL_PAD_ROWS),
                   _pack_small({n: given['v_' + n] for n in SMALL}, SMALL, SMALL_PAD_ROWS), name="adamw_replicated")
    total_loss = small[0][loss_row, 0]
    small = [_unpack_small(t, ws, SMALL) for t in small]

    outs = [total_loss, dx[None]]
    for kind in range(4):
        outs += [big[kind][n][None] if n in wl else small[kind][n] for n in WEIGHTS]
    return tuple(outs)
```

```python
import functools

import jax
import jax.numpy as jnp
from jax import lax
from jax.experimental import pallas as pl
from jax.experimental.pallas import tpu as pltpu

F32 = jnp.float32
BF16 = jnp.bfloat16
MXU_DT = jnp.bfloat16

N_DEV = 8
HEAD_DIM = 64
SB_HEADS = 8
DSA_GROUPS = ((128, 1), (512, 4), (2048, 16))
DSA_HPG = 4
MEM_HEADS = 4
SB_W = SB_HEADS * HEAD_DIM
DSA_W = DSA_HPG * len(DSA_GROUPS) * HEAD_DIM
DSA_OUT_W = DSA_HPG * HEAD_DIM
MEM_W = MEM_HEADS * HEAD_DIM
ROPE_THETA = 10000.0
NORM_EPS = 1e-6
QB = 128
SCALE = HEAD_DIM ** -0.5
ADAM_LR, ADAM_B1, ADAM_B2, ADAM_EPS, ADAM_WD, ADAM_STEP = 0.001, 0.9, 0.999, 1e-08, 0.01, 10

LANES = 128
VMEM_LIMIT = 48 * 1024 * 1024
SB_DEAD = -110.0 * 1.4426950408889634

WEIGHTS = ['ffn1_norm', 'ffn1_w1', 'ffn1_w3', 'ffn1_w2', 'mix_norm', 'mem_norm', 'w_in', 'w_mem_kv', 'qn_dsa', 'kn_dsa',
           'qn_mem', 'kn_mem', 'w_branch_sb', 'w_branch_dsa', 'w_branch_mem', 'w_gate', 'b_gate', 'w_out', 'ffn2_norm',
           'ffn2_w1', 'ffn2_w3', 'ffn2_w2']
SHARD_AXIS = {'ffn1_norm': None, 'ffn1_w1': 1, 'ffn1_w3': 1, 'ffn1_w2': 0, 'mix_norm': None, 'mem_norm': None, 'w_in': 1,
              'w_mem_kv': 0, 'qn_dsa': None, 'kn_dsa': None, 'qn_mem': None, 'kn_mem': None, 'w_branch_sb': 1,
              'w_branch_dsa': 1, 'w_branch_mem': 1, 'w_gate': 1, 'b_gate': None, 'w_out': 0, 'ffn2_norm': None,
              'ffn2_w1': 1, 'ffn2_w3': 1, 'ffn2_w2': 0}
BIG = [n for n in WEIGHTS if SHARD_AXIS[n] is not None]
SMALL = [n for n in WEIGHTS if SHARD_AXIS[n] is None]


def _pcall(kern, **kw):
    return pl.pallas_call(kern, **kw)


def _params(*sem):
    return pltpu.CompilerParams(dimension_semantics=sem, vmem_limit_bytes=VMEM_LIMIT)


def _dot(a, b, dims):
    return lax.dot_general(a.astype(MXU_DT), b.astype(MXU_DT), (dims, ((), ())), preferred_element_type=F32)


def _nn(a, b):
    return _dot(a, b, ((1,), (0,)))


def _nt(a, b):
    return _dot(a, b, ((1,), (1,)))


def _tn(a, b):
    return _dot(a, b, ((0,), (0,)))


def _pick(n, prefs):
    for p in prefs:
        if n % p == 0:
            return p
    return n


def _matmul(a, b, *, name, ta=False, tb=False, out_dtype=F32, res=None, alpha=1.0, tm=1024, tn=512, tk=1024, pair2=None,
            epilogue=None, side=None):
    if ta:
        kdim, m = a.shape
    else:
        m, kdim = a.shape
    n = b.shape[0] if tb else b.shape[1]
    tm = _pick(m, (tm, 512, 256, 128))
    tn = _pick(n, (tn, 512, 384, 256, 128))
    tk = _pick(kdim, (tk, 1024, 512, 256, 128))
    nk = kdim // tk
    a_spec = pl.BlockSpec((tk, tm), lambda i, j, k: (k, i)) if ta else pl.BlockSpec((tm, tk), lambda i, j, k: (i, k))
    b_spec = pl.BlockSpec((tn, tk), lambda i, j, k: (j, k)) if tb else pl.BlockSpec((tk, tn), lambda i, j, k: (k, j))
    o_spec = pl.BlockSpec((tm, tn), lambda i, j, k: (i, j))
    v_spec = pl.BlockSpec((1, tn), lambda i, j, k: (0, j))
    dims = ((0 if ta else 1,), (1 if tb else 0,))
    n_mm = 2 if pair2 is None else 4
    if epilogue is None:
        row_ins, vec_ins = ([] if res is None else [res]), []
        out_dtypes, n_vec = [out_dtype], 0
    else:
        assert tn == n and res is None
        epi_fn, row_ins, vec_ins, out_dtypes, n_vec = epilogue
    n_row_out = len(out_dtypes)

    def kern(*refs):
        refs = list(refs)
        acc_ref = refs.pop() if nk > 1 else None
        mm = refs[:n_mm]
        extra = refs[n_mm:n_mm + len(row_ins) + len(vec_ins)]
        outs = refs[n_mm + len(extra):]
        i = pl.program_id(0)
        k = pl.program_id(2)

        def product():
            part = _dot(mm[0][...], mm[1][...], dims)
            if pair2 is not None:
                part = part + _dot(mm[2][...], mm[3][...], dims)
            return part

        def finish(r):
            if alpha != 1.0:
                r = r * alpha
            if epilogue is None:
                if extra:
                    r = extra[0][...] + r
                outs[0][...] = r.astype(out_dtype)
                return
            vals = epi_fn(r, *[e[...] for e in extra])
            for o_ref, v in zip(outs[:n_row_out], vals[:n_row_out]):
                o_ref[...] = v.astype(o_ref.dtype)
            for o_ref, v in zip(outs[n_row_out:], vals[n_row_out:]):
                @pl.when(i == 0)
                def _():
                    o_ref[...] = jnp.zeros_like(o_ref)

                o_ref[...] += v

        if nk == 1:
            finish(product())
            return

        @pl.when(k == 0)
        def _():
            acc_ref[...] = jnp.zeros_like(acc_ref)

        acc_ref[...] += product()

        @pl.when(k == nk - 1)
        def _():
            finish(acc_ref[...])

    ins = [a, b] + ([] if pair2 is None else list(pair2)) + list(row_ins) + list(vec_ins)
    specs = [a_spec, b_spec] * (n_mm // 2) + [o_spec] * len(row_ins) + [v_spec] * len(vec_ins)
    out_specs = [o_spec] * n_row_out + [v_spec] * n_vec
    out_shape = [jax.ShapeDtypeStruct((m, n), dt) for dt in out_dtypes] + [jax.ShapeDtypeStruct((1, n), F32)] * n_vec
    outs = _call_2d(kern, name=name, grid=(m // tm, n // tn, nk), in_specs=specs, out_specs=out_specs, out_shape=out_shape,
                    ins=ins, scratch_shapes=[pltpu.VMEM((tm, tn), F32)] if nk > 1 else [],
                    semantics=("arbitrary" if n_vec else "parallel", "parallel", "arbitrary"), side=side)
    carried = None
    if side is not None:
        outs, carried = outs
    outs = outs[0] if epilogue is None else outs
    return outs if side is None else (outs, carried)


def _epi_residual_rms(r, res, gain):
    xn = res + r
    return xn, xn * lax.rsqrt(jnp.mean(xn * xn, axis=-1, keepdims=True) + NORM_EPS) * gain


def _epi_rms_bwd(r, x, dres, gain):
    rs = lax.rsqrt(jnp.mean(x * x, axis=-1, keepdims=True) + NORM_EPS)
    xh = x * rs
    dy = r * gain
    dx = dres + rs * (dy - xh * jnp.mean(dy * xh, axis=-1, keepdims=True))
    return dx, dx, jnp.sum(r * xh, axis=0, keepdims=True)


def _epi_rms_bwd_sum(r, r0, x, dres, gain):
    return _epi_rms_bwd(r + r0, x, dres, gain)


def _epi_loss(r, res, target):
    e = (res + r) - target
    dy = e / e.shape[-1]
    return dy, dy, jnp.sum(e * e, axis=0, keepdims=True)
def _rms_fwd(x, g, *, name, side=None):
    s, d = x.shape
    ts = _pick(s, (512, 256))

    def kern(x_ref, g_ref, h_ref):
        xf = x_ref[...]
        r = lax.rsqrt(jnp.mean(xf * xf, axis=-1, keepdims=True) + NORM_EPS)
        h_ref[...] = (xf * r * g_ref[...]).astype(h_ref.dtype)

    outs = _call_2d(kern, name=name, grid=(s // ts,),
                    in_specs=[pl.BlockSpec((ts, d), lambda i: (i, 0)), pl.BlockSpec((1, d), lambda i: (0, 0))],
                    out_specs=[pl.BlockSpec((ts, d), lambda i: (i, 0))], out_shape=[jax.ShapeDtypeStruct((s, d), BF16)],
                    ins=[x, g], semantics=("parallel",), side=side)
    return outs[0] if side is None else (outs[0][0], outs[1])


def _rms_bwd(x, g, dh, res, *, name):
    s, d = x.shape
    ts = _pick(s, (512, 256))

    def kern(*refs):
        if res is None:
            x_ref, g_ref, dh_ref, dx_ref, dxb_ref, dg_ref = refs
            r_ref = None
        else:
            x_ref, g_ref, dh_ref, r_ref, dx_ref, dxb_ref, dg_ref = refs
        xf = x_ref[...]
        r = lax.rsqrt(jnp.mean(xf * xf, axis=-1, keepdims=True) + NORM_EPS)
        xh = xf * r
        dhf = dh_ref[...].astype(F32)
        dy = dhf * g_ref[...]
        dx = r * (dy - xh * jnp.mean(dy * xh, axis=-1, keepdims=True))
        if r_ref is not None:
            dx = r_ref[...] + dx
        dx_ref[...] = dx
        dxb_ref[...] = dx.astype(dxb_ref.dtype)

        @pl.when(pl.program_id(0) == 0)
        def _():
            dg_ref[...] = jnp.zeros_like(dg_ref)

        dg_ref[...] += jnp.sum(dhf * xh, axis=0, keepdims=True)

    row = pl.BlockSpec((ts, d), lambda i: (i, 0))
    vec = pl.BlockSpec((1, d), lambda i: (0, 0))
    ins = [x, g, dh] + ([] if res is None else [res])
    return _pcall(kern, name=name, grid=(s // ts,), in_specs=[row, vec, row] + ([] if res is None else [row]),
                  out_specs=[row, row, vec],
                  out_shape=[jax.ShapeDtypeStruct((s, d), F32), jax.ShapeDtypeStruct((s, d), BF16), jax.ShapeDtypeStruct((1, d), F32)],
                  compiler_params=_params("arbitrary"))(*ins)


def _sigmoid(x):
    return 1.0 / (1.0 + jnp.exp(-x))


FFN_TM, FFN_TF = 512, 1408


def _ffn_up(h, w1, w3, *, name, side=None):
    s, d = h.shape
    fdim = w1.shape[1]
    tm, tf = _pick(s, (FFN_TM, 256)), _pick(fdim, (FFN_TF, 512, 256, 128))

    def kern(h_ref, w1_ref, w3_ref, a_ref, b_ref, f_ref):
        hb = h_ref[...]
        a = _nn(hb, w1_ref[...])
        b = _nn(hb, w3_ref[...])
        a_ref[...] = a.astype(a_ref.dtype)
        b_ref[...] = b.astype(b_ref.dtype)
        f_ref[...] = (a * _sigmoid(a) * b).astype(f_ref.dtype)

    wspec = pl.BlockSpec((d, tf), lambda i, j: (0, j))
    ospec = pl.BlockSpec((tm, tf), lambda i, j: (i, j))
    shp = jax.ShapeDtypeStruct((s, fdim), BF16)
    return _call_2d(kern, name=name, grid=(s // tm, fdim // tf), in_specs=[pl.BlockSpec((tm, d), lambda i, j: (i, 0)), wspec, wspec],
                    out_specs=[ospec, ospec, ospec], out_shape=[shp, shp, shp], ins=[h, w1, w3],
                    semantics=("parallel", "parallel"), side=side)


def _ffn_dact(dy, w2, a, b, *, name, side=None):
    s, d = dy.shape
    fdim = w2.shape[0]
    tm, tf = _pick(s, (FFN_TM, 256)), _pick(fdim, (FFN_TF, 512, 256, 128))

    half = (tf // LANES + 1) // 2 * LANES

    def kern(dy_ref, w2_ref, a_ref, b_ref, da_ref, db_ref):
        dyb = dy_ref[...]
        pieces = ((0, half), (half, tf))
        dfs = [_nt(dyb, w2_ref[lo:hi, :]) * 0.5 for lo, hi in pieces]
        for (lo, hi), df in zip(pieces, dfs):
            av = a_ref[:, lo:hi].astype(F32)
            sg = _sigmoid(av)
            da_ref[:, lo:hi] = (df * b_ref[:, lo:hi].astype(F32) * (sg + av * sg * (1.0 - sg))).astype(da_ref.dtype)
            db_ref[:, lo:hi] = (df * (av * sg)).astype(db_ref.dtype)

    ospec = pl.BlockSpec((tm, tf), lambda i, j: (i, j))
    shp = jax.ShapeDtypeStruct((s, fdim), BF16)
    return _call_2d(kern, name=name, grid=(s // tm, fdim // tf),
                    in_specs=[pl.BlockSpec((tm, d), lambda i, j: (i, 0)), pl.BlockSpec((tf, d), lambda i, j: (j, 0)), ospec, ospec],
                    out_specs=[ospec, ospec], out_shape=[shp, shp], ins=[dy, w2, a, b], semantics=("parallel", "parallel"), side=side)


def _head_mean(v, bd):
    outs = []
    for c in range(v.shape[1] // LANES):
        x = v[:, c * LANES:(c + 1) * LANES]
        hi = x.astype(BF16)
        lo = (x - hi.astype(F32)).astype(BF16)
        outs.append(lax.dot_general(jnp.concatenate([hi, lo], axis=1), bd, (((1,), (0,)), ((), ())), preferred_element_type=F32))
    return outs[0] if len(outs) == 1 else jnp.concatenate(outs, axis=1)


def _partner(v):
    w = v.shape[1]
    lane = lax.broadcasted_iota(jnp.int32, v.shape, 1)
    return jnp.where(lane % HEAD_DIM < HEAD_DIM // 2, pltpu.roll(v, w - HEAD_DIM // 2, 1), pltpu.roll(v, HEAD_DIM // 2, 1))


def _block_diag(w=None):
    r = (lax.broadcasted_iota(jnp.int32, (2 * LANES, LANES), 0) % LANES) // HEAD_DIM
    c = lax.broadcasted_iota(jnp.int32, (2 * LANES, LANES), 1) // HEAD_DIM
    return jnp.where(r == c, 1.0 / HEAD_DIM, 0.0).astype(BF16)


def _rope_tables(s):
    half = HEAD_DIM // 2
    inv_freq = jnp.power(ROPE_THETA, -jnp.arange(half, dtype=F32) / half)
    ang = jnp.arange(s).astype(F32)[:, None] * inv_freq[None, :]
    cos, sin = jnp.cos(ang), jnp.sin(ang)
    cos2 = jnp.concatenate([cos, cos, cos, cos], axis=1)
    sin2 = jnp.concatenate([-sin, sin, -sin, sin], axis=1)
    return cos2, sin2


def _qknorm_fwd(src, col0, width, gain, rope, *, name, out_dtype=BF16):
    s = src.shape[0]
    ts = _pick(s, (512, 256))
    cb = col0 // width
    assert col0 % width == 0
    reps = width // LANES
    g = jnp.tile(gain, (1, width // HEAD_DIM))

    def kern(*refs):
        if rope is None:
            x_ref, g_ref, o_ref = refs
        else:
            x_ref, g_ref, c_ref, s_ref, o_ref = refs
        x = x_ref[...].astype(F32)
        bd = _block_diag(width)
        r = lax.rsqrt(_head_mean(x * x, bd) + NORM_EPS)
        y = x * r * g_ref[...]
        if rope is not None:
            y = y * jnp.tile(c_ref[...], (1, reps)) + _partner(y) * jnp.tile(s_ref[...], (1, reps))
        o_ref[...] = y.astype(o_ref.dtype)

    xs = pl.BlockSpec((ts, width), lambda i: (i, cb))
    tab = pl.BlockSpec((ts, LANES), lambda i: (i, 0))
    ins = [src, g] + ([] if rope is None else list(rope))
    specs = [xs, pl.BlockSpec((1, width), lambda i: (0, 0))] + ([] if rope is None else [tab, tab])
    return _pcall(kern, name=name, grid=(s // ts,), in_specs=specs, out_specs=pl.BlockSpec((ts, width), lambda i: (i, 0)),
                  out_shape=jax.ShapeDtypeStruct((s, width), out_dtype), compiler_params=_params("parallel"))(*ins)


def _qknorm_bwd(src, col0, width, gain, rope, dout, *, name):
    s = src.shape[0]
    ts = _pick(s, (512, 256))
    cb = col0 // width
    reps = width // LANES
    g = jnp.tile(gain, (1, width // HEAD_DIM))

    douts = list(dout) if isinstance(dout, (list, tuple)) else [dout]
    piece = width // len(douts)

    def kern(*refs):
        refs = list(refs)
        dg_ref = refs.pop()
        dx_ref = refs.pop()
        do_refs = [refs.pop() for _ in douts][::-1]
        if rope is None:
            x_ref, g_ref = refs
        else:
            x_ref, g_ref, c_ref, s_ref = refs
        x = x_ref[...].astype(F32)
        bd = _block_diag(width)
        r = lax.rsqrt(_head_mean(x * x, bd) + NORM_EPS)
        xh = x * r
        dy = jnp.concatenate([d[...].astype(F32) for d in do_refs], axis=1) if len(do_refs) > 1 else do_refs[0][...].astype(F32)
        if rope is not None:
            dy = dy * jnp.tile(c_ref[...], (1, reps)) + _partner(dy * jnp.tile(s_ref[...], (1, reps)))
        dxh = dy * g_ref[...]
        dx_ref[...] = (r * (dxh - xh * _head_mean(dxh * xh, bd))).astype(dx_ref.dtype)

        @pl.when(pl.program_id(0) == 0)
        def _():
            dg_ref[...] = jnp.zeros_like(dg_ref)

        dg_ref[...] += jnp.sum(dy * xh, axis=0, keepdims=True)

    xs = pl.BlockSpec((ts, width), lambda i: (i, cb))
    row = pl.BlockSpec((ts, width), lambda i: (i, 0))
    vec = pl.BlockSpec((1, width), lambda i: (0, 0))
    tab = pl.BlockSpec((ts, LANES), lambda i: (i, 0))
    ins = [src, g] + ([] if rope is None else list(rope)) + douts
    specs = [xs, vec] + ([] if rope is None else [tab, tab]) + [pl.BlockSpec((ts, piece), lambda i: (i, 0))] * len(douts)
    dx, dg = _pcall(kern, name=name, grid=(s // ts,), in_specs=specs, out_specs=[row, vec],
                    out_shape=[jax.ShapeDtypeStruct((s, width), BF16), jax.ShapeDtypeStruct((1, width), F32)],
                    compiler_params=_params("arbitrary"))(*ins)
    return dx, jnp.sum(dg.reshape(width // HEAD_DIM, HEAD_DIM), axis=0, keepdims=True)


def _tri(strict, n):
    r = lax.broadcasted_iota(jnp.int32, (2 * n, n), 0) % n
    c = lax.broadcasted_iota(jnp.int32, (2 * n, n), 1)
    return jnp.where((r > c) if strict else (r >= c), 1.0, 0.0).astype(BF16)


def _split_dot(v, t2):
    hi = v.astype(BF16)
    lo = (v - hi.astype(F32)).astype(BF16)
    return lax.dot_general(jnp.concatenate([hi, lo], axis=1), t2, (((1,), (0,)), ((), ())), preferred_element_type=F32)


LOG2E = 1.4426950408889634


def _log2_sigmoids(z2):
    lf = -(jnp.maximum(z2, 0.0) + jnp.log2(1.0 + jnp.exp2(-jnp.abs(z2))))
    return z2 + lf, lf


SB2_SUB = 2
SB_KT = 128


def _first_half(shape):
    return lax.broadcasted_iota(jnp.int32, shape, 1) < HEAD_DIM


def _split_pair(t, first):
    zero = jnp.zeros_like(t)
    return [jnp.where(first, t, zero), jnp.where(first, zero, t)]


def _sb2_fwd(p, *, name, side=None):
    s = p.shape[0]
    rq = SB2_SUB * QB
    nq = s // rq
    npair = SB_W // LANES

    def kern(q_ref, k_ref, v_ref, o_ref):
        i = pl.program_id(1)
        first = _first_half((rq, LANES))
        q2 = jnp.concatenate(_split_pair(q_ref[...], first), axis=0)
        t2 = _tri(True, SB_KT)
        rel = lax.broadcasted_iota(jnp.int32, (2 * rq, SB_KT), 1) - lax.broadcasted_iota(jnp.int32, (2 * rq, SB_KT), 0) % rq

        def tile(j, q, rel, carry, acc, masked):
            off = pl.multiple_of(j * SB_KT, SB_KT)
            ls, lf = _log2_sigmoids(_nt(q, k_ref[pl.ds(off, SB_KT), :]) * (SCALE * LOG2E))
            if masked:
                before = rel < i * rq - j * SB_KT
                lf = jnp.where(before, lf, 0.0)
            w = jnp.exp2(ls + _split_dot(lf, t2) + carry)
            if masked:
                w = jnp.where(before, w, 0.0)
            return carry + jnp.sum(lf, axis=1, keepdims=True), acc + _nn(w, v_ref[pl.ds(off, SB_KT), :])

        carry, acc = jnp.zeros((2 * rq, 1), F32), jnp.zeros((2 * rq, LANES), F32)
        for a in range(rq // SB_KT):
            carry, acc = tile(i * (rq // SB_KT) + (rq // SB_KT - 1 - a), q2, rel, carry, acc, True)

        def cond(st):
            return jnp.logical_and(st[0] >= 0, st[1] > 0)

        def body(st):
            carry, acc = tile(st[0], q2, rel, st[2], st[3], False)
            return st[0] - 1, (jnp.max(carry) > SB_DEAD).astype(jnp.int32), carry, acc

        st = lax.while_loop(cond, body, (i * (rq // SB_KT) - 1, jnp.int32(1), carry, acc))
        o_ref[...] = jnp.where(first, st[3][:rq], st[3][rq:])

    outs = _call_2d(kern, name=name, grid=(npair, nq),
                    in_specs=[pl.BlockSpec((rq, LANES), lambda a, i: (i, a)), pl.BlockSpec((s, LANES), lambda a, i: (0, npair + a)),
                              pl.BlockSpec((s, LANES), lambda a, i: (0, 2 * npair + a))],
                    out_specs=[pl.BlockSpec((rq, LANES), lambda a, i: (i, a))], out_shape=[jax.ShapeDtypeStruct((s, SB_W), F32)],
                    ins=[p, p, p], semantics=("parallel", "arbitrary"), side=side)
    return outs[0] if side is None else (outs[0][0], outs[1])


def _sb2_bwd(p, o, do, *, name, side=None):
    s = p.shape[0]
    rq = SB2_SUB * QB
    nq = s // rq
    npair = SB_W // LANES

    def kern(q_ref, k_ref, v_ref, o_ref, do_ref, dq_ref, dk_hbm, dv_hbm, dk_acc, dv_acc, sem):
        pr = pl.program_id(0)
        i = pl.program_id(1)

        @pl.when(i == 0)
        def _():
            dk_acc[...] = jnp.zeros_like(dk_acc)
            dv_acc[...] = jnp.zeros_like(dv_acc)

        first = _first_half((rq, LANES))
        q2 = jnp.concatenate(_split_pair(q_ref[...], first), axis=0)
        do2 = jnp.concatenate(_split_pair(do_ref[...], first), axis=0)
        o2 = o_ref[...]
        dsum = jnp.sum(do2.astype(F32) * jnp.concatenate([o2, o2], axis=0), axis=1, keepdims=True)
        t_strict = _tri(True, SB_KT)
        t_incl = _tri(False, SB_KT)
        rel = lax.broadcasted_iota(jnp.int32, (2 * rq, SB_KT), 1) - lax.broadcasted_iota(jnp.int32, (2 * rq, SB_KT), 0) % rq

        def tile(j, rows, carry, gcarry, dq, masked):
            q, dob, dsm, rel = rows
            off = pl.multiple_of(j * SB_KT, SB_KT)
            kt = k_ref[pl.ds(off, SB_KT), :]
            ls, lf = _log2_sigmoids(_nt(q, kt) * (SCALE * LOG2E))
            if masked:
                before = rel < i * rq - j * SB_KT
                lf = jnp.where(before, lf, 0.0)
            w = jnp.exp2(ls + _split_dot(lf, t_strict) + carry)
            if masked:
                w = jnp.where(before, w, 0.0)
            wr = w.astype(MXU_DT)
            g = _nt(dob, v_ref[pl.ds(off, SB_KT), :]) * wr.astype(F32)
            big_g = dsm - (_split_dot(g, t_incl) + gcarry)
            sig = jnp.exp2(ls)
            dz = g * (1.0 - sig) - sig * big_g
            if masked:
                dz = jnp.where(before, dz, 0.0)
            dz = dz * SCALE
            dk_acc[pl.ds(off, SB_KT), :] += _tn(dz, q)
            dv_acc[pl.ds(off, SB_KT), :] += _tn(wr, dob)
            return (carry + jnp.sum(lf, axis=1, keepdims=True), gcarry + jnp.sum(g, axis=1, keepdims=True),
                    dq + _nn(dz, kt))

        zc = jnp.zeros((2 * rq, 1), F32)
        carry, gcarry, dq = zc, zc, jnp.zeros((2 * rq, LANES), F32)
        whole = (q2, do2, dsum, rel)
        for a in range(rq // SB_KT):
            carry, gcarry, dq = tile(i * (rq // SB_KT) + (rq // SB_KT - 1 - a), whole, carry, gcarry, dq, True)

        def cond(st):
            return jnp.logical_and(st[0] >= 0, st[1] > 0)

        def body(st):
            carry, gcarry, dq = tile(st[0], whole, st[2], st[3], st[4], False)
            return st[0] - 1, (jnp.max(carry) > SB_DEAD).astype(jnp.int32), carry, gcarry, dq

        st = lax.while_loop(cond, body, (i * (rq // SB_KT) - 1, jnp.int32(1), carry, gcarry, dq))
        dq_ref[...] = jnp.where(first, st[4][:rq], st[4][rq:]).astype(dq_ref.dtype)

        @pl.when(i == nq - 1)
        def _():
            cols = pl.ds(pl.multiple_of(pr * LANES, LANES), LANES)
            ck = pltpu.make_async_copy(dk_acc, dk_hbm.at[:, cols], sem.at[0])
            cv = pltpu.make_async_copy(dv_acc, dv_hbm.at[:, cols], sem.at[1])
            ck.start()
            cv.start()
            ck.wait()
            cv.wait()

    blk = pl.BlockSpec((rq, LANES), lambda a, i: (i, a))
    anyspace = pl.BlockSpec(memory_space=pl.ANY)
    shp = jax.ShapeDtypeStruct((s, SB_W), F32)
    return _call_2d(kern, name=name, grid=(npair, nq),
                    in_specs=[blk, pl.BlockSpec((s, LANES), lambda a, i: (0, npair + a)),
                              pl.BlockSpec((s, LANES), lambda a, i: (0, 2 * npair + a)), blk, blk],
                    out_specs=[blk, anyspace, anyspace], out_shape=[jax.ShapeDtypeStruct((s, SB_W), BF16), shp, shp], ins=[p, p, p, o, do],
                    scratch_shapes=[pltpu.VMEM((s, LANES), F32), pltpu.VMEM((s, LANES), F32), pltpu.SemaphoreType.DMA((2,))],
                    semantics=("arbitrary", "arbitrary"), side=side)


def _dsa_rel():
    qi = lax.broadcasted_iota(jnp.int32, (QB, QB), 0)
    kj = lax.broadcasted_iota(jnp.int32, (QB, QB), 1)
    return kj - qi


def _prev_mask(rel, has_prev):
    return rel >= jnp.where(has_prev, 0, QB)


DSA_BT = QB * max(r for _, r in DSA_GROUPS)
DSA_UB = 4


def _bdot(a, b, ca, cb):
    return lax.dot_general(a.astype(MXU_DT), b.astype(MXU_DT), (((ca,), (cb,)), ((0,), (0,))), preferred_element_type=F32)


def _bnt(a, b):
    return _bdot(a, b, 2, 2)


def _bnn(a, b):
    return _bdot(a, b, 2, 1)


def _btn(a, b):
    return _bdot(a, b, 1, 1)


def _unit_rows(r, c, b):
    return pl.ds(c + QB * r * b, QB, stride=r)


def _pair_cols(t, first):
    return [jnp.max(jnp.where(first, t, -jnp.inf), axis=1, keepdims=True),
            jnp.max(jnp.where(first, -jnp.inf, t), axis=1, keepdims=True)]


def _dsa2_fwd(qn, kn, v32, g, *, name):
    s = qn.shape[0]
    r = DSA_GROUPS[g][1]
    nbk = DSA_BT // (QB * r)
    npair = DSA_OUT_W // LANES

    def kern(q_ref, k_ref, kp_ref, v_ref, vp_ref, o_ref, l_ref):
        t = pl.program_id(1)
        first = _first_half((QB, LANES))
        rel = _dsa_rel()
        units = [(c, b) for c in range(r) for b in range(nbk)]
        for u0 in range(0, len(units), DSA_UB):
            batch = units[u0:u0 + DSA_UB]
            qs, kcs, vcs, kps, vps, masks = [], [], [], [], [], []
            for c, b in batch:
                rows = _unit_rows(r, c, b)
                kc, vc = k_ref[rows, :].astype(MXU_DT), v_ref[rows, :].astype(MXU_DT)
                if b > 0:
                    prow = _unit_rows(r, c, b - 1)
                    kpv, vpv, has_prev = k_ref[prow, :], v_ref[prow, :], True
                else:
                    prow = _unit_rows(r, c, nbk - 1)
                    kpv, vpv, has_prev = kp_ref[prow, :], vp_ref[prow, :], t > 0
                for qe in _split_pair(q_ref[rows, :], first):
                    qs.append(qe.astype(MXU_DT))
                    kcs.append(kc)
                    vcs.append(vc)
                    kps.append(kpv.astype(MXU_DT))
                    vps.append(vpv.astype(MXU_DT))
                    masks.append(_prev_mask(rel, has_prev))
            qq = jnp.stack(qs)
            sc = jnp.where(rel <= 0, _bnt(qq, jnp.stack(kcs)) * SCALE, -jnp.inf)
            sp = _bnt(qq, jnp.stack(kps)) * SCALE
            sp = jnp.stack([jnp.where(mk, sp[n], -jnp.inf) for n, mk in enumerate(masks)])
            m = jnp.maximum(jnp.max(sc, axis=2, keepdims=True), jnp.max(sp, axis=2, keepdims=True))
            pc = jnp.exp(sc - m)
            pp = jnp.exp(sp - m)
            den = jnp.sum(pc, axis=2, keepdims=True) + jnp.sum(pp, axis=2, keepdims=True)
            out = (_bnn(pc, jnp.stack(vcs)) + _bnn(pp, jnp.stack(vps))) / den
            lse = m + jnp.log(den)
            for idx, (c, b) in enumerate(batch):
                rows = _unit_rows(r, c, b)
                o_ref[rows, :] = jnp.where(first, out[2 * idx], out[2 * idx + 1])
                l_ref[rows, :] = jnp.where(first, lse[2 * idx], lse[2 * idx + 1])

    npg = DSA_HPG * HEAD_DIM // LANES
    cur = pl.BlockSpec((DSA_BT, LANES), lambda a, t: (t, npg * g + a))
    prev = pl.BlockSpec((DSA_BT, LANES), lambda a, t: (jnp.maximum(t - 1, 0), npg * g + a))
    out = pl.BlockSpec((DSA_BT, LANES), lambda a, t: (t, a))
    shp = jax.ShapeDtypeStruct((s, DSA_OUT_W), F32)
    return _pcall(kern, name=name, grid=(npair, s // DSA_BT), in_specs=[cur, cur, prev, cur, prev], out_specs=[out, out],
                  out_shape=[shp, shp], compiler_params=_params("parallel", "parallel"))(qn, kn, kn, v32, v32)


def _dsa2_combine(parts, *, name):
    s, wd = parts[0][0].shape
    ts = _pick(s, (512, 256))

    def kern(o0, l0, o1, l1, o2, l2, o_ref, l_ref):
        ls = [l0[...], l1[...], l2[...]]
        m = jnp.maximum(jnp.maximum(ls[0], ls[1]), ls[2])
        es = [jnp.exp(l - m) for l in ls]
        den = es[0] + es[1] + es[2]
        o_ref[...] = (es[0] * o0[...] + es[1] * o1[...] + es[2] * o2[...]) / den
        l_ref[...] = m + jnp.log(den)

    blk = pl.BlockSpec((ts, wd), lambda i: (i, 0))
    shp = jax.ShapeDtypeStruct((s, wd), F32)
    flat = [t for pair in parts for t in pair]
    return _pcall(kern, name=name, grid=(s // ts,), in_specs=[blk] * 6, out_specs=[blk, blk], out_shape=[shp, shp],
                  compiler_params=_params("parallel"))(*flat)


def _dsa2_prep(o, do, *, name):
    s, wd = o.shape
    ts = _pick(s, (512, 256))

    def kern(o_ref, do_ref, d_ref):
        d_ref[...] = _head_mean(do_ref[...] * o_ref[...], _block_diag(wd)) * HEAD_DIM

    blk = pl.BlockSpec((ts, wd), lambda i: (i, 0))
    return _pcall(kern, name=name, grid=(s // ts,), in_specs=[blk, blk], out_specs=blk,
                  out_shape=jax.ShapeDtypeStruct((s, wd), F32), compiler_params=_params("parallel"))(o, do)


def _dsa2_bwd(qn, kn, v32, do, lse, dd, g, *, name):
    s = qn.shape[0]
    r = DSA_GROUPS[g][1]
    nbk = DSA_BT // (QB * r)
    npair = DSA_OUT_W // LANES
    nsteps = s // DSA_BT

    def kern(q_ref, qn_ref, k_ref, kp_ref, v_ref, vp_ref, do_ref, don_ref, l_ref, ln_ref, d_ref, dn_ref,
             dq_ref, dk_ref, dv_ref):
        t = pl.program_id(1)
        first = _first_half((QB, LANES))
        rel = _dsa_rel()

        def pairs(items):
            qq = jnp.stack([it[0].astype(MXU_DT) for it in items])
            dd = jnp.stack([it[1].astype(MXU_DT) for it in items])
            kk = jnp.stack([it[4].astype(MXU_DT) for it in items])
            vv = jnp.stack([it[5].astype(MXU_DT) for it in items])
            p = jnp.exp(_bnt(qq, kk) * SCALE - jnp.stack([it[2] for it in items]))
            p = jnp.stack([jnp.where(it[6], p[n], 0.0) for n, it in enumerate(items)])
            ds = p * (_bnt(dd, vv) - jnp.stack([it[3] for it in items])) * SCALE
            return _bnn(ds, kk), _btn(ds, qq), _btn(p, dd)

        def heads(rows, qr, dor, lr, dr):
            return list(zip(_split_pair(qr[rows, :], first), _split_pair(dor[rows, :], first),
                            _pair_cols(lr[rows, :], first), _pair_cols(dr[rows, :], first)))

        units = [(c, b) for c in range(r) for b in range(nbk)]
        dk_of, dv_of = [None] * len(units), [None] * len(units)
        for u0 in range(0, len(units), DSA_UB // 2):
            batch = list(enumerate(units))[u0:u0 + DSA_UB // 2]
            items = []
            for u, (c, b) in batch:
                rows = _unit_rows(r, c, b)
                kc, vc = k_ref[rows, :], v_ref[rows, :]
                if b > 0:
                    prow = _unit_rows(r, c, b - 1)
                    kpv, vpv, pmask = k_ref[prow, :], v_ref[prow, :], _prev_mask(rel, True)
                else:
                    prow = _unit_rows(r, c, nbk - 1)
                    kpv, vpv, pmask = kp_ref[prow, :], vp_ref[prow, :], _prev_mask(rel, t > 0)
                for hd in heads(rows, q_ref, do_ref, l_ref, d_ref):
                    items.append(hd + (kc, vc, rel <= 0))
                    items.append(hd + (kpv, vpv, pmask))
            dq, dk, dv = pairs(items)
            for n, (u, (c, b)) in enumerate(batch):
                dq_ref[_unit_rows(r, c, b), :] = jnp.where(first, dq[4 * n] + dq[4 * n + 1], dq[4 * n + 2] + dq[4 * n + 3])
                dk_of[u] = dk[4 * n] + dk[4 * n + 2]
                dv_of[u] = dv[4 * n] + dv[4 * n + 2]
                if b > 0:
                    dk_of[u - 1] = dk_of[u - 1] + (dk[4 * n + 1] + dk[4 * n + 3])
                    dv_of[u - 1] = dv_of[u - 1] + (dv[4 * n + 1] + dv[4 * n + 3])
        lasts = [c * nbk + nbk - 1 for c in range(r)]
        for c0 in range(0, r, DSA_UB):
            chunk = list(range(c0, min(c0 + DSA_UB, r)))
            items = []
            for c in chunk:
                last = _unit_rows(r, c, nbk - 1)
                for hd in heads(_unit_rows(r, c, 0), qn_ref, don_ref, ln_ref, dn_ref):
                    items.append(hd + (k_ref[last, :], v_ref[last, :], _prev_mask(rel, t < nsteps - 1)))
            _, dk, dv = pairs(items)
            for n, c in enumerate(chunk):
                dk_of[lasts[c]] = dk_of[lasts[c]] + (dk[2 * n] + dk[2 * n + 1])
                dv_of[lasts[c]] = dv_of[lasts[c]] + (dv[2 * n] + dv[2 * n + 1])
        for u, (c, b) in enumerate(units):
            dk_ref[_unit_rows(r, c, b), :] = dk_of[u]
            dv_ref[_unit_rows(r, c, b), :] = dv_of[u]

    npg = DSA_HPG * HEAD_DIM // LANES

    def at(shift, col):
        return pl.BlockSpec((DSA_BT, LANES), lambda a, t: (jnp.clip(t + shift, 0, nsteps - 1), col(a)))

    gcol = lambda a: npg * g + a
    ocol = lambda a: a
    specs = [at(0, gcol), at(1, gcol), at(0, gcol), at(-1, gcol), at(0, gcol), at(-1, gcol),
             at(0, ocol), at(1, ocol), at(0, ocol), at(1, ocol), at(0, ocol), at(1, ocol)]
    shp = jax.ShapeDtypeStruct((s, DSA_OUT_W), F32)
    return _pcall(kern, name=name, grid=(npair, nsteps), in_specs=specs, out_specs=[at(0, ocol)] * 3, out_shape=[shp, shp, shp],
                  compiler_params=_params("parallel", "parallel"))(qn, qn, kn, kn, v32, v32, do, do, lse, lse, dd, dd)


def _mem2_fwd(qn, km, kv, *, name):
    s = qn.shape[0]
    ml = km.shape[0]
    tq = _pick(s, (512, 256))
    npair = MEM_W // LANES

    def kern(q_ref, k_ref, v_ref, o_ref):
        first = _first_half((tq, LANES))
        q2 = jnp.concatenate(_split_pair(q_ref[...], first), axis=0)
        sc = _nt(q2, k_ref[...]) * SCALE
        e = jnp.exp(sc - jnp.max(sc, axis=1, keepdims=True))
        o2 = _nn(e / jnp.sum(e, axis=1, keepdims=True), v_ref[...])
        o_ref[...] = jnp.where(first, o2[:tq], o2[tq:])

    blk = pl.BlockSpec((tq, LANES), lambda a, i: (i, a))
    return _pcall(kern, name=name, grid=(npair, s // tq),
                  in_specs=[blk, pl.BlockSpec((ml, LANES), lambda a, i: (0, a)), pl.BlockSpec((ml, LANES), lambda a, i: (0, npair + a))],
                  out_specs=blk, out_shape=jax.ShapeDtypeStruct((s, MEM_W), F32),
                  compiler_params=_params("parallel", "parallel"))(qn, km, kv)


def _mem2_bwd(qn, km, kv, do, *, name):
    s = qn.shape[0]
    ml = km.shape[0]
    tq = _pick(s, (512, 256))
    npair = MEM_W // LANES

    def kern(q_ref, k_ref, v_ref, do_ref, dq_ref, dk_ref, dv_ref):
        @pl.when(pl.program_id(1) == 0)
        def _():
            dk_ref[...] = jnp.zeros_like(dk_ref)
            dv_ref[...] = jnp.zeros_like(dv_ref)

        first = _first_half((tq, LANES))
        q2 = jnp.concatenate(_split_pair(q_ref[...], first), axis=0)
        do2 = jnp.concatenate(_split_pair(do_ref[...], first), axis=0)
        sc = _nt(q2, k_ref[...]) * SCALE
        e = jnp.exp(sc - jnp.max(sc, axis=1, keepdims=True))
        p = e / jnp.sum(e, axis=1, keepdims=True)
        dp = _nt(do2, v_ref[...])
        ds = p * (dp - jnp.sum(p * dp, axis=1, keepdims=True)) * SCALE
        dq2 = _nn(ds, k_ref[...])
        dk_ref[...] += _tn(ds, q2)
        dv_ref[...] += _tn(p, do2)
        dq_ref[...] = jnp.where(first, dq2[:tq], dq2[tq:])

    blk = pl.BlockSpec((tq, LANES), lambda a, i: (i, a))
    kblk = pl.BlockSpec((ml, LANES), lambda a, i: (0, a))
    kshape = jax.ShapeDtypeStruct((ml, MEM_W), F32)
    return _pcall(kern, name=name, grid=(npair, s // tq),
                  in_specs=[blk, kblk, pl.BlockSpec((ml, LANES), lambda a, i: (0, npair + a)), blk],
                  out_specs=[blk, kblk, kblk], out_shape=[jax.ShapeDtypeStruct((s, MEM_W), F32), kshape, kshape],
                  compiler_params=_params("parallel", "arbitrary"))(qn, km, kv, do)


def _merge_fwd(logits, bias, ya, yb, yc, *, name):
    s, d = ya.shape
    ts = _pick(s, (512, 256))

    def kern(l0, l1, l2, b0, b1, b2, a_ref, b_ref, c_ref, o_ref):
        m = 0.0
        for l_ref, bb_ref, y_ref in ((l0, b0, a_ref), (l1, b1, b_ref), (l2, b2, c_ref)):
            m = m + _sigmoid(l_ref[...].astype(F32) + bb_ref[...]) * y_ref[...].astype(F32)
        o_ref[...] = m.astype(o_ref.dtype)

    row = pl.BlockSpec((ts, d), lambda i: (i, 0))
    lg = [pl.BlockSpec((ts, d), functools.partial(lambda i, c: (i, c), c=c)) for c in range(3)]
    bs = [pl.BlockSpec((1, d), functools.partial(lambda i, c: (0, c), c=c)) for c in range(3)]
    return _pcall(kern, name=name, grid=(s // ts,), in_specs=lg + bs + [row, row, row], out_specs=row,
                  out_shape=jax.ShapeDtypeStruct((s, d), BF16),
                  compiler_params=_params("parallel"))(logits, logits, logits, bias, bias, bias, ya, yb, yc)


def _merge_bwd(logits, bias, ya, yb, yc, dm, *, name):
    s, d = ya.shape
    ts = _pick(s, (256,))

    def kern(l0, l1, l2, b0, b1, b2, a_ref, b_ref, c_ref, dm_ref, da_ref, db_ref, dc_ref, dl_ref, dbias_ref):
        dmv = dm_ref[...]

        @pl.when(pl.program_id(0) == 0)
        def _():
            dbias_ref[...] = jnp.zeros_like(dbias_ref)

        for c, (l_ref, bb_ref, y_ref, dy_ref) in enumerate(((l0, b0, a_ref, da_ref), (l1, b1, b_ref, db_ref), (l2, b2, c_ref, dc_ref))):
            g = _sigmoid(l_ref[...].astype(F32) + bb_ref[...])
            dy_ref[...] = (dmv * g).astype(dy_ref.dtype)
            dl = dmv * y_ref[...].astype(F32) * g * (1.0 - g)
            dl_ref[:, c * d:(c + 1) * d] = dl.astype(dl_ref.dtype)
            dbias_ref[:, c * d:(c + 1) * d] += jnp.sum(dl, axis=0, keepdims=True)

    row = pl.BlockSpec((ts, d), lambda i: (i, 0))
    lg = [pl.BlockSpec((ts, d), functools.partial(lambda i, c: (i, c), c=c)) for c in range(3)]
    bs = [pl.BlockSpec((1, d), functools.partial(lambda i, c: (0, c), c=c)) for c in range(3)]
    yshape = jax.ShapeDtypeStruct((s, d), BF16)
    return _pcall(kern, name=name, grid=(s // ts,), in_specs=lg + bs + [row, row, row, row],
                  out_specs=[row, row, row, pl.BlockSpec((ts, 3 * d), lambda i: (i, 0)), pl.BlockSpec((1, 3 * d), lambda i: (0, 0))],
                  out_shape=[yshape] * 3 + [jax.ShapeDtypeStruct((s, 3 * d), BF16), jax.ShapeDtypeStruct((1, 3 * d), F32)],
                  compiler_params=_params("arbitrary"))(logits, logits, logits, bias, bias, bias, ya, yb, yc, dm)


G_FFN1 = ['ffn1_w1', 'ffn1_w3', 'ffn1_w2']
G_FFN2 = ['ffn2_w1', 'ffn2_w3', 'ffn2_w2']
G_MID = [n for n in BIG if n not in G_FFN1 + G_FFN2]


def _ffn_fwd(h, w1, w3, w2, tag, epilogue, side=None):
    carried = None
    if side is None:
        a, b, f = _ffn_up(h, w1, w3, name=f"{tag}_up")
    else:
        (a, b, f), carried = _ffn_up(h, w1, w3, name=f"{tag}_up", side=side)
    if callable(w2):
        w2 = w2(carried)
    outs = _matmul(f, w2, name=f"{tag}_down", alpha=0.5, tm=512, tn=1024, tk=2816, epilogue=epilogue)
    return outs, (h, a, b, f), carried


def _ffn_bwd(x, norm, w1, w3, w2, saved, dy, dyb, tag, side=None, own_side=None):
    h, a, b, f = saved
    dw2 = _matmul(f, dyb, name=f"{tag}_dw2", ta=True, alpha=0.5, tm=1408, tn=1024, tk=2048)
    carried = None
    if side is None:
        da, db = _ffn_dact(dyb, w2, a, b, name=f"{tag}_dact")
    else:
        (da, db), carried = _ffn_dact(dyb, w2, a, b, name=f"{tag}_dact", side=side)
    dw1 = _matmul(h, da, name=f"{tag}_dw1", ta=True, tm=1024, tn=1408, tk=2048)
    dw3 = _matmul(h, db, name=f"{tag}_dw3", ta=True, tm=1024, tn=1408, tk=2048)
    outs = _matmul(da, w1, name=f"{tag}_dh", tb=True, tm=512, tn=1024, tk=1408, pair2=(db, w3),
                   epilogue=(_epi_rms_bwd, [x, dy], [norm], [F32, BF16], 1),
                   side=None if own_side is None else own_side(dw1, dw3, dw2))
    (dx, dxb, dnorm), own = outs if own_side is not None else (outs, None)
    return dx, dxb, dnorm, dw1, dw3, dw2, carried, own


def _local_step(x, mem, loss_target, wl, ws):
    s, d = x.shape
    assert s % (QB * 16) == 0
    rope = _rope_tables(s)
    bf = {n: wl[n].astype(BF16) for n in BIG}
    w = dict(ws)

    def gather(names):
        return _side([bf[n] for n in names], _two_level_phases())

    def whole(names, gathered):
        return {n: _whole_weight(n, t) for n, t in zip(names, gathered)}

    first_needed = ['ffn1_w1', 'ffn1_w3']
    then_needed = ['ffn1_w2'] + G_MID
    h1, early = _rms_fwd(x, w['ffn1_norm'], name="ffn1_rms", side=gather(first_needed))
    w.update(whole(first_needed, early))
    (x1, h), sv1, late = _ffn_fwd(h1, w['ffn1_w1'], w['ffn1_w3'], lambda got: _whole_weight('ffn1_w2', got[0]), "ffn1",
                                  (_epi_residual_rms, [x], [w['mix_norm']], [F32, BF16], 0),
                                  side=gather(then_needed))
    w.update(whole(then_needed, late))
    p = _matmul(h, w['w_in'], name="in_proj", out_dtype=BF16, tn=1024)
    logits = _matmul(h, w['w_gate'], name="gate_proj", out_dtype=BF16, tn=1024)
    c_qb, c_kb, c_vb, c_qc = 3 * SB_W, 3 * SB_W + DSA_W, 3 * SB_W + 2 * DSA_W, 3 * SB_W + 3 * DSA_W

    oa_t, late = _sb2_fwd(p, name="sb_fwd", side=gather(G_FFN2))
    w.update(whole(G_FFN2, late))
    ya = _matmul(oa_t, w['w_branch_sb'], name="sb_out", out_dtype=BF16)

    qb_n = _qknorm_fwd(p, c_qb, DSA_W, w['qn_dsa'], rope, name="dsa_qnorm", out_dtype=F32)
    kb_n = _qknorm_fwd(p, c_kb, DSA_W, w['kn_dsa'], rope, name="dsa_knorm", out_dtype=F32)
    vb32 = p[:, c_vb:c_vb + DSA_W].astype(F32)
    groups = range(len(DSA_GROUPS))
    ob_t, lse_b = _dsa2_combine([_dsa2_fwd(qb_n, kb_n, vb32, gi, name=f"dsa_fwd{gi}") for gi in groups], name="dsa_combine")
    yb = _matmul(ob_t, w['w_branch_dsa'], name="dsa_out", out_dtype=BF16)

    memh = _rms_fwd(mem, w['mem_norm'], name="mem_rms")
    kv = _matmul(memh, w['w_mem_kv'], name="mem_kv", out_dtype=BF16)
    km_n = _qknorm_fwd(kv, 0, MEM_W, w['kn_mem'], None, name="mem_knorm")
    qc_n = _qknorm_fwd(p, c_qc, MEM_W, w['qn_mem'], None, name="mem_qnorm")
    oc_t = _mem2_fwd(qc_n, km_n, kv, name="mem_fwd")
    yc = _matmul(oc_t, w['w_branch_mem'], name="mem_out", out_dtype=BF16)

    merged = _merge_fwd(logits, w['b_gate'], ya, yb, yc, name="merge")
    x2, h2 = _matmul(merged, w['w_out'], name="out_proj", tn=1024,
                     epilogue=(_epi_residual_rms, [x1], [w['ffn2_norm']], [F32, BF16], 0))
    (dx3, dx3b, sq), sv2, _ = _ffn_fwd(h2, w['ffn2_w1'], w['ffn2_w3'], w['ffn2_w2'], "ffn2",
                                       (_epi_loss, [x2, loss_target], [], [F32, BF16], 1))
    loss = jnp.sum(sq) * (0.5 / d)

    g, recv = {}, {}

    def owners(names):
        return [_for_owners(n, g[n], wl[n].shape) for n in names]

    dx2, dx2b, g['ffn2_norm'], g['ffn2_w1'], g['ffn2_w3'], g['ffn2_w2'], _, _ = _ffn_bwd(
        x2, w['ffn2_norm'], w['ffn2_w1'], w['ffn2_w3'], w['ffn2_w2'], sv2, dx3, dx3b, "ffn2")

    g['w_out'] = _matmul(merged, dx2b, name="d_w_out", ta=True, tn=1024, tk=512)
    dm = _matmul(dx2b, w['w_out'], name="d_merged", tb=True, tn=1024)
    dya, dyb, dyc, dlogits, g['b_gate'] = _merge_bwd(logits, w['b_gate'], ya, yb, yc, dm, name="d_merge")

    g['w_branch_sb'] = _matmul(oa_t, dya, name="d_w_sb", ta=True, tn=1024, tk=512)
    g['w_branch_dsa'] = _matmul(ob_t, dyb, name="d_w_dsa", ta=True, tk=512)
    g['w_branch_mem'] = _matmul(oc_t, dyc, name="d_w_mem", ta=True, tk=512)
    doa = _matmul(dya, w['w_branch_sb'], name="d_oa", tb=True, out_dtype=BF16)
    dob = _matmul(dyb, w['w_branch_dsa'], name="d_ob", tb=True)
    doc = _matmul(dyc, w['w_branch_mem'], name="d_oc", tb=True, out_dtype=BF16)

    (dqa, dka, dva), got = _sb2_bwd(p, oa_t, doa, name="sb_bwd", side=_side(owners(G_FFN2), _direct_phases(True)))
    recv.update(zip(G_FFN2, got))

    dd_b = _dsa2_prep(ob_t, dob, name="dsa_prep")
    dgrp = [_dsa2_bwd(qb_n, kb_n, vb32, dob, lse_b, dd_b, gi, name=f"dsa_bwd{gi}") for gi in groups]
    dvb = jnp.concatenate([t[2] for t in dgrp], axis=1).astype(BF16)
    dqb, g['qn_dsa'] = _qknorm_bwd(p, c_qb, DSA_W, w['qn_dsa'], rope, [t[0] for t in dgrp], name="d_dsa_qnorm")
    dkb, g['kn_dsa'] = _qknorm_bwd(p, c_kb, DSA_W, w['kn_dsa'], rope, [t[1] for t in dgrp], name="d_dsa_knorm")

    dqc_n, dkm_n, dvm = _mem2_bwd(qc_n, km_n, kv, doc, name="mem_bwd")
    dqc, g['qn_mem'] = _qknorm_bwd(p, c_qc, MEM_W, w['qn_mem'], None, dqc_n, name="d_mem_qnorm")
    dkm, g['kn_mem'] = _qknorm_bwd(kv, 0, MEM_W, w['kn_mem'], None, dkm_n, name="d_mem_knorm")
    dkv = jnp.concatenate([dkm, dvm.astype(BF16)], axis=1)
    g['w_mem_kv'] = _matmul(memh, dkv, name="d_w_mem_kv", ta=True)
    dmemh = _matmul(dkv, w['w_mem_kv'], name="d_memh", tb=True)
    _, _, g['mem_norm'] = _rms_bwd(mem, w['mem_norm'], dmemh, None, name="d_mem_rms")

    dp = jnp.concatenate([dqa.astype(BF16), dka.astype(BF16), dva.astype(BF16),
                          dqb, dkb, dvb, dqc], axis=1)
    g['w_in'] = _matmul(h, dp, name="d_w_in", ta=True, tn=2048, tk=1024)
    g['w_gate'] = _matmul(h, dlogits, name="d_w_gate", ta=True, tn=1536, tk=1024)
    dh = _matmul(dp, w['w_in'], name="d_h_in", tb=True, tn=1024, tk=2048)
    dx1, dx1b, g['mix_norm'] = _matmul(dlogits, w['w_gate'], name="d_h_gate", tb=True, tm=512, tn=1024, tk=3072,
                                       epilogue=(_epi_rms_bwd_sum, [dh, x1, dx2], [w['mix_norm']], [F32, BF16], 1))

    def own_side(dw1, dw3, dw2):
        g.update(ffn1_w1=dw1, ffn1_w3=dw3, ffn1_w2=dw2)
        return _side(owners(G_FFN1), _direct_phases(True))

    dx0, _, g['ffn1_norm'], _, _, _, got_mid, got_own = _ffn_bwd(
        x, w['ffn1_norm'], w['ffn1_w1'], w['ffn1_w3'], w['ffn1_w2'], sv1, dx1, dx1b, "ffn1",
        side=_side(owners(G_MID), _direct_phases(True)), own_side=own_side)
    recv.update(zip(G_MID, got_mid))
    recv.update(zip(G_FFN1, got_own))
    return loss, dx0, recv, {n: g[n] for n in SMALL}


def _whole_weight(name, gathered):
    _, r, c = gathered.shape
    return gathered.reshape(N_DEV * r, c) if SHARD_AXIS[name] == 0 else gathered.transpose(1, 0, 2).reshape(r, N_DEV * c)


def _for_owners(name, grad, shard_shape):
    r, c = shard_shape
    blk = grad.reshape(N_DEV, r, c) if SHARD_AXIS[name] == 0 else grad.reshape(r, N_DEV, c).transpose(1, 0, 2)
    return blk.astype(BF16)


def _pack_small(d, names, extra_rows):
    parts = []
    for n in names:
        v = d[n].reshape(-1)
        pad = (-v.size) % LANES
        parts.append(jnp.concatenate([v, jnp.zeros((pad,), v.dtype)]).reshape(-1, LANES))
    t = jnp.concatenate(parts, axis=0)
    return jnp.concatenate([t, jnp.zeros((extra_rows, LANES), t.dtype)], axis=0)


def _unpack_small(t, like, names):
    out, off = {}, 0
    for n in names:
        size = like[n].size
        rows = -(-size // LANES)
        out[n] = t[off:off + rows].reshape(-1)[:size].reshape(like[n].shape)
        off += rows
    return out


def _direct_phases(per_peer):
    def descriptors(src_ref, out_ref, send_sems, recv_sems, local_sem):
        x, y, c = lax.axis_index("x"), lax.axis_index("y"), lax.axis_index("c")
        me = 4 * x + 2 * y + c
        mine = pltpu.make_async_copy(src_ref.at[me] if per_peer else src_ref, out_ref.at[me], local_sem)
        copies = []
        for k in range(1, N_DEV):
            px = 1 - x if k & 4 else x
            py = 1 - y if k & 2 else y
            pc = 1 - c if k & 1 else c
            copies.append(pltpu.make_async_remote_copy(
                src_ref=src_ref.at[4 * px + 2 * py + pc] if per_peer else src_ref, dst_ref=out_ref.at[me],
                send_sem=send_sems.at[k - 1], recv_sem=recv_sems.at[k - 1],
                device_id=(px, py, pc), device_id_type=pl.DeviceIdType.MESH))
        return mine, copies

    def start(*refs):
        mine, copies = descriptors(*refs)
        mine.start()
        for cp in copies:
            cp.start()

    def forward(*refs):
        pass

    def finish(*refs):
        mine, copies = descriptors(*refs)
        for cp in copies:
            cp.wait_recv()
        for cp in copies:
            cp.wait_send()
        mine.wait()

    return start, forward, finish


def _exchange_parts(srcs, phases):
    n = len(srcs)
    shapes = [jax.ShapeDtypeStruct((N_DEV,) + tuple(s.shape[-2:]), s.dtype) for s in srcs]
    sems = [pltpu.SemaphoreType.DMA((n, N_DEV - 1)), pltpu.SemaphoreType.DMA((n, N_DEV - 1)), pltpu.SemaphoreType.DMA((n,))]

    def lift(phase):
        def run(src_refs, out_refs, send, recv, local):
            for a, (s_ref, o_ref) in enumerate(zip(src_refs, out_refs)):
                phase(s_ref, o_ref, send.at[a], recv.at[a], local.at[a])
        return run

    return shapes, sems, [lift(p) for p in phases]


def _exchange(srcs, phases, *, name):
    shapes, sems, runs = _exchange_parts(srcs, phases)
    n = len(srcs)

    def body(*refs):
        for run in runs:
            run(refs[:n], refs[n:2 * n], *refs[2 * n:])

    anyspace = pl.BlockSpec(memory_space=pl.ANY)
    return _pcall(body, name=name, in_specs=[anyspace] * n, out_specs=[anyspace] * n, out_shape=shapes, scratch_shapes=sems)(*srcs)


def _side(srcs, phases):
    shapes, sems, (start, forward, finish) = _exchange_parts(srcs, phases)

    def before(first, mid, ins, outs, scratch):
        pl.when(first)(lambda: start(ins, outs, *scratch))
        pl.when(mid)(lambda: forward(ins, outs, *scratch))

    def after(last, ins, outs, scratch):
        pl.when(last)(lambda: finish(ins, outs, *scratch))

    return list(srcs), shapes, sems, before, after


def _call_2d(kern, *, name, grid, in_specs, out_specs, out_shape, ins, scratch_shapes=(), semantics, side=None):
    if side is None:
        return _pcall(kern, name=name, grid=grid, in_specs=in_specs, out_specs=out_specs, out_shape=out_shape,
                      scratch_shapes=list(scratch_shapes), compiler_params=_params(*semantics))(*ins)
    s_ins, s_shapes, s_scratch, before, after = side
    n_in, n_out, n_scr = len(ins), len(out_shape), len(scratch_shapes)

    def combined(*refs):
        refs = list(refs)
        cut = [n_in, len(s_ins), n_out, len(s_shapes), n_scr, len(s_scratch)]
        parts, pos = [], 0
        for c in cut:
            parts.append(refs[pos:pos + c])
            pos += c
        m_in, c_in, m_out, c_out, m_scr, c_scr = parts
        ids = [pl.program_id(a) for a in range(len(grid))]
        inner_zero = functools.reduce(jnp.logical_and, [i == 0 for i in ids[1:]], True)
        first = jnp.logical_and(ids[0] == 0, inner_zero)
        mid = jnp.logical_and(ids[0] == grid[0] // 2, inner_zero)
        last = functools.reduce(jnp.logical_and, [i == n - 1 for i, n in zip(ids, grid)])
        before(first, mid, c_in, c_out, c_scr)
        kern(*m_in, *m_out, *m_scr)
        after(last, c_in, c_out, c_scr)

    anyspace = pl.BlockSpec(memory_space=pl.ANY)
    outs = _pcall(combined, name=name, grid=grid, in_specs=list(in_specs) + [anyspace] * len(s_ins),
                  out_specs=list(out_specs) + [anyspace] * len(s_shapes), out_shape=list(out_shape) + s_shapes,
                  scratch_shapes=list(scratch_shapes) + s_scratch, compiler_params=_params(*["arbitrary"] * len(grid)))(*ins, *s_ins)
    return outs[:n_out], outs[n_out:]


def _two_level_phases():
    def parts(src_ref, out_ref, send_sems, recv_sems, local_sem):
        x, y, c = lax.axis_index("x"), lax.axis_index("y"), lax.axis_index("c")
        me, sibling = (x, y, c), (x, y, 1 - c)
        chips = [(1 - x, y), (x, 1 - y), (1 - x, 1 - y)]

        def slab(px, py, pc):
            return out_ref.at[4 * px + 2 * py + pc]

        def copy(k, block, to, from_src=False):
            return pltpu.make_async_remote_copy(
                src_ref=src_ref if from_src else slab(*block), dst_ref=slab(*block),
                send_sem=send_sems.at[k], recv_sem=recv_sems.at[k], device_id=to, device_id_type=pl.DeviceIdType.MESH)

        return dict(
            mine=lambda: pltpu.make_async_copy(src_ref, slab(*me), local_sem),
            first=lambda: [copy(0, me, sibling, True)] + [copy(1 + j, me, (*chip, c), True) for j, chip in enumerate(chips)],
            passed=lambda: [copy(4 + j, (*chip, c), sibling) for j, chip in enumerate(chips)],
            landed=lambda: [copy(1 + j, (*chip, c), me) for j, chip in enumerate(chips)],
            late=lambda: [copy(0, sibling, me)] + [copy(4 + j, (*chip, 1 - c), me) for j, chip in enumerate(chips)])

    def start(*refs):
        make = parts(*refs)
        make['mine']().start()
        for cp in make['first']():
            cp.start()

    def forward(*refs):
        make = parts(*refs)
        for arrived, onward in zip(make['landed'](), make['passed']()):
            arrived.wait_recv()
            onward.start()

    def finish(*refs):
        make = parts(*refs)
        for cp in make['late']():
            cp.wait_recv()
        for cp in make['first']() + make['passed']():
            cp.wait_send()
        make['mine']().wait()

    return start, forward, finish


def _adamw(recv, w, m, v, *, name):
    rows, cols = w.shape
    tr = _pick(rows, (256, 128, 64))

    def kern(r_ref, w_ref, m_ref, v_ref, g_ref, d_ref, mo_ref, vo_ref):
        g = r_ref[0].astype(F32)
        for p in range(1, N_DEV):
            g = g + r_ref[p].astype(F32)
        mn = ADAM_B1 * m_ref[...] + (1.0 - ADAM_B1) * g
        vn = ADAM_B2 * v_ref[...] + (1.0 - ADAM_B2) * (g * g)
        m_hat = mn / (1.0 - ADAM_B1 ** ADAM_STEP)
        v_hat = vn / (1.0 - ADAM_B2 ** ADAM_STEP)
        g_ref[...] = g
        d_ref[...] = -ADAM_LR * (m_hat / (jnp.sqrt(v_hat) + ADAM_EPS) + ADAM_WD * w_ref[...])
        mo_ref[...] = mn
        vo_ref[...] = vn

    row = pl.BlockSpec((tr, cols), lambda i: (i, 0))
    shp = jax.ShapeDtypeStruct((rows, cols), F32)
    return _pcall(kern, name=name, grid=(rows // tr,), in_specs=[pl.BlockSpec((N_DEV, tr, cols), lambda i: (0, i, 0)), row, row, row],
                  out_specs=[row, row, row, row], out_shape=[shp, shp, shp, shp], compiler_params=_params("parallel"))(recv, w, m, v)


INPUTS = ['x', 'mem'] + WEIGHTS + ['loss_target'] + ['m_' + n for n in WEIGHTS] + ['v_' + n for n in WEIGHTS]
SMALL_PAD_ROWS = 4


def kernel(x, mem, ffn1_norm, ffn1_w1, ffn1_w3, ffn1_w2, mix_norm, mem_norm, w_in, w_mem_kv, qn_dsa, kn_dsa, qn_mem, kn_mem, w_branch_sb, w_branch_dsa, w_branch_mem, w_gate, b_gate, w_out, ffn2_norm, ffn2_w1, ffn2_w3, ffn2_w2, loss_target, m_ffn1_norm, m_ffn1_w1, m_ffn1_w3, m_ffn1_w2, m_mix_norm, m_mem_norm, m_w_in, m_w_mem_kv, m_qn_dsa, m_kn_dsa, m_qn_mem, m_kn_mem, m_w_branch_sb, m_w_branch_dsa, m_w_branch_mem, m_w_gate, m_b_gate, m_w_out, m_ffn2_norm, m_ffn2_w1, m_ffn2_w3, m_ffn2_w2, v_ffn1_norm, v_ffn1_w1, v_ffn1_w3, v_ffn1_w2, v_mix_norm, v_mem_norm, v_w_in, v_w_mem_kv, v_qn_dsa, v_kn_dsa, v_qn_mem, v_kn_mem, v_w_branch_sb, v_w_branch_dsa, v_w_branch_mem, v_w_gate, v_b_gate, v_w_out, v_ffn2_norm, v_ffn2_w1, v_ffn2_w3, v_ffn2_w2):
    given = dict(zip(INPUTS, (x, mem, ffn1_norm, ffn1_w1, ffn1_w3, ffn1_w2, mix_norm, mem_norm, w_in, w_mem_kv, qn_dsa, kn_dsa, qn_mem, kn_mem, w_branch_sb, w_branch_dsa, w_branch_mem, w_gate, b_gate, w_out, ffn2_norm, ffn2_w1, ffn2_w3, ffn2_w2, loss_target, m_ffn1_norm, m_ffn1_w1, m_ffn1_w3, m_ffn1_w2, m_mix_norm, m_mem_norm, m_w_in, m_w_mem_kv, m_qn_dsa, m_kn_dsa, m_qn_mem, m_kn_mem, m_w_branch_sb, m_w_branch_dsa, m_w_branch_mem, m_w_gate, m_b_gate, m_w_out, m_ffn2_norm, m_ffn2_w1, m_ffn2_w3, m_ffn2_w2, v_ffn1_norm, v_ffn1_w1, v_ffn1_w3, v_ffn1_w2, v_mix_norm, v_mem_norm, v_w_in, v_w_mem_kv, v_qn_dsa, v_kn_dsa, v_qn_mem, v_kn_mem, v_w_branch_sb, v_w_branch_dsa, v_w_branch_mem, v_w_gate, v_b_gate, v_w_out, v_ffn2_norm, v_ffn2_w1, v_ffn2_w3, v_ffn2_w2), strict=True))
    wl = {n: given[n][0] for n in BIG}
    ws = {n: given[n] for n in SMALL}

    loss, dx, recv, g = _local_step(x[0], mem[0], loss_target[0], wl, ws)

    big = [{}, {}, {}, {}]
    for n in G_FFN2 + G_MID + G_FFN1:
        outs = _adamw(recv[n], wl[n], given['m_' + n][0], given['v_' + n][0], name=f"adamw_{n}")
        for kind, t in enumerate(outs):
            big[kind][n] = t

    gs = _pack_small(g, SMALL, SMALL_PAD_ROWS)
    loss_row = gs.shape[0] - SMALL_PAD_ROWS
    gs = gs.at[loss_row, 0].set(loss)
    recv_s = _exchange([gs], _direct_phases(False), name="gather_small")[0]
    small = _adamw(recv_s, _pack_small(ws, SMALL, SMALL_PAD_ROWS), _pack_small({n: given['m_' + n] for n in SMALL}, SMALL, SMALL_PAD_ROWS),
                   _pack_small({n: given['v_' + n] for n in SMALL}, SMALL, SMALL_PAD_ROWS), name="adamw_replicated")
    total_loss = small[0][loss_row, 0]
    small = [_unpack_small(t, ws, SMALL) for t in small]

    outs = [total_loss, dx[None]]
    for kind in range(4):
        outs += [big[kind][n][None] if n in wl else small[kind][n] for n in WEIGHTS]
    return tuple(outs)
```

```python
import functools

import jax
import jax.numpy as jnp
from jax import lax
from jax.experimental import pallas as pl
from jax.experimental.pallas import tpu as pltpu

F32 = jnp.float32
BF16 = jnp.bfloat16
MXU_DT = jnp.bfloat16

N_DEV = 8
HEAD_DIM = 64
SB_HEADS = 8
DSA_GROUPS = ((128, 1), (512, 4), (2048, 16))
DSA_HPG = 4
MEM_HEADS = 4
SB_W = SB_HEADS * HEAD_DIM
DSA_W = DSA_HPG * len(DSA_GROUPS) * HEAD_DIM
DSA_OUT_W = DSA_HPG * HEAD_DIM
MEM_W = MEM_HEADS * HEAD_DIM
ROPE_THETA = 10000.0
NORM_EPS = 1e-6
QB = 128
SCALE = HEAD_DIM ** -0.5
ADAM_LR, ADAM_B1, ADAM_B2, ADAM_EPS, ADAM_WD, ADAM_STEP = 0.001, 0.9, 0.999, 1e-08, 0.01, 10

LANES = 128
VMEM_LIMIT = 48 * 1024 * 1024
SB_DEAD = -110.0 * 1.4426950408889634

WEIGHTS = ['ffn1_norm', 'ffn1_w1', 'ffn1_w3', 'ffn1_w2', 'mix_norm', 'mem_norm', 'w_in', 'w_mem_kv', 'qn_dsa', 'kn_dsa',
           'qn_mem', 'kn_mem', 'w_branch_sb', 'w_branch_dsa', 'w_branch_mem', 'w_gate', 'b_gate', 'w_out', 'ffn2_norm',
           'ffn2_w1', 'ffn2_w3', 'ffn2_w2']
SHARD_AXIS = {'ffn1_norm': None, 'ffn1_w1': 1, 'ffn1_w3': 1, 'ffn1_w2': 0, 'mix_norm': None, 'mem_norm': None, 'w_in': 1,
              'w_mem_kv': 0, 'qn_dsa': None, 'kn_dsa': None, 'qn_mem': None, 'kn_mem': None, 'w_branch_sb': 1,
              'w_branch_dsa': 1, 'w_branch_mem': 1, 'w_gate': 1, 'b_gate': None, 'w_out': 0, 'ffn2_norm': None,
              'ffn2_w1': 1, 'ffn2_w3': 1, 'ffn2_w2': 0}
BIG = [n for n in WEIGHTS if SHARD_AXIS[n] is not None]
SMALL = [n for n in WEIGHTS if SHARD_AXIS[n] is None]


def _pcall(kern, **kw):
    return pl.pallas_call(kern, **kw)


def _params(*sem):
    return pltpu.CompilerParams(dimension_semantics=sem, vmem_limit_bytes=VMEM_LIMIT)


def _dot(a, b, dims):
    return lax.dot_general(a.astype(MXU_DT), b.astype(MXU_DT), (dims, ((), ())), preferred_element_type=F32)


def _nn(a, b):
    return _dot(a, b, ((1,), (0,)))


def _nt(a, b):
    return _dot(a, b, ((1,), (1,)))


def _tn(a, b):
    return _dot(a, b, ((0,), (0,)))


def _pick(n, prefs):
    for p in prefs:
        if n % p == 0:
            return p
    return n


def _matmul(a, b, *, name, ta=False, tb=False, out_dtype=F32, res=None, alpha=1.0, tm=1024, tn=512, tk=1024, pair2=None,
            epilogue=None, side=None):
    if ta:
        kdim, m = a.shape
    else:
        m, kdim = a.shape
    n = b.shape[0] if tb else b.shape[1]
    tm = _pick(m, (tm, 512, 256, 128))
    tn = _pick(n, (tn, 512, 384, 256, 128))
    tk = _pick(kdim, (tk, 1024, 512, 256, 128))
    nk = kdim // tk
    a_spec = pl.BlockSpec((tk, tm), lambda i, j, k: (k, i)) if ta else pl.BlockSpec((tm, tk), lambda i, j, k: (i, k))
    b_spec = pl.BlockSpec((tn, tk), lambda i, j, k: (j, k)) if tb else pl.BlockSpec((tk, tn), lambda i, j, k: (k, j))
    o_spec = pl.BlockSpec((tm, tn), lambda i, j, k: (i, j))
    v_spec = pl.BlockSpec((1, tn), lambda i, j, k: (0, j))
    dims = ((0 if ta else 1,), (1 if tb else 0,))
    n_mm = 2 if pair2 is None else 4
    if epilogue is None:
        row_ins, vec_ins = ([] if res is None else [res]), []
        out_dtypes, n_vec = [out_dtype], 0
    else:
        assert tn == n and res is None
        epi_fn, row_ins, vec_ins, out_dtypes, n_vec = epilogue
    n_row_out = len(out_dtypes)

    def kern(*refs):
        refs = list(refs)
        acc_ref = refs.pop() if nk > 1 else None
        mm = refs[:n_mm]
        extra = refs[n_mm:n_mm + len(row_ins) + len(vec_ins)]
        outs = refs[n_mm + len(extra):]
        i = pl.program_id(0)
        k = pl.program_id(2)

        def product():
            part = _dot(mm[0][...], mm[1][...], dims)
            if pair2 is not None:
                part = part + _dot(mm[2][...], mm[3][...], dims)
            return part

        def finish(r):
            if alpha != 1.0:
                r = r * alpha
            if epilogue is None:
                if extra:
                    r = extra[0][...] + r
                outs[0][...] = r.astype(out_dtype)
                return
            vals = epi_fn(r, *[e[...] for e in extra])
            for o_ref, v in zip(outs[:n_row_out], vals[:n_row_out]):
                o_ref[...] = v.astype(o_ref.dtype)
            for o_ref, v in zip(outs[n_row_out:], vals[n_row_out:]):
                @pl.when(i == 0)
                def _():
                    o_ref[...] = jnp.zeros_like(o_ref)

                o_ref[...] += v

        if nk == 1:
            finish(product())
            return

        @pl.when(k == 0)
        def _():
            acc_ref[...] = jnp.zeros_like(acc_ref)

        acc_ref[...] += product()

        @pl.when(k == nk - 1)
        def _():
            finish(acc_ref[...])

    ins = [a, b] + ([] if pair2 is None else list(pair2)) + list(row_ins) + list(vec_ins)
    specs = [a_spec, b_spec] * (n_mm // 2) + [o_spec] * len(row_ins) + [v_spec] * len(vec_ins)
    out_specs = [o_spec] * n_row_out + [v_spec] * n_vec
    out_shape = [jax.ShapeDtypeStruct((m, n), dt) for dt in out_dtypes] + [jax.ShapeDtypeStruct((1, n), F32)] * n_vec
    outs = _call_2d(kern, name=name, grid=(m // tm, n // tn, nk), in_specs=specs, out_specs=out_specs, out_shape=out_shape,
                    ins=ins, scratch_shapes=[pltpu.VMEM((tm, tn), F32)] if nk > 1 else [],
                    semantics=("arbitrary" if n_vec else "parallel", "parallel", "arbitrary"), side=side)
    carried = None
    if side is not None:
        outs, carried = outs
    outs = outs[0] if epilogue is None else outs
    return outs if side is None else (outs, carried)


def _epi_residual_rms(r, res, gain):
    xn = res + r
    return xn, xn * lax.rsqrt(jnp.mean(xn * xn, axis=-1, keepdims=True) + NORM_EPS) * gain


def _epi_rms_bwd(r, x, dres, gain):
    rs = lax.rsqrt(jnp.mean(x * x, axis=-1, keepdims=True) + NORM_EPS)
    xh = x * rs
    dy = r * gain
    dx = dres + rs * (dy - xh * jnp.mean(dy * xh, axis=-1, keepdims=True))
    return dx, dx, jnp.sum(r * xh, axis=0, keepdims=True)


def _epi_rms_bwd_sum(r, r0, x, dres, gain):
    return _epi_rms_bwd(r + r0, x, dres, gain)


def _epi_loss(r, res, target):
    e = (res + r) - target
    dy = e / e.shape[-1]
    return dy, dy, jnp.sum(e * e, axis=0, keepdims=True)
def _rms_fwd(x, g, *, name, side=None):
    s, d = x.shape
    ts = _pick(s, (512, 256))

    def kern(x_ref, g_ref, h_ref):
        xf = x_ref[...]
        r = lax.rsqrt(jnp.mean(xf * xf, axis=-1, keepdims=True) + NORM_EPS)
        h_ref[...] = (xf * r * g_ref[...]).astype(h_ref.dtype)

    outs = _call_2d(kern, name=name, grid=(s // ts,),
                    in_specs=[pl.BlockSpec((ts, d), lambda i: (i, 0)), pl.BlockSpec((1, d), lambda i: (0, 0))],
                    out_specs=[pl.BlockSpec((ts, d), lambda i: (i, 0))], out_shape=[jax.ShapeDtypeStruct((s, d), BF16)],
                    ins=[x, g], semantics=("parallel",), side=side)
    return outs[0] if side is None else (outs[0][0], outs[1])


def _rms_bwd(x, g, dh, res, *, name):
    s, d = x.shape
    ts = _pick(s, (512, 256))

    def kern(*refs):
        if res is None:
            x_ref, g_ref, dh_ref, dx_ref, dxb_ref, dg_ref = refs
            r_ref = None
        else:
            x_ref, g_ref, dh_ref, r_ref, dx_ref, dxb_ref, dg_ref = refs
        xf = x_ref[...]
        r = lax.rsqrt(jnp.mean(xf * xf, axis=-1, keepdims=True) + NORM_EPS)
        xh = xf * r
        dhf = dh_ref[...].astype(F32)
        dy = dhf * g_ref[...]
        dx = r * (dy - xh * jnp.mean(dy * xh, axis=-1, keepdims=True))
        if r_ref is not None:
            dx = r_ref[...] + dx
        dx_ref[...] = dx
        dxb_ref[...] = dx.astype(dxb_ref.dtype)

        @pl.when(pl.program_id(0) == 0)
        def _():
            dg_ref[...] = jnp.zeros_like(dg_ref)

        dg_ref[...] += jnp.sum(dhf * xh, axis=0, keepdims=True)

    row = pl.BlockSpec((ts, d), lambda i: (i, 0))
    vec = pl.BlockSpec((1, d), lambda i: (0, 0))
    ins = [x, g, dh] + ([] if res is None else [res])
    return _pcall(kern, name=name, grid=(s // ts,), in_specs=[row, vec, row] + ([] if res is None else [row]),
                  out_specs=[row, row, vec],
                  out_shape=[jax.ShapeDtypeStruct((s, d), F32), jax.ShapeDtypeStruct((s, d), BF16), jax.ShapeDtypeStruct((1, d), F32)],
                  compiler_params=_params("arbitrary"))(*ins)


def _sigmoid(x):
    return 1.0 / (1.0 + jnp.exp(-x))


FFN_TM, FFN_TF = 512, 1408


def _ffn_up(h, w1, w3, *, name, side=None):
    s, d = h.shape
    fdim = w1.shape[1]
    tm, tf = _pick(s, (FFN_TM, 256)), _pick(fdim, (FFN_TF, 512, 256, 128))

    def kern(h_ref, w1_ref, w3_ref, a_ref, b_ref, f_ref):
        hb = h_ref[...]
        a = _nn(hb, w1_ref[...])
        b = _nn(hb, w3_ref[...])
        a_ref[...] = a.astype(a_ref.dtype)
        b_ref[...] = b.astype(b_ref.dtype)
        f_ref[...] = (a * _sigmoid(a) * b).astype(f_ref.dtype)

    wspec = pl.BlockSpec((d, tf), lambda i, j: (0, j))
    ospec = pl.BlockSpec((tm, tf), lambda i, j: (i, j))
    shp = jax.ShapeDtypeStruct((s, fdim), BF16)
    return _call_2d(kern, name=name, grid=(s // tm, fdim // tf), in_specs=[pl.BlockSpec((tm, d), lambda i, j: (i, 0)), wspec, wspec],
                    out_specs=[ospec, ospec, ospec], out_shape=[shp, shp, shp], ins=[h, w1, w3],
                    semantics=("parallel", "parallel"), side=side)


def _ffn_dact(dy, w2, a, b, *, name, side=None):
    s, d = dy.shape
    fdim = w2.shape[0]
    tm, tf = _pick(s, (FFN_TM, 256)), _pick(fdim, (FFN_TF, 512, 256, 128))

    half = (tf // LANES + 1) // 2 * LANES

    def kern(dy_ref, w2_ref, a_ref, b_ref, da_ref, db_ref):
        dyb = dy_ref[...]
        pieces = ((0, half), (half, tf))
        dfs = [_nt(dyb, w2_ref[lo:hi, :]) * 0.5 for lo, hi in pieces]
        for (lo, hi), df in zip(pieces, dfs):
            av = a_ref[:, lo:hi].astype(F32)
            sg = _sigmoid(av)
            da_ref[:, lo:hi] = (df * b_ref[:, lo:hi].astype(F32) * (sg + av * sg * (1.0 - sg))).astype(da_ref.dtype)
            db_ref[:, lo:hi] = (df * (av * sg)).astype(db_ref.dtype)

    ospec = pl.BlockSpec((tm, tf), lambda i, j: (i, j))
    shp = jax.ShapeDtypeStruct((s, fdim), BF16)
    return _call_2d(kern, name=name, grid=(s // tm, fdim // tf),
                    in_specs=[pl.BlockSpec((tm, d), lambda i, j: (i, 0)), pl.BlockSpec((tf, d), lambda i, j: (j, 0)), ospec, ospec],
                    out_specs=[ospec, ospec], out_shape=[shp, shp], ins=[dy, w2, a, b], semantics=("parallel", "parallel"), side=side)


def _head_mean(v, bd):
    outs = []
    for c in range(v.shape[1] // LANES):
        x = v[:, c * LANES:(c + 1) * LANES]
        hi = x.astype(BF16)
        lo = (x - hi.astype(F32)).astype(BF16)
        outs.append(lax.dot_general(jnp.concatenate([hi, lo], axis=1), bd, (((1,), (0,)), ((), ())), preferred_element_type=F32))
    return outs[0] if len(outs) == 1 else jnp.concatenate(outs, axis=1)


def _partner(v):
    w = v.shape[1]
    lane = lax.broadcasted_iota(jnp.int32, v.shape, 1)
    return jnp.where(lane % HEAD_DIM < HEAD_DIM // 2, pltpu.roll(v, w - HEAD_DIM // 2, 1), pltpu.roll(v, HEAD_DIM // 2, 1))


def _block_diag(w=None):
    r = (lax.broadcasted_iota(jnp.int32, (2 * LANES, LANES), 0) % LANES) // HEAD_DIM
    c = lax.broadcasted_iota(jnp.int32, (2 * LANES, LANES), 1) // HEAD_DIM
    return jnp.where(r == c, 1.0 / HEAD_DIM, 0.0).astype(BF16)


def _rope_tables(s):
    half = HEAD_DIM // 2
    inv_freq = jnp.power(ROPE_THETA, -jnp.arange(half, dtype=F32) / half)
    ang = jnp.arange(s).astype(F32)[:, None] * inv_freq[None, :]
    cos, sin = lax.optimization_barrier((jnp.cos(ang), jnp.sin(ang)))
    cos2 = jnp.concatenate([cos, cos, cos, cos], axis=1)
    sin2 = jnp.concatenate([-sin, sin, -sin, sin], axis=1)
    return cos2, sin2


def _qknorm_fwd(src, col0, width, gain, rope, *, name, out_dtype=BF16):
    s = src.shape[0]
    ts = _pick(s, (512, 256))
    cb = col0 // width
    assert col0 % width == 0
    reps = width // LANES
    g = jnp.tile(gain, (1, width // HEAD_DIM))

    def kern(*refs):
        if rope is None:
            x_ref, g_ref, o_ref = refs
        else:
            x_ref, g_ref, c_ref, s_ref, o_ref = refs
        x = x_ref[...].astype(F32)
        bd = _block_diag(width)
        r = lax.rsqrt(_head_mean(x * x, bd) + NORM_EPS)
        y = x * r * g_ref[...]
        if rope is not None:
            y = y * jnp.tile(c_ref[...], (1, reps)) + _partner(y) * jnp.tile(s_ref[...], (1, reps))
        o_ref[...] = y.astype(o_ref.dtype)

    xs = pl.BlockSpec((ts, width), lambda i: (i, cb))
    tab = pl.BlockSpec((ts, LANES), lambda i: (i, 0))
    ins = [src, g] + ([] if rope is None else list(rope))
    specs = [xs, pl.BlockSpec((1, width), lambda i: (0, 0))] + ([] if rope is None else [tab, tab])
    return _pcall(kern, name=name, grid=(s // ts,), in_specs=specs, out_specs=pl.BlockSpec((ts, width), lambda i: (i, 0)),
                  out_shape=jax.ShapeDtypeStruct((s, width), out_dtype), compiler_params=_params("parallel"))(*ins)


def _qknorm_bwd(src, col0, width, gain, rope, dout, *, name):
    s = src.shape[0]
    ts = _pick(s, (512, 256))
    cb = col0 // width
    reps = width // LANES
    g = jnp.tile(gain, (1, width // HEAD_DIM))

    douts = list(dout) if isinstance(dout, (list, tuple)) else [dout]
    piece = width // len(douts)

    def kern(*refs):
        refs = list(refs)
        dg_ref = refs.pop()
        dx_ref = refs.pop()
        do_refs = [refs.pop() for _ in douts][::-1]
        if rope is None:
            x_ref, g_ref = refs
        else:
            x_ref, g_ref, c_ref, s_ref = refs
        x = x_ref[...].astype(F32)
        bd = _block_diag(width)
        r = lax.rsqrt(_head_mean(x * x, bd) + NORM_EPS)
        xh = x * r
        dy = jnp.concatenate([d[...].astype(F32) for d in do_refs], axis=1) if len(do_refs) > 1 else do_refs[0][...].astype(F32)
        if rope is not None:
            dy = dy * jnp.tile(c_ref[...], (1, reps)) + _partner(dy * jnp.tile(s_ref[...], (1, reps)))
        dxh = dy * g_ref[...]
        dx_ref[...] = (r * (dxh - xh * _head_mean(dxh * xh, bd))).astype(dx_ref.dtype)

        @pl.when(pl.program_id(0) == 0)
        def _():
            dg_ref[...] = jnp.zeros_like(dg_ref)

        dg_ref[...] += jnp.sum(dy * xh, axis=0, keepdims=True)

    xs = pl.BlockSpec((ts, width), lambda i: (i, cb))
    row = pl.BlockSpec((ts, width), lambda i: (i, 0))
    vec = pl.BlockSpec((1, width), lambda i: (0, 0))
    tab = pl.BlockSpec((ts, LANES), lambda i: (i, 0))
    ins = [src, g] + ([] if rope is None else list(rope)) + douts
    specs = [xs, vec] + ([] if rope is None else [tab, tab]) + [pl.BlockSpec((ts, piece), lambda i: (i, 0))] * len(douts)
    dx, dg = _pcall(kern, name=name, grid=(s // ts,), in_specs=specs, out_specs=[row, vec],
                    out_shape=[jax.ShapeDtypeStruct((s, width), BF16), jax.ShapeDtypeStruct((1, width), F32)],
                    compiler_params=_params("arbitrary"))(*ins)
    return dx, jnp.sum(dg.reshape(width // HEAD_DIM, HEAD_DIM), axis=0, keepdims=True)


def _tri(strict, n):
    r = lax.broadcasted_iota(jnp.int32, (2 * n, n), 0) % n
    c = lax.broadcasted_iota(jnp.int32, (2 * n, n), 1)
    return jnp.where((r > c) if strict else (r >= c), 1.0, 0.0).astype(BF16)


def _split_dot(v, t2):
    hi = v.astype(BF16)
    lo = (v - hi.astype(F32)).astype(BF16)
    return lax.dot_general(jnp.concatenate([hi, lo], axis=1), t2, (((1,), (0,)), ((), ())), preferred_element_type=F32)


LOG2E = 1.4426950408889634


def _log2_sigmoids(z2):
    lf = -(jnp.maximum(z2, 0.0) + jnp.log2(1.0 + jnp.exp2(-jnp.abs(z2))))
    return z2 + lf, lf


SB2_SUB = 2
SB_KT = 128


def _first_half(shape):
    return lax.broadcasted_iota(jnp.int32, shape, 1) < HEAD_DIM


def _split_pair(t, first):
    zero = jnp.zeros_like(t)
    return [jnp.where(first, t, zero), jnp.where(first, zero, t)]


def _sb2_fwd(p, *, name, side=None):
    s = p.shape[0]
    rq = SB2_SUB * QB
    nq = s // rq
    npair = SB_W // LANES

    def kern(q_ref, k_ref, v_ref, o_ref):
        i = pl.program_id(1)
        first = _first_half((rq, LANES))
        q2 = jnp.concatenate(_split_pair(q_ref[...], first), axis=0)
        t2 = _tri(True, SB_KT)
        rel = lax.broadcasted_iota(jnp.int32, (2 * rq, SB_KT), 1) - lax.broadcasted_iota(jnp.int32, (2 * rq, SB_KT), 0) % rq

        def tile(j, q, rel, carry, acc, masked):
            off = pl.multiple_of(j * SB_KT, SB_KT)
            ls, lf = _log2_sigmoids(_nt(q, k_ref[pl.ds(off, SB_KT), :]) * (SCALE * LOG2E))
            if masked:
                before = rel < i * rq - j * SB_KT
                lf = jnp.where(before, lf, 0.0)
            w = jnp.exp2(ls + _split_dot(lf, t2) + carry)
            if masked:
                w = jnp.where(before, w, 0.0)
            return carry + jnp.sum(lf, axis=1, keepdims=True), acc + _nn(w, v_ref[pl.ds(off, SB_KT), :])

        carry, acc = jnp.zeros((2 * rq, 1), F32), jnp.zeros((2 * rq, LANES), F32)
        for a in range(rq // SB_KT):
            carry, acc = tile(i * (rq // SB_KT) + (rq // SB_KT - 1 - a), q2, rel, carry, acc, True)

        def cond(st):
            return jnp.logical_and(st[0] >= 0, st[1] > 0)

        def body(st):
            carry, acc = tile(st[0], q2, rel, st[2], st[3], False)
            return st[0] - 1, (jnp.max(carry) > SB_DEAD).astype(jnp.int32), carry, acc

        st = lax.while_loop(cond, body, (i * (rq // SB_KT) - 1, jnp.int32(1), carry, acc))
        o_ref[...] = jnp.where(first, st[3][:rq], st[3][rq:])

    outs = _call_2d(kern, name=name, grid=(npair, nq),
                    in_specs=[pl.BlockSpec((rq, LANES), lambda a, i: (i, a)), pl.BlockSpec((s, LANES), lambda a, i: (0, npair + a)),
                              pl.BlockSpec((s, LANES), lambda a, i: (0, 2 * npair + a))],
                    out_specs=[pl.BlockSpec((rq, LANES), lambda a, i: (i, a))], out_shape=[jax.ShapeDtypeStruct((s, SB_W), F32)],
                    ins=[p, p, p], semantics=("parallel", "arbitrary"), side=side)
    return outs[0] if side is None else (outs[0][0], outs[1])


def _sb2_bwd(p, o, do, *, name, side=None):
    s = p.shape[0]
    rq = SB2_SUB * QB
    nq = s // rq
    npair = SB_W // LANES

    def kern(q_ref, k_ref, v_ref, o_ref, do_ref, dq_ref, dk_hbm, dv_hbm, dk_acc, dv_acc, sem):
        pr = pl.program_id(0)
        i = pl.program_id(1)

        @pl.when(i == 0)
        def _():
            dk_acc[...] = jnp.zeros_like(dk_acc)
            dv_acc[...] = jnp.zeros_like(dv_acc)

        first = _first_half((rq, LANES))
        q2 = jnp.concatenate(_split_pair(q_ref[...], first), axis=0)
        do2 = jnp.concatenate(_split_pair(do_ref[...], first), axis=0)
        o2 = o_ref[...]
        dsum = jnp.sum(do2.astype(F32) * jnp.concatenate([o2, o2], axis=0), axis=1, keepdims=True)
        t_strict = _tri(True, SB_KT)
        t_incl = _tri(False, SB_KT)
        rel = lax.broadcasted_iota(jnp.int32, (2 * rq, SB_KT), 1) - lax.broadcasted_iota(jnp.int32, (2 * rq, SB_KT), 0) % rq

        def tile(j, rows, carry, gcarry, dq, masked):
            q, dob, dsm, rel = rows
            off = pl.multiple_of(j * SB_KT, SB_KT)
            kt = k_ref[pl.ds(off, SB_KT), :]
            ls, lf = _log2_sigmoids(_nt(q, kt) * (SCALE * LOG2E))
            if masked:
                before = rel < i * rq - j * SB_KT
                lf = jnp.where(before, lf, 0.0)
            w = jnp.exp2(ls + _split_dot(lf, t_strict) + carry)
            if masked:
                w = jnp.where(before, w, 0.0)
            wr = w.astype(MXU_DT)
            g = _nt(dob, v_ref[pl.ds(off, SB_KT), :]) * wr.astype(F32)
            big_g = dsm - (_split_dot(g, t_incl) + gcarry)
            sig = jnp.exp2(ls)
            dz = g * (1.0 - sig) - sig * big_g
            if masked:
                dz = jnp.where(before, dz, 0.0)
            dz = dz * SCALE
            dk_acc[pl.ds(off, SB_KT), :] += _tn(dz, q)
            dv_acc[pl.ds(off, SB_KT), :] += _tn(wr, dob)
            return (carry + jnp.sum(lf, axis=1, keepdims=True), gcarry + jnp.sum(g, axis=1, keepdims=True),
                    dq + _nn(dz, kt))

        zc = jnp.zeros((2 * rq, 1), F32)
        carry, gcarry, dq = zc, zc, jnp.zeros((2 * rq, LANES), F32)
        whole = (q2, do2, dsum, rel)
        for a in range(rq // SB_KT):
            carry, gcarry, dq = tile(i * (rq // SB_KT) + (rq // SB_KT - 1 - a), whole, carry, gcarry, dq, True)

        def cond(st):
            return jnp.logical_and(st[0] >= 0, st[1] > 0)

        def body(st):
            carry, gcarry, dq = tile(st[0], whole, st[2], st[3], st[4], False)
            return st[0] - 1, (jnp.max(carry) > SB_DEAD).astype(jnp.int32), carry, gcarry, dq

        st = lax.while_loop(cond, body, (i * (rq // SB_KT) - 1, jnp.int32(1), carry, gcarry, dq))
        dq_ref[...] = jnp.where(first, st[4][:rq], st[4][rq:]).astype(dq_ref.dtype)

        @pl.when(i == nq - 1)
        def _():
            cols = pl.ds(pl.multiple_of(pr * LANES, LANES), LANES)
            ck = pltpu.make_async_copy(dk_acc, dk_hbm.at[:, cols], sem.at[0])
            cv = pltpu.make_async_copy(dv_acc, dv_hbm.at[:, cols], sem.at[1])
            ck.start()
            cv.start()
            ck.wait()
            cv.wait()

    blk = pl.BlockSpec((rq, LANES), lambda a, i: (i, a))
    anyspace = pl.BlockSpec(memory_space=pl.ANY)
    shp = jax.ShapeDtypeStruct((s, SB_W), F32)
    return _call_2d(kern, name=name, grid=(npair, nq),
                    in_specs=[blk, pl.BlockSpec((s, LANES), lambda a, i: (0, npair + a)),
                              pl.BlockSpec((s, LANES), lambda a, i: (0, 2 * npair + a)), blk, blk],
                    out_specs=[blk, anyspace, anyspace], out_shape=[jax.ShapeDtypeStruct((s, SB_W), BF16), shp, shp], ins=[p, p, p, o, do],
                    scratch_shapes=[pltpu.VMEM((s, LANES), F32), pltpu.VMEM((s, LANES), F32), pltpu.SemaphoreType.DMA((2,))],
                    semantics=("arbitrary", "arbitrary"), side=side)


def _dsa_rel():
    qi = lax.broadcasted_iota(jnp.int32, (QB, QB), 0)
    kj = lax.broadcasted_iota(jnp.int32, (QB, QB), 1)
    return kj - qi


def _prev_mask(rel, has_prev):
    return rel >= jnp.where(has_prev, 0, QB)


DSA_BT = QB * max(r for _, r in DSA_GROUPS)
DSA_UB = 4


def _bdot(a, b, ca, cb):
    return lax.dot_general(a.astype(MXU_DT), b.astype(MXU_DT), (((ca,), (cb,)), ((0,), (0,))), preferred_element_type=F32)


def _bnt(a, b):
    return _bdot(a, b, 2, 2)


def _bnn(a, b):
    return _bdot(a, b, 2, 1)


def _btn(a, b):
    return _bdot(a, b, 1, 1)


def _unit_rows(r, c, b):
    return pl.ds(c + QB * r * b, QB, stride=r)


def _pair_cols(t, first):
    return [jnp.max(jnp.where(first, t, -jnp.inf), axis=1, keepdims=True),
            jnp.max(jnp.where(first, -jnp.inf, t), axis=1, keepdims=True)]


def _dsa2_fwd(qn, kn, v32, g, *, name):
    s = qn.shape[0]
    r = DSA_GROUPS[g][1]
    nbk = DSA_BT // (QB * r)
    npair = DSA_OUT_W // LANES

    def kern(q_ref, k_ref, kp_ref, v_ref, vp_ref, o_ref, l_ref):
        t = pl.program_id(1)
        first = _first_half((QB, LANES))
        rel = _dsa_rel()
        units = [(c, b) for c in range(r) for b in range(nbk)]
        for u0 in range(0, len(units), DSA_UB):
            batch = units[u0:u0 + DSA_UB]
            qs, kcs, vcs, kps, vps, masks = [], [], [], [], [], []
            for c, b in batch:
                rows = _unit_rows(r, c, b)
                kc, vc = k_ref[rows, :].astype(MXU_DT), v_ref[rows, :].astype(MXU_DT)
                if b > 0:
                    prow = _unit_rows(r, c, b - 1)
                    kpv, vpv, has_prev = k_ref[prow, :], v_ref[prow, :], True
                else:
                    prow = _unit_rows(r, c, nbk - 1)
                    kpv, vpv, has_prev = kp_ref[prow, :], vp_ref[prow, :], t > 0
                for qe in _split_pair(q_ref[rows, :], first):
                    qs.append(qe.astype(MXU_DT))
                    kcs.append(kc)
                    vcs.append(vc)
                    kps.append(kpv.astype(MXU_DT))
                    vps.append(vpv.astype(MXU_DT))
                    masks.append(_prev_mask(rel, has_prev))
            qq = jnp.stack(qs)
            sc = jnp.where(rel <= 0, _bnt(qq, jnp.stack(kcs)) * SCALE, -jnp.inf)
            sp = _bnt(qq, jnp.stack(kps)) * SCALE
            sp = jnp.stack([jnp.where(mk, sp[n], -jnp.inf) for n, mk in enumerate(masks)])
            m = jnp.maximum(jnp.max(sc, axis=2, keepdims=True), jnp.max(sp, axis=2, keepdims=True))
            pc = jnp.exp(sc - m)
            pp = jnp.exp(sp - m)
            den = jnp.sum(pc, axis=2, keepdims=True) + jnp.sum(pp, axis=2, keepdims=True)
            out = (_bnn(pc, jnp.stack(vcs)) + _bnn(pp, jnp.stack(vps))) / den
            lse = m + jnp.log(den)
            for idx, (c, b) in enumerate(batch):
                rows = _unit_rows(r, c, b)
                o_ref[rows, :] = jnp.where(first, out[2 * idx], out[2 * idx + 1])
                l_ref[rows, :] = jnp.where(first, lse[2 * idx], lse[2 * idx + 1])

    npg = DSA_HPG * HEAD_DIM // LANES
    cur = pl.BlockSpec((DSA_BT, LANES), lambda a, t: (t, npg * g + a))
    prev = pl.BlockSpec((DSA_BT, LANES), lambda a, t: (jnp.maximum(t - 1, 0), npg * g + a))
    out = pl.BlockSpec((DSA_BT, LANES), lambda a, t: (t, a))
    shp = jax.ShapeDtypeStruct((s, DSA_OUT_W), F32)
    return _pcall(kern, name=name, grid=(npair, s // DSA_BT), in_specs=[cur, cur, prev, cur, prev], out_specs=[out, out],
                  out_shape=[shp, shp], compiler_params=_params("parallel", "parallel"))(qn, kn, kn, v32, v32)


def _dsa2_combine(parts, *, name):
    s, wd = parts[0][0].shape
    ts = _pick(s, (512, 256))

    def kern(o0, l0, o1, l1, o2, l2, o_ref, l_ref):
        ls = [l0[...], l1[...], l2[...]]
        m = jnp.maximum(jnp.maximum(ls[0], ls[1]), ls[2])
        es = [jnp.exp(l - m) for l in ls]
        den = es[0] + es[1] + es[2]
        o_ref[...] = (es[0] * o0[...] + es[1] * o1[...] + es[2] * o2[...]) / den
        l_ref[...] = m + jnp.log(den)

    blk = pl.BlockSpec((ts, wd), lambda i: (i, 0))
    shp = jax.ShapeDtypeStruct((s, wd), F32)
    flat = [t for pair in parts for t in pair]
    return _pcall(kern, name=name, grid=(s // ts,), in_specs=[blk] * 6, out_specs=[blk, blk], out_shape=[shp, shp],
                  compiler_params=_params("parallel"))(*flat)


def _dsa2_prep(o, do, *, name):
    s, wd = o.shape
    ts = _pick(s, (512, 256))

    def kern(o_ref, do_ref, d_ref):
        d_ref[...] = _head_mean(do_ref[...] * o_ref[...], _block_diag(wd)) * HEAD_DIM

    blk = pl.BlockSpec((ts, wd), lambda i: (i, 0))
    return _pcall(kern, name=name, grid=(s // ts,), in_specs=[blk, blk], out_specs=blk,
                  out_shape=jax.ShapeDtypeStruct((s, wd), F32), compiler_params=_params("parallel"))(o, do)


def _dsa2_bwd(qn, kn, v32, do, lse, dd, g, *, name):
    s = qn.shape[0]
    r = DSA_GROUPS[g][1]
    nbk = DSA_BT // (QB * r)
    npair = DSA_OUT_W // LANES
    nsteps = s // DSA_BT

    def kern(q_ref, qn_ref, k_ref, kp_ref, v_ref, vp_ref, do_ref, don_ref, l_ref, ln_ref, d_ref, dn_ref,
             dq_ref, dk_ref, dv_ref):
        t = pl.program_id(1)
        first = _first_half((QB, LANES))
        rel = _dsa_rel()

        def pairs(items):
            qq = jnp.stack([it[0].astype(MXU_DT) for it in items])
            dd = jnp.stack([it[1].astype(MXU_DT) for it in items])
            kk = jnp.stack([it[4].astype(MXU_DT) for it in items])
            vv = jnp.stack([it[5].astype(MXU_DT) for it in items])
            p = jnp.exp(_bnt(qq, kk) * SCALE - jnp.stack([it[2] for it in items]))
            p = jnp.stack([jnp.where(it[6], p[n], 0.0) for n, it in enumerate(items)])
            ds = p * (_bnt(dd, vv) - jnp.stack([it[3] for it in items])) * SCALE
            return _bnn(ds, kk), _btn(ds, qq), _btn(p, dd)

        def heads(rows, qr, dor, lr, dr):
            return list(zip(_split_pair(qr[rows, :], first), _split_pair(dor[rows, :], first),
                            _pair_cols(lr[rows, :], first), _pair_cols(dr[rows, :], first)))

        units = [(c, b) for c in range(r) for b in range(nbk)]
        dk_of, dv_of = [None] * len(units), [None] * len(units)
        for u0 in range(0, len(units), DSA_UB // 2):
            batch = list(enumerate(units))[u0:u0 + DSA_UB // 2]
            items = []
            for u, (c, b) in batch:
                rows = _unit_rows(r, c, b)
                kc, vc = k_ref[rows, :], v_ref[rows, :]
                if b > 0:
                    prow = _unit_rows(r, c, b - 1)
                    kpv, vpv, pmask = k_ref[prow, :], v_ref[prow, :], _prev_mask(rel, True)
                else:
                    prow = _unit_rows(r, c, nbk - 1)
                    kpv, vpv, pmask = kp_ref[prow, :], vp_ref[prow, :], _prev_mask(rel, t > 0)
                for hd in heads(rows, q_ref, do_ref, l_ref, d_ref):
                    items.append(hd + (kc, vc, rel <= 0))
                    items.append(hd + (kpv, vpv, pmask))
            dq, dk, dv = pairs(items)
            for n, (u, (c, b)) in enumerate(batch):
                dq_ref[_unit_rows(r, c, b), :] = jnp.where(first, dq[4 * n] + dq[4 * n + 1], dq[4 * n + 2] + dq[4 * n + 3])
                dk_of[u] = dk[4 * n] + dk[4 * n + 2]
                dv_of[u] = dv[4 * n] + dv[4 * n + 2]
                if b > 0:
                    dk_of[u - 1] = dk_of[u - 1] + (dk[4 * n + 1] + dk[4 * n + 3])
                    dv_of[u - 1] = dv_of[u - 1] + (dv[4 * n + 1] + dv[4 * n + 3])
        lasts = [c * nbk + nbk - 1 for c in range(r)]
        for c0 in range(0, r, DSA_UB):
            chunk = list(range(c0, min(c0 + DSA_UB, r)))
            items = []
            for c in chunk:
                last = _unit_rows(r, c, nbk - 1)
                for hd in heads(_unit_rows(r, c, 0), qn_ref, don_ref, ln_ref, dn_ref):
                    items.append(hd + (k_ref[last, :], v_ref[last, :], _prev_mask(rel, t < nsteps - 1)))
            _, dk, dv = pairs(items)
            for n, c in enumerate(chunk):
                dk_of[lasts[c]] = dk_of[lasts[c]] + (dk[2 * n] + dk[2 * n + 1])
                dv_of[lasts[c]] = dv_of[lasts[c]] + (dv[2 * n] + dv[2 * n + 1])
        for u, (c, b) in enumerate(units):
            dk_ref[_unit_rows(r, c, b), :] = dk_of[u]
            dv_ref[_unit_rows(r, c, b), :] = dv_of[u]

    npg = DSA_HPG * HEAD_DIM // LANES

    def at(shift, col):
        return pl.BlockSpec((DSA_BT, LANES), lambda a, t: (jnp.clip(t + shift, 0, nsteps - 1), col(a)))

    gcol = lambda a: npg * g + a
    ocol = lambda a: a
    specs = [at(0, gcol), at(1, gcol), at(0, gcol), at(-1, gcol), at(0, gcol), at(-1, gcol),
             at(0, ocol), at(1, ocol), at(0, ocol), at(1, ocol), at(0, ocol), at(1, ocol)]
    shp = jax.ShapeDtypeStruct((s, DSA_OUT_W), F32)
    return _pcall(kern, name=name, grid=(npair, nsteps), in_specs=specs, out_specs=[at(0, ocol)] * 3, out_shape=[shp, shp, shp],
                  compiler_params=_params("parallel", "parallel"))(qn, qn, kn, kn, v32, v32, do, do, lse, lse, dd, dd)


def _mem2_fwd(qn, km, kv, *, name):
    s = qn.shape[0]
    ml = km.shape[0]
    tq = _pick(s, (512, 256))
    npair = MEM_W // LANES

    def kern(q_ref, k_ref, v_ref, o_ref):
        first = _first_half((tq, LANES))
        q2 = jnp.concatenate(_split_pair(q_ref[...], first), axis=0)
        sc = _nt(q2, k_ref[...]) * SCALE
        e = jnp.exp(sc - jnp.max(sc, axis=1, keepdims=True))
        o2 = _nn(e / jnp.sum(e, axis=1, keepdims=True), v_ref[...])
        o_ref[...] = jnp.where(first, o2[:tq], o2[tq:])

    blk = pl.BlockSpec((tq, LANES), lambda a, i: (i, a))
    return _pcall(kern, name=name, grid=(npair, s // tq),
                  in_specs=[blk, pl.BlockSpec((ml, LANES), lambda a, i: (0, a)), pl.BlockSpec((ml, LANES), lambda a, i: (0, npair + a))],
                  out_specs=blk, out_shape=jax.ShapeDtypeStruct((s, MEM_W), F32),
                  compiler_params=_params("parallel", "parallel"))(qn, km, kv)


def _mem2_bwd(qn, km, kv, do, *, name):
    s = qn.shape[0]
    ml = km.shape[0]
    tq = _pick(s, (512, 256))
    npair = MEM_W // LANES

    def kern(q_ref, k_ref, v_ref, do_ref, dq_ref, dk_ref, dv_ref):
        @pl.when(pl.program_id(1) == 0)
        def _():
            dk_ref[...] = jnp.zeros_like(dk_ref)
            dv_ref[...] = jnp.zeros_like(dv_ref)

        first = _first_half((tq, LANES))
        q2 = jnp.concatenate(_split_pair(q_ref[...], first), axis=0)
        do2 = jnp.concatenate(_split_pair(do_ref[...], first), axis=0)
        sc = _nt(q2, k_ref[...]) * SCALE
        e = jnp.exp(sc - jnp.max(sc, axis=1, keepdims=True))
        p = e / jnp.sum(e, axis=1, keepdims=True)
        dp = _nt(do2, v_ref[...])
        ds = p * (dp - jnp.sum(p * dp, axis=1, keepdims=True)) * SCALE
        dq2 = _nn(ds, k_ref[...])
        dk_ref[...] += _tn(ds, q2)
        dv_ref[...] += _tn(p, do2)
        dq_ref[...] = jnp.where(first, dq2[:tq], dq2[tq:])

    blk = pl.BlockSpec((tq, LANES), lambda a, i: (i, a))
    kblk = pl.BlockSpec((ml, LANES), lambda a, i: (0, a))
    kshape = jax.ShapeDtypeStruct((ml, MEM_W), F32)
    return _pcall(kern, name=name, grid=(npair, s // tq),
                  in_specs=[blk, kblk, pl.BlockSpec((ml, LANES), lambda a, i: (0, npair + a)), blk],
                  out_specs=[blk, kblk, kblk], out_shape=[jax.ShapeDtypeStruct((s, MEM_W), F32), kshape, kshape],
                  compiler_params=_params("parallel", "arbitrary"))(qn, km, kv, do)


def _merge_fwd(logits, bias, ya, yb, yc, *, name):
    s, d = ya.shape
    ts = _pick(s, (512, 256))

    def kern(l0, l1, l2, b0, b1, b2, a_ref, b_ref, c_ref, o_ref):
        m = 0.0
        for l_ref, bb_ref, y_ref in ((l0, b0, a_ref), (l1, b1, b_ref), (l2, b2, c_ref)):
            m = m + _sigmoid(l_ref[...].astype(F32) + bb_ref[...]) * y_ref[...].astype(F32)
        o_ref[...] = m.astype(o_ref.dtype)

    row = pl.BlockSpec((ts, d), lambda i: (i, 0))
    lg = [pl.BlockSpec((ts, d), functools.partial(lambda i, c: (i, c), c=c)) for c in range(3)]
    bs = [pl.BlockSpec((1, d), functools.partial(lambda i, c: (0, c), c=c)) for c in range(3)]
    return _pcall(kern, name=name, grid=(s // ts,), in_specs=lg + bs + [row, row, row], out_specs=row,
                  out_shape=jax.ShapeDtypeStruct((s, d), BF16),
                  compiler_params=_params("parallel"))(logits, logits, logits, bias, bias, bias, ya, yb, yc)


def _merge_bwd(logits, bias, ya, yb, yc, dm, *, name):
    s, d = ya.shape
    ts = _pick(s, (256,))

    def kern(l0, l1, l2, b0, b1, b2, a_ref, b_ref, c_ref, dm_ref, da_ref, db_ref, dc_ref, dl_ref, dbias_ref):
        dmv = dm_ref[...].astype(F32)

        @pl.when(pl.program_id(0) == 0)
        def _():
            dbias_ref[...] = jnp.zeros_like(dbias_ref)

        for c, (l_ref, bb_ref, y_ref, dy_ref) in enumerate(((l0, b0, a_ref, da_ref), (l1, b1, b_ref, db_ref), (l2, b2, c_ref, dc_ref))):
            g = _sigmoid(l_ref[...].astype(F32) + bb_ref[...])
            dy_ref[...] = (dmv * g).astype(dy_ref.dtype)
            dl = dmv * y_ref[...].astype(F32) * g * (1.0 - g)
            dl_ref[:, c * d:(c + 1) * d] = dl.astype(dl_ref.dtype)
            dbias_ref[:, c * d:(c + 1) * d] += jnp.sum(dl, axis=0, keepdims=True)

    row = pl.BlockSpec((ts, d), lambda i: (i, 0))
    lg = [pl.BlockSpec((ts, d), functools.partial(lambda i, c: (i, c), c=c)) for c in range(3)]
    bs = [pl.BlockSpec((1, d), functools.partial(lambda i, c: (0, c), c=c)) for c in range(3)]
    yshape = jax.ShapeDtypeStruct((s, d), BF16)
    return _pcall(kern, name=name, grid=(s // ts,), in_specs=lg + bs + [row, row, row, row],
                  out_specs=[row, row, row, pl.BlockSpec((ts, 3 * d), lambda i: (i, 0)), pl.BlockSpec((1, 3 * d), lambda i: (0, 0))],
                  out_shape=[yshape] * 3 + [jax.ShapeDtypeStruct((s, 3 * d), BF16), jax.ShapeDtypeStruct((1, 3 * d), F32)],
                  compiler_params=_params("arbitrary"))(logits, logits, logits, bias, bias, bias, ya, yb, yc, dm)


G_FFN1 = ['ffn1_w1', 'ffn1_w3', 'ffn1_w2']
G_FFN2 = ['ffn2_w1', 'ffn2_w3', 'ffn2_w2']
G_MID = [n for n in BIG if n not in G_FFN1 + G_FFN2]


def _ffn_fwd(h, w1, w3, w2, tag, epilogue, side=None):
    carried = None
    if side is None:
        a, b, f = _ffn_up(h, w1, w3, name=f"{tag}_up")
    else:
        (a, b, f), carried = _ffn_up(h, w1, w3, name=f"{tag}_up", side=side)
    if callable(w2):
        w2 = w2(carried)
    outs = _matmul(f, w2, name=f"{tag}_down", alpha=0.5, tm=512, tn=1024, tk=2816, epilogue=epilogue)
    return outs, (h, a, b, f), carried


def _ffn_bwd(x, norm, w1, w3, w2, saved, dy, dyb, tag, side_first=None, side=None, own_side=None):
    h, a, b, f = saved
    dw2 = _matmul(f, dyb, name=f"{tag}_dw2", ta=True, alpha=0.5, tm=1408, tn=1024, tk=2048, side=side_first)
    carried = None
    if side_first is not None:
        dw2, carried = dw2
    if side is None:
        da, db = _ffn_dact(dyb, w2, a, b, name=f"{tag}_dact")
    else:
        (da, db), got = _ffn_dact(dyb, w2, a, b, name=f"{tag}_dact", side=side)
        carried = (carried or []) + got
    dw1 = _matmul(h, da, name=f"{tag}_dw1", ta=True, tm=1024, tn=1408, tk=2048)
    dw3 = _matmul(h, db, name=f"{tag}_dw3", ta=True, tm=1024, tn=1408, tk=2048)
    outs = _matmul(da, w1, name=f"{tag}_dh", tb=True, tm=512, tn=1024, tk=1408, pair2=(db, w3),
                   epilogue=(_epi_rms_bwd, [x, dy], [norm], [F32, BF16], 1),
                   side=None if own_side is None else own_side(dw1, dw3, dw2))
    (dx, dxb, dnorm), own = outs if own_side is not None else (outs, None)
    return dx, dxb, dnorm, dw1, dw3, dw2, carried, own


def _local_step(x, mem, loss_target, wl, ws):
    s, d = x.shape
    assert s % (QB * 16) == 0
    rope = _rope_tables(s)
    bf = {n: wl[n].astype(BF16) for n in BIG}
    w = dict(ws)

    def gather(names):
        return _side([bf[n] for n in names], _two_level_phases())

    def whole(names, gathered):
        return {n: _whole_weight(n, t) for n, t in zip(names, gathered)}

    first_needed = ['ffn1_w1', 'ffn1_w3']
    then_needed = ['ffn1_w2'] + G_MID
    h1, early = _rms_fwd(x, w['ffn1_norm'], name="ffn1_rms", side=gather(first_needed))
    w.update(whole(first_needed, early))
    (x1, h), sv1, late = _ffn_fwd(h1, w['ffn1_w1'], w['ffn1_w3'], lambda got: _whole_weight('ffn1_w2', got[0]), "ffn1",
                                  (_epi_residual_rms, [x], [w['mix_norm']], [F32, BF16], 0),
                                  side=gather(then_needed))
    w.update(whole(then_needed, late))
    p = _matmul(h, w['w_in'], name="in_proj", out_dtype=BF16, tn=1024)
    logits = _matmul(h, w['w_gate'], name="gate_proj", out_dtype=BF16, tn=1024)
    c_qb, c_kb, c_vb, c_qc = 3 * SB_W, 3 * SB_W + DSA_W, 3 * SB_W + 2 * DSA_W, 3 * SB_W + 3 * DSA_W

    oa_t, late = _sb2_fwd(p, name="sb_fwd", side=gather(G_FFN2))
    w.update(whole(G_FFN2, late))
    ya = _matmul(oa_t, w['w_branch_sb'], name="sb_out", out_dtype=BF16)

    qb_n = _qknorm_fwd(p, c_qb, DSA_W, w['qn_dsa'], rope, name="dsa_qnorm", out_dtype=F32)
    kb_n = _qknorm_fwd(p, c_kb, DSA_W, w['kn_dsa'], rope, name="dsa_knorm", out_dtype=F32)
    vb32 = p[:, c_vb:c_vb + DSA_W].astype(F32)
    groups = range(len(DSA_GROUPS))
    ob_t, lse_b = _dsa2_combine([_dsa2_fwd(qb_n, kb_n, vb32, gi, name=f"dsa_fwd{gi}") for gi in groups], name="dsa_combine")
    yb = _matmul(ob_t, w['w_branch_dsa'], name="dsa_out", out_dtype=BF16)

    memh = _rms_fwd(mem, w['mem_norm'], name="mem_rms")
    kv = _matmul(memh, w['w_mem_kv'], name="mem_kv", out_dtype=BF16)
    km_n = _qknorm_fwd(kv, 0, MEM_W, w['kn_mem'], None, name="mem_knorm")
    qc_n = _qknorm_fwd(p, c_qc, MEM_W, w['qn_mem'], None, name="mem_qnorm")
    oc_t = _mem2_fwd(qc_n, km_n, kv, name="mem_fwd")
    yc = _matmul(oc_t, w['w_branch_mem'], name="mem_out", out_dtype=BF16)

    merged = _merge_fwd(logits, w['b_gate'], ya, yb, yc, name="merge")
    x2, h2 = _matmul(merged, w['w_out'], name="out_proj", tn=1024,
                     epilogue=(_epi_residual_rms, [x1], [w['ffn2_norm']], [F32, BF16], 0))
    (dx3, dx3b, sq), sv2, _ = _ffn_fwd(h2, w['ffn2_w1'], w['ffn2_w3'], w['ffn2_w2'], "ffn2",
                                       (_epi_loss, [x2, loss_target], [], [F32, BF16], 1))
    loss = jnp.sum(sq) * (0.5 / d)

    g, recv = {}, {}

    def owners(names):
        return [_for_owners(n, g[n], wl[n].shape) for n in names]

    dx2, dx2b, g['ffn2_norm'], g['ffn2_w1'], g['ffn2_w3'], g['ffn2_w2'], _, _ = _ffn_bwd(
        x2, w['ffn2_norm'], w['ffn2_w1'], w['ffn2_w3'], w['ffn2_w2'], sv2, dx3, dx3b, "ffn2")

    g['w_out'] = _matmul(merged, dx2b, name="d_w_out", ta=True, tn=1024, tk=512)
    dm = _matmul(dx2b, w['w_out'], name="d_merged", tb=True, out_dtype=BF16, tn=1024)
    dya, dyb, dyc, dlogits, g['b_gate'] = _merge_bwd(logits, w['b_gate'], ya, yb, yc, dm, name="d_merge")

    g['w_branch_sb'] = _matmul(oa_t, dya, name="d_w_sb", ta=True, tn=1024, tk=512)
    g['w_branch_dsa'] = _matmul(ob_t, dyb, name="d_w_dsa", ta=True, tk=512)
    g['w_branch_mem'] = _matmul(oc_t, dyc, name="d_w_mem", ta=True, tk=512)
    doa = _matmul(dya, w['w_branch_sb'], name="d_oa", tb=True, out_dtype=BF16)
    dob = _matmul(dyb, w['w_branch_dsa'], name="d_ob", tb=True)
    doc = _matmul(dyc, w['w_branch_mem'], name="d_oc", tb=True, out_dtype=BF16)

    (dqa, dka, dva), got = _sb2_bwd(p, oa_t, doa, name="sb_bwd", side=_side(owners(G_FFN2), _direct_phases(True)))
    recv.update(zip(G_FFN2, got))

    dd_b = _dsa2_prep(ob_t, dob, name="dsa_prep")
    dgrp = [_dsa2_bwd(qb_n, kb_n, vb32, dob, lse_b, dd_b, gi, name=f"dsa_bwd{gi}") for gi in groups]
    dvb = jnp.concatenate([t[2] for t in dgrp], axis=1).astype(BF16)
    dqb, g['qn_dsa'] = _qknorm_bwd(p, c_qb, DSA_W, w['qn_dsa'], rope, [t[0] for t in dgrp], name="d_dsa_qnorm")
    dkb, g['kn_dsa'] = _qknorm_bwd(p, c_kb, DSA_W, w['kn_dsa'], rope, [t[1] for t in dgrp], name="d_dsa_knorm")

    dqc_n, dkm_n, dvm = _mem2_bwd(qc_n, km_n, kv, doc, name="mem_bwd")
    dqc, g['qn_mem'] = _qknorm_bwd(p, c_qc, MEM_W, w['qn_mem'], None, dqc_n, name="d_mem_qnorm")
    dkm, g['kn_mem'] = _qknorm_bwd(kv, 0, MEM_W, w['kn_mem'], None, dkm_n, name="d_mem_knorm")
    dkv = jnp.concatenate([dkm, dvm.astype(BF16)], axis=1)
    g['w_mem_kv'] = _matmul(memh, dkv, name="d_w_mem_kv", ta=True)
    dmemh = _matmul(dkv, w['w_mem_kv'], name="d_memh", tb=True)
    _, _, g['mem_norm'] = _rms_bwd(mem, w['mem_norm'], dmemh, None, name="d_mem_rms")

    dp = jnp.concatenate([dqa.astype(BF16), dka.astype(BF16), dva.astype(BF16),
                          dqb, dkb, dvb, dqc], axis=1)
    g['w_in'] = _matmul(h, dp, name="d_w_in", ta=True, tn=2048, tk=1024)
    g['w_gate'] = _matmul(h, dlogits, name="d_w_gate", ta=True, tn=1536, tk=1024)
    dh = _matmul(dp, w['w_in'], name="d_h_in", tb=True, tn=1024, tk=2048)
    dx1, dx1b, g['mix_norm'] = _matmul(dlogits, w['w_gate'], name="d_h_gate", tb=True, tm=512, tn=1024, tk=3072,
                                       epilogue=(_epi_rms_bwd_sum, [dh, x1, dx2], [w['mix_norm']], [F32, BF16], 1))

    mid_b = ['w_gate', 'w_out']
    mid_a = [n for n in G_MID if n not in mid_b]

    def own_side(dw1, dw3, dw2):
        g.update(ffn1_w1=dw1, ffn1_w3=dw3, ffn1_w2=dw2)
        return _side(owners(G_FFN1), _direct_phases(True))

    dx0, _, g['ffn1_norm'], _, _, _, got_mid, got_own = _ffn_bwd(
        x, w['ffn1_norm'], w['ffn1_w1'], w['ffn1_w3'], w['ffn1_w2'], sv1, dx1, dx1b, "ffn1",
        side_first=_side(owners(mid_b), _direct_phases(True)), side=_side(owners(mid_a), _direct_phases(True)),
        own_side=own_side)
    recv.update(zip(mid_b + mid_a, got_mid))
    recv.update(zip(G_FFN1, got_own))
    return loss, dx0, recv, {n: g[n] for n in SMALL}


def _whole_weight(name, gathered):
    _, r, c = gathered.shape
    return gathered.reshape(N_DEV * r, c) if SHARD_AXIS[name] == 0 else gathered.transpose(1, 0, 2).reshape(r, N_DEV * c)


def _for_owners(name, grad, shard_shape):
    r, c = shard_shape
    blk = grad.reshape(N_DEV, r, c) if SHARD_AXIS[name] == 0 else grad.reshape(r, N_DEV, c).transpose(1, 0, 2)
    return blk.astype(BF16)


def _pack_small(d, names, extra_rows):
    parts = []
    for n in names:
        v = d[n].reshape(-1)
        pad = (-v.size) % LANES
        parts.append(jnp.concatenate([v, jnp.zeros((pad,), v.dtype)]).reshape(-1, LANES))
    t = jnp.concatenate(parts, axis=0)
    return jnp.concatenate([t, jnp.zeros((extra_rows, LANES), t.dtype)], axis=0)


def _unpack_small(t, like, names):
    out, off = {}, 0
    for n in names:
        size = like[n].size
        rows = -(-size // LANES)
        out[n] = t[off:off + rows].reshape(-1)[:size].reshape(like[n].shape)
        off += rows
    return out


def _direct_phases(per_peer):
    def descriptors(src_ref, out_ref, send_sems, recv_sems, local_sem):
        x, y, c = lax.axis_index("x"), lax.axis_index("y"), lax.axis_index("c")
        me = 4 * x + 2 * y + c
        mine = pltpu.make_async_copy(src_ref.at[me] if per_peer else src_ref, out_ref.at[me], local_sem)
        copies = []
        for k in range(1, N_DEV):
            px = 1 - x if k & 4 else x
            py = 1 - y if k & 2 else y
            pc = 1 - c if k & 1 else c
            copies.append(pltpu.make_async_remote_copy(
                src_ref=src_ref.at[4 * px + 2 * py + pc] if per_peer else src_ref, dst_ref=out_ref.at[me],
                send_sem=send_sems.at[k - 1], recv_sem=recv_sems.at[k - 1],
                device_id=(px, py, pc), device_id_type=pl.DeviceIdType.MESH))
        return mine, copies

    def start(*refs):
        mine, copies = descriptors(*refs)
        mine.start()
        for cp in copies:
            cp.start()

    def forward(*refs):
        pass

    def finish(*refs):
        mine, copies = descriptors(*refs)
        for cp in copies:
            cp.wait_recv()
        for cp in copies:
            cp.wait_send()
        mine.wait()

    return start, forward, finish


def _exchange_parts(srcs, phases):
    n = len(srcs)
    shapes = [jax.ShapeDtypeStruct((N_DEV,) + tuple(s.shape[-2:]), s.dtype) for s in srcs]
    sems = [pltpu.SemaphoreType.DMA((n, N_DEV - 1)), pltpu.SemaphoreType.DMA((n, N_DEV - 1)), pltpu.SemaphoreType.DMA((n,))]

    def lift(phase):
        def run(src_refs, out_refs, send, recv, local):
            for a, (s_ref, o_ref) in enumerate(zip(src_refs, out_refs)):
                phase(s_ref, o_ref, send.at[a], recv.at[a], local.at[a])
        return run

    return shapes, sems, [lift(p) for p in phases]


def _exchange(srcs, phases, *, name):
    shapes, sems, runs = _exchange_parts(srcs, phases)
    n = len(srcs)

    def body(*refs):
        for run in runs:
            run(refs[:n], refs[n:2 * n], *refs[2 * n:])

    anyspace = pl.BlockSpec(memory_space=pl.ANY)
    return _pcall(body, name=name, in_specs=[anyspace] * n, out_specs=[anyspace] * n, out_shape=shapes, scratch_shapes=sems)(*srcs)


def _side(srcs, phases):
    shapes, sems, (start, forward, finish) = _exchange_parts(srcs, phases)

    def before(first, mid, ins, outs, scratch):
        pl.when(first)(lambda: start(ins, outs, *scratch))
        pl.when(mid)(lambda: forward(ins, outs, *scratch))

    def after(last, ins, outs, scratch):
        pl.when(last)(lambda: finish(ins, outs, *scratch))

    return list(srcs), shapes, sems, before, after


def _call_2d(kern, *, name, grid, in_specs, out_specs, out_shape, ins, scratch_shapes=(), semantics, side=None):
    if side is None:
        return _pcall(kern, name=name, grid=grid, in_specs=in_specs, out_specs=out_specs, out_shape=out_shape,
                      scratch_shapes=list(scratch_shapes), compiler_params=_params(*semantics))(*ins)
    s_ins, s_shapes, s_scratch, before, after = side
    n_in, n_out, n_scr = len(ins), len(out_shape), len(scratch_shapes)

    def combined(*refs):
        refs = list(refs)
        cut = [n_in, len(s_ins), n_out, len(s_shapes), n_scr, len(s_scratch)]
        parts, pos = [], 0
        for c in cut:
            parts.append(refs[pos:pos + c])
            pos += c
        m_in, c_in, m_out, c_out, m_scr, c_scr = parts
        ids = [pl.program_id(a) for a in range(len(grid))]
        inner_zero = functools.reduce(jnp.logical_and, [i == 0 for i in ids[1:]], True)
        first = jnp.logical_and(ids[0] == 0, inner_zero)
        mid = jnp.logical_and(ids[0] == grid[0] // 2, inner_zero)
        last = functools.reduce(jnp.logical_and, [i == n - 1 for i, n in zip(ids, grid)])
        before(first, mid, c_in, c_out, c_scr)
        kern(*m_in, *m_out, *m_scr)
        after(last, c_in, c_out, c_scr)

    anyspace = pl.BlockSpec(memory_space=pl.ANY)
    outs = _pcall(combined, name=name, grid=grid, in_specs=list(in_specs) + [anyspace] * len(s_ins),
                  out_specs=list(out_specs) + [anyspace] * len(s_shapes), out_shape=list(out_shape) + s_shapes,
                  scratch_shapes=list(scratch_shapes) + s_scratch, compiler_params=_params(*["arbitrary"] * len(grid)))(*ins, *s_ins)
    return outs[:n_out], outs[n_out:]


def _two_level_phases():
    def parts(src_ref, out_ref, send_sems, recv_sems, local_sem):
        x, y, c = lax.axis_index("x"), lax.axis_index("y"), lax.axis_index("c")
        me, sibling = (x, y, c), (x, y, 1 - c)
        chips = [(1 - x, y), (x, 1 - y), (1 - x, 1 - y)]

        def slab(px, py, pc):
            return out_ref.at[4 * px + 2 * py + pc]

        def copy(k, block, to, from_src=False):
            return pltpu.make_async_remote_copy(
                src_ref=src_ref if from_src else slab(*block), dst_ref=slab(*block),
                send_sem=send_sems.at[k], recv_sem=recv_sems.at[k], device_id=to, device_id_type=pl.DeviceIdType.MESH)

        return dict(
            mine=lambda: pltpu.make_async_copy(src_ref, slab(*me), local_sem),
            first=lambda: [copy(0, me, sibling, True)] + [copy(1 + j, me, (*chip, c), True) for j, chip in enumerate(chips)],
            passed=lambda: [copy(4 + j, (*chip, c), sibling) for j, chip in enumerate(chips)],
            landed=lambda: [copy(1 + j, (*chip, c), me) for j, chip in enumerate(chips)],
            late=lambda: [copy(0, sibling, me)] + [copy(4 + j, (*chip, 1 - c), me) for j, chip in enumerate(chips)])

    def start(*refs):
        make = parts(*refs)
        make['mine']().start()
        for cp in make['first']():
            cp.start()

    def forward(*refs):
        make = parts(*refs)
        for arrived, onward in zip(make['landed'](), make['passed']()):
            arrived.wait_recv()
            onward.start()

    def finish(*refs):
        make = parts(*refs)
        for cp in make['late']():
            cp.wait_recv()
        for cp in make['first']() + make['passed']():
            cp.wait_send()
        make['mine']().wait()

    return start, forward, finish


def _adamw(recv, w, m, v, *, name):
    rows, cols = w.shape
    tr = _pick(rows, (256, 128, 64))

    def kern(r_ref, w_ref, m_ref, v_ref, g_ref, d_ref, mo_ref, vo_ref):
        g = r_ref[0].astype(F32)
        for p in range(1, N_DEV):
            g = g + r_ref[p].astype(F32)
        mn = ADAM_B1 * m_ref[...] + (1.0 - ADAM_B1) * g
        vn = ADAM_B2 * v_ref[...] + (1.0 - ADAM_B2) * (g * g)
        m_hat = mn / (1.0 - ADAM_B1 ** ADAM_STEP)
        v_hat = vn / (1.0 - ADAM_B2 ** ADAM_STEP)
        g_ref[...] = g
        d_ref[...] = -ADAM_LR * (m_hat / (jnp.sqrt(v_hat) + ADAM_EPS) + ADAM_WD * w_ref[...])
        mo_ref[...] = mn
        vo_ref[...] = vn

    row = pl.BlockSpec((tr, cols), lambda i: (i, 0))
    shp = jax.ShapeDtypeStruct((rows, cols), F32)
    return _pcall(kern, name=name, grid=(rows // tr,), in_specs=[pl.BlockSpec((N_DEV, tr, cols), lambda i: (0, i, 0)), row, row, row],
                  out_specs=[row, row, row, row], out_shape=[shp, shp, shp, shp], compiler_params=_params("parallel"))(recv, w, m, v)


INPUTS = ['x', 'mem'] + WEIGHTS + ['loss_target'] + ['m_' + n for n in WEIGHTS] + ['v_' + n for n in WEIGHTS]
SMALL_PAD_ROWS = 4


def kernel(x, mem, ffn1_norm, ffn1_w1, ffn1_w3, ffn1_w2, mix_norm, mem_norm, w_in, w_mem_kv, qn_dsa, kn_dsa, qn_mem, kn_mem, w_branch_sb, w_branch_dsa, w_branch_mem, w_gate, b_gate, w_out, ffn2_norm, ffn2_w1, ffn2_w3, ffn2_w2, loss_target, m_ffn1_norm, m_ffn1_w1, m_ffn1_w3, m_ffn1_w2, m_mix_norm, m_mem_norm, m_w_in, m_w_mem_kv, m_qn_dsa, m_kn_dsa, m_qn_mem, m_kn_mem, m_w_branch_sb, m_w_branch_dsa, m_w_branch_mem, m_w_gate, m_b_gate, m_w_out, m_ffn2_norm, m_ffn2_w1, m_ffn2_w3, m_ffn2_w2, v_ffn1_norm, v_ffn1_w1, v_ffn1_w3, v_ffn1_w2, v_mix_norm, v_mem_norm, v_w_in, v_w_mem_kv, v_qn_dsa, v_kn_dsa, v_qn_mem, v_kn_mem, v_w_branch_sb, v_w_branch_dsa, v_w_branch_mem, v_w_gate, v_b_gate, v_w_out, v_ffn2_norm, v_ffn2_w1, v_ffn2_w3, v_ffn2_w2):
    given = dict(zip(INPUTS, (x, mem, ffn1_norm, ffn1_w1, ffn1_w3, ffn1_w2, mix_norm, mem_norm, w_in, w_mem_kv, qn_dsa, kn_dsa, qn_mem, kn_mem, w_branch_sb, w_branch_dsa, w_branch_mem, w_gate, b_gate, w_out, ffn2_norm, ffn2_w1, ffn2_w3, ffn2_w2, loss_target, m_ffn1_norm, m_ffn1_w1, m_ffn1_w3, m_ffn1_w2, m_mix_norm, m_mem_norm, m_w_in, m_w_mem_kv, m_qn_dsa, m_kn_dsa, m_qn_mem, m_kn_mem, m_w_branch_sb, m_w_branch_dsa, m_w_branch_mem, m_w_gate, m_b_gate, m_w_out, m_ffn2_norm, m_ffn2_w1, m_ffn2_w3, m_ffn2_w2, v_ffn1_norm, v_ffn1_w1, v_ffn1_w3, v_ffn1_w2, v_mix_norm, v_mem_norm, v_w_in, v_w_mem_kv, v_qn_dsa, v_kn_dsa, v_qn_mem, v_kn_mem, v_w_branch_sb, v_w_branch_dsa, v_w_branch_mem, v_w_gate, v_b_gate, v_w_out, v_ffn2_norm, v_ffn2_w1, v_ffn2_w3, v_ffn2_w2), strict=True))
    wl = {n: given[n][0] for n in BIG}
    ws = {n: given[n] for n in SMALL}

    loss, dx, recv, g = _local_step(x[0], mem[0], loss_target[0], wl, ws)

    big = [{}, {}, {}, {}]
    for n in G_FFN2 + G_MID + G_FFN1:
        outs = _adamw(recv[n], wl[n], given['m_' + n][0], given['v_' + n][0], name=f"adamw_{n}")
        for kind, t in enumerate(outs):
            big[kind][n] = t

    gs = _pack_small(g, SMALL, SMALL_PAD_ROWS)
    loss_row = gs.shape[0] - SMALL_PAD_ROWS
    gs = gs.at[loss_row, 0].set(loss)
    recv_s = _exchange([gs], _direct_phases(False), name="gather_small")[0]
    small = _adamw(recv_s, _pack_small(ws, SMALL, SMALL_PAD_ROWS), _pack_small({n: given['m_' + n] for n in SMALL}, SMALL, SMALL_PAD_ROWS),
                   _pack_small({n: given['v_' + n] for n in SMALL}, SMALL, SMALL_PAD_ROWS), name="adamw_replicated")
    total_loss = small[0][loss_row, 0]
    small = [_unpack_small(t, ws, SMALL) for t in small]

    outs = [total_loss, dx[None]]
    for kind in range(4):
        outs += [big[kind][n][None] if n in wl else small[kind][n] for n in WEIGHTS]
    return tuple(outs)
```

```python
import functools

import jax
import jax.numpy as jnp
from jax import lax
from jax.experimental import pallas as pl
from jax.experimental.pallas import tpu as pltpu

F32 = jnp.float32
BF16 = jnp.bfloat16
MXU_DT = jnp.bfloat16

N_DEV = 8
HEAD_DIM = 64
SB_HEADS = 8
DSA_GROUPS = ((128, 1), (512, 4), (2048, 16))
DSA_HPG = 4
MEM_HEADS = 4
SB_W = SB_HEADS * HEAD_DIM
DSA_W = DSA_HPG * len(DSA_GROUPS) * HEAD_DIM
DSA_OUT_W = DSA_HPG * HEAD_DIM
MEM_W = MEM_HEADS * HEAD_DIM
ROPE_THETA = 10000.0
NORM_EPS = 1e-6
QB = 128
SCALE = HEAD_DIM ** -0.5
ADAM_LR, ADAM_B1, ADAM_B2, ADAM_EPS, ADAM_WD, ADAM_STEP = 0.001, 0.9, 0.999, 1e-08, 0.01, 10

LANES = 128
VMEM_LIMIT = 48 * 1024 * 1024
SB_DEAD = -110.0 * 1.4426950408889634

WEIGHTS = ['ffn1_norm', 'ffn1_w1', 'ffn1_w3', 'ffn1_w2', 'mix_norm', 'mem_norm', 'w_in', 'w_mem_kv', 'qn_dsa', 'kn_dsa',
           'qn_mem', 'kn_mem', 'w_branch_sb', 'w_branch_dsa', 'w_branch_mem', 'w_gate', 'b_gate', 'w_out', 'ffn2_norm',
           'ffn2_w1', 'ffn2_w3', 'ffn2_w2']
SHARD_AXIS = {'ffn1_norm': None, 'ffn1_w1': 1, 'ffn1_w3': 1, 'ffn1_w2': 0, 'mix_norm': None, 'mem_norm': None, 'w_in': 1,
              'w_mem_kv': 0, 'qn_dsa': None, 'kn_dsa': None, 'qn_mem': None, 'kn_mem': None, 'w_branch_sb': 1,
              'w_branch_dsa': 1, 'w_branch_mem': 1, 'w_gate': 1, 'b_gate': None, 'w_out': 0, 'ffn2_norm': None,
              'ffn2_w1': 1, 'ffn2_w3': 1, 'ffn2_w2': 0}
BIG = [n for n in WEIGHTS if SHARD_AXIS[n] is not None]
SMALL = [n for n in WEIGHTS if SHARD_AXIS[n] is None]


def _pcall(kern, **kw):
    return pl.pallas_call(kern, **kw)


def _params(*sem):
    return pltpu.CompilerParams(dimension_semantics=sem, vmem_limit_bytes=VMEM_LIMIT)


def _dot(a, b, dims):
    return lax.dot_general(a.astype(MXU_DT), b.astype(MXU_DT), (dims, ((), ())), preferred_element_type=F32)


def _nn(a, b):
    return _dot(a, b, ((1,), (0,)))


def _nt(a, b):
    return _dot(a, b, ((1,), (1,)))


def _tn(a, b):
    return _dot(a, b, ((0,), (0,)))


def _pick(n, prefs):
    for p in prefs:
        if n % p == 0:
            return p
    return n


def _matmul(a, b, *, name, ta=False, tb=False, out_dtype=F32, res=None, alpha=1.0, tm=1024, tn=512, tk=1024, pair2=None,
            epilogue=None, side=None):
    if ta:
        kdim, m = a.shape
    else:
        m, kdim = a.shape
    n = b.shape[0] if tb else b.shape[1]
    tm = _pick(m, (tm, 512, 256, 128))
    tn = _pick(n, (tn, 512, 384, 256, 128))
    tk = _pick(kdim, (tk, 1024, 512, 256, 128))
    nk = kdim // tk
    a_spec = pl.BlockSpec((tk, tm), lambda i, j, k: (k, i)) if ta else pl.BlockSpec((tm, tk), lambda i, j, k: (i, k))
    b_spec = pl.BlockSpec((tn, tk), lambda i, j, k: (j, k)) if tb else pl.BlockSpec((tk, tn), lambda i, j, k: (k, j))
    o_spec = pl.BlockSpec((tm, tn), lambda i, j, k: (i, j))
    v_spec = pl.BlockSpec((1, tn), lambda i, j, k: (0, j))
    dims = ((0 if ta else 1,), (1 if tb else 0,))
    n_mm = 2 if pair2 is None else 4
    if epilogue is None:
        row_ins, vec_ins = ([] if res is None else [res]), []
        out_dtypes, n_vec = [out_dtype], 0
    else:
        assert tn == n and res is None
        epi_fn, row_ins, vec_ins, out_dtypes, n_vec = epilogue
    n_row_out = len(out_dtypes)

    def kern(*refs):
        refs = list(refs)
        acc_ref = refs.pop() if nk > 1 else None
        mm = refs[:n_mm]
        extra = refs[n_mm:n_mm + len(row_ins) + len(vec_ins)]
        outs = refs[n_mm + len(extra):]
        i = pl.program_id(0)
        k = pl.program_id(2)

        def product():
            part = _dot(mm[0][...], mm[1][...], dims)
            if pair2 is not None:
                part = part + _dot(mm[2][...], mm[3][...], dims)
            return part

        def finish(r):
            if alpha != 1.0:
                r = r * alpha
            if epilogue is None:
                if extra:
                    r = extra[0][...] + r
                outs[0][...] = r.astype(out_dtype)
                return
            vals = epi_fn(r, *[e[...] for e in extra])
            for o_ref, v in zip(outs[:n_row_out], vals[:n_row_out]):
                o_ref[...] = v.astype(o_ref.dtype)
            for o_ref, v in zip(outs[n_row_out:], vals[n_row_out:]):
                @pl.when(i == 0)
                def _():
                    o_ref[...] = jnp.zeros_like(o_ref)

                o_ref[...] += v

        if nk == 1:
            finish(product())
            return

        @pl.when(k == 0)
        def _():
            acc_ref[...] = jnp.zeros_like(acc_ref)

        acc_ref[...] += product()

        @pl.when(k == nk - 1)
        def _():
            finish(acc_ref[...])

    ins = [a, b] + ([] if pair2 is None else list(pair2)) + list(row_ins) + list(vec_ins)
    specs = [a_spec, b_spec] * (n_mm // 2) + [o_spec] * len(row_ins) + [v_spec] * len(vec_ins)
    out_specs = [o_spec] * n_row_out + [v_spec] * n_vec
    out_shape = [jax.ShapeDtypeStruct((m, n), dt) for dt in out_dtypes] + [jax.ShapeDtypeStruct((1, n), F32)] * n_vec
    outs = _call_2d(kern, name=name, grid=(m // tm, n // tn, nk), in_specs=specs, out_specs=out_specs, out_shape=out_shape,
                    ins=ins, scratch_shapes=[pltpu.VMEM((tm, tn), F32)] if nk > 1 else [],
                    semantics=("arbitrary" if n_vec else "parallel", "parallel", "arbitrary"), side=side)
    carried = None
    if side is not None:
        outs, carried = outs
    outs = outs[0] if epilogue is None else outs
    return outs if side is None else (outs, carried)


def _epi_residual_rms(r, res, gain):
    xn = res + r
    return xn, xn * lax.rsqrt(jnp.mean(xn * xn, axis=-1, keepdims=True) + NORM_EPS) * gain


def _epi_rms_bwd(r, x, dres, gain):
    rs = lax.rsqrt(jnp.mean(x * x, axis=-1, keepdims=True) + NORM_EPS)
    xh = x * rs
    dy = r * gain
    dx = dres + rs * (dy - xh * jnp.mean(dy * xh, axis=-1, keepdims=True))
    return dx, dx, jnp.sum(r * xh, axis=0, keepdims=True)


def _epi_rms_bwd_sum(r, r0, x, dres, gain):
    return _epi_rms_bwd(r + r0, x, dres, gain)


def _epi_loss(r, res, target):
    e = (res + r) - target
    dy = e / e.shape[-1]
    return dy, dy, jnp.sum(e * e, axis=0, keepdims=True)
def _rms_fwd(x, g, *, name, side=None):
    s, d = x.shape
    ts = _pick(s, (512, 256))

    def kern(x_ref, g_ref, h_ref):
        xf = x_ref[...]
        r = lax.rsqrt(jnp.mean(xf * xf, axis=-1, keepdims=True) + NORM_EPS)
        h_ref[...] = (xf * r * g_ref[...]).astype(h_ref.dtype)

    outs = _call_2d(kern, name=name, grid=(s // ts,),
                    in_specs=[pl.BlockSpec((ts, d), lambda i: (i, 0)), pl.BlockSpec((1, d), lambda i: (0, 0))],
                    out_specs=[pl.BlockSpec((ts, d), lambda i: (i, 0))], out_shape=[jax.ShapeDtypeStruct((s, d), BF16)],
                    ins=[x, g], semantics=("parallel",), side=side)
    return outs[0] if side is None else (outs[0][0], outs[1])


def _rms_bwd(x, g, dh, res, *, name):
    s, d = x.shape
    ts = _pick(s, (512, 256))

    def kern(*refs):
        if res is None:
            x_ref, g_ref, dh_ref, dx_ref, dxb_ref, dg_ref = refs
            r_ref = None
        else:
            x_ref, g_ref, dh_ref, r_ref, dx_ref, dxb_ref, dg_ref = refs
        xf = x_ref[...]
        r = lax.rsqrt(jnp.mean(xf * xf, axis=-1, keepdims=True) + NORM_EPS)
        xh = xf * r
        dhf = dh_ref[...].astype(F32)
        dy = dhf * g_ref[...]
        dx = r * (dy - xh * jnp.mean(dy * xh, axis=-1, keepdims=True))
        if r_ref is not None:
            dx = r_ref[...] + dx
        dx_ref[...] = dx
        dxb_ref[...] = dx.astype(dxb_ref.dtype)

        @pl.when(pl.program_id(0) == 0)
        def _():
            dg_ref[...] = jnp.zeros_like(dg_ref)

        dg_ref[...] += jnp.sum(dhf * xh, axis=0, keepdims=True)

    row = pl.BlockSpec((ts, d), lambda i: (i, 0))
    vec = pl.BlockSpec((1, d), lambda i: (0, 0))
    ins = [x, g, dh] + ([] if res is None else [res])
    return _pcall(kern, name=name, grid=(s // ts,), in_specs=[row, vec, row] + ([] if res is None else [row]),
                  out_specs=[row, row, vec],
                  out_shape=[jax.ShapeDtypeStruct((s, d), F32), jax.ShapeDtypeStruct((s, d), BF16), jax.ShapeDtypeStruct((1, d), F32)],
                  compiler_params=_params("arbitrary"))(*ins)


def _sigmoid(x):
    return 1.0 / (1.0 + jnp.exp(-x))


FFN_TM, FFN_TF = 512, 1408


def _ffn_up(h, w1, w3, *, name, side=None):
    s, d = h.shape
    fdim = w1.shape[1]
    tm, tf = _pick(s, (FFN_TM, 256)), _pick(fdim, (FFN_TF, 512, 256, 128))

    def kern(h_ref, w1_ref, w3_ref, a_ref, b_ref, f_ref):
        hb = h_ref[...]
        a = _nn(hb, w1_ref[...])
        b = _nn(hb, w3_ref[...])
        a_ref[...] = a.astype(a_ref.dtype)
        b_ref[...] = b.astype(b_ref.dtype)
        f_ref[...] = (a * _sigmoid(a) * b).astype(f_ref.dtype)

    wspec = pl.BlockSpec((d, tf), lambda i, j: (0, j))
    ospec = pl.BlockSpec((tm, tf), lambda i, j: (i, j))
    shp = jax.ShapeDtypeStruct((s, fdim), BF16)
    return _call_2d(kern, name=name, grid=(s // tm, fdim // tf), in_specs=[pl.BlockSpec((tm, d), lambda i, j: (i, 0)), wspec, wspec],
                    out_specs=[ospec, ospec, ospec], out_shape=[shp, shp, shp], ins=[h, w1, w3],
                    semantics=("parallel", "parallel"), side=side)


def _ffn_dact(dy, w2, a, b, *, name, side=None):
    s, d = dy.shape
    fdim = w2.shape[0]
    tm, tf = _pick(s, (FFN_TM, 256)), _pick(fdim, (FFN_TF, 512, 256, 128))

    def kern(dy_ref, w2_ref, a_ref, b_ref, da_ref, db_ref):
        df = _nt(dy_ref[...], w2_ref[...]) * 0.5
        av = a_ref[...].astype(F32)
        sg = _sigmoid(av)
        da_ref[...] = (df * b_ref[...].astype(F32) * (sg + av * sg * (1.0 - sg))).astype(da_ref.dtype)
        db_ref[...] = (df * (av * sg)).astype(db_ref.dtype)

    ospec = pl.BlockSpec((tm, tf), lambda i, j: (i, j))
    shp = jax.ShapeDtypeStruct((s, fdim), BF16)
    return _call_2d(kern, name=name, grid=(s // tm, fdim // tf),
                    in_specs=[pl.BlockSpec((tm, d), lambda i, j: (i, 0)), pl.BlockSpec((tf, d), lambda i, j: (j, 0)), ospec, ospec],
                    out_specs=[ospec, ospec], out_shape=[shp, shp], ins=[dy, w2, a, b], semantics=("parallel", "parallel"), side=side)


def _head_mean(v, bd):
    outs = []
    for c in range(v.shape[1] // LANES):
        x = v[:, c * LANES:(c + 1) * LANES]
        hi = x.astype(BF16)
        lo = (x - hi.astype(F32)).astype(BF16)
        outs.append(lax.dot_general(jnp.concatenate([hi, lo], axis=1), bd, (((1,), (0,)), ((), ())), preferred_element_type=F32))
    return outs[0] if len(outs) == 1 else jnp.concatenate(outs, axis=1)


def _partner(v):
    w = v.shape[1]
    lane = lax.broadcasted_iota(jnp.int32, v.shape, 1)
    return jnp.where(lane % HEAD_DIM < HEAD_DIM // 2, pltpu.roll(v, w - HEAD_DIM // 2, 1), pltpu.roll(v, HEAD_DIM // 2, 1))


def _block_diag(w=None):
    r = (lax.broadcasted_iota(jnp.int32, (2 * LANES, LANES), 0) % LANES) // HEAD_DIM
    c = lax.broadcasted_iota(jnp.int32, (2 * LANES, LANES), 1) // HEAD_DIM
    return jnp.where(r == c, 1.0 / HEAD_DIM, 0.0).astype(BF16)


def _rope_tables(s):
    half = HEAD_DIM // 2
    inv_freq = jnp.power(ROPE_THETA, -jnp.arange(half, dtype=F32) / half)
    ang = jnp.arange(s).astype(F32)[:, None] * inv_freq[None, :]
    cos, sin = lax.optimization_barrier((jnp.cos(ang), jnp.sin(ang)))
    cos2 = jnp.concatenate([cos, cos, cos, cos], axis=1)
    sin2 = jnp.concatenate([-sin, sin, -sin, sin], axis=1)
    return cos2, sin2


def _qknorm_fwd(src, col0, width, gain, rope, *, name, out_dtype=BF16):
    s = src.shape[0]
    ts = _pick(s, (512, 256))
    cb = col0 // width
    assert col0 % width == 0
    reps = width // LANES
    g = jnp.tile(gain, (1, width // HEAD_DIM))

    def kern(*refs):
        if rope is None:
            x_ref, g_ref, o_ref = refs
        else:
            x_ref, g_ref, c_ref, s_ref, o_ref = refs
        x = x_ref[...].astype(F32)
        bd = _block_diag(width)
        r = lax.rsqrt(_head_mean(x * x, bd) + NORM_EPS)
        y = x * r * g_ref[...]
        if rope is not None:
            y = y * jnp.tile(c_ref[...], (1, reps)) + _partner(y) * jnp.tile(s_ref[...], (1, reps))
        o_ref[...] = y.astype(o_ref.dtype)

    xs = pl.BlockSpec((ts, width), lambda i: (i, cb))
    tab = pl.BlockSpec((ts, LANES), lambda i: (i, 0))
    ins = [src, g] + ([] if rope is None else list(rope))
    specs = [xs, pl.BlockSpec((1, width), lambda i: (0, 0))] + ([] if rope is None else [tab, tab])
    return _pcall(kern, name=name, grid=(s // ts,), in_specs=specs, out_specs=pl.BlockSpec((ts, width), lambda i: (i, 0)),
                  out_shape=jax.ShapeDtypeStruct((s, width), out_dtype), compiler_params=_params("parallel"))(*ins)


def _qknorm_bwd(src, col0, width, gain, rope, dout, *, name):
    s = src.shape[0]
    ts = _pick(s, (512, 256))
    cb = col0 // width
    reps = width // LANES
    g = jnp.tile(gain, (1, width // HEAD_DIM))

    douts = list(dout) if isinstance(dout, (list, tuple)) else [dout]
    piece = width // len(douts)

    def kern(*refs):
        refs = list(refs)
        dg_ref = refs.pop()
        dx_ref = refs.pop()
        do_refs = [refs.pop() for _ in douts][::-1]
        if rope is None:
            x_ref, g_ref = refs
        else:
            x_ref, g_ref, c_ref, s_ref = refs
        x = x_ref[...].astype(F32)
        bd = _block_diag(width)
        r = lax.rsqrt(_head_mean(x * x, bd) + NORM_EPS)
        xh = x * r
        dy = jnp.concatenate([d[...].astype(F32) for d in do_refs], axis=1) if len(do_refs) > 1 else do_refs[0][...].astype(F32)
        if rope is not None:
            dy = dy * jnp.tile(c_ref[...], (1, reps)) + _partner(dy * jnp.tile(s_ref[...], (1, reps)))
        dxh = dy * g_ref[...]
        dx_ref[...] = (r * (dxh - xh * _head_mean(dxh * xh, bd))).astype(dx_ref.dtype)

        @pl.when(pl.program_id(0) == 0)
        def _():
            dg_ref[...] = jnp.zeros_like(dg_ref)

        dg_ref[...] += jnp.sum(dy * xh, axis=0, keepdims=True)

    xs = pl.BlockSpec((ts, width), lambda i: (i, cb))
    row = pl.BlockSpec((ts, width), lambda i: (i, 0))
    vec = pl.BlockSpec((1, width), lambda i: (0, 0))
    tab = pl.BlockSpec((ts, LANES), lambda i: (i, 0))
    ins = [src, g] + ([] if rope is None else list(rope)) + douts
    specs = [xs, vec] + ([] if rope is None else [tab, tab]) + [pl.BlockSpec((ts, piece), lambda i: (i, 0))] * len(douts)
    dx, dg = _pcall(kern, name=name, grid=(s // ts,), in_specs=specs, out_specs=[row, vec],
                    out_shape=[jax.ShapeDtypeStruct((s, width), BF16), jax.ShapeDtypeStruct((1, width), F32)],
                    compiler_params=_params("arbitrary"))(*ins)
    return dx, jnp.sum(dg.reshape(width // HEAD_DIM, HEAD_DIM), axis=0, keepdims=True)


def _tri(strict, n):
    r = lax.broadcasted_iota(jnp.int32, (2 * n, n), 0) % n
    c = lax.broadcasted_iota(jnp.int32, (2 * n, n), 1)
    return jnp.where((r > c) if strict else (r >= c), 1.0, 0.0).astype(BF16)


def _split_dot(v, t2):
    hi = v.astype(BF16)
    lo = (v - hi.astype(F32)).astype(BF16)
    return lax.dot_general(jnp.concatenate([hi, lo], axis=1), t2, (((1,), (0,)), ((), ())), preferred_element_type=F32)


LOG2E = 1.4426950408889634


def _log2_sigmoids(z2):
    lf = -(jnp.maximum(z2, 0.0) + jnp.log2(1.0 + jnp.exp2(-jnp.abs(z2))))
    return z2 + lf, lf


SB2_SUB = 2
SB_KT = 128


def _first_half(shape):
    return lax.broadcasted_iota(jnp.int32, shape, 1) < HEAD_DIM


def _split_pair(t, first):
    zero = jnp.zeros_like(t)
    return [jnp.where(first, t, zero), jnp.where(first, zero, t)]


def _sb2_fwd(p, *, name, side=None):
    s = p.shape[0]
    rq = SB2_SUB * QB
    nq = s // rq
    npair = SB_W // LANES

    def kern(q_ref, k_ref, v_ref, o_ref):
        i = pl.program_id(1)
        first = _first_half((rq, LANES))
        q2 = jnp.concatenate(_split_pair(q_ref[...], first), axis=0)
        t2 = _tri(True, SB_KT)
        rel = lax.broadcasted_iota(jnp.int32, (2 * rq, SB_KT), 1) - lax.broadcasted_iota(jnp.int32, (2 * rq, SB_KT), 0) % rq

        def tile(j, q, rel, carry, acc, masked):
            off = pl.multiple_of(j * SB_KT, SB_KT)
            ls, lf = _log2_sigmoids(_nt(q, k_ref[pl.ds(off, SB_KT), :]) * (SCALE * LOG2E))
            if masked:
                before = rel < i * rq - j * SB_KT
                lf = jnp.where(before, lf, 0.0)
            w = jnp.exp2(ls + _split_dot(lf, t2) + carry)
            if masked:
                w = jnp.where(before, w, 0.0)
            return carry + jnp.sum(lf, axis=1, keepdims=True), acc + _nn(w, v_ref[pl.ds(off, SB_KT), :])

        carry, acc = jnp.zeros((2 * rq, 1), F32), jnp.zeros((2 * rq, LANES), F32)
        for a in range(rq // SB_KT):
            carry, acc = tile(i * (rq // SB_KT) + (rq // SB_KT - 1 - a), q2, rel, carry, acc, True)

        def cond(st):
            return jnp.logical_and(st[0] >= 0, st[1] > 0)

        def body(st):
            carry, acc = tile(st[0], q2, rel, st[2], st[3], False)
            return st[0] - 1, (jnp.max(carry) > SB_DEAD).astype(jnp.int32), carry, acc

        st = lax.while_loop(cond, body, (i * (rq // SB_KT) - 1, jnp.int32(1), carry, acc))
        o_ref[...] = jnp.where(first, st[3][:rq], st[3][rq:])

    outs = _call_2d(kern, name=name, grid=(npair, nq),
                    in_specs=[pl.BlockSpec((rq, LANES), lambda a, i: (i, a)), pl.BlockSpec((s, LANES), lambda a, i: (0, npair + a)),
                              pl.BlockSpec((s, LANES), lambda a, i: (0, 2 * npair + a))],
                    out_specs=[pl.BlockSpec((rq, LANES), lambda a, i: (i, a))], out_shape=[jax.ShapeDtypeStruct((s, SB_W), F32)],
                    ins=[p, p, p], semantics=("parallel", "arbitrary"), side=side)
    return outs[0] if side is None else (outs[0][0], outs[1])


def _sb2_bwd(p, o, do, *, name, side=None):
    s = p.shape[0]
    rq = SB2_SUB * QB
    nq = s // rq
    npair = SB_W // LANES

    def kern(q_ref, k_ref, v_ref, o_ref, do_ref, dq_ref, dk_hbm, dv_hbm, dk_acc, dv_acc, sem):
        pr = pl.program_id(0)
        i = pl.program_id(1)

        @pl.when(i == 0)
        def _():
            dk_acc[...] = jnp.zeros_like(dk_acc)
            dv_acc[...] = jnp.zeros_like(dv_acc)

        first = _first_half((rq, LANES))
        q2 = jnp.concatenate(_split_pair(q_ref[...], first), axis=0)
        do2 = jnp.concatenate(_split_pair(do_ref[...], first), axis=0)
        o2 = o_ref[...]
        dsum = jnp.sum(do2.astype(F32) * jnp.concatenate([o2, o2], axis=0), axis=1, keepdims=True)
        t_strict = _tri(True, SB_KT)
        t_incl = _tri(False, SB_KT)
        rel = lax.broadcasted_iota(jnp.int32, (2 * rq, SB_KT), 1) - lax.broadcasted_iota(jnp.int32, (2 * rq, SB_KT), 0) % rq

        def tile(j, rows, carry, gcarry, dq, masked):
            q, dob, dsm, rel = rows
            off = pl.multiple_of(j * SB_KT, SB_KT)
            kt = k_ref[pl.ds(off, SB_KT), :]
            ls, lf = _log2_sigmoids(_nt(q, kt) * (SCALE * LOG2E))
            if masked:
                before = rel < i * rq - j * SB_KT
                lf = jnp.where(before, lf, 0.0)
            w = jnp.exp2(ls + _split_dot(lf, t_strict) + carry)
            if masked:
                w = jnp.where(before, w, 0.0)
            wr = w.astype(MXU_DT)
            g = _nt(dob, v_ref[pl.ds(off, SB_KT), :]) * wr.astype(F32)
            big_g = dsm - (_split_dot(g, t_incl) + gcarry)
            sig = jnp.exp2(ls)
            dz = g * (1.0 - sig) - sig * big_g
            if masked:
                dz = jnp.where(before, dz, 0.0)
            dz = dz * SCALE
            dk_acc[pl.ds(off, SB_KT), :] += _tn(dz, q)
            dv_acc[pl.ds(off, SB_KT), :] += _tn(wr, dob)
            return (carry + jnp.sum(lf, axis=1, keepdims=True), gcarry + jnp.sum(g, axis=1, keepdims=True),
                    dq + _nn(dz, kt))

        zc = jnp.zeros((2 * rq, 1), F32)
        carry, gcarry, dq = zc, zc, jnp.zeros((2 * rq, LANES), F32)
        whole = (q2, do2, dsum, rel)
        for a in range(rq // SB_KT):
            carry, gcarry, dq = tile(i * (rq // SB_KT) + (rq // SB_KT - 1 - a), whole, carry, gcarry, dq, True)

        def cond(st):
            return jnp.logical_and(st[0] >= 0, st[1] > 0)

        def body(st):
            carry, gcarry, dq = tile(st[0], whole, st[2], st[3], st[4], False)
            return st[0] - 1, (jnp.max(carry) > SB_DEAD).astype(jnp.int32), carry, gcarry, dq

        st = lax.while_loop(cond, body, (i * (rq // SB_KT) - 1, jnp.int32(1), carry, gcarry, dq))
        dq_ref[...] = jnp.where(first, st[4][:rq], st[4][rq:]).astype(dq_ref.dtype)

        @pl.when(i == nq - 1)
        def _():
            cols = pl.ds(pl.multiple_of(pr * LANES, LANES), LANES)
            ck = pltpu.make_async_copy(dk_acc, dk_hbm.at[:, cols], sem.at[0])
            cv = pltpu.make_async_copy(dv_acc, dv_hbm.at[:, cols], sem.at[1])
            ck.start()
            cv.start()
            ck.wait()
            cv.wait()

    blk = pl.BlockSpec((rq, LANES), lambda a, i: (i, a))
    anyspace = pl.BlockSpec(memory_space=pl.ANY)
    shp = jax.ShapeDtypeStruct((s, SB_W), F32)
    return _call_2d(kern, name=name, grid=(npair, nq),
                    in_specs=[blk, pl.BlockSpec((s, LANES), lambda a, i: (0, npair + a)),
                              pl.BlockSpec((s, LANES), lambda a, i: (0, 2 * npair + a)), blk, blk],
                    out_specs=[blk, anyspace, anyspace], out_shape=[jax.ShapeDtypeStruct((s, SB_W), BF16), shp, shp], ins=[p, p, p, o, do],
                    scratch_shapes=[pltpu.VMEM((s, LANES), F32), pltpu.VMEM((s, LANES), F32), pltpu.SemaphoreType.DMA((2,))],
                    semantics=("arbitrary", "arbitrary"), side=side)


def _dsa_rel():
    qi = lax.broadcasted_iota(jnp.int32, (QB, QB), 0)
    kj = lax.broadcasted_iota(jnp.int32, (QB, QB), 1)
    return kj - qi


def _prev_mask(rel, has_prev):
    return rel >= jnp.where(has_prev, 0, QB)


DSA_BT = QB * max(r for _, r in DSA_GROUPS)
DSA_UB = 4


def _bdot(a, b, ca, cb):
    return lax.dot_general(a.astype(MXU_DT), b.astype(MXU_DT), (((ca,), (cb,)), ((0,), (0,))), preferred_element_type=F32)


def _bnt(a, b):
    return _bdot(a, b, 2, 2)


def _bnn(a, b):
    return _bdot(a, b, 2, 1)


def _btn(a, b):
    return _bdot(a, b, 1, 1)


def _unit_rows(r, c, b):
    return pl.ds(c + QB * r * b, QB, stride=r)


def _pair_cols(t, first):
    return [jnp.max(jnp.where(first, t, -jnp.inf), axis=1, keepdims=True),
            jnp.max(jnp.where(first, -jnp.inf, t), axis=1, keepdims=True)]


def _dsa2_fwd(qn, kn, v32, g, *, name):
    s = qn.shape[0]
    r = DSA_GROUPS[g][1]
    nbk = DSA_BT // (QB * r)
    npair = DSA_OUT_W // LANES

    def kern(q_ref, k_ref, kp_ref, v_ref, vp_ref, o_ref, l_ref):
        t = pl.program_id(1)
        first = _first_half((QB, LANES))
        rel = _dsa_rel()
        units = [(c, b) for c in range(r) for b in range(nbk)]
        for u0 in range(0, len(units), DSA_UB):
            batch = units[u0:u0 + DSA_UB]
            qs, kcs, vcs, kps, vps, masks = [], [], [], [], [], []
            for c, b in batch:
                rows = _unit_rows(r, c, b)
                kc, vc = k_ref[rows, :].astype(MXU_DT), v_ref[rows, :].astype(MXU_DT)
                if b > 0:
                    prow = _unit_rows(r, c, b - 1)
                    kpv, vpv, has_prev = k_ref[prow, :], v_ref[prow, :], True
                else:
                    prow = _unit_rows(r, c, nbk - 1)
                    kpv, vpv, has_prev = kp_ref[prow, :], vp_ref[prow, :], t > 0
                for qe in _split_pair(q_ref[rows, :], first):
                    qs.append(qe.astype(MXU_DT))
                    kcs.append(kc)
                    vcs.append(vc)
                    kps.append(kpv.astype(MXU_DT))
                    vps.append(vpv.astype(MXU_DT))
                    masks.append(_prev_mask(rel, has_prev))
            qq = jnp.stack(qs)
            sc = jnp.where(rel <= 0, _bnt(qq, jnp.stack(kcs)) * SCALE, -jnp.inf)
            sp = _bnt(qq, jnp.stack(kps)) * SCALE
            sp = jnp.stack([jnp.where(mk, sp[n], -jnp.inf) for n, mk in enumerate(masks)])
            m = jnp.maximum(jnp.max(sc, axis=2, keepdims=True), jnp.max(sp, axis=2, keepdims=True))
            pc = jnp.exp(sc - m)
            pp = jnp.exp(sp - m)
            den = jnp.sum(pc, axis=2, keepdims=True) + jnp.sum(pp, axis=2, keepdims=True)
            out = (_bnn(pc, jnp.stack(vcs)) + _bnn(pp, jnp.stack(vps))) / den
            lse = m + jnp.log(den)
            for idx, (c, b) in enumerate(batch):
                rows = _unit_rows(r, c, b)
                o_ref[rows, :] = jnp.where(first, out[2 * idx], out[2 * idx + 1])
                l_ref[rows, :] = jnp.where(first, lse[2 * idx], lse[2 * idx + 1])

    npg = DSA_HPG * HEAD_DIM // LANES
    cur = pl.BlockSpec((DSA_BT, LANES), lambda a, t: (t, npg * g + a))
    prev = pl.BlockSpec((DSA_BT, LANES), lambda a, t: (jnp.maximum(t - 1, 0), npg * g + a))
    out = pl.BlockSpec((DSA_BT, LANES), lambda a, t: (t, a))
    shp = jax.ShapeDtypeStruct((s, DSA_OUT_W), F32)
    return _pcall(kern, name=name, grid=(npair, s // DSA_BT), in_specs=[cur, cur, prev, cur, prev], out_specs=[out, out],
                  out_shape=[shp, shp], compiler_params=_params("parallel", "parallel"))(qn, kn, kn, v32, v32)


def _dsa2_combine(parts, *, name):
    s, wd = parts[0][0].shape
    ts = _pick(s, (512, 256))

    def kern(o0, l0, o1, l1, o2, l2, o_ref, l_ref):
        ls = [l0[...], l1[...], l2[...]]
        m = jnp.maximum(jnp.maximum(ls[0], ls[1]), ls[2])
        es = [jnp.exp(l - m) for l in ls]
        den = es[0] + es[1] + es[2]
        o_ref[...] = (es[0] * o0[...] + es[1] * o1[...] + es[2] * o2[...]) / den
        l_ref[...] = m + jnp.log(den)

    blk = pl.BlockSpec((ts, wd), lambda i: (i, 0))
    shp = jax.ShapeDtypeStruct((s, wd), F32)
    flat = [t for pair in parts for t in pair]
    return _pcall(kern, name=name, grid=(s // ts,), in_specs=[blk] * 6, out_specs=[blk, blk], out_shape=[shp, shp],
                  compiler_params=_params("parallel"))(*flat)


def _dsa2_prep(o, do, *, name):
    s, wd = o.shape
    ts = _pick(s, (512, 256))

    def kern(o_ref, do_ref, d_ref):
        d_ref[...] = _head_mean(do_ref[...] * o_ref[...], _block_diag(wd)) * HEAD_DIM

    blk = pl.BlockSpec((ts, wd), lambda i: (i, 0))
    return _pcall(kern, name=name, grid=(s // ts,), in_specs=[blk, blk], out_specs=blk,
                  out_shape=jax.ShapeDtypeStruct((s, wd), F32), compiler_params=_params("parallel"))(o, do)


def _dsa2_bwd(qn, kn, v32, do, lse, dd, g, *, name):
    s = qn.shape[0]
    r = DSA_GROUPS[g][1]
    nbk = DSA_BT // (QB * r)
    npair = DSA_OUT_W // LANES
    nsteps = s // DSA_BT

    def kern(q_ref, qn_ref, k_ref, kp_ref, v_ref, vp_ref, do_ref, don_ref, l_ref, ln_ref, d_ref, dn_ref,
             dq_ref, dk_ref, dv_ref):
        t = pl.program_id(1)
        first = _first_half((QB, LANES))
        rel = _dsa_rel()

        def pairs(items):
            qq = jnp.stack([it[0].astype(MXU_DT) for it in items])
            dd = jnp.stack([it[1].astype(MXU_DT) for it in items])
            kk = jnp.stack([it[4].astype(MXU_DT) for it in items])
            vv = jnp.stack([it[5].astype(MXU_DT) for it in items])
            p = jnp.exp(_bnt(qq, kk) * SCALE - jnp.stack([it[2] for it in items]))
            p = jnp.stack([jnp.where(it[6], p[n], 0.0) for n, it in enumerate(items)])
            ds = p * (_bnt(dd, vv) - jnp.stack([it[3] for it in items])) * SCALE
            return _bnn(ds, kk), _btn(ds, qq), _btn(p, dd)

        def heads(rows, qr, dor, lr, dr):
            return list(zip(_split_pair(qr[rows, :], first), _split_pair(dor[rows, :], first),
                            _pair_cols(lr[rows, :], first), _pair_cols(dr[rows, :], first)))

        units = [(c, b) for c in range(r) for b in range(nbk)]
        dk_of, dv_of = [None] * len(units), [None] * len(units)
        for u0 in range(0, len(units), DSA_UB // 2):
            batch = list(enumerate(units))[u0:u0 + DSA_UB // 2]
            items = []
            for u, (c, b) in batch:
                rows = _unit_rows(r, c, b)
                kc, vc = k_ref[rows, :], v_ref[rows, :]
                if b > 0:
                    prow = _unit_rows(r, c, b - 1)
                    kpv, vpv, pmask = k_ref[prow, :], v_ref[prow, :], _prev_mask(rel, True)
                else:
                    prow = _unit_rows(r, c, nbk - 1)
                    kpv, vpv, pmask = kp_ref[prow, :], vp_ref[prow, :], _prev_mask(rel, t > 0)
                for hd in heads(rows, q_ref, do_ref, l_ref, d_ref):
                    items.append(hd + (kc, vc, rel <= 0))
                    items.append(hd + (kpv, vpv, pmask))
            dq, dk, dv = pairs(items)
            for n, (u, (c, b)) in enumerate(batch):
                dq_ref[_unit_rows(r, c, b), :] = jnp.where(first, dq[4 * n] + dq[4 * n + 1], dq[4 * n + 2] + dq[4 * n + 3])
                dk_of[u] = dk[4 * n] + dk[4 * n + 2]
                dv_of[u] = dv[4 * n] + dv[4 * n + 2]
                if b > 0:
                    dk_of[u - 1] = dk_of[u - 1] + (dk[4 * n + 1] + dk[4 * n + 3])
                    dv_of[u - 1] = dv_of[u - 1] + (dv[4 * n + 1] + dv[4 * n + 3])
        lasts = [c * nbk + nbk - 1 for c in range(r)]
        for c0 in range(0, r, DSA_UB):
            chunk = list(range(c0, min(c0 + DSA_UB, r)))
            items = []
            for c in chunk:
                last = _unit_rows(r, c, nbk - 1)
                for hd in heads(_unit_rows(r, c, 0), qn_ref, don_ref, ln_ref, dn_ref):
                    items.append(hd + (k_ref[last, :], v_ref[last, :], _prev_mask(rel, t < nsteps - 1)))
            _, dk, dv = pairs(items)
            for n, c in enumerate(chunk):
                dk_of[lasts[c]] = dk_of[lasts[c]] + (dk[2 * n] + dk[2 * n + 1])
                dv_of[lasts[c]] = dv_of[lasts[c]] + (dv[2 * n] + dv[2 * n + 1])
        for u, (c, b) in enumerate(units):
            dk_ref[_unit_rows(r, c, b), :] = dk_of[u]
            dv_ref[_unit_rows(r, c, b), :] = dv_of[u]

    npg = DSA_HPG * HEAD_DIM // LANES

    def at(shift, col):
        return pl.BlockSpec((DSA_BT, LANES), lambda a, t: (jnp.clip(t + shift, 0, nsteps - 1), col(a)))

    gcol = lambda a: npg * g + a
    ocol = lambda a: a
    specs = [at(0, gcol), at(1, gcol), at(0, gcol), at(-1, gcol), at(0, gcol), at(-1, gcol),
             at(0, ocol), at(1, ocol), at(0, ocol), at(1, ocol), at(0, ocol), at(1, ocol)]
    shp = jax.ShapeDtypeStruct((s, DSA_OUT_W), F32)
    return _pcall(kern, name=name, grid=(npair, nsteps), in_specs=specs, out_specs=[at(0, ocol)] * 3, out_shape=[shp, shp, shp],
                  compiler_params=_params("parallel", "parallel"))(qn, qn, kn, kn, v32, v32, do, do, lse, lse, dd, dd)


def _mem2_fwd(qn, km, kv, *, name):
    s = qn.shape[0]
    ml = km.shape[0]
    tq = _pick(s, (512, 256))
    npair = MEM_W // LANES

    def kern(q_ref, k_ref, v_ref, o_ref):
        first = _first_half((tq, LANES))
        q2 = jnp.concatenate(_split_pair(q_ref[...], first), axis=0)
        sc = _nt(q2, k_ref[...]) * SCALE
        e = jnp.exp(sc - jnp.max(sc, axis=1, keepdims=True))
        o2 = _nn(e / jnp.sum(e, axis=1, keepdims=True), v_ref[...])
        o_ref[...] = jnp.where(first, o2[:tq], o2[tq:])

    blk = pl.BlockSpec((tq, LANES), lambda a, i: (i, a))
    return _pcall(kern, name=name, grid=(npair, s // tq),
                  in_specs=[blk, pl.BlockSpec((ml, LANES), lambda a, i: (0, a)), pl.BlockSpec((ml, LANES), lambda a, i: (0, npair + a))],
                  out_specs=blk, out_shape=jax.ShapeDtypeStruct((s, MEM_W), F32),
                  compiler_params=_params("parallel", "parallel"))(qn, km, kv)


def _mem2_bwd(qn, km, kv, do, *, name):
    s = qn.shape[0]
    ml = km.shape[0]
    tq = _pick(s, (512, 256))
    npair = MEM_W // LANES

    def kern(q_ref, k_ref, v_ref, do_ref, dq_ref, dk_ref, dv_ref):
        @pl.when(pl.program_id(1) == 0)
        def _():
            dk_ref[...] = jnp.zeros_like(dk_ref)
            dv_ref[...] = jnp.zeros_like(dv_ref)

        first = _first_half((tq, LANES))
        q2 = jnp.concatenate(_split_pair(q_ref[...], first), axis=0)
        do2 = jnp.concatenate(_split_pair(do_ref[...], first), axis=0)
        sc = _nt(q2, k_ref[...]) * SCALE
        e = jnp.exp(sc - jnp.max(sc, axis=1, keepdims=True))
        p = e / jnp.sum(e, axis=1, keepdims=True)
        dp = _nt(do2, v_ref[...])
        ds = p * (dp - jnp.sum(p * dp, axis=1, keepdims=True)) * SCALE
        dq2 = _nn(ds, k_ref[...])
        dk_ref[...] += _tn(ds, q2)
        dv_ref[...] += _tn(p, do2)
        dq_ref[...] = jnp.where(first, dq2[:tq], dq2[tq:])

    blk = pl.BlockSpec((tq, LANES), lambda a, i: (i, a))
    kblk = pl.BlockSpec((ml, LANES), lambda a, i: (0, a))
    kshape = jax.ShapeDtypeStruct((ml, MEM_W), F32)
    return _pcall(kern, name=name, grid=(npair, s // tq),
                  in_specs=[blk, kblk, pl.BlockSpec((ml, LANES), lambda a, i: (0, npair + a)), blk],
                  out_specs=[blk, kblk, kblk], out_shape=[jax.ShapeDtypeStruct((s, MEM_W), F32), kshape, kshape],
                  compiler_params=_params("parallel", "arbitrary"))(qn, km, kv, do)


def _merge_fwd(logits, bias, ya, yb, yc, *, name):
    s, d = ya.shape
    ts = _pick(s, (512, 256))

    def kern(l0, l1, l2, b0, b1, b2, a_ref, b_ref, c_ref, o_ref):
        m = 0.0
        for l_ref, bb_ref, y_ref in ((l0, b0, a_ref), (l1, b1, b_ref), (l2, b2, c_ref)):
            m = m + _sigmoid(l_ref[...].astype(F32) + bb_ref[...]) * y_ref[...].astype(F32)
        o_ref[...] = m.astype(o_ref.dtype)

    row = pl.BlockSpec((ts, d), lambda i: (i, 0))
    lg = [pl.BlockSpec((ts, d), functools.partial(lambda i, c: (i, c), c=c)) for c in range(3)]
    bs = [pl.BlockSpec((1, d), functools.partial(lambda i, c: (0, c), c=c)) for c in range(3)]
    return _pcall(kern, name=name, grid=(s // ts,), in_specs=lg + bs + [row, row, row], out_specs=row,
                  out_shape=jax.ShapeDtypeStruct((s, d), BF16),
                  compiler_params=_params("parallel"))(logits, logits, logits, bias, bias, bias, ya, yb, yc)


def _merge_bwd(logits, bias, ya, yb, yc, dm, *, name):
    s, d = ya.shape
    ts = _pick(s, (256,))

    def kern(l0, l1, l2, b0, b1, b2, a_ref, b_ref, c_ref, dm_ref, da_ref, db_ref, dc_ref, dl_ref, dbias_ref):
        dmv = dm_ref[...].astype(F32)

        @pl.when(pl.program_id(0) == 0)
        def _():
            dbias_ref[...] = jnp.zeros_like(dbias_ref)

        for c, (l_ref, bb_ref, y_ref, dy_ref) in enumerate(((l0, b0, a_ref, da_ref), (l1, b1, b_ref, db_ref), (l2, b2, c_ref, dc_ref))):
            g = _sigmoid(l_ref[...].astype(F32) + bb_ref[...])
            dy_ref[...] = (dmv * g).astype(dy_ref.dtype)
            dl = dmv * y_ref[...].astype(F32) * g * (1.0 - g)
            dl_ref[:, c * d:(c + 1) * d] = dl.astype(dl_ref.dtype)
            dbias_ref[:, c * d:(c + 1) * d] += jnp.sum(dl, axis=0, keepdims=True)

    row = pl.BlockSpec((ts, d), lambda i: (i, 0))
    lg = [pl.BlockSpec((ts, d), functools.partial(lambda i, c: (i, c), c=c)) for c in range(3)]
    bs = [pl.BlockSpec((1, d), functools.partial(lambda i, c: (0, c), c=c)) for c in range(3)]
    yshape = jax.ShapeDtypeStruct((s, d), BF16)
    return _pcall(kern, name=name, grid=(s // ts,), in_specs=lg + bs + [row, row, row, row],
                  out_specs=[row, row, row, pl.BlockSpec((ts, 3 * d), lambda i: (i, 0)), pl.BlockSpec((1, 3 * d), lambda i: (0, 0))],
                  out_shape=[yshape] * 3 + [jax.ShapeDtypeStruct((s, 3 * d), BF16), jax.ShapeDtypeStruct((1, 3 * d), F32)],
                  compiler_params=_params("arbitrary"))(logits, logits, logits, bias, bias, bias, ya, yb, yc, dm)


G_FFN1 = ['ffn1_w1', 'ffn1_w3', 'ffn1_w2']
G_FFN2 = ['ffn2_w1', 'ffn2_w3', 'ffn2_w2']
G_MID = [n for n in BIG if n not in G_FFN1 + G_FFN2]


def _ffn_fwd(h, w1, w3, w2, tag, epilogue, side=None):
    carried = None
    if side is None:
        a, b, f = _ffn_up(h, w1, w3, name=f"{tag}_up")
    else:
        (a, b, f), carried = _ffn_up(h, w1, w3, name=f"{tag}_up", side=side)
    if callable(w2):
        w2 = w2(carried)
    outs = _matmul(f, w2, name=f"{tag}_down", alpha=0.5, tm=512, tn=1024, tk=2816, epilogue=epilogue)
    return outs, (h, a, b, f), carried


def _ffn_bwd(x, norm, w1, w3, w2, saved, dy, dyb, tag, side_first=None, side=None, own_side=None):
    h, a, b, f = saved
    dw2 = _matmul(f, dyb, name=f"{tag}_dw2", ta=True, alpha=0.5, tm=1408, tn=1024, tk=2048, side=side_first)
    carried = None
    if side_first is not None:
        dw2, carried = dw2
    if side is None:
        da, db = _ffn_dact(dyb, w2, a, b, name=f"{tag}_dact")
    else:
        (da, db), got = _ffn_dact(dyb, w2, a, b, name=f"{tag}_dact", side=side)
        carried = (carried or []) + got
    dw1 = _matmul(h, da, name=f"{tag}_dw1", ta=True, tm=1024, tn=1408, tk=2048)
    dw3 = _matmul(h, db, name=f"{tag}_dw3", ta=True, tm=1024, tn=1408, tk=2048)
    outs = _matmul(da, w1, name=f"{tag}_dh", tb=True, tm=512, tn=1024, tk=1408, pair2=(db, w3),
                   epilogue=(_epi_rms_bwd, [x, dy], [norm], [F32, BF16], 1),
                   side=None if own_side is None else own_side(dw1, dw3, dw2))
    (dx, dxb, dnorm), own = outs if own_side is not None else (outs, None)
    return dx, dxb, dnorm, dw1, dw3, dw2, carried, own


def _local_step(x, mem, loss_target, wl, ws):
    s, d = x.shape
    assert s % (QB * 16) == 0
    rope = _rope_tables(s)
    bf = {n: wl[n].astype(BF16) for n in BIG}
    w = dict(ws)

    def gather(names):
        return _side([bf[n] for n in names], _two_level_phases())

    def whole(names, gathered):
        return {n: _whole_weight(n, t) for n, t in zip(names, gathered)}

    first_needed = ['ffn1_w1', 'ffn1_w3']
    then_needed = ['ffn1_w2'] + G_MID
    h1, early = _rms_fwd(x, w['ffn1_norm'], name="ffn1_rms", side=gather(first_needed))
    w.update(whole(first_needed, early))
    (x1, h), sv1, late = _ffn_fwd(h1, w['ffn1_w1'], w['ffn1_w3'], lambda got: _whole_weight('ffn1_w2', got[0]), "ffn1",
                                  (_epi_residual_rms, [x], [w['mix_norm']], [F32, BF16], 0),
                                  side=gather(then_needed))
    w.update(whole(then_needed, late))
    p = _matmul(h, w['w_in'], name="in_proj", out_dtype=BF16, tn=1024)
    logits = _matmul(h, w['w_gate'], name="gate_proj", out_dtype=BF16, tn=1024)
    c_qb, c_kb, c_vb, c_qc = 3 * SB_W, 3 * SB_W + DSA_W, 3 * SB_W + 2 * DSA_W, 3 * SB_W + 3 * DSA_W

    oa_t, late = _sb2_fwd(p, name="sb_fwd", side=gather(G_FFN2))
    w.update(whole(G_FFN2, late))
    ya = _matmul(oa_t, w['w_branch_sb'], name="sb_out", out_dtype=BF16)

    qb_n = _qknorm_fwd(p, c_qb, DSA_W, w['qn_dsa'], rope, name="dsa_qnorm", out_dtype=F32)
    kb_n = _qknorm_fwd(p, c_kb, DSA_W, w['kn_dsa'], rope, name="dsa_knorm", out_dtype=F32)
    vb32 = p[:, c_vb:c_vb + DSA_W].astype(F32)
    groups = range(len(DSA_GROUPS))
    ob_t, lse_b = _dsa2_combine([_dsa2_fwd(qb_n, kb_n, vb32, gi, name=f"dsa_fwd{gi}") for gi in groups], name="dsa_combine")
    yb = _matmul(ob_t, w['w_branch_dsa'], name="dsa_out", out_dtype=BF16)

    memh = _rms_fwd(mem, w['mem_norm'], name="mem_rms")
    kv = _matmul(memh, w['w_mem_kv'], name="mem_kv", out_dtype=BF16)
    km_n = _qknorm_fwd(kv, 0, MEM_W, w['kn_mem'], None, name="mem_knorm")
    qc_n = _qknorm_fwd(p, c_qc, MEM_W, w['qn_mem'], None, name="mem_qnorm")
    oc_t = _mem2_fwd(qc_n, km_n, kv, name="mem_fwd")
    yc = _matmul(oc_t, w['w_branch_mem'], name="mem_out", out_dtype=BF16)

    merged = _merge_fwd(logits, w['b_gate'], ya, yb, yc, name="merge")
    x2, h2 = _matmul(merged, w['w_out'], name="out_proj", tn=1024,
                     epilogue=(_epi_residual_rms, [x1], [w['ffn2_norm']], [F32, BF16], 0))
    (dx3, dx3b, sq), sv2, _ = _ffn_fwd(h2, w['ffn2_w1'], w['ffn2_w3'], w['ffn2_w2'], "ffn2",
                                       (_epi_loss, [x2, loss_target], [], [F32, BF16], 1))
    loss = jnp.sum(sq) * (0.5 / d)

    g, recv = {}, {}

    def owners(names):
        return [_for_owners(n, g[n], wl[n].shape) for n in names]

    dx2, dx2b, g['ffn2_norm'], g['ffn2_w1'], g['ffn2_w3'], g['ffn2_w2'], _, _ = _ffn_bwd(
        x2, w['ffn2_norm'], w['ffn2_w1'], w['ffn2_w3'], w['ffn2_w2'], sv2, dx3, dx3b, "ffn2")

    g['w_out'] = _matmul(merged, dx2b, name="d_w_out", ta=True, tn=1024, tk=512)
    dm = _matmul(dx2b, w['w_out'], name="d_merged", tb=True, out_dtype=BF16, tn=1024)
    dya, dyb, dyc, dlogits, g['b_gate'] = _merge_bwd(logits, w['b_gate'], ya, yb, yc, dm, name="d_merge")

    g['w_branch_sb'] = _matmul(oa_t, dya, name="d_w_sb", ta=True, tn=1024, tk=512)
    g['w_branch_dsa'] = _matmul(ob_t, dyb, name="d_w_dsa", ta=True, tk=512)
    g['w_branch_mem'] = _matmul(oc_t, dyc, name="d_w_mem", ta=True, tk=512)
    doa = _matmul(dya, w['w_branch_sb'], name="d_oa", tb=True, out_dtype=BF16)
    dob = _matmul(dyb, w['w_branch_dsa'], name="d_ob", tb=True)
    doc = _matmul(dyc, w['w_branch_mem'], name="d_oc", tb=True, out_dtype=BF16)

    (dqa, dka, dva), got = _sb2_bwd(p, oa_t, doa, name="sb_bwd", side=_side(owners(G_FFN2), _direct_phases(True)))
    recv.update(zip(G_FFN2, got))

    dd_b = _dsa2_prep(ob_t, dob, name="dsa_prep")
    dgrp = [_dsa2_bwd(qb_n, kb_n, vb32, dob, lse_b, dd_b, gi, name=f"dsa_bwd{gi}") for gi in groups]
    dvb = jnp.concatenate([t[2] for t in dgrp], axis=1).astype(BF16)
    dqb, g['qn_dsa'] = _qknorm_bwd(p, c_qb, DSA_W, w['qn_dsa'], rope, [t[0] for t in dgrp], name="d_dsa_qnorm")
    dkb, g['kn_dsa'] = _qknorm_bwd(p, c_kb, DSA_W, w['kn_dsa'], rope, [t[1] for t in dgrp], name="d_dsa_knorm")

    dqc_n, dkm_n, dvm = _mem2_bwd(qc_n, km_n, kv, doc, name="mem_bwd")
    dqc, g['qn_mem'] = _qknorm_bwd(p, c_qc, MEM_W, w['qn_mem'], None, dqc_n, name="d_mem_qnorm")
    dkm, g['kn_mem'] = _qknorm_bwd(kv, 0, MEM_W, w['kn_mem'], None, dkm_n, name="d_mem_knorm")
    dkv = jnp.concatenate([dkm, dvm.astype(BF16)], axis=1)
    g['w_mem_kv'] = _matmul(memh, dkv, name="d_w_mem_kv", ta=True)
    dmemh = _matmul(dkv, w['w_mem_kv'], name="d_memh", tb=True)
    _, _, g['mem_norm'] = _rms_bwd(mem, w['mem_norm'], dmemh, None, name="d_mem_rms")

    dp = jnp.concatenate([dqa.astype(BF16), dka.astype(BF16), dva.astype(BF16),
                          dqb, dkb, dvb, dqc], axis=1)
    g['w_in'] = _matmul(h, dp, name="d_w_in", ta=True, tn=2048, tk=1024)
    g['w_gate'] = _matmul(h, dlogits, name="d_w_gate", ta=True, tn=1536, tk=1024)
    dh = _matmul(dp, w['w_in'], name="d_h_in", tb=True, tn=1024, tk=2048)
    dx1, dx1b, g['mix_norm'] = _matmul(dlogits, w['w_gate'], name="d_h_gate", tb=True, tm=512, tn=1024, tk=3072,
                                       epilogue=(_epi_rms_bwd_sum, [dh, x1, dx2], [w['mix_norm']], [F32, BF16], 1))

    mid_b = ['w_gate', 'w_out']
    mid_a = [n for n in G_MID if n not in mid_b]

    def own_side(dw1, dw3, dw2):
        g.update(ffn1_w1=dw1, ffn1_w3=dw3, ffn1_w2=dw2)
        return _side(owners(G_FFN1), _direct_phases(True))

    dx0, _, g['ffn1_norm'], _, _, _, got_mid, got_own = _ffn_bwd(
        x, w['ffn1_norm'], w['ffn1_w1'], w['ffn1_w3'], w['ffn1_w2'], sv1, dx1, dx1b, "ffn1",
        side_first=_side(owners(mid_b), _direct_phases(True)), side=_side(owners(mid_a), _direct_phases(True)),
        own_side=own_side)
    recv.update(zip(mid_b + mid_a, got_mid))
    recv.update(zip(G_FFN1, got_own))
    return loss, dx0, recv, {n: g[n] for n in SMALL}


def _whole_weight(name, gathered):
    _, r, c = gathered.shape
    return gathered.reshape(N_DEV * r, c) if SHARD_AXIS[name] == 0 else gathered.transpose(1, 0, 2).reshape(r, N_DEV * c)


def _for_owners(name, grad, shard_shape):
    r, c = shard_shape
    blk = grad.reshape(N_DEV, r, c) if SHARD_AXIS[name] == 0 else grad.reshape(r, N_DEV, c).transpose(1, 0, 2)
    return blk.astype(BF16)


def _pack_small(d, names, extra_rows):
    parts = []
    for n in names:
        v = d[n].reshape(-1)
        pad = (-v.size) % LANES
        parts.append(jnp.concatenate([v, jnp.zeros((pad,), v.dtype)]).reshape(-1, LANES))
    t = jnp.concatenate(parts, axis=0)
    return jnp.concatenate([t, jnp.zeros((extra_rows, LANES), t.dtype)], axis=0)


def _unpack_small(t, like, names):
    out, off = {}, 0
    for n in names:
        size = like[n].size
        rows = -(-size // LANES)
        out[n] = t[off:off + rows].reshape(-1)[:size].reshape(like[n].shape)
        off += rows
    return out


def _direct_phases(per_peer):
    def descriptors(src_ref, out_ref, send_sems, recv_sems, local_sem):
        x, y, c = lax.axis_index("x"), lax.axis_index("y"), lax.axis_index("c")
        me = 4 * x + 2 * y + c
        mine = pltpu.make_async_copy(src_ref.at[me] if per_peer else src_ref, out_ref.at[me], local_sem)
        copies = []
        for k in range(1, N_DEV):
            px = 1 - x if k & 4 else x
            py = 1 - y if k & 2 else y
            pc = 1 - c if k & 1 else c
            copies.append(pltpu.make_async_remote_copy(
                src_ref=src_ref.at[4 * px + 2 * py + pc] if per_peer else src_ref, dst_ref=out_ref.at[me],
                send_sem=send_sems.at[k - 1], recv_sem=recv_sems.at[k - 1],
                device_id=(px, py, pc), device_id_type=pl.DeviceIdType.MESH))
        return mine, copies

    def start(*refs):
        mine, copies = descriptors(*refs)
        mine.start()
        for cp in copies:
            cp.start()

    def forward(*refs):
        pass

    def finish(*refs):
        mine, copies = descriptors(*refs)
        for cp in copies:
            cp.wait_recv()
        for cp in copies:
            cp.wait_send()
        mine.wait()

    return start, forward, finish


def _exchange_parts(srcs, phases):
    n = len(srcs)
    shapes = [jax.ShapeDtypeStruct((N_DEV,) + tuple(s.shape[-2:]), s.dtype) for s in srcs]
    sems = [pltpu.SemaphoreType.DMA((n, N_DEV - 1)), pltpu.SemaphoreType.DMA((n, N_DEV - 1)), pltpu.SemaphoreType.DMA((n,))]

    def lift(phase):
        def run(src_refs, out_refs, send, recv, local):
            for a, (s_ref, o_ref) in enumerate(zip(src_refs, out_refs)):
                phase(s_ref, o_ref, send.at[a], recv.at[a], local.at[a])
        return run

    return shapes, sems, [lift(p) for p in phases]


def _exchange(srcs, phases, *, name):
    shapes, sems, runs = _exchange_parts(srcs, phases)
    n = len(srcs)

    def body(*refs):
        for run in runs:
            run(refs[:n], refs[n:2 * n], *refs[2 * n:])

    anyspace = pl.BlockSpec(memory_space=pl.ANY)
    return _pcall(body, name=name, in_specs=[anyspace] * n, out_specs=[anyspace] * n, out_shape=shapes, scratch_shapes=sems)(*srcs)


def _side(srcs, phases):
    shapes, sems, (start, forward, finish) = _exchange_parts(srcs, phases)

    def before(first, mid, ins, outs, scratch):
        pl.when(first)(lambda: start(ins, outs, *scratch))
        pl.when(mid)(lambda: forward(ins, outs, *scratch))

    def after(last, ins, outs, scratch):
        pl.when(last)(lambda: finish(ins, outs, *scratch))

    return list(srcs), shapes, sems, before, after


def _call_2d(kern, *, name, grid, in_specs, out_specs, out_shape, ins, scratch_shapes=(), semantics, side=None):
    if side is None:
        return _pcall(kern, name=name, grid=grid, in_specs=in_specs, out_specs=out_specs, out_shape=out_shape,
                      scratch_shapes=list(scratch_shapes), compiler_params=_params(*semantics))(*ins)
    s_ins, s_shapes, s_scratch, before, after = side
    n_in, n_out, n_scr = len(ins), len(out_shape), len(scratch_shapes)

    def combined(*refs):
        refs = list(refs)
        cut = [n_in, len(s_ins), n_out, len(s_shapes), n_scr, len(s_scratch)]
        parts, pos = [], 0
        for c in cut:
            parts.append(refs[pos:pos + c])
            pos += c
        m_in, c_in, m_out, c_out, m_scr, c_scr = parts
        ids = [pl.program_id(a) for a in range(len(grid))]
        inner_zero = functools.reduce(jnp.logical_and, [i == 0 for i in ids[1:]], True)
        first = jnp.logical_and(ids[0] == 0, inner_zero)
        mid = jnp.logical_and(ids[0] == grid[0] // 2, inner_zero)
        last = functools.reduce(jnp.logical_and, [i == n - 1 for i, n in zip(ids, grid)])
        before(first, mid, c_in, c_out, c_scr)
        kern(*m_in, *m_out, *m_scr)
        after(last, c_in, c_out, c_scr)

    anyspace = pl.BlockSpec(memory_space=pl.ANY)
    outs = _pcall(combined, name=name, grid=grid, in_specs=list(in_specs) + [anyspace] * len(s_ins),
                  out_specs=list(out_specs) + [anyspace] * len(s_shapes), out_shape=list(out_shape) + s_shapes,
                  scratch_shapes=list(scratch_shapes) + s_scratch, compiler_params=_params(*["arbitrary"] * len(grid)))(*ins, *s_ins)
    return outs[:n_out], outs[n_out:]


def _two_level_phases():
    def parts(src_ref, out_ref, send_sems, recv_sems, local_sem):
        x, y, c = lax.axis_index("x"), lax.axis_index("y"), lax.axis_index("c")
        me, sibling = (x, y, c), (x, y, 1 - c)
        chips = [(1 - x, y), (x, 1 - y), (1 - x, 1 - y)]

        def slab(px, py, pc):
            return out_ref.at[4 * px + 2 * py + pc]

        def copy(k, block, to, from_src=False):
            return pltpu.make_async_remote_copy(
                src_ref=src_ref if from_src else slab(*block), dst_ref=slab(*block),
                send_sem=send_sems.at[k], recv_sem=recv_sems.at[k], device_id=to, device_id_type=pl.DeviceIdType.MESH)

        return dict(
            mine=lambda: pltpu.make_async_copy(src_ref, slab(*me), local_sem),
            first=lambda: [copy(0, me, sibling, True)] + [copy(1 + j, me, (*chip, c), True) for j, chip in enumerate(chips)],
            passed=lambda: [copy(4 + j, (*chip, c), sibling) for j, chip in enumerate(chips)],
            landed=lambda: [copy(1 + j, (*chip, c), me) for j, chip in enumerate(chips)],
            late=lambda: [copy(0, sibling, me)] + [copy(4 + j, (*chip, 1 - c), me) for j, chip in enumerate(chips)])

    def start(*refs):
        make = parts(*refs)
        make['mine']().start()
        for cp in make['first']():
            cp.start()

    def forward(*refs):
        make = parts(*refs)
        for arrived, onward in zip(make['landed'](), make['passed']()):
            arrived.wait_recv()
            onward.start()

    def finish(*refs):
        make = parts(*refs)
        for cp in make['late']():
            cp.wait_recv()
        for cp in make['first']() + make['passed']():
            cp.wait_send()
        make['mine']().wait()

    return start, forward, finish


def _adamw(recv, w, m, v, *, name):
    rows, cols = w.shape
    tr = _pick(rows, (256, 128, 64))

    def kern(r_ref, w_ref, m_ref, v_ref, g_ref, d_ref, mo_ref, vo_ref):
        g = r_ref[0].astype(F32)
        for p in range(1, N_DEV):
            g = g + r_ref[p].astype(F32)
        mn = ADAM_B1 * m_ref[...] + (1.0 - ADAM_B1) * g
        vn = ADAM_B2 * v_ref[...] + (1.0 - ADAM_B2) * (g * g)
        m_hat = mn / (1.0 - ADAM_B1 ** ADAM_STEP)
        v_hat = vn / (1.0 - ADAM_B2 ** ADAM_STEP)
        g_ref[...] = g
        d_ref[...] = -ADAM_LR * (m_hat / (jnp.sqrt(v_hat) + ADAM_EPS) + ADAM_WD * w_ref[...])
        mo_ref[...] = mn
        vo_ref[...] = vn

    row = pl.BlockSpec((tr, cols), lambda i: (i, 0))
    shp = jax.ShapeDtypeStruct((rows, cols), F32)
    return _pcall(kern, name=name, grid=(rows // tr,), in_specs=[pl.BlockSpec((N_DEV, tr, cols), lambda i: (0, i, 0)), row, row, row],
                  out_specs=[row, row, row, row], out_shape=[shp, shp, shp, shp], compiler_params=_params("parallel"))(recv, w, m, v)


INPUTS = ['x', 'mem'] + WEIGHTS + ['loss_target'] + ['m_' + n for n in WEIGHTS] + ['v_' + n for n in WEIGHTS]
SMALL_PAD_ROWS = 4


def kernel(x, mem, ffn1_norm, ffn1_w1, ffn1_w3, ffn1_w2, mix_norm, mem_norm, w_in, w_mem_kv, qn_dsa, kn_dsa, qn_mem, kn_mem, w_branch_sb, w_branch_dsa, w_branch_mem, w_gate, b_gate, w_out, ffn2_norm, ffn2_w1, ffn2_w3, ffn2_w2, loss_target, m_ffn1_norm, m_ffn1_w1, m_ffn1_w3, m_ffn1_w2, m_mix_norm, m_mem_norm, m_w_in, m_w_mem_kv, m_qn_dsa, m_kn_dsa, m_qn_mem, m_kn_mem, m_w_branch_sb, m_w_branch_dsa, m_w_branch_mem, m_w_gate, m_b_gate, m_w_out, m_ffn2_norm, m_ffn2_w1, m_ffn2_w3, m_ffn2_w2, v_ffn1_norm, v_ffn1_w1, v_ffn1_w3, v_ffn1_w2, v_mix_norm, v_mem_norm, v_w_in, v_w_mem_kv, v_qn_dsa, v_kn_dsa, v_qn_mem, v_kn_mem, v_w_branch_sb, v_w_branch_dsa, v_w_branch_mem, v_w_gate, v_b_gate, v_w_out, v_ffn2_norm, v_ffn2_w1, v_ffn2_w3, v_ffn2_w2):
    given = dict(zip(INPUTS, (x, mem, ffn1_norm, ffn1_w1, ffn1_w3, ffn1_w2, mix_norm, mem_norm, w_in, w_mem_kv, qn_dsa, kn_dsa, qn_mem, kn_mem, w_branch_sb, w_branch_dsa, w_branch_mem, w_gate, b_gate, w_out, ffn2_norm, ffn2_w1, ffn2_w3, ffn2_w2, loss_target, m_ffn1_norm, m_ffn1_w1, m_ffn1_w3, m_ffn1_w2, m_mix_norm, m_mem_norm, m_w_in, m_w_mem_kv, m_qn_dsa, m_kn_dsa, m_qn_mem, m_kn_mem, m_w_branch_sb, m_w_branch_dsa, m_w_branch_mem, m_w_gate, m_b_gate, m_w_out, m_ffn2_norm, m_ffn2_w1, m_ffn2_w3, m_ffn2_w2, v_ffn1_norm, v_ffn1_w1, v_ffn1_w3, v_ffn1_w2, v_mix_norm, v_mem_norm, v_w_in, v_w_mem_kv, v_qn_dsa, v_kn_dsa, v_qn_mem, v_kn_mem, v_w_branch_sb, v_w_branch_dsa, v_w_branch_mem, v_w_gate, v_b_gate, v_w_out, v_ffn2_norm, v_ffn2_w1, v_ffn2_w3, v_ffn2_w2), strict=True))
    wl = {n: given[n][0] for n in BIG}
    ws = {n: given[n] for n in SMALL}

    loss, dx, recv, g = _local_step(x[0], mem[0], loss_target[0], wl, ws)

    big = [{}, {}, {}, {}]
    for n in G_FFN2 + G_MID + G_FFN1:
        outs = _adamw(recv[n], wl[n], given['m_' + n][0], given['v_' + n][0], name=f"adamw_{n}")
        for kind, t in enumerate(outs):
            big[kind][n] = t

    gs = _pack_small(g, SMALL, SMALL_PAD_ROWS)
    loss_row = gs.shape[0] - SMALL_PAD_ROWS
    gs = gs.at[loss_row, 0].set(loss)
    recv_s = _exchange([gs], _direct_phases(False), name="gather_small")[0]
    small = _adamw(recv_s, _pack_small(ws, SMALL, SMALL_PAD_ROWS), _pack_small({n: given['m_' + n] for n in SMALL}, SMALL, SMALL_PAD_ROWS),
                   _pack_small({n: given['v_' + n] for n in SMALL}, SMALL, SMALL_PAD_ROWS), name="adamw_replicated")
    total_loss = small[0][loss_row, 0]
    small = [_unpack_small(t, ws, SMALL) for t in small]

    outs = [total_loss, dx[None]]
    for kind in range(4):
        outs += [big[kind][n][None] if n in wl else small[kind][n] for n in WEIGHTS]
    return tuple(outs)
```

```python
import functools

import jax
import jax.numpy as jnp
from jax import lax
from jax.experimental import pallas as pl
from jax.experimental.pallas import tpu as pltpu

F32 = jnp.float32
BF16 = jnp.bfloat16
MXU_DT = jnp.bfloat16

N_DEV = 8
HEAD_DIM = 64
SB_HEADS = 8
DSA_GROUPS = ((128, 1), (512, 4), (2048, 16))
DSA_HPG = 4
MEM_HEADS = 4
SB_W = SB_HEADS * HEAD_DIM
DSA_W = DSA_HPG * len(DSA_GROUPS) * HEAD_DIM
DSA_OUT_W = DSA_HPG * HEAD_DIM
MEM_W = MEM_HEADS * HEAD_DIM
ROPE_THETA = 10000.0
NORM_EPS = 1e-6
QB = 128
SCALE = HEAD_DIM ** -0.5
ADAM_LR, ADAM_B1, ADAM_B2, ADAM_EPS, ADAM_WD, ADAM_STEP = 0.001, 0.9, 0.999, 1e-08, 0.01, 10

LANES = 128
VMEM_LIMIT = 48 * 1024 * 1024
SB_DEAD = -110.0 * 1.4426950408889634

WEIGHTS = ['ffn1_norm', 'ffn1_w1', 'ffn1_w3', 'ffn1_w2', 'mix_norm', 'mem_norm', 'w_in', 'w_mem_kv', 'qn_dsa', 'kn_dsa',
           'qn_mem', 'kn_mem', 'w_branch_sb', 'w_branch_dsa', 'w_branch_mem', 'w_gate', 'b_gate', 'w_out', 'ffn2_norm',
           'ffn2_w1', 'ffn2_w3', 'ffn2_w2']
SHARD_AXIS = {'ffn1_norm': None, 'ffn1_w1': 1, 'ffn1_w3': 1, 'ffn1_w2': 0, 'mix_norm': None, 'mem_norm': None, 'w_in': 1,
              'w_mem_kv': 0, 'qn_dsa': None, 'kn_dsa': None, 'qn_mem': None, 'kn_mem': None, 'w_branch_sb': 1,
              'w_branch_dsa': 1, 'w_branch_mem': 1, 'w_gate': 1, 'b_gate': None, 'w_out': 0, 'ffn2_norm': None,
              'ffn2_w1': 1, 'ffn2_w3': 1, 'ffn2_w2': 0}
BIG = [n for n in WEIGHTS if SHARD_AXIS[n] is not None]
SMALL = [n for n in WEIGHTS if SHARD_AXIS[n] is None]


def _pcall(kern, **kw):
    return pl.pallas_call(kern, **kw)


def _params(*sem):
    return pltpu.CompilerParams(dimension_semantics=sem, vmem_limit_bytes=VMEM_LIMIT)


def _dot(a, b, dims):
    return lax.dot_general(a.astype(MXU_DT), b.astype(MXU_DT), (dims, ((), ())), preferred_element_type=F32)


def _nn(a, b):
    return _dot(a, b, ((1,), (0,)))


def _nt(a, b):
    return _dot(a, b, ((1,), (1,)))


def _tn(a, b):
    return _dot(a, b, ((0,), (0,)))


def _pick(n, prefs):
    for p in prefs:
        if n % p == 0:
            return p
    return n


def _matmul(a, b, *, name, ta=False, tb=False, out_dtype=F32, res=None, alpha=1.0, tm=1024, tn=512, tk=1024, pair2=None,
            epilogue=None, side=None):
    if ta:
        kdim, m = a.shape
    else:
        m, kdim = a.shape
    n = b.shape[0] if tb else b.shape[1]
    tm = _pick(m, (tm, 512, 256, 128))
    tn = _pick(n, (tn, 512, 384, 256, 128))
    tk = _pick(kdim, (tk, 1024, 512, 256, 128))
    nk = kdim // tk
    a_spec = pl.BlockSpec((tk, tm), lambda i, j, k: (k, i)) if ta else pl.BlockSpec((tm, tk), lambda i, j, k: (i, k))
    b_spec = pl.BlockSpec((tn, tk), lambda i, j, k: (j, k)) if tb else pl.BlockSpec((tk, tn), lambda i, j, k: (k, j))
    o_spec = pl.BlockSpec((tm, tn), lambda i, j, k: (i, j))
    v_spec = pl.BlockSpec((1, tn), lambda i, j, k: (0, j))
    dims = ((0 if ta else 1,), (1 if tb else 0,))
    n_mm = 2 if pair2 is None else 4
    if epilogue is None:
        row_ins, vec_ins = ([] if res is None else [res]), []
        out_dtypes, n_vec = [out_dtype], 0
    else:
        assert tn == n and res is None
        epi_fn, row_ins, vec_ins, out_dtypes, n_vec = epilogue
    n_row_out = len(out_dtypes)

    def kern(*refs):
        refs = list(refs)
        acc_ref = refs.pop() if nk > 1 else None
        mm = refs[:n_mm]
        extra = refs[n_mm:n_mm + len(row_ins) + len(vec_ins)]
        outs = refs[n_mm + len(extra):]
        i = pl.program_id(0)
        k = pl.program_id(2)

        def product():
            part = _dot(mm[0][...], mm[1][...], dims)
            if pair2 is not None:
                part = part + _dot(mm[2][...], mm[3][...], dims)
            return part

        def finish(r):
            if alpha != 1.0:
                r = r * alpha
            if epilogue is None:
                if extra:
                    r = extra[0][...] + r
                outs[0][...] = r.astype(out_dtype)
                return
            vals = epi_fn(r, *[e[...] for e in extra])
            for o_ref, v in zip(outs[:n_row_out], vals[:n_row_out]):
                o_ref[...] = v.astype(o_ref.dtype)
            for o_ref, v in zip(outs[n_row_out:], vals[n_row_out:]):
                @pl.when(i == 0)
                def _():
                    o_ref[...] = jnp.zeros_like(o_ref)

                o_ref[...] += v

        if nk == 1:
            finish(product())
            return

        @pl.when(k == 0)
        def _():
            acc_ref[...] = jnp.zeros_like(acc_ref)

        acc_ref[...] += product()

        @pl.when(k == nk - 1)
        def _():
            finish(acc_ref[...])

    ins = [a, b] + ([] if pair2 is None else list(pair2)) + list(row_ins) + list(vec_ins)
    specs = [a_spec, b_spec] * (n_mm // 2) + [o_spec] * len(row_ins) + [v_spec] * len(vec_ins)
    out_specs = [o_spec] * n_row_out + [v_spec] * n_vec
    out_shape = [jax.ShapeDtypeStruct((m, n), dt) for dt in out_dtypes] + [jax.ShapeDtypeStruct((1, n), F32)] * n_vec
    outs = _call_2d(kern, name=name, grid=(m // tm, n // tn, nk), in_specs=specs, out_specs=out_specs, out_shape=out_shape,
                    ins=ins, scratch_shapes=[pltpu.VMEM((tm, tn), F32)] if nk > 1 else [],
                    semantics=("arbitrary" if n_vec else "parallel", "parallel", "arbitrary"), side=side)
    carried = None
    if side is not None:
        outs, carried = outs
    outs = outs[0] if epilogue is None else outs
    return outs if side is None else (outs, carried)


def _epi_residual_rms(r, res, gain):
    xn = res + r
    return xn, xn * lax.rsqrt(jnp.mean(xn * xn, axis=-1, keepdims=True) + NORM_EPS) * gain


def _epi_rms_bwd(r, x, dres, gain):
    rs = lax.rsqrt(jnp.mean(x * x, axis=-1, keepdims=True) + NORM_EPS)
    xh = x * rs
    dy = r * gain
    dx = dres + rs * (dy - xh * jnp.mean(dy * xh, axis=-1, keepdims=True))
    return dx, dx, jnp.sum(r * xh, axis=0, keepdims=True)


def _epi_rms_bwd_sum(r, r0, x, dres, gain):
    return _epi_rms_bwd(r + r0, x, dres, gain)


def _epi_loss(r, res, target):
    e = (res + r) - target
    dy = e / e.shape[-1]
    return dy, dy, jnp.sum(e * e, axis=0, keepdims=True)
def _rms_fwd(x, g, *, name, side=None):
    s, d = x.shape
    ts = _pick(s, (512, 256))

    def kern(x_ref, g_ref, h_ref):
        xf = x_ref[...]
        r = lax.rsqrt(jnp.mean(xf * xf, axis=-1, keepdims=True) + NORM_EPS)
        h_ref[...] = (xf * r * g_ref[...]).astype(h_ref.dtype)

    outs = _call_2d(kern, name=name, grid=(s // ts,),
                    in_specs=[pl.BlockSpec((ts, d), lambda i: (i, 0)), pl.BlockSpec((1, d), lambda i: (0, 0))],
                    out_specs=[pl.BlockSpec((ts, d), lambda i: (i, 0))], out_shape=[jax.ShapeDtypeStruct((s, d), BF16)],
                    ins=[x, g], semantics=("parallel",), side=side)
    return outs[0] if side is None else (outs[0][0], outs[1])


def _rms_bwd(x, g, dh, res, *, name):
    s, d = x.shape
    ts = _pick(s, (512, 256))

    def kern(*refs):
        if res is None:
            x_ref, g_ref, dh_ref, dx_ref, dxb_ref, dg_ref = refs
            r_ref = None
        else:
            x_ref, g_ref, dh_ref, r_ref, dx_ref, dxb_ref, dg_ref = refs
        xf = x_ref[...]
        r = lax.rsqrt(jnp.mean(xf * xf, axis=-1, keepdims=True) + NORM_EPS)
        xh = xf * r
        dhf = dh_ref[...].astype(F32)
        dy = dhf * g_ref[...]
        dx = r * (dy - xh * jnp.mean(dy * xh, axis=-1, keepdims=True))
        if r_ref is not None:
            dx = r_ref[...] + dx
        dx_ref[...] = dx
        dxb_ref[...] = dx.astype(dxb_ref.dtype)

        @pl.when(pl.program_id(0) == 0)
        def _():
            dg_ref[...] = jnp.zeros_like(dg_ref)

        dg_ref[...] += jnp.sum(dhf * xh, axis=0, keepdims=True)

    row = pl.BlockSpec((ts, d), lambda i: (i, 0))
    vec = pl.BlockSpec((1, d), lambda i: (0, 0))
    ins = [x, g, dh] + ([] if res is None else [res])
    return _pcall(kern, name=name, grid=(s // ts,), in_specs=[row, vec, row] + ([] if res is None else [row]),
                  out_specs=[row, row, vec],
                  out_shape=[jax.ShapeDtypeStruct((s, d), F32), jax.ShapeDtypeStruct((s, d), BF16), jax.ShapeDtypeStruct((1, d), F32)],
                  compiler_params=_params("arbitrary"))(*ins)


def _sigmoid(x):
    return 1.0 / (1.0 + jnp.exp(-x))


FFN_TM, FFN_TF = 512, 1408


def _ffn_up(h, w1, w3, *, name, side=None):
    s, d = h.shape
    fdim = w1.shape[1]
    tm, tf = _pick(s, (FFN_TM, 256)), _pick(fdim, (FFN_TF, 512, 256, 128))

    def kern(h_ref, w1_ref, w3_ref, a_ref, b_ref, f_ref):
        hb = h_ref[...]
        a = _nn(hb, w1_ref[...])
        b = _nn(hb, w3_ref[...])
        a_ref[...] = a.astype(a_ref.dtype)
        b_ref[...] = b.astype(b_ref.dtype)
        f_ref[...] = (a * _sigmoid(a) * b).astype(f_ref.dtype)

    wspec = pl.BlockSpec((d, tf), lambda i, j: (0, j))
    ospec = pl.BlockSpec((tm, tf), lambda i, j: (i, j))
    shp = jax.ShapeDtypeStruct((s, fdim), BF16)
    return _call_2d(kern, name=name, grid=(s // tm, fdim // tf), in_specs=[pl.BlockSpec((tm, d), lambda i, j: (i, 0)), wspec, wspec],
                    out_specs=[ospec, ospec, ospec], out_shape=[shp, shp, shp], ins=[h, w1, w3],
                    semantics=("parallel", "parallel"), side=side)


def _ffn_dact(dy, w2, a, b, *, name, side=None):
    s, d = dy.shape
    fdim = w2.shape[0]
    tm, tf = _pick(s, (FFN_TM, 256)), _pick(fdim, (FFN_TF, 512, 256, 128))

    def kern(dy_ref, w2_ref, a_ref, b_ref, da_ref, db_ref):
        df = _nt(dy_ref[...], w2_ref[...]) * 0.5
        av = a_ref[...].astype(F32)
        sg = _sigmoid(av)
        da_ref[...] = (df * b_ref[...].astype(F32) * (sg + av * sg * (1.0 - sg))).astype(da_ref.dtype)
        db_ref[...] = (df * (av * sg)).astype(db_ref.dtype)

    ospec = pl.BlockSpec((tm, tf), lambda i, j: (i, j))
    shp = jax.ShapeDtypeStruct((s, fdim), BF16)
    return _call_2d(kern, name=name, grid=(s // tm, fdim // tf),
                    in_specs=[pl.BlockSpec((tm, d), lambda i, j: (i, 0)), pl.BlockSpec((tf, d), lambda i, j: (j, 0)), ospec, ospec],
                    out_specs=[ospec, ospec], out_shape=[shp, shp], ins=[dy, w2, a, b], semantics=("parallel", "parallel"), side=side)


def _head_mean(v, bd):
    outs = []
    for c in range(v.shape[1] // LANES):
        x = v[:, c * LANES:(c + 1) * LANES]
        hi = x.astype(BF16)
        lo = (x - hi.astype(F32)).astype(BF16)
        outs.append(lax.dot_general(jnp.concatenate([hi, lo], axis=1), bd, (((1,), (0,)), ((), ())), preferred_element_type=F32))
    return outs[0] if len(outs) == 1 else jnp.concatenate(outs, axis=1)


def _partner(v):
    w = v.shape[1]
    lane = lax.broadcasted_iota(jnp.int32, v.shape, 1)
    return jnp.where(lane % HEAD_DIM < HEAD_DIM // 2, pltpu.roll(v, w - HEAD_DIM // 2, 1), pltpu.roll(v, HEAD_DIM // 2, 1))


def _block_diag(w=None):
    r = (lax.broadcasted_iota(jnp.int32, (2 * LANES, LANES), 0) % LANES) // HEAD_DIM
    c = lax.broadcasted_iota(jnp.int32, (2 * LANES, LANES), 1) // HEAD_DIM
    return jnp.where(r == c, 1.0 / HEAD_DIM, 0.0).astype(BF16)


def _rope_tables(s):
    half = HEAD_DIM // 2
    inv_freq = jnp.power(ROPE_THETA, -jnp.arange(half, dtype=F32) / half)
    ang = jnp.arange(s).astype(F32)[:, None] * inv_freq[None, :]
    cos, sin = lax.optimization_barrier((jnp.cos(ang), jnp.sin(ang)))
    cos2 = jnp.concatenate([cos, cos, cos, cos], axis=1)
    sin2 = jnp.concatenate([-sin, sin, -sin, sin], axis=1)
    return cos2, sin2


def _qknorm_fwd(src, col0, width, gain, rope, *, name, out_dtype=BF16):
    s = src.shape[0]
    ts = _pick(s, (512, 256))
    cb = col0 // width
    assert col0 % width == 0
    reps = width // LANES
    g = jnp.tile(gain, (1, width // HEAD_DIM))

    def kern(*refs):
        if rope is None:
            x_ref, g_ref, o_ref = refs
        else:
            x_ref, g_ref, c_ref, s_ref, o_ref = refs
        x = x_ref[...].astype(F32)
        bd = _block_diag(width)
        r = lax.rsqrt(_head_mean(x * x, bd) + NORM_EPS)
        y = x * r * g_ref[...]
        if rope is not None:
            y = y * jnp.tile(c_ref[...], (1, reps)) + _partner(y) * jnp.tile(s_ref[...], (1, reps))
        o_ref[...] = y.astype(o_ref.dtype)

    xs = pl.BlockSpec((ts, width), lambda i: (i, cb))
    tab = pl.BlockSpec((ts, LANES), lambda i: (i, 0))
    ins = [src, g] + ([] if rope is None else list(rope))
    specs = [xs, pl.BlockSpec((1, width), lambda i: (0, 0))] + ([] if rope is None else [tab, tab])
    return _pcall(kern, name=name, grid=(s // ts,), in_specs=specs, out_specs=pl.BlockSpec((ts, width), lambda i: (i, 0)),
                  out_shape=jax.ShapeDtypeStruct((s, width), out_dtype), compiler_params=_params("parallel"))(*ins)


def _qknorm_bwd(src, col0, width, gain, rope, dout, *, name):
    s = src.shape[0]
    ts = _pick(s, (512, 256))
    cb = col0 // width
    reps = width // LANES
    g = jnp.tile(gain, (1, width // HEAD_DIM))

    douts = list(dout) if isinstance(dout, (list, tuple)) else [dout]
    piece = width // len(douts)

    def kern(*refs):
        refs = list(refs)
        dg_ref = refs.pop()
        dx_ref = refs.pop()
        do_refs = [refs.pop() for _ in douts][::-1]
        if rope is None:
            x_ref, g_ref = refs
        else:
            x_ref, g_ref, c_ref, s_ref = refs
        x = x_ref[...].astype(F32)
        bd = _block_diag(width)
        r = lax.rsqrt(_head_mean(x * x, bd) + NORM_EPS)
        xh = x * r
        dy = jnp.concatenate([d[...].astype(F32) for d in do_refs], axis=1) if len(do_refs) > 1 else do_refs[0][...].astype(F32)
        if rope is not None:
            dy = dy * jnp.tile(c_ref[...], (1, reps)) + _partner(dy * jnp.tile(s_ref[...], (1, reps)))
        dxh = dy * g_ref[...]
        dx_ref[...] = (r * (dxh - xh * _head_mean(dxh * xh, bd))).astype(dx_ref.dtype)

        @pl.when(pl.program_id(0) == 0)
        def _():
            dg_ref[...] = jnp.zeros_like(dg_ref)

        dg_ref[...] += jnp.sum(dy * xh, axis=0, keepdims=True)

    xs = pl.BlockSpec((ts, width), lambda i: (i, cb))
    row = pl.BlockSpec((ts, width), lambda i: (i, 0))
    vec = pl.BlockSpec((1, width), lambda i: (0, 0))
    tab = pl.BlockSpec((ts, LANES), lambda i: (i, 0))
    ins = [src, g] + ([] if rope is None else list(rope)) + douts
    specs = [xs, vec] + ([] if rope is None else [tab, tab]) + [pl.BlockSpec((ts, piece), lambda i: (i, 0))] * len(douts)
    dx, dg = _pcall(kern, name=name, grid=(s // ts,), in_specs=specs, out_specs=[row, vec],
                    out_shape=[jax.ShapeDtypeStruct((s, width), BF16), jax.ShapeDtypeStruct((1, width), F32)],
                    compiler_params=_params("arbitrary"))(*ins)
    return dx, jnp.sum(dg.reshape(width // HEAD_DIM, HEAD_DIM), axis=0, keepdims=True)


def _tri(strict, n):
    r = lax.broadcasted_iota(jnp.int32, (2 * n, n), 0) % n
    c = lax.broadcasted_iota(jnp.int32, (2 * n, n), 1)
    return jnp.where((r > c) if strict else (r >= c), 1.0, 0.0).astype(BF16)


def _split_dot(v, t2):
    hi = v.astype(BF16)
    lo = (v - hi.astype(F32)).astype(BF16)
    return lax.dot_general(jnp.concatenate([hi, lo], axis=1), t2, (((1,), (0,)), ((), ())), preferred_element_type=F32)


LOG2E = 1.4426950408889634


def _log2_sigmoids(z2):
    lf = -(jnp.maximum(z2, 0.0) + jnp.log2(1.0 + jnp.exp2(-jnp.abs(z2))))
    return z2 + lf, lf


SB2_SUB = 2
SB_KT = 128


def _first_half(shape):
    return lax.broadcasted_iota(jnp.int32, shape, 1) < HEAD_DIM


def _split_pair(t, first):
    zero = jnp.zeros_like(t)
    return [jnp.where(first, t, zero), jnp.where(first, zero, t)]


def _sb2_fwd(p, *, name, side=None):
    s = p.shape[0]
    rq = SB2_SUB * QB
    nq = s // rq
    npair = SB_W // LANES

    def kern(q_ref, k_ref, v_ref, o_ref):
        i = pl.program_id(1)
        first = _first_half((rq, LANES))
        q2 = jnp.concatenate(_split_pair(q_ref[...], first), axis=0)
        t2 = _tri(True, SB_KT)
        rel = lax.broadcasted_iota(jnp.int32, (2 * rq, SB_KT), 1) - lax.broadcasted_iota(jnp.int32, (2 * rq, SB_KT), 0) % rq

        def tile(j, q, rel, carry, acc, masked):
            off = pl.multiple_of(j * SB_KT, SB_KT)
            ls, lf = _log2_sigmoids(_nt(q, k_ref[pl.ds(off, SB_KT), :]) * (SCALE * LOG2E))
            if masked:
                before = rel < i * rq - j * SB_KT
                lf = jnp.where(before, lf, 0.0)
            w = jnp.exp2(ls + _split_dot(lf, t2) + carry)
            if masked:
                w = jnp.where(before, w, 0.0)
            return carry + jnp.sum(lf, axis=1, keepdims=True), acc + _nn(w, v_ref[pl.ds(off, SB_KT), :])

        carry, acc = jnp.zeros((2 * rq, 1), F32), jnp.zeros((2 * rq, LANES), F32)
        for a in range(rq // SB_KT):
            carry, acc = tile(i * (rq // SB_KT) + (rq // SB_KT - 1 - a), q2, rel, carry, acc, True)

        def cond(st):
            return jnp.logical_and(st[0] >= 0, st[1] > 0)

        def body(st):
            carry, acc = tile(st[0], q2, rel, st[2], st[3], False)
            return st[0] - 1, (jnp.max(carry) > SB_DEAD).astype(jnp.int32), carry, acc

        st = lax.while_loop(cond, body, (i * (rq // SB_KT) - 1, jnp.int32(1), carry, acc))
        o_ref[...] = jnp.where(first, st[3][:rq], st[3][rq:])

    outs = _call_2d(kern, name=name, grid=(npair, nq),
                    in_specs=[pl.BlockSpec((rq, LANES), lambda a, i: (i, a)), pl.BlockSpec((s, LANES), lambda a, i: (0, npair + a)),
                              pl.BlockSpec((s, LANES), lambda a, i: (0, 2 * npair + a))],
                    out_specs=[pl.BlockSpec((rq, LANES), lambda a, i: (i, a))], out_shape=[jax.ShapeDtypeStruct((s, SB_W), F32)],
                    ins=[p, p, p], semantics=("parallel", "arbitrary"), side=side)
    return outs[0] if side is None else (outs[0][0], outs[1])


def _sb2_bwd(p, o, do, *, name, side=None):
    s = p.shape[0]
    rq = SB2_SUB * QB
    nq = s // rq
    npair = SB_W // LANES

    def kern(q_ref, k_ref, v_ref, o_ref, do_ref, dq_ref, dk_hbm, dv_hbm, dk_acc, dv_acc, sem):
        pr = pl.program_id(0)
        i = pl.program_id(1)

        @pl.when(i == 0)
        def _():
            dk_acc[...] = jnp.zeros_like(dk_acc)
            dv_acc[...] = jnp.zeros_like(dv_acc)

        first = _first_half((rq, LANES))
        q2 = jnp.concatenate(_split_pair(q_ref[...], first), axis=0)
        do2 = jnp.concatenate(_split_pair(do_ref[...], first), axis=0)
        o2 = o_ref[...]
        dsum = jnp.sum(do2.astype(F32) * jnp.concatenate([o2, o2], axis=0), axis=1, keepdims=True)
        t_strict = _tri(True, SB_KT)
        t_incl = _tri(False, SB_KT)
        rel = lax.broadcasted_iota(jnp.int32, (2 * rq, SB_KT), 1) - lax.broadcasted_iota(jnp.int32, (2 * rq, SB_KT), 0) % rq

        def tile(j, rows, carry, gcarry, dq, masked):
            q, dob, dsm, rel = rows
            off = pl.multiple_of(j * SB_KT, SB_KT)
            kt = k_ref[pl.ds(off, SB_KT), :]
            ls, lf = _log2_sigmoids(_nt(q, kt) * (SCALE * LOG2E))
            if masked:
                before = rel < i * rq - j * SB_KT
                lf = jnp.where(before, lf, 0.0)
            w = jnp.exp2(ls + _split_dot(lf, t_strict) + carry)
            if masked:
                w = jnp.where(before, w, 0.0)
            wr = w.astype(MXU_DT)
            g = _nt(dob, v_ref[pl.ds(off, SB_KT), :]) * wr.astype(F32)
            big_g = dsm - (_split_dot(g, t_incl) + gcarry)
            sig = jnp.exp2(ls)
            dz = g * (1.0 - sig) - sig * big_g
            if masked:
                dz = jnp.where(before, dz, 0.0)
            dz = dz * SCALE
            dk_acc[pl.ds(off, SB_KT), :] += _tn(dz, q)
            dv_acc[pl.ds(off, SB_KT), :] += _tn(wr, dob)
            return (carry + jnp.sum(lf, axis=1, keepdims=True), gcarry + jnp.sum(g, axis=1, keepdims=True),
                    dq + _nn(dz, kt))

        zc = jnp.zeros((2 * rq, 1), F32)
        carry, gcarry, dq = zc, zc, jnp.zeros((2 * rq, LANES), F32)
        whole = (q2, do2, dsum, rel)
        for a in range(rq // SB_KT):
            carry, gcarry, dq = tile(i * (rq // SB_KT) + (rq // SB_KT - 1 - a), whole, carry, gcarry, dq, True)

        def cond(st):
            return jnp.logical_and(st[0] >= 0, st[1] > 0)

        def body(st):
            carry, gcarry, dq = tile(st[0], whole, st[2], st[3], st[4], False)
            return st[0] - 1, (jnp.max(carry) > SB_DEAD).astype(jnp.int32), carry, gcarry, dq

        st = lax.while_loop(cond, body, (i * (rq // SB_KT) - 1, jnp.int32(1), carry, gcarry, dq))
        dq_ref[...] = jnp.where(first, st[4][:rq], st[4][rq:]).astype(dq_ref.dtype)

        @pl.when(i == nq - 1)
        def _():
            cols = pl.ds(pl.multiple_of(pr * LANES, LANES), LANES)
            ck = pltpu.make_async_copy(dk_acc, dk_hbm.at[:, cols], sem.at[0])
            cv = pltpu.make_async_copy(dv_acc, dv_hbm.at[:, cols], sem.at[1])
            ck.start()
            cv.start()
            ck.wait()
            cv.wait()

    blk = pl.BlockSpec((rq, LANES), lambda a, i: (i, a))
    anyspace = pl.BlockSpec(memory_space=pl.ANY)
    shp = jax.ShapeDtypeStruct((s, SB_W), F32)
    return _call_2d(kern, name=name, grid=(npair, nq),
                    in_specs=[blk, pl.BlockSpec((s, LANES), lambda a, i: (0, npair + a)),
                              pl.BlockSpec((s, LANES), lambda a, i: (0, 2 * npair + a)), blk, blk],
                    out_specs=[blk, anyspace, anyspace], out_shape=[jax.ShapeDtypeStruct((s, SB_W), BF16), shp, shp], ins=[p, p, p, o, do],
                    scratch_shapes=[pltpu.VMEM((s, LANES), F32), pltpu.VMEM((s, LANES), F32), pltpu.SemaphoreType.DMA((2,))],
                    semantics=("arbitrary", "arbitrary"), side=side)


def _dsa_rel():
    qi = lax.broadcasted_iota(jnp.int32, (QB, QB), 0)
    kj = lax.broadcasted_iota(jnp.int32, (QB, QB), 1)
    return kj - qi


def _prev_mask(rel, has_prev):
    return rel >= jnp.where(has_prev, 0, QB)


DSA_BT = QB * max(r for _, r in DSA_GROUPS)
def _units_per_batch(r):
    return 8 if r < max(d for _, d in DSA_GROUPS) else 4


def _bdot(a, b, ca, cb):
    return lax.dot_general(a.astype(MXU_DT), b.astype(MXU_DT), (((ca,), (cb,)), ((0,), (0,))), preferred_element_type=F32)


def _bnt(a, b):
    return _bdot(a, b, 2, 2)


def _bnn(a, b):
    return _bdot(a, b, 2, 1)


def _btn(a, b):
    return _bdot(a, b, 1, 1)


def _unit_rows(r, c, b):
    return pl.ds(c + QB * r * b, QB, stride=r)


def _pair_cols(t, first):
    return [jnp.max(jnp.where(first, t, -jnp.inf), axis=1, keepdims=True),
            jnp.max(jnp.where(first, -jnp.inf, t), axis=1, keepdims=True)]


def _dsa2_fwd(qn, kn, v32, g, *, name):
    s = qn.shape[0]
    r = DSA_GROUPS[g][1]
    nbk = DSA_BT // (QB * r)
    npair = DSA_OUT_W // LANES

    def kern(q_ref, k_ref, kp_ref, v_ref, vp_ref, o_ref, l_ref):
        t = pl.program_id(1)
        first = _first_half((QB, LANES))
        rel = _dsa_rel()
        units = [(c, b) for c in range(r) for b in range(nbk)]
        ub = _units_per_batch(r)
        for u0 in range(0, len(units), ub):
            batch = units[u0:u0 + ub]
            qs, kcs, vcs, kps, vps, masks = [], [], [], [], [], []
            for c, b in batch:
                rows = _unit_rows(r, c, b)
                kc, vc = k_ref[rows, :].astype(MXU_DT), v_ref[rows, :].astype(MXU_DT)
                if b > 0:
                    prow = _unit_rows(r, c, b - 1)
                    kpv, vpv, has_prev = k_ref[prow, :], v_ref[prow, :], True
                else:
                    prow = _unit_rows(r, c, nbk - 1)
                    kpv, vpv, has_prev = kp_ref[prow, :], vp_ref[prow, :], t > 0
                for qe in _split_pair(q_ref[rows, :], first):
                    qs.append(qe.astype(MXU_DT))
                    kcs.append(kc)
                    vcs.append(vc)
                    kps.append(kpv.astype(MXU_DT))
                    vps.append(vpv.astype(MXU_DT))
                    masks.append(_prev_mask(rel, has_prev))
            qq = jnp.stack(qs)
            sc = jnp.where(rel <= 0, _bnt(qq, jnp.stack(kcs)) * SCALE, -jnp.inf)
            sp = _bnt(qq, jnp.stack(kps)) * SCALE
            sp = jnp.stack([jnp.where(mk, sp[n], -jnp.inf) for n, mk in enumerate(masks)])
            m = jnp.maximum(jnp.max(sc, axis=2, keepdims=True), jnp.max(sp, axis=2, keepdims=True))
            pc = jnp.exp(sc - m)
            pp = jnp.exp(sp - m)
            den = jnp.sum(pc, axis=2, keepdims=True) + jnp.sum(pp, axis=2, keepdims=True)
            out = (_bnn(pc, jnp.stack(vcs)) + _bnn(pp, jnp.stack(vps))) / den
            lse = m + jnp.log(den)
            for idx, (c, b) in enumerate(batch):
                rows = _unit_rows(r, c, b)
                o_ref[rows, :] = jnp.where(first, out[2 * idx], out[2 * idx + 1])
                l_ref[rows, :] = jnp.where(first, lse[2 * idx], lse[2 * idx + 1])

    npg = DSA_HPG * HEAD_DIM // LANES
    cur = pl.BlockSpec((DSA_BT, LANES), lambda a, t: (t, npg * g + a))
    prev = pl.BlockSpec((DSA_BT, LANES), lambda a, t: (jnp.maximum(t - 1, 0), npg * g + a))
    out = pl.BlockSpec((DSA_BT, LANES), lambda a, t: (t, a))
    shp = jax.ShapeDtypeStruct((s, DSA_OUT_W), F32)
    return _pcall(kern, name=name, grid=(npair, s // DSA_BT), in_specs=[cur, cur, prev, cur, prev], out_specs=[out, out],
                  out_shape=[shp, shp], compiler_params=_params("parallel", "parallel"))(qn, kn, kn, v32, v32)


def _dsa2_combine(parts, *, name):
    s, wd = parts[0][0].shape
    ts = _pick(s, (512, 256))

    def kern(o0, l0, o1, l1, o2, l2, o_ref, l_ref):
        ls = [l0[...], l1[...], l2[...]]
        m = jnp.maximum(jnp.maximum(ls[0], ls[1]), ls[2])
        es = [jnp.exp(l - m) for l in ls]
        den = es[0] + es[1] + es[2]
        o_ref[...] = (es[0] * o0[...] + es[1] * o1[...] + es[2] * o2[...]) / den
        l_ref[...] = m + jnp.log(den)

    blk = pl.BlockSpec((ts, wd), lambda i: (i, 0))
    shp = jax.ShapeDtypeStruct((s, wd), F32)
    flat = [t for pair in parts for t in pair]
    return _pcall(kern, name=name, grid=(s // ts,), in_specs=[blk] * 6, out_specs=[blk, blk], out_shape=[shp, shp],
                  compiler_params=_params("parallel"))(*flat)


def _dsa2_prep(o, do, *, name):
    s, wd = o.shape
    ts = _pick(s, (512, 256))

    def kern(o_ref, do_ref, d_ref):
        d_ref[...] = _head_mean(do_ref[...] * o_ref[...], _block_diag(wd)) * HEAD_DIM

    blk = pl.BlockSpec((ts, wd), lambda i: (i, 0))
    return _pcall(kern, name=name, grid=(s // ts,), in_specs=[blk, blk], out_specs=blk,
                  out_shape=jax.ShapeDtypeStruct((s, wd), F32), compiler_params=_params("parallel"))(o, do)


def _dsa2_bwd(qn, kn, v32, do, lse, dd, g, *, name):
    s = qn.shape[0]
    r = DSA_GROUPS[g][1]
    nbk = DSA_BT // (QB * r)
    npair = DSA_OUT_W // LANES
    nsteps = s // DSA_BT

    def kern(q_ref, qn_ref, k_ref, kp_ref, v_ref, vp_ref, do_ref, don_ref, l_ref, ln_ref, d_ref, dn_ref,
             dq_ref, dk_ref, dv_ref):
        t = pl.program_id(1)
        first = _first_half((QB, LANES))
        rel = _dsa_rel()

        def pairs(items):
            qq = jnp.stack([it[0].astype(MXU_DT) for it in items])
            dd = jnp.stack([it[1].astype(MXU_DT) for it in items])
            kk = jnp.stack([it[4].astype(MXU_DT) for it in items])
            vv = jnp.stack([it[5].astype(MXU_DT) for it in items])
            p = jnp.exp(_bnt(qq, kk) * SCALE - jnp.stack([it[2] for it in items]))
            p = jnp.stack([jnp.where(it[6], p[n], 0.0) for n, it in enumerate(items)])
            ds = p * (_bnt(dd, vv) - jnp.stack([it[3] for it in items])) * SCALE
            return _bnn(ds, kk), _btn(ds, qq), _btn(p, dd)

        def heads(rows, qr, dor, lr, dr):
            return list(zip(_split_pair(qr[rows, :], first), _split_pair(dor[rows, :], first),
                            _pair_cols(lr[rows, :], first), _pair_cols(dr[rows, :], first)))

        units = [(c, b) for c in range(r) for b in range(nbk)]
        dk_of, dv_of = [None] * len(units), [None] * len(units)
        ub = _units_per_batch(r)
        for u0 in range(0, len(units), ub // 2):
            batch = list(enumerate(units))[u0:u0 + ub // 2]
            items = []
            for u, (c, b) in batch:
                rows = _unit_rows(r, c, b)
                kc, vc = k_ref[rows, :], v_ref[rows, :]
                if b > 0:
                    prow = _unit_rows(r, c, b - 1)
                    kpv, vpv, pmask = k_ref[prow, :], v_ref[prow, :], _prev_mask(rel, True)
                else:
                    prow = _unit_rows(r, c, nbk - 1)
                    kpv, vpv, pmask = kp_ref[prow, :], vp_ref[prow, :], _prev_mask(rel, t > 0)
                for hd in heads(rows, q_ref, do_ref, l_ref, d_ref):
                    items.append(hd + (kc, vc, rel <= 0))
                    items.append(hd + (kpv, vpv, pmask))
            dq, dk, dv = pairs(items)
            for n, (u, (c, b)) in enumerate(batch):
                dq_ref[_unit_rows(r, c, b), :] = jnp.where(first, dq[4 * n] + dq[4 * n + 1], dq[4 * n + 2] + dq[4 * n + 3])
                dk_of[u] = dk[4 * n] + dk[4 * n + 2]
                dv_of[u] = dv[4 * n] + dv[4 * n + 2]
                if b > 0:
                    dk_of[u - 1] = dk_of[u - 1] + (dk[4 * n + 1] + dk[4 * n + 3])
                    dv_of[u - 1] = dv_of[u - 1] + (dv[4 * n + 1] + dv[4 * n + 3])
        lasts = [c * nbk + nbk - 1 for c in range(r)]
        for c0 in range(0, r, 4):
            chunk = list(range(c0, min(c0 + 4, r)))
            items = []
            for c in chunk:
                last = _unit_rows(r, c, nbk - 1)
                for hd in heads(_unit_rows(r, c, 0), qn_ref, don_ref, ln_ref, dn_ref):
                    items.append(hd + (k_ref[last, :], v_ref[last, :], _prev_mask(rel, t < nsteps - 1)))
            _, dk, dv = pairs(items)
            for n, c in enumerate(chunk):
                dk_of[lasts[c]] = dk_of[lasts[c]] + (dk[2 * n] + dk[2 * n + 1])
                dv_of[lasts[c]] = dv_of[lasts[c]] + (dv[2 * n] + dv[2 * n + 1])
        for u, (c, b) in enumerate(units):
            dk_ref[_unit_rows(r, c, b), :] = dk_of[u]
            dv_ref[_unit_rows(r, c, b), :] = dv_of[u]

    npg = DSA_HPG * HEAD_DIM // LANES

    def at(shift, col):
        return pl.BlockSpec((DSA_BT, LANES), lambda a, t: (jnp.clip(t + shift, 0, nsteps - 1), col(a)))

    gcol = lambda a: npg * g + a
    ocol = lambda a: a
    specs = [at(0, gcol), at(1, gcol), at(0, gcol), at(-1, gcol), at(0, gcol), at(-1, gcol),
             at(0, ocol), at(1, ocol), at(0, ocol), at(1, ocol), at(0, ocol), at(1, ocol)]
    shp = jax.ShapeDtypeStruct((s, DSA_OUT_W), F32)
    return _pcall(kern, name=name, grid=(npair, nsteps), in_specs=specs, out_specs=[at(0, ocol)] * 3, out_shape=[shp, shp, shp],
                  compiler_params=_params("parallel", "parallel"))(qn, qn, kn, kn, v32, v32, do, do, lse, lse, dd, dd)


def _mem2_fwd(qn, km, kv, *, name):
    s = qn.shape[0]
    ml = km.shape[0]
    tq = _pick(s, (1024, 512, 256))
    npair = MEM_W // LANES

    def kern(q_ref, k_ref, v_ref, o_ref):
        first = _first_half((tq, LANES))
        q2 = jnp.concatenate(_split_pair(q_ref[...], first), axis=0)
        sc = _nt(q2, k_ref[...]) * SCALE
        e = jnp.exp(sc - jnp.max(sc, axis=1, keepdims=True))
        o2 = _nn(e / jnp.sum(e, axis=1, keepdims=True), v_ref[...])
        o_ref[...] = jnp.where(first, o2[:tq], o2[tq:])

    blk = pl.BlockSpec((tq, LANES), lambda a, i: (i, a))
    return _pcall(kern, name=name, grid=(npair, s // tq),
                  in_specs=[blk, pl.BlockSpec((ml, LANES), lambda a, i: (0, a)), pl.BlockSpec((ml, LANES), lambda a, i: (0, npair + a))],
                  out_specs=blk, out_shape=jax.ShapeDtypeStruct((s, MEM_W), F32),
                  compiler_params=_params("parallel", "parallel"))(qn, km, kv)


def _mem2_bwd(qn, km, kv, do, *, name):
    s = qn.shape[0]
    ml = km.shape[0]
    tq = _pick(s, (1024, 512, 256))
    npair = MEM_W // LANES

    def kern(q_ref, k_ref, v_ref, do_ref, dq_ref, dk_ref, dv_ref):
        @pl.when(pl.program_id(1) == 0)
        def _():
            dk_ref[...] = jnp.zeros_like(dk_ref)
            dv_ref[...] = jnp.zeros_like(dv_ref)

        first = _first_half((tq, LANES))
        q2 = jnp.concatenate(_split_pair(q_ref[...], first), axis=0)
        do2 = jnp.concatenate(_split_pair(do_ref[...], first), axis=0)
        sc = _nt(q2, k_ref[...]) * SCALE
        e = jnp.exp(sc - jnp.max(sc, axis=1, keepdims=True))
        p = e / jnp.sum(e, axis=1, keepdims=True)
        dp = _nt(do2, v_ref[...])
        ds = p * (dp - jnp.sum(p * dp, axis=1, keepdims=True)) * SCALE
        dq2 = _nn(ds, k_ref[...])
        dk_ref[...] += _tn(ds, q2)
        dv_ref[...] += _tn(p, do2)
        dq_ref[...] = jnp.where(first, dq2[:tq], dq2[tq:])

    blk = pl.BlockSpec((tq, LANES), lambda a, i: (i, a))
    kblk = pl.BlockSpec((ml, LANES), lambda a, i: (0, a))
    kshape = jax.ShapeDtypeStruct((ml, MEM_W), F32)
    return _pcall(kern, name=name, grid=(npair, s // tq),
                  in_specs=[blk, kblk, pl.BlockSpec((ml, LANES), lambda a, i: (0, npair + a)), blk],
                  out_specs=[blk, kblk, kblk], out_shape=[jax.ShapeDtypeStruct((s, MEM_W), F32), kshape, kshape],
                  compiler_params=_params("parallel", "arbitrary"))(qn, km, kv, do)


def _merge_fwd(logits, bias, ya, yb, yc, *, name):
    s, d = ya.shape
    ts = _pick(s, (512, 256))

    def kern(l0, l1, l2, b0, b1, b2, a_ref, b_ref, c_ref, o_ref):
        m = 0.0
        for l_ref, bb_ref, y_ref in ((l0, b0, a_ref), (l1, b1, b_ref), (l2, b2, c_ref)):
            m = m + _sigmoid(l_ref[...].astype(F32) + bb_ref[...]) * y_ref[...].astype(F32)
        o_ref[...] = m.astype(o_ref.dtype)

    row = pl.BlockSpec((ts, d), lambda i: (i, 0))
    lg = [pl.BlockSpec((ts, d), functools.partial(lambda i, c: (i, c), c=c)) for c in range(3)]
    bs = [pl.BlockSpec((1, d), functools.partial(lambda i, c: (0, c), c=c)) for c in range(3)]
    return _pcall(kern, name=name, grid=(s // ts,), in_specs=lg + bs + [row, row, row], out_specs=row,
                  out_shape=jax.ShapeDtypeStruct((s, d), BF16),
                  compiler_params=_params("parallel"))(logits, logits, logits, bias, bias, bias, ya, yb, yc)


def _merge_bwd(logits, bias, ya, yb, yc, dm, *, name):
    s, d = ya.shape
    ts = _pick(s, (256,))

    def kern(l0, l1, l2, b0, b1, b2, a_ref, b_ref, c_ref, dm_ref, da_ref, db_ref, dc_ref, dl_ref, dbias_ref):
        dmv = dm_ref[...].astype(F32)

        @pl.when(pl.program_id(0) == 0)
        def _():
            dbias_ref[...] = jnp.zeros_like(dbias_ref)

        for c, (l_ref, bb_ref, y_ref, dy_ref) in enumerate(((l0, b0, a_ref, da_ref), (l1, b1, b_ref, db_ref), (l2, b2, c_ref, dc_ref))):
            g = _sigmoid(l_ref[...].astype(F32) + bb_ref[...])
            dy_ref[...] = (dmv * g).astype(dy_ref.dtype)
            dl = dmv * y_ref[...].astype(F32) * g * (1.0 - g)
            dl_ref[:, c * d:(c + 1) * d] = dl.astype(dl_ref.dtype)
            dbias_ref[:, c * d:(c + 1) * d] += jnp.sum(dl, axis=0, keepdims=True)

    row = pl.BlockSpec((ts, d), lambda i: (i, 0))
    lg = [pl.BlockSpec((ts, d), functools.partial(lambda i, c: (i, c), c=c)) for c in range(3)]
    bs = [pl.BlockSpec((1, d), functools.partial(lambda i, c: (0, c), c=c)) for c in range(3)]
    yshape = jax.ShapeDtypeStruct((s, d), BF16)
    return _pcall(kern, name=name, grid=(s // ts,), in_specs=lg + bs + [row, row, row, row],
                  out_specs=[row, row, row, pl.BlockSpec((ts, 3 * d), lambda i: (i, 0)), pl.BlockSpec((1, 3 * d), lambda i: (0, 0))],
                  out_shape=[yshape] * 3 + [jax.ShapeDtypeStruct((s, 3 * d), BF16), jax.ShapeDtypeStruct((1, 3 * d), F32)],
                  compiler_params=_params("arbitrary"))(logits, logits, logits, bias, bias, bias, ya, yb, yc, dm)


G_FFN1 = ['ffn1_w1', 'ffn1_w3', 'ffn1_w2']
G_FFN2 = ['ffn2_w1', 'ffn2_w3', 'ffn2_w2']
G_MID = [n for n in BIG if n not in G_FFN1 + G_FFN2]


def _ffn_fwd(h, w1, w3, w2, tag, epilogue, side=None):
    carried = None
    if side is None:
        a, b, f = _ffn_up(h, w1, w3, name=f"{tag}_up")
    else:
        (a, b, f), carried = _ffn_up(h, w1, w3, name=f"{tag}_up", side=side)
    if callable(w2):
        w2 = w2(carried)
    outs = _matmul(f, w2, name=f"{tag}_down", alpha=0.5, tm=512, tn=1024, tk=2816, epilogue=epilogue)
    return outs, (h, a, b, f), carried


def _ffn_bwd(x, norm, w1, w3, w2, saved, dy, dyb, tag, side_first=None, side=None, own_side=None):
    h, a, b, f = saved
    dw2 = _matmul(f, dyb, name=f"{tag}_dw2", ta=True, alpha=0.5, tm=1408, tn=1024, tk=2048, side=side_first)
    carried = None
    if side_first is not None:
        dw2, carried = dw2
    if side is None:
        da, db = _ffn_dact(dyb, w2, a, b, name=f"{tag}_dact")
    else:
        (da, db), got = _ffn_dact(dyb, w2, a, b, name=f"{tag}_dact", side=side)
        carried = (carried or []) + got
    dw1 = _matmul(h, da, name=f"{tag}_dw1", ta=True, tm=1024, tn=1408, tk=2048)
    dw3 = _matmul(h, db, name=f"{tag}_dw3", ta=True, tm=1024, tn=1408, tk=2048)
    outs = _matmul(da, w1, name=f"{tag}_dh", tb=True, tm=512, tn=1024, tk=1408, pair2=(db, w3),
                   epilogue=(_epi_rms_bwd, [x, dy], [norm], [F32, BF16], 1),
                   side=None if own_side is None else own_side(dw1, dw3, dw2))
    (dx, dxb, dnorm), own = outs if own_side is not None else (outs, None)
    return dx, dxb, dnorm, dw1, dw3, dw2, carried, own


def _local_step(x, mem, loss_target, wl, ws):
    s, d = x.shape
    assert s % (QB * 16) == 0
    rope = _rope_tables(s)
    bf = {n: wl[n].astype(BF16) for n in BIG}
    w = dict(ws)

    def gather(names):
        return _side([bf[n] for n in names], _two_level_phases())

    def whole(names, gathered):
        return {n: _whole_weight(n, t) for n, t in zip(names, gathered)}

    first_needed = ['ffn1_w1', 'ffn1_w3']
    then_needed = ['ffn1_w2'] + G_MID
    h1, early = _rms_fwd(x, w['ffn1_norm'], name="ffn1_rms", side=gather(first_needed))
    w.update(whole(first_needed, early))
    (x1, h), sv1, late = _ffn_fwd(h1, w['ffn1_w1'], w['ffn1_w3'], lambda got: _whole_weight('ffn1_w2', got[0]), "ffn1",
                                  (_epi_residual_rms, [x], [w['mix_norm']], [F32, BF16], 0),
                                  side=gather(then_needed))
    w.update(whole(then_needed, late))
    p = _matmul(h, w['w_in'], name="in_proj", out_dtype=BF16, tn=1024)
    logits = _matmul(h, w['w_gate'], name="gate_proj", out_dtype=BF16, tn=1024)
    c_qb, c_kb, c_vb, c_qc = 3 * SB_W, 3 * SB_W + DSA_W, 3 * SB_W + 2 * DSA_W, 3 * SB_W + 3 * DSA_W

    oa_t, late = _sb2_fwd(p, name="sb_fwd", side=gather(G_FFN2))
    w.update(whole(G_FFN2, late))
    ya = _matmul(oa_t, w['w_branch_sb'], name="sb_out", out_dtype=BF16)

    qb_n = _qknorm_fwd(p, c_qb, DSA_W, w['qn_dsa'], rope, name="dsa_qnorm", out_dtype=F32)
    kb_n = _qknorm_fwd(p, c_kb, DSA_W, w['kn_dsa'], rope, name="dsa_knorm", out_dtype=F32)
    vb32 = p[:, c_vb:c_vb + DSA_W].astype(F32)
    groups = range(len(DSA_GROUPS))
    ob_t, lse_b = _dsa2_combine([_dsa2_fwd(qb_n, kb_n, vb32, gi, name=f"dsa_fwd{gi}") for gi in groups], name="dsa_combine")
    yb = _matmul(ob_t, w['w_branch_dsa'], name="dsa_out", out_dtype=BF16)

    memh = _rms_fwd(mem, w['mem_norm'], name="mem_rms")
    kv = _matmul(memh, w['w_mem_kv'], name="mem_kv", out_dtype=BF16)
    km_n = _qknorm_fwd(kv, 0, MEM_W, w['kn_mem'], None, name="mem_knorm")
    qc_n = _qknorm_fwd(p, c_qc, MEM_W, w['qn_mem'], None, name="mem_qnorm")
    oc_t = _mem2_fwd(qc_n, km_n, kv, name="mem_fwd")
    yc = _matmul(oc_t, w['w_branch_mem'], name="mem_out", out_dtype=BF16)

    merged = _merge_fwd(logits, w['b_gate'], ya, yb, yc, name="merge")
    x2, h2 = _matmul(merged, w['w_out'], name="out_proj", tn=1024,
                     epilogue=(_epi_residual_rms, [x1], [w['ffn2_norm']], [F32, BF16], 0))
    (dx3, dx3b, sq), sv2, _ = _ffn_fwd(h2, w['ffn2_w1'], w['ffn2_w3'], w['ffn2_w2'], "ffn2",
                                       (_epi_loss, [x2, loss_target], [], [F32, BF16], 1))
    loss = jnp.sum(sq) * (0.5 / d)

    g, recv = {}, {}

    def owners(names):
        return [_for_owners(n, g[n], wl[n].shape) for n in names]

    dx2, dx2b, g['ffn2_norm'], g['ffn2_w1'], g['ffn2_w3'], g['ffn2_w2'], _, _ = _ffn_bwd(
        x2, w['ffn2_norm'], w['ffn2_w1'], w['ffn2_w3'], w['ffn2_w2'], sv2, dx3, dx3b, "ffn2")

    g['w_out'] = _matmul(merged, dx2b, name="d_w_out", ta=True, tn=1024, tk=512)
    dm = _matmul(dx2b, w['w_out'], name="d_merged", tb=True, out_dtype=BF16, tn=1024)
    dya, dyb, dyc, dlogits, g['b_gate'] = _merge_bwd(logits, w['b_gate'], ya, yb, yc, dm, name="d_merge")

    g['w_branch_sb'] = _matmul(oa_t, dya, name="d_w_sb", ta=True, tn=1024, tk=512)
    g['w_branch_dsa'] = _matmul(ob_t, dyb, name="d_w_dsa", ta=True, tk=512)
    g['w_branch_mem'] = _matmul(oc_t, dyc, name="d_w_mem", ta=True, tk=512)
    doa = _matmul(dya, w['w_branch_sb'], name="d_oa", tb=True, out_dtype=BF16)
    dob = _matmul(dyb, w['w_branch_dsa'], name="d_ob", tb=True)
    doc = _matmul(dyc, w['w_branch_mem'], name="d_oc", tb=True, out_dtype=BF16)

    (dqa, dka, dva), got = _sb2_bwd(p, oa_t, doa, name="sb_bwd", side=_side(owners(G_FFN2), _direct_phases(True)))
    recv.update(zip(G_FFN2, got))

    dd_b = _dsa2_prep(ob_t, dob, name="dsa_prep")
    dgrp = [_dsa2_bwd(qb_n, kb_n, vb32, dob, lse_b, dd_b, gi, name=f"dsa_bwd{gi}") for gi in groups]
    dvb = jnp.concatenate([t[2] for t in dgrp], axis=1).astype(BF16)
    dqb, g['qn_dsa'] = _qknorm_bwd(p, c_qb, DSA_W, w['qn_dsa'], rope, [t[0] for t in dgrp], name="d_dsa_qnorm")
    dkb, g['kn_dsa'] = _qknorm_bwd(p, c_kb, DSA_W, w['kn_dsa'], rope, [t[1] for t in dgrp], name="d_dsa_knorm")

    dqc_n, dkm_n, dvm = _mem2_bwd(qc_n, km_n, kv, doc, name="mem_bwd")
    dqc, g['qn_mem'] = _qknorm_bwd(p, c_qc, MEM_W, w['qn_mem'], None, dqc_n, name="d_mem_qnorm")
    dkm, g['kn_mem'] = _qknorm_bwd(kv, 0, MEM_W, w['kn_mem'], None, dkm_n, name="d_mem_knorm")
    dkv = jnp.concatenate([dkm, dvm.astype(BF16)], axis=1)
    g['w_mem_kv'] = _matmul(memh, dkv, name="d_w_mem_kv", ta=True)
    dmemh = _matmul(dkv, w['w_mem_kv'], name="d_memh", tb=True)
    _, _, g['mem_norm'] = _rms_bwd(mem, w['mem_norm'], dmemh, None, name="d_mem_rms")

    dp = jnp.concatenate([dqa.astype(BF16), dka.astype(BF16), dva.astype(BF16),
                          dqb, dkb, dvb, dqc], axis=1)
    g['w_in'] = _matmul(h, dp, name="d_w_in", ta=True, tn=2048, tk=1024)
    g['w_gate'] = _matmul(h, dlogits, name="d_w_gate", ta=True, tn=1536, tk=1024)
    dh = _matmul(dp, w['w_in'], name="d_h_in", tb=True, tn=1024, tk=2048)
    dx1, dx1b, g['mix_norm'] = _matmul(dlogits, w['w_gate'], name="d_h_gate", tb=True, tm=512, tn=1024, tk=3072,
                                       epilogue=(_epi_rms_bwd_sum, [dh, x1, dx2], [w['mix_norm']], [F32, BF16], 1))

    mid_b = ['w_gate', 'w_out']
    mid_a = [n for n in G_MID if n not in mid_b]

    def own_side(dw1, dw3, dw2):
        g.update(ffn1_w1=dw1, ffn1_w3=dw3, ffn1_w2=dw2)
        return _side(owners(G_FFN1), _direct_phases(True))

    dx0, _, g['ffn1_norm'], _, _, _, got_mid, got_own = _ffn_bwd(
        x, w['ffn1_norm'], w['ffn1_w1'], w['ffn1_w3'], w['ffn1_w2'], sv1, dx1, dx1b, "ffn1",
        side_first=_side(owners(mid_b), _direct_phases(True)), side=_side(owners(mid_a), _direct_phases(True)),
        own_side=own_side)
    recv.update(zip(mid_b + mid_a, got_mid))
    recv.update(zip(G_FFN1, got_own))
    return loss, dx0, recv, {n: g[n] for n in SMALL}


def _whole_weight(name, gathered):
    _, r, c = gathered.shape
    return gathered.reshape(N_DEV * r, c) if SHARD_AXIS[name] == 0 else gathered.transpose(1, 0, 2).reshape(r, N_DEV * c)


def _for_owners(name, grad, shard_shape):
    r, c = shard_shape
    blk = grad.reshape(N_DEV, r, c) if SHARD_AXIS[name] == 0 else grad.reshape(r, N_DEV, c).transpose(1, 0, 2)
    return blk.astype(BF16)


def _pack_small(d, names, extra_rows):
    parts = []
    for n in names:
        v = d[n].reshape(-1)
        pad = (-v.size) % LANES
        parts.append(jnp.concatenate([v, jnp.zeros((pad,), v.dtype)]).reshape(-1, LANES))
    t = jnp.concatenate(parts, axis=0)
    return jnp.concatenate([t, jnp.zeros((extra_rows, LANES), t.dtype)], axis=0)


def _unpack_small(t, like, names):
    out, off = {}, 0
    for n in names:
        size = like[n].size
        rows = -(-size // LANES)
        out[n] = t[off:off + rows].reshape(-1)[:size].reshape(like[n].shape)
        off += rows
    return out


def _direct_phases(per_peer):
    def descriptors(src_ref, out_ref, send_sems, recv_sems, local_sem):
        x, y, c = lax.axis_index("x"), lax.axis_index("y"), lax.axis_index("c")
        me = 4 * x + 2 * y + c
        mine = pltpu.make_async_copy(src_ref.at[me] if per_peer else src_ref, out_ref.at[me], local_sem)
        copies = []
        for k in range(1, N_DEV):
            px = 1 - x if k & 4 else x
            py = 1 - y if k & 2 else y
            pc = 1 - c if k & 1 else c
            copies.append(pltpu.make_async_remote_copy(
                src_ref=src_ref.at[4 * px + 2 * py + pc] if per_peer else src_ref, dst_ref=out_ref.at[me],
                send_sem=send_sems.at[k - 1], recv_sem=recv_sems.at[k - 1],
                device_id=(px, py, pc), device_id_type=pl.DeviceIdType.MESH))
        return mine, copies

    def start(*refs):
        mine, copies = descriptors(*refs)
        mine.start()
        for cp in copies:
            cp.start()

    def forward(*refs):
        pass

    def finish(*refs):
        mine, copies = descriptors(*refs)
        for cp in copies:
            cp.wait_recv()
        for cp in copies:
            cp.wait_send()
        mine.wait()

    return start, forward, finish


def _exchange_parts(srcs, phases):
    n = len(srcs)
    shapes = [jax.ShapeDtypeStruct((N_DEV,) + tuple(s.shape[-2:]), s.dtype) for s in srcs]
    sems = [pltpu.SemaphoreType.DMA((n, N_DEV - 1)), pltpu.SemaphoreType.DMA((n, N_DEV - 1)), pltpu.SemaphoreType.DMA((n,))]

    def lift(phase):
        def run(src_refs, out_refs, send, recv, local):
            for a, (s_ref, o_ref) in enumerate(zip(src_refs, out_refs)):
                phase(s_ref, o_ref, send.at[a], recv.at[a], local.at[a])
        return run

    return shapes, sems, [lift(p) for p in phases]


def _exchange(srcs, phases, *, name):
    shapes, sems, runs = _exchange_parts(srcs, phases)
    n = len(srcs)

    def body(*refs):
        for run in runs:
            run(refs[:n], refs[n:2 * n], *refs[2 * n:])

    anyspace = pl.BlockSpec(memory_space=pl.ANY)
    return _pcall(body, name=name, in_specs=[anyspace] * n, out_specs=[anyspace] * n, out_shape=shapes, scratch_shapes=sems)(*srcs)


def _side(srcs, phases):
    shapes, sems, (start, forward, finish) = _exchange_parts(srcs, phases)

    def before(first, mid, ins, outs, scratch):
        pl.when(first)(lambda: start(ins, outs, *scratch))
        pl.when(mid)(lambda: forward(ins, outs, *scratch))

    def after(last, ins, outs, scratch):
        pl.when(last)(lambda: finish(ins, outs, *scratch))

    return list(srcs), shapes, sems, before, after


def _call_2d(kern, *, name, grid, in_specs, out_specs, out_shape, ins, scratch_shapes=(), semantics, side=None):
    if side is None:
        return _pcall(kern, name=name, grid=grid, in_specs=in_specs, out_specs=out_specs, out_shape=out_shape,
                      scratch_shapes=list(scratch_shapes), compiler_params=_params(*semantics))(*ins)
    s_ins, s_shapes, s_scratch, before, after = side
    n_in, n_out, n_scr = len(ins), len(out_shape), len(scratch_shapes)

    def combined(*refs):
        refs = list(refs)
        cut = [n_in, len(s_ins), n_out, len(s_shapes), n_scr, len(s_scratch)]
        parts, pos = [], 0
        for c in cut:
            parts.append(refs[pos:pos + c])
            pos += c
        m_in, c_in, m_out, c_out, m_scr, c_scr = parts
        ids = [pl.program_id(a) for a in range(len(grid))]
        inner_zero = functools.reduce(jnp.logical_and, [i == 0 for i in ids[1:]], True)
        first = jnp.logical_and(ids[0] == 0, inner_zero)
        mid = jnp.logical_and(ids[0] == grid[0] // 2, inner_zero)
        last = functools.reduce(jnp.logical_and, [i == n - 1 for i, n in zip(ids, grid)])
        before(first, mid, c_in, c_out, c_scr)
        kern(*m_in, *m_out, *m_scr)
        after(last, c_in, c_out, c_scr)

    anyspace = pl.BlockSpec(memory_space=pl.ANY)
    outs = _pcall(combined, name=name, grid=grid, in_specs=list(in_specs) + [anyspace] * len(s_ins),
                  out_specs=list(out_specs) + [anyspace] * len(s_shapes), out_shape=list(out_shape) + s_shapes,
                  scratch_shapes=list(scratch_shapes) + s_scratch, compiler_params=_params(*["arbitrary"] * len(grid)))(*ins, *s_ins)
    return outs[:n_out], outs[n_out:]


def _two_level_phases():
    def parts(src_ref, out_ref, send_sems, recv_sems, local_sem):
        x, y, c = lax.axis_index("x"), lax.axis_index("y"), lax.axis_index("c")
        me, sibling = (x, y, c), (x, y, 1 - c)
        chips = [(1 - x, y), (x, 1 - y), (1 - x, 1 - y)]

        def slab(px, py, pc):
            return out_ref.at[4 * px + 2 * py + pc]

        def copy(k, block, to, from_src=False):
            return pltpu.make_async_remote_copy(
                src_ref=src_ref if from_src else slab(*block), dst_ref=slab(*block),
                send_sem=send_sems.at[k], recv_sem=recv_sems.at[k], device_id=to, device_id_type=pl.DeviceIdType.MESH)

        return dict(
            mine=lambda: pltpu.make_async_copy(src_ref, slab(*me), local_sem),
            first=lambda: [copy(0, me, sibling, True)] + [copy(1 + j, me, (*chip, c), True) for j, chip in enumerate(chips)],
            passed=lambda: [copy(4 + j, (*chip, c), sibling) for j, chip in enumerate(chips)],
            landed=lambda: [copy(1 + j, (*chip, c), me) for j, chip in enumerate(chips)],
            late=lambda: [copy(0, sibling, me)] + [copy(4 + j, (*chip, 1 - c), me) for j, chip in enumerate(chips)])

    def start(*refs):
        make = parts(*refs)
        make['mine']().start()
        for cp in make['first']():
            cp.start()

    def forward(*refs):
        make = parts(*refs)
        for arrived, onward in zip(make['landed'](), make['passed']()):
            arrived.wait_recv()
            onward.start()

    def finish(*refs):
        make = parts(*refs)
        for cp in make['late']():
            cp.wait_recv()
        for cp in make['first']() + make['passed']():
            cp.wait_send()
        make['mine']().wait()

    return start, forward, finish


def _adamw(recv, w, m, v, *, name):
    rows, cols = w.shape
    tr = _pick(rows, (256, 128, 64))

    def kern(r_ref, w_ref, m_ref, v_ref, g_ref, d_ref, mo_ref, vo_ref):
        g = r_ref[0].astype(F32)
        for p in range(1, N_DEV):
            g = g + r_ref[p].astype(F32)
        mn = ADAM_B1 * m_ref[...] + (1.0 - ADAM_B1) * g
        vn = ADAM_B2 * v_ref[...] + (1.0 - ADAM_B2) * (g * g)
        m_hat = mn / (1.0 - ADAM_B1 ** ADAM_STEP)
        v_hat = vn / (1.0 - ADAM_B2 ** ADAM_STEP)
        g_ref[...] = g
        d_ref[...] = -ADAM_LR * (m_hat / (jnp.sqrt(v_hat) + ADAM_EPS) + ADAM_WD * w_ref[...])
        mo_ref[...] = mn
        vo_ref[...] = vn

    row = pl.BlockSpec((tr, cols), lambda i: (i, 0))
    shp = jax.ShapeDtypeStruct((rows, cols), F32)
    return _pcall(kern, name=name, grid=(rows // tr,), in_specs=[pl.BlockSpec((N_DEV, tr, cols), lambda i: (0, i, 0)), row, row, row],
                  out_specs=[row, row, row, row], out_shape=[shp, shp, shp, shp], compiler_params=_params("parallel"))(recv, w, m, v)


INPUTS = ['x', 'mem'] + WEIGHTS + ['loss_target'] + ['m_' + n for n in WEIGHTS] + ['v_' + n for n in WEIGHTS]
SMALL_PAD_ROWS = 4


def kernel(x, mem, ffn1_norm, ffn1_w1, ffn1_w3, ffn1_w2, mix_norm, mem_norm, w_in, w_mem_kv, qn_dsa, kn_dsa, qn_mem, kn_mem, w_branch_sb, w_branch_dsa, w_branch_mem, w_gate, b_gate, w_out, ffn2_norm, ffn2_w1, ffn2_w3, ffn2_w2, loss_target, m_ffn1_norm, m_ffn1_w1, m_ffn1_w3, m_ffn1_w2, m_mix_norm, m_mem_norm, m_w_in, m_w_mem_kv, m_qn_dsa, m_kn_dsa, m_qn_mem, m_kn_mem, m_w_branch_sb, m_w_branch_dsa, m_w_branch_mem, m_w_gate, m_b_gate, m_w_out, m_ffn2_norm, m_ffn2_w1, m_ffn2_w3, m_ffn2_w2, v_ffn1_norm, v_ffn1_w1, v_ffn1_w3, v_ffn1_w2, v_mix_norm, v_mem_norm, v_w_in, v_w_mem_kv, v_qn_dsa, v_kn_dsa, v_qn_mem, v_kn_mem, v_w_branch_sb, v_w_branch_dsa, v_w_branch_mem, v_w_gate, v_b_gate, v_w_out, v_ffn2_norm, v_ffn2_w1, v_ffn2_w3, v_ffn2_w2):
    given = dict(zip(INPUTS, (x, mem, ffn1_norm, ffn1_w1, ffn1_w3, ffn1_w2, mix_norm, mem_norm, w_in, w_mem_kv, qn_dsa, kn_dsa, qn_mem, kn_mem, w_branch_sb, w_branch_dsa, w_branch_mem, w_gate, b_gate, w_out, ffn2_norm, ffn2_w1, ffn2_w3, ffn2_w2, loss_target, m_ffn1_norm, m_ffn1_w1, m_ffn1_w3, m_ffn1_w2, m_mix_norm, m_mem_norm, m_w_in, m_w_mem_kv, m_qn_dsa, m_kn_dsa, m_qn_mem, m_kn_mem, m_w_branch_sb, m_w_branch_dsa, m_w_branch_mem, m_w_gate, m_b_gate, m_w_out, m_ffn2_norm, m_ffn2_w1, m_ffn2_w3, m_ffn2_w2, v_ffn1_norm, v_ffn1_w1, v_ffn1_w3, v_ffn1_w2, v_mix_norm, v_mem_norm, v_w_in, v_w_mem_kv, v_qn_dsa, v_kn_dsa, v_qn_mem, v_kn_mem, v_w_branch_sb, v_w_branch_dsa, v_w_branch_mem, v_w_gate, v_b_gate, v_w_out, v_ffn2_norm, v_ffn2_w1, v_ffn2_w3, v_ffn2_w2), strict=True))
    wl = {n: given[n][0] for n in BIG}
    ws = {n: given[n] for n in SMALL}

    loss, dx, recv, g = _local_step(x[0], mem[0], loss_target[0], wl, ws)

    big = [{}, {}, {}, {}]
    for n in G_FFN2 + G_MID + G_FFN1:
        outs = _adamw(recv[n], wl[n], given['m_' + n][0], given['v_' + n][0], name=f"adamw_{n}")
        for kind, t in enumerate(outs):
            big[kind][n] = t

    gs = _pack_small(g, SMALL, SMALL_PAD_ROWS)
    loss_row = gs.shape[0] - SMALL_PAD_ROWS
    gs = gs.at[loss_row, 0].set(loss)
    recv_s = _exchange([gs], _direct_phases(False), name="gather_small")[0]
    small = _adamw(recv_s, _pack_small(ws, SMALL, SMALL_PAD_ROWS), _pack_small({n: given['m_' + n] for n in SMALL}, SMALL, SMALL_PAD_ROWS),
                   _pack_small({n: given['v_' + n] for n in SMALL}, SMALL, SMALL_PAD_ROWS), name="adamw_replicated")
    total_loss = small[0][loss_row, 0]
    small = [_unpack_small(t, ws, SMALL) for t in small]

    outs = [total_loss, dx[None]]
    for kind in range(4):
        outs += [big[kind][n][None] if n in wl else small[kind][n] for n in WEIGHTS]
    return tuple(outs)
```

```python
import functools

import jax
import jax.numpy as jnp
from jax import lax
from jax.experimental import pallas as pl
from jax.experimental.pallas import tpu as pltpu

F32 = jnp.float32
BF16 = jnp.bfloat16
MXU_DT = jnp.bfloat16

N_DEV = 8
HEAD_DIM = 64
SB_HEADS = 8
DSA_GROUPS = ((128, 1), (512, 4), (2048, 16))
DSA_HPG = 4
MEM_HEADS = 4
SB_W = SB_HEADS * HEAD_DIM
DSA_W = DSA_HPG * len(DSA_GROUPS) * HEAD_DIM
DSA_OUT_W = DSA_HPG * HEAD_DIM
MEM_W = MEM_HEADS * HEAD_DIM
ROPE_THETA = 10000.0
NORM_EPS = 1e-6
QB = 128
SCALE = HEAD_DIM ** -0.5
ADAM_LR, ADAM_B1, ADAM_B2, ADAM_EPS, ADAM_WD, ADAM_STEP = 0.001, 0.9, 0.999, 1e-08, 0.01, 10

LANES = 128
VMEM_LIMIT = 48 * 1024 * 1024
SB_DEAD = -110.0 * 1.4426950408889634

WEIGHTS = ['ffn1_norm', 'ffn1_w1', 'ffn1_w3', 'ffn1_w2', 'mix_norm', 'mem_norm', 'w_in', 'w_mem_kv', 'qn_dsa', 'kn_dsa',
           'qn_mem', 'kn_mem', 'w_branch_sb', 'w_branch_dsa', 'w_branch_mem', 'w_gate', 'b_gate', 'w_out', 'ffn2_norm',
           'ffn2_w1', 'ffn2_w3', 'ffn2_w2']
SHARD_AXIS = {'ffn1_norm': None, 'ffn1_w1': 1, 'ffn1_w3': 1, 'ffn1_w2': 0, 'mix_norm': None, 'mem_norm': None, 'w_in': 1,
              'w_mem_kv': 0, 'qn_dsa': None, 'kn_dsa': None, 'qn_mem': None, 'kn_mem': None, 'w_branch_sb': 1,
              'w_branch_dsa': 1, 'w_branch_mem': 1, 'w_gate': 1, 'b_gate': None, 'w_out': 0, 'ffn2_norm': None,
              'ffn2_w1': 1, 'ffn2_w3': 1, 'ffn2_w2': 0}
BIG = [n for n in WEIGHTS if SHARD_AXIS[n] is not None]
SMALL = [n for n in WEIGHTS if SHARD_AXIS[n] is None]


def _pcall(kern, **kw):
    return pl.pallas_call(kern, **kw)


def _params(*sem):
    return pltpu.CompilerParams(dimension_semantics=sem, vmem_limit_bytes=VMEM_LIMIT)


def _dot(a, b, dims):
    return lax.dot_general(a.astype(MXU_DT), b.astype(MXU_DT), (dims, ((), ())), preferred_element_type=F32)


def _nn(a, b):
    return _dot(a, b, ((1,), (0,)))


def _nt(a, b):
    return _dot(a, b, ((1,), (1,)))


def _tn(a, b):
    return _dot(a, b, ((0,), (0,)))


def _pick(n, prefs):
    for p in prefs:
        if n % p == 0:
            return p
    return n


def _matmul(a, b, *, name, ta=False, tb=False, out_dtype=F32, res=None, alpha=1.0, tm=1024, tn=512, tk=1024, pair2=None,
            epilogue=None, side=None):
    if ta:
        kdim, m = a.shape
    else:
        m, kdim = a.shape
    n = b.shape[0] if tb else b.shape[1]
    tm = _pick(m, (tm, 512, 256, 128))
    tn = _pick(n, (tn, 512, 384, 256, 128))
    tk = _pick(kdim, (tk, 1024, 512, 256, 128))
    nk = kdim // tk
    a_spec = pl.BlockSpec((tk, tm), lambda i, j, k: (k, i)) if ta else pl.BlockSpec((tm, tk), lambda i, j, k: (i, k))
    b_spec = pl.BlockSpec((tn, tk), lambda i, j, k: (j, k)) if tb else pl.BlockSpec((tk, tn), lambda i, j, k: (k, j))
    o_spec = pl.BlockSpec((tm, tn), lambda i, j, k: (i, j))
    v_spec = pl.BlockSpec((1, tn), lambda i, j, k: (0, j))
    dims = ((0 if ta else 1,), (1 if tb else 0,))
    n_mm = 2 if pair2 is None else 4
    if epilogue is None:
        row_ins, vec_ins = ([] if res is None else [res]), []
        out_dtypes, n_vec = [out_dtype], 0
    else:
        assert tn == n and res is None
        epi_fn, row_ins, vec_ins, out_dtypes, n_vec = epilogue
    n_row_out = len(out_dtypes)

    def kern(*refs):
        refs = list(refs)
        acc_ref = refs.pop() if nk > 1 else None
        mm = refs[:n_mm]
        extra = refs[n_mm:n_mm + len(row_ins) + len(vec_ins)]
        outs = refs[n_mm + len(extra):]
        i = pl.program_id(0)
        k = pl.program_id(2)

        def product():
            part = _dot(mm[0][...], mm[1][...], dims)
            if pair2 is not None:
                part = part + _dot(mm[2][...], mm[3][...], dims)
            return part

        def finish(r):
            if alpha != 1.0:
                r = r * alpha
            if epilogue is None:
                if extra:
                    r = extra[0][...] + r
                outs[0][...] = r.astype(out_dtype)
                return
            vals = epi_fn(r, *[e[...] for e in extra])
            for o_ref, v in zip(outs[:n_row_out], vals[:n_row_out]):
                o_ref[...] = v.astype(o_ref.dtype)
            for o_ref, v in zip(outs[n_row_out:], vals[n_row_out:]):
                @pl.when(i == 0)
                def _():
                    o_ref[...] = jnp.zeros_like(o_ref)

                o_ref[...] += v

        if nk == 1:
            finish(product())
            return

        @pl.when(k == 0)
        def _():
            acc_ref[...] = jnp.zeros_like(acc_ref)

        acc_ref[...] += product()

        @pl.when(k == nk - 1)
        def _():
            finish(acc_ref[...])

    ins = [a, b] + ([] if pair2 is None else list(pair2)) + list(row_ins) + list(vec_ins)
    specs = [a_spec, b_spec] * (n_mm // 2) + [o_spec] * len(row_ins) + [v_spec] * len(vec_ins)
    out_specs = [o_spec] * n_row_out + [v_spec] * n_vec
    out_shape = [jax.ShapeDtypeStruct((m, n), dt) for dt in out_dtypes] + [jax.ShapeDtypeStruct((1, n), F32)] * n_vec
    outs = _call_2d(kern, name=name, grid=(m // tm, n // tn, nk), in_specs=specs, out_specs=out_specs, out_shape=out_shape,
                    ins=ins, scratch_shapes=[pltpu.VMEM((tm, tn), F32)] if nk > 1 else [],
                    semantics=("arbitrary" if n_vec else "parallel", "parallel", "arbitrary"), side=side)
    carried = None
    if side is not None:
        outs, carried = outs
    outs = outs[0] if epilogue is None else outs
    return outs if side is None else (outs, carried)


def _epi_residual_rms(r, res, gain):
    xn = res + r
    return xn, xn * lax.rsqrt(jnp.mean(xn * xn, axis=-1, keepdims=True) + NORM_EPS) * gain


def _epi_rms_bwd(r, x, dres, gain):
    rs = lax.rsqrt(jnp.mean(x * x, axis=-1, keepdims=True) + NORM_EPS)
    xh = x * rs
    dy = r * gain
    dx = dres + rs * (dy - xh * jnp.mean(dy * xh, axis=-1, keepdims=True))
    return dx, dx, jnp.sum(r * xh, axis=0, keepdims=True)


def _epi_rms_bwd_sum(r, r0, x, dres, gain):
    return _epi_rms_bwd(r + r0, x, dres, gain)


def _epi_loss(r, res, target):
    e = (res + r) - target
    dy = e / e.shape[-1]
    return dy, dy, jnp.sum(e * e, axis=0, keepdims=True)
def _rms_fwd(x, g, *, name, side=None):
    s, d = x.shape
    ts = _pick(s, (512, 256))

    def kern(x_ref, g_ref, h_ref):
        xf = x_ref[...]
        r = lax.rsqrt(jnp.mean(xf * xf, axis=-1, keepdims=True) + NORM_EPS)
        h_ref[...] = (xf * r * g_ref[...]).astype(h_ref.dtype)

    outs = _call_2d(kern, name=name, grid=(s // ts,),
                    in_specs=[pl.BlockSpec((ts, d), lambda i: (i, 0)), pl.BlockSpec((1, d), lambda i: (0, 0))],
                    out_specs=[pl.BlockSpec((ts, d), lambda i: (i, 0))], out_shape=[jax.ShapeDtypeStruct((s, d), BF16)],
                    ins=[x, g], semantics=("parallel",), side=side)
    return outs[0] if side is None else (outs[0][0], outs[1])


def _rms_bwd(x, g, dh, res, *, name):
    s, d = x.shape
    ts = _pick(s, (512, 256))

    def kern(*refs):
        if res is None:
            x_ref, g_ref, dh_ref, dx_ref, dxb_ref, dg_ref = refs
            r_ref = None
        else:
            x_ref, g_ref, dh_ref, r_ref, dx_ref, dxb_ref, dg_ref = refs
        xf = x_ref[...]
        r = lax.rsqrt(jnp.mean(xf * xf, axis=-1, keepdims=True) + NORM_EPS)
        xh = xf * r
        dhf = dh_ref[...].astype(F32)
        dy = dhf * g_ref[...]
        dx = r * (dy - xh * jnp.mean(dy * xh, axis=-1, keepdims=True))
        if r_ref is not None:
            dx = r_ref[...] + dx
        dx_ref[...] = dx
        dxb_ref[...] = dx.astype(dxb_ref.dtype)

        @pl.when(pl.program_id(0) == 0)
        def _():
            dg_ref[...] = jnp.zeros_like(dg_ref)

        dg_ref[...] += jnp.sum(dhf * xh, axis=0, keepdims=True)

    row = pl.BlockSpec((ts, d), lambda i: (i, 0))
    vec = pl.BlockSpec((1, d), lambda i: (0, 0))
    ins = [x, g, dh] + ([] if res is None else [res])
    return _pcall(kern, name=name, grid=(s // ts,), in_specs=[row, vec, row] + ([] if res is None else [row]),
                  out_specs=[row, row, vec],
                  out_shape=[jax.ShapeDtypeStruct((s, d), F32), jax.ShapeDtypeStruct((s, d), BF16), jax.ShapeDtypeStruct((1, d), F32)],
                  compiler_params=_params("arbitrary"))(*ins)


def _sigmoid(x):
    return 1.0 / (1.0 + jnp.exp(-x))


FFN_TM, FFN_TF = 512, 1408


def _ffn_up(h, w1, w3, *, name, side=None):
    s, d = h.shape
    fdim = w1.shape[1]
    tm, tf = _pick(s, (FFN_TM, 256)), _pick(fdim, (FFN_TF, 512, 256, 128))

    def kern(h_ref, w1_ref, w3_ref, a_ref, b_ref, f_ref):
        hb = h_ref[...]
        a = _nn(hb, w1_ref[...])
        b = _nn(hb, w3_ref[...])
        a_ref[...] = a.astype(a_ref.dtype)
        b_ref[...] = b.astype(b_ref.dtype)
        f_ref[...] = (a * _sigmoid(a) * b).astype(f_ref.dtype)

    wspec = pl.BlockSpec((d, tf), lambda i, j: (0, j))
    ospec = pl.BlockSpec((tm, tf), lambda i, j: (i, j))
    shp = jax.ShapeDtypeStruct((s, fdim), BF16)
    return _call_2d(kern, name=name, grid=(s // tm, fdim // tf), in_specs=[pl.BlockSpec((tm, d), lambda i, j: (i, 0)), wspec, wspec],
                    out_specs=[ospec, ospec, ospec], out_shape=[shp, shp, shp], ins=[h, w1, w3],
                    semantics=("parallel", "parallel"), side=side)


def _ffn_dact(dy, w2, a, b, *, name, side=None):
    s, d = dy.shape
    fdim = w2.shape[0]
    tm, tf = _pick(s, (FFN_TM, 256)), _pick(fdim, (FFN_TF, 512, 256, 128))

    def kern(dy_ref, w2_ref, a_ref, b_ref, da_ref, db_ref):
        df = _nt(dy_ref[...], w2_ref[...]) * 0.5
        av = a_ref[...].astype(F32)
        sg = _sigmoid(av)
        da_ref[...] = (df * b_ref[...].astype(F32) * (sg + av * sg * (1.0 - sg))).astype(da_ref.dtype)
        db_ref[...] = (df * (av * sg)).astype(db_ref.dtype)

    ospec = pl.BlockSpec((tm, tf), lambda i, j: (i, j))
    shp = jax.ShapeDtypeStruct((s, fdim), BF16)
    return _call_2d(kern, name=name, grid=(s // tm, fdim // tf),
                    in_specs=[pl.BlockSpec((tm, d), lambda i, j: (i, 0)), pl.BlockSpec((tf, d), lambda i, j: (j, 0)), ospec, ospec],
                    out_specs=[ospec, ospec], out_shape=[shp, shp], ins=[dy, w2, a, b], semantics=("parallel", "parallel"), side=side)


def _head_mean(v, bd):
    outs = []
    for c in range(v.shape[1] // LANES):
        x = v[:, c * LANES:(c + 1) * LANES]
        hi = x.astype(BF16)
        lo = (x - hi.astype(F32)).astype(BF16)
        outs.append(lax.dot_general(jnp.concatenate([hi, lo], axis=1), bd, (((1,), (0,)), ((), ())), preferred_element_type=F32))
    return outs[0] if len(outs) == 1 else jnp.concatenate(outs, axis=1)


def _partner(v):
    w = v.shape[1]
    lane = lax.broadcasted_iota(jnp.int32, v.shape, 1)
    return jnp.where(lane % HEAD_DIM < HEAD_DIM // 2, pltpu.roll(v, w - HEAD_DIM // 2, 1), pltpu.roll(v, HEAD_DIM // 2, 1))


def _block_diag(w=None):
    r = (lax.broadcasted_iota(jnp.int32, (2 * LANES, LANES), 0) % LANES) // HEAD_DIM
    c = lax.broadcasted_iota(jnp.int32, (2 * LANES, LANES), 1) // HEAD_DIM
    return jnp.where(r == c, 1.0 / HEAD_DIM, 0.0).astype(BF16)


def _rope_tables(s):
    half = HEAD_DIM // 2
    inv_freq = jnp.power(ROPE_THETA, -jnp.arange(half, dtype=F32) / half)
    ang = jnp.arange(s).astype(F32)[:, None] * inv_freq[None, :]
    cos, sin = lax.optimization_barrier((jnp.cos(ang), jnp.sin(ang)))
    cos2 = jnp.concatenate([cos, cos, cos, cos], axis=1)
    sin2 = jnp.concatenate([-sin, sin, -sin, sin], axis=1)
    return cos2, sin2


def _qknorm_fwd(src, col0, width, gain, rope, *, name, out_dtype=BF16):
    s = src.shape[0]
    ts = _pick(s, (512, 256))
    cb = col0 // width
    assert col0 % width == 0
    reps = width // LANES
    g = jnp.tile(gain, (1, width // HEAD_DIM))

    def kern(*refs):
        if rope is None:
            x_ref, g_ref, o_ref = refs
        else:
            x_ref, g_ref, c_ref, s_ref, o_ref = refs
        x = x_ref[...].astype(F32)
        bd = _block_diag(width)
        r = lax.rsqrt(_head_mean(x * x, bd) + NORM_EPS)
        y = x * r * g_ref[...]
        if rope is not None:
            y = y * jnp.tile(c_ref[...], (1, reps)) + _partner(y) * jnp.tile(s_ref[...], (1, reps))
        o_ref[...] = y.astype(o_ref.dtype)

    xs = pl.BlockSpec((ts, width), lambda i: (i, cb))
    tab = pl.BlockSpec((ts, LANES), lambda i: (i, 0))
    ins = [src, g] + ([] if rope is None else list(rope))
    specs = [xs, pl.BlockSpec((1, width), lambda i: (0, 0))] + ([] if rope is None else [tab, tab])
    return _pcall(kern, name=name, grid=(s // ts,), in_specs=specs, out_specs=pl.BlockSpec((ts, width), lambda i: (i, 0)),
                  out_shape=jax.ShapeDtypeStruct((s, width), out_dtype), compiler_params=_params("parallel"))(*ins)


def _qknorm_bwd(src, col0, width, gain, rope, dout, *, name):
    s = src.shape[0]
    ts = _pick(s, (512, 256))
    cb = col0 // width
    reps = width // LANES
    g = jnp.tile(gain, (1, width // HEAD_DIM))

    douts = list(dout) if isinstance(dout, (list, tuple)) else [dout]
    piece = width // len(douts)

    def kern(*refs):
        refs = list(refs)
        dg_ref = refs.pop()
        dx_ref = refs.pop()
        do_refs = [refs.pop() for _ in douts][::-1]
        if rope is None:
            x_ref, g_ref = refs
        else:
            x_ref, g_ref, c_ref, s_ref = refs
        x = x_ref[...].astype(F32)
        bd = _block_diag(width)
        r = lax.rsqrt(_head_mean(x * x, bd) + NORM_EPS)
        xh = x * r
        dy = jnp.concatenate([d[...].astype(F32) for d in do_refs], axis=1) if len(do_refs) > 1 else do_refs[0][...].astype(F32)
        if rope is not None:
            dy = dy * jnp.tile(c_ref[...], (1, reps)) + _partner(dy * jnp.tile(s_ref[...], (1, reps)))
        dxh = dy * g_ref[...]
        dx_ref[...] = (r * (dxh - xh * _head_mean(dxh * xh, bd))).astype(dx_ref.dtype)

        @pl.when(pl.program_id(0) == 0)
        def _():
            dg_ref[...] = jnp.zeros_like(dg_ref)

        dg_ref[...] += jnp.sum(dy * xh, axis=0, keepdims=True)

    xs = pl.BlockSpec((ts, width), lambda i: (i, cb))
    row = pl.BlockSpec((ts, width), lambda i: (i, 0))
    vec = pl.BlockSpec((1, width), lambda i: (0, 0))
    tab = pl.BlockSpec((ts, LANES), lambda i: (i, 0))
    ins = [src, g] + ([] if rope is None else list(rope)) + douts
    specs = [xs, vec] + ([] if rope is None else [tab, tab]) + [pl.BlockSpec((ts, piece), lambda i: (i, 0))] * len(douts)
    dx, dg = _pcall(kern, name=name, grid=(s // ts,), in_specs=specs, out_specs=[row, vec],
                    out_shape=[jax.ShapeDtypeStruct((s, width), BF16), jax.ShapeDtypeStruct((1, width), F32)],
                    compiler_params=_params("arbitrary"))(*ins)
    return dx, jnp.sum(dg.reshape(width // HEAD_DIM, HEAD_DIM), axis=0, keepdims=True)


def _tri(strict, n):
    r = lax.broadcasted_iota(jnp.int32, (2 * n, n), 0) % n
    c = lax.broadcasted_iota(jnp.int32, (2 * n, n), 1)
    return jnp.where((r > c) if strict else (r >= c), 1.0, 0.0).astype(BF16)


def _split_dot(v, t2):
    hi = v.astype(BF16)
    lo = (v - hi.astype(F32)).astype(BF16)
    return lax.dot_general(jnp.concatenate([hi, lo], axis=1), t2, (((1,), (0,)), ((), ())), preferred_element_type=F32)


LOG2E = 1.4426950408889634


def _log2_sigmoids(z2):
    lf = -(jnp.maximum(z2, 0.0) + jnp.log2(1.0 + jnp.exp2(-jnp.abs(z2))))
    return z2 + lf, lf


SB2_SUB = 2
SB_KT = 128


def _first_half(shape):
    return lax.broadcasted_iota(jnp.int32, shape, 1) < HEAD_DIM


def _split_pair(t, first):
    zero = jnp.zeros_like(t)
    return [jnp.where(first, t, zero), jnp.where(first, zero, t)]


def _sb2_fwd(p, *, name, side=None):
    s = p.shape[0]
    rq = SB2_SUB * QB
    nq = s // rq
    npair = SB_W // LANES

    def kern(q_ref, k_ref, v_ref, o_ref):
        i = pl.program_id(1)
        first = _first_half((rq, LANES))
        q2 = jnp.concatenate(_split_pair(q_ref[...], first), axis=0)
        t2 = _tri(True, SB_KT)
        rel = lax.broadcasted_iota(jnp.int32, (2 * rq, SB_KT), 1) - lax.broadcasted_iota(jnp.int32, (2 * rq, SB_KT), 0) % rq

        def tile(j, q, rel, carry, acc, masked):
            off = pl.multiple_of(j * SB_KT, SB_KT)
            ls, lf = _log2_sigmoids(_nt(q, k_ref[pl.ds(off, SB_KT), :]) * (SCALE * LOG2E))
            if masked:
                before = rel < i * rq - j * SB_KT
                lf = jnp.where(before, lf, 0.0)
            w = jnp.exp2(ls + _split_dot(lf, t2) + carry)
            if masked:
                w = jnp.where(before, w, 0.0)
            return carry + jnp.sum(lf, axis=1, keepdims=True), acc + _nn(w, v_ref[pl.ds(off, SB_KT), :])

        carry, acc = jnp.zeros((2 * rq, 1), F32), jnp.zeros((2 * rq, LANES), F32)
        for a in range(rq // SB_KT):
            carry, acc = tile(i * (rq // SB_KT) + (rq // SB_KT - 1 - a), q2, rel, carry, acc, True)

        def cond(st):
            return jnp.logical_and(st[0] >= 0, st[1] > 0)

        def body(st):
            carry, acc = tile(st[0], q2, rel, st[2], st[3], False)
            return st[0] - 1, (jnp.max(carry) > SB_DEAD).astype(jnp.int32), carry, acc

        st = lax.while_loop(cond, body, (i * (rq // SB_KT) - 1, jnp.int32(1), carry, acc))
        o_ref[...] = jnp.where(first, st[3][:rq], st[3][rq:])

    outs = _call_2d(kern, name=name, grid=(npair, nq),
                    in_specs=[pl.BlockSpec((rq, LANES), lambda a, i: (i, a)), pl.BlockSpec((s, LANES), lambda a, i: (0, npair + a)),
                              pl.BlockSpec((s, LANES), lambda a, i: (0, 2 * npair + a))],
                    out_specs=[pl.BlockSpec((rq, LANES), lambda a, i: (i, a))], out_shape=[jax.ShapeDtypeStruct((s, SB_W), F32)],
                    ins=[p, p, p], semantics=("parallel", "arbitrary"), side=side)
    return outs[0] if side is None else (outs[0][0], outs[1])


def _sb2_bwd(p, o, do, *, name, side=None):
    s = p.shape[0]
    rq = SB2_SUB * QB
    nq = s // rq
    npair = SB_W // LANES

    def kern(q_ref, k_ref, v_ref, o_ref, do_ref, dq_ref, dk_hbm, dv_hbm, dk_acc, dv_acc, sem):
        pr = pl.program_id(0)
        i = pl.program_id(1)

        @pl.when(i == 0)
        def _():
            dk_acc[...] = jnp.zeros_like(dk_acc)
            dv_acc[...] = jnp.zeros_like(dv_acc)

        first = _first_half((rq, LANES))
        q2 = jnp.concatenate(_split_pair(q_ref[...], first), axis=0)
        do2 = jnp.concatenate(_split_pair(do_ref[...], first), axis=0)
        o2 = o_ref[...]
        dsum = jnp.sum(do2.astype(F32) * jnp.concatenate([o2, o2], axis=0), axis=1, keepdims=True)
        t_strict = _tri(True, SB_KT)
        t_incl = _tri(False, SB_KT)
        rel = lax.broadcasted_iota(jnp.int32, (2 * rq, SB_KT), 1) - lax.broadcasted_iota(jnp.int32, (2 * rq, SB_KT), 0) % rq

        def tile(j, rows, carry, gcarry, dq, masked):
            q, dob, dsm, rel = rows
            off = pl.multiple_of(j * SB_KT, SB_KT)
            kt = k_ref[pl.ds(off, SB_KT), :]
            ls, lf = _log2_sigmoids(_nt(q, kt) * (SCALE * LOG2E))
            if masked:
                before = rel < i * rq - j * SB_KT
                lf = jnp.where(before, lf, 0.0)
            w = jnp.exp2(ls + _split_dot(lf, t_strict) + carry)
            if masked:
                w = jnp.where(before, w, 0.0)
            wr = w.astype(MXU_DT)
            g = _nt(dob, v_ref[pl.ds(off, SB_KT), :]) * wr.astype(F32)
            big_g = dsm - (_split_dot(g, t_incl) + gcarry)
            sig = jnp.exp2(ls)
            dz = g * (1.0 - sig) - sig * big_g
            if masked:
                dz = jnp.where(before, dz, 0.0)
            dz = dz * SCALE
            dk_acc[pl.ds(off, SB_KT), :] += _tn(dz, q)
            dv_acc[pl.ds(off, SB_KT), :] += _tn(wr, dob)
            return (carry + jnp.sum(lf, axis=1, keepdims=True), gcarry + jnp.sum(g, axis=1, keepdims=True),
                    dq + _nn(dz, kt))

        zc = jnp.zeros((2 * rq, 1), F32)
        carry, gcarry, dq = zc, zc, jnp.zeros((2 * rq, LANES), F32)
        whole = (q2, do2, dsum, rel)
        for a in range(rq // SB_KT):
            carry, gcarry, dq = tile(i * (rq // SB_KT) + (rq // SB_KT - 1 - a), whole, carry, gcarry, dq, True)

        def cond(st):
            return jnp.logical_and(st[0] >= 0, st[1] > 0)

        def body(st):
            carry, gcarry, dq = tile(st[0], whole, st[2], st[3], st[4], False)
            return st[0] - 1, (jnp.max(carry) > SB_DEAD).astype(jnp.int32), carry, gcarry, dq

        st = lax.while_loop(cond, body, (i * (rq // SB_KT) - 1, jnp.int32(1), carry, gcarry, dq))
        dq_ref[...] = jnp.where(first, st[4][:rq], st[4][rq:]).astype(dq_ref.dtype)

        @pl.when(i == nq - 1)
        def _():
            cols = pl.ds(pl.multiple_of(pr * LANES, LANES), LANES)
            ck = pltpu.make_async_copy(dk_acc, dk_hbm.at[:, cols], sem.at[0])
            cv = pltpu.make_async_copy(dv_acc, dv_hbm.at[:, cols], sem.at[1])
            ck.start()
            cv.start()
            ck.wait()
            cv.wait()

    blk = pl.BlockSpec((rq, LANES), lambda a, i: (i, a))
    anyspace = pl.BlockSpec(memory_space=pl.ANY)
    shp = jax.ShapeDtypeStruct((s, SB_W), F32)
    return _call_2d(kern, name=name, grid=(npair, nq),
                    in_specs=[blk, pl.BlockSpec((s, LANES), lambda a, i: (0, npair + a)),
                              pl.BlockSpec((s, LANES), lambda a, i: (0, 2 * npair + a)), blk, blk],
                    out_specs=[blk, anyspace, anyspace], out_shape=[jax.ShapeDtypeStruct((s, SB_W), BF16), shp, shp], ins=[p, p, p, o, do],
                    scratch_shapes=[pltpu.VMEM((s, LANES), F32), pltpu.VMEM((s, LANES), F32), pltpu.SemaphoreType.DMA((2,))],
                    semantics=("arbitrary", "arbitrary"), side=side)


def _dsa_rel():
    qi = lax.broadcasted_iota(jnp.int32, (QB, QB), 0)
    kj = lax.broadcasted_iota(jnp.int32, (QB, QB), 1)
    return kj - qi


def _prev_mask(rel, has_prev):
    return rel >= jnp.where(has_prev, 0, QB)


DSA_BT = QB * max(r for _, r in DSA_GROUPS)
def _units_per_batch(r):
    return 8 if r < max(d for _, d in DSA_GROUPS) else 4


def _bdot(a, b, ca, cb):
    return lax.dot_general(a.astype(MXU_DT), b.astype(MXU_DT), (((ca,), (cb,)), ((0,), (0,))), preferred_element_type=F32)


def _bnt(a, b):
    return _bdot(a, b, 2, 2)


def _bnn(a, b):
    return _bdot(a, b, 2, 1)


def _btn(a, b):
    return _bdot(a, b, 1, 1)


def _unit_rows(r, c, b):
    return pl.ds(c + QB * r * b, QB, stride=r)


def _pair_cols(t, first):
    return [jnp.max(jnp.where(first, t, -jnp.inf), axis=1, keepdims=True),
            jnp.max(jnp.where(first, -jnp.inf, t), axis=1, keepdims=True)]


def _dsa2_fwd(qn, kn, v32, g, *, name):
    s = qn.shape[0]
    r = DSA_GROUPS[g][1]
    nbk = DSA_BT // (QB * r)
    npair = DSA_OUT_W // LANES

    def kern(q_ref, k_ref, kp_ref, v_ref, vp_ref, o_ref, l_ref):
        t = pl.program_id(1)
        first = _first_half((QB, LANES))
        rel = _dsa_rel()
        units = [(c, b) for c in range(r) for b in range(nbk)]
        ub = _units_per_batch(r)
        for u0 in range(0, len(units), ub):
            batch = units[u0:u0 + ub]
            qs, kcs, vcs, kps, vps, masks = [], [], [], [], [], []
            for c, b in batch:
                rows = _unit_rows(r, c, b)
                kc, vc = k_ref[rows, :].astype(MXU_DT), v_ref[rows, :].astype(MXU_DT)
                if b > 0:
                    prow = _unit_rows(r, c, b - 1)
                    kpv, vpv, has_prev = k_ref[prow, :], v_ref[prow, :], True
                else:
                    prow = _unit_rows(r, c, nbk - 1)
                    kpv, vpv, has_prev = kp_ref[prow, :], vp_ref[prow, :], t > 0
                for qe in _split_pair(q_ref[rows, :], first):
                    qs.append(qe.astype(MXU_DT))
                    kcs.append(kc)
                    vcs.append(vc)
                    kps.append(kpv.astype(MXU_DT))
                    vps.append(vpv.astype(MXU_DT))
                    masks.append(_prev_mask(rel, has_prev))
            qq = jnp.stack(qs)
            sc = jnp.where(rel <= 0, _bnt(qq, jnp.stack(kcs)) * SCALE, -jnp.inf)
            sp = _bnt(qq, jnp.stack(kps)) * SCALE
            sp = jnp.stack([jnp.where(mk, sp[n], -jnp.inf) for n, mk in enumerate(masks)])
            m = jnp.maximum(jnp.max(sc, axis=2, keepdims=True), jnp.max(sp, axis=2, keepdims=True))
            pc = jnp.exp(sc - m)
            pp = jnp.exp(sp - m)
            den = jnp.sum(pc, axis=2, keepdims=True) + jnp.sum(pp, axis=2, keepdims=True)
            out = (_bnn(pc, jnp.stack(vcs)) + _bnn(pp, jnp.stack(vps))) / den
            lse = m + jnp.log(den)
            for idx, (c, b) in enumerate(batch):
                rows = _unit_rows(r, c, b)
                o_ref[rows, :] = jnp.where(first, out[2 * idx], out[2 * idx + 1])
                l_ref[rows, :] = jnp.where(first, lse[2 * idx], lse[2 * idx + 1])

    npg = DSA_HPG * HEAD_DIM // LANES
    cur = pl.BlockSpec((DSA_BT, LANES), lambda a, t: (t, npg * g + a))
    prev = pl.BlockSpec((DSA_BT, LANES), lambda a, t: (jnp.maximum(t - 1, 0), npg * g + a))
    out = pl.BlockSpec((DSA_BT, LANES), lambda a, t: (t, a))
    shp = jax.ShapeDtypeStruct((s, DSA_OUT_W), F32)
    return _pcall(kern, name=name, grid=(npair, s // DSA_BT), in_specs=[cur, cur, prev, cur, prev], out_specs=[out, out],
                  out_shape=[shp, shp], compiler_params=_params("parallel", "parallel"))(qn, kn, kn, v32, v32)


def _dsa2_combine(parts, *, name):
    s, wd = parts[0][0].shape
    ts = _pick(s, (1024, 512, 256))

    def kern(o0, l0, o1, l1, o2, l2, o_ref, l_ref):
        ls = [l0[...], l1[...], l2[...]]
        m = jnp.maximum(jnp.maximum(ls[0], ls[1]), ls[2])
        es = [jnp.exp(l - m) for l in ls]
        den = es[0] + es[1] + es[2]
        o_ref[...] = (es[0] * o0[...] + es[1] * o1[...] + es[2] * o2[...]) / den
        l_ref[...] = m + jnp.log(den)

    blk = pl.BlockSpec((ts, wd), lambda i: (i, 0))
    shp = jax.ShapeDtypeStruct((s, wd), F32)
    flat = [t for pair in parts for t in pair]
    return _pcall(kern, name=name, grid=(s // ts,), in_specs=[blk] * 6, out_specs=[blk, blk], out_shape=[shp, shp],
                  compiler_params=_params("parallel"))(*flat)


def _dsa2_prep(o, do, *, name):
    s, wd = o.shape
    ts = _pick(s, (1024, 512, 256))

    def kern(o_ref, do_ref, d_ref):
        d_ref[...] = _head_mean(do_ref[...] * o_ref[...], _block_diag(wd)) * HEAD_DIM

    blk = pl.BlockSpec((ts, wd), lambda i: (i, 0))
    return _pcall(kern, name=name, grid=(s // ts,), in_specs=[blk, blk], out_specs=blk,
                  out_shape=jax.ShapeDtypeStruct((s, wd), F32), compiler_params=_params("parallel"))(o, do)


def _dsa2_bwd(qn, kn, v32, do, lse, dd, g, *, name):
    s = qn.shape[0]
    r = DSA_GROUPS[g][1]
    nbk = DSA_BT // (QB * r)
    npair = DSA_OUT_W // LANES
    nsteps = s // DSA_BT

    def kern(q_ref, qn_ref, k_ref, kp_ref, v_ref, vp_ref, do_ref, don_ref, l_ref, ln_ref, d_ref, dn_ref,
             dq_ref, dk_ref, dv_ref):
        t = pl.program_id(1)
        first = _first_half((QB, LANES))
        rel = _dsa_rel()

        def pairs(items):
            qq = jnp.stack([it[0].astype(MXU_DT) for it in items])
            dd = jnp.stack([it[1].astype(MXU_DT) for it in items])
            kk = jnp.stack([it[4].astype(MXU_DT) for it in items])
            vv = jnp.stack([it[5].astype(MXU_DT) for it in items])
            p = jnp.exp(_bnt(qq, kk) * SCALE - jnp.stack([it[2] for it in items]))
            p = jnp.stack([jnp.where(it[6], p[n], 0.0) for n, it in enumerate(items)])
            ds = p * (_bnt(dd, vv) - jnp.stack([it[3] for it in items])) * SCALE
            return _bnn(ds, kk), _btn(ds, qq), _btn(p, dd)

        def heads(rows, qr, dor, lr, dr):
            return list(zip(_split_pair(qr[rows, :], first), _split_pair(dor[rows, :], first),
                            _pair_cols(lr[rows, :], first), _pair_cols(dr[rows, :], first)))

        units = [(c, b) for c in range(r) for b in range(nbk)]
        dk_of, dv_of = [None] * len(units), [None] * len(units)
        ub = _units_per_batch(r)
        for u0 in range(0, len(units), ub // 2):
            batch = list(enumerate(units))[u0:u0 + ub // 2]
            items = []
            for u, (c, b) in batch:
                rows = _unit_rows(r, c, b)
                kc, vc = k_ref[rows, :], v_ref[rows, :]
                if b > 0:
                    prow = _unit_rows(r, c, b - 1)
                    kpv, vpv, pmask = k_ref[prow, :], v_ref[prow, :], _prev_mask(rel, True)
                else:
                    prow = _unit_rows(r, c, nbk - 1)
                    kpv, vpv, pmask = kp_ref[prow, :], vp_ref[prow, :], _prev_mask(rel, t > 0)
                for hd in heads(rows, q_ref, do_ref, l_ref, d_ref):
                    items.append(hd + (kc, vc, rel <= 0))
                    items.append(hd + (kpv, vpv, pmask))
            dq, dk, dv = pairs(items)
            for n, (u, (c, b)) in enumerate(batch):
                dq_ref[_unit_rows(r, c, b), :] = jnp.where(first, dq[4 * n] + dq[4 * n + 1], dq[4 * n + 2] + dq[4 * n + 3])
                dk_of[u] = dk[4 * n] + dk[4 * n + 2]
                dv_of[u] = dv[4 * n] + dv[4 * n + 2]
                if b > 0:
                    dk_of[u - 1] = dk_of[u - 1] + (dk[4 * n + 1] + dk[4 * n + 3])
                    dv_of[u - 1] = dv_of[u - 1] + (dv[4 * n + 1] + dv[4 * n + 3])
        lasts = [c * nbk + nbk - 1 for c in range(r)]
        for c0 in range(0, r, 4):
            chunk = list(range(c0, min(c0 + 4, r)))
            items = []
            for c in chunk:
                last = _unit_rows(r, c, nbk - 1)
                for hd in heads(_unit_rows(r, c, 0), qn_ref, don_ref, ln_ref, dn_ref):
                    items.append(hd + (k_ref[last, :], v_ref[last, :], _prev_mask(rel, t < nsteps - 1)))
            _, dk, dv = pairs(items)
            for n, c in enumerate(chunk):
                dk_of[lasts[c]] = dk_of[lasts[c]] + (dk[2 * n] + dk[2 * n + 1])
                dv_of[lasts[c]] = dv_of[lasts[c]] + (dv[2 * n] + dv[2 * n + 1])
        for u, (c, b) in enumerate(units):
            dk_ref[_unit_rows(r, c, b), :] = dk_of[u]
            dv_ref[_unit_rows(r, c, b), :] = dv_of[u]

    npg = DSA_HPG * HEAD_DIM // LANES

    def at(shift, col):
        return pl.BlockSpec((DSA_BT, LANES), lambda a, t: (jnp.clip(t + shift, 0, nsteps - 1), col(a)))

    gcol = lambda a: npg * g + a
    ocol = lambda a: a
    specs = [at(0, gcol), at(1, gcol), at(0, gcol), at(-1, gcol), at(0, gcol), at(-1, gcol),
             at(0, ocol), at(1, ocol), at(0, ocol), at(1, ocol), at(0, ocol), at(1, ocol)]
    shp = jax.ShapeDtypeStruct((s, DSA_OUT_W), F32)
    return _pcall(kern, name=name, grid=(npair, nsteps), in_specs=specs, out_specs=[at(0, ocol)] * 3, out_shape=[shp, shp, shp],
                  compiler_params=_params("parallel", "parallel"))(qn, qn, kn, kn, v32, v32, do, do, lse, lse, dd, dd)


def _mem2_fwd(qn, km, kv, *, name):
    s = qn.shape[0]
    ml = km.shape[0]
    tq = _pick(s, (2048, 1024, 512, 256))
    npair = MEM_W // LANES

    def kern(q_ref, k_ref, v_ref, o_ref):
        first = _first_half((tq, LANES))
        q2 = jnp.concatenate(_split_pair(q_ref[...], first), axis=0)
        sc = _nt(q2, k_ref[...]) * SCALE
        e = jnp.exp(sc - jnp.max(sc, axis=1, keepdims=True))
        o2 = _nn(e / jnp.sum(e, axis=1, keepdims=True), v_ref[...])
        o_ref[...] = jnp.where(first, o2[:tq], o2[tq:])

    blk = pl.BlockSpec((tq, LANES), lambda a, i: (i, a))
    return _pcall(kern, name=name, grid=(npair, s // tq),
                  in_specs=[blk, pl.BlockSpec((ml, LANES), lambda a, i: (0, a)), pl.BlockSpec((ml, LANES), lambda a, i: (0, npair + a))],
                  out_specs=blk, out_shape=jax.ShapeDtypeStruct((s, MEM_W), F32),
                  compiler_params=_params("parallel", "parallel"))(qn, km, kv)


def _mem2_bwd(qn, km, kv, do, *, name):
    s = qn.shape[0]
    ml = km.shape[0]
    tq = _pick(s, (2048, 1024, 512, 256))
    npair = MEM_W // LANES

    def kern(q_ref, k_ref, v_ref, do_ref, dq_ref, dk_ref, dv_ref):
        @pl.when(pl.program_id(1) == 0)
        def _():
            dk_ref[...] = jnp.zeros_like(dk_ref)
            dv_ref[...] = jnp.zeros_like(dv_ref)

        first = _first_half((tq, LANES))
        q2 = jnp.concatenate(_split_pair(q_ref[...], first), axis=0)
        do2 = jnp.concatenate(_split_pair(do_ref[...], first), axis=0)
        sc = _nt(q2, k_ref[...]) * SCALE
        e = jnp.exp(sc - jnp.max(sc, axis=1, keepdims=True))
        p = e / jnp.sum(e, axis=1, keepdims=True)
        dp = _nt(do2, v_ref[...])
        ds = p * (dp - jnp.sum(p * dp, axis=1, keepdims=True)) * SCALE
        dq2 = _nn(ds, k_ref[...])
        dk_ref[...] += _tn(ds, q2)
        dv_ref[...] += _tn(p, do2)
        dq_ref[...] = jnp.where(first, dq2[:tq], dq2[tq:])

    blk = pl.BlockSpec((tq, LANES), lambda a, i: (i, a))
    kblk = pl.BlockSpec((ml, LANES), lambda a, i: (0, a))
    kshape = jax.ShapeDtypeStruct((ml, MEM_W), F32)
    return _pcall(kern, name=name, grid=(npair, s // tq),
                  in_specs=[blk, kblk, pl.BlockSpec((ml, LANES), lambda a, i: (0, npair + a)), blk],
                  out_specs=[blk, kblk, kblk], out_shape=[jax.ShapeDtypeStruct((s, MEM_W), F32), kshape, kshape],
                  compiler_params=_params("parallel", "arbitrary"))(qn, km, kv, do)


def _merge_fwd(logits, bias, ya, yb, yc, *, name):
    s, d = ya.shape
    ts = _pick(s, (1024, 512, 256))

    def kern(l0, l1, l2, b0, b1, b2, a_ref, b_ref, c_ref, o_ref):
        m = 0.0
        for l_ref, bb_ref, y_ref in ((l0, b0, a_ref), (l1, b1, b_ref), (l2, b2, c_ref)):
            m = m + _sigmoid(l_ref[...].astype(F32) + bb_ref[...]) * y_ref[...].astype(F32)
        o_ref[...] = m.astype(o_ref.dtype)

    row = pl.BlockSpec((ts, d), lambda i: (i, 0))
    lg = [pl.BlockSpec((ts, d), functools.partial(lambda i, c: (i, c), c=c)) for c in range(3)]
    bs = [pl.BlockSpec((1, d), functools.partial(lambda i, c: (0, c), c=c)) for c in range(3)]
    return _pcall(kern, name=name, grid=(s // ts,), in_specs=lg + bs + [row, row, row], out_specs=row,
                  out_shape=jax.ShapeDtypeStruct((s, d), BF16),
                  compiler_params=_params("parallel"))(logits, logits, logits, bias, bias, bias, ya, yb, yc)


def _merge_bwd(logits, bias, ya, yb, yc, dm, *, name):
    s, d = ya.shape
    ts = _pick(s, (512, 256))

    def kern(l0, l1, l2, b0, b1, b2, a_ref, b_ref, c_ref, dm_ref, da_ref, db_ref, dc_ref, dl_ref, dbias_ref):
        dmv = dm_ref[...].astype(F32)

        @pl.when(pl.program_id(0) == 0)
        def _():
            dbias_ref[...] = jnp.zeros_like(dbias_ref)

        for c, (l_ref, bb_ref, y_ref, dy_ref) in enumerate(((l0, b0, a_ref, da_ref), (l1, b1, b_ref, db_ref), (l2, b2, c_ref, dc_ref))):
            g = _sigmoid(l_ref[...].astype(F32) + bb_ref[...])
            dy_ref[...] = (dmv * g).astype(dy_ref.dtype)
            dl = dmv * y_ref[...].astype(F32) * g * (1.0 - g)
            dl_ref[:, c * d:(c + 1) * d] = dl.astype(dl_ref.dtype)
            dbias_ref[:, c * d:(c + 1) * d] += jnp.sum(dl, axis=0, keepdims=True)

    row = pl.BlockSpec((ts, d), lambda i: (i, 0))
    lg = [pl.BlockSpec((ts, d), functools.partial(lambda i, c: (i, c), c=c)) for c in range(3)]
    bs = [pl.BlockSpec((1, d), functools.partial(lambda i, c: (0, c), c=c)) for c in range(3)]
    yshape = jax.ShapeDtypeStruct((s, d), BF16)
    return _pcall(kern, name=name, grid=(s // ts,), in_specs=lg + bs + [row, row, row, row],
                  out_specs=[row, row, row, pl.BlockSpec((ts, 3 * d), lambda i: (i, 0)), pl.BlockSpec((1, 3 * d), lambda i: (0, 0))],
                  out_shape=[yshape] * 3 + [jax.ShapeDtypeStruct((s, 3 * d), BF16), jax.ShapeDtypeStruct((1, 3 * d), F32)],
                  compiler_params=_params("arbitrary"))(logits, logits, logits, bias, bias, bias, ya, yb, yc, dm)


G_FFN1 = ['ffn1_w1', 'ffn1_w3', 'ffn1_w2']
G_FFN2 = ['ffn2_w1', 'ffn2_w3', 'ffn2_w2']
G_MID = [n for n in BIG if n not in G_FFN1 + G_FFN2]


def _ffn_fwd(h, w1, w3, w2, tag, epilogue, side=None):
    carried = None
    if side is None:
        a, b, f = _ffn_up(h, w1, w3, name=f"{tag}_up")
    else:
        (a, b, f), carried = _ffn_up(h, w1, w3, name=f"{tag}_up", side=side)
    if callable(w2):
        w2 = w2(carried)
    outs = _matmul(f, w2, name=f"{tag}_down", alpha=0.5, tm=512, tn=1024, tk=2816, epilogue=epilogue)
    return outs, (h, a, b, f), carried


def _ffn_bwd(x, norm, w1, w3, w2, saved, dy, dyb, tag, side_first=None, side=None, own_side=None):
    h, a, b, f = saved
    dw2 = _matmul(f, dyb, name=f"{tag}_dw2", ta=True, alpha=0.5, tm=1408, tn=1024, tk=2048, side=side_first)
    carried = None
    if side_first is not None:
        dw2, carried = dw2
    if side is None:
        da, db = _ffn_dact(dyb, w2, a, b, name=f"{tag}_dact")
    else:
        (da, db), got = _ffn_dact(dyb, w2, a, b, name=f"{tag}_dact", side=side)
        carried = (carried or []) + got
    dw1 = _matmul(h, da, name=f"{tag}_dw1", ta=True, tm=1024, tn=1408, tk=2048)
    dw3 = _matmul(h, db, name=f"{tag}_dw3", ta=True, tm=1024, tn=1408, tk=2048)
    outs = _matmul(da, w1, name=f"{tag}_dh", tb=True, tm=512, tn=1024, tk=1408, pair2=(db, w3),
                   epilogue=(_epi_rms_bwd, [x, dy], [norm], [F32, BF16], 1),
                   side=None if own_side is None else own_side(dw1, dw3, dw2))
    (dx, dxb, dnorm), own = outs if own_side is not None else (outs, None)
    return dx, dxb, dnorm, dw1, dw3, dw2, carried, own


def _local_step(x, mem, loss_target, wl, ws):
    s, d = x.shape
    assert s % (QB * 16) == 0
    rope = _rope_tables(s)
    bf = {n: wl[n].astype(BF16) for n in BIG}
    w = dict(ws)

    def gather(names):
        return _side([bf[n] for n in names], _two_level_phases())

    def whole(names, gathered):
        return {n: _whole_weight(n, t) for n, t in zip(names, gathered)}

    first_needed = ['ffn1_w1', 'ffn1_w3']
    then_needed = ['ffn1_w2'] + G_MID
    h1, early = _rms_fwd(x, w['ffn1_norm'], name="ffn1_rms", side=gather(first_needed))
    w.update(whole(first_needed, early))
    (x1, h), sv1, late = _ffn_fwd(h1, w['ffn1_w1'], w['ffn1_w3'], lambda got: _whole_weight('ffn1_w2', got[0]), "ffn1",
                                  (_epi_residual_rms, [x], [w['mix_norm']], [F32, BF16], 0),
                                  side=gather(then_needed))
    w.update(whole(then_needed, late))
    p = _matmul(h, w['w_in'], name="in_proj", out_dtype=BF16, tn=1024)
    logits = _matmul(h, w['w_gate'], name="gate_proj", out_dtype=BF16, tn=1024)
    c_qb, c_kb, c_vb, c_qc = 3 * SB_W, 3 * SB_W + DSA_W, 3 * SB_W + 2 * DSA_W, 3 * SB_W + 3 * DSA_W

    oa_t, late = _sb2_fwd(p, name="sb_fwd", side=gather(G_FFN2))
    w.update(whole(G_FFN2, late))
    ya = _matmul(oa_t, w['w_branch_sb'], name="sb_out", out_dtype=BF16)

    qb_n = _qknorm_fwd(p, c_qb, DSA_W, w['qn_dsa'], rope, name="dsa_qnorm", out_dtype=F32)
    kb_n = _qknorm_fwd(p, c_kb, DSA_W, w['kn_dsa'], rope, name="dsa_knorm", out_dtype=F32)
    vb32 = p[:, c_vb:c_vb + DSA_W].astype(F32)
    groups = range(len(DSA_GROUPS))
    ob_t, lse_b = _dsa2_combine([_dsa2_fwd(qb_n, kb_n, vb32, gi, name=f"dsa_fwd{gi}") for gi in groups], name="dsa_combine")
    yb = _matmul(ob_t, w['w_branch_dsa'], name="dsa_out", out_dtype=BF16)

    memh = _rms_fwd(mem, w['mem_norm'], name="mem_rms")
    kv = _matmul(memh, w['w_mem_kv'], name="mem_kv", out_dtype=BF16)
    km_n = _qknorm_fwd(kv, 0, MEM_W, w['kn_mem'], None, name="mem_knorm")
    qc_n = _qknorm_fwd(p, c_qc, MEM_W, w['qn_mem'], None, name="mem_qnorm")
    oc_t = _mem2_fwd(qc_n, km_n, kv, name="mem_fwd")
    yc = _matmul(oc_t, w['w_branch_mem'], name="mem_out", out_dtype=BF16)

    merged = _merge_fwd(logits, w['b_gate'], ya, yb, yc, name="merge")
    x2, h2 = _matmul(merged, w['w_out'], name="out_proj", tn=1024,
                     epilogue=(_epi_residual_rms, [x1], [w['ffn2_norm']], [F32, BF16], 0))
    (dx3, dx3b, sq), sv2, _ = _ffn_fwd(h2, w['ffn2_w1'], w['ffn2_w3'], w['ffn2_w2'], "ffn2",
                                       (_epi_loss, [x2, loss_target], [], [F32, BF16], 1))
    loss = jnp.sum(sq) * (0.5 / d)

    g, recv = {}, {}

    def owners(names):
        return [_for_owners(n, g[n], wl[n].shape) for n in names]

    dx2, dx2b, g['ffn2_norm'], g['ffn2_w1'], g['ffn2_w3'], g['ffn2_w2'], _, _ = _ffn_bwd(
        x2, w['ffn2_norm'], w['ffn2_w1'], w['ffn2_w3'], w['ffn2_w2'], sv2, dx3, dx3b, "ffn2")

    g['w_out'] = _matmul(merged, dx2b, name="d_w_out", ta=True, tn=1024, tk=512)
    dm = _matmul(dx2b, w['w_out'], name="d_merged", tb=True, out_dtype=BF16, tn=1024)
    dya, dyb, dyc, dlogits, g['b_gate'] = _merge_bwd(logits, w['b_gate'], ya, yb, yc, dm, name="d_merge")

    g['w_branch_sb'] = _matmul(oa_t, dya, name="d_w_sb", ta=True, tn=1024, tk=512)
    g['w_branch_dsa'] = _matmul(ob_t, dyb, name="d_w_dsa", ta=True, tk=512)
    g['w_branch_mem'] = _matmul(oc_t, dyc, name="d_w_mem", ta=True, tk=512)
    doa = _matmul(dya, w['w_branch_sb'], name="d_oa", tb=True, out_dtype=BF16)
    dob = _matmul(dyb, w['w_branch_dsa'], name="d_ob", tb=True)
    doc = _matmul(dyc, w['w_branch_mem'], name="d_oc", tb=True, out_dtype=BF16)

    (dqa, dka, dva), got = _sb2_bwd(p, oa_t, doa, name="sb_bwd", side=_side(owners(G_FFN2), _direct_phases(True)))
    recv.update(zip(G_FFN2, got))

    dd_b = _dsa2_prep(ob_t, dob, name="dsa_prep")
    dgrp = [_dsa2_bwd(qb_n, kb_n, vb32, dob, lse_b, dd_b, gi, name=f"dsa_bwd{gi}") for gi in groups]
    dvb = jnp.concatenate([t[2] for t in dgrp], axis=1).astype(BF16)
    dqb, g['qn_dsa'] = _qknorm_bwd(p, c_qb, DSA_W, w['qn_dsa'], rope, [t[0] for t in dgrp], name="d_dsa_qnorm")
    dkb, g['kn_dsa'] = _qknorm_bwd(p, c_kb, DSA_W, w['kn_dsa'], rope, [t[1] for t in dgrp], name="d_dsa_knorm")

    dqc_n, dkm_n, dvm = _mem2_bwd(qc_n, km_n, kv, doc, name="mem_bwd")
    dqc, g['qn_mem'] = _qknorm_bwd(p, c_qc, MEM_W, w['qn_mem'], None, dqc_n, name="d_mem_qnorm")
    dkm, g['kn_mem'] = _qknorm_bwd(kv, 0, MEM_W, w['kn_mem'], None, dkm_n, name="d_mem_knorm")
    dkv = jnp.concatenate([dkm, dvm.astype(BF16)], axis=1)
    g['w_mem_kv'] = _matmul(memh, dkv, name="d_w_mem_kv", ta=True)
    dmemh = _matmul(dkv, w['w_mem_kv'], name="d_memh", tb=True)
    _, _, g['mem_norm'] = _rms_bwd(mem, w['mem_norm'], dmemh, None, name="d_mem_rms")

    dp = jnp.concatenate([dqa.astype(BF16), dka.astype(BF16), dva.astype(BF16),
                          dqb, dkb, dvb, dqc], axis=1)
    g['w_in'] = _matmul(h, dp, name="d_w_in", ta=True, tn=2048, tk=1024)
    g['w_gate'] = _matmul(h, dlogits, name="d_w_gate", ta=True, tn=1536, tk=1024)
    dh = _matmul(dp, w['w_in'], name="d_h_in", tb=True, tn=1024, tk=2048)
    dx1, dx1b, g['mix_norm'] = _matmul(dlogits, w['w_gate'], name="d_h_gate", tb=True, tm=512, tn=1024, tk=3072,
                                       epilogue=(_epi_rms_bwd_sum, [dh, x1, dx2], [w['mix_norm']], [F32, BF16], 1))

    mid_b = ['w_gate', 'w_out']
    mid_a = [n for n in G_MID if n not in mid_b]

    def own_side(dw1, dw3, dw2):
        g.update(ffn1_w1=dw1, ffn1_w3=dw3, ffn1_w2=dw2)
        return _side(owners(G_FFN1), _direct_phases(True))

    dx0, _, g['ffn1_norm'], _, _, _, got_mid, got_own = _ffn_bwd(
        x, w['ffn1_norm'], w['ffn1_w1'], w['ffn1_w3'], w['ffn1_w2'], sv1, dx1, dx1b, "ffn1",
        side_first=_side(owners(mid_b), _direct_phases(True)), side=_side(owners(mid_a), _direct_phases(True)),
        own_side=own_side)
    recv.update(zip(mid_b + mid_a, got_mid))
    recv.update(zip(G_FFN1, got_own))
    return loss, dx0, recv, {n: g[n] for n in SMALL}


def _whole_weight(name, gathered):
    _, r, c = gathered.shape
    return gathered.reshape(N_DEV * r, c) if SHARD_AXIS[name] == 0 else gathered.transpose(1, 0, 2).reshape(r, N_DEV * c)


def _for_owners(name, grad, shard_shape):
    r, c = shard_shape
    blk = grad.reshape(N_DEV, r, c) if SHARD_AXIS[name] == 0 else grad.reshape(r, N_DEV, c).transpose(1, 0, 2)
    return blk.astype(BF16)


def _pack_small(d, names, extra_rows):
    parts = []
    for n in names:
        v = d[n].reshape(-1)
        pad = (-v.size) % LANES
        parts.append(jnp.concatenate([v, jnp.zeros((pad,), v.dtype)]).reshape(-1, LANES))
    t = jnp.concatenate(parts, axis=0)
    return jnp.concatenate([t, jnp.zeros((extra_rows, LANES), t.dtype)], axis=0)


def _unpack_small(t, like, names):
    out, off = {}, 0
    for n in names:
        size = like[n].size
        rows = -(-size // LANES)
        out[n] = t[off:off + rows].reshape(-1)[:size].reshape(like[n].shape)
        off += rows
    return out


def _direct_phases(per_peer):
    def descriptors(src_ref, out_ref, send_sems, recv_sems, local_sem):
        x, y, c = lax.axis_index("x"), lax.axis_index("y"), lax.axis_index("c")
        me = 4 * x + 2 * y + c
        mine = pltpu.make_async_copy(src_ref.at[me] if per_peer else src_ref, out_ref.at[me], local_sem)
        copies = []
        for k in range(1, N_DEV):
            px = 1 - x if k & 4 else x
            py = 1 - y if k & 2 else y
            pc = 1 - c if k & 1 else c
            copies.append(pltpu.make_async_remote_copy(
                src_ref=src_ref.at[4 * px + 2 * py + pc] if per_peer else src_ref, dst_ref=out_ref.at[me],
                send_sem=send_sems.at[k - 1], recv_sem=recv_sems.at[k - 1],
                device_id=(px, py, pc), device_id_type=pl.DeviceIdType.MESH))
        return mine, copies

    def start(*refs):
        mine, copies = descriptors(*refs)
        mine.start()
        for cp in copies:
            cp.start()

    def forward(*refs):
        pass

    def finish(*refs):
        mine, copies = descriptors(*refs)
        for cp in copies:
            cp.wait_recv()
        for cp in copies:
            cp.wait_send()
        mine.wait()

    return start, forward, finish


def _exchange_parts(srcs, phases):
    n = len(srcs)
    shapes = [jax.ShapeDtypeStruct((N_DEV,) + tuple(s.shape[-2:]), s.dtype) for s in srcs]
    sems = [pltpu.SemaphoreType.DMA((n, N_DEV - 1)), pltpu.SemaphoreType.DMA((n, N_DEV - 1)), pltpu.SemaphoreType.DMA((n,))]

    def lift(phase):
        def run(src_refs, out_refs, send, recv, local):
            for a, (s_ref, o_ref) in enumerate(zip(src_refs, out_refs)):
                phase(s_ref, o_ref, send.at[a], recv.at[a], local.at[a])
        return run

    return shapes, sems, [lift(p) for p in phases]


def _exchange(srcs, phases, *, name):
    shapes, sems, runs = _exchange_parts(srcs, phases)
    n = len(srcs)

    def body(*refs):
        for run in runs:
            run(refs[:n], refs[n:2 * n], *refs[2 * n:])

    anyspace = pl.BlockSpec(memory_space=pl.ANY)
    return _pcall(body, name=name, in_specs=[anyspace] * n, out_specs=[anyspace] * n, out_shape=shapes, scratch_shapes=sems)(*srcs)


def _side(srcs, phases):
    shapes, sems, (start, forward, finish) = _exchange_parts(srcs, phases)

    def before(first, mid, ins, outs, scratch):
        pl.when(first)(lambda: start(ins, outs, *scratch))
        pl.when(mid)(lambda: forward(ins, outs, *scratch))

    def after(last, ins, outs, scratch):
        pl.when(last)(lambda: finish(ins, outs, *scratch))

    return list(srcs), shapes, sems, before, after


def _call_2d(kern, *, name, grid, in_specs, out_specs, out_shape, ins, scratch_shapes=(), semantics, side=None):
    if side is None:
        return _pcall(kern, name=name, grid=grid, in_specs=in_specs, out_specs=out_specs, out_shape=out_shape,
                      scratch_shapes=list(scratch_shapes), compiler_params=_params(*semantics))(*ins)
    s_ins, s_shapes, s_scratch, before, after = side
    n_in, n_out, n_scr = len(ins), len(out_shape), len(scratch_shapes)

    def combined(*refs):
        refs = list(refs)
        cut = [n_in, len(s_ins), n_out, len(s_shapes), n_scr, len(s_scratch)]
        parts, pos = [], 0
        for c in cut:
            parts.append(refs[pos:pos + c])
            pos += c
        m_in, c_in, m_out, c_out, m_scr, c_scr = parts
        ids = [pl.program_id(a) for a in range(len(grid))]
        inner_zero = functools.reduce(jnp.logical_and, [i == 0 for i in ids[1:]], True)
        first = jnp.logical_and(ids[0] == 0, inner_zero)
        mid = jnp.logical_and(ids[0] == grid[0] // 2, inner_zero)
        last = functools.reduce(jnp.logical_and, [i == n - 1 for i, n in zip(ids, grid)])
        before(first, mid, c_in, c_out, c_scr)
        kern(*m_in, *m_out, *m_scr)
        after(last, c_in, c_out, c_scr)

    anyspace = pl.BlockSpec(memory_space=pl.ANY)
    outs = _pcall(combined, name=name, grid=grid, in_specs=list(in_specs) + [anyspace] * len(s_ins),
                  out_specs=list(out_specs) + [anyspace] * len(s_shapes), out_shape=list(out_shape) + s_shapes,
                  scratch_shapes=list(scratch_shapes) + s_scratch, compiler_params=_params(*["arbitrary"] * len(grid)))(*ins, *s_ins)
    return outs[:n_out], outs[n_out:]


def _two_level_phases():
    def parts(src_ref, out_ref, send_sems, recv_sems, local_sem):
        x, y, c = lax.axis_index("x"), lax.axis_index("y"), lax.axis_index("c")
        me, sibling = (x, y, c), (x, y, 1 - c)
        chips = [(1 - x, y), (x, 1 - y), (1 - x, 1 - y)]

        def slab(px, py, pc):
            return out_ref.at[4 * px + 2 * py + pc]

        def copy(k, block, to, from_src=False):
            return pltpu.make_async_remote_copy(
                src_ref=src_ref if from_src else slab(*block), dst_ref=slab(*block),
                send_sem=send_sems.at[k], recv_sem=recv_sems.at[k], device_id=to, device_id_type=pl.DeviceIdType.MESH)

        return dict(
            mine=lambda: pltpu.make_async_copy(src_ref, slab(*me), local_sem),
            first=lambda: [copy(0, me, sibling, True)] + [copy(1 + j, me, (*chip, c), True) for j, chip in enumerate(chips)],
            passed=lambda: [copy(4 + j, (*chip, c), sibling) for j, chip in enumerate(chips)],
            landed=lambda: [copy(1 + j, (*chip, c), me) for j, chip in enumerate(chips)],
            late=lambda: [copy(0, sibling, me)] + [copy(4 + j, (*chip, 1 - c), me) for j, chip in enumerate(chips)])

    def start(*refs):
        make = parts(*refs)
        make['mine']().start()
        for cp in make['first']():
            cp.start()

    def forward(*refs):
        make = parts(*refs)
        for arrived, onward in zip(make['landed'](), make['passed']()):
            arrived.wait_recv()
            onward.start()

    def finish(*refs):
        make = parts(*refs)
        for cp in make['late']():
            cp.wait_recv()
        for cp in make['first']() + make['passed']():
            cp.wait_send()
        make['mine']().wait()

    return start, forward, finish


def _adamw(recv, w, m, v, *, name):
    rows, cols = w.shape
    tr = _pick(rows, (256, 128, 64))

    def kern(r_ref, w_ref, m_ref, v_ref, g_ref, d_ref, mo_ref, vo_ref):
        g = r_ref[0].astype(F32)
        for p in range(1, N_DEV):
            g = g + r_ref[p].astype(F32)
        mn = ADAM_B1 * m_ref[...] + (1.0 - ADAM_B1) * g
        vn = ADAM_B2 * v_ref[...] + (1.0 - ADAM_B2) * (g * g)
        m_hat = mn / (1.0 - ADAM_B1 ** ADAM_STEP)
        v_hat = vn / (1.0 - ADAM_B2 ** ADAM_STEP)
        g_ref[...] = g
        d_ref[...] = -ADAM_LR * (m_hat / (jnp.sqrt(v_hat) + ADAM_EPS) + ADAM_WD * w_ref[...])
        mo_ref[...] = mn
        vo_ref[...] = vn

    row = pl.BlockSpec((tr, cols), lambda i: (i, 0))
    shp = jax.ShapeDtypeStruct((rows, cols), F32)
    return _pcall(kern, name=name, grid=(rows // tr,), in_specs=[pl.BlockSpec((N_DEV, tr, cols), lambda i: (0, i, 0)), row, row, row],
                  out_specs=[row, row, row, row], out_shape=[shp, shp, shp, shp], compiler_params=_params("parallel"))(recv, w, m, v)


INPUTS = ['x', 'mem'] + WEIGHTS + ['loss_target'] + ['m_' + n for n in WEIGHTS] + ['v_' + n for n in WEIGHTS]
SMALL_PAD_ROWS = 4


def kernel(x, mem, ffn1_norm, ffn1_w1, ffn1_w3, ffn1_w2, mix_norm, mem_norm, w_in, w_mem_kv, qn_dsa, kn_dsa, qn_mem, kn_mem, w_branch_sb, w_branch_dsa, w_branch_mem, w_gate, b_gate, w_out, ffn2_norm, ffn2_w1, ffn2_w3, ffn2_w2, loss_target, m_ffn1_norm, m_ffn1_w1, m_ffn1_w3, m_ffn1_w2, m_mix_norm, m_mem_norm, m_w_in, m_w_mem_kv, m_qn_dsa, m_kn_dsa, m_qn_mem, m_kn_mem, m_w_branch_sb, m_w_branch_dsa, m_w_branch_mem, m_w_gate, m_b_gate, m_w_out, m_ffn2_norm, m_ffn2_w1, m_ffn2_w3, m_ffn2_w2, v_ffn1_norm, v_ffn1_w1, v_ffn1_w3, v_ffn1_w2, v_mix_norm, v_mem_norm, v_w_in, v_w_mem_kv, v_qn_dsa, v_kn_dsa, v_qn_mem, v_kn_mem, v_w_branch_sb, v_w_branch_dsa, v_w_branch_mem, v_w_gate, v_b_gate, v_w_out, v_ffn2_norm, v_ffn2_w1, v_ffn2_w3, v_ffn2_w2):
    given = dict(zip(INPUTS, (x, mem, ffn1_norm, ffn1_w1, ffn1_w3, ffn1_w2, mix_norm, mem_norm, w_in, w_mem_kv, qn_dsa, kn_dsa, qn_mem, kn_mem, w_branch_sb, w_branch_dsa, w_branch_mem, w_gate, b_gate, w_out, ffn2_norm, ffn2_w1, ffn2_w3, ffn2_w2, loss_target, m_ffn1_norm, m_ffn1_w1, m_ffn1_w3, m_ffn1_w2, m_mix_norm, m_mem_norm, m_w_in, m_w_mem_kv, m_qn_dsa, m_kn_dsa, m_qn_mem, m_kn_mem, m_w_branch_sb, m_w_branch_dsa, m_w_branch_mem, m_w_gate, m_b_gate, m_w_out, m_ffn2_norm, m_ffn2_w1, m_ffn2_w3, m_ffn2_w2, v_ffn1_norm, v_ffn1_w1, v_ffn1_w3, v_ffn1_w2, v_mix_norm, v_mem_norm, v_w_in, v_w_mem_kv, v_qn_dsa, v_kn_dsa, v_qn_mem, v_kn_mem, v_w_branch_sb, v_w_branch_dsa, v_w_branch_mem, v_w_gate, v_b_gate, v_w_out, v_ffn2_norm, v_ffn2_w1, v_ffn2_w3, v_ffn2_w2), strict=True))
    wl = {n: given[n][0] for n in BIG}
    ws = {n: given[n] for n in SMALL}

    loss, dx, recv, g = _local_step(x[0], mem[0], loss_target[0], wl, ws)

    big = [{}, {}, {}, {}]
    for n in G_FFN2 + G_MID + G_FFN1:
        outs = _adamw(recv[n], wl[n], given['m_' + n][0], given['v_' + n][0], name=f"adamw_{n}")
        for kind, t in enumerate(outs):
            big[kind][n] = t

    gs = _pack_small(g, SMALL, SMALL_PAD_ROWS)
    loss_row = gs.shape[0] - SMALL_PAD_ROWS
    gs = gs.at[loss_row, 0].set(loss)
    recv_s = _exchange([gs], _direct_phases(False), name="gather_small")[0]
    small = _adamw(recv_s, _pack_small(ws, SMALL, SMALL_PAD_ROWS), _pack_small({n: given['m_' + n] for n in SMALL}, SMALL, SMALL_PAD_ROWS),
                   _pack_small({n: given['v_' + n] for n in SMALL}, SMALL, SMALL_PAD_ROWS), name="adamw_replicated")
    total_loss = small[0][loss_row, 0]
    small = [_unpack_small(t, ws, SMALL) for t in small]

    outs = [total_loss, dx[None]]
    for kind in range(4):
        outs += [big[kind][n][None] if n in wl else small[kind][n] for n in WEIGHTS]
    return tuple(outs)
```

```python
import functools

import jax
import jax.numpy as jnp
from jax import lax
from jax.experimental import pallas as pl
from jax.experimental.pallas import tpu as pltpu

F32 = jnp.float32
BF16 = jnp.bfloat16
MXU_DT = jnp.bfloat16

N_DEV = 8
HEAD_DIM = 64
SB_HEADS = 8
DSA_GROUPS = ((128, 1), (512, 4), (2048, 16))
DSA_HPG = 4
MEM_HEADS = 4
SB_W = SB_HEADS * HEAD_DIM
DSA_W = DSA_HPG * len(DSA_GROUPS) * HEAD_DIM
DSA_OUT_W = DSA_HPG * HEAD_DIM
MEM_W = MEM_HEADS * HEAD_DIM
ROPE_THETA = 10000.0
NORM_EPS = 1e-6
QB = 128
SCALE = HEAD_DIM ** -0.5
ADAM_LR, ADAM_B1, ADAM_B2, ADAM_EPS, ADAM_WD, ADAM_STEP = 0.001, 0.9, 0.999, 1e-08, 0.01, 10

LANES = 128
VMEM_LIMIT = 48 * 1024 * 1024
SB_DEAD = -110.0 * 1.4426950408889634

WEIGHTS = ['ffn1_norm', 'ffn1_w1', 'ffn1_w3', 'ffn1_w2', 'mix_norm', 'mem_norm', 'w_in', 'w_mem_kv', 'qn_dsa', 'kn_dsa',
           'qn_mem', 'kn_mem', 'w_branch_sb', 'w_branch_dsa', 'w_branch_mem', 'w_gate', 'b_gate', 'w_out', 'ffn2_norm',
           'ffn2_w1', 'ffn2_w3', 'ffn2_w2']
SHARD_AXIS = {'ffn1_norm': None, 'ffn1_w1': 1, 'ffn1_w3': 1, 'ffn1_w2': 0, 'mix_norm': None, 'mem_norm': None, 'w_in': 1,
              'w_mem_kv': 0, 'qn_dsa': None, 'kn_dsa': None, 'qn_mem': None, 'kn_mem': None, 'w_branch_sb': 1,
              'w_branch_dsa': 1, 'w_branch_mem': 1, 'w_gate': 1, 'b_gate': None, 'w_out': 0, 'ffn2_norm': None,
              'ffn2_w1': 1, 'ffn2_w3': 1, 'ffn2_w2': 0}
BIG = [n for n in WEIGHTS if SHARD_AXIS[n] is not None]
SMALL = [n for n in WEIGHTS if SHARD_AXIS[n] is None]


def _pcall(kern, **kw):
    return pl.pallas_call(kern, **kw)


def _params(*sem):
    return pltpu.CompilerParams(dimension_semantics=sem, vmem_limit_bytes=VMEM_LIMIT)


def _dot(a, b, dims):
    return lax.dot_general(a.astype(MXU_DT), b.astype(MXU_DT), (dims, ((), ())), preferred_element_type=F32)


def _nn(a, b):
    return _dot(a, b, ((1,), (0,)))


def _nt(a, b):
    return _dot(a, b, ((1,), (1,)))


def _tn(a, b):
    return _dot(a, b, ((0,), (0,)))


def _pick(n, prefs):
    for p in prefs:
        if n % p == 0:
            return p
    return n


def _matmul(a, b, *, name, ta=False, tb=False, out_dtype=F32, res=None, alpha=1.0, tm=1024, tn=512, tk=1024, pair2=None,
            epilogue=None, side=None):
    if ta:
        kdim, m = a.shape
    else:
        m, kdim = a.shape
    n = b.shape[0] if tb else b.shape[1]
    tm = _pick(m, (tm, 512, 256, 128))
    tn = _pick(n, (tn, 512, 384, 256, 128))
    tk = _pick(kdim, (tk, 1024, 512, 256, 128))
    nk = kdim // tk
    a_spec = pl.BlockSpec((tk, tm), lambda i, j, k: (k, i)) if ta else pl.BlockSpec((tm, tk), lambda i, j, k: (i, k))
    b_spec = pl.BlockSpec((tn, tk), lambda i, j, k: (j, k)) if tb else pl.BlockSpec((tk, tn), lambda i, j, k: (k, j))
    o_spec = pl.BlockSpec((tm, tn), lambda i, j, k: (i, j))
    v_spec = pl.BlockSpec((1, tn), lambda i, j, k: (0, j))
    dims = ((0 if ta else 1,), (1 if tb else 0,))
    n_mm = 2 if pair2 is None else 4
    if epilogue is None:
        row_ins, vec_ins = ([] if res is None else [res]), []
        out_dtypes, n_vec = [out_dtype], 0
    else:
        assert tn == n and res is None
        epi_fn, row_ins, vec_ins, out_dtypes, n_vec = epilogue
    n_row_out = len(out_dtypes)

    def kern(*refs):
        refs = list(refs)
        acc_ref = refs.pop() if nk > 1 else None
        mm = refs[:n_mm]
        extra = refs[n_mm:n_mm + len(row_ins) + len(vec_ins)]
        outs = refs[n_mm + len(extra):]
        i = pl.program_id(0)
        k = pl.program_id(2)

        def product():
            part = _dot(mm[0][...], mm[1][...], dims)
            if pair2 is not None:
                part = part + _dot(mm[2][...], mm[3][...], dims)
            return part

        def finish(r):
            if alpha != 1.0:
                r = r * alpha
            if epilogue is None:
                if extra:
                    r = extra[0][...] + r
                outs[0][...] = r.astype(out_dtype)
                return
            vals = epi_fn(r, *[e[...] for e in extra])
            for o_ref, v in zip(outs[:n_row_out], vals[:n_row_out]):
                o_ref[...] = v.astype(o_ref.dtype)
            for o_ref, v in zip(outs[n_row_out:], vals[n_row_out:]):
                @pl.when(i == 0)
                def _():
                    o_ref[...] = jnp.zeros_like(o_ref)

                o_ref[...] += v

        if nk == 1:
            finish(product())
            return

        @pl.when(k == 0)
        def _():
            acc_ref[...] = jnp.zeros_like(acc_ref)

        acc_ref[...] += product()

        @pl.when(k == nk - 1)
        def _():
            finish(acc_ref[...])

    ins = [a, b] + ([] if pair2 is None else list(pair2)) + list(row_ins) + list(vec_ins)
    specs = [a_spec, b_spec] * (n_mm // 2) + [o_spec] * len(row_ins) + [v_spec] * len(vec_ins)
    out_specs = [o_spec] * n_row_out + [v_spec] * n_vec
    out_shape = [jax.ShapeDtypeStruct((m, n), dt) for dt in out_dtypes] + [jax.ShapeDtypeStruct((1, n), F32)] * n_vec
    outs = _call_2d(kern, name=name, grid=(m // tm, n // tn, nk), in_specs=specs, out_specs=out_specs, out_shape=out_shape,
                    ins=ins, scratch_shapes=[pltpu.VMEM((tm, tn), F32)] if nk > 1 else [],
                    semantics=("arbitrary" if n_vec else "parallel", "parallel", "arbitrary"), side=side)
    carried = None
    if side is not None:
        outs, carried = outs
    outs = outs[0] if epilogue is None else outs
    return outs if side is None else (outs, carried)


def _epi_residual_rms(r, res, gain):
    xn = res + r
    return xn, xn * lax.rsqrt(jnp.mean(xn * xn, axis=-1, keepdims=True) + NORM_EPS) * gain


def _epi_rms_bwd(r, x, dres, gain):
    rs = lax.rsqrt(jnp.mean(x * x, axis=-1, keepdims=True) + NORM_EPS)
    xh = x * rs
    dy = r * gain
    dx = dres + rs * (dy - xh * jnp.mean(dy * xh, axis=-1, keepdims=True))
    return dx, dx, jnp.sum(r * xh, axis=0, keepdims=True)


def _epi_rms_bwd_sum(r, r0, x, dres, gain):
    return _epi_rms_bwd(r + r0, x, dres, gain)


def _epi_loss(r, res, target):
    e = (res + r) - target
    dy = e / e.shape[-1]
    return dy, dy, jnp.sum(e * e, axis=0, keepdims=True)
def _rms_fwd(x, g, *, name, side=None):
    s, d = x.shape
    ts = _pick(s, (512, 256))

    def kern(x_ref, g_ref, h_ref):
        xf = x_ref[...]
        r = lax.rsqrt(jnp.mean(xf * xf, axis=-1, keepdims=True) + NORM_EPS)
        h_ref[...] = (xf * r * g_ref[...]).astype(h_ref.dtype)

    outs = _call_2d(kern, name=name, grid=(s // ts,),
                    in_specs=[pl.BlockSpec((ts, d), lambda i: (i, 0)), pl.BlockSpec((1, d), lambda i: (0, 0))],
                    out_specs=[pl.BlockSpec((ts, d), lambda i: (i, 0))], out_shape=[jax.ShapeDtypeStruct((s, d), BF16)],
                    ins=[x, g], semantics=("parallel",), side=side)
    return outs[0] if side is None else (outs[0][0], outs[1])


def _rms_bwd(x, g, dh, res, *, name):
    s, d = x.shape
    ts = _pick(s, (512, 256))

    def kern(*refs):
        if res is None:
            x_ref, g_ref, dh_ref, dx_ref, dxb_ref, dg_ref = refs
            r_ref = None
        else:
            x_ref, g_ref, dh_ref, r_ref, dx_ref, dxb_ref, dg_ref = refs
        xf = x_ref[...]
        r = lax.rsqrt(jnp.mean(xf * xf, axis=-1, keepdims=True) + NORM_EPS)
        xh = xf * r
        dhf = dh_ref[...].astype(F32)
        dy = dhf * g_ref[...]
        dx = r * (dy - xh * jnp.mean(dy * xh, axis=-1, keepdims=True))
        if r_ref is not None:
            dx = r_ref[...] + dx
        dx_ref[...] = dx
        dxb_ref[...] = dx.astype(dxb_ref.dtype)

        @pl.when(pl.program_id(0) == 0)
        def _():
            dg_ref[...] = jnp.zeros_like(dg_ref)

        dg_ref[...] += jnp.sum(dhf * xh, axis=0, keepdims=True)

    row = pl.BlockSpec((ts, d), lambda i: (i, 0))
    vec = pl.BlockSpec((1, d), lambda i: (0, 0))
    ins = [x, g, dh] + ([] if res is None else [res])
    return _pcall(kern, name=name, grid=(s // ts,), in_specs=[row, vec, row] + ([] if res is None else [row]),
                  out_specs=[row, row, vec],
                  out_shape=[jax.ShapeDtypeStruct((s, d), F32), jax.ShapeDtypeStruct((s, d), BF16), jax.ShapeDtypeStruct((1, d), F32)],
                  compiler_params=_params("arbitrary"))(*ins)


def _sigmoid(x):
    return 1.0 / (1.0 + jnp.exp(-x))


FFN_TM, FFN_TF = 512, 1408


def _ffn_up(h, w1, w3, *, name, side=None):
    s, d = h.shape
    fdim = w1.shape[1]
    tm, tf = _pick(s, (FFN_TM, 256)), _pick(fdim, (FFN_TF, 512, 256, 128))

    def kern(h_ref, w1_ref, w3_ref, a_ref, b_ref, f_ref):
        hb = h_ref[...]
        a = _nn(hb, w1_ref[...])
        b = _nn(hb, w3_ref[...])
        a_ref[...] = a.astype(a_ref.dtype)
        b_ref[...] = b.astype(b_ref.dtype)
        f_ref[...] = (a * _sigmoid(a) * b).astype(f_ref.dtype)

    wspec = pl.BlockSpec((d, tf), lambda i, j: (0, j))
    ospec = pl.BlockSpec((tm, tf), lambda i, j: (i, j))
    shp = jax.ShapeDtypeStruct((s, fdim), BF16)
    return _call_2d(kern, name=name, grid=(s // tm, fdim // tf), in_specs=[pl.BlockSpec((tm, d), lambda i, j: (i, 0)), wspec, wspec],
                    out_specs=[ospec, ospec, ospec], out_shape=[shp, shp, shp], ins=[h, w1, w3],
                    semantics=("parallel", "parallel"), side=side)


def _ffn_dact(dy, w2, a, b, *, name, side=None):
    s, d = dy.shape
    fdim = w2.shape[0]
    tm, tf = _pick(s, (FFN_TM, 256)), _pick(fdim, (FFN_TF, 512, 256, 128))

    def kern(dy_ref, w2_ref, a_ref, b_ref, da_ref, db_ref):
        df = _nt(dy_ref[...], w2_ref[...]) * 0.5
        av = a_ref[...].astype(F32)
        sg = _sigmoid(av)
        da_ref[...] = (df * b_ref[...].astype(F32) * (sg + av * sg * (1.0 - sg))).astype(da_ref.dtype)
        db_ref[...] = (df * (av * sg)).astype(db_ref.dtype)

    ospec = pl.BlockSpec((tm, tf), lambda i, j: (i, j))
    shp = jax.ShapeDtypeStruct((s, fdim), BF16)
    return _call_2d(kern, name=name, grid=(s // tm, fdim // tf),
                    in_specs=[pl.BlockSpec((tm, d), lambda i, j: (i, 0)), pl.BlockSpec((tf, d), lambda i, j: (j, 0)), ospec, ospec],
                    out_specs=[ospec, ospec], out_shape=[shp, shp], ins=[dy, w2, a, b], semantics=("parallel", "parallel"), side=side)


def _head_mean(v, bd):
    outs = []
    for c in range(v.shape[1] // LANES):
        x = v[:, c * LANES:(c + 1) * LANES]
        hi = x.astype(BF16)
        lo = (x - hi.astype(F32)).astype(BF16)
        outs.append(lax.dot_general(jnp.concatenate([hi, lo], axis=1), bd, (((1,), (0,)), ((), ())), preferred_element_type=F32))
    return outs[0] if len(outs) == 1 else jnp.concatenate(outs, axis=1)


def _partner(v):
    w = v.shape[1]
    lane = lax.broadcasted_iota(jnp.int32, v.shape, 1)
    return jnp.where(lane % HEAD_DIM < HEAD_DIM // 2, pltpu.roll(v, w - HEAD_DIM // 2, 1), pltpu.roll(v, HEAD_DIM // 2, 1))


def _block_diag(w=None):
    r = (lax.broadcasted_iota(jnp.int32, (2 * LANES, LANES), 0) % LANES) // HEAD_DIM
    c = lax.broadcasted_iota(jnp.int32, (2 * LANES, LANES), 1) // HEAD_DIM
    return jnp.where(r == c, 1.0 / HEAD_DIM, 0.0).astype(BF16)


def _rope_tables(s):
    half = HEAD_DIM // 2
    inv_freq = jnp.power(ROPE_THETA, -jnp.arange(half, dtype=F32) / half)
    ang = jnp.arange(s).astype(F32)[:, None] * inv_freq[None, :]
    cos, sin = lax.optimization_barrier((jnp.cos(ang), jnp.sin(ang)))
    cos2 = jnp.concatenate([cos, cos, cos, cos], axis=1)
    sin2 = jnp.concatenate([-sin, sin, -sin, sin], axis=1)
    return cos2, sin2


def _qknorm_fwd(src, col0, width, gain, rope, *, name, out_dtype=BF16):
    s = src.shape[0]
    ts = _pick(s, (1024, 512, 256))
    cb = col0 // width
    assert col0 % width == 0
    reps = width // LANES
    g = jnp.tile(gain, (1, width // HEAD_DIM))

    def kern(*refs):
        if rope is None:
            x_ref, g_ref, o_ref = refs
        else:
            x_ref, g_ref, c_ref, s_ref, o_ref = refs
        x = x_ref[...].astype(F32)
        bd = _block_diag(width)
        r = lax.rsqrt(_head_mean(x * x, bd) + NORM_EPS)
        y = x * r * g_ref[...]
        if rope is not None:
            y = y * jnp.tile(c_ref[...], (1, reps)) + _partner(y) * jnp.tile(s_ref[...], (1, reps))
        o_ref[...] = y.astype(o_ref.dtype)

    xs = pl.BlockSpec((ts, width), lambda i: (i, cb))
    tab = pl.BlockSpec((ts, LANES), lambda i: (i, 0))
    ins = [src, g] + ([] if rope is None else list(rope))
    specs = [xs, pl.BlockSpec((1, width), lambda i: (0, 0))] + ([] if rope is None else [tab, tab])
    return _pcall(kern, name=name, grid=(s // ts,), in_specs=specs, out_specs=pl.BlockSpec((ts, width), lambda i: (i, 0)),
                  out_shape=jax.ShapeDtypeStruct((s, width), out_dtype), compiler_params=_params("parallel"))(*ins)


def _qknorm_bwd(src, col0, width, gain, rope, dout, *, name):
    s = src.shape[0]
    ts = _pick(s, (1024, 512, 256))
    cb = col0 // width
    reps = width // LANES
    g = jnp.tile(gain, (1, width // HEAD_DIM))

    douts = list(dout) if isinstance(dout, (list, tuple)) else [dout]
    piece = width // len(douts)

    def kern(*refs):
        refs = list(refs)
        dg_ref = refs.pop()
        dx_ref = refs.pop()
        do_refs = [refs.pop() for _ in douts][::-1]
        if rope is None:
            x_ref, g_ref = refs
        else:
            x_ref, g_ref, c_ref, s_ref = refs
        x = x_ref[...].astype(F32)
        bd = _block_diag(width)
        r = lax.rsqrt(_head_mean(x * x, bd) + NORM_EPS)
        xh = x * r
        dy = jnp.concatenate([d[...].astype(F32) for d in do_refs], axis=1) if len(do_refs) > 1 else do_refs[0][...].astype(F32)
        if rope is not None:
            dy = dy * jnp.tile(c_ref[...], (1, reps)) + _partner(dy * jnp.tile(s_ref[...], (1, reps)))
        dxh = dy * g_ref[...]
        dx_ref[...] = (r * (dxh - xh * _head_mean(dxh * xh, bd))).astype(dx_ref.dtype)

        @pl.when(pl.program_id(0) == 0)
        def _():
            dg_ref[...] = jnp.zeros_like(dg_ref)

        dg_ref[...] += jnp.sum(dy * xh, axis=0, keepdims=True)

    xs = pl.BlockSpec((ts, width), lambda i: (i, cb))
    row = pl.BlockSpec((ts, width), lambda i: (i, 0))
    vec = pl.BlockSpec((1, width), lambda i: (0, 0))
    tab = pl.BlockSpec((ts, LANES), lambda i: (i, 0))
    ins = [src, g] + ([] if rope is None else list(rope)) + douts
    specs = [xs, vec] + ([] if rope is None else [tab, tab]) + [pl.BlockSpec((ts, piece), lambda i: (i, 0))] * len(douts)
    dx, dg = _pcall(kern, name=name, grid=(s // ts,), in_specs=specs, out_specs=[row, vec],
                    out_shape=[jax.ShapeDtypeStruct((s, width), BF16), jax.ShapeDtypeStruct((1, width), F32)],
                    compiler_params=_params("arbitrary"))(*ins)
    return dx, jnp.sum(dg.reshape(width // HEAD_DIM, HEAD_DIM), axis=0, keepdims=True)


def _tri(strict, n):
    r = lax.broadcasted_iota(jnp.int32, (2 * n, n), 0) % n
    c = lax.broadcasted_iota(jnp.int32, (2 * n, n), 1)
    return jnp.where((r > c) if strict else (r >= c), 1.0, 0.0).astype(BF16)


def _split_dot(v, t2):
    hi = v.astype(BF16)
    lo = (v - hi.astype(F32)).astype(BF16)
    return lax.dot_general(jnp.concatenate([hi, lo], axis=1), t2, (((1,), (0,)), ((), ())), preferred_element_type=F32)


LOG2E = 1.4426950408889634


def _log2_sigmoids(z2):
    lf = -(jnp.maximum(z2, 0.0) + jnp.log2(1.0 + jnp.exp2(-jnp.abs(z2))))
    return z2 + lf, lf


SB2_SUB = 2
SB_KT = 128


def _first_half(shape):
    return lax.broadcasted_iota(jnp.int32, shape, 1) < HEAD_DIM


def _split_pair(t, first):
    zero = jnp.zeros_like(t)
    return [jnp.where(first, t, zero), jnp.where(first, zero, t)]


def _sb2_fwd(p, *, name, side=None):
    s = p.shape[0]
    rq = SB2_SUB * QB
    nq = s // rq
    npair = SB_W // LANES

    def kern(q_ref, k_ref, v_ref, o_ref):
        i = pl.program_id(1)
        first = _first_half((rq, LANES))
        q2 = jnp.concatenate(_split_pair(q_ref[...], first), axis=0)
        t2 = _tri(True, SB_KT)
        rel = lax.broadcasted_iota(jnp.int32, (2 * rq, SB_KT), 1) - lax.broadcasted_iota(jnp.int32, (2 * rq, SB_KT), 0) % rq

        def tile(j, q, rel, carry, acc, masked):
            off = pl.multiple_of(j * SB_KT, SB_KT)
            ls, lf = _log2_sigmoids(_nt(q, k_ref[pl.ds(off, SB_KT), :]) * (SCALE * LOG2E))
            if masked:
                before = rel < i * rq - j * SB_KT
                lf = jnp.where(before, lf, 0.0)
            w = jnp.exp2(ls + _split_dot(lf, t2) + carry)
            if masked:
                w = jnp.where(before, w, 0.0)
            return carry + jnp.sum(lf, axis=1, keepdims=True), acc + _nn(w, v_ref[pl.ds(off, SB_KT), :])

        carry, acc = jnp.zeros((2 * rq, 1), F32), jnp.zeros((2 * rq, LANES), F32)
        for a in range(rq // SB_KT):
            carry, acc = tile(i * (rq // SB_KT) + (rq // SB_KT - 1 - a), q2, rel, carry, acc, True)

        def cond(st):
            return jnp.logical_and(st[0] >= 0, st[1] > 0)

        def body(st):
            carry, acc = tile(st[0], q2, rel, st[2], st[3], False)
            return st[0] - 1, (jnp.max(carry) > SB_DEAD).astype(jnp.int32), carry, acc

        st = lax.while_loop(cond, body, (i * (rq // SB_KT) - 1, jnp.int32(1), carry, acc))
        o_ref[...] = jnp.where(first, st[3][:rq], st[3][rq:])

    outs = _call_2d(kern, name=name, grid=(npair, nq),
                    in_specs=[pl.BlockSpec((rq, LANES), lambda a, i: (i, a)), pl.BlockSpec((s, LANES), lambda a, i: (0, npair + a)),
                              pl.BlockSpec((s, LANES), lambda a, i: (0, 2 * npair + a))],
                    out_specs=[pl.BlockSpec((rq, LANES), lambda a, i: (i, a))], out_shape=[jax.ShapeDtypeStruct((s, SB_W), F32)],
                    ins=[p, p, p], semantics=("parallel", "arbitrary"), side=side)
    return outs[0] if side is None else (outs[0][0], outs[1])


def _sb2_bwd(p, o, do, *, name, side=None):
    s = p.shape[0]
    rq = SB2_SUB * QB
    nq = s // rq
    npair = SB_W // LANES

    def kern(q_ref, k_ref, v_ref, o_ref, do_ref, dq_ref, dk_hbm, dv_hbm, dk_acc, dv_acc, sem):
        pr = pl.program_id(0)
        i = pl.program_id(1)

        @pl.when(i == 0)
        def _():
            dk_acc[...] = jnp.zeros_like(dk_acc)
            dv_acc[...] = jnp.zeros_like(dv_acc)

        first = _first_half((rq, LANES))
        q2 = jnp.concatenate(_split_pair(q_ref[...], first), axis=0)
        do2 = jnp.concatenate(_split_pair(do_ref[...], first), axis=0)
        o2 = o_ref[...]
        dsum = jnp.sum(do2.astype(F32) * jnp.concatenate([o2, o2], axis=0), axis=1, keepdims=True)
        t_strict = _tri(True, SB_KT)
        t_incl = _tri(False, SB_KT)
        rel = lax.broadcasted_iota(jnp.int32, (2 * rq, SB_KT), 1) - lax.broadcasted_iota(jnp.int32, (2 * rq, SB_KT), 0) % rq

        def tile(j, rows, carry, gcarry, dq, masked):
            q, dob, dsm, rel = rows
            off = pl.multiple_of(j * SB_KT, SB_KT)
            kt = k_ref[pl.ds(off, SB_KT), :]
            ls, lf = _log2_sigmoids(_nt(q, kt) * (SCALE * LOG2E))
            if masked:
                before = rel < i * rq - j * SB_KT
                lf = jnp.where(before, lf, 0.0)
            w = jnp.exp2(ls + _split_dot(lf, t_strict) + carry)
            if masked:
                w = jnp.where(before, w, 0.0)
            wr = w.astype(MXU_DT)
            g = _nt(dob, v_ref[pl.ds(off, SB_KT), :]) * wr.astype(F32)
            big_g = dsm - (_split_dot(g, t_incl) + gcarry)
            sig = jnp.exp2(ls)
            dz = g * (1.0 - sig) - sig * big_g
            if masked:
                dz = jnp.where(before, dz, 0.0)
            dz = dz * SCALE
            dk_acc[pl.ds(off, SB_KT), :] += _tn(dz, q)
            dv_acc[pl.ds(off, SB_KT), :] += _tn(wr, dob)
            return (carry + jnp.sum(lf, axis=1, keepdims=True), gcarry + jnp.sum(g, axis=1, keepdims=True),
                    dq + _nn(dz, kt))

        zc = jnp.zeros((2 * rq, 1), F32)
        carry, gcarry, dq = zc, zc, jnp.zeros((2 * rq, LANES), F32)
        whole = (q2, do2, dsum, rel)
        for a in range(rq // SB_KT):
            carry, gcarry, dq = tile(i * (rq // SB_KT) + (rq // SB_KT - 1 - a), whole, carry, gcarry, dq, True)

        def cond(st):
            return jnp.logical_and(st[0] >= 0, st[1] > 0)

        def body(st):
            carry, gcarry, dq = tile(st[0], whole, st[2], st[3], st[4], False)
            return st[0] - 1, (jnp.max(carry) > SB_DEAD).astype(jnp.int32), carry, gcarry, dq

        st = lax.while_loop(cond, body, (i * (rq // SB_KT) - 1, jnp.int32(1), carry, gcarry, dq))
        dq_ref[...] = jnp.where(first, st[4][:rq], st[4][rq:]).astype(dq_ref.dtype)

        @pl.when(i == nq - 1)
        def _():
            cols = pl.ds(pl.multiple_of(pr * LANES, LANES), LANES)
            ck = pltpu.make_async_copy(dk_acc, dk_hbm.at[:, cols], sem.at[0])
            cv = pltpu.make_async_copy(dv_acc, dv_hbm.at[:, cols], sem.at[1])
            ck.start()
            cv.start()
            ck.wait()
            cv.wait()

    blk = pl.BlockSpec((rq, LANES), lambda a, i: (i, a))
    anyspace = pl.BlockSpec(memory_space=pl.ANY)
    shp = jax.ShapeDtypeStruct((s, SB_W), F32)
    return _call_2d(kern, name=name, grid=(npair, nq),
                    in_specs=[blk, pl.BlockSpec((s, LANES), lambda a, i: (0, npair + a)),
                              pl.BlockSpec((s, LANES), lambda a, i: (0, 2 * npair + a)), blk, blk],
                    out_specs=[blk, anyspace, anyspace], out_shape=[jax.ShapeDtypeStruct((s, SB_W), BF16), shp, shp], ins=[p, p, p, o, do],
                    scratch_shapes=[pltpu.VMEM((s, LANES), F32), pltpu.VMEM((s, LANES), F32), pltpu.SemaphoreType.DMA((2,))],
                    semantics=("arbitrary", "arbitrary"), side=side)


def _dsa_rel():
    qi = lax.broadcasted_iota(jnp.int32, (QB, QB), 0)
    kj = lax.broadcasted_iota(jnp.int32, (QB, QB), 1)
    return kj - qi


def _prev_mask(rel, has_prev):
    return rel >= jnp.where(has_prev, 0, QB)


DSA_BT = QB * max(r for _, r in DSA_GROUPS)
def _units_per_batch(r):
    return 8 if r < max(d for _, d in DSA_GROUPS) else 4


def _bdot(a, b, ca, cb):
    return lax.dot_general(a.astype(MXU_DT), b.astype(MXU_DT), (((ca,), (cb,)), ((0,), (0,))), preferred_element_type=F32)


def _bnt(a, b):
    return _bdot(a, b, 2, 2)


def _bnn(a, b):
    return _bdot(a, b, 2, 1)


def _btn(a, b):
    return _bdot(a, b, 1, 1)


def _unit_rows(r, c, b):
    return pl.ds(c + QB * r * b, QB, stride=r)


def _pair_cols(t, first):
    return [jnp.max(jnp.where(first, t, -jnp.inf), axis=1, keepdims=True),
            jnp.max(jnp.where(first, -jnp.inf, t), axis=1, keepdims=True)]


def _dsa2_fwd(qn, kn, v32, g, *, name):
    s = qn.shape[0]
    r = DSA_GROUPS[g][1]
    nbk = DSA_BT // (QB * r)
    npair = DSA_OUT_W // LANES

    def kern(q_ref, k_ref, kp_ref, v_ref, vp_ref, o_ref, l_ref):
        t = pl.program_id(1)
        first = _first_half((QB, LANES))
        rel = _dsa_rel()
        units = [(c, b) for c in range(r) for b in range(nbk)]
        ub = _units_per_batch(r)
        for u0 in range(0, len(units), ub):
            batch = units[u0:u0 + ub]
            qs, kcs, vcs, kps, vps, masks = [], [], [], [], [], []
            for c, b in batch:
                rows = _unit_rows(r, c, b)
                kc, vc = k_ref[rows, :].astype(MXU_DT), v_ref[rows, :].astype(MXU_DT)
                if b > 0:
                    prow = _unit_rows(r, c, b - 1)
                    kpv, vpv, has_prev = k_ref[prow, :], v_ref[prow, :], True
                else:
                    prow = _unit_rows(r, c, nbk - 1)
                    kpv, vpv, has_prev = kp_ref[prow, :], vp_ref[prow, :], t > 0
                for qe in _split_pair(q_ref[rows, :], first):
                    qs.append(qe.astype(MXU_DT))
                    kcs.append(kc)
                    vcs.append(vc)
                    kps.append(kpv.astype(MXU_DT))
                    vps.append(vpv.astype(MXU_DT))
                    masks.append(_prev_mask(rel, has_prev))
            qq = jnp.stack(qs)
            sc = jnp.where(rel <= 0, _bnt(qq, jnp.stack(kcs)) * SCALE, -jnp.inf)
            sp = _bnt(qq, jnp.stack(kps)) * SCALE
            sp = jnp.stack([jnp.where(mk, sp[n], -jnp.inf) for n, mk in enumerate(masks)])
            m = jnp.maximum(jnp.max(sc, axis=2, keepdims=True), jnp.max(sp, axis=2, keepdims=True))
            pc = jnp.exp(sc - m)
            pp = jnp.exp(sp - m)
            den = jnp.sum(pc, axis=2, keepdims=True) + jnp.sum(pp, axis=2, keepdims=True)
            out = (_bnn(pc, jnp.stack(vcs)) + _bnn(pp, jnp.stack(vps))) / den
            lse = m + jnp.log(den)
            for idx, (c, b) in enumerate(batch):
                rows = _unit_rows(r, c, b)
                o_ref[rows, :] = jnp.where(first, out[2 * idx], out[2 * idx + 1])
                l_ref[rows, :] = jnp.where(first, lse[2 * idx], lse[2 * idx + 1])

    npg = DSA_HPG * HEAD_DIM // LANES
    cur = pl.BlockSpec((DSA_BT, LANES), lambda a, t: (t, npg * g + a))
    prev = pl.BlockSpec((DSA_BT, LANES), lambda a, t: (jnp.maximum(t - 1, 0), npg * g + a))
    out = pl.BlockSpec((DSA_BT, LANES), lambda a, t: (t, a))
    shp = jax.ShapeDtypeStruct((s, DSA_OUT_W), F32)
    return _pcall(kern, name=name, grid=(npair, s // DSA_BT), in_specs=[cur, cur, prev, cur, prev], out_specs=[out, out],
                  out_shape=[shp, shp], compiler_params=_params("parallel", "parallel"))(qn, kn, kn, v32, v32)


def _dsa2_combine(parts, *, name):
    s, wd = parts[0][0].shape
    ts = _pick(s, (2048, 1024, 512, 256))

    def kern(o0, l0, o1, l1, o2, l2, o_ref, l_ref):
        ls = [l0[...], l1[...], l2[...]]
        m = jnp.maximum(jnp.maximum(ls[0], ls[1]), ls[2])
        es = [jnp.exp(l - m) for l in ls]
        den = es[0] + es[1] + es[2]
        o_ref[...] = (es[0] * o0[...] + es[1] * o1[...] + es[2] * o2[...]) / den
        l_ref[...] = m + jnp.log(den)

    blk = pl.BlockSpec((ts, wd), lambda i: (i, 0))
    shp = jax.ShapeDtypeStruct((s, wd), F32)
    flat = [t for pair in parts for t in pair]
    return _pcall(kern, name=name, grid=(s // ts,), in_specs=[blk] * 6, out_specs=[blk, blk], out_shape=[shp, shp],
                  compiler_params=_params("parallel"))(*flat)


def _dsa2_prep(o, do, *, name):
    s, wd = o.shape
    ts = _pick(s, (2048, 1024, 512, 256))

    def kern(o_ref, do_ref, d_ref):
        d_ref[...] = _head_mean(do_ref[...] * o_ref[...], _block_diag(wd)) * HEAD_DIM

    blk = pl.BlockSpec((ts, wd), lambda i: (i, 0))
    return _pcall(kern, name=name, grid=(s // ts,), in_specs=[blk, blk], out_specs=blk,
                  out_shape=jax.ShapeDtypeStruct((s, wd), F32), compiler_params=_params("parallel"))(o, do)


def _dsa2_bwd(qn, kn, v32, do, lse, dd, g, *, name):
    s = qn.shape[0]
    r = DSA_GROUPS[g][1]
    nbk = DSA_BT // (QB * r)
    npair = DSA_OUT_W // LANES
    nsteps = s // DSA_BT

    def kern(q_ref, qn_ref, k_ref, kp_ref, v_ref, vp_ref, do_ref, don_ref, l_ref, ln_ref, d_ref, dn_ref,
             dq_ref, dk_ref, dv_ref):
        t = pl.program_id(1)
        first = _first_half((QB, LANES))
        rel = _dsa_rel()

        def pairs(items):
            qq = jnp.stack([it[0].astype(MXU_DT) for it in items])
            dd = jnp.stack([it[1].astype(MXU_DT) for it in items])
            kk = jnp.stack([it[4].astype(MXU_DT) for it in items])
            vv = jnp.stack([it[5].astype(MXU_DT) for it in items])
            p = jnp.exp(_bnt(qq, kk) * SCALE - jnp.stack([it[2] for it in items]))
            p = jnp.stack([jnp.where(it[6], p[n], 0.0) for n, it in enumerate(items)])
            ds = p * (_bnt(dd, vv) - jnp.stack([it[3] for it in items])) * SCALE
            return _bnn(ds, kk), _btn(ds, qq), _btn(p, dd)

        def heads(rows, qr, dor, lr, dr):
            return list(zip(_split_pair(qr[rows, :], first), _split_pair(dor[rows, :], first),
                            _pair_cols(lr[rows, :], first), _pair_cols(dr[rows, :], first)))

        units = [(c, b) for c in range(r) for b in range(nbk)]
        dk_of, dv_of = [None] * len(units), [None] * len(units)
        ub = _units_per_batch(r)
        for u0 in range(0, len(units), ub // 2):
            batch = list(enumerate(units))[u0:u0 + ub // 2]
            items = []
            for u, (c, b) in batch:
                rows = _unit_rows(r, c, b)
                kc, vc = k_ref[rows, :], v_ref[rows, :]
                if b > 0:
                    prow = _unit_rows(r, c, b - 1)
                    kpv, vpv, pmask = k_ref[prow, :], v_ref[prow, :], _prev_mask(rel, True)
                else:
                    prow = _unit_rows(r, c, nbk - 1)
                    kpv, vpv, pmask = kp_ref[prow, :], vp_ref[prow, :], _prev_mask(rel, t > 0)
                for hd in heads(rows, q_ref, do_ref, l_ref, d_ref):
                    items.append(hd + (kc, vc, rel <= 0))
                    items.append(hd + (kpv, vpv, pmask))
            dq, dk, dv = pairs(items)
            for n, (u, (c, b)) in enumerate(batch):
                dq_ref[_unit_rows(r, c, b), :] = jnp.where(first, dq[4 * n] + dq[4 * n + 1], dq[4 * n + 2] + dq[4 * n + 3])
                dk_of[u] = dk[4 * n] + dk[4 * n + 2]
                dv_of[u] = dv[4 * n] + dv[4 * n + 2]
                if b > 0:
                    dk_of[u - 1] = dk_of[u - 1] + (dk[4 * n + 1] + dk[4 * n + 3])
                    dv_of[u - 1] = dv_of[u - 1] + (dv[4 * n + 1] + dv[4 * n + 3])
        lasts = [c * nbk + nbk - 1 for c in range(r)]
        for c0 in range(0, r, 4):
            chunk = list(range(c0, min(c0 + 4, r)))
            items = []
            for c in chunk:
                last = _unit_rows(r, c, nbk - 1)
                for hd in heads(_unit_rows(r, c, 0), qn_ref, don_ref, ln_ref, dn_ref):
                    items.append(hd + (k_ref[last, :], v_ref[last, :], _prev_mask(rel, t < nsteps - 1)))
            _, dk, dv = pairs(items)
            for n, c in enumerate(chunk):
                dk_of[lasts[c]] = dk_of[lasts[c]] + (dk[2 * n] + dk[2 * n + 1])
                dv_of[lasts[c]] = dv_of[lasts[c]] + (dv[2 * n] + dv[2 * n + 1])
        for u, (c, b) in enumerate(units):
            dk_ref[_unit_rows(r, c, b), :] = dk_of[u]
            dv_ref[_unit_rows(r, c, b), :] = dv_of[u]

    npg = DSA_HPG * HEAD_DIM // LANES

    def at(shift, col):
        return pl.BlockSpec((DSA_BT, LANES), lambda a, t: (jnp.clip(t + shift, 0, nsteps - 1), col(a)))

    gcol = lambda a: npg * g + a
    ocol = lambda a: a
    specs = [at(0, gcol), at(1, gcol), at(0, gcol), at(-1, gcol), at(0, gcol), at(-1, gcol),
             at(0, ocol), at(1, ocol), at(0, ocol), at(1, ocol), at(0, ocol), at(1, ocol)]
    shp = jax.ShapeDtypeStruct((s, DSA_OUT_W), F32)
    return _pcall(kern, name=name, grid=(npair, nsteps), in_specs=specs, out_specs=[at(0, ocol)] * 3, out_shape=[shp, shp, shp],
                  compiler_params=_params("parallel", "parallel"))(qn, qn, kn, kn, v32, v32, do, do, lse, lse, dd, dd)


def _mem2_fwd(qn, km, kv, *, name):
    s = qn.shape[0]
    ml = km.shape[0]
    tq = _pick(s, (2048, 1024, 512, 256))
    npair = MEM_W // LANES

    def kern(q_ref, k_ref, v_ref, o_ref):
        first = _first_half((tq, LANES))
        q2 = jnp.concatenate(_split_pair(q_ref[...], first), axis=0)
        sc = _nt(q2, k_ref[...]) * SCALE
        e = jnp.exp(sc - jnp.max(sc, axis=1, keepdims=True))
        o2 = _nn(e / jnp.sum(e, axis=1, keepdims=True), v_ref[...])
        o_ref[...] = jnp.where(first, o2[:tq], o2[tq:])

    blk = pl.BlockSpec((tq, LANES), lambda a, i: (i, a))
    return _pcall(kern, name=name, grid=(npair, s // tq),
                  in_specs=[blk, pl.BlockSpec((ml, LANES), lambda a, i: (0, a)), pl.BlockSpec((ml, LANES), lambda a, i: (0, npair + a))],
                  out_specs=blk, out_shape=jax.ShapeDtypeStruct((s, MEM_W), F32),
                  compiler_params=_params("parallel", "parallel"))(qn, km, kv)


def _mem2_bwd(qn, km, kv, do, *, name):
    s = qn.shape[0]
    ml = km.shape[0]
    tq = _pick(s, (2048, 1024, 512, 256))
    npair = MEM_W // LANES

    def kern(q_ref, k_ref, v_ref, do_ref, dq_ref, dk_ref, dv_ref):
        @pl.when(pl.program_id(1) == 0)
        def _():
            dk_ref[...] = jnp.zeros_like(dk_ref)
            dv_ref[...] = jnp.zeros_like(dv_ref)

        first = _first_half((tq, LANES))
        q2 = jnp.concatenate(_split_pair(q_ref[...], first), axis=0)
        do2 = jnp.concatenate(_split_pair(do_ref[...], first), axis=0)
        sc = _nt(q2, k_ref[...]) * SCALE
        e = jnp.exp(sc - jnp.max(sc, axis=1, keepdims=True))
        p = e / jnp.sum(e, axis=1, keepdims=True)
        dp = _nt(do2, v_ref[...])
        ds = p * (dp - jnp.sum(p * dp, axis=1, keepdims=True)) * SCALE
        dq2 = _nn(ds, k_ref[...])
        dk_ref[...] += _tn(ds, q2)
        dv_ref[...] += _tn(p, do2)
        dq_ref[...] = jnp.where(first, dq2[:tq], dq2[tq:])

    blk = pl.BlockSpec((tq, LANES), lambda a, i: (i, a))
    kblk = pl.BlockSpec((ml, LANES), lambda a, i: (0, a))
    kshape = jax.ShapeDtypeStruct((ml, MEM_W), F32)
    return _pcall(kern, name=name, grid=(npair, s // tq),
                  in_specs=[blk, kblk, pl.BlockSpec((ml, LANES), lambda a, i: (0, npair + a)), blk],
                  out_specs=[blk, kblk, kblk], out_shape=[jax.ShapeDtypeStruct((s, MEM_W), F32), kshape, kshape],
                  compiler_params=_params("parallel", "arbitrary"))(qn, km, kv, do)


def _merge_fwd(logits, bias, ya, yb, yc, *, name):
    s, d = ya.shape
    ts = _pick(s, (1024, 512, 256))

    def kern(l0, l1, l2, b0, b1, b2, a_ref, b_ref, c_ref, o_ref):
        m = 0.0
        for l_ref, bb_ref, y_ref in ((l0, b0, a_ref), (l1, b1, b_ref), (l2, b2, c_ref)):
            m = m + _sigmoid(l_ref[...].astype(F32) + bb_ref[...]) * y_ref[...].astype(F32)
        o_ref[...] = m.astype(o_ref.dtype)

    row = pl.BlockSpec((ts, d), lambda i: (i, 0))
    lg = [pl.BlockSpec((ts, d), functools.partial(lambda i, c: (i, c), c=c)) for c in range(3)]
    bs = [pl.BlockSpec((1, d), functools.partial(lambda i, c: (0, c), c=c)) for c in range(3)]
    return _pcall(kern, name=name, grid=(s // ts,), in_specs=lg + bs + [row, row, row], out_specs=row,
                  out_shape=jax.ShapeDtypeStruct((s, d), BF16),
                  compiler_params=_params("parallel"))(logits, logits, logits, bias, bias, bias, ya, yb, yc)


def _merge_bwd(logits, bias, ya, yb, yc, dm, *, name):
    s, d = ya.shape
    ts = _pick(s, (512, 256))

    def kern(l0, l1, l2, b0, b1, b2, a_ref, b_ref, c_ref, dm_ref, da_ref, db_ref, dc_ref, dl_ref, dbias_ref):
        dmv = dm_ref[...].astype(F32)

        @pl.when(pl.program_id(0) == 0)
        def _():
            dbias_ref[...] = jnp.zeros_like(dbias_ref)

        for c, (l_ref, bb_ref, y_ref, dy_ref) in enumerate(((l0, b0, a_ref, da_ref), (l1, b1, b_ref, db_ref), (l2, b2, c_ref, dc_ref))):
            g = _sigmoid(l_ref[...].astype(F32) + bb_ref[...])
            dy_ref[...] = (dmv * g).astype(dy_ref.dtype)
            dl = dmv * y_ref[...].astype(F32) * g * (1.0 - g)
            dl_ref[:, c * d:(c + 1) * d] = dl.astype(dl_ref.dtype)
            dbias_ref[:, c * d:(c + 1) * d] += jnp.sum(dl, axis=0, keepdims=True)

    row = pl.BlockSpec((ts, d), lambda i: (i, 0))
    lg = [pl.BlockSpec((ts, d), functools.partial(lambda i, c: (i, c), c=c)) for c in range(3)]
    bs = [pl.BlockSpec((1, d), functools.partial(lambda i, c: (0, c), c=c)) for c in range(3)]
    yshape = jax.ShapeDtypeStruct((s, d), BF16)
    return _pcall(kern, name=name, grid=(s // ts,), in_specs=lg + bs + [row, row, row, row],
                  out_specs=[row, row, row, pl.BlockSpec((ts, 3 * d), lambda i: (i, 0)), pl.BlockSpec((1, 3 * d), lambda i: (0, 0))],
                  out_shape=[yshape] * 3 + [jax.ShapeDtypeStruct((s, 3 * d), BF16), jax.ShapeDtypeStruct((1, 3 * d), F32)],
                  compiler_params=_params("arbitrary"))(logits, logits, logits, bias, bias, bias, ya, yb, yc, dm)


G_FFN1 = ['ffn1_w1', 'ffn1_w3', 'ffn1_w2']
G_FFN2 = ['ffn2_w1', 'ffn2_w3', 'ffn2_w2']
G_MID = [n for n in BIG if n not in G_FFN1 + G_FFN2]


def _ffn_fwd(h, w1, w3, w2, tag, epilogue, side=None):
    carried = None
    if side is None:
        a, b, f = _ffn_up(h, w1, w3, name=f"{tag}_up")
    else:
        (a, b, f), carried = _ffn_up(h, w1, w3, name=f"{tag}_up", side=side)
    if callable(w2):
        w2 = w2(carried)
    outs = _matmul(f, w2, name=f"{tag}_down", alpha=0.5, tm=512, tn=1024, tk=2816, epilogue=epilogue)
    return outs, (h, a, b, f), carried


def _ffn_bwd(x, norm, w1, w3, w2, saved, dy, dyb, tag, side_first=None, side=None, own_side=None):
    h, a, b, f = saved
    dw2 = _matmul(f, dyb, name=f"{tag}_dw2", ta=True, alpha=0.5, tm=1408, tn=1024, tk=2048, side=side_first)
    carried = None
    if side_first is not None:
        dw2, carried = dw2
    if side is None:
        da, db = _ffn_dact(dyb, w2, a, b, name=f"{tag}_dact")
    else:
        (da, db), got = _ffn_dact(dyb, w2, a, b, name=f"{tag}_dact", side=side)
        carried = (carried or []) + got
    dw1 = _matmul(h, da, name=f"{tag}_dw1", ta=True, tm=1024, tn=1408, tk=2048)
    dw3 = _matmul(h, db, name=f"{tag}_dw3", ta=True, tm=1024, tn=1408, tk=2048)
    outs = _matmul(da, w1, name=f"{tag}_dh", tb=True, tm=512, tn=1024, tk=1408, pair2=(db, w3),
                   epilogue=(_epi_rms_bwd, [x, dy], [norm], [F32, BF16], 1),
                   side=None if own_side is None else own_side(dw1, dw3, dw2))
    (dx, dxb, dnorm), own = outs if own_side is not None else (outs, None)
    return dx, dxb, dnorm, dw1, dw3, dw2, carried, own


def _local_step(x, mem, loss_target, wl, ws):
    s, d = x.shape
    assert s % (QB * 16) == 0
    rope = _rope_tables(s)
    bf = {n: wl[n].astype(BF16) for n in BIG}
    w = dict(ws)

    def gather(names):
        return _side([bf[n] for n in names], _two_level_phases())

    def whole(names, gathered):
        return {n: _whole_weight(n, t) for n, t in zip(names, gathered)}

    first_needed = ['ffn1_w1', 'ffn1_w3']
    then_needed = ['ffn1_w2'] + G_MID
    h1, early = _rms_fwd(x, w['ffn1_norm'], name="ffn1_rms", side=gather(first_needed))
    w.update(whole(first_needed, early))
    (x1, h), sv1, late = _ffn_fwd(h1, w['ffn1_w1'], w['ffn1_w3'], lambda got: _whole_weight('ffn1_w2', got[0]), "ffn1",
                                  (_epi_residual_rms, [x], [w['mix_norm']], [F32, BF16], 0),
                                  side=gather(then_needed))
    w.update(whole(then_needed, late))
    p = _matmul(h, w['w_in'], name="in_proj", out_dtype=BF16, tn=1024)
    logits = _matmul(h, w['w_gate'], name="gate_proj", out_dtype=BF16, tn=1024)
    c_qb, c_kb, c_vb, c_qc = 3 * SB_W, 3 * SB_W + DSA_W, 3 * SB_W + 2 * DSA_W, 3 * SB_W + 3 * DSA_W

    oa_t, late = _sb2_fwd(p, name="sb_fwd", side=gather(G_FFN2))
    w.update(whole(G_FFN2, late))
    ya = _matmul(oa_t, w['w_branch_sb'], name="sb_out", out_dtype=BF16)

    qb_n = _qknorm_fwd(p, c_qb, DSA_W, w['qn_dsa'], rope, name="dsa_qnorm", out_dtype=F32)
    kb_n = _qknorm_fwd(p, c_kb, DSA_W, w['kn_dsa'], rope, name="dsa_knorm", out_dtype=F32)
    vb32 = p[:, c_vb:c_vb + DSA_W].astype(F32)
    groups = range(len(DSA_GROUPS))
    ob_t, lse_b = _dsa2_combine([_dsa2_fwd(qb_n, kb_n, vb32, gi, name=f"dsa_fwd{gi}") for gi in groups], name="dsa_combine")
    yb = _matmul(ob_t, w['w_branch_dsa'], name="dsa_out", out_dtype=BF16)

    memh = _rms_fwd(mem, w['mem_norm'], name="mem_rms")
    kv = _matmul(memh, w['w_mem_kv'], name="mem_kv", out_dtype=BF16)
    km_n = _qknorm_fwd(kv, 0, MEM_W, w['kn_mem'], None, name="mem_knorm")
    qc_n = _qknorm_fwd(p, c_qc, MEM_W, w['qn_mem'], None, name="mem_qnorm")
    oc_t = _mem2_fwd(qc_n, km_n, kv, name="mem_fwd")
    yc = _matmul(oc_t, w['w_branch_mem'], name="mem_out", out_dtype=BF16)

    merged = _merge_fwd(logits, w['b_gate'], ya, yb, yc, name="merge")
    x2, h2 = _matmul(merged, w['w_out'], name="out_proj", tn=1024,
                     epilogue=(_epi_residual_rms, [x1], [w['ffn2_norm']], [F32, BF16], 0))
    (dx3, dx3b, sq), sv2, _ = _ffn_fwd(h2, w['ffn2_w1'], w['ffn2_w3'], w['ffn2_w2'], "ffn2",
                                       (_epi_loss, [x2, loss_target], [], [F32, BF16], 1))
    loss = jnp.sum(sq) * (0.5 / d)

    g, recv = {}, {}

    def owners(names):
        return [_for_owners(n, g[n], wl[n].shape) for n in names]

    dx2, dx2b, g['ffn2_norm'], g['ffn2_w1'], g['ffn2_w3'], g['ffn2_w2'], _, _ = _ffn_bwd(
        x2, w['ffn2_norm'], w['ffn2_w1'], w['ffn2_w3'], w['ffn2_w2'], sv2, dx3, dx3b, "ffn2")

    g['w_out'] = _matmul(merged, dx2b, name="d_w_out", ta=True, tn=1024, tk=512)
    dm = _matmul(dx2b, w['w_out'], name="d_merged", tb=True, out_dtype=BF16, tn=1024)
    dya, dyb, dyc, dlogits, g['b_gate'] = _merge_bwd(logits, w['b_gate'], ya, yb, yc, dm, name="d_merge")

    g['w_branch_sb'] = _matmul(oa_t, dya, name="d_w_sb", ta=True, tn=1024, tk=512)
    g['w_branch_dsa'] = _matmul(ob_t, dyb, name="d_w_dsa", ta=True, tk=512)
    g['w_branch_mem'] = _matmul(oc_t, dyc, name="d_w_mem", ta=True, tk=512)
    doa = _matmul(dya, w['w_branch_sb'], name="d_oa", tb=True, out_dtype=BF16)
    dob = _matmul(dyb, w['w_branch_dsa'], name="d_ob", tb=True)
    doc = _matmul(dyc, w['w_branch_mem'], name="d_oc", tb=True, out_dtype=BF16)

    (dqa, dka, dva), got = _sb2_bwd(p, oa_t, doa, name="sb_bwd", side=_side(owners(G_FFN2), _direct_phases(True)))
    recv.update(zip(G_FFN2, got))

    dd_b = _dsa2_prep(ob_t, dob, name="dsa_prep")
    dgrp = [_dsa2_bwd(qb_n, kb_n, vb32, dob, lse_b, dd_b, gi, name=f"dsa_bwd{gi}") for gi in groups]
    dvb = jnp.concatenate([t[2] for t in dgrp], axis=1).astype(BF16)
    dqb, g['qn_dsa'] = _qknorm_bwd(p, c_qb, DSA_W, w['qn_dsa'], rope, [t[0] for t in dgrp], name="d_dsa_qnorm")
    dkb, g['kn_dsa'] = _qknorm_bwd(p, c_kb, DSA_W, w['kn_dsa'], rope, [t[1] for t in dgrp], name="d_dsa_knorm")

    dqc_n, dkm_n, dvm = _mem2_bwd(qc_n, km_n, kv, doc, name="mem_bwd")
    dqc, g['qn_mem'] = _qknorm_bwd(p, c_qc, MEM_W, w['qn_mem'], None, dqc_n, name="d_mem_qnorm")
    dkm, g['kn_mem'] = _qknorm_bwd(kv, 0, MEM_W, w['kn_mem'], None, dkm_n, name="d_mem_knorm")
    dkv = jnp.concatenate([dkm, dvm.astype(BF16)], axis=1)
    g['w_mem_kv'] = _matmul(memh, dkv, name="d_w_mem_kv", ta=True)
    dmemh = _matmul(dkv, w['w_mem_kv'], name="d_memh", tb=True)
    _, _, g['mem_norm'] = _rms_bwd(mem, w['mem_norm'], dmemh, None, name="d_mem_rms")

    dp = jnp.concatenate([dqa.astype(BF16), dka.astype(BF16), dva.astype(BF16),
                          dqb, dkb, dvb, dqc], axis=1)
    g['w_in'] = _matmul(h, dp, name="d_w_in", ta=True, tn=2048, tk=1024)
    g['w_gate'] = _matmul(h, dlogits, name="d_w_gate", ta=True, tn=1536, tk=1024)
    dh = _matmul(dp, w['w_in'], name="d_h_in", tb=True, tn=1024, tk=2048)
    dx1, dx1b, g['mix_norm'] = _matmul(dlogits, w['w_gate'], name="d_h_gate", tb=True, tm=512, tn=1024, tk=3072,
                                       epilogue=(_epi_rms_bwd_sum, [dh, x1, dx2], [w['mix_norm']], [F32, BF16], 1))

    mid_b = ['w_gate', 'w_out']
    mid_a = [n for n in G_MID if n not in mid_b]

    def own_side(dw1, dw3, dw2):
        g.update(ffn1_w1=dw1, ffn1_w3=dw3, ffn1_w2=dw2)
        return _side(owners(G_FFN1), _direct_phases(True))

    dx0, _, g['ffn1_norm'], _, _, _, got_mid, got_own = _ffn_bwd(
        x, w['ffn1_norm'], w['ffn1_w1'], w['ffn1_w3'], w['ffn1_w2'], sv1, dx1, dx1b, "ffn1",
        side_first=_side(owners(mid_b), _direct_phases(True)), side=_side(owners(mid_a), _direct_phases(True)),
        own_side=own_side)
    recv.update(zip(mid_b + mid_a, got_mid))
    recv.update(zip(G_FFN1, got_own))
    return loss, dx0, recv, {n: g[n] for n in SMALL}


def _whole_weight(name, gathered):
    _, r, c = gathered.shape
    return gathered.reshape(N_DEV * r, c) if SHARD_AXIS[name] == 0 else gathered.transpose(1, 0, 2).reshape(r, N_DEV * c)


def _for_owners(name, grad, shard_shape):
    r, c = shard_shape
    blk = grad.reshape(N_DEV, r, c) if SHARD_AXIS[name] == 0 else grad.reshape(r, N_DEV, c).transpose(1, 0, 2)
    return blk.astype(BF16)


def _pack_small(d, names, extra_rows):
    parts = []
    for n in names:
        v = d[n].reshape(-1)
        pad = (-v.size) % LANES
        parts.append(jnp.concatenate([v, jnp.zeros((pad,), v.dtype)]).reshape(-1, LANES))
    t = jnp.concatenate(parts, axis=0)
    return jnp.concatenate([t, jnp.zeros((extra_rows, LANES), t.dtype)], axis=0)


def _unpack_small(t, like, names):
    out, off = {}, 0
    for n in names:
        size = like[n].size
        rows = -(-size // LANES)
        out[n] = t[off:off + rows].reshape(-1)[:size].reshape(like[n].shape)
        off += rows
    return out


def _direct_phases(per_peer):
    def descriptors(src_ref, out_ref, send_sems, recv_sems, local_sem):
        x, y, c = lax.axis_index("x"), lax.axis_index("y"), lax.axis_index("c")
        me = 4 * x + 2 * y + c
        mine = pltpu.make_async_copy(src_ref.at[me] if per_peer else src_ref, out_ref.at[me], local_sem)
        copies = []
        for k in range(1, N_DEV):
            px = 1 - x if k & 4 else x
            py = 1 - y if k & 2 else y
            pc = 1 - c if k & 1 else c
            copies.append(pltpu.make_async_remote_copy(
                src_ref=src_ref.at[4 * px + 2 * py + pc] if per_peer else src_ref, dst_ref=out_ref.at[me],
                send_sem=send_sems.at[k - 1], recv_sem=recv_sems.at[k - 1],
                device_id=(px, py, pc), device_id_type=pl.DeviceIdType.MESH))
        return mine, copies

    def start(*refs):
        mine, copies = descriptors(*refs)
        mine.start()
        for cp in copies:
            cp.start()

    def forward(*refs):
        pass

    def finish(*refs):
        mine, copies = descriptors(*refs)
        for cp in copies:
            cp.wait_recv()
        for cp in copies:
            cp.wait_send()
        mine.wait()

    return start, forward, finish


def _exchange_parts(srcs, phases):
    n = len(srcs)
    shapes = [jax.ShapeDtypeStruct((N_DEV,) + tuple(s.shape[-2:]), s.dtype) for s in srcs]
    sems = [pltpu.SemaphoreType.DMA((n, N_DEV - 1)), pltpu.SemaphoreType.DMA((n, N_DEV - 1)), pltpu.SemaphoreType.DMA((n,))]

    def lift(phase):
        def run(src_refs, out_refs, send, recv, local):
            for a, (s_ref, o_ref) in enumerate(zip(src_refs, out_refs)):
                phase(s_ref, o_ref, send.at[a], recv.at[a], local.at[a])
        return run

    return shapes, sems, [lift(p) for p in phases]


def _exchange(srcs, phases, *, name):
    shapes, sems, runs = _exchange_parts(srcs, phases)
    n = len(srcs)

    def body(*refs):
        for run in runs:
            run(refs[:n], refs[n:2 * n], *refs[2 * n:])

    anyspace = pl.BlockSpec(memory_space=pl.ANY)
    return _pcall(body, name=name, in_specs=[anyspace] * n, out_specs=[anyspace] * n, out_shape=shapes, scratch_shapes=sems)(*srcs)


def _side(srcs, phases):
    shapes, sems, (start, forward, finish) = _exchange_parts(srcs, phases)

    def before(first, mid, ins, outs, scratch):
        pl.when(first)(lambda: start(ins, outs, *scratch))
        pl.when(mid)(lambda: forward(ins, outs, *scratch))

    def after(last, ins, outs, scratch):
        pl.when(last)(lambda: finish(ins, outs, *scratch))

    return list(srcs), shapes, sems, before, after


def _call_2d(kern, *, name, grid, in_specs, out_specs, out_shape, ins, scratch_shapes=(), semantics, side=None):
    if side is None:
        return _pcall(kern, name=name, grid=grid, in_specs=in_specs, out_specs=out_specs, out_shape=out_shape,
                      scratch_shapes=list(scratch_shapes), compiler_params=_params(*semantics))(*ins)
    s_ins, s_shapes, s_scratch, before, after = side
    n_in, n_out, n_scr = len(ins), len(out_shape), len(scratch_shapes)

    def combined(*refs):
        refs = list(refs)
        cut = [n_in, len(s_ins), n_out, len(s_shapes), n_scr, len(s_scratch)]
        parts, pos = [], 0
        for c in cut:
            parts.append(refs[pos:pos + c])
            pos += c
        m_in, c_in, m_out, c_out, m_scr, c_scr = parts
        ids = [pl.program_id(a) for a in range(len(grid))]
        inner_zero = functools.reduce(jnp.logical_and, [i == 0 for i in ids[1:]], True)
        first = jnp.logical_and(ids[0] == 0, inner_zero)
        mid = jnp.logical_and(ids[0] == grid[0] // 2, inner_zero)
        last = functools.reduce(jnp.logical_and, [i == n - 1 for i, n in zip(ids, grid)])
        before(first, mid, c_in, c_out, c_scr)
        kern(*m_in, *m_out, *m_scr)
        after(last, c_in, c_out, c_scr)

    anyspace = pl.BlockSpec(memory_space=pl.ANY)
    outs = _pcall(combined, name=name, grid=grid, in_specs=list(in_specs) + [anyspace] * len(s_ins),
                  out_specs=list(out_specs) + [anyspace] * len(s_shapes), out_shape=list(out_shape) + s_shapes,
                  scratch_shapes=list(scratch_shapes) + s_scratch, compiler_params=_params(*["arbitrary"] * len(grid)))(*ins, *s_ins)
    return outs[:n_out], outs[n_out:]


def _two_level_phases():
    def parts(src_ref, out_ref, send_sems, recv_sems, local_sem):
        x, y, c = lax.axis_index("x"), lax.axis_index("y"), lax.axis_index("c")
        me, sibling = (x, y, c), (x, y, 1 - c)
        chips = [(1 - x, y), (x, 1 - y), (1 - x, 1 - y)]

        def slab(px, py, pc):
            return out_ref.at[4 * px + 2 * py + pc]

        def copy(k, block, to, from_src=False):
            return pltpu.make_async_remote_copy(
                src_ref=src_ref if from_src else slab(*block), dst_ref=slab(*block),
                send_sem=send_sems.at[k], recv_sem=recv_sems.at[k], device_id=to, device_id_type=pl.DeviceIdType.MESH)

        return dict(
            mine=lambda: pltpu.make_async_copy(src_ref, slab(*me), local_sem),
            first=lambda: [copy(0, me, sibling, True)] + [copy(1 + j, me, (*chip, c), True) for j, chip in enumerate(chips)],
            passed=lambda: [copy(4 + j, (*chip, c), sibling) for j, chip in enumerate(chips)],
            landed=lambda: [copy(1 + j, (*chip, c), me) for j, chip in enumerate(chips)],
            late=lambda: [copy(0, sibling, me)] + [copy(4 + j, (*chip, 1 - c), me) for j, chip in enumerate(chips)])

    def start(*refs):
        make = parts(*refs)
        make['mine']().start()
        for cp in make['first']():
            cp.start()

    def forward(*refs):
        make = parts(*refs)
        for arrived, onward in zip(make['landed'](), make['passed']()):
            arrived.wait_recv()
            onward.start()

    def finish(*refs):
        make = parts(*refs)
        for cp in make['late']():
            cp.wait_recv()
        for cp in make['first']() + make['passed']():
            cp.wait_send()
        make['mine']().wait()

    return start, forward, finish


def _adamw(recv, w, m, v, *, name):
    rows, cols = w.shape
    tr = _pick(rows, (256, 128, 64))

    def kern(r_ref, w_ref, m_ref, v_ref, g_ref, d_ref, mo_ref, vo_ref):
        g = r_ref[0].astype(F32)
        for p in range(1, N_DEV):
            g = g + r_ref[p].astype(F32)
        mn = ADAM_B1 * m_ref[...] + (1.0 - ADAM_B1) * g
        vn = ADAM_B2 * v_ref[...] + (1.0 - ADAM_B2) * (g * g)
        m_hat = mn / (1.0 - ADAM_B1 ** ADAM_STEP)
        v_hat = vn / (1.0 - ADAM_B2 ** ADAM_STEP)
        g_ref[...] = g
        d_ref[...] = -ADAM_LR * (m_hat / (jnp.sqrt(v_hat) + ADAM_EPS) + ADAM_WD * w_ref[...])
        mo_ref[...] = mn
        vo_ref[...] = vn

    row = pl.BlockSpec((tr, cols), lambda i: (i, 0))
    shp = jax.ShapeDtypeStruct((rows, cols), F32)
    return _pcall(kern, name=name, grid=(rows // tr,), in_specs=[pl.BlockSpec((N_DEV, tr, cols), lambda i: (0, i, 0)), row, row, row],
                  out_specs=[row, row, row, row], out_shape=[shp, shp, shp, shp], compiler_params=_params("parallel"))(recv, w, m, v)


INPUTS = ['x', 'mem'] + WEIGHTS + ['loss_target'] + ['m_' + n for n in WEIGHTS] + ['v_' + n for n in WEIGHTS]
SMALL_PAD_ROWS = 4


def kernel(x, mem, ffn1_norm, ffn1_w1, ffn1_w3, ffn1_w2, mix_norm, mem_norm, w_in, w_mem_kv, qn_dsa, kn_dsa, qn_mem, kn_mem, w_branch_sb, w_branch_dsa, w_branch_mem, w_gate, b_gate, w_out, ffn2_norm, ffn2_w1, ffn2_w3, ffn2_w2, loss_target, m_ffn1_norm, m_ffn1_w1, m_ffn1_w3, m_ffn1_w2, m_mix_norm, m_mem_norm, m_w_in, m_w_mem_kv, m_qn_dsa, m_kn_dsa, m_qn_mem, m_kn_mem, m_w_branch_sb, m_w_branch_dsa, m_w_branch_mem, m_w_gate, m_b_gate, m_w_out, m_ffn2_norm, m_ffn2_w1, m_ffn2_w3, m_ffn2_w2, v_ffn1_norm, v_ffn1_w1, v_ffn1_w3, v_ffn1_w2, v_mix_norm, v_mem_norm, v_w_in, v_w_mem_kv, v_qn_dsa, v_kn_dsa, v_qn_mem, v_kn_mem, v_w_branch_sb, v_w_branch_dsa, v_w_branch_mem, v_w_gate, v_b_gate, v_w_out, v_ffn2_norm, v_ffn2_w1, v_ffn2_w3, v_ffn2_w2):
    given = dict(zip(INPUTS, (x, mem, ffn1_norm, ffn1_w1, ffn1_w3, ffn1_w2, mix_norm, mem_norm, w_in, w_mem_kv, qn_dsa, kn_dsa, qn_mem, kn_mem, w_branch_sb, w_branch_dsa, w_branch_mem, w_gate, b_gate, w_out, ffn2_norm, ffn2_w1, ffn2_w3, ffn2_w2, loss_target, m_ffn1_norm, m_ffn1_w1, m_ffn1_w3, m_ffn1_w2, m_mix_norm, m_mem_norm, m_w_in, m_w_mem_kv, m_qn_dsa, m_kn_dsa, m_qn_mem, m_kn_mem, m_w_branch_sb, m_w_branch_dsa, m_w_branch_mem, m_w_gate, m_b_gate, m_w_out, m_ffn2_norm, m_ffn2_w1, m_ffn2_w3, m_ffn2_w2, v_ffn1_norm, v_ffn1_w1, v_ffn1_w3, v_ffn1_w2, v_mix_norm, v_mem_norm, v_w_in, v_w_mem_kv, v_qn_dsa, v_kn_dsa, v_qn_mem, v_kn_mem, v_w_branch_sb, v_w_branch_dsa, v_w_branch_mem, v_w_gate, v_b_gate, v_w_out, v_ffn2_norm, v_ffn2_w1, v_ffn2_w3, v_ffn2_w2), strict=True))
    wl = {n: given[n][0] for n in BIG}
    ws = {n: given[n] for n in SMALL}

    loss, dx, recv, g = _local_step(x[0], mem[0], loss_target[0], wl, ws)

    big = [{}, {}, {}, {}]
    for n in G_FFN2 + G_MID + G_FFN1:
        outs = _adamw(recv[n], wl[n], given['m_' + n][0], given['v_' + n][0], name=f"adamw_{n}")
        for kind, t in enumerate(outs):
            big[kind][n] = t

    gs = _pack_small(g, SMALL, SMALL_PAD_ROWS)
    loss_row = gs.shape[0] - SMALL_PAD_ROWS
    gs = gs.at[loss_row, 0].set(loss)
    recv_s = _exchange([gs], _direct_phases(False), name="gather_small")[0]
    small = _adamw(recv_s, _pack_small(ws, SMALL, SMALL_PAD_ROWS), _pack_small({n: given['m_' + n] for n in SMALL}, SMALL, SMALL_PAD_ROWS),
                   _pack_small({n: given['v_' + n] for n in SMALL}, SMALL, SMALL_PAD_ROWS), name="adamw_replicated")
    total_loss = small[0][loss_row, 0]
    small = [_unpack_small(t, ws, SMALL) for t in small]

    outs = [total_loss, dx[None]]
    for kind in range(4):
        outs += [big[kind][n][None] if n in wl else small[kind][n] for n in WEIGHTS]
    return tuple(outs)
```

```python
import functools

import jax
import jax.numpy as jnp
from jax import lax
from jax.experimental import pallas as pl
from jax.experimental.pallas import tpu as pltpu

F32 = jnp.float32
BF16 = jnp.bfloat16
MXU_DT = jnp.bfloat16

N_DEV = 8
HEAD_DIM = 64
SB_HEADS = 8
DSA_GROUPS = ((128, 1), (512, 4), (2048, 16))
DSA_HPG = 4
MEM_HEADS = 4
SB_W = SB_HEADS * HEAD_DIM
DSA_W = DSA_HPG * len(DSA_GROUPS) * HEAD_DIM
DSA_OUT_W = DSA_HPG * HEAD_DIM
MEM_W = MEM_HEADS * HEAD_DIM
ROPE_THETA = 10000.0
NORM_EPS = 1e-6
QB = 128
SCALE = HEAD_DIM ** -0.5
ADAM_LR, ADAM_B1, ADAM_B2, ADAM_EPS, ADAM_WD, ADAM_STEP = 0.001, 0.9, 0.999, 1e-08, 0.01, 10

LANES = 128
VMEM_LIMIT = 48 * 1024 * 1024
SB_DEAD = -110.0 * 1.4426950408889634

WEIGHTS = ['ffn1_norm', 'ffn1_w1', 'ffn1_w3', 'ffn1_w2', 'mix_norm', 'mem_norm', 'w_in', 'w_mem_kv', 'qn_dsa', 'kn_dsa',
           'qn_mem', 'kn_mem', 'w_branch_sb', 'w_branch_dsa', 'w_branch_mem', 'w_gate', 'b_gate', 'w_out', 'ffn2_norm',
           'ffn2_w1', 'ffn2_w3', 'ffn2_w2']
SHARD_AXIS = {'ffn1_norm': None, 'ffn1_w1': 1, 'ffn1_w3': 1, 'ffn1_w2': 0, 'mix_norm': None, 'mem_norm': None, 'w_in': 1,
              'w_mem_kv': 0, 'qn_dsa': None, 'kn_dsa': None, 'qn_mem': None, 'kn_mem': None, 'w_branch_sb': 1,
              'w_branch_dsa': 1, 'w_branch_mem': 1, 'w_gate': 1, 'b_gate': None, 'w_out': 0, 'ffn2_norm': None,
              'ffn2_w1': 1, 'ffn2_w3': 1, 'ffn2_w2': 0}
BIG = [n for n in WEIGHTS if SHARD_AXIS[n] is not None]
SMALL = [n for n in WEIGHTS if SHARD_AXIS[n] is None]


def _pcall(kern, **kw):
    return pl.pallas_call(kern, **kw)


def _params(*sem):
    return pltpu.CompilerParams(dimension_semantics=sem, vmem_limit_bytes=VMEM_LIMIT)


def _dot(a, b, dims):
    return lax.dot_general(a.astype(MXU_DT), b.astype(MXU_DT), (dims, ((), ())), preferred_element_type=F32)


def _nn(a, b):
    return _dot(a, b, ((1,), (0,)))


def _nt(a, b):
    return _dot(a, b, ((1,), (1,)))


def _tn(a, b):
    return _dot(a, b, ((0,), (0,)))


def _pick(n, prefs):
    for p in prefs:
        if n % p == 0:
            return p
    return n


def _matmul(a, b, *, name, ta=False, tb=False, out_dtype=F32, res=None, alpha=1.0, tm=1024, tn=512, tk=1024, pair2=None,
            epilogue=None, side=None):
    if ta:
        kdim, m = a.shape
    else:
        m, kdim = a.shape
    n = b.shape[0] if tb else b.shape[1]
    tm = _pick(m, (tm, 512, 256, 128))
    tn = _pick(n, (tn, 512, 384, 256, 128))
    tk = _pick(kdim, (tk, 1024, 512, 256, 128))
    nk = kdim // tk
    a_spec = pl.BlockSpec((tk, tm), lambda i, j, k: (k, i)) if ta else pl.BlockSpec((tm, tk), lambda i, j, k: (i, k))
    b_spec = pl.BlockSpec((tn, tk), lambda i, j, k: (j, k)) if tb else pl.BlockSpec((tk, tn), lambda i, j, k: (k, j))
    o_spec = pl.BlockSpec((tm, tn), lambda i, j, k: (i, j))
    v_spec = pl.BlockSpec((1, tn), lambda i, j, k: (0, j))
    dims = ((0 if ta else 1,), (1 if tb else 0,))
    n_mm = 2 if pair2 is None else 4
    if epilogue is None:
        row_ins, vec_ins = ([] if res is None else [res]), []
        out_dtypes, n_vec = [out_dtype], 0
    else:
        assert tn == n and res is None
        epi_fn, row_ins, vec_ins, out_dtypes, n_vec = epilogue
    n_row_out = len(out_dtypes)

    def kern(*refs):
        refs = list(refs)
        acc_ref = refs.pop() if nk > 1 else None
        mm = refs[:n_mm]
        extra = refs[n_mm:n_mm + len(row_ins) + len(vec_ins)]
        outs = refs[n_mm + len(extra):]
        i = pl.program_id(0)
        k = pl.program_id(2)

        def product():
            part = _dot(mm[0][...], mm[1][...], dims)
            if pair2 is not None:
                part = part + _dot(mm[2][...], mm[3][...], dims)
            return part

        def finish(r):
            if alpha != 1.0:
                r = r * alpha
            if epilogue is None:
                if extra:
                    r = extra[0][...] + r
                outs[0][...] = r.astype(out_dtype)
                return
            vals = epi_fn(r, *[e[...] for e in extra])
            for o_ref, v in zip(outs[:n_row_out], vals[:n_row_out]):
                o_ref[...] = v.astype(o_ref.dtype)
            for o_ref, v in zip(outs[n_row_out:], vals[n_row_out:]):
                @pl.when(i == 0)
                def _():
                    o_ref[...] = jnp.zeros_like(o_ref)

                o_ref[...] += v

        if nk == 1:
            finish(product())
            return

        @pl.when(k == 0)
        def _():
            acc_ref[...] = jnp.zeros_like(acc_ref)

        acc_ref[...] += product()

        @pl.when(k == nk - 1)
        def _():
            finish(acc_ref[...])

    ins = [a, b] + ([] if pair2 is None else list(pair2)) + list(row_ins) + list(vec_ins)
    specs = [a_spec, b_spec] * (n_mm // 2) + [o_spec] * len(row_ins) + [v_spec] * len(vec_ins)
    out_specs = [o_spec] * n_row_out + [v_spec] * n_vec
    out_shape = [jax.ShapeDtypeStruct((m, n), dt) for dt in out_dtypes] + [jax.ShapeDtypeStruct((1, n), F32)] * n_vec
    outs = _call_2d(kern, name=name, grid=(m // tm, n // tn, nk), in_specs=specs, out_specs=out_specs, out_shape=out_shape,
                    ins=ins, scratch_shapes=[pltpu.VMEM((tm, tn), F32)] if nk > 1 else [],
                    semantics=("arbitrary" if n_vec else "parallel", "parallel", "arbitrary"), side=side)
    carried = None
    if side is not None:
        outs, carried = outs
    outs = outs[0] if epilogue is None else outs
    return outs if side is None else (outs, carried)


def _epi_residual_rms(r, res, gain):
    xn = res + r
    return xn, xn * lax.rsqrt(jnp.mean(xn * xn, axis=-1, keepdims=True) + NORM_EPS) * gain


def _epi_rms_bwd(r, x, dres, gain):
    rs = lax.rsqrt(jnp.mean(x * x, axis=-1, keepdims=True) + NORM_EPS)
    xh = x * rs
    dy = r * gain
    dx = dres + rs * (dy - xh * jnp.mean(dy * xh, axis=-1, keepdims=True))
    return dx, dx, jnp.sum(r * xh, axis=0, keepdims=True)


def _epi_rms_bwd_sum(r, r0, x, dres, gain):
    return _epi_rms_bwd(r + r0, x, dres, gain)


def _epi_loss(r, res, target):
    e = (res + r) - target
    dy = e / e.shape[-1]
    return dy, dy, jnp.sum(e * e, axis=0, keepdims=True)
def _rms_fwd(x, g, *, name, side=None):
    s, d = x.shape
    ts = _pick(s, (512, 256))

    def kern(x_ref, g_ref, h_ref):
        xf = x_ref[...]
        r = lax.rsqrt(jnp.mean(xf * xf, axis=-1, keepdims=True) + NORM_EPS)
        h_ref[...] = (xf * r * g_ref[...]).astype(h_ref.dtype)

    outs = _call_2d(kern, name=name, grid=(s // ts,),
                    in_specs=[pl.BlockSpec((ts, d), lambda i: (i, 0)), pl.BlockSpec((1, d), lambda i: (0, 0))],
                    out_specs=[pl.BlockSpec((ts, d), lambda i: (i, 0))], out_shape=[jax.ShapeDtypeStruct((s, d), BF16)],
                    ins=[x, g], semantics=("parallel",), side=side)
    return outs[0] if side is None else (outs[0][0], outs[1])


def _rms_bwd(x, g, dh, res, *, name):
    s, d = x.shape
    ts = _pick(s, (512, 256))

    def kern(*refs):
        if res is None:
            x_ref, g_ref, dh_ref, dx_ref, dxb_ref, dg_ref = refs
            r_ref = None
        else:
            x_ref, g_ref, dh_ref, r_ref, dx_ref, dxb_ref, dg_ref = refs
        xf = x_ref[...]
        r = lax.rsqrt(jnp.mean(xf * xf, axis=-1, keepdims=True) + NORM_EPS)
        xh = xf * r
        dhf = dh_ref[...].astype(F32)
        dy = dhf * g_ref[...]
        dx = r * (dy - xh * jnp.mean(dy * xh, axis=-1, keepdims=True))
        if r_ref is not None:
            dx = r_ref[...] + dx
        dx_ref[...] = dx
        dxb_ref[...] = dx.astype(dxb_ref.dtype)

        @pl.when(pl.program_id(0) == 0)
        def _():
            dg_ref[...] = jnp.zeros_like(dg_ref)

        dg_ref[...] += jnp.sum(dhf * xh, axis=0, keepdims=True)

    row = pl.BlockSpec((ts, d), lambda i: (i, 0))
    vec = pl.BlockSpec((1, d), lambda i: (0, 0))
    ins = [x, g, dh] + ([] if res is None else [res])
    return _pcall(kern, name=name, grid=(s // ts,), in_specs=[row, vec, row] + ([] if res is None else [row]),
                  out_specs=[row, row, vec],
                  out_shape=[jax.ShapeDtypeStruct((s, d), F32), jax.ShapeDtypeStruct((s, d), BF16), jax.ShapeDtypeStruct((1, d), F32)],
                  compiler_params=_params("arbitrary"))(*ins)


def _sigmoid(x):
    return 1.0 / (1.0 + jnp.exp(-x))


FFN_TM, FFN_TF = 512, 1408


def _ffn_up(h, w1, w3, *, name, side=None):
    s, d = h.shape
    fdim = w1.shape[1]
    tm, tf = _pick(s, (FFN_TM, 256)), _pick(fdim, (FFN_TF, 512, 256, 128))

    def kern(h_ref, w1_ref, w3_ref, a_ref, b_ref, f_ref):
        hb = h_ref[...]
        a = _nn(hb, w1_ref[...])
        b = _nn(hb, w3_ref[...])
        a_ref[...] = a.astype(a_ref.dtype)
        b_ref[...] = b.astype(b_ref.dtype)
        f_ref[...] = (a * _sigmoid(a) * b).astype(f_ref.dtype)

    wspec = pl.BlockSpec((d, tf), lambda i, j: (0, j))
    ospec = pl.BlockSpec((tm, tf), lambda i, j: (i, j))
    shp = jax.ShapeDtypeStruct((s, fdim), BF16)
    return _call_2d(kern, name=name, grid=(s // tm, fdim // tf), in_specs=[pl.BlockSpec((tm, d), lambda i, j: (i, 0)), wspec, wspec],
                    out_specs=[ospec, ospec, ospec], out_shape=[shp, shp, shp], ins=[h, w1, w3],
                    semantics=("parallel", "parallel"), side=side)


def _ffn_dact(dy, w2, a, b, *, name, side=None):
    s, d = dy.shape
    fdim = w2.shape[0]
    tm, tf = _pick(s, (FFN_TM, 256)), _pick(fdim, (FFN_TF, 512, 256, 128))

    def kern(dy_ref, w2_ref, a_ref, b_ref, da_ref, db_ref):
        df = _nt(dy_ref[...], w2_ref[...]) * 0.5
        av = a_ref[...].astype(F32)
        sg = _sigmoid(av)
        da_ref[...] = (df * b_ref[...].astype(F32) * (sg + av * sg * (1.0 - sg))).astype(da_ref.dtype)
        db_ref[...] = (df * (av * sg)).astype(db_ref.dtype)

    ospec = pl.BlockSpec((tm, tf), lambda i, j: (i, j))
    shp = jax.ShapeDtypeStruct((s, fdim), BF16)
    return _call_2d(kern, name=name, grid=(s // tm, fdim // tf),
                    in_specs=[pl.BlockSpec((tm, d), lambda i, j: (i, 0)), pl.BlockSpec((tf, d), lambda i, j: (j, 0)), ospec, ospec],
                    out_specs=[ospec, ospec], out_shape=[shp, shp], ins=[dy, w2, a, b], semantics=("parallel", "parallel"), side=side)


def _head_mean(v, bd):
    outs = []
    for c in range(v.shape[1] // LANES):
        x = v[:, c * LANES:(c + 1) * LANES]
        hi = x.astype(BF16)
        lo = (x - hi.astype(F32)).astype(BF16)
        outs.append(lax.dot_general(jnp.concatenate([hi, lo], axis=1), bd, (((1,), (0,)), ((), ())), preferred_element_type=F32))
    return outs[0] if len(outs) == 1 else jnp.concatenate(outs, axis=1)


def _partner(v):
    w = v.shape[1]
    lane = lax.broadcasted_iota(jnp.int32, v.shape, 1)
    return jnp.where(lane % HEAD_DIM < HEAD_DIM // 2, pltpu.roll(v, w - HEAD_DIM // 2, 1), pltpu.roll(v, HEAD_DIM // 2, 1))


def _block_diag(w=None):
    r = (lax.broadcasted_iota(jnp.int32, (2 * LANES, LANES), 0) % LANES) // HEAD_DIM
    c = lax.broadcasted_iota(jnp.int32, (2 * LANES, LANES), 1) // HEAD_DIM
    return jnp.where(r == c, 1.0 / HEAD_DIM, 0.0).astype(BF16)


def _rope_tables(s):
    half = HEAD_DIM // 2
    inv_freq = jnp.power(ROPE_THETA, -jnp.arange(half, dtype=F32) / half)
    ang = jnp.arange(s).astype(F32)[:, None] * inv_freq[None, :]
    cos, sin = lax.optimization_barrier((jnp.cos(ang), jnp.sin(ang)))
    cos2 = jnp.concatenate([cos, cos, cos, cos], axis=1)
    sin2 = jnp.concatenate([-sin, sin, -sin, sin], axis=1)
    return cos2, sin2


def _qknorm_fwd(src, col0, width, gain, rope, *, name, out_dtype=BF16):
    s = src.shape[0]
    ts = _pick(s, (1024, 512, 256))
    cb = col0 // width
    assert col0 % width == 0
    reps = width // LANES
    g = jnp.tile(gain, (1, width // HEAD_DIM))

    def kern(*refs):
        if rope is None:
            x_ref, g_ref, o_ref = refs
        else:
            x_ref, g_ref, c_ref, s_ref, o_ref = refs
        x = x_ref[...].astype(F32)
        bd = _block_diag(width)
        r = lax.rsqrt(_head_mean(x * x, bd) + NORM_EPS)
        y = x * r * g_ref[...]
        if rope is not None:
            y = y * jnp.tile(c_ref[...], (1, reps)) + _partner(y) * jnp.tile(s_ref[...], (1, reps))
        o_ref[...] = y.astype(o_ref.dtype)

    xs = pl.BlockSpec((ts, width), lambda i: (i, cb))
    tab = pl.BlockSpec((ts, LANES), lambda i: (i, 0))
    ins = [src, g] + ([] if rope is None else list(rope))
    specs = [xs, pl.BlockSpec((1, width), lambda i: (0, 0))] + ([] if rope is None else [tab, tab])
    return _pcall(kern, name=name, grid=(s // ts,), in_specs=specs, out_specs=pl.BlockSpec((ts, width), lambda i: (i, 0)),
                  out_shape=jax.ShapeDtypeStruct((s, width), out_dtype), compiler_params=_params("parallel"))(*ins)


def _qknorm_bwd(src, col0, width, gain, rope, dout, *, name):
    s = src.shape[0]
    ts = _pick(s, (1024, 512, 256))
    cb = col0 // width
    reps = width // LANES
    g = jnp.tile(gain, (1, width // HEAD_DIM))

    douts = list(dout) if isinstance(dout, (list, tuple)) else [dout]
    piece = width // len(douts)

    def kern(*refs):
        refs = list(refs)
        dg_ref = refs.pop()
        dx_ref = refs.pop()
        do_refs = [refs.pop() for _ in douts][::-1]
        if rope is None:
            x_ref, g_ref = refs
        else:
            x_ref, g_ref, c_ref, s_ref = refs
        x = x_ref[...].astype(F32)
        bd = _block_diag(width)
        r = lax.rsqrt(_head_mean(x * x, bd) + NORM_EPS)
        xh = x * r
        dy = jnp.concatenate([d[...].astype(F32) for d in do_refs], axis=1) if len(do_refs) > 1 else do_refs[0][...].astype(F32)
        if rope is not None:
            dy = dy * jnp.tile(c_ref[...], (1, reps)) + _partner(dy * jnp.tile(s_ref[...], (1, reps)))
        dxh = dy * g_ref[...]
        dx_ref[...] = (r * (dxh - xh * _head_mean(dxh * xh, bd))).astype(dx_ref.dtype)

        @pl.when(pl.program_id(0) == 0)
        def _():
            dg_ref[...] = jnp.zeros_like(dg_ref)

        dg_ref[...] += jnp.sum(dy * xh, axis=0, keepdims=True)

    xs = pl.BlockSpec((ts, width), lambda i: (i, cb))
    row = pl.BlockSpec((ts, width), lambda i: (i, 0))
    vec = pl.BlockSpec((1, width), lambda i: (0, 0))
    tab = pl.BlockSpec((ts, LANES), lambda i: (i, 0))
    ins = [src, g] + ([] if rope is None else list(rope)) + douts
    specs = [xs, vec] + ([] if rope is None else [tab, tab]) + [pl.BlockSpec((ts, piece), lambda i: (i, 0))] * len(douts)
    dx, dg = _pcall(kern, name=name, grid=(s // ts,), in_specs=specs, out_specs=[row, vec],
                    out_shape=[jax.ShapeDtypeStruct((s, width), BF16), jax.ShapeDtypeStruct((1, width), F32)],
                    compiler_params=_params("arbitrary"))(*ins)
    return dx, jnp.sum(dg.reshape(width // HEAD_DIM, HEAD_DIM), axis=0, keepdims=True)


def _tri(strict, n):
    r = lax.broadcasted_iota(jnp.int32, (2 * n, n), 0) % n
    c = lax.broadcasted_iota(jnp.int32, (2 * n, n), 1)
    return jnp.where((r > c) if strict else (r >= c), 1.0, 0.0).astype(BF16)


def _split_dot(v, t2):
    hi = v.astype(BF16)
    lo = (v - hi.astype(F32)).astype(BF16)
    return lax.dot_general(jnp.concatenate([hi, lo], axis=1), t2, (((1,), (0,)), ((), ())), preferred_element_type=F32)


LOG2E = 1.4426950408889634


def _log2_sigmoids(z2):
    lf = -(jnp.maximum(z2, 0.0) + jnp.log2(1.0 + jnp.exp2(-jnp.abs(z2))))
    return z2 + lf, lf


SB2_SUB = 2
SB_KT = 128


def _first_half(shape):
    return lax.broadcasted_iota(jnp.int32, shape, 1) < HEAD_DIM


def _split_pair(t, first):
    zero = jnp.zeros_like(t)
    return [jnp.where(first, t, zero), jnp.where(first, zero, t)]


def _sb2_fwd(p, *, name, side=None):
    s = p.shape[0]
    rq = SB2_SUB * QB
    nq = s // rq
    npair = SB_W // LANES

    def kern(q_ref, k_ref, v_ref, o_ref):
        i = pl.program_id(1)
        first = _first_half((rq, LANES))
        q2 = jnp.concatenate(_split_pair(q_ref[...], first), axis=0)
        t2 = _tri(True, SB_KT)
        rel = lax.broadcasted_iota(jnp.int32, (2 * rq, SB_KT), 1) - lax.broadcasted_iota(jnp.int32, (2 * rq, SB_KT), 0) % rq

        def tile(j, q, rel, carry, acc, masked):
            off = pl.multiple_of(j * SB_KT, SB_KT)
            ls, lf = _log2_sigmoids(_nt(q, k_ref[pl.ds(off, SB_KT), :]) * (SCALE * LOG2E))
            if masked:
                before = rel < i * rq - j * SB_KT
                lf = jnp.where(before, lf, 0.0)
            w = jnp.exp2(ls + _split_dot(lf, t2) + carry)
            if masked:
                w = jnp.where(before, w, 0.0)
            return carry + jnp.sum(lf, axis=1, keepdims=True), acc + _nn(w, v_ref[pl.ds(off, SB_KT), :])

        carry, acc = jnp.zeros((2 * rq, 1), F32), jnp.zeros((2 * rq, LANES), F32)
        for a in range(rq // SB_KT):
            carry, acc = tile(i * (rq // SB_KT) + (rq // SB_KT - 1 - a), q2, rel, carry, acc, True)

        def cond(st):
            return jnp.logical_and(st[0] >= 0, st[1] > 0)

        def body(st):
            carry, acc = tile(st[0], q2, rel, st[2], st[3], False)
            return st[0] - 1, (jnp.max(carry) > SB_DEAD).astype(jnp.int32), carry, acc

        st = lax.while_loop(cond, body, (i * (rq // SB_KT) - 1, jnp.int32(1), carry, acc))
        o_ref[...] = jnp.where(first, st[3][:rq], st[3][rq:])

    outs = _call_2d(kern, name=name, grid=(npair, nq),
                    in_specs=[pl.BlockSpec((rq, LANES), lambda a, i: (i, a)), pl.BlockSpec((s, LANES), lambda a, i: (0, npair + a)),
                              pl.BlockSpec((s, LANES), lambda a, i: (0, 2 * npair + a))],
                    out_specs=[pl.BlockSpec((rq, LANES), lambda a, i: (i, a))], out_shape=[jax.ShapeDtypeStruct((s, SB_W), F32)],
                    ins=[p, p, p], semantics=("parallel", "arbitrary"), side=side)
    return outs[0] if side is None else (outs[0][0], outs[1])


def _sb2_bwd(p, o, do, *, name, side=None):
    s = p.shape[0]
    rq = SB2_SUB * QB
    nq = s // rq
    npair = SB_W // LANES

    def kern(q_ref, k_ref, v_ref, o_ref, do_ref, dq_ref, dk_hbm, dv_hbm, dk_acc, dv_acc, sem):
        pr = pl.program_id(0)
        i = pl.program_id(1)

        @pl.when(i == 0)
        def _():
            dk_acc[...] = jnp.zeros_like(dk_acc)
            dv_acc[...] = jnp.zeros_like(dv_acc)

        first = _first_half((rq, LANES))
        q2 = jnp.concatenate(_split_pair(q_ref[...], first), axis=0) * SCALE
        do2 = jnp.concatenate(_split_pair(do_ref[...], first), axis=0)
        o2 = o_ref[...]
        dsum = jnp.sum(do2.astype(F32) * jnp.concatenate([o2, o2], axis=0), axis=1, keepdims=True)
        t_strict = _tri(True, SB_KT)
        t_incl = _tri(False, SB_KT)
        rel = lax.broadcasted_iota(jnp.int32, (2 * rq, SB_KT), 1) - lax.broadcasted_iota(jnp.int32, (2 * rq, SB_KT), 0) % rq

        def tile(j, rows, carry, gcarry, dq, masked):
            q, dob, dsm, rel = rows
            off = pl.multiple_of(j * SB_KT, SB_KT)
            kt = k_ref[pl.ds(off, SB_KT), :]
            ls, lf = _log2_sigmoids(_nt(q, kt) * LOG2E)
            if masked:
                before = rel < i * rq - j * SB_KT
                lf = jnp.where(before, lf, 0.0)
            w = jnp.exp2(ls + _split_dot(lf, t_strict) + carry)
            if masked:
                w = jnp.where(before, w, 0.0)
            wr = w.astype(MXU_DT)
            g = _nt(dob, v_ref[pl.ds(off, SB_KT), :]) * wr.astype(F32)
            big_g = dsm - (_split_dot(g, t_incl) + gcarry)
            sig = jnp.exp2(ls)
            dz = g * (1.0 - sig) - sig * big_g
            if masked:
                dz = jnp.where(before, dz, 0.0)
            dk_acc[pl.ds(off, SB_KT), :] += _tn(dz, q)
            dv_acc[pl.ds(off, SB_KT), :] += _tn(wr, dob)
            return (carry + jnp.sum(lf, axis=1, keepdims=True), gcarry + jnp.sum(g, axis=1, keepdims=True),
                    dq + _nn(dz, kt))

        zc = jnp.zeros((2 * rq, 1), F32)
        carry, gcarry, dq = zc, zc, jnp.zeros((2 * rq, LANES), F32)
        whole = (q2, do2, dsum, rel)
        for a in range(rq // SB_KT):
            carry, gcarry, dq = tile(i * (rq // SB_KT) + (rq // SB_KT - 1 - a), whole, carry, gcarry, dq, True)

        def cond(st):
            return jnp.logical_and(st[0] >= 0, st[1] > 0)

        def body(st):
            carry, gcarry, dq = tile(st[0], whole, st[2], st[3], st[4], False)
            return st[0] - 1, (jnp.max(carry) > SB_DEAD).astype(jnp.int32), carry, gcarry, dq

        st = lax.while_loop(cond, body, (i * (rq // SB_KT) - 1, jnp.int32(1), carry, gcarry, dq))
        dq_ref[...] = (jnp.where(first, st[4][:rq], st[4][rq:]) * SCALE).astype(dq_ref.dtype)

        @pl.when(i == nq - 1)
        def _():
            cols = pl.ds(pl.multiple_of(pr * LANES, LANES), LANES)
            ck = pltpu.make_async_copy(dk_acc, dk_hbm.at[:, cols], sem.at[0])
            cv = pltpu.make_async_copy(dv_acc, dv_hbm.at[:, cols], sem.at[1])
            ck.start()
            cv.start()
            ck.wait()
            cv.wait()

    blk = pl.BlockSpec((rq, LANES), lambda a, i: (i, a))
    anyspace = pl.BlockSpec(memory_space=pl.ANY)
    shp = jax.ShapeDtypeStruct((s, SB_W), F32)
    return _call_2d(kern, name=name, grid=(npair, nq),
                    in_specs=[blk, pl.BlockSpec((s, LANES), lambda a, i: (0, npair + a)),
                              pl.BlockSpec((s, LANES), lambda a, i: (0, 2 * npair + a)), blk, blk],
                    out_specs=[blk, anyspace, anyspace], out_shape=[jax.ShapeDtypeStruct((s, SB_W), BF16), shp, shp], ins=[p, p, p, o, do],
                    scratch_shapes=[pltpu.VMEM((s, LANES), F32), pltpu.VMEM((s, LANES), F32), pltpu.SemaphoreType.DMA((2,))],
                    semantics=("arbitrary", "arbitrary"), side=side)


def _dsa_rel():
    qi = lax.broadcasted_iota(jnp.int32, (QB, QB), 0)
    kj = lax.broadcasted_iota(jnp.int32, (QB, QB), 1)
    return kj - qi


def _prev_mask(rel, has_prev):
    return rel >= jnp.where(has_prev, 0, QB)


DSA_BT = QB * max(r for _, r in DSA_GROUPS)
def _units_per_batch(r):
    return 8 if r < max(d for _, d in DSA_GROUPS) else 4


def _bdot(a, b, ca, cb):
    return lax.dot_general(a.astype(MXU_DT), b.astype(MXU_DT), (((ca,), (cb,)), ((0,), (0,))), preferred_element_type=F32)


def _bnt(a, b):
    return _bdot(a, b, 2, 2)


def _bnn(a, b):
    return _bdot(a, b, 2, 1)


def _btn(a, b):
    return _bdot(a, b, 1, 1)


def _unit_rows(r, c, b):
    return pl.ds(c + QB * r * b, QB, stride=r)


def _pair_cols(t, first):
    return [jnp.max(jnp.where(first, t, -jnp.inf), axis=1, keepdims=True),
            jnp.max(jnp.where(first, -jnp.inf, t), axis=1, keepdims=True)]


def _dsa2_fwd(qn, kn, v32, g, *, name):
    s = qn.shape[0]
    r = DSA_GROUPS[g][1]
    nbk = DSA_BT // (QB * r)
    npair = DSA_OUT_W // LANES

    def kern(q_ref, k_ref, kp_ref, v_ref, vp_ref, o_ref, l_ref):
        t = pl.program_id(1)
        first = _first_half((QB, LANES))
        rel = _dsa_rel()
        units = [(c, b) for c in range(r) for b in range(nbk)]
        ub = _units_per_batch(r)
        for u0 in range(0, len(units), ub):
            batch = units[u0:u0 + ub]
            qs, kcs, vcs, kps, vps, masks = [], [], [], [], [], []
            for c, b in batch:
                rows = _unit_rows(r, c, b)
                kc, vc = k_ref[rows, :].astype(MXU_DT), v_ref[rows, :].astype(MXU_DT)
                if b > 0:
                    prow = _unit_rows(r, c, b - 1)
                    kpv, vpv, has_prev = k_ref[prow, :], v_ref[prow, :], True
                else:
                    prow = _unit_rows(r, c, nbk - 1)
                    kpv, vpv, has_prev = kp_ref[prow, :], vp_ref[prow, :], t > 0
                for qe in _split_pair(q_ref[rows, :], first):
                    qs.append(qe.astype(MXU_DT))
                    kcs.append(kc)
                    vcs.append(vc)
                    kps.append(kpv.astype(MXU_DT))
                    vps.append(vpv.astype(MXU_DT))
                    masks.append(_prev_mask(rel, has_prev))
            qq = jnp.stack(qs)
            sc = jnp.where(rel <= 0, _bnt(qq, jnp.stack(kcs)) * SCALE, -jnp.inf)
            sp = _bnt(qq, jnp.stack(kps)) * SCALE
            sp = jnp.stack([jnp.where(mk, sp[n], -jnp.inf) for n, mk in enumerate(masks)])
            m = jnp.maximum(jnp.max(sc, axis=2, keepdims=True), jnp.max(sp, axis=2, keepdims=True))
            pc = jnp.exp(sc - m)
            pp = jnp.exp(sp - m)
            den = jnp.sum(pc, axis=2, keepdims=True) + jnp.sum(pp, axis=2, keepdims=True)
            out = (_bnn(pc, jnp.stack(vcs)) + _bnn(pp, jnp.stack(vps))) / den
            lse = m + jnp.log(den)
            for idx, (c, b) in enumerate(batch):
                rows = _unit_rows(r, c, b)
                o_ref[rows, :] = jnp.where(first, out[2 * idx], out[2 * idx + 1])
                l_ref[rows, :] = jnp.where(first, lse[2 * idx], lse[2 * idx + 1])

    npg = DSA_HPG * HEAD_DIM // LANES
    cur = pl.BlockSpec((DSA_BT, LANES), lambda a, t: (t, npg * g + a))
    prev = pl.BlockSpec((DSA_BT, LANES), lambda a, t: (jnp.maximum(t - 1, 0), npg * g + a))
    out = pl.BlockSpec((DSA_BT, LANES), lambda a, t: (t, a))
    shp = jax.ShapeDtypeStruct((s, DSA_OUT_W), F32)
    return _pcall(kern, name=name, grid=(npair, s // DSA_BT), in_specs=[cur, cur, prev, cur, prev], out_specs=[out, out],
                  out_shape=[shp, shp], compiler_params=_params("parallel", "parallel"))(qn, kn, kn, v32, v32)


def _dsa2_combine(parts, *, name):
    s, wd = parts[0][0].shape
    ts = _pick(s, (2048, 1024, 512, 256))

    def kern(o0, l0, o1, l1, o2, l2, o_ref, l_ref):
        ls = [l0[...], l1[...], l2[...]]
        m = jnp.maximum(jnp.maximum(ls[0], ls[1]), ls[2])
        es = [jnp.exp(l - m) for l in ls]
        den = es[0] + es[1] + es[2]
        o_ref[...] = (es[0] * o0[...] + es[1] * o1[...] + es[2] * o2[...]) / den
        l_ref[...] = m + jnp.log(den)

    blk = pl.BlockSpec((ts, wd), lambda i: (i, 0))
    shp = jax.ShapeDtypeStruct((s, wd), F32)
    flat = [t for pair in parts for t in pair]
    return _pcall(kern, name=name, grid=(s // ts,), in_specs=[blk] * 6, out_specs=[blk, blk], out_shape=[shp, shp],
                  compiler_params=_params("parallel"))(*flat)


def _dsa2_prep(o, do, *, name):
    s, wd = o.shape
    ts = _pick(s, (2048, 1024, 512, 256))

    def kern(o_ref, do_ref, d_ref):
        d_ref[...] = _head_mean(do_ref[...] * o_ref[...], _block_diag(wd)) * HEAD_DIM

    blk = pl.BlockSpec((ts, wd), lambda i: (i, 0))
    return _pcall(kern, name=name, grid=(s // ts,), in_specs=[blk, blk], out_specs=blk,
                  out_shape=jax.ShapeDtypeStruct((s, wd), F32), compiler_params=_params("parallel"))(o, do)


def _dsa2_bwd(qn, kn, v32, do, lse, dd, g, *, name):
    s = qn.shape[0]
    r = DSA_GROUPS[g][1]
    nbk = DSA_BT // (QB * r)
    npair = DSA_OUT_W // LANES
    nsteps = s // DSA_BT

    def kern(q_ref, qn_ref, k_ref, kp_ref, v_ref, vp_ref, do_ref, don_ref, l_ref, ln_ref, d_ref, dn_ref,
             dq_ref, dk_ref, dv_ref):
        t = pl.program_id(1)
        first = _first_half((QB, LANES))
        rel = _dsa_rel()

        def pairs(items):
            qq = jnp.stack([it[0].astype(MXU_DT) for it in items])
            dd = jnp.stack([it[1].astype(MXU_DT) for it in items])
            kk = jnp.stack([it[4].astype(MXU_DT) for it in items])
            vv = jnp.stack([it[5].astype(MXU_DT) for it in items])
            p = jnp.exp(_bnt(qq, kk) * SCALE - jnp.stack([it[2] for it in items]))
            p = jnp.stack([jnp.where(it[6], p[n], 0.0) for n, it in enumerate(items)])
            ds = p * (_bnt(dd, vv) - jnp.stack([it[3] for it in items])) * SCALE
            return _bnn(ds, kk), _btn(ds, qq), _btn(p, dd)

        def heads(rows, qr, dor, lr, dr):
            return list(zip(_split_pair(qr[rows, :], first), _split_pair(dor[rows, :], first),
                            _pair_cols(lr[rows, :], first), _pair_cols(dr[rows, :], first)))

        units = [(c, b) for c in range(r) for b in range(nbk)]
        dk_of, dv_of = [None] * len(units), [None] * len(units)
        ub = _units_per_batch(r)
        for u0 in range(0, len(units), ub // 2):
            batch = list(enumerate(units))[u0:u0 + ub // 2]
            items = []
            for u, (c, b) in batch:
                rows = _unit_rows(r, c, b)
                kc, vc = k_ref[rows, :], v_ref[rows, :]
                if b > 0:
                    prow = _unit_rows(r, c, b - 1)
                    kpv, vpv, pmask = k_ref[prow, :], v_ref[prow, :], _prev_mask(rel, True)
                else:
                    prow = _unit_rows(r, c, nbk - 1)
                    kpv, vpv, pmask = kp_ref[prow, :], vp_ref[prow, :], _prev_mask(rel, t > 0)
                for hd in heads(rows, q_ref, do_ref, l_ref, d_ref):
                    items.append(hd + (kc, vc, rel <= 0))
                    items.append(hd + (kpv, vpv, pmask))
            dq, dk, dv = pairs(items)
            for n, (u, (c, b)) in enumerate(batch):
                dq_ref[_unit_rows(r, c, b), :] = jnp.where(first, dq[4 * n] + dq[4 * n + 1], dq[4 * n + 2] + dq[4 * n + 3])
                dk_of[u] = dk[4 * n] + dk[4 * n + 2]
                dv_of[u] = dv[4 * n] + dv[4 * n + 2]
                if b > 0:
                    dk_of[u - 1] = dk_of[u - 1] + (dk[4 * n + 1] + dk[4 * n + 3])
                    dv_of[u - 1] = dv_of[u - 1] + (dv[4 * n + 1] + dv[4 * n + 3])
        lasts = [c * nbk + nbk - 1 for c in range(r)]
        for c0 in range(0, r, 4):
            chunk = list(range(c0, min(c0 + 4, r)))
            items = []
            for c in chunk:
                last = _unit_rows(r, c, nbk - 1)
                for hd in heads(_unit_rows(r, c, 0), qn_ref, don_ref, ln_ref, dn_ref):
                    items.append(hd + (k_ref[last, :], v_ref[last, :], _prev_mask(rel, t < nsteps - 1)))
            _, dk, dv = pairs(items)
            for n, c in enumerate(chunk):
                dk_of[lasts[c]] = dk_of[lasts[c]] + (dk[2 * n] + dk[2 * n + 1])
                dv_of[lasts[c]] = dv_of[lasts[c]] + (dv[2 * n] + dv[2 * n + 1])
        for u, (c, b) in enumerate(units):
            dk_ref[_unit_rows(r, c, b), :] = dk_of[u]
            dv_ref[_unit_rows(r, c, b), :] = dv_of[u]

    npg = DSA_HPG * HEAD_DIM // LANES

    def at(shift, col):
        return pl.BlockSpec((DSA_BT, LANES), lambda a, t: (jnp.clip(t + shift, 0, nsteps - 1), col(a)))

    gcol = lambda a: npg * g + a
    ocol = lambda a: a
    specs = [at(0, gcol), at(1, gcol), at(0, gcol), at(-1, gcol), at(0, gcol), at(-1, gcol),
             at(0, ocol), at(1, ocol), at(0, ocol), at(1, ocol), at(0, ocol), at(1, ocol)]
    shp = jax.ShapeDtypeStruct((s, DSA_OUT_W), F32)
    return _pcall(kern, name=name, grid=(npair, nsteps), in_specs=specs, out_specs=[at(0, ocol)] * 3, out_shape=[shp, shp, shp],
                  compiler_params=_params("parallel", "parallel"))(qn, qn, kn, kn, v32, v32, do, do, lse, lse, dd, dd)


def _mem2_fwd(qn, km, kv, *, name):
    s = qn.shape[0]
    ml = km.shape[0]
    tq = _pick(s, (2048, 1024, 512, 256))
    npair = MEM_W // LANES

    def kern(q_ref, k_ref, v_ref, o_ref):
        first = _first_half((tq, LANES))
        q2 = jnp.concatenate(_split_pair(q_ref[...], first), axis=0)
        sc = _nt(q2, k_ref[...]) * SCALE
        e = jnp.exp(sc - jnp.max(sc, axis=1, keepdims=True))
        o2 = _nn(e / jnp.sum(e, axis=1, keepdims=True), v_ref[...])
        o_ref[...] = jnp.where(first, o2[:tq], o2[tq:])

    blk = pl.BlockSpec((tq, LANES), lambda a, i: (i, a))
    return _pcall(kern, name=name, grid=(npair, s // tq),
                  in_specs=[blk, pl.BlockSpec((ml, LANES), lambda a, i: (0, a)), pl.BlockSpec((ml, LANES), lambda a, i: (0, npair + a))],
                  out_specs=blk, out_shape=jax.ShapeDtypeStruct((s, MEM_W), F32),
                  compiler_params=_params("parallel", "parallel"))(qn, km, kv)


def _mem2_bwd(qn, km, kv, do, *, name):
    s = qn.shape[0]
    ml = km.shape[0]
    tq = _pick(s, (2048, 1024, 512, 256))
    npair = MEM_W // LANES

    def kern(q_ref, k_ref, v_ref, do_ref, dq_ref, dk_ref, dv_ref):
        @pl.when(pl.program_id(1) == 0)
        def _():
            dk_ref[...] = jnp.zeros_like(dk_ref)
            dv_ref[...] = jnp.zeros_like(dv_ref)

        first = _first_half((tq, LANES))
        q2 = jnp.concatenate(_split_pair(q_ref[...], first), axis=0)
        do2 = jnp.concatenate(_split_pair(do_ref[...], first), axis=0)
        sc = _nt(q2, k_ref[...]) * SCALE
        e = jnp.exp(sc - jnp.max(sc, axis=1, keepdims=True))
        p = e / jnp.sum(e, axis=1, keepdims=True)
        dp = _nt(do2, v_ref[...])
        ds = p * (dp - jnp.sum(p * dp, axis=1, keepdims=True)) * SCALE
        dq2 = _nn(ds, k_ref[...])
        dk_ref[...] += _tn(ds, q2)
        dv_ref[...] += _tn(p, do2)
        dq_ref[...] = jnp.where(first, dq2[:tq], dq2[tq:])

    blk = pl.BlockSpec((tq, LANES), lambda a, i: (i, a))
    kblk = pl.BlockSpec((ml, LANES), lambda a, i: (0, a))
    kshape = jax.ShapeDtypeStruct((ml, MEM_W), F32)
    return _pcall(kern, name=name, grid=(npair, s // tq),
                  in_specs=[blk, kblk, pl.BlockSpec((ml, LANES), lambda a, i: (0, npair + a)), blk],
                  out_specs=[blk, kblk, kblk], out_shape=[jax.ShapeDtypeStruct((s, MEM_W), F32), kshape, kshape],
                  compiler_params=_params("parallel", "arbitrary"))(qn, km, kv, do)


def _merge_fwd(logits, bias, ya, yb, yc, *, name):
    s, d = ya.shape
    ts = _pick(s, (1024, 512, 256))

    def kern(l0, l1, l2, b0, b1, b2, a_ref, b_ref, c_ref, o_ref):
        m = 0.0
        for l_ref, bb_ref, y_ref in ((l0, b0, a_ref), (l1, b1, b_ref), (l2, b2, c_ref)):
            m = m + _sigmoid(l_ref[...].astype(F32) + bb_ref[...]) * y_ref[...].astype(F32)
        o_ref[...] = m.astype(o_ref.dtype)

    row = pl.BlockSpec((ts, d), lambda i: (i, 0))
    lg = [pl.BlockSpec((ts, d), functools.partial(lambda i, c: (i, c), c=c)) for c in range(3)]
    bs = [pl.BlockSpec((1, d), functools.partial(lambda i, c: (0, c), c=c)) for c in range(3)]
    return _pcall(kern, name=name, grid=(s // ts,), in_specs=lg + bs + [row, row, row], out_specs=row,
                  out_shape=jax.ShapeDtypeStruct((s, d), BF16),
                  compiler_params=_params("parallel"))(logits, logits, logits, bias, bias, bias, ya, yb, yc)


def _merge_bwd(logits, bias, ya, yb, yc, dm, *, name):
    s, d = ya.shape
    ts = _pick(s, (512, 256))

    def kern(l0, l1, l2, b0, b1, b2, a_ref, b_ref, c_ref, dm_ref, da_ref, db_ref, dc_ref, dl_ref, dbias_ref):
        dmv = dm_ref[...].astype(F32)

        @pl.when(pl.program_id(0) == 0)
        def _():
            dbias_ref[...] = jnp.zeros_like(dbias_ref)

        for c, (l_ref, bb_ref, y_ref, dy_ref) in enumerate(((l0, b0, a_ref, da_ref), (l1, b1, b_ref, db_ref), (l2, b2, c_ref, dc_ref))):
            g = _sigmoid(l_ref[...].astype(F32) + bb_ref[...])
            dy_ref[...] = (dmv * g).astype(dy_ref.dtype)
            dl = dmv * y_ref[...].astype(F32) * g * (1.0 - g)
            dl_ref[:, c * d:(c + 1) * d] = dl.astype(dl_ref.dtype)
            dbias_ref[:, c * d:(c + 1) * d] += jnp.sum(dl, axis=0, keepdims=True)

    row = pl.BlockSpec((ts, d), lambda i: (i, 0))
    lg = [pl.BlockSpec((ts, d), functools.partial(lambda i, c: (i, c), c=c)) for c in range(3)]
    bs = [pl.BlockSpec((1, d), functools.partial(lambda i, c: (0, c), c=c)) for c in range(3)]
    yshape = jax.ShapeDtypeStruct((s, d), BF16)
    return _pcall(kern, name=name, grid=(s // ts,), in_specs=lg + bs + [row, row, row, row],
                  out_specs=[row, row, row, pl.BlockSpec((ts, 3 * d), lambda i: (i, 0)), pl.BlockSpec((1, 3 * d), lambda i: (0, 0))],
                  out_shape=[yshape] * 3 + [jax.ShapeDtypeStruct((s, 3 * d), BF16), jax.ShapeDtypeStruct((1, 3 * d), F32)],
                  compiler_params=_params("arbitrary"))(logits, logits, logits, bias, bias, bias, ya, yb, yc, dm)


G_FFN1 = ['ffn1_w1', 'ffn1_w3', 'ffn1_w2']
G_FFN2 = ['ffn2_w1', 'ffn2_w3', 'ffn2_w2']
G_MID = [n for n in BIG if n not in G_FFN1 + G_FFN2]


def _ffn_fwd(h, w1, w3, w2, tag, epilogue, side=None):
    carried = None
    if side is None:
        a, b, f = _ffn_up(h, w1, w3, name=f"{tag}_up")
    else:
        (a, b, f), carried = _ffn_up(h, w1, w3, name=f"{tag}_up", side=side)
    if callable(w2):
        w2 = w2(carried)
    outs = _matmul(f, w2, name=f"{tag}_down", alpha=0.5, tm=512, tn=1024, tk=2816, epilogue=epilogue)
    return outs, (h, a, b, f), carried


def _ffn_bwd(x, norm, w1, w3, w2, saved, dy, dyb, tag, side_first=None, side=None, own_side=None):
    h, a, b, f = saved
    dw2 = _matmul(f, dyb, name=f"{tag}_dw2", ta=True, alpha=0.5, tm=1408, tn=1024, tk=2048, side=side_first)
    carried = None
    if side_first is not None:
        dw2, carried = dw2
    if side is None:
        da, db = _ffn_dact(dyb, w2, a, b, name=f"{tag}_dact")
    else:
        (da, db), got = _ffn_dact(dyb, w2, a, b, name=f"{tag}_dact", side=side)
        carried = (carried or []) + got
    dw1 = _matmul(h, da, name=f"{tag}_dw1", ta=True, tm=1024, tn=1408, tk=2048)
    dw3 = _matmul(h, db, name=f"{tag}_dw3", ta=True, tm=1024, tn=1408, tk=2048)
    outs = _matmul(da, w1, name=f"{tag}_dh", tb=True, tm=512, tn=1024, tk=1408, pair2=(db, w3),
                   epilogue=(_epi_rms_bwd, [x, dy], [norm], [F32, BF16], 1),
                   side=None if own_side is None else own_side(dw1, dw3, dw2))
    (dx, dxb, dnorm), own = outs if own_side is not None else (outs, None)
    return dx, dxb, dnorm, dw1, dw3, dw2, carried, own


def _local_step(x, mem, loss_target, wl, ws):
    s, d = x.shape
    assert s % (QB * 16) == 0
    rope = _rope_tables(s)
    bf = {n: wl[n].astype(BF16) for n in BIG}
    w = dict(ws)

    def gather(names):
        return _side([bf[n] for n in names], _two_level_phases())

    def whole(names, gathered):
        return {n: _whole_weight(n, t) for n, t in zip(names, gathered)}

    first_needed = ['ffn1_w1', 'ffn1_w3']
    then_needed = ['ffn1_w2'] + G_MID
    h1, early = _rms_fwd(x, w['ffn1_norm'], name="ffn1_rms", side=gather(first_needed))
    w.update(whole(first_needed, early))
    (x1, h), sv1, late = _ffn_fwd(h1, w['ffn1_w1'], w['ffn1_w3'], lambda got: _whole_weight('ffn1_w2', got[0]), "ffn1",
                                  (_epi_residual_rms, [x], [w['mix_norm']], [F32, BF16], 0),
                                  side=gather(then_needed))
    w.update(whole(then_needed, late))
    p = _matmul(h, w['w_in'], name="in_proj", out_dtype=BF16, tn=1024)
    logits = _matmul(h, w['w_gate'], name="gate_proj", out_dtype=BF16, tn=1024)
    c_qb, c_kb, c_vb, c_qc = 3 * SB_W, 3 * SB_W + DSA_W, 3 * SB_W + 2 * DSA_W, 3 * SB_W + 3 * DSA_W

    oa_t, late = _sb2_fwd(p, name="sb_fwd", side=gather(G_FFN2))
    w.update(whole(G_FFN2, late))
    ya = _matmul(oa_t, w['w_branch_sb'], name="sb_out", out_dtype=BF16)

    qb_n = _qknorm_fwd(p, c_qb, DSA_W, w['qn_dsa'], rope, name="dsa_qnorm", out_dtype=F32)
    kb_n = _qknorm_fwd(p, c_kb, DSA_W, w['kn_dsa'], rope, name="dsa_knorm", out_dtype=F32)
    vb32 = p[:, c_vb:c_vb + DSA_W].astype(F32)
    groups = range(len(DSA_GROUPS))
    ob_t, lse_b = _dsa2_combine([_dsa2_fwd(qb_n, kb_n, vb32, gi, name=f"dsa_fwd{gi}") for gi in groups], name="dsa_combine")
    yb = _matmul(ob_t, w['w_branch_dsa'], name="dsa_out", out_dtype=BF16)

    memh = _rms_fwd(mem, w['mem_norm'], name="mem_rms")
    kv = _matmul(memh, w['w_mem_kv'], name="mem_kv", out_dtype=BF16)
    km_n = _qknorm_fwd(kv, 0, MEM_W, w['kn_mem'], None, name="mem_knorm")
    qc_n = _qknorm_fwd(p, c_qc, MEM_W, w['qn_mem'], None, name="mem_qnorm")
    oc_t = _mem2_fwd(qc_n, km_n, kv, name="mem_fwd")
    yc = _matmul(oc_t, w['w_branch_mem'], name="mem_out", out_dtype=BF16)

    merged = _merge_fwd(logits, w['b_gate'], ya, yb, yc, name="merge")
    x2, h2 = _matmul(merged, w['w_out'], name="out_proj", tn=1024,
                     epilogue=(_epi_residual_rms, [x1], [w['ffn2_norm']], [F32, BF16], 0))
    (dx3, dx3b, sq), sv2, _ = _ffn_fwd(h2, w['ffn2_w1'], w['ffn2_w3'], w['ffn2_w2'], "ffn2",
                                       (_epi_loss, [x2, loss_target], [], [F32, BF16], 1))
    loss = jnp.sum(sq) * (0.5 / d)

    g, recv = {}, {}

    def owners(names):
        return [_for_owners(n, g[n], wl[n].shape) for n in names]

    dx2, dx2b, g['ffn2_norm'], g['ffn2_w1'], g['ffn2_w3'], g['ffn2_w2'], _, _ = _ffn_bwd(
        x2, w['ffn2_norm'], w['ffn2_w1'], w['ffn2_w3'], w['ffn2_w2'], sv2, dx3, dx3b, "ffn2")

    g['w_out'] = _matmul(merged, dx2b, name="d_w_out", ta=True, tn=1024, tk=512)
    dm = _matmul(dx2b, w['w_out'], name="d_merged", tb=True, out_dtype=BF16, tn=1024)
    dya, dyb, dyc, dlogits, g['b_gate'] = _merge_bwd(logits, w['b_gate'], ya, yb, yc, dm, name="d_merge")

    g['w_branch_sb'] = _matmul(oa_t, dya, name="d_w_sb", ta=True, tn=1024, tk=512)
    g['w_branch_dsa'] = _matmul(ob_t, dyb, name="d_w_dsa", ta=True, tk=512)
    g['w_branch_mem'] = _matmul(oc_t, dyc, name="d_w_mem", ta=True, tk=512)
    doa = _matmul(dya, w['w_branch_sb'], name="d_oa", tb=True, out_dtype=BF16)
    dob = _matmul(dyb, w['w_branch_dsa'], name="d_ob", tb=True)
    doc = _matmul(dyc, w['w_branch_mem'], name="d_oc", tb=True, out_dtype=BF16)

    (dqa, dka, dva), got = _sb2_bwd(p, oa_t, doa, name="sb_bwd", side=_side(owners(G_FFN2), _direct_phases(True)))
    recv.update(zip(G_FFN2, got))

    dd_b = _dsa2_prep(ob_t, dob, name="dsa_prep")
    dgrp = [_dsa2_bwd(qb_n, kb_n, vb32, dob, lse_b, dd_b, gi, name=f"dsa_bwd{gi}") for gi in groups]
    dvb = jnp.concatenate([t[2] for t in dgrp], axis=1).astype(BF16)
    dqb, g['qn_dsa'] = _qknorm_bwd(p, c_qb, DSA_W, w['qn_dsa'], rope, [t[0] for t in dgrp], name="d_dsa_qnorm")
    dkb, g['kn_dsa'] = _qknorm_bwd(p, c_kb, DSA_W, w['kn_dsa'], rope, [t[1] for t in dgrp], name="d_dsa_knorm")

    dqc_n, dkm_n, dvm = _mem2_bwd(qc_n, km_n, kv, doc, name="mem_bwd")
    dqc, g['qn_mem'] = _qknorm_bwd(p, c_qc, MEM_W, w['qn_mem'], None, dqc_n, name="d_mem_qnorm")
    dkm, g['kn_mem'] = _qknorm_bwd(kv, 0, MEM_W, w['kn_mem'], None, dkm_n, name="d_mem_knorm")
    dkv = jnp.concatenate([dkm, dvm.astype(BF16)], axis=1)
    g['w_mem_kv'] = _matmul(memh, dkv, name="d_w_mem_kv", ta=True)
    dmemh = _matmul(dkv, w['w_mem_kv'], name="d_memh", tb=True)
    _, _, g['mem_norm'] = _rms_bwd(mem, w['mem_norm'], dmemh, None, name="d_mem_rms")

    dp = jnp.concatenate([dqa.astype(BF16), dka.astype(BF16), dva.astype(BF16),
                          dqb, dkb, dvb, dqc], axis=1)
    g['w_in'] = _matmul(h, dp, name="d_w_in", ta=True, tn=2048, tk=1024)
    g['w_gate'] = _matmul(h, dlogits, name="d_w_gate", ta=True, tn=1536, tk=1024)
    dh = _matmul(dp, w['w_in'], name="d_h_in", tb=True, tn=1024, tk=2048)
    dx1, dx1b, g['mix_norm'] = _matmul(dlogits, w['w_gate'], name="d_h_gate", tb=True, tm=512, tn=1024, tk=3072,
                                       epilogue=(_epi_rms_bwd_sum, [dh, x1, dx2], [w['mix_norm']], [F32, BF16], 1))

    mid_b = ['w_gate', 'w_out']
    mid_a = [n for n in G_MID if n not in mid_b]

    def own_side(dw1, dw3, dw2):
        g.update(ffn1_w1=dw1, ffn1_w3=dw3, ffn1_w2=dw2)
        return _side(owners(G_FFN1), _direct_phases(True))

    dx0, _, g['ffn1_norm'], _, _, _, got_mid, got_own = _ffn_bwd(
        x, w['ffn1_norm'], w['ffn1_w1'], w['ffn1_w3'], w['ffn1_w2'], sv1, dx1, dx1b, "ffn1",
        side_first=_side(owners(mid_b), _direct_phases(True)), side=_side(owners(mid_a), _direct_phases(True)),
        own_side=own_side)
    recv.update(zip(mid_b + mid_a, got_mid))
    recv.update(zip(G_FFN1, got_own))
    return loss, dx0, recv, {n: g[n] for n in SMALL}


def _whole_weight(name, gathered):
    _, r, c = gathered.shape
    return gathered.reshape(N_DEV * r, c) if SHARD_AXIS[name] == 0 else gathered.transpose(1, 0, 2).reshape(r, N_DEV * c)


def _for_owners(name, grad, shard_shape):
    r, c = shard_shape
    blk = grad.reshape(N_DEV, r, c) if SHARD_AXIS[name] == 0 else grad.reshape(r, N_DEV, c).transpose(1, 0, 2)
    return blk.astype(BF16)


def _pack_small(d, names, extra_rows):
    parts = []
    for n in names:
        v = d[n].reshape(-1)
        pad = (-v.size) % LANES
        parts.append(jnp.concatenate([v, jnp.zeros((pad,), v.dtype)]).reshape(-1, LANES))
    t = jnp.concatenate(parts, axis=0)
    return jnp.concatenate([t, jnp.zeros((extra_rows, LANES), t.dtype)], axis=0)


def _unpack_small(t, like, names):
    out, off = {}, 0
    for n in names:
        size = like[n].size
        rows = -(-size // LANES)
        out[n] = t[off:off + rows].reshape(-1)[:size].reshape(like[n].shape)
        off += rows
    return out


def _direct_phases(per_peer):
    def descriptors(src_ref, out_ref, send_sems, recv_sems, local_sem):
        x, y, c = lax.axis_index("x"), lax.axis_index("y"), lax.axis_index("c")
        me = 4 * x + 2 * y + c
        mine = pltpu.make_async_copy(src_ref.at[me] if per_peer else src_ref, out_ref.at[me], local_sem)
        copies = []
        for k in range(1, N_DEV):
            px = 1 - x if k & 4 else x
            py = 1 - y if k & 2 else y
            pc = 1 - c if k & 1 else c
            copies.append(pltpu.make_async_remote_copy(
                src_ref=src_ref.at[4 * px + 2 * py + pc] if per_peer else src_ref, dst_ref=out_ref.at[me],
                send_sem=send_sems.at[k - 1], recv_sem=recv_sems.at[k - 1],
                device_id=(px, py, pc), device_id_type=pl.DeviceIdType.MESH))
        return mine, copies

    def start(*refs):
        mine, copies = descriptors(*refs)
        mine.start()
        for cp in copies:
            cp.start()

    def forward(*refs):
        pass

    def finish(*refs):
        mine, copies = descriptors(*refs)
        for cp in copies:
            cp.wait_recv()
        for cp in copies:
            cp.wait_send()
        mine.wait()

    return start, forward, finish


def _exchange_parts(srcs, phases):
    n = len(srcs)
    shapes = [jax.ShapeDtypeStruct((N_DEV,) + tuple(s.shape[-2:]), s.dtype) for s in srcs]
    sems = [pltpu.SemaphoreType.DMA((n, N_DEV - 1)), pltpu.SemaphoreType.DMA((n, N_DEV - 1)), pltpu.SemaphoreType.DMA((n,))]

    def lift(phase):
        def run(src_refs, out_refs, send, recv, local):
            for a, (s_ref, o_ref) in enumerate(zip(src_refs, out_refs)):
                phase(s_ref, o_ref, send.at[a], recv.at[a], local.at[a])
        return run

    return shapes, sems, [lift(p) for p in phases]


def _exchange(srcs, phases, *, name):
    shapes, sems, runs = _exchange_parts(srcs, phases)
    n = len(srcs)

    def body(*refs):
        for run in runs:
            run(refs[:n], refs[n:2 * n], *refs[2 * n:])

    anyspace = pl.BlockSpec(memory_space=pl.ANY)
    return _pcall(body, name=name, in_specs=[anyspace] * n, out_specs=[anyspace] * n, out_shape=shapes, scratch_shapes=sems)(*srcs)


def _side(srcs, phases):
    shapes, sems, (start, forward, finish) = _exchange_parts(srcs, phases)

    def before(first, mid, ins, outs, scratch):
        pl.when(first)(lambda: start(ins, outs, *scratch))
        pl.when(mid)(lambda: forward(ins, outs, *scratch))

    def after(last, ins, outs, scratch):
        pl.when(last)(lambda: finish(ins, outs, *scratch))

    return list(srcs), shapes, sems, before, after


def _call_2d(kern, *, name, grid, in_specs, out_specs, out_shape, ins, scratch_shapes=(), semantics, side=None):
    if side is None:
        return _pcall(kern, name=name, grid=grid, in_specs=in_specs, out_specs=out_specs, out_shape=out_shape,
                      scratch_shapes=list(scratch_shapes), compiler_params=_params(*semantics))(*ins)
    s_ins, s_shapes, s_scratch, before, after = side
    n_in, n_out, n_scr = len(ins), len(out_shape), len(scratch_shapes)

    def combined(*refs):
        refs = list(refs)
        cut = [n_in, len(s_ins), n_out, len(s_shapes), n_scr, len(s_scratch)]
        parts, pos = [], 0
        for c in cut:
            parts.append(refs[pos:pos + c])
            pos += c
        m_in, c_in, m_out, c_out, m_scr, c_scr = parts
        ids = [pl.program_id(a) for a in range(len(grid))]
        inner_zero = functools.reduce(jnp.logical_and, [i == 0 for i in ids[1:]], True)
        first = jnp.logical_and(ids[0] == 0, inner_zero)
        mid = jnp.logical_and(ids[0] == grid[0] // 2, inner_zero)
        last = functools.reduce(jnp.logical_and, [i == n - 1 for i, n in zip(ids, grid)])
        before(first, mid, c_in, c_out, c_scr)
        kern(*m_in, *m_out, *m_scr)
        after(last, c_in, c_out, c_scr)

    anyspace = pl.BlockSpec(memory_space=pl.ANY)
    outs = _pcall(combined, name=name, grid=grid, in_specs=list(in_specs) + [anyspace] * len(s_ins),
                  out_specs=list(out_specs) + [anyspace] * len(s_shapes), out_shape=list(out_shape) + s_shapes,
                  scratch_shapes=list(scratch_shapes) + s_scratch, compiler_params=_params(*["arbitrary"] * len(grid)))(*ins, *s_ins)
    return outs[:n_out], outs[n_out:]


def _two_level_phases():
    def parts(src_ref, out_ref, send_sems, recv_sems, local_sem):
        x, y, c = lax.axis_index("x"), lax.axis_index("y"), lax.axis_index("c")
        me, sibling = (x, y, c), (x, y, 1 - c)
        chips = [(1 - x, y), (x, 1 - y), (1 - x, 1 - y)]

        def slab(px, py, pc):
            return out_ref.at[4 * px + 2 * py + pc]

        def copy(k, block, to, from_src=False):
            return pltpu.make_async_remote_copy(
                src_ref=src_ref if from_src else slab(*block), dst_ref=slab(*block),
                send_sem=send_sems.at[k], recv_sem=recv_sems.at[k], device_id=to, device_id_type=pl.DeviceIdType.MESH)

        return dict(
            mine=lambda: pltpu.make_async_copy(src_ref, slab(*me), local_sem),
            first=lambda: [copy(0, me, sibling, True)] + [copy(1 + j, me, (*chip, c), True) for j, chip in enumerate(chips)],
            passed=lambda: [copy(4 + j, (*chip, c), sibling) for j, chip in enumerate(chips)],
            landed=lambda: [copy(1 + j, (*chip, c), me) for j, chip in enumerate(chips)],
            late=lambda: [copy(0, sibling, me)] + [copy(4 + j, (*chip, 1 - c), me) for j, chip in enumerate(chips)])

    def start(*refs):
        make = parts(*refs)
        make['mine']().start()
        for cp in make['first']():
            cp.start()

    def forward(*refs):
        make = parts(*refs)
        for arrived, onward in zip(make['landed'](), make['passed']()):
            arrived.wait_recv()
            onward.start()

    def finish(*refs):
        make = parts(*refs)
        for cp in make['late']():
            cp.wait_recv()
        for cp in make['first']() + make['passed']():
            cp.wait_send()
        make['mine']().wait()

    return start, forward, finish


def _adamw(recv, w, m, v, *, name):
    rows, cols = w.shape
    tr = _pick(rows, (256, 128, 64))

    def kern(r_ref, w_ref, m_ref, v_ref, g_ref, d_ref, mo_ref, vo_ref):
        g = r_ref[0].astype(F32)
        for p in range(1, N_DEV):
            g = g + r_ref[p].astype(F32)
        mn = ADAM_B1 * m_ref[...] + (1.0 - ADAM_B1) * g
        vn = ADAM_B2 * v_ref[...] + (1.0 - ADAM_B2) * (g * g)
        m_hat = mn / (1.0 - ADAM_B1 ** ADAM_STEP)
        v_hat = vn / (1.0 - ADAM_B2 ** ADAM_STEP)
        g_ref[...] = g
        d_ref[...] = -ADAM_LR * (m_hat / (jnp.sqrt(v_hat) + ADAM_EPS) + ADAM_WD * w_ref[...])
        mo_ref[...] = mn
        vo_ref[...] = vn

    row = pl.BlockSpec((tr, cols), lambda i: (i, 0))
    shp = jax.ShapeDtypeStruct((rows, cols), F32)
    return _pcall(kern, name=name, grid=(rows // tr,), in_specs=[pl.BlockSpec((N_DEV, tr, cols), lambda i: (0, i, 0)), row, row, row],
                  out_specs=[row, row, row, row], out_shape=[shp, shp, shp, shp], compiler_params=_params("parallel"))(recv, w, m, v)


INPUTS = ['x', 'mem'] + WEIGHTS + ['loss_target'] + ['m_' + n for n in WEIGHTS] + ['v_' + n for n in WEIGHTS]
SMALL_PAD_ROWS = 4


def kernel(x, mem, ffn1_norm, ffn1_w1, ffn1_w3, ffn1_w2, mix_norm, mem_norm, w_in, w_mem_kv, qn_dsa, kn_dsa, qn_mem, kn_mem, w_branch_sb, w_branch_dsa, w_branch_mem, w_gate, b_gate, w_out, ffn2_norm, ffn2_w1, ffn2_w3, ffn2_w2, loss_target, m_ffn1_norm, m_ffn1_w1, m_ffn1_w3, m_ffn1_w2, m_mix_norm, m_mem_norm, m_w_in, m_w_mem_kv, m_qn_dsa, m_kn_dsa, m_qn_mem, m_kn_mem, m_w_branch_sb, m_w_branch_dsa, m_w_branch_mem, m_w_gate, m_b_gate, m_w_out, m_ffn2_norm, m_ffn2_w1, m_ffn2_w3, m_ffn2_w2, v_ffn1_norm, v_ffn1_w1, v_ffn1_w3, v_ffn1_w2, v_mix_norm, v_mem_norm, v_w_in, v_w_mem_kv, v_qn_dsa, v_kn_dsa, v_qn_mem, v_kn_mem, v_w_branch_sb, v_w_branch_dsa, v_w_branch_mem, v_w_gate, v_b_gate, v_w_out, v_ffn2_norm, v_ffn2_w1, v_ffn2_w3, v_ffn2_w2):
    given = dict(zip(INPUTS, (x, mem, ffn1_norm, ffn1_w1, ffn1_w3, ffn1_w2, mix_norm, mem_norm, w_in, w_mem_kv, qn_dsa, kn_dsa, qn_mem, kn_mem, w_branch_sb, w_branch_dsa, w_branch_mem, w_gate, b_gate, w_out, ffn2_norm, ffn2_w1, ffn2_w3, ffn2_w2, loss_target, m_ffn1_norm, m_ffn1_w1, m_ffn1_w3, m_ffn1_w2, m_mix_norm, m_mem_norm, m_w_in, m_w_mem_kv, m_qn_dsa, m_kn_dsa, m_qn_mem, m_kn_mem, m_w_branch_sb, m_w_branch_dsa, m_w_branch_mem, m_w_gate, m_b_gate, m_w_out, m_ffn2_norm, m_ffn2_w1, m_ffn2_w3, m_ffn2_w2, v_ffn1_norm, v_ffn1_w1, v_ffn1_w3, v_ffn1_w2, v_mix_norm, v_mem_norm, v_w_in, v_w_mem_kv, v_qn_dsa, v_kn_dsa, v_qn_mem, v_kn_mem, v_w_branch_sb, v_w_branch_dsa, v_w_branch_mem, v_w_gate, v_b_gate, v_w_out, v_ffn2_norm, v_ffn2_w1, v_ffn2_w3, v_ffn2_w2), strict=True))
    wl = {n: given[n][0] for n in BIG}
    ws = {n: given[n] for n in SMALL}

    loss, dx, recv, g = _local_step(x[0], mem[0], loss_target[0], wl, ws)

    big = [{}, {}, {}, {}]
    for n in G_FFN2 + G_MID + G_FFN1:
        outs = _adamw(recv[n], wl[n], given['m_' + n][0], given['v_' + n][0], name=f"adamw_{n}")
        for kind, t in enumerate(outs):
            big[kind][n] = t

    gs = _pack_small(g, SMALL, SMALL_PAD_ROWS)
    loss_row = gs.shape[0] - SMALL_PAD_ROWS
    gs = gs.at[loss_row, 0].set(loss)
    recv_s = _exchange([gs], _direct_phases(False), name="gather_small")[0]
    small = _adamw(recv_s, _pack_small(ws, SMALL, SMALL_PAD_ROWS), _pack_small({n: given['m_' + n] for n in SMALL}, SMALL, SMALL_PAD_ROWS),
                   _pack_small({n: given['v_' + n] for n in SMALL}, SMALL, SMALL_PAD_ROWS), name="adamw_replicated")
    total_loss = small[0][loss_row, 0]
    small = [_unpack_small(t, ws, SMALL) for t in small]

    outs = [total_loss, dx[None]]
    for kind in range(4):
        outs += [big[kind][n][None] if n in wl else small[kind][n] for n in WEIGHTS]
    return tuple(outs)
```
